```python
import jax, jax.numpy as jnp
from jax import lax
import numpy as np

D_MODEL = 1024
BATCH = 8
SEQ = 8192
DEPTH = 1

D_MIX = D_MODEL
D_LRU = D_MIX // 2
LRU_BLOCKS = 8
LRU_BLOCK = D_LRU // LRU_BLOCKS
CONV_WIDTH = 4
LRU_C = 8.0
N_HEADS = 8
N_KV_HEADS = 2
GQA_GROUP = N_HEADS // N_KV_HEADS
HEAD_DIM = (D_MIX - D_LRU) // N_HEADS
D_ATTN = N_HEADS * HEAD_DIM
WINDOW = 128
BLOCK_Q = 128
D_FF = 2816
RMS_EPS = 1e-6
D_IN = 2 * D_LRU + (N_HEADS + 2 * N_KV_HEADS) * HEAD_DIM
MASK_VALUE = -1e30

kernel_name = "hybrid_rglru_swa_sink_macaron"


def rms_norm(x, g):
    xf = x.astype(jnp.float32)
    y = xf * lax.rsqrt(jnp.mean(xf * xf, axis=-1, keepdims=True) + RMS_EPS)
    return (y * g.astype(jnp.float32)).astype(x.dtype)


def swiglu(x, w_gu, w_down):
    gate, up = jnp.split(x @ w_gu, 2, axis=-1)
    return (jax.nn.silu(gate) * up) @ w_down


def causal_depthwise_conv(x, w, b):
    c = x.shape[-1]
    rhs = w.astype(x.dtype)[:, None, :]
    y = lax.conv_general_dilated(x, rhs, window_strides=(1,), padding=[(CONV_WIDTH - 1, 0)],
                                 dimension_numbers=("NWC", "WIO", "NWC"), feature_group_count=c)
    return y + b.astype(x.dtype)


def block_diag(x, w, b):
    bsz, s, _ = x.shape
    xh = x.reshape(bsz, s, LRU_BLOCKS, LRU_BLOCK)
    y = jnp.einsum("bshi,hij->bshj", xh, w).reshape(bsz, s, D_LRU)
    return y + b


def rg_lru(x_in, conv_w, conv_b, w_rg, b_rg, w_ig, b_ig, lam):
    xc = causal_depthwise_conv(x_in, conv_w, conv_b)
    r = jax.nn.sigmoid(block_diag(xc, w_rg, b_rg).astype(jnp.float32))
    i = jax.nn.sigmoid(block_diag(xc, w_ig, b_ig).astype(jnp.float32))
    log_a = LRU_C * r * jax.nn.log_sigmoid(lam.astype(jnp.float32))
    a = jnp.exp(log_a)
    mult = jnp.sqrt(-jnp.expm1(2.0 * log_a))
    u = mult * (i * xc.astype(jnp.float32))

    def combine(left, right):
        a_l, u_l = left
        a_r, u_r = right
        return a_l * a_r, a_r * u_l + u_r

    _, h = lax.associative_scan(combine, (a, u), axis=1)
    return h.astype(x_in.dtype)


def sliding_window_sink_attention(q, k, v, sinks):
    bsz, s = q.shape[0], q.shape[1]
    nb = s // BLOCK_Q
    qb = q.reshape(bsz, nb, BLOCK_Q, N_KV_HEADS, GQA_GROUP, HEAD_DIM)
    kb = k.reshape(bsz, nb, BLOCK_Q, N_KV_HEADS, HEAD_DIM)
    vb = v.reshape(bsz, nb, BLOCK_Q, N_KV_HEADS, HEAD_DIM)
    k_band = jnp.concatenate([jnp.concatenate([jnp.zeros_like(kb[:, :1]), kb[:, :-1]], axis=1), kb], axis=2)
    v_band = jnp.concatenate([jnp.concatenate([jnp.zeros_like(vb[:, :1]), vb[:, :-1]], axis=1), vb], axis=2)
    scale = HEAD_DIM ** -0.5
    scores = jnp.einsum("bnqkgd,bnjkd->bnkgqj", qb, k_band).astype(jnp.float32) * scale
    qi = jnp.arange(BLOCK_Q)[:, None]
    kj = jnp.arange(2 * BLOCK_Q)[None, :]
    rel = qi + BLOCK_Q - kj
    band = (rel >= 0) & (rel < WINDOW)
    blk = jnp.arange(nb)[:, None, None]
    mask = band[None] & ((blk > 0) | (kj >= BLOCK_Q)[None])
    scores = jnp.where(mask[None, :, None, None], scores, MASK_VALUE)
    sink = sinks.astype(jnp.float32).reshape(1, 1, N_KV_HEADS, GQA_GROUP, 1, 1)
    m = jnp.maximum(jnp.max(scores, axis=-1, keepdims=True), sink)
    p = jnp.exp(scores - m)
    denom = jnp.sum(p, axis=-1, keepdims=True) + jnp.exp(sink - m)
    p = (p / denom).astype(v.dtype)
    out = jnp.einsum("bnkgqj,bnjkd->bnqkgd", p, v_band)
    return out.reshape(bsz, s, D_ATTN)


def hybrid_mixer(h, w_in, conv_w, conv_b, w_rg, b_rg, w_ig, b_ig, lam, sinks, g_lru_out, g_attn_out, w_o):
    bsz, s, _ = h.shape
    proj = h @ w_in
    o1 = D_LRU
    o2 = o1 + D_LRU
    o3 = o2 + D_ATTN
    o4 = o3 + N_KV_HEADS * HEAD_DIM
    x_lru = proj[..., :o1]
    gate_lru = proj[..., o1:o2]
    q = proj[..., o2:o3].reshape(bsz, s, N_HEADS, HEAD_DIM)
    k = proj[..., o3:o4].reshape(bsz, s, N_KV_HEADS, HEAD_DIM)
    v = proj[..., o4:].reshape(bsz, s, N_KV_HEADS, HEAD_DIM)
    y_lru = rg_lru(x_lru, conv_w, conv_b, w_rg, b_rg, w_ig, b_ig, lam) * jax.nn.gelu(gate_lru)
    y_attn = sliding_window_sink_attention(q, k, v, sinks)
    y = jnp.concatenate([rms_norm(y_lru, g_lru_out), rms_norm(y_attn, g_attn_out)], axis=-1)
    return y @ w_o


def _fwd_setup_inputs(seed: int = 0) -> dict:
    key = jax.random.key(seed)
    ks = jax.random.split(key, 32)
    f32 = jnp.float32

    def nrm(k, shape, scale):
        return jax.random.normal(k, shape, f32) * scale

    def gain(k, n):
        return 1.0 + 0.05 * jax.random.normal(k, (DEPTH, n), f32)

    u = jax.random.uniform(ks[13], (DEPTH, D_LRU), f32, 0.9, 0.999)
    return {
        "x": jax.random.normal(ks[0], (BATCH, SEQ, D_MODEL), f32),
        "ffn1_pre_g": gain(ks[1], D_MODEL),
        "ffn1_w_gu": nrm(ks[2], (DEPTH, D_MODEL, 2 * D_FF), D_MODEL ** -0.5),
        "ffn1_w_down": nrm(ks[3], (DEPTH, D_FF, D_MODEL), D_FF ** -0.5),
        "ffn1_post_g": gain(ks[4], D_MODEL),
        "mix_pre_g": gain(ks[5], D_MODEL),
        "w_in": nrm(ks[6], (DEPTH, D_MODEL, D_IN), D_MODEL ** -0.5),
        "conv_w": nrm(ks[7], (DEPTH, CONV_WIDTH, D_LRU), CONV_WIDTH ** -0.5),
        "conv_b": nrm(ks[8], (DEPTH, D_LRU), 0.01),
        "w_rg": nrm(ks[9], (DEPTH, LRU_BLOCKS, LRU_BLOCK, LRU_BLOCK), LRU_BLOCK ** -0.5),
        "b_rg": nrm(ks[10], (DEPTH, D_LRU), 0.01),
        "w_ig": nrm(ks[11], (DEPTH, LRU_BLOCKS, LRU_BLOCK, LRU_BLOCK), LRU_BLOCK ** -0.5),
        "b_ig": nrm(ks[12], (DEPTH, D_LRU), 0.01),
        "lru_lambda": jnp.log(u) - jnp.log1p(-u),
        "sinks": nrm(ks[14], (DEPTH, N_HEADS), 1.0),
        "g_lru_out": gain(ks[15], D_LRU),
        "g_attn_out": gain(ks[16], D_ATTN),
        "w_o": nrm(ks[17], (DEPTH, D_MIX, D_MODEL), D_MIX ** -0.5),
        "mix_post_g": gain(ks[18], D_MODEL),
        "ffn2_pre_g": gain(ks[19], D_MODEL),
        "ffn2_w_gu": nrm(ks[20], (DEPTH, D_MODEL, 2 * D_FF), D_MODEL ** -0.5),
        "ffn2_w_down": nrm(ks[21], (DEPTH, D_FF, D_MODEL), D_FF ** -0.5),
        "ffn2_post_g": gain(ks[22], D_MODEL),
    }


def _fwd_reference(x, ffn1_pre_g, ffn1_w_gu, ffn1_w_down, ffn1_post_g, mix_pre_g, w_in, conv_w, conv_b,
              w_rg, b_rg, w_ig, b_ig, lru_lambda, sinks, g_lru_out, g_attn_out, w_o, mix_post_g,
              ffn2_pre_g, ffn2_w_gu, ffn2_w_down, ffn2_post_g):
    for l in range(DEPTH):
        x = x + 0.5 * rms_norm(swiglu(rms_norm(x, ffn1_pre_g[l]), ffn1_w_gu[l], ffn1_w_down[l]), ffn1_post_g[l])
        y = hybrid_mixer(rms_norm(x, mix_pre_g[l]), w_in[l], conv_w[l], conv_b[l], w_rg[l], b_rg[l],
                         w_ig[l], b_ig[l], lru_lambda[l], sinks[l], g_lru_out[l], g_attn_out[l], w_o[l])
        x = x + rms_norm(y, mix_post_g[l])
        x = x + 0.5 * rms_norm(swiglu(rms_norm(x, ffn2_pre_g[l]), ffn2_w_gu[l], ffn2_w_down[l]), ffn2_post_g[l])
    return x


import jax as _jax
import jax.numpy as _jnp

TWIN_FORMAT = 'train_step'
FWD_PARAMS = ['x', 'ffn1_pre_g', 'ffn1_w_gu', 'ffn1_w_down', 'ffn1_post_g', 'mix_pre_g', 'w_in', 'conv_w', 'conv_b', 'w_rg', 'b_rg', 'w_ig', 'b_ig', 'lru_lambda', 'sinks', 'g_lru_out', 'g_attn_out', 'w_o', 'mix_post_g', 'ffn2_pre_g', 'ffn2_w_gu', 'ffn2_w_down', 'ffn2_post_g']
TWIN_WEIGHTS = ['ffn1_pre_g', 'ffn1_w_gu', 'ffn1_w_down', 'ffn1_post_g', 'mix_pre_g', 'w_in', 'conv_w', 'conv_b', 'w_rg', 'b_rg', 'w_ig', 'b_ig', 'lru_lambda', 'sinks', 'g_lru_out', 'g_attn_out', 'w_o', 'mix_post_g', 'ffn2_pre_g', 'ffn2_w_gu', 'ffn2_w_down', 'ffn2_post_g']
TWIN_DIFF_INPUT = 'x'
TWIN_INPUTS = ['x', 'ffn1_pre_g', 'ffn1_w_gu', 'ffn1_w_down', 'ffn1_post_g', 'mix_pre_g', 'w_in', 'conv_w', 'conv_b', 'w_rg', 'b_rg', 'w_ig', 'b_ig', 'lru_lambda', 'sinks', 'g_lru_out', 'g_attn_out', 'w_o', 'mix_post_g', 'ffn2_pre_g', 'ffn2_w_gu', 'ffn2_w_down', 'ffn2_post_g', 'loss_target', 'm_ffn1_pre_g', 'm_ffn1_w_gu', 'm_ffn1_w_down', 'm_ffn1_post_g', 'm_mix_pre_g', 'm_w_in', 'm_conv_w', 'm_conv_b', 'm_w_rg', 'm_b_rg', 'm_w_ig', 'm_b_ig', 'm_lru_lambda', 'm_sinks', 'm_g_lru_out', 'm_g_attn_out', 'm_w_o', 'm_mix_post_g', 'm_ffn2_pre_g', 'm_ffn2_w_gu', 'm_ffn2_w_down', 'm_ffn2_post_g', 'v_ffn1_pre_g', 'v_ffn1_w_gu', 'v_ffn1_w_down', 'v_ffn1_post_g', 'v_mix_pre_g', 'v_w_in', 'v_conv_w', 'v_conv_b', 'v_w_rg', 'v_b_rg', 'v_w_ig', 'v_b_ig', 'v_lru_lambda', 'v_sinks', 'v_g_lru_out', 'v_g_attn_out', 'v_w_o', 'v_mix_post_g', 'v_ffn2_pre_g', 'v_ffn2_w_gu', 'v_ffn2_w_down', 'v_ffn2_post_g']
TWIN_OUTPUTS = ['loss', 'grad_x', 'grad_ffn1_pre_g', 'grad_ffn1_w_gu', 'grad_ffn1_w_down', 'grad_ffn1_post_g', 'grad_mix_pre_g', 'grad_w_in', 'grad_conv_w', 'grad_conv_b', 'grad_w_rg', 'grad_b_rg', 'grad_w_ig', 'grad_b_ig', 'grad_lru_lambda', 'grad_sinks', 'grad_g_lru_out', 'grad_g_attn_out', 'grad_w_o', 'grad_mix_post_g', 'grad_ffn2_pre_g', 'grad_ffn2_w_gu', 'grad_ffn2_w_down', 'grad_ffn2_post_g', 'delta_ffn1_pre_g', 'delta_ffn1_w_gu', 'delta_ffn1_w_down', 'delta_ffn1_post_g', 'delta_mix_pre_g', 'delta_w_in', 'delta_conv_w', 'delta_conv_b', 'delta_w_rg', 'delta_b_rg', 'delta_w_ig', 'delta_b_ig', 'delta_lru_lambda', 'delta_sinks', 'delta_g_lru_out', 'delta_g_attn_out', 'delta_w_o', 'delta_mix_post_g', 'delta_ffn2_pre_g', 'delta_ffn2_w_gu', 'delta_ffn2_w_down', 'delta_ffn2_post_g', 'new_m_ffn1_pre_g', 'new_m_ffn1_w_gu', 'new_m_ffn1_w_down', 'new_m_ffn1_post_g', 'new_m_mix_pre_g', 'new_m_w_in', 'new_m_conv_w', 'new_m_conv_b', 'new_m_w_rg', 'new_m_b_rg', 'new_m_w_ig', 'new_m_b_ig', 'new_m_lru_lambda', 'new_m_sinks', 'new_m_g_lru_out', 'new_m_g_attn_out', 'new_m_w_o', 'new_m_mix_post_g', 'new_m_ffn2_pre_g', 'new_m_ffn2_w_gu', 'new_m_ffn2_w_down', 'new_m_ffn2_post_g', 'new_v_ffn1_pre_g', 'new_v_ffn1_w_gu', 'new_v_ffn1_w_down', 'new_v_ffn1_post_g', 'new_v_mix_pre_g', 'new_v_w_in', 'new_v_conv_w', 'new_v_conv_b', 'new_v_w_rg', 'new_v_b_rg', 'new_v_w_ig', 'new_v_b_ig', 'new_v_lru_lambda', 'new_v_sinks', 'new_v_g_lru_out', 'new_v_g_attn_out', 'new_v_w_o', 'new_v_mix_post_g', 'new_v_ffn2_pre_g', 'new_v_ffn2_w_gu', 'new_v_ffn2_w_down', 'new_v_ffn2_post_g']
TWIN_LEAF_KINDS = {'loss': 'loss', 'grad_x': 'grad_x', 'grad_ffn1_pre_g': 'grad_w', 'grad_ffn1_w_gu': 'grad_w', 'grad_ffn1_w_down': 'grad_w', 'grad_ffn1_post_g': 'grad_w', 'grad_mix_pre_g': 'grad_w', 'grad_w_in': 'grad_w', 'grad_conv_w': 'grad_w', 'grad_conv_b': 'grad_w', 'grad_w_rg': 'grad_w', 'grad_b_rg': 'grad_w', 'grad_w_ig': 'grad_w', 'grad_b_ig': 'grad_w', 'grad_lru_lambda': 'grad_w', 'grad_sinks': 'grad_w', 'grad_g_lru_out': 'grad_w', 'grad_g_attn_out': 'grad_w', 'grad_w_o': 'grad_w', 'grad_mix_post_g': 'grad_w', 'grad_ffn2_pre_g': 'grad_w', 'grad_ffn2_w_gu': 'grad_w', 'grad_ffn2_w_down': 'grad_w', 'grad_ffn2_post_g': 'grad_w', 'delta_ffn1_pre_g': 'delta_w', 'delta_ffn1_w_gu': 'delta_w', 'delta_ffn1_w_down': 'delta_w', 'delta_ffn1_post_g': 'delta_w', 'delta_mix_pre_g': 'delta_w', 'delta_w_in': 'delta_w', 'delta_conv_w': 'delta_w', 'delta_conv_b': 'delta_w', 'delta_w_rg': 'delta_w', 'delta_b_rg': 'delta_w', 'delta_w_ig': 'delta_w', 'delta_b_ig': 'delta_w', 'delta_lru_lambda': 'delta_w', 'delta_sinks': 'delta_w', 'delta_g_lru_out': 'delta_w', 'delta_g_attn_out': 'delta_w', 'delta_w_o': 'delta_w', 'delta_mix_post_g': 'delta_w', 'delta_ffn2_pre_g': 'delta_w', 'delta_ffn2_w_gu': 'delta_w', 'delta_ffn2_w_down': 'delta_w', 'delta_ffn2_post_g': 'delta_w', 'new_m_ffn1_pre_g': 'new_m', 'new_m_ffn1_w_gu': 'new_m', 'new_m_ffn1_w_down': 'new_m', 'new_m_ffn1_post_g': 'new_m', 'new_m_mix_pre_g': 'new_m', 'new_m_w_in': 'new_m', 'new_m_conv_w': 'new_m', 'new_m_conv_b': 'new_m', 'new_m_w_rg': 'new_m', 'new_m_b_rg': 'new_m', 'new_m_w_ig': 'new_m', 'new_m_b_ig': 'new_m', 'new_m_lru_lambda': 'new_m', 'new_m_sinks': 'new_m', 'new_m_g_lru_out': 'new_m', 'new_m_g_attn_out': 'new_m', 'new_m_w_o': 'new_m', 'new_m_mix_post_g': 'new_m', 'new_m_ffn2_pre_g': 'new_m', 'new_m_ffn2_w_gu': 'new_m', 'new_m_ffn2_w_down': 'new_m', 'new_m_ffn2_post_g': 'new_m', 'new_v_ffn1_pre_g': 'new_v', 'new_v_ffn1_w_gu': 'new_v', 'new_v_ffn1_w_down': 'new_v', 'new_v_ffn1_post_g': 'new_v', 'new_v_mix_pre_g': 'new_v', 'new_v_w_in': 'new_v', 'new_v_conv_w': 'new_v', 'new_v_conv_b': 'new_v', 'new_v_w_rg': 'new_v', 'new_v_b_rg': 'new_v', 'new_v_w_ig': 'new_v', 'new_v_b_ig': 'new_v', 'new_v_lru_lambda': 'new_v', 'new_v_sinks': 'new_v', 'new_v_g_lru_out': 'new_v', 'new_v_g_attn_out': 'new_v', 'new_v_w_o': 'new_v', 'new_v_mix_post_g': 'new_v', 'new_v_ffn2_pre_g': 'new_v', 'new_v_ffn2_w_gu': 'new_v', 'new_v_ffn2_w_down': 'new_v', 'new_v_ffn2_post_g': 'new_v'}


def _forward(args):
    return _fwd_reference(*[args[k] for k in FWD_PARAMS])


def _output_shape():
    def fwd():
        inp = _fwd_setup_inputs(0)
        return _fwd_reference(*[inp[k] for k in FWD_PARAMS])
    out = _jax.eval_shape(fwd)
    return out.shape, out.dtype

N_MICROBATCH = 1
ADAM_LR = 0.001
ADAM_B1 = 0.9
ADAM_B2 = 0.999
ADAM_EPS = 1e-08
ADAM_WD = 0.01
ADAM_STEP = 10
PER_EXAMPLE_BATCH_AXIS = {'x': 0, 'loss_target': 0}
SHARED_INPUTS = []
_WEIGHT_DTYPES = {'ffn1_pre_g': _jnp.float32, 'ffn1_w_gu': _jnp.float32, 'ffn1_w_down': _jnp.float32, 'ffn1_post_g': _jnp.float32, 'mix_pre_g': _jnp.float32, 'w_in': _jnp.float32, 'conv_w': _jnp.float32, 'conv_b': _jnp.float32, 'w_rg': _jnp.float32, 'b_rg': _jnp.float32, 'w_ig': _jnp.float32, 'b_ig': _jnp.float32, 'lru_lambda': _jnp.float32, 'sinks': _jnp.float32, 'g_lru_out': _jnp.float32, 'g_attn_out': _jnp.float32, 'w_o': _jnp.float32, 'mix_post_g': _jnp.float32, 'ffn2_pre_g': _jnp.float32, 'ffn2_w_gu': _jnp.float32, 'ffn2_w_down': _jnp.float32, 'ffn2_post_g': _jnp.float32}
MOMENT_SCALE = {'ffn1_pre_g': 6.528591e-01, 'ffn1_w_gu': 2.746183e-01, 'ffn1_w_down': 4.790523e-01, 'ffn1_post_g': 1.587683e+01, 'mix_pre_g': 8.268610e-01, 'w_in': 6.545923e-01, 'conv_w': 6.330803e-01, 'conv_b': 7.970273e+00, 'w_rg': 2.164948e-01, 'b_rg': 1.687893e-01, 'w_ig': 4.050762e-01, 'b_ig': 2.085695e-01, 'lru_lambda': 3.007162e-01, 'sinks': 2.235035e-01, 'g_lru_out': 9.141576e-01, 'g_attn_out': 6.944211e-01, 'w_o': 7.620746e-01, 'mix_post_g': 6.426283e+01, 'ffn2_pre_g': 6.043661e-01, 'ffn2_w_gu': 2.462859e-01, 'ffn2_w_down': 5.224409e-01, 'ffn2_post_g': 1.608036e+01}


def _to_microbatches(a, axis):
    t = _jnp.moveaxis(a, axis, 0)
    t = t.reshape((N_MICROBATCH, t.shape[0] // N_MICROBATCH) + t.shape[1:])
    return _jnp.moveaxis(t, 1, axis + 1)


def setup_inputs(seed: int = 0) -> dict:
    inp = _fwd_setup_inputs(seed)
    key = _jax.random.fold_in(_jax.random.key(seed), 7919)
    shape, _ = _output_shape()
    out = dict(inp)
    out["loss_target"] = _jax.random.normal(_jax.random.fold_in(key, 0), shape, _jnp.float32)
    for i, name in enumerate(TWIN_WEIGHTS):
        w = inp[name].astype(_jnp.float32)
        if MOMENT_SCALE is None:
            s = _jnp.sqrt(_jnp.mean(_jnp.square(w)) + 1e-30)
        else:
            s = MOMENT_SCALE[name]
        km, kv = _jax.random.split(_jax.random.fold_in(key, i + 1))
        out[name] = w
        out["m_" + name] = s * _jax.random.normal(km, w.shape, _jnp.float32)
        out["v_" + name] = (s * s) * _jax.random.uniform(kv, w.shape, _jnp.float32, 0.5, 1.5)
    if N_MICROBATCH > 1:
        for name, axis in PER_EXAMPLE_BATCH_AXIS.items():
            out[name] = _to_microbatches(out[name], axis)
    return {'x': out['x'], 'ffn1_pre_g': out['ffn1_pre_g'], 'ffn1_w_gu': out['ffn1_w_gu'], 'ffn1_w_down': out['ffn1_w_down'], 'ffn1_post_g': out['ffn1_post_g'], 'mix_pre_g': out['mix_pre_g'], 'w_in': out['w_in'], 'conv_w': out['conv_w'], 'conv_b': out['conv_b'], 'w_rg': out['w_rg'], 'b_rg': out['b_rg'], 'w_ig': out['w_ig'], 'b_ig': out['b_ig'], 'lru_lambda': out['lru_lambda'], 'sinks': out['sinks'], 'g_lru_out': out['g_lru_out'], 'g_attn_out': out['g_attn_out'], 'w_o': out['w_o'], 'mix_post_g': out['mix_post_g'], 'ffn2_pre_g': out['ffn2_pre_g'], 'ffn2_w_gu': out['ffn2_w_gu'], 'ffn2_w_down': out['ffn2_w_down'], 'ffn2_post_g': out['ffn2_post_g'], 'loss_target': out['loss_target'], 'm_ffn1_pre_g': out['m_ffn1_pre_g'], 'm_ffn1_w_gu': out['m_ffn1_w_gu'], 'm_ffn1_w_down': out['m_ffn1_w_down'], 'm_ffn1_post_g': out['m_ffn1_post_g'], 'm_mix_pre_g': out['m_mix_pre_g'], 'm_w_in': out['m_w_in'], 'm_conv_w': out['m_conv_w'], 'm_conv_b': out['m_conv_b'], 'm_w_rg': out['m_w_rg'], 'm_b_rg': out['m_b_rg'], 'm_w_ig': out['m_w_ig'], 'm_b_ig': out['m_b_ig'], 'm_lru_lambda': out['m_lru_lambda'], 'm_sinks': out['m_sinks'], 'm_g_lru_out': out['m_g_lru_out'], 'm_g_attn_out': out['m_g_attn_out'], 'm_w_o': out['m_w_o'], 'm_mix_post_g': out['m_mix_post_g'], 'm_ffn2_pre_g': out['m_ffn2_pre_g'], 'm_ffn2_w_gu': out['m_ffn2_w_gu'], 'm_ffn2_w_down': out['m_ffn2_w_down'], 'm_ffn2_post_g': out['m_ffn2_post_g'], 'v_ffn1_pre_g': out['v_ffn1_pre_g'], 'v_ffn1_w_gu': out['v_ffn1_w_gu'], 'v_ffn1_w_down': out['v_ffn1_w_down'], 'v_ffn1_post_g': out['v_ffn1_post_g'], 'v_mix_pre_g': out['v_mix_pre_g'], 'v_w_in': out['v_w_in'], 'v_conv_w': out['v_conv_w'], 'v_conv_b': out['v_conv_b'], 'v_w_rg': out['v_w_rg'], 'v_b_rg': out['v_b_rg'], 'v_w_ig': out['v_w_ig'], 'v_b_ig': out['v_b_ig'], 'v_lru_lambda': out['v_lru_lambda'], 'v_sinks': out['v_sinks'], 'v_g_lru_out': out['v_g_lru_out'], 'v_g_attn_out': out['v_g_attn_out'], 'v_w_o': out['v_w_o'], 'v_mix_post_g': out['v_mix_post_g'], 'v_ffn2_pre_g': out['v_ffn2_pre_g'], 'v_ffn2_w_gu': out['v_ffn2_w_gu'], 'v_ffn2_w_down': out['v_ffn2_w_down'], 'v_ffn2_post_g': out['v_ffn2_post_g']}


def _loss(weights, diff, rest, loss_target):
    with _jax.named_scope("forward"):
        args = {**rest, TWIN_DIFF_INPUT: diff, **{k: w.astype(_WEIGHT_DTYPES[k]) for k, w in weights.items()}}
        y = _forward(args)
    with _jax.named_scope("loss_head"):
        err = _jnp.square(y.astype(_jnp.float32) - loss_target)
        return 0.5 * _jnp.sum(_jnp.mean(err, axis=-1)) if err.ndim else 0.5 * err


def _adamw(w, g, m, v):
    m = ADAM_B1 * m + (1.0 - ADAM_B1) * g
    v = ADAM_B2 * v + (1.0 - ADAM_B2) * _jnp.square(g)
    m_hat = m / (1.0 - ADAM_B1 ** ADAM_STEP)
    v_hat = v / (1.0 - ADAM_B2 ** ADAM_STEP)
    delta = -ADAM_LR * (m_hat / (_jnp.sqrt(v_hat) + ADAM_EPS) + ADAM_WD * w)
    return delta, m, v


def reference(x, ffn1_pre_g, ffn1_w_gu, ffn1_w_down, ffn1_post_g, mix_pre_g, w_in, conv_w, conv_b, w_rg, b_rg, w_ig, b_ig, lru_lambda, sinks, g_lru_out, g_attn_out, w_o, mix_post_g, ffn2_pre_g, ffn2_w_gu, ffn2_w_down, ffn2_post_g, loss_target, m_ffn1_pre_g, m_ffn1_w_gu, m_ffn1_w_down, m_ffn1_post_g, m_mix_pre_g, m_w_in, m_conv_w, m_conv_b, m_w_rg, m_b_rg, m_w_ig, m_b_ig, m_lru_lambda, m_sinks, m_g_lru_out, m_g_attn_out, m_w_o, m_mix_post_g, m_ffn2_pre_g, m_ffn2_w_gu, m_ffn2_w_down, m_ffn2_post_g, v_ffn1_pre_g, v_ffn1_w_gu, v_ffn1_w_down, v_ffn1_post_g, v_mix_pre_g, v_w_in, v_conv_w, v_conv_b, v_w_rg, v_b_rg, v_w_ig, v_b_ig, v_lru_lambda, v_sinks, v_g_lru_out, v_g_attn_out, v_w_o, v_mix_post_g, v_ffn2_pre_g, v_ffn2_w_gu, v_ffn2_w_down, v_ffn2_post_g):
    given = dict(x=x, ffn1_pre_g=ffn1_pre_g, ffn1_w_gu=ffn1_w_gu, ffn1_w_down=ffn1_w_down, ffn1_post_g=ffn1_post_g, mix_pre_g=mix_pre_g, w_in=w_in, conv_w=conv_w, conv_b=conv_b, w_rg=w_rg, b_rg=b_rg, w_ig=w_ig, b_ig=b_ig, lru_lambda=lru_lambda, sinks=sinks, g_lru_out=g_lru_out, g_attn_out=g_attn_out, w_o=w_o, mix_post_g=mix_post_g, ffn2_pre_g=ffn2_pre_g, ffn2_w_gu=ffn2_w_gu, ffn2_w_down=ffn2_w_down, ffn2_post_g=ffn2_post_g, loss_target=loss_target, m_ffn1_pre_g=m_ffn1_pre_g, m_ffn1_w_gu=m_ffn1_w_gu, m_ffn1_w_down=m_ffn1_w_down, m_ffn1_post_g=m_ffn1_post_g, m_mix_pre_g=m_mix_pre_g, m_w_in=m_w_in, m_conv_w=m_conv_w, m_conv_b=m_conv_b, m_w_rg=m_w_rg, m_b_rg=m_b_rg, m_w_ig=m_w_ig, m_b_ig=m_b_ig, m_lru_lambda=m_lru_lambda, m_sinks=m_sinks, m_g_lru_out=m_g_lru_out, m_g_attn_out=m_g_attn_out, m_w_o=m_w_o, m_mix_post_g=m_mix_post_g, m_ffn2_pre_g=m_ffn2_pre_g, m_ffn2_w_gu=m_ffn2_w_gu, m_ffn2_w_down=m_ffn2_w_down, m_ffn2_post_g=m_ffn2_post_g, v_ffn1_pre_g=v_ffn1_pre_g, v_ffn1_w_gu=v_ffn1_w_gu, v_ffn1_w_down=v_ffn1_w_down, v_ffn1_post_g=v_ffn1_post_g, v_mix_pre_g=v_mix_pre_g, v_w_in=v_w_in, v_conv_w=v_conv_w, v_conv_b=v_conv_b, v_w_rg=v_w_rg, v_b_rg=v_b_rg, v_w_ig=v_w_ig, v_b_ig=v_b_ig, v_lru_lambda=v_lru_lambda, v_sinks=v_sinks, v_g_lru_out=v_g_lru_out, v_g_attn_out=v_g_attn_out, v_w_o=v_w_o, v_mix_post_g=v_mix_post_g, v_ffn2_pre_g=v_ffn2_pre_g, v_ffn2_w_gu=v_ffn2_w_gu, v_ffn2_w_down=v_ffn2_w_down, v_ffn2_post_g=v_ffn2_post_g)
    weights = {n: given[n] for n in TWIN_WEIGHTS}
    shared = {n: given[n] for n in SHARED_INPUTS}
    per_example = {n: given[n] for n in ['x']}
    grad_fn = _jax.value_and_grad(_loss, argnums=(0, 1))

    def one_microbatch(ex, loss_target):
        ex = dict(ex)
        diff = ex.pop(TWIN_DIFF_INPUT)
        return grad_fn(weights, diff, {**shared, **ex}, loss_target)

    if N_MICROBATCH == 1:
        loss, (grad_w, grad_x) = one_microbatch(per_example, given["loss_target"])
    else:
        def body(carry, xs):
            loss_sum, grad_sum = carry
            l_k, (gw_k, gx_k) = one_microbatch(xs[0], xs[1])
            with _jax.named_scope("update"):
                return (loss_sum + l_k, _jax.tree.map(_jnp.add, grad_sum, gw_k)), gx_k

        init = (_jnp.zeros((), _jnp.float32), _jax.tree.map(_jnp.zeros_like, weights))
        (loss, grad_w), grad_x = _jax.lax.scan(body, init, (per_example, given["loss_target"]))
    with _jax.named_scope("update"):
        delta_w, new_m, new_v = {}, {}, {}
        for n in TWIN_WEIGHTS:
            delta_w[n], new_m[n], new_v[n] = _adamw(weights[n], grad_w[n], given["m_" + n], given["v_" + n])
    return (loss, grad_x, *[grad_w[n] for n in TWIN_WEIGHTS], *[delta_w[n] for n in TWIN_WEIGHTS],
            *[new_m[n] for n in TWIN_WEIGHTS], *[new_v[n] for n in TWIN_WEIGHTS])
```

```python
import functools

import jax
import jax.numpy as jnp
from jax import lax
from jax.experimental import pallas as pl
from jax.experimental.pallas import tpu as pltpu

F32 = jnp.float32
BF16 = jnp.bfloat16

D_MODEL = 1024
D_FF = 2816
N_DEV = 8
N_CHUNK = 4
CHUNK = D_FF // N_CHUNK
D_LRU = 512
D_ATTN = 512
LRU_GROUP = 128
N_LRU_GROUP = D_LRU // LRU_GROUP
HEAD_DIM = 64
BLOCK_Q = 128
D_IN = 1792
D_IN_DUP = 2048
RMS_EPS = 1e-6
LRU_C = 8.0
MASK_VALUE = -1e30
ATTN_SCALE = HEAD_DIM ** -0.5

ADAM_LR = 0.001
ADAM_B1 = 0.9
ADAM_B2 = 0.999
ADAM_EPS = 1e-08
ADAM_WD = 0.01
ADAM_STEP = 10

VMEM_LIMIT_V7X = 56 * 2 ** 20

ANY = pl.BlockSpec(memory_space=pl.ANY)
SMEM = pl.BlockSpec(memory_space=pltpu.SMEM)
MESH = pl.DeviceIdType.MESH


def _params(n_grid=0):
    sem = ("arbitrary",) * n_grid if n_grid else None
    return pltpu.CompilerParams(dimension_semantics=sem, vmem_limit_bytes=VMEM_LIMIT_V7X)


def _dot(a, b):
    return lax.dot_general(a, b, (((1,), (0,)), ((), ())), preferred_element_type=F32)


def _dot_nt(a, b):
    return lax.dot_general(a, b, (((1,), (1,)), ((), ())), preferred_element_type=F32)


def _dot_tn(a, b):
    return lax.dot_general(a, b, (((0,), (0,)), ((), ())), preferred_element_type=F32)


def _sigmoid(x):
    return 1.0 / (1.0 + jnp.exp(-x))


def _rms_fwd(x, g):
    r = lax.rsqrt(jnp.mean(x * x, axis=-1, keepdims=True) + RMS_EPS)
    return x * r * g


def _rms_bwd(x, g, dy):
    r = lax.rsqrt(jnp.mean(x * x, axis=-1, keepdims=True) + RMS_EPS)
    xh = x * r
    dg = jnp.sum(dy * xh, axis=0, keepdims=True)
    dxh = dy * g
    dx = r * (dxh - xh * jnp.mean(dxh * xh, axis=-1, keepdims=True))
    return dx, dg


def _gelu(x):
    c = 0.7978845608028654
    inner = c * (x + 0.044715 * x * x * x)
    th = jnp.tanh(inner)
    ge = 0.5 * x * (1.0 + th)
    dge = 0.5 * (1.0 + th) + 0.5 * x * (1.0 - th * th) * c * (1.0 + 3.0 * 0.044715 * x * x)
    return ge, dge


def _accumulate(ref, val, first):
    @pl.when(first)
    def _():
        ref[...] = val

    @pl.when(jnp.logical_not(first))
    def _():
        ref[...] += val


def _token_tile(t):
    return 512 if t >= 2048 else t // 2


def _ffn_fwd(x, g_pre, wgu, wd, g_post, target, name):
    t = x.shape[0]
    tm = _token_tile(t)
    n_i = t // tm
    with_loss = target is not None

    def body(*refs):
        if with_loss:
            x_ref, gpre_ref, wgu_ref, wd_ref, gpost_ref, tgt_ref, xo_ref, f_ref, gu_ref, loss_ref, n_bf, acc = refs
        else:
            x_ref, gpre_ref, wgu_ref, wd_ref, gpost_ref, xo_ref, f_ref, gu_ref, n_bf, acc = refs
        j = pl.program_id(1)

        @pl.when(j == 0)
        def _():
            n_bf[...] = _rms_fwd(x_ref[...], gpre_ref[...]).astype(BF16)
            acc[...] = jnp.zeros_like(acc)

        n = n_bf[...]
        gate = _dot(n, wgu_ref[0, 0])
        up = _dot(n, wgu_ref[1, 0])
        gu_ref[0, 0] = gate.astype(BF16)
        gu_ref[1, 0] = up.astype(BF16)
        a = (gate * _sigmoid(gate) * up).astype(BF16)
        acc[...] += _dot(a, wd_ref[0])

        @pl.when(j == N_CHUNK - 1)
        def _():
            f = acc[...]
            f_ref[...] = f
            xo = x_ref[...] + 0.5 * _rms_fwd(f, gpost_ref[...])
            if with_loss:
                err = xo - tgt_ref[...]
                xo_ref[...] = err * (1.0 / D_MODEL)
                part = 0.5 * jnp.sum(jnp.sum(err * err, axis=-1, keepdims=True) * (1.0 / D_MODEL), axis=0, keepdims=True)
                loss_ref[...] = jnp.broadcast_to(part, loss_ref.shape)
            else:
                xo_ref[...] = xo

    tok = pl.BlockSpec((tm, D_MODEL), lambda i, j: (i, 0))
    vec = pl.BlockSpec((1, D_MODEL), lambda i, j: (0, 0))
    in_specs = [tok, vec,
                pl.BlockSpec((2, 1, D_MODEL, CHUNK), lambda i, j: (0, j, 0, 0)),
                pl.BlockSpec((1, CHUNK, D_MODEL), lambda i, j: (j, 0, 0)),
                vec]
    out_shape = [jax.ShapeDtypeStruct((t, D_MODEL), F32), jax.ShapeDtypeStruct((t, D_MODEL), F32),
                 jax.ShapeDtypeStruct((2, N_CHUNK, t, CHUNK), BF16)]
    out_specs = [tok, tok, pl.BlockSpec((2, 1, tm, CHUNK), lambda i, j: (0, j, i, 0))]
    args = [x, g_pre, wgu, wd, g_post]
    if with_loss:
        in_specs.append(tok)
        args.append(target)
        out_shape.append(jax.ShapeDtypeStruct((n_i * 8, 128), F32))
        out_specs.append(pl.BlockSpec((8, 128), lambda i, j: (i, 0)))
    return pl.pallas_call(
        body, name=name, grid=(n_i, N_CHUNK), in_specs=in_specs, out_specs=out_specs, out_shape=out_shape,
        scratch_shapes=[pltpu.VMEM((tm, D_MODEL), BF16), pltpu.VMEM((tm, D_MODEL), F32)],
        compiler_params=_params(2),
    )(*args)


def _ffn_bwd_pre(d_out, x, f, g_pre, g_post, name):
    t = x.shape[0]
    tm = _token_tile(t)

    def body(do_ref, x_ref, f_ref, gpre_ref, gpost_ref, n_ref, df_ref, dgpost_ref):
        i = pl.program_id(0)
        n_ref[...] = _rms_fwd(x_ref[...], gpre_ref[...]).astype(BF16)
        df, dg = _rms_bwd(f_ref[...], gpost_ref[...], 0.5 * do_ref[...])
        df_ref[...] = df.astype(BF16)
        _accumulate(dgpost_ref, dg, i == 0)

    tok = pl.BlockSpec((tm, D_MODEL), lambda i: (i, 0))
    vec = pl.BlockSpec((1, D_MODEL), lambda i: (0, 0))
    return pl.pallas_call(
        body, name=name, grid=(t // tm,), in_specs=[tok, tok, tok, vec, vec], out_specs=[tok, tok, vec],
        out_shape=[jax.ShapeDtypeStruct((t, D_MODEL), BF16), jax.ShapeDtypeStruct((t, D_MODEL), BF16),
                   jax.ShapeDtypeStruct((1, D_MODEL), F32)],
        compiler_params=_params(1),
    )(d_out, x, f, g_pre, g_post)


def _ffn_bwd_main(n, df, gu, wgu, wd, name):
    t = n.shape[0]
    tm = _token_tile(t)

    def body(n_ref, df_ref, gu_ref, wgu_ref, wd_ref, dn_ref, dwgu_ref, dwd_ref):
        i = pl.program_id(1)
        nb = n_ref[...]
        dfb = df_ref[...]
        gate = gu_ref[0, 0].astype(F32)
        up = gu_ref[1, 0].astype(F32)
        s = _sigmoid(gate)
        silu = gate * s
        a = (silu * up).astype(BF16)
        da = _dot_nt(dfb, wd_ref[0])
        dup = (da * silu).astype(BF16)
        dgate = (da * up * (s * (1.0 + gate * (1.0 - s)))).astype(BF16)
        dn_ref[0] = _dot_nt(dgate, wgu_ref[0, 0]) + _dot_nt(dup, wgu_ref[1, 0])
        first = i == 0
        _accumulate(dwgu_ref.at[0, 0], _dot_tn(nb, dgate), first)
        _accumulate(dwgu_ref.at[1, 0], _dot_tn(nb, dup), first)
        _accumulate(dwd_ref.at[0], _dot_tn(a, dfb), first)

    tok = pl.BlockSpec((tm, D_MODEL), lambda j, i: (i, 0))
    wgu_spec = pl.BlockSpec((2, 1, D_MODEL, CHUNK), lambda j, i: (0, j, 0, 0))
    wd_spec = pl.BlockSpec((1, CHUNK, D_MODEL), lambda j, i: (j, 0, 0))
    return pl.pallas_call(
        body, name=name, grid=(N_CHUNK, t // tm),
        in_specs=[tok, tok, pl.BlockSpec((2, 1, tm, CHUNK), lambda j, i: (0, j, i, 0)), wgu_spec, wd_spec],
        out_specs=[pl.BlockSpec((1, tm, D_MODEL), lambda j, i: (j, i, 0)), wgu_spec, wd_spec],
        out_shape=[jax.ShapeDtypeStruct((N_CHUNK, t, D_MODEL), F32),
                   jax.ShapeDtypeStruct((2, N_CHUNK, D_MODEL, CHUNK), F32),
                   jax.ShapeDtypeStruct((N_CHUNK, CHUNK, D_MODEL), F32)],
        compiler_params=_params(2),
    )(n, df, gu, wgu, wd)


def _ffn_bwd_post(dn_parts, x, g_pre, d_out, name):
    t = x.shape[0]
    tm = _token_tile(t)

    def body(dn_ref, x_ref, gpre_ref, do_ref, dx_ref, dgpre_ref):
        i = pl.program_id(0)
        dn = (dn_ref[0] + dn_ref[1]) + (dn_ref[2] + dn_ref[3])
        dx, dg = _rms_bwd(x_ref[...], gpre_ref[...], dn)
        dx_ref[...] = do_ref[...] + dx
        _accumulate(dgpre_ref, dg, i == 0)

    tok = pl.BlockSpec((tm, D_MODEL), lambda i: (i, 0))
    vec = pl.BlockSpec((1, D_MODEL), lambda i: (0, 0))
    return pl.pallas_call(
        body, name=name, grid=(t // tm,),
        in_specs=[pl.BlockSpec((N_CHUNK, tm, D_MODEL), lambda i: (0, i, 0)), tok, vec, tok], out_specs=[tok, vec],
        out_shape=[jax.ShapeDtypeStruct((t, D_MODEL), F32), jax.ShapeDtypeStruct((1, D_MODEL), F32)],
        compiler_params=_params(1),
    )(dn_parts, x, g_pre, d_out)


def _ffn_bwd(d_out, x, f, gu, g_pre, wgu, wd, g_post, tag):
    n, df, dg_post = _ffn_bwd_pre(d_out, x, f, g_pre, g_post, tag + "_bwd_pre")
    dn_parts, dwgu, dwd = _ffn_bwd_main(n, df, gu, wgu, wd, tag + "_bwd_main")
    dx, dg_pre = _ffn_bwd_post(dn_parts, x, g_pre, d_out, tag + "_bwd_post")
    return dx, dg_pre, dwgu, dwd, dg_post


def _mix_in_fwd(x1, g, w_in):
    t = x1.shape[0]
    tm = _token_tile(t)

    def body(x_ref, g_ref, w_ref, xl_ref, gl_ref, q_ref, kv_ref):
        n = _rms_fwd(x_ref[...], g_ref[...]).astype(BF16)
        proj = _dot(n, w_ref[...])
        xl_ref[...] = proj[:, 0:512]
        gl_ref[...] = proj[:, 512:1024]
        q_ref[...] = proj[:, 1024:1536].astype(BF16)
        kv_ref[...] = proj[:, 1536:2048].astype(BF16)

    tok = pl.BlockSpec((tm, D_MODEL), lambda i: (i, 0))
    half = pl.BlockSpec((tm, 512), lambda i: (i, 0))
    return pl.pallas_call(
        body, name="mix_in_fwd", grid=(t // tm,),
        in_specs=[tok, pl.BlockSpec((1, D_MODEL), lambda i: (0, 0)), pl.BlockSpec((D_MODEL, D_IN_DUP), lambda i: (0, 0))],
        out_specs=[half, half, half, half],
        out_shape=[jax.ShapeDtypeStruct((t, 512), F32), jax.ShapeDtypeStruct((t, 512), F32),
                   jax.ShapeDtypeStruct((t, 512), BF16), jax.ShapeDtypeStruct((t, 512), BF16)],
        compiler_params=_params(1),
    )(x1, g, w_in)


def _shift_down(x, before, s):
    if s == 0:
        return x
    rolled = pltpu.roll(x, s, 0)
    ext = jnp.concatenate([before, x[0:8]], axis=0)
    first8 = pltpu.roll(ext, s, 0)[8:16]
    return jnp.concatenate([first8, rolled[8:]], axis=0)


def _shift_up(x, after, s):
    if s == 0:
        return x
    rows = x.shape[0]
    rolled = pltpu.roll(x, rows - s, 0)
    ext = jnp.concatenate([x[rows - 8:rows], after], axis=0)
    last8 = pltpu.roll(ext, 16 - s, 0)[0:8]
    return jnp.concatenate([rolled[:rows - 8], last8], axis=0)


def _log_sigmoid(x):
    e = jnp.exp(-jnp.abs(x))
    log1p_e = jnp.where(e < 0.01, e * (1.0 - e * (0.5 - e * (1.0 / 3.0))), jnp.log(1.0 + e))
    return jnp.minimum(x, 0.0) - log1p_e


def _lru_gates(xc, p_ref, wrg, wig):
    xcb = xc.astype(BF16)
    r = _sigmoid(_dot(xcb, wrg) + p_ref[1:2, :])
    ig = _sigmoid(_dot(xcb, wig) + p_ref[2:3, :])
    ls = _log_sigmoid(p_ref[3:4, :])
    log_a = LRU_C * r * ls
    a = jnp.exp(log_a)
    z = 2.0 * log_a
    series = z * (1.0 + z * (0.5 + z * (1.0 / 6.0 + z * (1.0 / 24.0 + z * (1.0 / 120.0 + z * (1.0 / 720.0))))))
    expm1 = jnp.where(z > -0.1, series, jnp.exp(z) - 1.0)
    mult = jnp.sqrt(-expm1)
    return xcb, r, ig, ls, a, mult


def _conv_taps(x, before, p_ref):
    xc = x * p_ref[7:8, :]
    for s in (1, 2, 3):
        xc = xc + _shift_down(x, before, s) * p_ref[7 - s:8 - s, :]
    return xc + p_ref[0:1, :]


def _lru_block_rows(t):
    return 512 if t >= 1024 else t // 2


def _lru_fwd(xl, p, wrg2, wig2):
    t = xl.shape[0]
    tb = _lru_block_rows(t)

    def body(xl_ref, p_ref, wrg_ref, wig_ref, h_ref, x_tail, h_carry):
        tt = pl.program_id(1)

        @pl.when(tt == 0)
        def _():
            x_tail[...] = jnp.zeros_like(x_tail)
            h_carry[...] = jnp.zeros_like(h_carry)

        x = xl_ref[...]
        xc = _conv_taps(x, x_tail[...], p_ref)
        x_tail[...] = x[tb - 8:tb]
        _, r, ig, ls, a, mult = _lru_gates(xc, p_ref, wrg_ref[0], wig_ref[0])
        u = mult * ig * xc
        row = lax.broadcasted_iota(jnp.int32, (tb, LRU_GROUP), 0)
        s = 1
        while s < tb:
            keep = row >= s
            u = jnp.where(keep, a * pltpu.roll(u, s, 0) + u, u)
            a = jnp.where(keep, a * pltpu.roll(a, s, 0), a)
            s *= 2
        h = u + a * h_carry[0:1, :]
        h_ref[...] = h
        h_carry[...] = jnp.broadcast_to(h[tb - 1:tb], h_carry.shape)

    blk = pl.BlockSpec((tb, LRU_GROUP), lambda g, tt: (tt, g))
    par = pl.BlockSpec((8, LRU_GROUP), lambda g, tt: (0, g))
    wsp = pl.BlockSpec((1, LRU_GROUP, LRU_GROUP), lambda g, tt: (g, 0, 0))
    return pl.pallas_call(
        body, name="lru_fwd", grid=(N_LRU_GROUP, t // tb), in_specs=[blk, par, wsp, wsp], out_specs=blk,
        out_shape=jax.ShapeDtypeStruct((t, D_LRU), F32),
        scratch_shapes=[pltpu.VMEM((8, LRU_GROUP), F32), pltpu.VMEM((8, LRU_GROUP), F32)],
        compiler_params=_params(2),
    )(xl, p, wrg2, wig2)


def _lru_bwd(dy, h, xl, gl, p, wrg2, wig2):
    t = xl.shape[0]
    tb = _lru_block_rows(t)
    n_tb = t // tb
    tb8 = tb // 8

    def body(dy_ref, h_ref, hprev_ref, xl_ref, xprev_ref, gl_ref, p_ref, wrg_ref, wig_ref,
             dxl_ref, dgl_ref, dp_ref, dwrg_ref, dwig_ref, g_carry, a_carry, dxc_head):
        step = pl.program_id(1)
        tt = n_tb - 1 - step
        first = step == 0

        @pl.when(first)
        def _():
            g_carry[...] = jnp.zeros_like(g_carry)
            a_carry[...] = jnp.zeros_like(a_carry)
            dxc_head[...] = jnp.zeros_like(dxc_head)

        has_prev = (tt > 0).astype(F32)
        x = xl_ref[...]
        x_before = xprev_ref[...] * has_prev
        xs = [_shift_down(x, x_before, s) for s in range(4)]
        xc = xs[0] * p_ref[7:8, :] + xs[1] * p_ref[6:7, :] + xs[2] * p_ref[5:6, :] + xs[3] * p_ref[4:5, :] + p_ref[0:1, :]
        wrg = wrg_ref[0]
        wig = wig_ref[0]
        xcb, r, ig, ls, a, mult = _lru_gates(xc, p_ref, wrg, wig)

        hh = h_ref[...]
        h_m1 = _shift_down(hh, hprev_ref[...] * has_prev, 1)
        ge, dge = _gelu(gl_ref[...])
        dy = dy_ref[...]
        dgl_ref[...] = dy * hh * dge
        dh = dy * ge

        b = _shift_up(a, a_carry[...], 1)
        row = lax.broadcasted_iota(jnp.int32, (tb, LRU_GROUP), 0)
        g = dh
        s = 1
        while s < tb:
            keep = row < tb - s
            g = jnp.where(keep, b * pltpu.roll(g, tb - s, 0) + g, g)
            b = jnp.where(keep, b * pltpu.roll(b, tb - s, 0), b)
            s *= 2
        g = g + b * g_carry[0:1, :]
        g_carry[...] = jnp.broadcast_to(g[0:1], g_carry.shape)
        a_carry[...] = jnp.broadcast_to(a[0:1], a_carry.shape)

        da = g * h_m1
        dmult = g * ig * xc
        dig = g * mult * xc
        dxc = g * mult * ig
        dlog_a = da * a - dmult * (a * a) / mult
        dr = dlog_a * (LRU_C * ls)
        dls = jnp.sum(dlog_a * (LRU_C * r), axis=0, keepdims=True)
        dlam = dls * _sigmoid(-p_ref[3:4, :])
        dpre_r = dr * r * (1.0 - r)
        dpre_i = dig * ig * (1.0 - ig)
        dprb = dpre_r.astype(BF16)
        dpib = dpre_i.astype(BF16)
        dxc = dxc + _dot_nt(dprb, wrg) + _dot_nt(dpib, wig)
        _accumulate(dwrg_ref.at[0], _dot_tn(xcb, dprb), first)
        _accumulate(dwig_ref.at[0], _dot_tn(xcb, dpib), first)

        after = dxc_head[...]
        dxl = dxc * p_ref[7:8, :]
        for s in (1, 2, 3):
            dxl = dxl + _shift_up(dxc, after, s) * p_ref[7 - s:8 - s, :]
        dxl_ref[...] = dxl
        dxc_head[...] = dxc[0:8]

        rows = [jnp.sum(dxc, axis=0, keepdims=True), jnp.sum(dpre_r, axis=0, keepdims=True),
                jnp.sum(dpre_i, axis=0, keepdims=True), dlam]
        rows += [jnp.sum(dxc * xs[3 - k], axis=0, keepdims=True) for k in range(4)]
        _accumulate(dp_ref, jnp.concatenate(rows, axis=0), first)

    blk = pl.BlockSpec((tb, LRU_GROUP), lambda g, s: (n_tb - 1 - s, g))
    prev8 = pl.BlockSpec((8, LRU_GROUP), lambda g, s: (jnp.maximum((n_tb - 1 - s) * tb8 - 1, 0), g))
    par = pl.BlockSpec((8, LRU_GROUP), lambda g, s: (0, g))
    wsp = pl.BlockSpec((1, LRU_GROUP, LRU_GROUP), lambda g, s: (g, 0, 0))
    return pl.pallas_call(
        body, name="lru_bwd", grid=(N_LRU_GROUP, n_tb),
        in_specs=[blk, blk, prev8, blk, prev8, blk, par, wsp, wsp], out_specs=[blk, blk, par, wsp, wsp],
        out_shape=[jax.ShapeDtypeStruct((t, D_LRU), F32), jax.ShapeDtypeStruct((t, D_LRU), F32),
                   jax.ShapeDtypeStruct((8, D_LRU), F32),
                   jax.ShapeDtypeStruct((N_LRU_GROUP, LRU_GROUP, LRU_GROUP), F32),
                   jax.ShapeDtypeStruct((N_LRU_GROUP, LRU_GROUP, LRU_GROUP), F32)],
        scratch_shapes=[pltpu.VMEM((8, LRU_GROUP), F32)] * 3,
        compiler_params=_params(2),
    )(dy, h, h, xl, xl, gl, p, wrg2, wig2)


def _attn_scores(qv, kvv, n, sinks, lo):
    r0 = pl.multiple_of(n * BLOCK_Q, BLOCK_Q)
    rp = pl.multiple_of(jnp.maximum(n - 1, 0) * BLOCK_Q, BLOCK_Q)
    kvb = jnp.concatenate([kvv[pl.ds(rp, BLOCK_Q), :], kvv[pl.ds(r0, BLOCK_Q), :]], axis=0)
    k2 = kvb[:, 0:128]
    v2 = kvb[:, 128:256]
    qs = _stack_heads(qv[pl.ds(r0, BLOCK_Q), :], lo)
    s = _dot_nt(qs, k2) * ATTN_SCALE
    qi = jnp.bitwise_and(lax.broadcasted_iota(jnp.int32, s.shape, 0), BLOCK_Q - 1)
    kj = lax.broadcasted_iota(jnp.int32, s.shape, 1)
    rel = qi + BLOCK_Q - kj
    mask = (rel >= 0) & (rel < BLOCK_Q) & ((n > 0) | (kj >= BLOCK_Q))
    s = jnp.where(mask, s, MASK_VALUE)
    hrow = lax.broadcasted_iota(jnp.int32, (4 * BLOCK_Q, 1), 0)
    sk = jnp.where(hrow < BLOCK_Q, sinks[0],
                   jnp.where(hrow < 2 * BLOCK_Q, sinks[1], jnp.where(hrow < 3 * BLOCK_Q, sinks[2], sinks[3])))
    m = jnp.maximum(jnp.max(s, axis=-1, keepdims=True), sk)
    e = jnp.exp(s - m)
    es = jnp.exp(sk - m)
    inv = 1.0 / (jnp.sum(e, axis=-1, keepdims=True) + es)
    return r0, rp, qs, k2, v2, e * inv, es * inv


def _stack_heads(pair2, lo):
    p0 = pair2[:, 0:128]
    p1 = pair2[:, 128:256]
    z = jnp.zeros_like(p0)
    return jnp.concatenate([jnp.where(lo, p0, z), jnp.where(lo, z, p0), jnp.where(lo, p1, z), jnp.where(lo, z, p1)], axis=0)


def _unstack_heads(st, lo):
    b = BLOCK_Q
    return jnp.concatenate([jnp.where(lo, st[0:b], st[b:2 * b]), jnp.where(lo, st[2 * b:3 * b], st[3 * b:4 * b])], axis=1)


def _attn_fwd(q, kv, sinks):
    t = q.shape[0]
    n_blk = t // BLOCK_Q

    def body(q_hbm, kv_hbm, s_ref, o_hbm, qv, kvv, ov, sem):
        lo = lax.broadcasted_iota(jnp.int32, (BLOCK_Q, 128), 1) < HEAD_DIM
        for g in range(2):
            cols = pl.ds(256 * g, 256)
            loads = [pltpu.make_async_copy(q_hbm.at[:, cols], qv, sem.at[0]),
                     pltpu.make_async_copy(kv_hbm.at[:, cols], kvv, sem.at[1])]
            for cp in loads:
                cp.start()
            for cp in loads:
                cp.wait()
            sinks_g = [s_ref[0, 4 * g + i] for i in range(4)]

            def block(n, carry):
                r0, _, _, _, v2, prob, _ = _attn_scores(qv, kvv, n, sinks_g, lo)
                ov[pl.ds(r0, BLOCK_Q), :] = _unstack_heads(_dot(prob.astype(BF16), v2), lo)
                return carry

            lax.fori_loop(0, n_blk, block, 0)
            store = pltpu.make_async_copy(ov, o_hbm.at[:, cols], sem.at[2])
            store.start()
            store.wait()

    return pl.pallas_call(
        body, name="attn_fwd", in_specs=[ANY, ANY, SMEM], out_specs=ANY,
        out_shape=jax.ShapeDtypeStruct((t, D_ATTN), F32),
        scratch_shapes=[pltpu.VMEM((t, 256), BF16), pltpu.VMEM((t, 256), BF16), pltpu.VMEM((t, 256), F32),
                        pltpu.SemaphoreType.DMA((3,))],
        compiler_params=_params(),
    )(q, kv, sinks)


def _attn_bwd(q, kv, do, sinks):
    t = q.shape[0]
    n_blk = t // BLOCK_Q

    def body(q_hbm, kv_hbm, do_hbm, s_ref, dq_hbm, dkv_hbm, dsink_ref, qv, kvv, dov, dqv, dkvv, ds_acc, sem):
        lo = lax.broadcasted_iota(jnp.int32, (BLOCK_Q, 128), 1) < HEAD_DIM
        for g in range(2):
            cols = pl.ds(256 * g, 256)
            loads = [pltpu.make_async_copy(q_hbm.at[:, cols], qv, sem.at[0]),
                     pltpu.make_async_copy(kv_hbm.at[:, cols], kvv, sem.at[1]),
                     pltpu.make_async_copy(do_hbm.at[:, cols], dov, sem.at[2])]
            for cp in loads:
                cp.start()
            for cp in loads:
                cp.wait()
            sinks_g = [s_ref[0, 4 * g + i] for i in range(4)]
            ds_acc[...] = jnp.zeros_like(ds_acc)

            def block(n, carry):
                r0, rp, qs, k2, v2, prob, psink = _attn_scores(qv, kvv, n, sinks_g, lo)
                pb = prob.astype(BF16)
                dos = _stack_heads(dov[pl.ds(r0, BLOCK_Q), :], lo)
                dp = _dot_nt(dos, v2)
                dsum = jnp.sum(prob * dp, axis=-1, keepdims=True)
                dsb = (prob * (dp - dsum) * ATTN_SCALE).astype(BF16)
                ds_acc[...] -= psink * dsum
                dqv[pl.ds(r0, BLOCK_Q), :] = _unstack_heads(_dot(dsb, k2), lo).astype(BF16)
                dk2 = _dot_tn(dsb, qs)
                dv2 = _dot_tn(pb, dos)
                dkvv[pl.ds(r0, BLOCK_Q), :] = jnp.concatenate([dk2[BLOCK_Q:], dv2[BLOCK_Q:]], axis=1)

                @pl.when(n > 0)
                def _():
                    dkvv[pl.ds(rp, BLOCK_Q), :] += jnp.concatenate([dk2[:BLOCK_Q], dv2[:BLOCK_Q]], axis=1)

                return carry

            lax.fori_loop(0, n_blk, block, 0)
            for i in range(4):
                tot = jnp.sum(ds_acc[BLOCK_Q * i:BLOCK_Q * (i + 1), :], axis=0, keepdims=True)
                dsink_ref[4 * g + i:4 * g + i + 1, :] = jnp.broadcast_to(tot, (1, 128))
            stores = [pltpu.make_async_copy(dqv, dq_hbm.at[:, cols], sem.at[0]),
                      pltpu.make_async_copy(dkvv, dkv_hbm.at[:, cols], sem.at[1])]
            for cp in stores:
                cp.start()
            for cp in stores:
                cp.wait()

    return pl.pallas_call(
        body, name="attn_bwd", in_specs=[ANY, ANY, ANY, SMEM],
        out_specs=[ANY, ANY, pl.BlockSpec(memory_space=pltpu.VMEM)],
        out_shape=[jax.ShapeDtypeStruct((t, D_ATTN), BF16), jax.ShapeDtypeStruct((t, 512), F32),
                   jax.ShapeDtypeStruct((8, 128), F32)],
        scratch_shapes=[pltpu.VMEM((t, 256), BF16), pltpu.VMEM((t, 256), BF16), pltpu.VMEM((t, 256), BF16),
                        pltpu.VMEM((t, 256), BF16), pltpu.VMEM((t, 256), F32), pltpu.VMEM((4 * BLOCK_Q, 1), F32),
                        pltpu.SemaphoreType.DMA((3,))],
        compiler_params=_params(),
    )(q, kv, do, sinks)


def _mix_out_fwd(x1, h, gl, o, g_lru, g_attn, g_post, w_o):
    t = x1.shape[0]
    tm = _token_tile(t)

    def body(x_ref, h_ref, gl_ref, o_ref, g1_ref, g2_ref, gp_ref, w_ref, x2_ref, m_ref):
        y = h_ref[...] * _gelu(gl_ref[...])[0]
        yn1 = _rms_fwd(y, g1_ref[...]).astype(BF16)
        yn2 = _rms_fwd(o_ref[...], g2_ref[...]).astype(BF16)
        m = _dot(yn1, w_ref[0:512, :]) + _dot(yn2, w_ref[512:1024, :])
        m_ref[...] = m
        x2_ref[...] = x_ref[...] + _rms_fwd(m, gp_ref[...])

    tok = pl.BlockSpec((tm, D_MODEL), lambda i: (i, 0))
    half = pl.BlockSpec((tm, 512), lambda i: (i, 0))
    vec = pl.BlockSpec((1, D_MODEL), lambda i: (0, 0))
    hvec = pl.BlockSpec((1, 512), lambda i: (0, 0))
    return pl.pallas_call(
        body, name="mix_out_fwd", grid=(t // tm,),
        in_specs=[tok, half, half, half, hvec, hvec, vec, pl.BlockSpec((D_MODEL, D_MODEL), lambda i: (0, 0))],
        out_specs=[tok, tok],
        out_shape=[jax.ShapeDtypeStruct((t, D_MODEL), F32), jax.ShapeDtypeStruct((t, D_MODEL), F32)],
        compiler_params=_params(1),
    )(x1, h, gl, o, g_lru, g_attn, g_post, w_o)


def _mix_out_bwd(dx2, m, h, gl, o, g_lru, g_attn, g_post, w_o):
    t = dx2.shape[0]
    tm = _token_tile(t)

    def body(dx_ref, m_ref, h_ref, gl_ref, o_ref, g1_ref, g2_ref, gp_ref, w_ref,
             dy_ref, do_ref, dw_ref, dgp_ref, dg1_ref, dg2_ref):
        first = pl.program_id(0) == 0
        dm, dgp = _rms_bwd(m_ref[...], gp_ref[...], dx_ref[...])
        dmb = dm.astype(BF16)
        y = h_ref[...] * _gelu(gl_ref[...])[0]
        o = o_ref[...]
        yn1 = _rms_fwd(y, g1_ref[...]).astype(BF16)
        yn2 = _rms_fwd(o, g2_ref[...]).astype(BF16)
        _accumulate(dw_ref.at[0:512, :], _dot_tn(yn1, dmb), first)
        _accumulate(dw_ref.at[512:1024, :], _dot_tn(yn2, dmb), first)
        dy, dg1 = _rms_bwd(y, g1_ref[...], _dot_nt(dmb, w_ref[0:512, :]))
        do, dg2 = _rms_bwd(o, g2_ref[...], _dot_nt(dmb, w_ref[512:1024, :]))
        dy_ref[...] = dy
        do_ref[...] = do.astype(BF16)
        _accumulate(dgp_ref, dgp, first)
        _accumulate(dg1_ref, dg1, first)
        _accumulate(dg2_ref, dg2, first)

    tok = pl.BlockSpec((tm, D_MODEL), lambda i: (i, 0))
    half = pl.BlockSpec((tm, 512), lambda i: (i, 0))
    vec = pl.BlockSpec((1, D_MODEL), lambda i: (0, 0))
    hvec = pl.BlockSpec((1, 512), lambda i: (0, 0))
    mat = pl.BlockSpec((D_MODEL, D_MODEL), lambda i: (0, 0))
    return pl.pallas_call(
        body, name="mix_out_bwd", grid=(t // tm,),
        in_specs=[tok, tok, half, half, half, hvec, hvec, vec, mat],
        out_specs=[half, half, mat, vec, hvec, hvec],
        out_shape=[jax.ShapeDtypeStruct((t, 512), F32), jax.ShapeDtypeStruct((t, 512), BF16),
                   jax.ShapeDtypeStruct((D_MODEL, D_MODEL), F32), jax.ShapeDtypeStruct((1, D_MODEL), F32),
                   jax.ShapeDtypeStruct((1, 512), F32), jax.ShapeDtypeStruct((1, 512), F32)],
        compiler_params=_params(1),
    )(dx2, m, h, gl, o, g_lru, g_attn, g_post, w_o)


def _mix_in_bwd(dx2, x1, g, dxl, dgl, dq, dkv, w_in):
    t = x1.shape[0]
    tm = _token_tile(t)

    def body(dx2_ref, x_ref, g_ref, dxl_ref, dgl_ref, dq_ref, dkv_ref, w_ref, dx1_ref, dw_ref, dg_ref):
        first = pl.program_id(0) == 0
        x = x_ref[...]
        nb = _rms_fwd(x, g_ref[...]).astype(BF16)
        lo = lax.broadcasted_iota(jnp.int32, (tm, 128), 1) < HEAD_DIM
        dkv = dkv_ref[...]
        folded = []
        for k in range(4):
            seg = dkv[:, 128 * k:128 * (k + 1)]
            folded.append(jnp.where(lo, seg + pltpu.roll(seg, HEAD_DIM, 1), 0.0).astype(BF16))
        dproj = jnp.concatenate([dxl_ref[...].astype(BF16), dgl_ref[...].astype(BF16), dq_ref[...]] + folded, axis=1)
        _accumulate(dw_ref, _dot_tn(nb, dproj), first)
        dx, dg = _rms_bwd(x, g_ref[...], _dot_nt(dproj, w_ref[...]))
        dx1_ref[...] = dx2_ref[...] + dx
        _accumulate(dg_ref, dg, first)

    tok = pl.BlockSpec((tm, D_MODEL), lambda i: (i, 0))
    half = pl.BlockSpec((tm, 512), lambda i: (i, 0))
    vec = pl.BlockSpec((1, D_MODEL), lambda i: (0, 0))
    mat = pl.BlockSpec((D_MODEL, D_IN_DUP), lambda i: (0, 0))
    return pl.pallas_call(
        body, name="mix_in_bwd", grid=(t // tm,),
        in_specs=[tok, tok, vec, half, half, half, half, mat], out_specs=[tok, mat, vec],
        out_shape=[jax.ShapeDtypeStruct((t, D_MODEL), F32), jax.ShapeDtypeStruct((D_MODEL, D_IN_DUP), F32),
                   jax.ShapeDtypeStruct((1, D_MODEL), F32)],
        compiler_params=_params(1),
    )(dx2, x1, g, dxl, dgl, dq, dkv, w_in)


def _coords():
    return lax.axis_index("x"), lax.axis_index("y"), lax.axis_index("c")


def _all_gather(shards, name):
    k_arr = len(shards)

    def body(*refs):
        ins, outs = refs[:k_arr], refs[k_arr:2 * k_arr]
        send_sems, recv_sems, local_sems = refs[2 * k_arr:]
        x, y, c = _coords()
        me, sibling = (x, y, c), (x, y, 1 - c)
        chips = [(1 - x, y), (x, 1 - y), (1 - x, 1 - y)]

        def rows(k, dev):
            return outs[k].at[4 * dev[0] + 2 * dev[1] + dev[2]]

        def copy(k, slot, block, to, src=None):
            return pltpu.make_async_remote_copy(
                src_ref=rows(k, block) if src is None else src, dst_ref=rows(k, block),
                send_sem=send_sems.at[7 * k + slot], recv_sem=recv_sems.at[7 * k + slot],
                device_id=to, device_id_type=MESH)

        mine = [pltpu.make_async_copy(ins[k], rows(k, me), local_sems.at[k]) for k in range(k_arr)]
        for cp in mine:
            cp.start()
        sent = []
        for k in range(k_arr):
            sent.append(copy(k, 0, me, sibling, src=ins[k]))
            sent += [copy(k, 1 + j, me, (*chip, c), src=ins[k]) for j, chip in enumerate(chips)]
        for cp in sent:
            cp.start()
        for j, chip in enumerate(chips):
            for k in range(k_arr):
                copy(k, 1 + j, (*chip, c), me).wait_recv()
                fwd = copy(k, 4 + j, (*chip, c), sibling)
                fwd.start()
                sent.append(fwd)
        for k in range(k_arr):
            copy(k, 0, sibling, me).wait_recv()
            for j, chip in enumerate(chips):
                copy(k, 4 + j, (*chip, 1 - c), me).wait_recv()
        for cp in sent:
            cp.wait_send()
        for cp in mine:
            cp.wait()

    return pl.pallas_call(
        body, name=name, in_specs=[ANY] * k_arr, out_specs=[ANY] * k_arr,
        out_shape=[jax.ShapeDtypeStruct((N_DEV,) + s.shape, s.dtype) for s in shards],
        scratch_shapes=[pltpu.SemaphoreType.DMA((7 * k_arr,)), pltpu.SemaphoreType.DMA((7 * k_arr,)),
                        pltpu.SemaphoreType.DMA((k_arr,))],
    )(*shards)


def _sibling_exchange(grads):
    k_arr = len(grads)

    def body(*refs):
        ins, outs = refs[:k_arr], refs[k_arr:2 * k_arr]
        send_sems, recv_sems = refs[2 * k_arr:]
        x, y, c = _coords()
        copies = []
        for k in range(k_arr):
            for q in range(4):
                copies.append(pltpu.make_async_remote_copy(
                    src_ref=ins[k].at[2 * q + (1 - c)], dst_ref=outs[k].at[q],
                    send_sem=send_sems.at[4 * k + q], recv_sem=recv_sems.at[4 * k + q],
                    device_id=(x, y, 1 - c), device_id_type=MESH))
        for cp in copies:
            cp.start()
        for cp in copies:
            cp.wait_recv()
        for cp in copies:
            cp.wait_send()

    return pl.pallas_call(
        body, name="rs_sibling_exchange", in_specs=[ANY] * k_arr, out_specs=[ANY] * k_arr,
        out_shape=[jax.ShapeDtypeStruct((4,) + g.shape[1:], g.dtype) for g in grads],
        scratch_shapes=[pltpu.SemaphoreType.DMA((4 * k_arr,)), pltpu.SemaphoreType.DMA((4 * k_arr,))],
    )(*grads)


def _chip_exchange(chip_sums):
    k_arr = len(chip_sums)

    def body(*refs):
        ins, outs = refs[:k_arr], refs[k_arr:2 * k_arr]
        send_sems, recv_sems = refs[2 * k_arr:]
        x, y, c = _coords()
        chips = [(1 - x, y), (x, 1 - y), (1 - x, 1 - y)]
        copies = []
        for k in range(k_arr):
            for j, chip in enumerate(chips):
                copies.append(pltpu.make_async_remote_copy(
                    src_ref=ins[k].at[2 * chip[0] + chip[1]], dst_ref=outs[k].at[j],
                    send_sem=send_sems.at[3 * k + j], recv_sem=recv_sems.at[3 * k + j],
                    device_id=(*chip, c), device_id_type=MESH))
        for cp in copies:
            cp.start()
        for cp in copies:
            cp.wait_recv()
        for cp in copies:
            cp.wait_send()

    return pl.pallas_call(
        body, name="rs_chip_exchange", in_specs=[ANY] * k_arr, out_specs=[ANY] * k_arr,
        out_shape=[jax.ShapeDtypeStruct((3,) + s.shape[1:], s.dtype) for s in chip_sums],
        scratch_shapes=[pltpu.SemaphoreType.DMA((3 * k_arr,)), pltpu.SemaphoreType.DMA((3 * k_arr,))],
    )(*chip_sums)


def _row_tile(rows):
    return rows if rows <= 512 else 256


def _chip_sum(grad, from_sibling, core, name):
    _, rows, cols = grad.shape
    tr = _row_tile(rows)

    def body(core_ref, g_ref, s_ref, out_ref):
        out_ref[0] = (g_ref[0, 0] + s_ref[0]).astype(BF16)

    grid_spec = pltpu.PrefetchScalarGridSpec(
        num_scalar_prefetch=1, grid=(4, rows // tr),
        in_specs=[pl.BlockSpec((1, 1, tr, cols), lambda q, i, core: (q, core[0], i, 0)),
                  pl.BlockSpec((1, tr, cols), lambda q, i, core: (q, i, 0))],
        out_specs=pl.BlockSpec((1, tr, cols), lambda q, i, core: (q, i, 0)))
    return pl.pallas_call(
        body, name=name, grid_spec=grid_spec, out_shape=jax.ShapeDtypeStruct((4, rows, cols), BF16),
        compiler_params=_params(2),
    )(core, grad.reshape(4, 2, rows, cols), from_sibling)


def _adamw(w, g, m, v):
    m = ADAM_B1 * m + (1.0 - ADAM_B1) * g
    v = ADAM_B2 * v + (1.0 - ADAM_B2) * (g * g)
    m_hat = m / (1.0 - ADAM_B1 ** ADAM_STEP)
    v_hat = v / (1.0 - ADAM_B2 ** ADAM_STEP)
    delta = -ADAM_LR * (m_hat / (jnp.sqrt(v_hat) + ADAM_EPS) + ADAM_WD * w)
    return delta, m, v


def _shard_update(grad, from_sibling, from_chips, w, m, v, place, name):
    _, rows, cols = grad.shape
    tr = _row_tile(rows)

    def body(place_ref, g_ref, s_ref, c_ref, w_ref, m_ref, v_ref, go_ref, d_ref, mo_ref, vo_ref):
        g = g_ref[0, 0] + s_ref[0]
        g = g + c_ref[0].astype(F32)
        g = g + c_ref[1].astype(F32)
        g = g + c_ref[2].astype(F32)
        go_ref[...] = g
        d_ref[...], mo_ref[...], vo_ref[...] = _adamw(w_ref[...], g, m_ref[...], v_ref[...])

    flat = pl.BlockSpec((tr, cols), lambda i, place: (i, 0))
    grid_spec = pltpu.PrefetchScalarGridSpec(
        num_scalar_prefetch=1, grid=(rows // tr,),
        in_specs=[pl.BlockSpec((1, 1, tr, cols), lambda i, place: (place[0], place[1], i, 0)),
                  pl.BlockSpec((1, tr, cols), lambda i, place: (place[0], i, 0)),
                  pl.BlockSpec((3, tr, cols), lambda i, place: (0, i, 0)), flat, flat, flat],
        out_specs=[flat, flat, flat, flat])
    return pl.pallas_call(
        body, name=name, grid_spec=grid_spec, out_shape=[jax.ShapeDtypeStruct((rows, cols), F32)] * 4,
        compiler_params=_params(1),
    )(place, grad.reshape(4, 2, rows, cols), from_sibling, from_chips, w, m, v)


def _small_update(gathered, w, m, v):
    rows = w.shape[0]

    def body(g_ref, w_ref, m_ref, v_ref, go_ref, d_ref, mo_ref, vo_ref):
        g = g_ref[0]
        for d in range(1, N_DEV):
            g = g + g_ref[d]
        go_ref[...] = g
        d_ref[...], mo_ref[...], vo_ref[...] = _adamw(w_ref[...], g, m_ref[...], v_ref[...])

    return pl.pallas_call(
        body, name="small_update", out_shape=[jax.ShapeDtypeStruct((rows, 128), F32)] * 4,
        compiler_params=_params(),
    )(gathered, w, m, v)


SMALL_VECTORS = ("ffn1_pre_g", "ffn1_post_g", "mix_pre_g", "mix_post_g", "ffn2_pre_g", "ffn2_post_g",
                 "conv_b", "b_rg", "b_ig", "lru_lambda", "g_lru_out", "g_attn_out")


def _pad_rows(a, rows):
    return jnp.concatenate([a, jnp.zeros((rows - a.shape[0], a.shape[1]), a.dtype)], axis=0)


def _pack_small(vals, loss, conv_w_full):
    parts = [_pad_rows(jnp.pad(jnp.reshape(loss, (1, 1)), ((0, 0), (0, 127))), 8)]
    parts += [vals[k].reshape(-1, 128) for k in SMALL_VECTORS]
    parts += [vals["w_rg"].reshape(-1, 128), vals["w_ig"].reshape(-1, 128)]
    parts.append(_pad_rows(jnp.pad(vals["sinks"].reshape(1, 8), ((0, 0), (0, 120))), 8))
    parts.append(conv_w_full.reshape(16, 128))
    return jnp.concatenate(parts, axis=0)


def _unpack_small(blob, like):
    out = {}
    r = 8
    for k in SMALL_VECTORS:
        n = like[k].size // 128
        out[k] = blob[r:r + n].reshape(like[k].shape)
        r += n
    for k in ("w_rg", "w_ig"):
        out[k] = blob[r:r + 256].reshape(like[k].shape)
        r += 256
    out["sinks"] = blob[r:r + 1, 0:8].reshape(like["sinks"].shape)
    r += 8
    conv_w_full = blob[r:r + 16].reshape(4, D_LRU)
    return out, blob[0, 0], conv_w_full


def _dup_in_columns(w):
    k0, k1, v0, v1 = w[:, 1536:1600], w[:, 1600:1664], w[:, 1664:1728], w[:, 1728:1792]
    return jnp.concatenate([w[:, :1536], k0, k0, v0, v0, k1, k1, v1, v1], axis=1)


def _undup_in_columns(dw):
    return jnp.concatenate([dw[:, :1536], dw[:, 1536:1600], dw[:, 1792:1856], dw[:, 1664:1728], dw[:, 1920:1984]], axis=1)


def _pair_block_diag(w):
    w = w.reshape(N_LRU_GROUP, 2, 64, 64)
    z = jnp.zeros((N_LRU_GROUP, 64, 64), w.dtype)
    top = jnp.concatenate([w[:, 0], z], axis=2)
    bot = jnp.concatenate([z, w[:, 1]], axis=2)
    return jnp.concatenate([top, bot], axis=1)


def _pair_block_diag_grad(dw2, shape):
    return jnp.stack([dw2[:, :64, :64], dw2[:, 64:, 64:]], axis=1).reshape(shape)


def kernel(x, ffn1_pre_g, ffn1_w_gu, ffn1_w_down, ffn1_post_g, mix_pre_g, w_in, conv_w, conv_b, w_rg, b_rg, w_ig, b_ig, lru_lambda, sinks, g_lru_out, g_attn_out, w_o, mix_post_g, ffn2_pre_g, ffn2_w_gu, ffn2_w_down, ffn2_post_g, loss_target, m_ffn1_pre_g, m_ffn1_w_gu, m_ffn1_w_down, m_ffn1_post_g, m_mix_pre_g, m_w_in, m_conv_w, m_conv_b, m_w_rg, m_b_rg, m_w_ig, m_b_ig, m_lru_lambda, m_sinks, m_g_lru_out, m_g_attn_out, m_w_o, m_mix_post_g, m_ffn2_pre_g, m_ffn2_w_gu, m_ffn2_w_down, m_ffn2_post_g, v_ffn1_pre_g, v_ffn1_w_gu, v_ffn1_w_down, v_ffn1_post_g, v_mix_pre_g, v_w_in, v_conv_w, v_conv_b, v_w_rg, v_b_rg, v_w_ig, v_b_ig, v_lru_lambda, v_sinks, v_g_lru_out, v_g_attn_out, v_w_o, v_mix_post_g, v_ffn2_pre_g, v_ffn2_w_gu, v_ffn2_w_down, v_ffn2_post_g):
    args = dict(locals())
    names = ["ffn1_pre_g", "ffn1_w_gu", "ffn1_w_down", "ffn1_post_g", "mix_pre_g", "w_in", "conv_w", "conv_b", "w_rg",
             "b_rg", "w_ig", "b_ig", "lru_lambda", "sinks", "g_lru_out", "g_attn_out", "w_o", "mix_post_g",
             "ffn2_pre_g", "ffn2_w_gu", "ffn2_w_down", "ffn2_post_g"]
    big = ["ffn1_w_gu", "ffn1_w_down", "w_in", "w_o", "ffn2_w_gu", "ffn2_w_down"]
    w = {k: args[k] for k in names}
    mom = {k: args["m_" + k] for k in names}
    var = {k: args["v_" + k] for k in names}
    t = x.shape[1]
    xs = x.reshape(t, D_MODEL)
    target = loss_target.reshape(t, D_MODEL)
    cx, cy, cc = _coords()
    me = 4 * cx + 2 * cy + cc
    core = jnp.reshape(cc, (1,)).astype(jnp.int32)
    place = jnp.stack([2 * cx + cy, cc]).astype(jnp.int32)

    shard2d = {k: w[k].reshape(w[k].shape[1:]) for k in big}
    conv_pad = jnp.pad(conv_w.reshape(4, 64), ((0, 4), (0, 64)))
    gathered = _all_gather([shard2d[k].astype(BF16) for k in big] + [conv_pad], "all_gather_weights")
    wgu1 = gathered[0].reshape(2, N_CHUNK, D_MODEL, CHUNK)
    wd1 = gathered[1].reshape(N_CHUNK, CHUNK, D_MODEL)
    w_in_full = _dup_in_columns(jnp.transpose(gathered[2], (1, 0, 2)).reshape(D_MODEL, D_IN))
    w_o_full = gathered[3].reshape(D_MODEL, D_MODEL)
    wgu2 = gathered[4].reshape(2, N_CHUNK, D_MODEL, CHUNK)
    wd2 = gathered[5].reshape(N_CHUNK, CHUNK, D_MODEL)
    conv_w_full = jnp.transpose(gathered[6][:, 0:4, 0:64], (1, 0, 2)).reshape(4, D_LRU)
    p_lru = jnp.concatenate([conv_b, b_rg, b_ig, lru_lambda, conv_w_full], axis=0)
    wrg2 = _pair_block_diag(w_rg[0]).astype(BF16)
    wig2 = _pair_block_diag(w_ig[0]).astype(BF16)

    x1, f1, gu1 = _ffn_fwd(xs, ffn1_pre_g, wgu1, wd1, ffn1_post_g, None, "ffn1_fwd")
    xl, gl, q, kv = _mix_in_fwd(x1, mix_pre_g, w_in_full)
    h = _lru_fwd(xl, p_lru, wrg2, wig2)
    o = _attn_fwd(q, kv, sinks)
    x2, mo = _mix_out_fwd(x1, h, gl, o, g_lru_out, g_attn_out, mix_post_g, w_o_full)
    dx3, f2, gu2, loss_parts = _ffn_fwd(x2, ffn2_pre_g, wgu2, wd2, ffn2_post_g, target, "ffn2_fwd")
    loss_local = jnp.sum(loss_parts[::8, 0])

    g = {}
    dx2, g["ffn2_pre_g"], dwgu2, dwd2, g["ffn2_post_g"] = _ffn_bwd(dx3, x2, f2, gu2, ffn2_pre_g, wgu2, wd2, ffn2_post_g, "ffn2")
    dy, do, dwo, g["mix_post_g"], g["g_lru_out"], g["g_attn_out"] = _mix_out_bwd(
        dx2, mo, h, gl, o, g_lru_out, g_attn_out, mix_post_g, w_o_full)
    dq, dkv, dsink = _attn_bwd(q, kv, do, sinks)
    dxl, dgl, dp, dwrg2, dwig2 = _lru_bwd(dy, h, xl, gl, p_lru, wrg2, wig2)
    dx1, dwin_dup, g["mix_pre_g"] = _mix_in_bwd(dx2, x1, mix_pre_g, dxl, dgl, dq, dkv, w_in_full)
    dx0, g["ffn1_pre_g"], dwgu1, dwd1, g["ffn1_post_g"] = _ffn_bwd(dx1, xs, f1, gu1, ffn1_pre_g, wgu1, wd1, ffn1_post_g, "ffn1")
    g["conv_b"], g["b_rg"], g["b_ig"], g["lru_lambda"] = dp[0:1], dp[1:2], dp[2:3], dp[3:4]
    g["w_rg"] = _pair_block_diag_grad(dwrg2, w_rg.shape)
    g["w_ig"] = _pair_block_diag_grad(dwig2, w_ig.shape)
    g["sinks"] = dsink[:, 0].reshape(1, 8)
    dwin = jnp.transpose(_undup_in_columns(dwin_dup).reshape(D_MODEL, N_DEV, D_IN // N_DEV), (1, 0, 2))
    partial = {"ffn1_w_gu": dwgu1.reshape(N_DEV, D_MODEL, CHUNK), "ffn1_w_down": dwd1.reshape(N_DEV, D_FF // N_DEV, D_MODEL),
               "w_in": dwin, "w_o": dwo.reshape(N_DEV, D_MODEL // N_DEV, D_MODEL),
               "ffn2_w_gu": dwgu2.reshape(N_DEV, D_MODEL, CHUNK), "ffn2_w_down": dwd2.reshape(N_DEV, D_FF // N_DEV, D_MODEL)}

    from_sibling = _sibling_exchange([partial[k] for k in big])
    chip_sums = [_chip_sum(partial[k], from_sibling[i], core, "chip_sum_" + k) for i, k in enumerate(big)]
    from_chips = _chip_exchange(chip_sums)
    grads, delta, new_m, new_v = {}, {}, {}, {}
    for i, k in enumerate(big):
        shape = w[k].shape
        res = _shard_update(partial[k], from_sibling[i], from_chips[i], shard2d[k], mom[k].reshape(shape[1:]),
                            var[k].reshape(shape[1:]), place, "update_" + k)
        grads[k], delta[k], new_m[k], new_v[k] = [r.reshape(shape) for r in res]

    col = me * 64

    def conv_rows(a):
        return lax.dynamic_update_slice(jnp.zeros((4, D_LRU), F32), a.reshape(4, 64), (0, col))

    zero = jnp.zeros((), F32)
    g_blob = _pack_small(g, loss_local, dp[4:8])
    all_blobs = _all_gather([g_blob], "all_gather_small_grads")[0]
    res = _small_update(all_blobs, _pack_small(w, zero, conv_rows(conv_w)), _pack_small(mom, zero, conv_rows(m_conv_w)),
                        _pack_small(var, zero, conv_rows(v_conv_w)))
    loss = None
    for dst, blob in zip((grads, delta, new_m, new_v), res):
        small, first, conv_full = _unpack_small(blob, w)
        dst.update(small)
        dst["conv_w"] = lax.dynamic_slice(conv_full, (0, col), (4, 64)).reshape(conv_w.shape)
        if loss is None:
            loss = first
    return (loss, dx0.reshape(x.shape), *[grads[k] for k in names], *[delta[k] for k in names],
            *[new_m[k] for k in names], *[new_v[k] for k in names])
```

```python
import functools

import jax
import jax.numpy as jnp
from jax import lax
from jax.experimental import pallas as pl
from jax.experimental.pallas import tpu as pltpu

F32 = jnp.float32
BF16 = jnp.bfloat16

D_MODEL = 1024
D_FF = 2816
N_DEV = 8
N_CHUNK = 4
CHUNK = D_FF // N_CHUNK
D_LRU = 512
D_ATTN = 512
LRU_GROUP = 128
N_LRU_GROUP = D_LRU // LRU_GROUP
HEAD_DIM = 64
BLOCK_Q = 128
D_IN = 1792
D_IN_DUP = 2048
RMS_EPS = 1e-6
LRU_C = 8.0
MASK_VALUE = -1e30
ATTN_SCALE = HEAD_DIM ** -0.5

ADAM_LR = 0.001
ADAM_B1 = 0.9
ADAM_B2 = 0.999
ADAM_EPS = 1e-08
ADAM_WD = 0.01
ADAM_STEP = 10

VMEM_LIMIT_V7X = 56 * 2 ** 20

ANY = pl.BlockSpec(memory_space=pl.ANY)
SMEM = pl.BlockSpec(memory_space=pltpu.SMEM)
MESH = pl.DeviceIdType.MESH


def _params(n_grid=0):
    sem = ("arbitrary",) * n_grid if n_grid else None
    return pltpu.CompilerParams(dimension_semantics=sem, vmem_limit_bytes=VMEM_LIMIT_V7X)


def _dot(a, b):
    return lax.dot_general(a, b, (((1,), (0,)), ((), ())), preferred_element_type=F32)


def _dot_nt(a, b):
    return lax.dot_general(a, b, (((1,), (1,)), ((), ())), preferred_element_type=F32)


def _dot_tn(a, b):
    return lax.dot_general(a, b, (((0,), (0,)), ((), ())), preferred_element_type=F32)


def _sigmoid(x):
    return 1.0 / (1.0 + jnp.exp(-x))


def _rms_fwd(x, g):
    r = lax.rsqrt(jnp.mean(x * x, axis=-1, keepdims=True) + RMS_EPS)
    return x * r * g


def _rms_bwd(x, g, dy):
    r = lax.rsqrt(jnp.mean(x * x, axis=-1, keepdims=True) + RMS_EPS)
    xh = x * r
    dg = jnp.sum(dy * xh, axis=0, keepdims=True)
    dxh = dy * g
    dx = r * (dxh - xh * jnp.mean(dxh * xh, axis=-1, keepdims=True))
    return dx, dg


def _gelu(x):
    c = 0.7978845608028654
    inner = c * (x + 0.044715 * x * x * x)
    th = jnp.tanh(inner)
    ge = 0.5 * x * (1.0 + th)
    dge = 0.5 * (1.0 + th) + 0.5 * x * (1.0 - th * th) * c * (1.0 + 3.0 * 0.044715 * x * x)
    return ge, dge


def _accumulate(ref, val, first):
    @pl.when(first)
    def _():
        ref[...] = val

    @pl.when(jnp.logical_not(first))
    def _():
        ref[...] += val


def _token_tile(t):
    return 512 if t >= 2048 else t // 2


def _ffn_bwd_tile(t):
    return 1024 if t >= 4096 else t // 2


def _coords():
    return lax.axis_index("x"), lax.axis_index("y"), lax.axis_index("c")


class _Gather:
    n_phases = 3
    at = (0.0, 0.7, 1.0)

    def __init__(self, shards):
        k = len(shards)
        self.arrays = list(shards)
        self.out_shape = [jax.ShapeDtypeStruct((N_DEV,) + s.shape, s.dtype) for s in shards]
        self.scratch = [pltpu.SemaphoreType.DMA((7 * k,)), pltpu.SemaphoreType.DMA((7 * k,)), pltpu.SemaphoreType.DMA((k,))]

    def run(self, phase, ins, outs, sems):
        send_sems, recv_sems, local_sems = sems
        k_arr = len(ins)
        x, y, c = _coords()
        me, sibling = (x, y, c), (x, y, 1 - c)
        chips = [(1 - x, y), (x, 1 - y), (1 - x, 1 - y)]

        def rows(k, dev):
            return outs[k].at[4 * dev[0] + 2 * dev[1] + dev[2]]

        def copy(k, slot, block, to, src=None):
            return pltpu.make_async_remote_copy(
                src_ref=rows(k, block) if src is None else src, dst_ref=rows(k, block),
                send_sem=send_sems.at[7 * k + slot], recv_sem=recv_sems.at[7 * k + slot],
                device_id=to, device_id_type=MESH)

        def mine():
            return [pltpu.make_async_copy(ins[k], rows(k, me), local_sems.at[k]) for k in range(k_arr)]

        def first():
            return [copy(k, slot, me, to, src=ins[k]) for k in range(k_arr)
                    for slot, to in enumerate([sibling] + [(*chip, c) for chip in chips])]

        def passed(j, k):
            return copy(k, 4 + j, (*chips[j], c), sibling)

        if phase == 0:
            for cp in mine() + first():
                cp.start()
        elif phase == 1:
            for j, chip in enumerate(chips):
                for k in range(k_arr):
                    copy(k, 1 + j, (*chip, c), me).wait_recv()
                    passed(j, k).start()
        else:
            for k in range(k_arr):
                copy(k, 0, sibling, me).wait_recv()
                for j, chip in enumerate(chips):
                    copy(k, 4 + j, (*chip, 1 - c), me).wait_recv()
            for cp in first() + [passed(j, k) for j in range(3) for k in range(k_arr)]:
                cp.wait_send()
            for cp in mine():
                cp.wait()


class _SiblingExchange:
    n_phases = 2
    at = (0.0, 1.0)

    def __init__(self, grads):
        k = len(grads)
        self.arrays = list(grads)
        self.out_shape = [jax.ShapeDtypeStruct((4,) + g.shape[1:], g.dtype) for g in grads]
        self.scratch = [pltpu.SemaphoreType.DMA((4 * k,)), pltpu.SemaphoreType.DMA((4 * k,))]

    def run(self, phase, ins, outs, sems):
        send_sems, recv_sems = sems
        x, y, c = _coords()
        copies = [pltpu.make_async_remote_copy(
            src_ref=ins[k].at[2 * q + (1 - c)], dst_ref=outs[k].at[q],
            send_sem=send_sems.at[4 * k + q], recv_sem=recv_sems.at[4 * k + q],
            device_id=(x, y, 1 - c), device_id_type=MESH) for k in range(len(ins)) for q in range(4)]
        for cp in copies:
            if phase == 0:
                cp.start()
            else:
                cp.wait_recv()
                cp.wait_send()


class _ChipExchange:
    n_phases = 2
    at = (0.0, 1.0)

    def __init__(self, chip_sums):
        k = len(chip_sums)
        self.arrays = list(chip_sums)
        self.out_shape = [jax.ShapeDtypeStruct((3,) + s.shape[1:], s.dtype) for s in chip_sums]
        self.scratch = [pltpu.SemaphoreType.DMA((3 * k,)), pltpu.SemaphoreType.DMA((3 * k,))]

    def run(self, phase, ins, outs, sems):
        send_sems, recv_sems = sems
        x, y, c = _coords()
        chips = [(1 - x, y), (x, 1 - y), (1 - x, 1 - y)]
        copies = [pltpu.make_async_remote_copy(
            src_ref=ins[k].at[2 * chip[0] + chip[1]], dst_ref=outs[k].at[j],
            send_sem=send_sems.at[3 * k + j], recv_sem=recv_sems.at[3 * k + j],
            device_id=(*chip, c), device_id_type=MESH) for k in range(len(ins)) for j, chip in enumerate(chips)]
        for cp in copies:
            if phase == 0:
                cp.start()
            else:
                cp.wait_recv()
                cp.wait_send()


class _Host:
    def __init__(self, exchange):
        self.ex = exchange
        self.args = [] if exchange is None else exchange.arrays
        self.in_specs = [ANY] * len(self.args)
        self.out_shape = [] if exchange is None else exchange.out_shape
        self.out_specs = [ANY] * len(self.out_shape)
        self.scratch = [] if exchange is None else exchange.scratch

    def split(self, refs, n_in, n_out, n_scratch):
        a, b, s = len(self.args), len(self.out_shape), len(self.scratch)
        own_in, ex_in = refs[:n_in], refs[n_in:n_in + a]
        rest = refs[n_in + a:]
        own_out, ex_out = rest[:n_out], rest[n_out:n_out + b]
        rest = rest[n_out + b:]
        own_scratch, ex_sems = rest[:n_scratch], rest[n_scratch:n_scratch + s]
        return list(own_in) + list(own_out) + list(own_scratch), (ex_in, ex_out, ex_sems)

    def at_steps(self, step, n_steps, ex_refs):
        if self.ex is None:
            return
        for p in range(self.ex.n_phases):
            pl.when(step == int(round(self.ex.at[p] * (n_steps - 1))))(functools.partial(self.ex.run, p, *ex_refs))

    def phase(self, p, ex_refs):
        if self.ex is not None:
            self.ex.run(p, *ex_refs)


def _run_exchanges(exchanges, name):
    hosts = [_Host(ex) for ex in exchanges]
    n_in = [len(h.args) for h in hosts]
    n_out = [len(h.out_shape) for h in hosts]
    n_sc = [len(h.scratch) for h in hosts]

    def body(*refs):
        ins, outs, scr = refs[:sum(n_in)], refs[sum(n_in):sum(n_in) + sum(n_out)], refs[sum(n_in) + sum(n_out):]
        parts = []
        for e in range(len(hosts)):
            parts.append((ins[sum(n_in[:e]):sum(n_in[:e + 1])], outs[sum(n_out[:e]):sum(n_out[:e + 1])],
                          scr[sum(n_sc[:e]):sum(n_sc[:e + 1])]))
        for h, part in zip(hosts, parts):
            h.phase(0, part)
        for h, part in zip(hosts, parts):
            for p in range(1, h.ex.n_phases):
                h.phase(p, part)

    res = pl.pallas_call(
        body, name=name, in_specs=[ANY] * sum(n_in), out_specs=[ANY] * sum(n_out),
        out_shape=[s for h in hosts for s in h.out_shape], scratch_shapes=[s for h in hosts for s in h.scratch],
    )(*[a for h in hosts for a in h.args])
    return [res[sum(n_out[:e]):sum(n_out[:e + 1])] for e in range(len(hosts))]


def _ffn_fwd(x, g_pre, wgu, wd, g_post, target, name, exchange=None):
    t = x.shape[0]
    tm = _token_tile(t)
    n_i = t // tm
    with_loss = target is not None
    host = _Host(exchange)
    n_in, n_out = (6, 4) if with_loss else (5, 3)

    def body(*refs):
        own, ex_refs = host.split(refs, n_in, n_out, 2)
        if with_loss:
            x_ref, gpre_ref, wgu_ref, wd_ref, gpost_ref, tgt_ref, xo_ref, f_ref, gu_ref, loss_ref, n_bf, acc = own
        else:
            x_ref, gpre_ref, wgu_ref, wd_ref, gpost_ref, xo_ref, f_ref, gu_ref, n_bf, acc = own
        j = pl.program_id(1)
        host.at_steps(pl.program_id(0) * N_CHUNK + j, n_i * N_CHUNK, ex_refs)

        @pl.when(j == 0)
        def _():
            n_bf[...] = _rms_fwd(x_ref[...], gpre_ref[...]).astype(BF16)
            acc[...] = jnp.zeros_like(acc)

        n = n_bf[...]
        gate = _dot(n, wgu_ref[0, 0])
        up = _dot(n, wgu_ref[1, 0])
        gu_ref[0, 0] = gate.astype(BF16)
        gu_ref[1, 0] = up.astype(BF16)
        a = (gate * _sigmoid(gate) * up).astype(BF16)
        acc[...] += _dot(a, wd_ref[0])

        @pl.when(j == N_CHUNK - 1)
        def _():
            f = acc[...]
            f_ref[...] = f
            xo = x_ref[...] + 0.5 * _rms_fwd(f, gpost_ref[...])
            if with_loss:
                err = xo - tgt_ref[...]
                xo_ref[...] = err * (1.0 / D_MODEL)
                part = 0.5 * jnp.sum(jnp.sum(err * err, axis=-1, keepdims=True) * (1.0 / D_MODEL), axis=0, keepdims=True)
                loss_ref[...] = jnp.broadcast_to(part, loss_ref.shape)
            else:
                xo_ref[...] = xo

    tok = pl.BlockSpec((tm, D_MODEL), lambda i, j: (i, 0))
    vec = pl.BlockSpec((1, D_MODEL), lambda i, j: (0, 0))
    in_specs = [tok, vec,
                pl.BlockSpec((2, 1, D_MODEL, CHUNK), lambda i, j: (0, j, 0, 0)),
                pl.BlockSpec((1, CHUNK, D_MODEL), lambda i, j: (j, 0, 0)),
                vec]
    out_shape = [jax.ShapeDtypeStruct((t, D_MODEL), F32), jax.ShapeDtypeStruct((t, D_MODEL), F32),
                 jax.ShapeDtypeStruct((2, N_CHUNK, t, CHUNK), BF16)]
    out_specs = [tok, tok, pl.BlockSpec((2, 1, tm, CHUNK), lambda i, j: (0, j, i, 0))]
    args = [x, g_pre, wgu, wd, g_post]
    if with_loss:
        in_specs.append(tok)
        args.append(target)
        out_shape.append(jax.ShapeDtypeStruct((n_i * 8, 128), F32))
        out_specs.append(pl.BlockSpec((8, 128), lambda i, j: (i, 0)))
    res = pl.pallas_call(
        body, name=name, grid=(n_i, N_CHUNK), in_specs=in_specs + host.in_specs, out_specs=out_specs + host.out_specs,
        out_shape=out_shape + host.out_shape,
        scratch_shapes=[pltpu.VMEM((tm, D_MODEL), BF16), pltpu.VMEM((tm, D_MODEL), F32)] + host.scratch,
        compiler_params=_params(2),
    )(*args, *host.args)
    return (*res[:n_out], list(res[n_out:]))


def _ffn_bwd_pre(d_out, x, f, g_pre, g_post, name, exchange=None):
    t = x.shape[0]
    tm = _token_tile(t)
    host = _Host(exchange)

    def body(*refs):
        (do_ref, x_ref, f_ref, gpre_ref, gpost_ref, n_ref, df_ref, dgpost_ref), ex_refs = host.split(refs, 5, 3, 0)
        i = pl.program_id(0)
        host.at_steps(i, t // tm, ex_refs)
        n_ref[...] = _rms_fwd(x_ref[...], gpre_ref[...]).astype(BF16)
        df, dg = _rms_bwd(f_ref[...], gpost_ref[...], 0.5 * do_ref[...])
        df_ref[...] = df.astype(BF16)
        _accumulate(dgpost_ref, dg, i == 0)

    tok = pl.BlockSpec((tm, D_MODEL), lambda i: (i, 0))
    vec = pl.BlockSpec((1, D_MODEL), lambda i: (0, 0))
    res = pl.pallas_call(
        body, name=name, grid=(t // tm,), in_specs=[tok, tok, tok, vec, vec] + host.in_specs,
        out_specs=[tok, tok, vec] + host.out_specs,
        out_shape=[jax.ShapeDtypeStruct((t, D_MODEL), BF16), jax.ShapeDtypeStruct((t, D_MODEL), BF16),
                   jax.ShapeDtypeStruct((1, D_MODEL), F32)] + host.out_shape,
        scratch_shapes=host.scratch, compiler_params=_params(1),
    )(d_out, x, f, g_pre, g_post, *host.args)
    return (*res[:3], list(res[3:]))


def _ffn_bwd_main(n, df, gu, wgu, wd, name, exchange=None):
    t = n.shape[0]
    tm = _ffn_bwd_tile(t)
    n_i = t // tm
    host = _Host(exchange)

    def body(*refs):
        (n_ref, df_ref, gu_ref, wgu_ref, wd_ref, dn_ref, dwgu_ref, dwd_ref), ex_refs = host.split(refs, 5, 3, 0)
        i = pl.program_id(1)
        host.at_steps(pl.program_id(0) * n_i + i, N_CHUNK * n_i, ex_refs)
        nb = n_ref[...]
        dfb = df_ref[...]
        gate = gu_ref[0, 0].astype(F32)
        up = gu_ref[1, 0].astype(F32)
        s = _sigmoid(gate)
        silu = gate * s
        a = (silu * up).astype(BF16)
        da = _dot_nt(dfb, wd_ref[0])
        dup = (da * silu).astype(BF16)
        dgate = (da * up * (s * (1.0 + gate * (1.0 - s)))).astype(BF16)
        dn_ref[0] = _dot_nt(dgate, wgu_ref[0, 0]) + _dot_nt(dup, wgu_ref[1, 0])
        first = i == 0
        _accumulate(dwgu_ref.at[0, 0], _dot_tn(nb, dgate), first)
        _accumulate(dwgu_ref.at[1, 0], _dot_tn(nb, dup), first)
        _accumulate(dwd_ref.at[0], _dot_tn(a, dfb), first)

    tok = pl.BlockSpec((tm, D_MODEL), lambda j, i: (i, 0))
    wgu_spec = pl.BlockSpec((2, 1, D_MODEL, CHUNK), lambda j, i: (0, j, 0, 0), pipeline_mode=pl.Buffered(1))
    wd_spec = pl.BlockSpec((1, CHUNK, D_MODEL), lambda j, i: (j, 0, 0), pipeline_mode=pl.Buffered(1))
    res = pl.pallas_call(
        body, name=name, grid=(N_CHUNK, n_i),
        in_specs=[tok, tok, pl.BlockSpec((2, 1, tm, CHUNK), lambda j, i: (0, j, i, 0)), wgu_spec, wd_spec] + host.in_specs,
        out_specs=[pl.BlockSpec((1, tm, D_MODEL), lambda j, i: (j, i, 0)), wgu_spec, wd_spec] + host.out_specs,
        out_shape=[jax.ShapeDtypeStruct((N_CHUNK, t, D_MODEL), F32),
                   jax.ShapeDtypeStruct((2, N_CHUNK, D_MODEL, CHUNK), F32),
                   jax.ShapeDtypeStruct((N_CHUNK, CHUNK, D_MODEL), F32)] + host.out_shape,
        scratch_shapes=host.scratch, compiler_params=_params(2),
    )(n, df, gu, wgu, wd, *host.args)
    return (*res[:3], list(res[3:]))


def _ffn_bwd_post(dn_parts, x, g_pre, d_out, name, exchange=None):
    t = x.shape[0]
    tm = _token_tile(t)
    host = _Host(exchange)

    def body(*refs):
        (dn_ref, x_ref, gpre_ref, do_ref, dx_ref, dgpre_ref), ex_refs = host.split(refs, 4, 2, 0)
        i = pl.program_id(0)
        host.at_steps(i, t // tm, ex_refs)
        dn = (dn_ref[0] + dn_ref[1]) + (dn_ref[2] + dn_ref[3])
        dx, dg = _rms_bwd(x_ref[...], gpre_ref[...], dn)
        dx_ref[...] = do_ref[...] + dx
        _accumulate(dgpre_ref, dg, i == 0)

    tok = pl.BlockSpec((tm, D_MODEL), lambda i: (i, 0))
    vec = pl.BlockSpec((1, D_MODEL), lambda i: (0, 0))
    res = pl.pallas_call(
        body, name=name, grid=(t // tm,),
        in_specs=[pl.BlockSpec((N_CHUNK, tm, D_MODEL), lambda i: (0, i, 0)), tok, vec, tok] + host.in_specs,
        out_specs=[tok, vec] + host.out_specs,
        out_shape=[jax.ShapeDtypeStruct((t, D_MODEL), F32), jax.ShapeDtypeStruct((1, D_MODEL), F32)] + host.out_shape,
        scratch_shapes=host.scratch, compiler_params=_params(1),
    )(dn_parts, x, g_pre, d_out, *host.args)
    return (*res[:2], list(res[2:]))


def _mix_in_fwd(x1, g, w_in):
    t = x1.shape[0]
    tm = _token_tile(t)

    def body(x_ref, g_ref, w_ref, xl_ref, gl_ref, q_ref, kv_ref):
        n = _rms_fwd(x_ref[...], g_ref[...]).astype(BF16)
        proj = _dot(n, w_ref[...])
        xl_ref[...] = proj[:, 0:512]
        gl_ref[...] = proj[:, 512:1024]
        q_ref[...] = proj[:, 1024:1536].astype(BF16)
        kv_ref[...] = proj[:, 1536:2048].astype(BF16)

    tok = pl.BlockSpec((tm, D_MODEL), lambda i: (i, 0))
    half = pl.BlockSpec((tm, 512), lambda i: (i, 0))
    return pl.pallas_call(
        body, name="mix_in_fwd", grid=(t // tm,),
        in_specs=[tok, pl.BlockSpec((1, D_MODEL), lambda i: (0, 0)), pl.BlockSpec((D_MODEL, D_IN_DUP), lambda i: (0, 0))],
        out_specs=[half, half, half, half],
        out_shape=[jax.ShapeDtypeStruct((t, 512), F32), jax.ShapeDtypeStruct((t, 512), F32),
                   jax.ShapeDtypeStruct((t, 512), BF16), jax.ShapeDtypeStruct((t, 512), BF16)],
        compiler_params=_params(1),
    )(x1, g, w_in)


def _shift_down(x, before, s):
    if s == 0:
        return x
    rolled = pltpu.roll(x, s, 0)
    ext = jnp.concatenate([before, x[0:8]], axis=0)
    first8 = pltpu.roll(ext, s, 0)[8:16]
    return jnp.concatenate([first8, rolled[8:]], axis=0)


def _shift_up(x, after, s):
    if s == 0:
        return x
    rows = x.shape[0]
    rolled = pltpu.roll(x, rows - s, 0)
    ext = jnp.concatenate([x[rows - 8:rows], after], axis=0)
    last8 = pltpu.roll(ext, 16 - s, 0)[0:8]
    return jnp.concatenate([rolled[:rows - 8], last8], axis=0)


def _log_sigmoid(x):
    e = jnp.exp(-jnp.abs(x))
    log1p_e = jnp.where(e < 0.01, e * (1.0 - e * (0.5 - e * (1.0 / 3.0))), jnp.log(1.0 + e))
    return jnp.minimum(x, 0.0) - log1p_e


def _lru_gates(xc, p_ref, wrg, wig):
    xcb = xc.astype(BF16)
    r = _sigmoid(_dot(xcb, wrg) + p_ref[1:2, :])
    ig = _sigmoid(_dot(xcb, wig) + p_ref[2:3, :])
    ls = _log_sigmoid(p_ref[3:4, :])
    log_a = LRU_C * r * ls
    a = jnp.exp(log_a)
    z = 2.0 * log_a
    series = z * (1.0 + z * (0.5 + z * (1.0 / 6.0 + z * (1.0 / 24.0 + z * (1.0 / 120.0 + z * (1.0 / 720.0))))))
    expm1 = jnp.where(z > -0.1, series, jnp.exp(z) - 1.0)
    mult = jnp.sqrt(-expm1)
    return xcb, r, ig, ls, a, mult


def _conv_taps(x, before, p_ref):
    xc = x * p_ref[7:8, :]
    for s in (1, 2, 3):
        xc = xc + _shift_down(x, before, s) * p_ref[7 - s:8 - s, :]
    return xc + p_ref[0:1, :]


def _lru_block_rows(t):
    return 512 if t >= 1024 else t // 2


def _lru_fwd(xl, p, wrg2, wig2):
    t = xl.shape[0]
    tb = _lru_block_rows(t)

    def body(xl_ref, p_ref, wrg_ref, wig_ref, h_ref, x_tail, h_carry):
        tt = pl.program_id(1)

        @pl.when(tt == 0)
        def _():
            x_tail[...] = jnp.zeros_like(x_tail)
            h_carry[...] = jnp.zeros_like(h_carry)

        x = xl_ref[...]
        xc = _conv_taps(x, x_tail[...], p_ref)
        x_tail[...] = x[tb - 8:tb]
        _, r, ig, ls, a, mult = _lru_gates(xc, p_ref, wrg_ref[0], wig_ref[0])
        u = mult * ig * xc
        row = lax.broadcasted_iota(jnp.int32, (tb, LRU_GROUP), 0)
        s = 1
        while s < tb:
            keep = row >= s
            u = jnp.where(keep, a * pltpu.roll(u, s, 0) + u, u)
            a = jnp.where(keep, a * pltpu.roll(a, s, 0), a)
            s *= 2
        h = u + a * h_carry[0:1, :]
        h_ref[...] = h
        h_carry[...] = jnp.broadcast_to(h[tb - 1:tb], h_carry.shape)

    blk = pl.BlockSpec((tb, LRU_GROUP), lambda g, tt: (tt, g))
    par = pl.BlockSpec((8, LRU_GROUP), lambda g, tt: (0, g))
    wsp = pl.BlockSpec((1, LRU_GROUP, LRU_GROUP), lambda g, tt: (g, 0, 0))
    return pl.pallas_call(
        body, name="lru_fwd", grid=(N_LRU_GROUP, t // tb), in_specs=[blk, par, wsp, wsp], out_specs=blk,
        out_shape=jax.ShapeDtypeStruct((t, D_LRU), F32),
        scratch_shapes=[pltpu.VMEM((8, LRU_GROUP), F32), pltpu.VMEM((8, LRU_GROUP), F32)],
        compiler_params=_params(2),
    )(xl, p, wrg2, wig2)


def _lru_bwd(dy, h, xl, gl, p, wrg2, wig2):
    t = xl.shape[0]
    tb = _lru_block_rows(t)
    n_tb = t // tb
    tb8 = tb // 8

    def body(dy_ref, h_ref, hprev_ref, xl_ref, xprev_ref, gl_ref, p_ref, wrg_ref, wig_ref,
             dxl_ref, dgl_ref, dp_ref, dwrg_ref, dwig_ref, g_carry, a_carry, dxc_head):
        step = pl.program_id(1)
        tt = n_tb - 1 - step
        first = step == 0

        @pl.when(first)
        def _():
            g_carry[...] = jnp.zeros_like(g_carry)
            a_carry[...] = jnp.zeros_like(a_carry)
            dxc_head[...] = jnp.zeros_like(dxc_head)

        has_prev = (tt > 0).astype(F32)
        x = xl_ref[...]
        x_before = xprev_ref[...] * has_prev
        xs = [_shift_down(x, x_before, s) for s in range(4)]
        xc = xs[0] * p_ref[7:8, :] + xs[1] * p_ref[6:7, :] + xs[2] * p_ref[5:6, :] + xs[3] * p_ref[4:5, :] + p_ref[0:1, :]
        wrg = wrg_ref[0]
        wig = wig_ref[0]
        xcb, r, ig, ls, a, mult = _lru_gates(xc, p_ref, wrg, wig)

        hh = h_ref[...]
        h_m1 = _shift_down(hh, hprev_ref[...] * has_prev, 1)
        ge, dge = _gelu(gl_ref[...])
        dy = dy_ref[...]
        dgl_ref[...] = dy * hh * dge
        dh = dy * ge

        b = _shift_up(a, a_carry[...], 1)
        row = lax.broadcasted_iota(jnp.int32, (tb, LRU_GROUP), 0)
        g = dh
        s = 1
        while s < tb:
            keep = row < tb - s
            g = jnp.where(keep, b * pltpu.roll(g, tb - s, 0) + g, g)
            b = jnp.where(keep, b * pltpu.roll(b, tb - s, 0), b)
            s *= 2
        g = g + b * g_carry[0:1, :]
        g_carry[...] = jnp.broadcast_to(g[0:1], g_carry.shape)
        a_carry[...] = jnp.broadcast_to(a[0:1], a_carry.shape)

        da = g * h_m1
        dmult = g * ig * xc
        dig = g * mult * xc
        dxc = g * mult * ig
        dlog_a = da * a - dmult * (a * a) / mult
        dr = dlog_a * (LRU_C * ls)
        dls = jnp.sum(dlog_a * (LRU_C * r), axis=0, keepdims=True)
        dlam = dls * _sigmoid(-p_ref[3:4, :])
        dpre_r = dr * r * (1.0 - r)
        dpre_i = dig * ig * (1.0 - ig)
        dprb = dpre_r.astype(BF16)
        dpib = dpre_i.astype(BF16)
        dxc = dxc + _dot_nt(dprb, wrg) + _dot_nt(dpib, wig)
        _accumulate(dwrg_ref.at[0], _dot_tn(xcb, dprb), first)
        _accumulate(dwig_ref.at[0], _dot_tn(xcb, dpib), first)

        after = dxc_head[...]
        dxl = dxc * p_ref[7:8, :]
        for s in (1, 2, 3):
            dxl = dxl + _shift_up(dxc, after, s) * p_ref[7 - s:8 - s, :]
        dxl_ref[...] = dxl
        dxc_head[...] = dxc[0:8]

        rows = [jnp.sum(dxc, axis=0, keepdims=True), jnp.sum(dpre_r, axis=0, keepdims=True),
                jnp.sum(dpre_i, axis=0, keepdims=True), dlam]
        rows += [jnp.sum(dxc * xs[3 - k], axis=0, keepdims=True) for k in range(4)]
        _accumulate(dp_ref, jnp.concatenate(rows, axis=0), first)

    blk = pl.BlockSpec((tb, LRU_GROUP), lambda g, s: (n_tb - 1 - s, g))
    prev8 = pl.BlockSpec((8, LRU_GROUP), lambda g, s: (jnp.maximum((n_tb - 1 - s) * tb8 - 1, 0), g))
    par = pl.BlockSpec((8, LRU_GROUP), lambda g, s: (0, g))
    wsp = pl.BlockSpec((1, LRU_GROUP, LRU_GROUP), lambda g, s: (g, 0, 0))
    return pl.pallas_call(
        body, name="lru_bwd", grid=(N_LRU_GROUP, n_tb),
        in_specs=[blk, blk, prev8, blk, prev8, blk, par, wsp, wsp], out_specs=[blk, blk, par, wsp, wsp],
        out_shape=[jax.ShapeDtypeStruct((t, D_LRU), F32), jax.ShapeDtypeStruct((t, D_LRU), F32),
                   jax.ShapeDtypeStruct((8, D_LRU), F32),
                   jax.ShapeDtypeStruct((N_LRU_GROUP, LRU_GROUP, LRU_GROUP), F32),
                   jax.ShapeDtypeStruct((N_LRU_GROUP, LRU_GROUP, LRU_GROUP), F32)],
        scratch_shapes=[pltpu.VMEM((8, LRU_GROUP), F32)] * 3,
        compiler_params=_params(2),
    )(dy, h, h, xl, xl, gl, p, wrg2, wig2)


def _attn_scores(qv, kvv, n, sinks, lo):
    r0 = pl.multiple_of(n * BLOCK_Q, BLOCK_Q)
    rp = pl.multiple_of(jnp.maximum(n - 1, 0) * BLOCK_Q, BLOCK_Q)
    kvb = jnp.concatenate([kvv[pl.ds(rp, BLOCK_Q), :], kvv[pl.ds(r0, BLOCK_Q), :]], axis=0)
    k2 = kvb[:, 0:128]
    v2 = kvb[:, 128:256]
    qs = _stack_heads(qv[pl.ds(r0, BLOCK_Q), :], lo)
    s = _dot_nt(qs, k2) * ATTN_SCALE
    qi = jnp.bitwise_and(lax.broadcasted_iota(jnp.int32, s.shape, 0), BLOCK_Q - 1)
    kj = lax.broadcasted_iota(jnp.int32, s.shape, 1)
    rel = qi + BLOCK_Q - kj
    mask = (rel >= 0) & (rel < BLOCK_Q) & ((n > 0) | (kj >= BLOCK_Q))
    s = jnp.where(mask, s, MASK_VALUE)
    hrow = lax.broadcasted_iota(jnp.int32, (4 * BLOCK_Q, 1), 0)
    sk = jnp.where(hrow < BLOCK_Q, sinks[0],
                   jnp.where(hrow < 2 * BLOCK_Q, sinks[1], jnp.where(hrow < 3 * BLOCK_Q, sinks[2], sinks[3])))
    m = jnp.maximum(jnp.max(s, axis=-1, keepdims=True), sk)
    e = jnp.exp(s - m)
    es = jnp.exp(sk - m)
    inv = 1.0 / (jnp.sum(e, axis=-1, keepdims=True) + es)
    return r0, rp, qs, k2, v2, e * inv, es * inv


def _stack_heads(pair2, lo):
    p0 = pair2[:, 0:128]
    p1 = pair2[:, 128:256]
    z = jnp.zeros_like(p0)
    return jnp.concatenate([jnp.where(lo, p0, z), jnp.where(lo, z, p0), jnp.where(lo, p1, z), jnp.where(lo, z, p1)], axis=0)


def _unstack_heads(st, lo):
    b = BLOCK_Q
    return jnp.concatenate([jnp.where(lo, st[0:b], st[b:2 * b]), jnp.where(lo, st[2 * b:3 * b], st[3 * b:4 * b])], axis=1)


def _attn_fwd(q, kv, sinks):
    t = q.shape[0]
    n_blk = t // BLOCK_Q

    def body(q_hbm, kv_hbm, s_ref, o_hbm, qv, kvv, ov, sem):
        lo = lax.broadcasted_iota(jnp.int32, (BLOCK_Q, 128), 1) < HEAD_DIM
        for g in range(2):
            cols = pl.ds(256 * g, 256)
            loads = [pltpu.make_async_copy(q_hbm.at[:, cols], qv, sem.at[0]),
                     pltpu.make_async_copy(kv_hbm.at[:, cols], kvv, sem.at[1])]
            for cp in loads:
                cp.start()
            for cp in loads:
                cp.wait()
            sinks_g = [s_ref[0, 4 * g + i] for i in range(4)]

            def block(n, carry):
                r0, _, _, _, v2, prob, _ = _attn_scores(qv, kvv, n, sinks_g, lo)
                ov[pl.ds(r0, BLOCK_Q), :] = _unstack_heads(_dot(prob.astype(BF16), v2), lo)
                return carry

            lax.fori_loop(0, n_blk, block, 0)
            store = pltpu.make_async_copy(ov, o_hbm.at[:, cols], sem.at[2])
            store.start()
            store.wait()

    return pl.pallas_call(
        body, name="attn_fwd", in_specs=[ANY, ANY, SMEM], out_specs=ANY,
        out_shape=jax.ShapeDtypeStruct((t, D_ATTN), F32),
        scratch_shapes=[pltpu.VMEM((t, 256), BF16), pltpu.VMEM((t, 256), BF16), pltpu.VMEM((t, 256), F32),
                        pltpu.SemaphoreType.DMA((3,))],
        compiler_params=_params(),
    )(q, kv, sinks)


def _attn_bwd(q, kv, do, sinks, exchange=None):
    t = q.shape[0]
    n_blk = t // BLOCK_Q
    host = _Host(exchange)

    def body(*refs):
        own, ex_refs = host.split(refs, 4, 3, 7)
        q_hbm, kv_hbm, do_hbm, s_ref, dq_hbm, dkv_hbm, dsink_ref, qv, kvv, dov, dqv, dkvv, ds_acc, sem = own
        host.phase(0, ex_refs)
        lo = lax.broadcasted_iota(jnp.int32, (BLOCK_Q, 128), 1) < HEAD_DIM
        for g in range(2):
            cols = pl.ds(256 * g, 256)
            loads = [pltpu.make_async_copy(q_hbm.at[:, cols], qv, sem.at[0]),
                     pltpu.make_async_copy(kv_hbm.at[:, cols], kvv, sem.at[1]),
                     pltpu.make_async_copy(do_hbm.at[:, cols], dov, sem.at[2])]
            for cp in loads:
                cp.start()
            for cp in loads:
                cp.wait()
            sinks_g = [s_ref[0, 4 * g + i] for i in range(4)]
            ds_acc[...] = jnp.zeros_like(ds_acc)

            def block(n, carry):
                r0, rp, qs, k2, v2, prob, psink = _attn_scores(qv, kvv, n, sinks_g, lo)
                pb = prob.astype(BF16)
                dos = _stack_heads(dov[pl.ds(r0, BLOCK_Q), :], lo)
                dp = _dot_nt(dos, v2)
                dsum = jnp.sum(prob * dp, axis=-1, keepdims=True)
                dsb = (prob * (dp - dsum) * ATTN_SCALE).astype(BF16)
                ds_acc[...] -= psink * dsum
                dqv[pl.ds(r0, BLOCK_Q), :] = _unstack_heads(_dot(dsb, k2), lo).astype(BF16)
                dk2 = _dot_tn(dsb, qs)
                dv2 = _dot_tn(pb, dos)
                dkvv[pl.ds(r0, BLOCK_Q), :] = jnp.concatenate([dk2[BLOCK_Q:], dv2[BLOCK_Q:]], axis=1)

                @pl.when(n > 0)
                def _():
                    dkvv[pl.ds(rp, BLOCK_Q), :] += jnp.concatenate([dk2[:BLOCK_Q], dv2[:BLOCK_Q]], axis=1)

                return carry

            lax.fori_loop(0, n_blk, block, 0)
            for i in range(4):
                tot = jnp.sum(ds_acc[BLOCK_Q * i:BLOCK_Q * (i + 1), :], axis=0, keepdims=True)
                dsink_ref[4 * g + i:4 * g + i + 1, :] = jnp.broadcast_to(tot, (1, 128))
            stores = [pltpu.make_async_copy(dqv, dq_hbm.at[:, cols], sem.at[0]),
                      pltpu.make_async_copy(dkvv, dkv_hbm.at[:, cols], sem.at[1])]
            for cp in stores:
                cp.start()
            for cp in stores:
                cp.wait()
        if exchange is not None:
            for p in range(1, exchange.n_phases):
                host.phase(p, ex_refs)

    res = pl.pallas_call(
        body, name="attn_bwd", in_specs=[ANY, ANY, ANY, SMEM] + host.in_specs,
        out_specs=[ANY, ANY, pl.BlockSpec(memory_space=pltpu.VMEM)] + host.out_specs,
        out_shape=[jax.ShapeDtypeStruct((t, D_ATTN), BF16), jax.ShapeDtypeStruct((t, 512), F32),
                   jax.ShapeDtypeStruct((8, 128), F32)] + host.out_shape,
        scratch_shapes=[pltpu.VMEM((t, 256), BF16), pltpu.VMEM((t, 256), BF16), pltpu.VMEM((t, 256), BF16),
                        pltpu.VMEM((t, 256), BF16), pltpu.VMEM((t, 256), F32), pltpu.VMEM((4 * BLOCK_Q, 1), F32),
                        pltpu.SemaphoreType.DMA((3,))] + host.scratch,
        compiler_params=_params(),
    )(q, kv, do, sinks, *host.args)
    return (*res[:3], list(res[3:]))


def _mix_out_fwd(x1, h, gl, o, g_lru, g_attn, g_post, w_o):
    t = x1.shape[0]
    tm = _token_tile(t)

    def body(x_ref, h_ref, gl_ref, o_ref, g1_ref, g2_ref, gp_ref, w_ref, x2_ref, m_ref):
        y = h_ref[...] * _gelu(gl_ref[...])[0]
        yn1 = _rms_fwd(y, g1_ref[...]).astype(BF16)
        yn2 = _rms_fwd(o_ref[...], g2_ref[...]).astype(BF16)
        m = _dot(yn1, w_ref[0:512, :]) + _dot(yn2, w_ref[512:1024, :])
        m_ref[...] = m
        x2_ref[...] = x_ref[...] + _rms_fwd(m, gp_ref[...])

    tok = pl.BlockSpec((tm, D_MODEL), lambda i: (i, 0))
    half = pl.BlockSpec((tm, 512), lambda i: (i, 0))
    vec = pl.BlockSpec((1, D_MODEL), lambda i: (0, 0))
    hvec = pl.BlockSpec((1, 512), lambda i: (0, 0))
    return pl.pallas_call(
        body, name="mix_out_fwd", grid=(t // tm,),
        in_specs=[tok, half, half, half, hvec, hvec, vec, pl.BlockSpec((D_MODEL, D_MODEL), lambda i: (0, 0))],
        out_specs=[tok, tok],
        out_shape=[jax.ShapeDtypeStruct((t, D_MODEL), F32), jax.ShapeDtypeStruct((t, D_MODEL), F32)],
        compiler_params=_params(1),
    )(x1, h, gl, o, g_lru, g_attn, g_post, w_o)


def _mix_out_bwd(dx2, m, h, gl, o, g_lru, g_attn, g_post, w_o):
    t = dx2.shape[0]
    tm = _token_tile(t)

    def body(dx_ref, m_ref, h_ref, gl_ref, o_ref, g1_ref, g2_ref, gp_ref, w_ref,
             dy_ref, do_ref, dw_ref, dgp_ref, dg1_ref, dg2_ref):
        first = pl.program_id(0) == 0
        dm, dgp = _rms_bwd(m_ref[...], gp_ref[...], dx_ref[...])
        dmb = dm.astype(BF16)
        y = h_ref[...] * _gelu(gl_ref[...])[0]
        o = o_ref[...]
        yn1 = _rms_fwd(y, g1_ref[...]).astype(BF16)
        yn2 = _rms_fwd(o, g2_ref[...]).astype(BF16)
        _accumulate(dw_ref.at[0:512, :], _dot_tn(yn1, dmb), first)
        _accumulate(dw_ref.at[512:1024, :], _dot_tn(yn2, dmb), first)
        dy, dg1 = _rms_bwd(y, g1_ref[...], _dot_nt(dmb, w_ref[0:512, :]))
        do, dg2 = _rms_bwd(o, g2_ref[...], _dot_nt(dmb, w_ref[512:1024, :]))
        dy_ref[...] = dy
        do_ref[...] = do.astype(BF16)
        _accumulate(dgp_ref, dgp, first)
        _accumulate(dg1_ref, dg1, first)
        _accumulate(dg2_ref, dg2, first)

    tok = pl.BlockSpec((tm, D_MODEL), lambda i: (i, 0))
    half = pl.BlockSpec((tm, 512), lambda i: (i, 0))
    vec = pl.BlockSpec((1, D_MODEL), lambda i: (0, 0))
    hvec = pl.BlockSpec((1, 512), lambda i: (0, 0))
    mat = pl.BlockSpec((D_MODEL, D_MODEL), lambda i: (0, 0))
    return pl.pallas_call(
        body, name="mix_out_bwd", grid=(t // tm,),
        in_specs=[tok, tok, half, half, half, hvec, hvec, vec, mat],
        out_specs=[half, half, mat, vec, hvec, hvec],
        out_shape=[jax.ShapeDtypeStruct((t, 512), F32), jax.ShapeDtypeStruct((t, 512), BF16),
                   jax.ShapeDtypeStruct((D_MODEL, D_MODEL), F32), jax.ShapeDtypeStruct((1, D_MODEL), F32),
                   jax.ShapeDtypeStruct((1, 512), F32), jax.ShapeDtypeStruct((1, 512), F32)],
        compiler_params=_params(1),
    )(dx2, m, h, gl, o, g_lru, g_attn, g_post, w_o)


def _mix_in_bwd(dx2, x1, g, dxl, dgl, dq, dkv, w_in):
    t = x1.shape[0]
    tm = _token_tile(t)

    def body(dx2_ref, x_ref, g_ref, dxl_ref, dgl_ref, dq_ref, dkv_ref, w_ref, dx1_ref, dw_ref, dg_ref):
        first = pl.program_id(0) == 0
        x = x_ref[...]
        nb = _rms_fwd(x, g_ref[...]).astype(BF16)
        lo = lax.broadcasted_iota(jnp.int32, (tm, 128), 1) < HEAD_DIM
        dkv = dkv_ref[...]
        folded = []
        for k in range(4):
            seg = dkv[:, 128 * k:128 * (k + 1)]
            folded.append(jnp.where(lo, seg + pltpu.roll(seg, HEAD_DIM, 1), 0.0).astype(BF16))
        dproj = jnp.concatenate([dxl_ref[...].astype(BF16), dgl_ref[...].astype(BF16), dq_ref[...]] + folded, axis=1)
        _accumulate(dw_ref, _dot_tn(nb, dproj), first)
        dx, dg = _rms_bwd(x, g_ref[...], _dot_nt(dproj, w_ref[...]))
        dx1_ref[...] = dx2_ref[...] + dx
        _accumulate(dg_ref, dg, first)

    tok = pl.BlockSpec((tm, D_MODEL), lambda i: (i, 0))
    half = pl.BlockSpec((tm, 512), lambda i: (i, 0))
    vec = pl.BlockSpec((1, D_MODEL), lambda i: (0, 0))
    mat = pl.BlockSpec((D_MODEL, D_IN_DUP), lambda i: (0, 0))
    return pl.pallas_call(
        body, name="mix_in_bwd", grid=(t // tm,),
        in_specs=[tok, tok, vec, half, half, half, half, mat], out_specs=[tok, mat, vec],
        out_shape=[jax.ShapeDtypeStruct((t, D_MODEL), F32), jax.ShapeDtypeStruct((D_MODEL, D_IN_DUP), F32),
                   jax.ShapeDtypeStruct((1, D_MODEL), F32)],
        compiler_params=_params(1),
    )(dx2, x1, g, dxl, dgl, dq, dkv, w_in)


def _row_tile(rows):
    return rows if rows <= 512 else 256


def _chip_sum(grad, from_sibling, core, name):
    _, rows, cols = grad.shape
    tr = _row_tile(rows)

    def body(core_ref, g_ref, s_ref, out_ref):
        out_ref[0] = (g_ref[0, 0] + s_ref[0]).astype(BF16)

    grid_spec = pltpu.PrefetchScalarGridSpec(
        num_scalar_prefetch=1, grid=(4, rows // tr),
        in_specs=[pl.BlockSpec((1, 1, tr, cols), lambda q, i, core: (q, core[0], i, 0)),
                  pl.BlockSpec((1, tr, cols), lambda q, i, core: (q, i, 0))],
        out_specs=pl.BlockSpec((1, tr, cols), lambda q, i, core: (q, i, 0)))
    return pl.pallas_call(
        body, name=name, grid_spec=grid_spec, out_shape=jax.ShapeDtypeStruct((4, rows, cols), BF16),
        compiler_params=_params(2),
    )(core, grad.reshape(4, 2, rows, cols), from_sibling)


def _adamw(w, g, m, v):
    m = ADAM_B1 * m + (1.0 - ADAM_B1) * g
    v = ADAM_B2 * v + (1.0 - ADAM_B2) * (g * g)
    m_hat = m / (1.0 - ADAM_B1 ** ADAM_STEP)
    v_hat = v / (1.0 - ADAM_B2 ** ADAM_STEP)
    delta = -ADAM_LR * (m_hat / (jnp.sqrt(v_hat) + ADAM_EPS) + ADAM_WD * w)
    return delta, m, v


def _shard_update(grad, from_sibling, from_chips, w, m, v, place, name):
    _, rows, cols = grad.shape
    tr = _row_tile(rows)

    def body(place_ref, g_ref, s_ref, c_ref, w_ref, m_ref, v_ref, go_ref, d_ref, mo_ref, vo_ref):
        g = g_ref[0, 0] + s_ref[0]
        g = g + c_ref[0].astype(F32)
        g = g + c_ref[1].astype(F32)
        g = g + c_ref[2].astype(F32)
        go_ref[...] = g
        d_ref[...], mo_ref[...], vo_ref[...] = _adamw(w_ref[...], g, m_ref[...], v_ref[...])

    flat = pl.BlockSpec((tr, cols), lambda i, place: (i, 0))
    grid_spec = pltpu.PrefetchScalarGridSpec(
        num_scalar_prefetch=1, grid=(rows // tr,),
        in_specs=[pl.BlockSpec((1, 1, tr, cols), lambda i, place: (place[0], place[1], i, 0)),
                  pl.BlockSpec((1, tr, cols), lambda i, place: (place[0], i, 0)),
                  pl.BlockSpec((3, tr, cols), lambda i, place: (0, i, 0)), flat, flat, flat],
        out_specs=[flat, flat, flat, flat])
    return pl.pallas_call(
        body, name=name, grid_spec=grid_spec, out_shape=[jax.ShapeDtypeStruct((rows, cols), F32)] * 4,
        compiler_params=_params(1),
    )(place, grad.reshape(4, 2, rows, cols), from_sibling, from_chips, w, m, v)


def _small_update(gathered, w, m, v):
    rows = w.shape[0]

    def body(g_ref, w_ref, m_ref, v_ref, go_ref, d_ref, mo_ref, vo_ref):
        g = g_ref[0]
        for d in range(1, N_DEV):
            g = g + g_ref[d]
        go_ref[...] = g
        d_ref[...], mo_ref[...], vo_ref[...] = _adamw(w_ref[...], g, m_ref[...], v_ref[...])

    return pl.pallas_call(
        body, name="small_update", out_shape=[jax.ShapeDtypeStruct((rows, 128), F32)] * 4,
        compiler_params=_params(),
    )(gathered, w, m, v)


SMALL_VECTORS = ("ffn1_pre_g", "ffn1_post_g", "mix_pre_g", "mix_post_g", "ffn2_pre_g", "ffn2_post_g",
                 "conv_b", "b_rg", "b_ig", "lru_lambda", "g_lru_out", "g_attn_out")


def _pad_rows(a, rows):
    return jnp.concatenate([a, jnp.zeros((rows - a.shape[0], a.shape[1]), a.dtype)], axis=0)


def _pack_small(vals, loss, conv_w_full):
    parts = [_pad_rows(jnp.pad(jnp.reshape(loss, (1, 1)), ((0, 0), (0, 127))), 8)]
    parts += [vals[k].reshape(-1, 128) for k in SMALL_VECTORS]
    parts += [vals["w_rg"].reshape(-1, 128), vals["w_ig"].reshape(-1, 128)]
    parts.append(_pad_rows(jnp.pad(vals["sinks"].reshape(1, 8), ((0, 0), (0, 120))), 8))
    parts.append(conv_w_full.reshape(16, 128))
    return jnp.concatenate(parts, axis=0)


def _unpack_small(blob, like):
    out = {}
    r = 8
    for k in SMALL_VECTORS:
        n = like[k].size // 128
        out[k] = blob[r:r + n].reshape(like[k].shape)
        r += n
    for k in ("w_rg", "w_ig"):
        out[k] = blob[r:r + 256].reshape(like[k].shape)
        r += 256
    out["sinks"] = blob[r:r + 1, 0:8].reshape(like["sinks"].shape)
    r += 8
    conv_w_full = blob[r:r + 16].reshape(4, D_LRU)
    return out, blob[0, 0], conv_w_full


def _dup_in_columns(w):
    k0, k1, v0, v1 = w[:, 1536:1600], w[:, 1600:1664], w[:, 1664:1728], w[:, 1728:1792]
    return jnp.concatenate([w[:, :1536], k0, k0, v0, v0, k1, k1, v1, v1], axis=1)


def _undup_in_columns(dw):
    return jnp.concatenate([dw[:, :1536], dw[:, 1536:1600], dw[:, 1792:1856], dw[:, 1664:1728], dw[:, 1920:1984]], axis=1)


def _pair_block_diag(w):
    w = w.reshape(N_LRU_GROUP, 2, 64, 64)
    z = jnp.zeros((N_LRU_GROUP, 64, 64), w.dtype)
    top = jnp.concatenate([w[:, 0], z], axis=2)
    bot = jnp.concatenate([z, w[:, 1]], axis=2)
    return jnp.concatenate([top, bot], axis=1)


def _pair_block_diag_grad(dw2, shape):
    return jnp.stack([dw2[:, :64, :64], dw2[:, 64:, 64:]], axis=1).reshape(shape)


def kernel(x, ffn1_pre_g, ffn1_w_gu, ffn1_w_down, ffn1_post_g, mix_pre_g, w_in, conv_w, conv_b, w_rg, b_rg, w_ig, b_ig, lru_lambda, sinks, g_lru_out, g_attn_out, w_o, mix_post_g, ffn2_pre_g, ffn2_w_gu, ffn2_w_down, ffn2_post_g, loss_target, m_ffn1_pre_g, m_ffn1_w_gu, m_ffn1_w_down, m_ffn1_post_g, m_mix_pre_g, m_w_in, m_conv_w, m_conv_b, m_w_rg, m_b_rg, m_w_ig, m_b_ig, m_lru_lambda, m_sinks, m_g_lru_out, m_g_attn_out, m_w_o, m_mix_post_g, m_ffn2_pre_g, m_ffn2_w_gu, m_ffn2_w_down, m_ffn2_post_g, v_ffn1_pre_g, v_ffn1_w_gu, v_ffn1_w_down, v_ffn1_post_g, v_mix_pre_g, v_w_in, v_conv_w, v_conv_b, v_w_rg, v_b_rg, v_w_ig, v_b_ig, v_lru_lambda, v_sinks, v_g_lru_out, v_g_attn_out, v_w_o, v_mix_post_g, v_ffn2_pre_g, v_ffn2_w_gu, v_ffn2_w_down, v_ffn2_post_g):
    args = dict(locals())
    names = ["ffn1_pre_g", "ffn1_w_gu", "ffn1_w_down", "ffn1_post_g", "mix_pre_g", "w_in", "conv_w", "conv_b", "w_rg",
             "b_rg", "w_ig", "b_ig", "lru_lambda", "sinks", "g_lru_out", "g_attn_out", "w_o", "mix_post_g",
             "ffn2_pre_g", "ffn2_w_gu", "ffn2_w_down", "ffn2_post_g"]
    big = ["ffn1_w_gu", "ffn1_w_down", "w_in", "w_o", "ffn2_w_gu", "ffn2_w_down"]
    w = {k: args[k] for k in names}
    mom = {k: args["m_" + k] for k in names}
    var = {k: args["v_" + k] for k in names}
    t = x.shape[1]
    xs = x.reshape(t, D_MODEL)
    target = loss_target.reshape(t, D_MODEL)
    cx, cy, cc = _coords()
    me = 4 * cx + 2 * cy + cc
    core = jnp.reshape(cc, (1,)).astype(jnp.int32)
    place = jnp.stack([2 * cx + cy, cc]).astype(jnp.int32)

    shard2d = {k: w[k].reshape(w[k].shape[1:]) for k in big}
    shard_bf = {k: shard2d[k].astype(BF16) for k in big}
    conv_pad = jnp.pad(conv_w.reshape(4, 64), ((0, 4), (0, 64)))
    (first_w,) = _run_exchanges([_Gather([shard_bf["ffn1_w_gu"], shard_bf["ffn1_w_down"]])], "all_gather_ffn1")
    wgu1 = first_w[0].reshape(2, N_CHUNK, D_MODEL, CHUNK)
    wd1 = first_w[1].reshape(N_CHUNK, CHUNK, D_MODEL)
    rest = _Gather([shard_bf["w_in"], shard_bf["w_o"], shard_bf["ffn2_w_gu"], shard_bf["ffn2_w_down"], conv_pad])

    x1, f1, gu1, gathered = _ffn_fwd(xs, ffn1_pre_g, wgu1, wd1, ffn1_post_g, None, "ffn1_fwd", rest)
    w_in_full = _dup_in_columns(jnp.transpose(gathered[0], (1, 0, 2)).reshape(D_MODEL, D_IN))
    w_o_full = gathered[1].reshape(D_MODEL, D_MODEL)
    wgu2 = gathered[2].reshape(2, N_CHUNK, D_MODEL, CHUNK)
    wd2 = gathered[3].reshape(N_CHUNK, CHUNK, D_MODEL)
    conv_w_full = jnp.transpose(gathered[4][:, 0:4, 0:64], (1, 0, 2)).reshape(4, D_LRU)
    p_lru = jnp.concatenate([conv_b, b_rg, b_ig, lru_lambda, conv_w_full], axis=0)
    wrg2 = _pair_block_diag(w_rg[0]).astype(BF16)
    wig2 = _pair_block_diag(w_ig[0]).astype(BF16)
    xl, gl, q, kv = _mix_in_fwd(x1, mix_pre_g, w_in_full)
    h = _lru_fwd(xl, p_lru, wrg2, wig2)
    o = _attn_fwd(q, kv, sinks)
    x2, mo = _mix_out_fwd(x1, h, gl, o, g_lru_out, g_attn_out, mix_post_g, w_o_full)
    dx3, f2, gu2, loss_parts, _ = _ffn_fwd(x2, ffn2_pre_g, wgu2, wd2, ffn2_post_g, target, "ffn2_fwd")
    loss_local = jnp.sum(loss_parts[::8, 0])

    g = {}
    partial, from_sibling, from_chips = {}, {}, {}

    def chip_sums(keys):
        return [_chip_sum(partial[k], from_sibling[k], core, "chip_sum_" + k) for k in keys]

    n2, df2, g["ffn2_post_g"], _ = _ffn_bwd_pre(dx3, x2, f2, ffn2_pre_g, ffn2_post_g, "ffn2_bwd_pre")
    dn2, dwgu2, dwd2, _ = _ffn_bwd_main(n2, df2, gu2, wgu2, wd2, "ffn2_bwd_main")
    partial["ffn2_w_gu"] = dwgu2.reshape(N_DEV, D_MODEL, CHUNK)
    partial["ffn2_w_down"] = dwd2.reshape(N_DEV, D_FF // N_DEV, D_MODEL)
    ffn2_keys = ["ffn2_w_gu", "ffn2_w_down"]
    dx2, g["ffn2_pre_g"], got = _ffn_bwd_post(dn2, x2, ffn2_pre_g, dx3, "ffn2_bwd_post",
                                              _SiblingExchange([partial[k] for k in ffn2_keys]))
    from_sibling.update(zip(ffn2_keys, got))
    dy, do, dwo, g["mix_post_g"], g["g_lru_out"], g["g_attn_out"] = _mix_out_bwd(
        dx2, mo, h, gl, o, g_lru_out, g_attn_out, mix_post_g, w_o_full)
    dq, dkv, dsink, got = _attn_bwd(q, kv, do, sinks, _ChipExchange(chip_sums(ffn2_keys)))
    from_chips.update(zip(ffn2_keys, got))
    dxl, dgl, dp, dwrg2, dwig2 = _lru_bwd(dy, h, xl, gl, p_lru, wrg2, wig2)
    dx1, dwin_dup, g["mix_pre_g"] = _mix_in_bwd(dx2, x1, mix_pre_g, dxl, dgl, dq, dkv, w_in_full)
    partial["w_in"] = jnp.transpose(_undup_in_columns(dwin_dup).reshape(D_MODEL, N_DEV, D_IN // N_DEV), (1, 0, 2))
    partial["w_o"] = dwo.reshape(N_DEV, D_MODEL // N_DEV, D_MODEL)
    mix_keys = ["w_in", "w_o"]
    n1, df1, g["ffn1_post_g"], got = _ffn_bwd_pre(dx1, xs, f1, ffn1_pre_g, ffn1_post_g, "ffn1_bwd_pre",
                                                  _SiblingExchange([partial[k] for k in mix_keys]))
    from_sibling.update(zip(mix_keys, got))
    dn1, dwgu1, dwd1, got = _ffn_bwd_main(n1, df1, gu1, wgu1, wd1, "ffn1_bwd_main", _ChipExchange(chip_sums(mix_keys)))
    from_chips.update(zip(mix_keys, got))
    partial["ffn1_w_gu"] = dwgu1.reshape(N_DEV, D_MODEL, CHUNK)
    partial["ffn1_w_down"] = dwd1.reshape(N_DEV, D_FF // N_DEV, D_MODEL)
    ffn1_keys = ["ffn1_w_gu", "ffn1_w_down"]
    dx0, g["ffn1_pre_g"], got = _ffn_bwd_post(dn1, xs, ffn1_pre_g, dx1, "ffn1_bwd_post",
                                              _SiblingExchange([partial[k] for k in ffn1_keys]))
    from_sibling.update(zip(ffn1_keys, got))
    g["conv_b"], g["b_rg"], g["b_ig"], g["lru_lambda"] = dp[0:1], dp[1:2], dp[2:3], dp[3:4]
    g["w_rg"] = _pair_block_diag_grad(dwrg2, w_rg.shape)
    g["w_ig"] = _pair_block_diag_grad(dwig2, w_ig.shape)
    g["sinks"] = dsink[:, 0].reshape(1, 8)

    col = me * 64

    def conv_rows(a):
        return lax.dynamic_update_slice(jnp.zeros((4, D_LRU), F32), a.reshape(4, 64), (0, col))

    zero = jnp.zeros((), F32)
    g_blob = _pack_small(g, loss_local, dp[4:8])
    got, (all_blobs,) = _run_exchanges([_ChipExchange(chip_sums(ffn1_keys)), _Gather([g_blob])], "last_exchange")
    from_chips.update(zip(ffn1_keys, got))
    grads, delta, new_m, new_v = {}, {}, {}, {}
    for k in big:
        shape = w[k].shape
        res = _shard_update(partial[k], from_sibling[k], from_chips[k], shard2d[k], mom[k].reshape(shape[1:]),
                            var[k].reshape(shape[1:]), place, "update_" + k)
        grads[k], delta[k], new_m[k], new_v[k] = [r.reshape(shape) for r in res]

    res = _small_update(all_blobs, _pack_small(w, zero, conv_rows(conv_w)), _pack_small(mom, zero, conv_rows(m_conv_w)),
                        _pack_small(var, zero, conv_rows(v_conv_w)))
    loss = None
    for dst, blob in zip((grads, delta, new_m, new_v), res):
        small, first, conv_full = _unpack_small(blob, w)
        dst.update(small)
        dst["conv_w"] = lax.dynamic_slice(conv_full, (0, col), (4, 64)).reshape(conv_w.shape)
        if loss is None:
            loss = first
    return (loss, dx0.reshape(x.shape), *[grads[k] for k in names], *[delta[k] for k in names],
            *[new_m[k] for k in names], *[new_v[k] for k in names])
```

```python
import functools

import jax
import jax.numpy as jnp
from jax import lax
from jax.experimental import pallas as pl
from jax.experimental.pallas import tpu as pltpu

F32 = jnp.float32
BF16 = jnp.bfloat16

D_MODEL = 1024
D_FF = 2816
N_DEV = 8
N_CHUNK = 4
CHUNK = D_FF // N_CHUNK
D_LRU = 512
D_ATTN = 512
LRU_GROUP = 128
N_LRU_GROUP = D_LRU // LRU_GROUP
HEAD_DIM = 64
BLOCK_Q = 128
D_IN = 1792
D_IN_DUP = 2048
RMS_EPS = 1e-6
LRU_C = 8.0
MASK_VALUE = -1e30
ATTN_SCALE = HEAD_DIM ** -0.5

ADAM_LR = 0.001
ADAM_B1 = 0.9
ADAM_B2 = 0.999
ADAM_EPS = 1e-08
ADAM_WD = 0.01
ADAM_STEP = 10

VMEM_LIMIT_V7X = 56 * 2 ** 20

ANY = pl.BlockSpec(memory_space=pl.ANY)
SMEM = pl.BlockSpec(memory_space=pltpu.SMEM)
MESH = pl.DeviceIdType.MESH


def _params(n_grid=0):
    sem = ("arbitrary",) * n_grid if n_grid else None
    return pltpu.CompilerParams(dimension_semantics=sem, vmem_limit_bytes=VMEM_LIMIT_V7X)


def _dot(a, b):
    return lax.dot_general(a, b, (((1,), (0,)), ((), ())), preferred_element_type=F32)


def _dot_nt(a, b):
    return lax.dot_general(a, b, (((1,), (1,)), ((), ())), preferred_element_type=F32)


def _dot_tn(a, b):
    return lax.dot_general(a, b, (((0,), (0,)), ((), ())), preferred_element_type=F32)


def _sigmoid(x):
    return 1.0 / (1.0 + jnp.exp(-x))


def _rms_fwd(x, g):
    r = lax.rsqrt(jnp.mean(x * x, axis=-1, keepdims=True) + RMS_EPS)
    return x * r * g


def _rms_bwd(x, g, dy):
    r = lax.rsqrt(jnp.mean(x * x, axis=-1, keepdims=True) + RMS_EPS)
    xh = x * r
    dg = jnp.sum(dy * xh, axis=0, keepdims=True)
    dxh = dy * g
    dx = r * (dxh - xh * jnp.mean(dxh * xh, axis=-1, keepdims=True))
    return dx, dg


def _gelu(x):
    c = 0.7978845608028654
    inner = c * (x + 0.044715 * x * x * x)
    th = jnp.tanh(inner)
    ge = 0.5 * x * (1.0 + th)
    dge = 0.5 * (1.0 + th) + 0.5 * x * (1.0 - th * th) * c * (1.0 + 3.0 * 0.044715 * x * x)
    return ge, dge


def _accumulate(ref, val, first):
    @pl.when(first)
    def _():
        ref[...] = val

    @pl.when(jnp.logical_not(first))
    def _():
        ref[...] += val


def _token_tile(t):
    return 512 if t >= 2048 else t // 2


def _ffn_bwd_tile(t):
    return 1024 if t >= 4096 else t // 2


def _coords():
    return lax.axis_index("x"), lax.axis_index("y"), lax.axis_index("c")


class _Gather:
    n_phases = 3
    at = (0.0, 0.7, 1.0)

    def __init__(self, shards):
        k = len(shards)
        self.arrays = list(shards)
        self.out_shape = [jax.ShapeDtypeStruct((N_DEV,) + s.shape, s.dtype) for s in shards]
        self.scratch = [pltpu.SemaphoreType.DMA((7 * k,)), pltpu.SemaphoreType.DMA((7 * k,)), pltpu.SemaphoreType.DMA((k,))]

    def run(self, phase, ins, outs, sems):
        send_sems, recv_sems, local_sems = sems
        k_arr = len(ins)
        x, y, c = _coords()
        me, sibling = (x, y, c), (x, y, 1 - c)
        chips = [(1 - x, y), (x, 1 - y), (1 - x, 1 - y)]

        def rows(k, dev):
            return outs[k].at[4 * dev[0] + 2 * dev[1] + dev[2]]

        def copy(k, slot, block, to, src=None):
            return pltpu.make_async_remote_copy(
                src_ref=rows(k, block) if src is None else src, dst_ref=rows(k, block),
                send_sem=send_sems.at[7 * k + slot], recv_sem=recv_sems.at[7 * k + slot],
                device_id=to, device_id_type=MESH)

        def mine():
            return [pltpu.make_async_copy(ins[k], rows(k, me), local_sems.at[k]) for k in range(k_arr)]

        def first():
            return [copy(k, slot, me, to, src=ins[k]) for k in range(k_arr)
                    for slot, to in enumerate([sibling] + [(*chip, c) for chip in chips])]

        def passed(j, k):
            return copy(k, 4 + j, (*chips[j], c), sibling)

        if phase == 0:
            for cp in mine() + first():
                cp.start()
        elif phase == 1:
            for j, chip in enumerate(chips):
                for k in range(k_arr):
                    copy(k, 1 + j, (*chip, c), me).wait_recv()
                    passed(j, k).start()
        else:
            for k in range(k_arr):
                copy(k, 0, sibling, me).wait_recv()
                for j, chip in enumerate(chips):
                    copy(k, 4 + j, (*chip, 1 - c), me).wait_recv()
            for cp in first() + [passed(j, k) for j in range(3) for k in range(k_arr)]:
                cp.wait_send()
            for cp in mine():
                cp.wait()


class _SiblingExchange:
    n_phases = 2
    at = (0.0, 1.0)

    def __init__(self, grads):
        k = len(grads)
        self.arrays = list(grads)
        self.out_shape = [jax.ShapeDtypeStruct((4,) + g.shape[1:], g.dtype) for g in grads]
        self.scratch = [pltpu.SemaphoreType.DMA((4 * k,)), pltpu.SemaphoreType.DMA((4 * k,))]

    def run(self, phase, ins, outs, sems):
        send_sems, recv_sems = sems
        x, y, c = _coords()
        copies = [pltpu.make_async_remote_copy(
            src_ref=ins[k].at[2 * q + (1 - c)], dst_ref=outs[k].at[q],
            send_sem=send_sems.at[4 * k + q], recv_sem=recv_sems.at[4 * k + q],
            device_id=(x, y, 1 - c), device_id_type=MESH) for k in range(len(ins)) for q in range(4)]
        for cp in copies:
            if phase == 0:
                cp.start()
            else:
                cp.wait_recv()
                cp.wait_send()


class _ChipExchange:
    n_phases = 2
    at = (0.0, 1.0)

    def __init__(self, chip_sums):
        k = len(chip_sums)
        self.arrays = list(chip_sums)
        self.out_shape = [jax.ShapeDtypeStruct((3,) + s.shape[1:], s.dtype) for s in chip_sums]
        self.scratch = [pltpu.SemaphoreType.DMA((3 * k,)), pltpu.SemaphoreType.DMA((3 * k,))]

    def run(self, phase, ins, outs, sems):
        send_sems, recv_sems = sems
        x, y, c = _coords()
        chips = [(1 - x, y), (x, 1 - y), (1 - x, 1 - y)]
        copies = [pltpu.make_async_remote_copy(
            src_ref=ins[k].at[2 * chip[0] + chip[1]], dst_ref=outs[k].at[j],
            send_sem=send_sems.at[3 * k + j], recv_sem=recv_sems.at[3 * k + j],
            device_id=(*chip, c), device_id_type=MESH) for k in range(len(ins)) for j, chip in enumerate(chips)]
        for cp in copies:
            if phase == 0:
                cp.start()
            else:
                cp.wait_recv()
                cp.wait_send()


class _Host:
    def __init__(self, exchange):
        self.ex = exchange
        self.args = [] if exchange is None else exchange.arrays
        self.in_specs = [ANY] * len(self.args)
        self.out_shape = [] if exchange is None else exchange.out_shape
        self.out_specs = [ANY] * len(self.out_shape)
        self.scratch = [] if exchange is None else exchange.scratch

    def split(self, refs, n_in, n_out, n_scratch):
        a, b, s = len(self.args), len(self.out_shape), len(self.scratch)
        own_in, ex_in = refs[:n_in], refs[n_in:n_in + a]
        rest = refs[n_in + a:]
        own_out, ex_out = rest[:n_out], rest[n_out:n_out + b]
        rest = rest[n_out + b:]
        own_scratch, ex_sems = rest[:n_scratch], rest[n_scratch:n_scratch + s]
        return list(own_in) + list(own_out) + list(own_scratch), (ex_in, ex_out, ex_sems)

    def at_steps(self, step, n_steps, ex_refs):
        if self.ex is None:
            return
        for p in range(self.ex.n_phases):
            pl.when(step == int(round(self.ex.at[p] * (n_steps - 1))))(functools.partial(self.ex.run, p, *ex_refs))

    def phase(self, p, ex_refs):
        if self.ex is not None:
            self.ex.run(p, *ex_refs)


def _run_exchanges(exchanges, name):
    hosts = [_Host(ex) for ex in exchanges]
    n_in = [len(h.args) for h in hosts]
    n_out = [len(h.out_shape) for h in hosts]
    n_sc = [len(h.scratch) for h in hosts]

    def body(*refs):
        ins, outs, scr = refs[:sum(n_in)], refs[sum(n_in):sum(n_in) + sum(n_out)], refs[sum(n_in) + sum(n_out):]
        parts = []
        for e in range(len(hosts)):
            parts.append((ins[sum(n_in[:e]):sum(n_in[:e + 1])], outs[sum(n_out[:e]):sum(n_out[:e + 1])],
                          scr[sum(n_sc[:e]):sum(n_sc[:e + 1])]))
        for h, part in zip(hosts, parts):
            h.phase(0, part)
        for h, part in zip(hosts, parts):
            for p in range(1, h.ex.n_phases):
                h.phase(p, part)

    res = pl.pallas_call(
        body, name=name, in_specs=[ANY] * sum(n_in), out_specs=[ANY] * sum(n_out),
        out_shape=[s for h in hosts for s in h.out_shape], scratch_shapes=[s for h in hosts for s in h.scratch],
    )(*[a for h in hosts for a in h.args])
    return [res[sum(n_out[:e]):sum(n_out[:e + 1])] for e in range(len(hosts))]


def _ffn_fwd(x, g_pre, wgu, wd, g_post, target, name, exchange=None):
    t = x.shape[0]
    tm = _token_tile(t)
    n_i = t // tm
    with_loss = target is not None
    host = _Host(exchange)
    n_in, n_out = (6, 4) if with_loss else (5, 3)

    def body(*refs):
        own, ex_refs = host.split(refs, n_in, n_out, 2)
        if with_loss:
            x_ref, gpre_ref, wgu_ref, wd_ref, gpost_ref, tgt_ref, xo_ref, f_ref, gu_ref, loss_ref, n_bf, acc = own
        else:
            x_ref, gpre_ref, wgu_ref, wd_ref, gpost_ref, xo_ref, f_ref, gu_ref, n_bf, acc = own
        j = pl.program_id(1)
        host.at_steps(pl.program_id(0) * N_CHUNK + j, n_i * N_CHUNK, ex_refs)

        @pl.when(j == 0)
        def _():
            n_bf[...] = _rms_fwd(x_ref[...], gpre_ref[...]).astype(BF16)
            acc[...] = jnp.zeros_like(acc)

        n = n_bf[...]
        gate = _dot(n, wgu_ref[0, 0])
        up = _dot(n, wgu_ref[1, 0])
        gu_ref[0, 0] = gate.astype(BF16)
        gu_ref[1, 0] = up.astype(BF16)
        a = (gate * _sigmoid(gate) * up).astype(BF16)
        acc[...] += _dot(a, wd_ref[0])

        @pl.when(j == N_CHUNK - 1)
        def _():
            f = acc[...]
            f_ref[...] = f
            xo = x_ref[...] + 0.5 * _rms_fwd(f, gpost_ref[...])
            if with_loss:
                err = xo - tgt_ref[...]
                xo_ref[...] = err * (1.0 / D_MODEL)
                part = 0.5 * jnp.sum(jnp.sum(err * err, axis=-1, keepdims=True) * (1.0 / D_MODEL), axis=0, keepdims=True)
                loss_ref[...] = jnp.broadcast_to(part, loss_ref.shape)
            else:
                xo_ref[...] = xo

    tok = pl.BlockSpec((tm, D_MODEL), lambda i, j: (i, 0))
    vec = pl.BlockSpec((1, D_MODEL), lambda i, j: (0, 0))
    in_specs = [tok, vec,
                pl.BlockSpec((2, 1, D_MODEL, CHUNK), lambda i, j: (0, j, 0, 0)),
                pl.BlockSpec((1, CHUNK, D_MODEL), lambda i, j: (j, 0, 0)),
                vec]
    out_shape = [jax.ShapeDtypeStruct((t, D_MODEL), F32), jax.ShapeDtypeStruct((t, D_MODEL), F32),
                 jax.ShapeDtypeStruct((2, N_CHUNK, t, CHUNK), BF16)]
    out_specs = [tok, tok, pl.BlockSpec((2, 1, tm, CHUNK), lambda i, j: (0, j, i, 0))]
    args = [x, g_pre, wgu, wd, g_post]
    if with_loss:
        in_specs.append(tok)
        args.append(target)
        out_shape.append(jax.ShapeDtypeStruct((n_i * 8, 128), F32))
        out_specs.append(pl.BlockSpec((8, 128), lambda i, j: (i, 0)))
    res = pl.pallas_call(
        body, name=name, grid=(n_i, N_CHUNK), in_specs=in_specs + host.in_specs, out_specs=out_specs + host.out_specs,
        out_shape=out_shape + host.out_shape,
        scratch_shapes=[pltpu.VMEM((tm, D_MODEL), BF16), pltpu.VMEM((tm, D_MODEL), F32)] + host.scratch,
        compiler_params=_params(2),
    )(*args, *host.args)
    return (*res[:n_out], list(res[n_out:]))


def _ffn_bwd_pre(d_out, x, f, g_pre, g_post, name, exchange=None):
    t = x.shape[0]
    tm = _token_tile(t)
    host = _Host(exchange)

    def body(*refs):
        (do_ref, x_ref, f_ref, gpre_ref, gpost_ref, n_ref, df_ref, dgpost_ref), ex_refs = host.split(refs, 5, 3, 0)
        i = pl.program_id(0)
        host.at_steps(i, t // tm, ex_refs)
        n_ref[...] = _rms_fwd(x_ref[...], gpre_ref[...]).astype(BF16)
        df, dg = _rms_bwd(f_ref[...], gpost_ref[...], 0.5 * do_ref[...])
        df_ref[...] = df.astype(BF16)
        _accumulate(dgpost_ref, dg, i == 0)

    tok = pl.BlockSpec((tm, D_MODEL), lambda i: (i, 0))
    vec = pl.BlockSpec((1, D_MODEL), lambda i: (0, 0))
    res = pl.pallas_call(
        body, name=name, grid=(t // tm,), in_specs=[tok, tok, tok, vec, vec] + host.in_specs,
        out_specs=[tok, tok, vec] + host.out_specs,
        out_shape=[jax.ShapeDtypeStruct((t, D_MODEL), BF16), jax.ShapeDtypeStruct((t, D_MODEL), BF16),
                   jax.ShapeDtypeStruct((1, D_MODEL), F32)] + host.out_shape,
        scratch_shapes=host.scratch, compiler_params=_params(1),
    )(d_out, x, f, g_pre, g_post, *host.args)
    return (*res[:3], list(res[3:]))


def _ffn_bwd_w(n, df, gu, wd, name, exchange=None):
    t = n.shape[0]
    tm = _ffn_bwd_tile(t)
    n_i = t // tm
    host = _Host(exchange)

    def body(*refs):
        (n_ref, df_ref, gu_ref, wd_ref, dgu_ref, dwgu_ref, dwd_ref), ex_refs = host.split(refs, 4, 3, 0)
        i = pl.program_id(1)
        host.at_steps(pl.program_id(0) * n_i + i, N_CHUNK * n_i, ex_refs)
        nb = n_ref[...]
        dfb = df_ref[...]
        gate = gu_ref[0, 0].astype(F32)
        up = gu_ref[1, 0].astype(F32)
        s = _sigmoid(gate)
        silu = gate * s
        a = (silu * up).astype(BF16)
        da = _dot_nt(dfb, wd_ref[0])
        dup = (da * silu).astype(BF16)
        dgate = (da * up * (s * (1.0 + gate * (1.0 - s)))).astype(BF16)
        dgu_ref[0, 0] = dgate
        dgu_ref[1, 0] = dup
        first = i == 0
        _accumulate(dwgu_ref.at[0, 0], _dot_tn(nb, dgate), first)
        _accumulate(dwgu_ref.at[1, 0], _dot_tn(nb, dup), first)
        _accumulate(dwd_ref.at[0], _dot_tn(a, dfb), first)

    tok = pl.BlockSpec((tm, D_MODEL), lambda j, i: (i, 0))
    act = pl.BlockSpec((2, 1, tm, CHUNK), lambda j, i: (0, j, i, 0))
    wgu_spec = pl.BlockSpec((2, 1, D_MODEL, CHUNK), lambda j, i: (0, j, 0, 0), pipeline_mode=pl.Buffered(1))
    wd_spec = pl.BlockSpec((1, CHUNK, D_MODEL), lambda j, i: (j, 0, 0), pipeline_mode=pl.Buffered(1))
    res = pl.pallas_call(
        body, name=name, grid=(N_CHUNK, n_i),
        in_specs=[tok, tok, act, wd_spec] + host.in_specs,
        out_specs=[act, wgu_spec, wd_spec] + host.out_specs,
        out_shape=[jax.ShapeDtypeStruct((2, N_CHUNK, t, CHUNK), BF16),
                   jax.ShapeDtypeStruct((2, N_CHUNK, D_MODEL, CHUNK), F32),
                   jax.ShapeDtypeStruct((N_CHUNK, CHUNK, D_MODEL), F32)] + host.out_shape,
        scratch_shapes=host.scratch, compiler_params=_params(2),
    )(n, df, gu, wd, *host.args)
    return (*res[:3], list(res[3:]))


def _ffn_bwd_x(dgu, wgu, x, g_pre, d_out, name, exchange=None):
    t = x.shape[0]
    tm = _token_tile(t)
    n_i = t // tm
    host = _Host(exchange)

    def body(*refs):
        (dgu_ref, wgu_ref, x_ref, gpre_ref, do_ref, dx_ref, dgpre_ref), ex_refs = host.split(refs, 5, 2, 0)
        i = pl.program_id(0)
        host.at_steps(i, n_i, ex_refs)
        dn = _dot_nt(dgu_ref[0, 0], wgu_ref[0, 0]) + _dot_nt(dgu_ref[1, 0], wgu_ref[1, 0])
        for j in range(1, N_CHUNK):
            dn = dn + _dot_nt(dgu_ref[0, j], wgu_ref[0, j]) + _dot_nt(dgu_ref[1, j], wgu_ref[1, j])
        dx, dg = _rms_bwd(x_ref[...], gpre_ref[...], dn)
        dx_ref[...] = do_ref[...] + dx
        _accumulate(dgpre_ref, dg, i == 0)

    tok = pl.BlockSpec((tm, D_MODEL), lambda i: (i, 0))
    vec = pl.BlockSpec((1, D_MODEL), lambda i: (0, 0))
    res = pl.pallas_call(
        body, name=name, grid=(n_i,),
        in_specs=[pl.BlockSpec((2, N_CHUNK, tm, CHUNK), lambda i: (0, 0, i, 0)),
                  pl.BlockSpec((2, N_CHUNK, D_MODEL, CHUNK), lambda i: (0, 0, 0, 0), pipeline_mode=pl.Buffered(1)),
                  tok, vec, tok] + host.in_specs,
        out_specs=[tok, vec] + host.out_specs,
        out_shape=[jax.ShapeDtypeStruct((t, D_MODEL), F32), jax.ShapeDtypeStruct((1, D_MODEL), F32)] + host.out_shape,
        scratch_shapes=host.scratch, compiler_params=_params(1),
    )(dgu, wgu, x, g_pre, d_out, *host.args)
    return (*res[:2], list(res[2:]))


def _mix_in_fwd(x1, g, w_in):
    t = x1.shape[0]
    tm = _token_tile(t)

    def body(x_ref, g_ref, w_ref, xl_ref, gl_ref, q_ref, kv_ref):
        n = _rms_fwd(x_ref[...], g_ref[...]).astype(BF16)
        proj = _dot(n, w_ref[...])
        xl_ref[...] = proj[:, 0:512]
        gl_ref[...] = proj[:, 512:1024]
        q_ref[...] = proj[:, 1024:1536].astype(BF16)
        kv_ref[...] = proj[:, 1536:2048].astype(BF16)

    tok = pl.BlockSpec((tm, D_MODEL), lambda i: (i, 0))
    half = pl.BlockSpec((tm, 512), lambda i: (i, 0))
    return pl.pallas_call(
        body, name="mix_in_fwd", grid=(t // tm,),
        in_specs=[tok, pl.BlockSpec((1, D_MODEL), lambda i: (0, 0)), pl.BlockSpec((D_MODEL, D_IN_DUP), lambda i: (0, 0))],
        out_specs=[half, half, half, half],
        out_shape=[jax.ShapeDtypeStruct((t, 512), F32), jax.ShapeDtypeStruct((t, 512), F32),
                   jax.ShapeDtypeStruct((t, 512), BF16), jax.ShapeDtypeStruct((t, 512), BF16)],
        compiler_params=_params(1),
    )(x1, g, w_in)


def _shift_down(x, before, s):
    if s == 0:
        return x
    rolled = pltpu.roll(x, s, 0)
    ext = jnp.concatenate([before, x[0:8]], axis=0)
    first8 = pltpu.roll(ext, s, 0)[8:16]
    return jnp.concatenate([first8, rolled[8:]], axis=0)


def _shift_up(x, after, s):
    if s == 0:
        return x
    rows = x.shape[0]
    rolled = pltpu.roll(x, rows - s, 0)
    ext = jnp.concatenate([x[rows - 8:rows], after], axis=0)
    last8 = pltpu.roll(ext, 16 - s, 0)[0:8]
    return jnp.concatenate([rolled[:rows - 8], last8], axis=0)


def _log_sigmoid(x):
    e = jnp.exp(-jnp.abs(x))
    log1p_e = jnp.where(e < 0.01, e * (1.0 - e * (0.5 - e * (1.0 / 3.0))), jnp.log(1.0 + e))
    return jnp.minimum(x, 0.0) - log1p_e


def _lru_gates(xc, p_ref, wrg, wig):
    xcb = xc.astype(BF16)
    r = _sigmoid(_dot(xcb, wrg) + p_ref[1:2, :])
    ig = _sigmoid(_dot(xcb, wig) + p_ref[2:3, :])
    ls = _log_sigmoid(p_ref[3:4, :])
    log_a = LRU_C * r * ls
    a = jnp.exp(log_a)
    z = 2.0 * log_a
    series = z * (1.0 + z * (0.5 + z * (1.0 / 6.0 + z * (1.0 / 24.0 + z * (1.0 / 120.0 + z * (1.0 / 720.0))))))
    expm1 = jnp.where(z > -0.1, series, jnp.exp(z) - 1.0)
    mult = jnp.sqrt(-expm1)
    return xcb, r, ig, ls, a, mult


def _conv_taps(x, before, p_ref):
    xc = x * p_ref[7:8, :]
    for s in (1, 2, 3):
        xc = xc + _shift_down(x, before, s) * p_ref[7 - s:8 - s, :]
    return xc + p_ref[0:1, :]


def _lru_block_rows(t):
    return 512 if t >= 1024 else t // 2


def _lru_fwd(xl, p, wrg2, wig2):
    t = xl.shape[0]
    tb = _lru_block_rows(t)

    def body(xl_ref, p_ref, wrg_ref, wig_ref, h_ref, x_tail, h_carry):
        tt = pl.program_id(1)

        @pl.when(tt == 0)
        def _():
            x_tail[...] = jnp.zeros_like(x_tail)
            h_carry[...] = jnp.zeros_like(h_carry)

        x = xl_ref[...]
        xc = _conv_taps(x, x_tail[...], p_ref)
        x_tail[...] = x[tb - 8:tb]
        _, r, ig, ls, a, mult = _lru_gates(xc, p_ref, wrg_ref[0], wig_ref[0])
        u = mult * ig * xc
        row = lax.broadcasted_iota(jnp.int32, (tb, LRU_GROUP), 0)
        s = 1
        while s < tb:
            keep = row >= s
            u = jnp.where(keep, a * pltpu.roll(u, s, 0) + u, u)
            a = jnp.where(keep, a * pltpu.roll(a, s, 0), a)
            s *= 2
        h = u + a * h_carry[0:1, :]
        h_ref[...] = h
        h_carry[...] = jnp.broadcast_to(h[tb - 1:tb], h_carry.shape)

    blk = pl.BlockSpec((tb, LRU_GROUP), lambda g, tt: (tt, g))
    par = pl.BlockSpec((8, LRU_GROUP), lambda g, tt: (0, g))
    wsp = pl.BlockSpec((1, LRU_GROUP, LRU_GROUP), lambda g, tt: (g, 0, 0))
    return pl.pallas_call(
        body, name="lru_fwd", grid=(N_LRU_GROUP, t // tb), in_specs=[blk, par, wsp, wsp], out_specs=blk,
        out_shape=jax.ShapeDtypeStruct((t, D_LRU), F32),
        scratch_shapes=[pltpu.VMEM((8, LRU_GROUP), F32), pltpu.VMEM((8, LRU_GROUP), F32)],
        compiler_params=_params(2),
    )(xl, p, wrg2, wig2)


def _lru_bwd(dy, h, xl, gl, p, wrg2, wig2):
    t = xl.shape[0]
    tb = _lru_block_rows(t)
    n_tb = t // tb
    tb8 = tb // 8

    def body(dy_ref, h_ref, hprev_ref, xl_ref, xprev_ref, gl_ref, p_ref, wrg_ref, wig_ref,
             dxl_ref, dgl_ref, dp_ref, dwrg_ref, dwig_ref, g_carry, a_carry, dxc_head):
        step = pl.program_id(1)
        tt = n_tb - 1 - step
        first = step == 0

        @pl.when(first)
        def _():
            g_carry[...] = jnp.zeros_like(g_carry)
            a_carry[...] = jnp.zeros_like(a_carry)
            dxc_head[...] = jnp.zeros_like(dxc_head)

        has_prev = (tt > 0).astype(F32)
        x = xl_ref[...]
        x_before = xprev_ref[...] * has_prev
        xs = [_shift_down(x, x_before, s) for s in range(4)]
        xc = xs[0] * p_ref[7:8, :] + xs[1] * p_ref[6:7, :] + xs[2] * p_ref[5:6, :] + xs[3] * p_ref[4:5, :] + p_ref[0:1, :]
        wrg = wrg_ref[0]
        wig = wig_ref[0]
        xcb, r, ig, ls, a, mult = _lru_gates(xc, p_ref, wrg, wig)

        hh = h_ref[...]
        h_m1 = _shift_down(hh, hprev_ref[...] * has_prev, 1)
        ge, dge = _gelu(gl_ref[...])
        dy = dy_ref[...]
        dgl_ref[...] = dy * hh * dge
        dh = dy * ge

        b = _shift_up(a, a_carry[...], 1)
        row = lax.broadcasted_iota(jnp.int32, (tb, LRU_GROUP), 0)
        g = dh
        s = 1
        while s < tb:
            keep = row < tb - s
            g = jnp.where(keep, b * pltpu.roll(g, tb - s, 0) + g, g)
            b = jnp.where(keep, b * pltpu.roll(b, tb - s, 0), b)
            s *= 2
        g = g + b * g_carry[0:1, :]
        g_carry[...] = jnp.broadcast_to(g[0:1], g_carry.shape)
        a_carry[...] = jnp.broadcast_to(a[0:1], a_carry.shape)

        da = g * h_m1
        dmult = g * ig * xc
        dig = g * mult * xc
        dxc = g * mult * ig
        dlog_a = da * a - dmult * (a * a) / mult
        dr = dlog_a * (LRU_C * ls)
        dls = jnp.sum(dlog_a * (LRU_C * r), axis=0, keepdims=True)
        dlam = dls * _sigmoid(-p_ref[3:4, :])
        dpre_r = dr * r * (1.0 - r)
        dpre_i = dig * ig * (1.0 - ig)
        dprb = dpre_r.astype(BF16)
        dpib = dpre_i.astype(BF16)
        dxc = dxc + _dot_nt(dprb, wrg) + _dot_nt(dpib, wig)
        _accumulate(dwrg_ref.at[0], _dot_tn(xcb, dprb), first)
        _accumulate(dwig_ref.at[0], _dot_tn(xcb, dpib), first)

        after = dxc_head[...]
        dxl = dxc * p_ref[7:8, :]
        for s in (1, 2, 3):
            dxl = dxl + _shift_up(dxc, after, s) * p_ref[7 - s:8 - s, :]
        dxl_ref[...] = dxl
        dxc_head[...] = dxc[0:8]

        rows = [jnp.sum(dxc, axis=0, keepdims=True), jnp.sum(dpre_r, axis=0, keepdims=True),
                jnp.sum(dpre_i, axis=0, keepdims=True), dlam]
        rows += [jnp.sum(dxc * xs[3 - k], axis=0, keepdims=True) for k in range(4)]
        _accumulate(dp_ref, jnp.concatenate(rows, axis=0), first)

    blk = pl.BlockSpec((tb, LRU_GROUP), lambda g, s: (n_tb - 1 - s, g))
    prev8 = pl.BlockSpec((8, LRU_GROUP), lambda g, s: (jnp.maximum((n_tb - 1 - s) * tb8 - 1, 0), g))
    par = pl.BlockSpec((8, LRU_GROUP), lambda g, s: (0, g))
    wsp = pl.BlockSpec((1, LRU_GROUP, LRU_GROUP), lambda g, s: (g, 0, 0))
    return pl.pallas_call(
        body, name="lru_bwd", grid=(N_LRU_GROUP, n_tb),
        in_specs=[blk, blk, prev8, blk, prev8, blk, par, wsp, wsp], out_specs=[blk, blk, par, wsp, wsp],
        out_shape=[jax.ShapeDtypeStruct((t, D_LRU), F32), jax.ShapeDtypeStruct((t, D_LRU), F32),
                   jax.ShapeDtypeStruct((8, D_LRU), F32),
                   jax.ShapeDtypeStruct((N_LRU_GROUP, LRU_GROUP, LRU_GROUP), F32),
                   jax.ShapeDtypeStruct((N_LRU_GROUP, LRU_GROUP, LRU_GROUP), F32)],
        scratch_shapes=[pltpu.VMEM((8, LRU_GROUP), F32)] * 3,
        compiler_params=_params(2),
    )(dy, h, h, xl, xl, gl, p, wrg2, wig2)


def _attn_bias(first_block):
    qi = jnp.bitwise_and(lax.broadcasted_iota(jnp.int32, (4 * BLOCK_Q, 2 * BLOCK_Q), 0), BLOCK_Q - 1)
    kj = lax.broadcasted_iota(jnp.int32, (4 * BLOCK_Q, 2 * BLOCK_Q), 1)
    rel = qi + BLOCK_Q - kj
    mask = (rel >= 0) & (rel < BLOCK_Q)
    if first_block:
        mask = mask & (kj >= BLOCK_Q)
    return jnp.where(mask, 0.0, MASK_VALUE)


def _sink_column(sinks):
    hrow = lax.broadcasted_iota(jnp.int32, (4 * BLOCK_Q, 1), 0)
    return jnp.where(hrow < BLOCK_Q, sinks[0],
                     jnp.where(hrow < 2 * BLOCK_Q, sinks[1], jnp.where(hrow < 3 * BLOCK_Q, sinks[2], sinks[3])))


def _attn_scores(qv, kvv, n, bias, sk, lo):
    r0 = pl.multiple_of(n * BLOCK_Q, BLOCK_Q)
    rp = pl.multiple_of(jnp.maximum(n - 1, 0) * BLOCK_Q, BLOCK_Q)
    kvb = jnp.concatenate([kvv[pl.ds(rp, BLOCK_Q), :], kvv[pl.ds(r0, BLOCK_Q), :]], axis=0)
    k2 = kvb[:, 0:128]
    v2 = kvb[:, 128:256]
    qs = _stack_heads(qv[pl.ds(r0, BLOCK_Q), :], lo)
    s = _dot_nt(qs, k2) * ATTN_SCALE + bias
    m = jnp.maximum(jnp.max(s, axis=-1, keepdims=True), sk)
    e = jnp.exp(s - m)
    es = jnp.exp(sk - m)
    inv = 1.0 / (jnp.sum(e, axis=-1, keepdims=True) + es)
    return r0, rp, qs, k2, v2, e * inv, es * inv


def _stack_heads(pair2, lo):
    p0 = pair2[:, 0:128]
    p1 = pair2[:, 128:256]
    z = jnp.zeros_like(p0)
    return jnp.concatenate([jnp.where(lo, p0, z), jnp.where(lo, z, p0), jnp.where(lo, p1, z), jnp.where(lo, z, p1)], axis=0)


def _unstack_heads(st, lo):
    b = BLOCK_Q
    return jnp.concatenate([jnp.where(lo, st[0:b], st[b:2 * b]), jnp.where(lo, st[2 * b:3 * b], st[3 * b:4 * b])], axis=1)


def _attn_fwd(q, kv, sinks):
    t = q.shape[0]
    n_blk = t // BLOCK_Q

    def body(q_hbm, kv_hbm, s_ref, o_hbm, qv, kvv, ov, bias0, bias, sem):
        lo = lax.broadcasted_iota(jnp.int32, (BLOCK_Q, 128), 1) < HEAD_DIM
        bias0[...] = _attn_bias(True)
        bias[...] = _attn_bias(False)
        for g in range(2):
            cols = pl.ds(256 * g, 256)
            loads = [pltpu.make_async_copy(q_hbm.at[:, cols], qv, sem.at[0]),
                     pltpu.make_async_copy(kv_hbm.at[:, cols], kvv, sem.at[1])]
            for cp in loads:
                cp.start()
            for cp in loads:
                cp.wait()
            sk = _sink_column([s_ref[0, 4 * g + i] for i in range(4)])

            def block(n, bias_ref):
                r0, _, _, _, v2, prob, _ = _attn_scores(qv, kvv, n, bias_ref[...], sk, lo)
                ov[pl.ds(r0, BLOCK_Q), :] = _unstack_heads(_dot(prob.astype(BF16), v2), lo)

            block(0, bias0)

            def later(n, carry):
                block(n, bias)
                return carry

            lax.fori_loop(1, n_blk, later, 0, unroll=2)
            store = pltpu.make_async_copy(ov, o_hbm.at[:, cols], sem.at[2])
            store.start()
            store.wait()

    return pl.pallas_call(
        body, name="attn_fwd", in_specs=[ANY, ANY, SMEM], out_specs=ANY,
        out_shape=jax.ShapeDtypeStruct((t, D_ATTN), F32),
        scratch_shapes=[pltpu.VMEM((t, 256), BF16), pltpu.VMEM((t, 256), BF16), pltpu.VMEM((t, 256), F32),
                        pltpu.VMEM((4 * BLOCK_Q, 2 * BLOCK_Q), F32), pltpu.VMEM((4 * BLOCK_Q, 2 * BLOCK_Q), F32),
                        pltpu.SemaphoreType.DMA((3,))],
        compiler_params=_params(),
    )(q, kv, sinks)


def _attn_bwd(q, kv, do, sinks, exchange=None):
    t = q.shape[0]
    n_blk = t // BLOCK_Q
    host = _Host(exchange)

    def body(*refs):
        own, ex_refs = host.split(refs, 4, 3, 9)
        q_hbm, kv_hbm, do_hbm, s_ref, dq_hbm, dkv_hbm, dsink_ref, qv, kvv, dov, dqv, dkvv, ds_acc, bias0, bias, sem = own
        host.phase(0, ex_refs)
        lo = lax.broadcasted_iota(jnp.int32, (BLOCK_Q, 128), 1) < HEAD_DIM
        bias0[...] = _attn_bias(True)
        bias[...] = _attn_bias(False)
        for g in range(2):
            cols = pl.ds(256 * g, 256)
            loads = [pltpu.make_async_copy(q_hbm.at[:, cols], qv, sem.at[0]),
                     pltpu.make_async_copy(kv_hbm.at[:, cols], kvv, sem.at[1]),
                     pltpu.make_async_copy(do_hbm.at[:, cols], dov, sem.at[2])]
            for cp in loads:
                cp.start()
            for cp in loads:
                cp.wait()
            sk = _sink_column([s_ref[0, 4 * g + i] for i in range(4)])
            ds_acc[...] = jnp.zeros_like(ds_acc)

            def block(n, bias_ref, has_prev):
                r0, rp, qs, k2, v2, prob, psink = _attn_scores(qv, kvv, n, bias_ref[...], sk, lo)
                pb = prob.astype(BF16)
                dos = _stack_heads(dov[pl.ds(r0, BLOCK_Q), :], lo)
                dp = _dot_nt(dos, v2)
                dsum = jnp.sum(prob * dp, axis=-1, keepdims=True)
                dsb = (prob * (dp - dsum) * ATTN_SCALE).astype(BF16)
                ds_acc[...] -= psink * dsum
                dqv[pl.ds(r0, BLOCK_Q), :] = _unstack_heads(_dot(dsb, k2), lo).astype(BF16)
                dk2 = _dot_tn(dsb, qs)
                dv2 = _dot_tn(pb, dos)
                dkvv[pl.ds(r0, BLOCK_Q), :] = jnp.concatenate([dk2[BLOCK_Q:], dv2[BLOCK_Q:]], axis=1)
                if has_prev:
                    dkvv[pl.ds(rp, BLOCK_Q), :] += jnp.concatenate([dk2[:BLOCK_Q], dv2[:BLOCK_Q]], axis=1)

            block(0, bias0, False)

            def later(n, carry):
                block(n, bias, True)
                return carry

            lax.fori_loop(1, n_blk, later, 0, unroll=2)
            for i in range(4):
                tot = jnp.sum(ds_acc[BLOCK_Q * i:BLOCK_Q * (i + 1), :], axis=0, keepdims=True)
                dsink_ref[4 * g + i:4 * g + i + 1, :] = jnp.broadcast_to(tot, (1, 128))
            stores = [pltpu.make_async_copy(dqv, dq_hbm.at[:, cols], sem.at[0]),
                      pltpu.make_async_copy(dkvv, dkv_hbm.at[:, cols], sem.at[1])]
            for cp in stores:
                cp.start()
            for cp in stores:
                cp.wait()
        if exchange is not None:
            for p in range(1, exchange.n_phases):
                host.phase(p, ex_refs)

    res = pl.pallas_call(
        body, name="attn_bwd", in_specs=[ANY, ANY, ANY, SMEM] + host.in_specs,
        out_specs=[ANY, ANY, pl.BlockSpec(memory_space=pltpu.VMEM)] + host.out_specs,
        out_shape=[jax.ShapeDtypeStruct((t, D_ATTN), BF16), jax.ShapeDtypeStruct((t, 512), F32),
                   jax.ShapeDtypeStruct((8, 128), F32)] + host.out_shape,
        scratch_shapes=[pltpu.VMEM((t, 256), BF16), pltpu.VMEM((t, 256), BF16), pltpu.VMEM((t, 256), BF16),
                        pltpu.VMEM((t, 256), BF16), pltpu.VMEM((t, 256), F32), pltpu.VMEM((4 * BLOCK_Q, 1), F32),
                        pltpu.VMEM((4 * BLOCK_Q, 2 * BLOCK_Q), F32), pltpu.VMEM((4 * BLOCK_Q, 2 * BLOCK_Q), F32),
                        pltpu.SemaphoreType.DMA((3,))] + host.scratch,
        compiler_params=_params(),
    )(q, kv, do, sinks, *host.args)
    return (*res[:3], list(res[3:]))


def _mix_out_fwd(x1, h, gl, o, g_lru, g_attn, g_post, w_o):
    t = x1.shape[0]
    tm = _token_tile(t)

    def body(x_ref, h_ref, gl_ref, o_ref, g1_ref, g2_ref, gp_ref, w_ref, x2_ref, m_ref):
        y = h_ref[...] * _gelu(gl_ref[...])[0]
        yn1 = _rms_fwd(y, g1_ref[...]).astype(BF16)
        yn2 = _rms_fwd(o_ref[...], g2_ref[...]).astype(BF16)
        m = _dot(yn1, w_ref[0:512, :]) + _dot(yn2, w_ref[512:1024, :])
        m_ref[...] = m
        x2_ref[...] = x_ref[...] + _rms_fwd(m, gp_ref[...])

    tok = pl.BlockSpec((tm, D_MODEL), lambda i: (i, 0))
    half = pl.BlockSpec((tm, 512), lambda i: (i, 0))
    vec = pl.BlockSpec((1, D_MODEL), lambda i: (0, 0))
    hvec = pl.BlockSpec((1, 512), lambda i: (0, 0))
    return pl.pallas_call(
        body, name="mix_out_fwd", grid=(t // tm,),
        in_specs=[tok, half, half, half, hvec, hvec, vec, pl.BlockSpec((D_MODEL, D_MODEL), lambda i: (0, 0))],
        out_specs=[tok, tok],
        out_shape=[jax.ShapeDtypeStruct((t, D_MODEL), F32), jax.ShapeDtypeStruct((t, D_MODEL), F32)],
        compiler_params=_params(1),
    )(x1, h, gl, o, g_lru, g_attn, g_post, w_o)


def _mix_out_bwd(dx2, m, h, gl, o, g_lru, g_attn, g_post, w_o):
    t = dx2.shape[0]
    tm = _token_tile(t)

    def body(dx_ref, m_ref, h_ref, gl_ref, o_ref, g1_ref, g2_ref, gp_ref, w_ref,
             dy_ref, do_ref, dw_ref, dgp_ref, dg1_ref, dg2_ref):
        first = pl.program_id(0) == 0
        dm, dgp = _rms_bwd(m_ref[...], gp_ref[...], dx_ref[...])
        dmb = dm.astype(BF16)
        y = h_ref[...] * _gelu(gl_ref[...])[0]
        o = o_ref[...]
        yn1 = _rms_fwd(y, g1_ref[...]).astype(BF16)
        yn2 = _rms_fwd(o, g2_ref[...]).astype(BF16)
        _accumulate(dw_ref.at[0:512, :], _dot_tn(yn1, dmb), first)
        _accumulate(dw_ref.at[512:1024, :], _dot_tn(yn2, dmb), first)
        dy, dg1 = _rms_bwd(y, g1_ref[...], _dot_nt(dmb, w_ref[0:512, :]))
        do, dg2 = _rms_bwd(o, g2_ref[...], _dot_nt(dmb, w_ref[512:1024, :]))
        dy_ref[...] = dy
        do_ref[...] = do.astype(BF16)
        _accumulate(dgp_ref, dgp, first)
        _accumulate(dg1_ref, dg1, first)
        _accumulate(dg2_ref, dg2, first)

    tok = pl.BlockSpec((tm, D_MODEL), lambda i: (i, 0))
    half = pl.BlockSpec((tm, 512), lambda i: (i, 0))
    vec = pl.BlockSpec((1, D_MODEL), lambda i: (0, 0))
    hvec = pl.BlockSpec((1, 512), lambda i: (0, 0))
    mat = pl.BlockSpec((D_MODEL, D_MODEL), lambda i: (0, 0))
    return pl.pallas_call(
        body, name="mix_out_bwd", grid=(t // tm,),
        in_specs=[tok, tok, half, half, half, hvec, hvec, vec, mat],
        out_specs=[half, half, mat, vec, hvec, hvec],
        out_shape=[jax.ShapeDtypeStruct((t, 512), F32), jax.ShapeDtypeStruct((t, 512), BF16),
                   jax.ShapeDtypeStruct((D_MODEL, D_MODEL), F32), jax.ShapeDtypeStruct((1, D_MODEL), F32),
                   jax.ShapeDtypeStruct((1, 512), F32), jax.ShapeDtypeStruct((1, 512), F32)],
        compiler_params=_params(1),
    )(dx2, m, h, gl, o, g_lru, g_attn, g_post, w_o)


def _mix_in_bwd(dx2, x1, g, dxl, dgl, dq, dkv, w_in):
    t = x1.shape[0]
    tm = _token_tile(t)

    def body(dx2_ref, x_ref, g_ref, dxl_ref, dgl_ref, dq_ref, dkv_ref, w_ref, dx1_ref, dw_ref, dg_ref):
        first = pl.program_id(0) == 0
        x = x_ref[...]
        nb = _rms_fwd(x, g_ref[...]).astype(BF16)
        lo = lax.broadcasted_iota(jnp.int32, (tm, 128), 1) < HEAD_DIM
        dkv = dkv_ref[...]
        folded = []
        for k in range(4):
            seg = dkv[:, 128 * k:128 * (k + 1)]
            folded.append(jnp.where(lo, seg + pltpu.roll(seg, HEAD_DIM, 1), 0.0).astype(BF16))
        dproj = jnp.concatenate([dxl_ref[...].astype(BF16), dgl_ref[...].astype(BF16), dq_ref[...]] + folded, axis=1)
        _accumulate(dw_ref, _dot_tn(nb, dproj), first)
        dx, dg = _rms_bwd(x, g_ref[...], _dot_nt(dproj, w_ref[...]))
        dx1_ref[...] = dx2_ref[...] + dx
        _accumulate(dg_ref, dg, first)

    tok = pl.BlockSpec((tm, D_MODEL), lambda i: (i, 0))
    half = pl.BlockSpec((tm, 512), lambda i: (i, 0))
    vec = pl.BlockSpec((1, D_MODEL), lambda i: (0, 0))
    mat = pl.BlockSpec((D_MODEL, D_IN_DUP), lambda i: (0, 0))
    return pl.pallas_call(
        body, name="mix_in_bwd", grid=(t // tm,),
        in_specs=[tok, tok, vec, half, half, half, half, mat], out_specs=[tok, mat, vec],
        out_shape=[jax.ShapeDtypeStruct((t, D_MODEL), F32), jax.ShapeDtypeStruct((D_MODEL, D_IN_DUP), F32),
                   jax.ShapeDtypeStruct((1, D_MODEL), F32)],
        compiler_params=_params(1),
    )(dx2, x1, g, dxl, dgl, dq, dkv, w_in)


def _row_tile(rows):
    return rows if rows <= 512 else 256


def _chip_sum(grad, from_sibling, core, name):
    _, rows, cols = grad.shape
    tr = _row_tile(rows)

    def body(core_ref, g_ref, s_ref, out_ref):
        out_ref[0] = (g_ref[0, 0] + s_ref[0]).astype(BF16)

    grid_spec = pltpu.PrefetchScalarGridSpec(
        num_scalar_prefetch=1, grid=(4, rows // tr),
        in_specs=[pl.BlockSpec((1, 1, tr, cols), lambda q, i, core: (q, core[0], i, 0)),
                  pl.BlockSpec((1, tr, cols), lambda q, i, core: (q, i, 0))],
        out_specs=pl.BlockSpec((1, tr, cols), lambda q, i, core: (q, i, 0)))
    return pl.pallas_call(
        body, name=name, grid_spec=grid_spec, out_shape=jax.ShapeDtypeStruct((4, rows, cols), BF16),
        compiler_params=_params(2),
    )(core, grad.reshape(4, 2, rows, cols), from_sibling)


def _adamw(w, g, m, v):
    m = ADAM_B1 * m + (1.0 - ADAM_B1) * g
    v = ADAM_B2 * v + (1.0 - ADAM_B2) * (g * g)
    m_hat = m / (1.0 - ADAM_B1 ** ADAM_STEP)
    v_hat = v / (1.0 - ADAM_B2 ** ADAM_STEP)
    delta = -ADAM_LR * (m_hat / (jnp.sqrt(v_hat) + ADAM_EPS) + ADAM_WD * w)
    return delta, m, v


def _shard_update(grad, from_sibling, from_chips, w, m, v, place, name):
    _, rows, cols = grad.shape
    tr = _row_tile(rows)

    def body(place_ref, g_ref, s_ref, c_ref, w_ref, m_ref, v_ref, go_ref, d_ref, mo_ref, vo_ref):
        g = g_ref[0, 0] + s_ref[0]
        g = g + c_ref[0].astype(F32)
        g = g + c_ref[1].astype(F32)
        g = g + c_ref[2].astype(F32)
        go_ref[...] = g
        d_ref[...], mo_ref[...], vo_ref[...] = _adamw(w_ref[...], g, m_ref[...], v_ref[...])

    flat = pl.BlockSpec((tr, cols), lambda i, place: (i, 0))
    grid_spec = pltpu.PrefetchScalarGridSpec(
        num_scalar_prefetch=1, grid=(rows // tr,),
        in_specs=[pl.BlockSpec((1, 1, tr, cols), lambda i, place: (place[0], place[1], i, 0)),
                  pl.BlockSpec((1, tr, cols), lambda i, place: (place[0], i, 0)),
                  pl.BlockSpec((3, tr, cols), lambda i, place: (0, i, 0)), flat, flat, flat],
        out_specs=[flat, flat, flat, flat])
    return pl.pallas_call(
        body, name=name, grid_spec=grid_spec, out_shape=[jax.ShapeDtypeStruct((rows, cols), F32)] * 4,
        compiler_params=_params(1),
    )(place, grad.reshape(4, 2, rows, cols), from_sibling, from_chips, w, m, v)


def _small_update(gathered, w, m, v):
    rows = w.shape[0]

    def body(g_ref, w_ref, m_ref, v_ref, go_ref, d_ref, mo_ref, vo_ref):
        g = g_ref[0]
        for d in range(1, N_DEV):
            g = g + g_ref[d]
        go_ref[...] = g
        d_ref[...], mo_ref[...], vo_ref[...] = _adamw(w_ref[...], g, m_ref[...], v_ref[...])

    return pl.pallas_call(
        body, name="small_update", out_shape=[jax.ShapeDtypeStruct((rows, 128), F32)] * 4,
        compiler_params=_params(),
    )(gathered, w, m, v)


SMALL_VECTORS = ("ffn1_pre_g", "ffn1_post_g", "mix_pre_g", "mix_post_g", "ffn2_pre_g", "ffn2_post_g",
                 "conv_b", "b_rg", "b_ig", "lru_lambda", "g_lru_out", "g_attn_out")


def _pad_rows(a, rows):
    return jnp.concatenate([a, jnp.zeros((rows - a.shape[0], a.shape[1]), a.dtype)], axis=0)


def _pack_small(vals, loss, conv_w_full):
    parts = [_pad_rows(jnp.pad(jnp.reshape(loss, (1, 1)), ((0, 0), (0, 127))), 8)]
    parts += [vals[k].reshape(-1, 128) for k in SMALL_VECTORS]
    parts += [vals["w_rg"].reshape(-1, 128), vals["w_ig"].reshape(-1, 128)]
    parts.append(_pad_rows(jnp.pad(vals["sinks"].reshape(1, 8), ((0, 0), (0, 120))), 8))
    parts.append(conv_w_full.reshape(16, 128))
    return jnp.concatenate(parts, axis=0)


def _unpack_small(blob, like):
    out = {}
    r = 8
    for k in SMALL_VECTORS:
        n = like[k].size // 128
        out[k] = blob[r:r + n].reshape(like[k].shape)
        r += n
    for k in ("w_rg", "w_ig"):
        out[k] = blob[r:r + 256].reshape(like[k].shape)
        r += 256
    out["sinks"] = blob[r:r + 1, 0:8].reshape(like["sinks"].shape)
    r += 8
    conv_w_full = blob[r:r + 16].reshape(4, D_LRU)
    return out, blob[0, 0], conv_w_full


def _dup_in_columns(w):
    k0, k1, v0, v1 = w[:, 1536:1600], w[:, 1600:1664], w[:, 1664:1728], w[:, 1728:1792]
    return jnp.concatenate([w[:, :1536], k0, k0, v0, v0, k1, k1, v1, v1], axis=1)


def _undup_in_columns(dw):
    return jnp.concatenate([dw[:, :1536], dw[:, 1536:1600], dw[:, 1792:1856], dw[:, 1664:1728], dw[:, 1920:1984]], axis=1)


def _pair_block_diag(w):
    w = w.reshape(N_LRU_GROUP, 2, 64, 64)
    z = jnp.zeros((N_LRU_GROUP, 64, 64), w.dtype)
    top = jnp.concatenate([w[:, 0], z], axis=2)
    bot = jnp.concatenate([z, w[:, 1]], axis=2)
    return jnp.concatenate([top, bot], axis=1)


def _pair_block_diag_grad(dw2, shape):
    return jnp.stack([dw2[:, :64, :64], dw2[:, 64:, 64:]], axis=1).reshape(shape)


def kernel(x, ffn1_pre_g, ffn1_w_gu, ffn1_w_down, ffn1_post_g, mix_pre_g, w_in, conv_w, conv_b, w_rg, b_rg, w_ig, b_ig, lru_lambda, sinks, g_lru_out, g_attn_out, w_o, mix_post_g, ffn2_pre_g, ffn2_w_gu, ffn2_w_down, ffn2_post_g, loss_target, m_ffn1_pre_g, m_ffn1_w_gu, m_ffn1_w_down, m_ffn1_post_g, m_mix_pre_g, m_w_in, m_conv_w, m_conv_b, m_w_rg, m_b_rg, m_w_ig, m_b_ig, m_lru_lambda, m_sinks, m_g_lru_out, m_g_attn_out, m_w_o, m_mix_post_g, m_ffn2_pre_g, m_ffn2_w_gu, m_ffn2_w_down, m_ffn2_post_g, v_ffn1_pre_g, v_ffn1_w_gu, v_ffn1_w_down, v_ffn1_post_g, v_mix_pre_g, v_w_in, v_conv_w, v_conv_b, v_w_rg, v_b_rg, v_w_ig, v_b_ig, v_lru_lambda, v_sinks, v_g_lru_out, v_g_attn_out, v_w_o, v_mix_post_g, v_ffn2_pre_g, v_ffn2_w_gu, v_ffn2_w_down, v_ffn2_post_g):
    args = dict(locals())
    names = ["ffn1_pre_g", "ffn1_w_gu", "ffn1_w_down", "ffn1_post_g", "mix_pre_g", "w_in", "conv_w", "conv_b", "w_rg",
             "b_rg", "w_ig", "b_ig", "lru_lambda", "sinks", "g_lru_out", "g_attn_out", "w_o", "mix_post_g",
             "ffn2_pre_g", "ffn2_w_gu", "ffn2_w_down", "ffn2_post_g"]
    big = ["ffn1_w_gu", "ffn1_w_down", "w_in", "w_o", "ffn2_w_gu", "ffn2_w_down"]
    w = {k: args[k] for k in names}
    mom = {k: args["m_" + k] for k in names}
    var = {k: args["v_" + k] for k in names}
    t = x.shape[1]
    xs = x.reshape(t, D_MODEL)
    target = loss_target.reshape(t, D_MODEL)
    cx, cy, cc = _coords()
    me = 4 * cx + 2 * cy + cc
    core = jnp.reshape(cc, (1,)).astype(jnp.int32)
    place = jnp.stack([2 * cx + cy, cc]).astype(jnp.int32)

    shard2d = {k: w[k].reshape(w[k].shape[1:]) for k in big}
    shard_bf = {k: shard2d[k].astype(BF16) for k in big}
    conv_pad = jnp.pad(conv_w.reshape(4, 64), ((0, 4), (0, 64)))
    (first_w,) = _run_exchanges([_Gather([shard_bf["ffn1_w_gu"], shard_bf["ffn1_w_down"]])], "all_gather_ffn1")
    wgu1 = first_w[0].reshape(2, N_CHUNK, D_MODEL, CHUNK)
    wd1 = first_w[1].reshape(N_CHUNK, CHUNK, D_MODEL)
    rest = _Gather([shard_bf["w_in"], shard_bf["w_o"], shard_bf["ffn2_w_gu"], shard_bf["ffn2_w_down"], conv_pad])

    x1, f1, gu1, gathered = _ffn_fwd(xs, ffn1_pre_g, wgu1, wd1, ffn1_post_g, None, "ffn1_fwd", rest)
    w_in_full = _dup_in_columns(jnp.transpose(gathered[0], (1, 0, 2)).reshape(D_MODEL, D_IN))
    w_o_full = gathered[1].reshape(D_MODEL, D_MODEL)
    wgu2 = gathered[2].reshape(2, N_CHUNK, D_MODEL, CHUNK)
    wd2 = gathered[3].reshape(N_CHUNK, CHUNK, D_MODEL)
    conv_w_full = jnp.transpose(gathered[4][:, 0:4, 0:64], (1, 0, 2)).reshape(4, D_LRU)
    p_lru = jnp.concatenate([conv_b, b_rg, b_ig, lru_lambda, conv_w_full], axis=0)
    wrg2 = _pair_block_diag(w_rg[0]).astype(BF16)
    wig2 = _pair_block_diag(w_ig[0]).astype(BF16)
    xl, gl, q, kv = _mix_in_fwd(x1, mix_pre_g, w_in_full)
    h = _lru_fwd(xl, p_lru, wrg2, wig2)
    o = _attn_fwd(q, kv, sinks)
    x2, mo = _mix_out_fwd(x1, h, gl, o, g_lru_out, g_attn_out, mix_post_g, w_o_full)
    dx3, f2, gu2, loss_parts, _ = _ffn_fwd(x2, ffn2_pre_g, wgu2, wd2, ffn2_post_g, target, "ffn2_fwd")
    loss_local = jnp.sum(loss_parts[::8, 0])

    g = {}
    partial, from_sibling, from_chips = {}, {}, {}

    def chip_sums(keys):
        return [_chip_sum(partial[k], from_sibling[k], core, "chip_sum_" + k) for k in keys]

    n2, df2, g["ffn2_post_g"], _ = _ffn_bwd_pre(dx3, x2, f2, ffn2_pre_g, ffn2_post_g, "ffn2_bwd_pre")
    dgu2, dwgu2, dwd2, _ = _ffn_bwd_w(n2, df2, gu2, wd2, "ffn2_bwd_w")
    partial["ffn2_w_gu"] = dwgu2.reshape(N_DEV, D_MODEL, CHUNK)
    partial["ffn2_w_down"] = dwd2.reshape(N_DEV, D_FF // N_DEV, D_MODEL)
    ffn2_keys = ["ffn2_w_gu", "ffn2_w_down"]
    dx2, g["ffn2_pre_g"], got = _ffn_bwd_x(dgu2, wgu2, x2, ffn2_pre_g, dx3, "ffn2_bwd_x",
                                           _SiblingExchange([partial[k] for k in ffn2_keys]))
    from_sibling.update(zip(ffn2_keys, got))
    dy, do, dwo, g["mix_post_g"], g["g_lru_out"], g["g_attn_out"] = _mix_out_bwd(
        dx2, mo, h, gl, o, g_lru_out, g_attn_out, mix_post_g, w_o_full)
    dq, dkv, dsink, got = _attn_bwd(q, kv, do, sinks, _ChipExchange(chip_sums(ffn2_keys)))
    from_chips.update(zip(ffn2_keys, got))
    dxl, dgl, dp, dwrg2, dwig2 = _lru_bwd(dy, h, xl, gl, p_lru, wrg2, wig2)
    dx1, dwin_dup, g["mix_pre_g"] = _mix_in_bwd(dx2, x1, mix_pre_g, dxl, dgl, dq, dkv, w_in_full)
    partial["w_in"] = jnp.transpose(_undup_in_columns(dwin_dup).reshape(D_MODEL, N_DEV, D_IN // N_DEV), (1, 0, 2))
    partial["w_o"] = dwo.reshape(N_DEV, D_MODEL // N_DEV, D_MODEL)
    mix_keys = ["w_in", "w_o"]
    n1, df1, g["ffn1_post_g"], got = _ffn_bwd_pre(dx1, xs, f1, ffn1_pre_g, ffn1_post_g, "ffn1_bwd_pre",
                                                  _SiblingExchange([partial[k] for k in mix_keys]))
    from_sibling.update(zip(mix_keys, got))
    dgu1, dwgu1, dwd1, got = _ffn_bwd_w(n1, df1, gu1, wd1, "ffn1_bwd_w", _ChipExchange(chip_sums(mix_keys)))
    from_chips.update(zip(mix_keys, got))
    partial["ffn1_w_gu"] = dwgu1.reshape(N_DEV, D_MODEL, CHUNK)
    partial["ffn1_w_down"] = dwd1.reshape(N_DEV, D_FF // N_DEV, D_MODEL)
    ffn1_keys = ["ffn1_w_gu", "ffn1_w_down"]
    (got,) = _run_exchanges([_SiblingExchange([partial[k] for k in ffn1_keys])], "ffn1_sibling_exchange")
    from_sibling.update(zip(ffn1_keys, got))
    dx0, g["ffn1_pre_g"], got = _ffn_bwd_x(dgu1, wgu1, xs, ffn1_pre_g, dx1, "ffn1_bwd_x",
                                           _ChipExchange(chip_sums(ffn1_keys)))
    from_chips.update(zip(ffn1_keys, got))
    g["conv_b"], g["b_rg"], g["b_ig"], g["lru_lambda"] = dp[0:1], dp[1:2], dp[2:3], dp[3:4]
    g["w_rg"] = _pair_block_diag_grad(dwrg2, w_rg.shape)
    g["w_ig"] = _pair_block_diag_grad(dwig2, w_ig.shape)
    g["sinks"] = dsink[:, 0].reshape(1, 8)

    col = me * 64

    def conv_rows(a):
        return lax.dynamic_update_slice(jnp.zeros((4, D_LRU), F32), a.reshape(4, 64), (0, col))

    zero = jnp.zeros((), F32)
    g_blob = _pack_small(g, loss_local, dp[4:8])
    ((all_blobs,),) = _run_exchanges([_Gather([g_blob])], "all_gather_small_grads")
    grads, delta, new_m, new_v = {}, {}, {}, {}
    for k in big:
        shape = w[k].shape
        res = _shard_update(partial[k], from_sibling[k], from_chips[k], shard2d[k], mom[k].reshape(shape[1:]),
                            var[k].reshape(shape[1:]), place, "update_" + k)
        grads[k], delta[k], new_m[k], new_v[k] = [r.reshape(shape) for r in res]

    res = _small_update(all_blobs, _pack_small(w, zero, conv_rows(conv_w)), _pack_small(mom, zero, conv_rows(m_conv_w)),
                        _pack_small(var, zero, conv_rows(v_conv_w)))
    loss = None
    for dst, blob in zip((grads, delta, new_m, new_v), res):
        small, first, conv_full = _unpack_small(blob, w)
        dst.update(small)
        dst["conv_w"] = lax.dynamic_slice(conv_full, (0, col), (4, 64)).reshape(conv_w.shape)
        if loss is None:
            loss = first
    return (loss, dx0.reshape(x.shape), *[grads[k] for k in names], *[delta[k] for k in names],
            *[new_m[k] for k in names], *[new_v[k] for k in names])
```

```python
import functools

import jax
import jax.numpy as jnp
from jax import lax
from jax.experimental import pallas as pl
from jax.experimental.pallas import tpu as pltpu

F32 = jnp.float32
BF16 = jnp.bfloat16

D_MODEL = 1024
D_FF = 2816
N_DEV = 8
N_CHUNK = 4
CHUNK = D_FF // N_CHUNK
D_LRU = 512
D_ATTN = 512
LRU_GROUP = 128
N_LRU_GROUP = D_LRU // LRU_GROUP
HEAD_DIM = 64
BLOCK_Q = 128
D_IN = 1792
D_IN_DUP = 2048
RMS_EPS = 1e-6
LRU_C = 8.0
MASK_VALUE = -1e30
ATTN_SCALE = HEAD_DIM ** -0.5

ADAM_LR = 0.001
ADAM_B1 = 0.9
ADAM_B2 = 0.999
ADAM_EPS = 1e-08
ADAM_WD = 0.01
ADAM_STEP = 10

VMEM_LIMIT_V7X = 56 * 2 ** 20

ANY = pl.BlockSpec(memory_space=pl.ANY)
SMEM = pl.BlockSpec(memory_space=pltpu.SMEM)
MESH = pl.DeviceIdType.MESH


def _params(n_grid=0):
    sem = ("arbitrary",) * n_grid if n_grid else None
    return pltpu.CompilerParams(dimension_semantics=sem, vmem_limit_bytes=VMEM_LIMIT_V7X)


def _dot(a, b):
    return lax.dot_general(a, b, (((1,), (0,)), ((), ())), preferred_element_type=F32)


def _dot_nt(a, b):
    return lax.dot_general(a, b, (((1,), (1,)), ((), ())), preferred_element_type=F32)


def _dot_tn(a, b):
    return lax.dot_general(a, b, (((0,), (0,)), ((), ())), preferred_element_type=F32)


def _sigmoid(x):
    return 1.0 / (1.0 + jnp.exp(-x))


def _rms_fwd(x, g):
    r = lax.rsqrt(jnp.mean(x * x, axis=-1, keepdims=True) + RMS_EPS)
    return x * r * g


def _rms_bwd(x, g, dy):
    r = lax.rsqrt(jnp.mean(x * x, axis=-1, keepdims=True) + RMS_EPS)
    xh = x * r
    dg = jnp.sum(dy * xh, axis=0, keepdims=True)
    dxh = dy * g
    dx = r * (dxh - xh * jnp.mean(dxh * xh, axis=-1, keepdims=True))
    return dx, dg


def _gelu(x):
    c = 0.7978845608028654
    inner = c * (x + 0.044715 * x * x * x)
    th = jnp.tanh(inner)
    ge = 0.5 * x * (1.0 + th)
    dge = 0.5 * (1.0 + th) + 0.5 * x * (1.0 - th * th) * c * (1.0 + 3.0 * 0.044715 * x * x)
    return ge, dge


def _zero_at_first(first, *refs):
    @pl.when(first)
    def _():
        for ref in refs:
            ref[...] = jnp.zeros_like(ref)


def _token_tile(t):
    return 512 if t >= 2048 else t // 2


def _ffn_bwd_tile(t):
    return 1024 if t >= 4096 else t // 2


def _coords():
    return lax.axis_index("x"), lax.axis_index("y"), lax.axis_index("c")


class _Gather:
    n_phases = 3
    at = (0.0, 0.7, 1.0)

    def __init__(self, shards):
        k = len(shards)
        self.arrays = list(shards)
        self.out_shape = [jax.ShapeDtypeStruct((N_DEV,) + s.shape, s.dtype) for s in shards]
        self.scratch = [pltpu.SemaphoreType.DMA((7 * k,)), pltpu.SemaphoreType.DMA((7 * k,)), pltpu.SemaphoreType.DMA((k,))]

    def run(self, phase, ins, outs, sems):
        send_sems, recv_sems, local_sems = sems
        k_arr = len(ins)
        x, y, c = _coords()
        me, sibling = (x, y, c), (x, y, 1 - c)
        chips = [(1 - x, y), (x, 1 - y), (1 - x, 1 - y)]

        def rows(k, dev):
            return outs[k].at[4 * dev[0] + 2 * dev[1] + dev[2]]

        def copy(k, slot, block, to, src=None):
            return pltpu.make_async_remote_copy(
                src_ref=rows(k, block) if src is None else src, dst_ref=rows(k, block),
                send_sem=send_sems.at[7 * k + slot], recv_sem=recv_sems.at[7 * k + slot],
                device_id=to, device_id_type=MESH)

        def mine():
            return [pltpu.make_async_copy(ins[k], rows(k, me), local_sems.at[k]) for k in range(k_arr)]

        def first():
            return [copy(k, slot, me, to, src=ins[k]) for k in range(k_arr)
                    for slot, to in enumerate([sibling] + [(*chip, c) for chip in chips])]

        def passed(j, k):
            return copy(k, 4 + j, (*chips[j], c), sibling)

        if phase == 0:
            for cp in mine() + first():
                cp.start()
        elif phase == 1:
            for j, chip in enumerate(chips):
                for k in range(k_arr):
                    copy(k, 1 + j, (*chip, c), me).wait_recv()
                    passed(j, k).start()
        else:
            for k in range(k_arr):
                copy(k, 0, sibling, me).wait_recv()
                for j, chip in enumerate(chips):
                    copy(k, 4 + j, (*chip, 1 - c), me).wait_recv()
            for cp in first() + [passed(j, k) for j in range(3) for k in range(k_arr)]:
                cp.wait_send()
            for cp in mine():
                cp.wait()


class _SiblingExchange:
    n_phases = 2
    at = (0.0, 1.0)

    def __init__(self, grads):
        k = len(grads)
        self.arrays = list(grads)
        self.out_shape = [jax.ShapeDtypeStruct((4,) + g.shape[1:], g.dtype) for g in grads]
        self.scratch = [pltpu.SemaphoreType.DMA((4 * k,)), pltpu.SemaphoreType.DMA((4 * k,))]

    def run(self, phase, ins, outs, sems):
        send_sems, recv_sems = sems
        x, y, c = _coords()
        copies = [pltpu.make_async_remote_copy(
            src_ref=ins[k].at[2 * q + (1 - c)], dst_ref=outs[k].at[q],
            send_sem=send_sems.at[4 * k + q], recv_sem=recv_sems.at[4 * k + q],
            device_id=(x, y, 1 - c), device_id_type=MESH) for k in range(len(ins)) for q in range(4)]
        for cp in copies:
            if phase == 0:
                cp.start()
            else:
                cp.wait_recv()
                cp.wait_send()


class _ChipExchange:
    n_phases = 2
    at = (0.0, 1.0)

    def __init__(self, chip_sums):
        k = len(chip_sums)
        self.arrays = list(chip_sums)
        self.out_shape = [jax.ShapeDtypeStruct((3,) + s.shape[1:], s.dtype) for s in chip_sums]
        self.scratch = [pltpu.SemaphoreType.DMA((3 * k,)), pltpu.SemaphoreType.DMA((3 * k,))]

    def run(self, phase, ins, outs, sems):
        send_sems, recv_sems = sems
        x, y, c = _coords()
        chips = [(1 - x, y), (x, 1 - y), (1 - x, 1 - y)]
        copies = [pltpu.make_async_remote_copy(
            src_ref=ins[k].at[2 * chip[0] + chip[1]], dst_ref=outs[k].at[j],
            send_sem=send_sems.at[3 * k + j], recv_sem=recv_sems.at[3 * k + j],
            device_id=(*chip, c), device_id_type=MESH) for k in range(len(ins)) for j, chip in enumerate(chips)]
        for cp in copies:
            if phase == 0:
                cp.start()
            else:
                cp.wait_recv()
                cp.wait_send()


class _Host:
    def __init__(self, exchange):
        self.ex = exchange
        self.args = [] if exchange is None else exchange.arrays
        self.in_specs = [ANY] * len(self.args)
        self.out_shape = [] if exchange is None else exchange.out_shape
        self.out_specs = [ANY] * len(self.out_shape)
        self.scratch = [] if exchange is None else exchange.scratch

    def split(self, refs, n_in, n_out, n_scratch):
        a, b, s = len(self.args), len(self.out_shape), len(self.scratch)
        own_in, ex_in = refs[:n_in], refs[n_in:n_in + a]
        rest = refs[n_in + a:]
        own_out, ex_out = rest[:n_out], rest[n_out:n_out + b]
        rest = rest[n_out + b:]
        own_scratch, ex_sems = rest[:n_scratch], rest[n_scratch:n_scratch + s]
        return list(own_in) + list(own_out) + list(own_scratch), (ex_in, ex_out, ex_sems)

    def at_steps(self, step, n_steps, ex_refs):
        if self.ex is None:
            return
        for p in range(self.ex.n_phases):
            pl.when(step == int(round(self.ex.at[p] * (n_steps - 1))))(functools.partial(self.ex.run, p, *ex_refs))

    def phase(self, p, ex_refs):
        if self.ex is not None:
            self.ex.run(p, *ex_refs)


def _run_exchanges(exchanges, name):
    hosts = [_Host(ex) for ex in exchanges]
    n_in = [len(h.args) for h in hosts]
    n_out = [len(h.out_shape) for h in hosts]
    n_sc = [len(h.scratch) for h in hosts]

    def body(*refs):
        ins, outs, scr = refs[:sum(n_in)], refs[sum(n_in):sum(n_in) + sum(n_out)], refs[sum(n_in) + sum(n_out):]
        parts = []
        for e in range(len(hosts)):
            parts.append((ins[sum(n_in[:e]):sum(n_in[:e + 1])], outs[sum(n_out[:e]):sum(n_out[:e + 1])],
                          scr[sum(n_sc[:e]):sum(n_sc[:e + 1])]))
        for h, part in zip(hosts, parts):
            h.phase(0, part)
        for h, part in zip(hosts, parts):
            for p in range(1, h.ex.n_phases):
                h.phase(p, part)

    res = pl.pallas_call(
        body, name=name, in_specs=[ANY] * sum(n_in), out_specs=[ANY] * sum(n_out),
        out_shape=[s for h in hosts for s in h.out_shape], scratch_shapes=[s for h in hosts for s in h.scratch],
    )(*[a for h in hosts for a in h.args])
    return [res[sum(n_out[:e]):sum(n_out[:e + 1])] for e in range(len(hosts))]


def _ffn_fwd(x, g_pre, wgu, wd, g_post, target, name, exchange=None):
    t = x.shape[0]
    tm = _token_tile(t)
    n_i = t // tm
    with_loss = target is not None
    host = _Host(exchange)
    n_in, n_out = (6, 4) if with_loss else (5, 3)

    def body(*refs):
        own, ex_refs = host.split(refs, n_in, n_out, 0)
        if with_loss:
            x_ref, gpre_ref, wgu_ref, wd_ref, gpost_ref, tgt_ref, xo_ref, f_ref, gu_ref, loss_ref = own
        else:
            x_ref, gpre_ref, wgu_ref, wd_ref, gpost_ref, xo_ref, f_ref, gu_ref = own
        host.at_steps(pl.program_id(0), n_i, ex_refs)
        x = x_ref[...]
        n = _rms_fwd(x, gpre_ref[...]).astype(BF16)
        f = None
        for j in range(N_CHUNK):
            gate = _dot(n, wgu_ref[0, j])
            up = _dot(n, wgu_ref[1, j])
            gu_ref[0, j] = gate.astype(BF16)
            gu_ref[1, j] = up.astype(BF16)
            part = _dot((gate * _sigmoid(gate) * up).astype(BF16), wd_ref[j])
            f = part if f is None else f + part
        f_ref[...] = f
        xo = x + 0.5 * _rms_fwd(f, gpost_ref[...])
        if with_loss:
            err = xo - tgt_ref[...]
            xo_ref[...] = err * (1.0 / D_MODEL)
            part = 0.5 * jnp.sum(jnp.sum(err * err, axis=-1, keepdims=True) * (1.0 / D_MODEL), axis=0, keepdims=True)
            loss_ref[...] = jnp.broadcast_to(part, loss_ref.shape)
        else:
            xo_ref[...] = xo

    tok = pl.BlockSpec((tm, D_MODEL), lambda i: (i, 0))
    vec = pl.BlockSpec((1, D_MODEL), lambda i: (0, 0))
    in_specs = [tok, vec,
                pl.BlockSpec((2, N_CHUNK, D_MODEL, CHUNK), lambda i: (0, 0, 0, 0), pipeline_mode=pl.Buffered(1)),
                pl.BlockSpec((N_CHUNK, CHUNK, D_MODEL), lambda i: (0, 0, 0), pipeline_mode=pl.Buffered(1)),
                vec]
    out_shape = [jax.ShapeDtypeStruct((t, D_MODEL), F32), jax.ShapeDtypeStruct((t, D_MODEL), F32),
                 jax.ShapeDtypeStruct((2, N_CHUNK, t, CHUNK), BF16)]
    out_specs = [tok, tok, pl.BlockSpec((2, N_CHUNK, tm, CHUNK), lambda i: (0, 0, i, 0))]
    args = [x, g_pre, wgu, wd, g_post]
    if with_loss:
        in_specs.append(tok)
        args.append(target)
        out_shape.append(jax.ShapeDtypeStruct((n_i * 8, 128), F32))
        out_specs.append(pl.BlockSpec((8, 128), lambda i: (i, 0)))
    res = pl.pallas_call(
        body, name=name, grid=(n_i,), in_specs=in_specs + host.in_specs, out_specs=out_specs + host.out_specs,
        out_shape=out_shape + host.out_shape, scratch_shapes=host.scratch, compiler_params=_params(1),
    )(*args, *host.args)
    return (*res[:n_out], list(res[n_out:]))


def _ffn_bwd_pre(d_out, x, f, g_pre, g_post, name, exchange=None):
    t = x.shape[0]
    tm = _token_tile(t)
    host = _Host(exchange)

    def body(*refs):
        (do_ref, x_ref, f_ref, gpre_ref, gpost_ref, n_ref, df_ref, dgpost_ref), ex_refs = host.split(refs, 5, 3, 0)
        i = pl.program_id(0)
        host.at_steps(i, t // tm, ex_refs)
        _zero_at_first(i == 0, dgpost_ref)
        n_ref[...] = _rms_fwd(x_ref[...], gpre_ref[...]).astype(BF16)
        df, dg = _rms_bwd(f_ref[...], gpost_ref[...], 0.5 * do_ref[...])
        df_ref[...] = df.astype(BF16)
        dgpost_ref[...] += dg

    tok = pl.BlockSpec((tm, D_MODEL), lambda i: (i, 0))
    vec = pl.BlockSpec((1, D_MODEL), lambda i: (0, 0))
    res = pl.pallas_call(
        body, name=name, grid=(t // tm,), in_specs=[tok, tok, tok, vec, vec] + host.in_specs,
        out_specs=[tok, tok, vec] + host.out_specs,
        out_shape=[jax.ShapeDtypeStruct((t, D_MODEL), BF16), jax.ShapeDtypeStruct((t, D_MODEL), BF16),
                   jax.ShapeDtypeStruct((1, D_MODEL), F32)] + host.out_shape,
        scratch_shapes=host.scratch, compiler_params=_params(1),
    )(d_out, x, f, g_pre, g_post, *host.args)
    return (*res[:3], list(res[3:]))


def _ffn_bwd_w(n, df, gu, wd, name, exchange=None):
    t = n.shape[0]
    tm = _ffn_bwd_tile(t)
    n_i = t // tm
    host = _Host(exchange)

    def body(*refs):
        (n_ref, df_ref, gu_ref, wd_ref, dgu_ref, dwgu_ref, dwd_ref), ex_refs = host.split(refs, 4, 3, 0)
        i = pl.program_id(1)
        host.at_steps(pl.program_id(0) * n_i + i, N_CHUNK * n_i, ex_refs)
        _zero_at_first(i == 0, dwgu_ref, dwd_ref)
        nb = n_ref[...]
        dfb = df_ref[...]
        gate = gu_ref[0, 0].astype(F32)
        up = gu_ref[1, 0].astype(F32)
        s = _sigmoid(gate)
        silu = gate * s
        a = (silu * up).astype(BF16)
        da = _dot_nt(dfb, wd_ref[0])
        dup = (da * silu).astype(BF16)
        dgate = (da * up * (s * (1.0 + gate * (1.0 - s)))).astype(BF16)
        dgu_ref[0, 0] = dgate
        dgu_ref[1, 0] = dup
        dwgu_ref[0, 0] += _dot_tn(nb, dgate)
        dwgu_ref[1, 0] += _dot_tn(nb, dup)
        dwd_ref[0] += _dot_tn(a, dfb)

    tok = pl.BlockSpec((tm, D_MODEL), lambda j, i: (i, 0))
    act = pl.BlockSpec((2, 1, tm, CHUNK), lambda j, i: (0, j, i, 0))
    wgu_spec = pl.BlockSpec((2, 1, D_MODEL, CHUNK), lambda j, i: (0, j, 0, 0), pipeline_mode=pl.Buffered(1))
    wd_spec = pl.BlockSpec((1, CHUNK, D_MODEL), lambda j, i: (j, 0, 0), pipeline_mode=pl.Buffered(1))
    res = pl.pallas_call(
        body, name=name, grid=(N_CHUNK, n_i),
        in_specs=[tok, tok, act, wd_spec] + host.in_specs,
        out_specs=[act, wgu_spec, wd_spec] + host.out_specs,
        out_shape=[jax.ShapeDtypeStruct((2, N_CHUNK, t, CHUNK), BF16),
                   jax.ShapeDtypeStruct((2, N_CHUNK, D_MODEL, CHUNK), F32),
                   jax.ShapeDtypeStruct((N_CHUNK, CHUNK, D_MODEL), F32)] + host.out_shape,
        scratch_shapes=host.scratch, compiler_params=_params(2),
    )(n, df, gu, wd, *host.args)
    return (*res[:3], list(res[3:]))


def _ffn_bwd_x(dgu, wgu, x, g_pre, d_out, name, exchange=None):
    t = x.shape[0]
    tm = _token_tile(t)
    n_i = t // tm
    host = _Host(exchange)

    def body(*refs):
        (dgu_ref, wgu_ref, x_ref, gpre_ref, do_ref, dx_ref, dgpre_ref), ex_refs = host.split(refs, 5, 2, 0)
        i = pl.program_id(0)
        host.at_steps(i, n_i, ex_refs)
        _zero_at_first(i == 0, dgpre_ref)
        dn = _dot_nt(dgu_ref[0, 0], wgu_ref[0, 0]) + _dot_nt(dgu_ref[1, 0], wgu_ref[1, 0])
        for j in range(1, N_CHUNK):
            dn = dn + _dot_nt(dgu_ref[0, j], wgu_ref[0, j]) + _dot_nt(dgu_ref[1, j], wgu_ref[1, j])
        dx, dg = _rms_bwd(x_ref[...], gpre_ref[...], dn)
        dx_ref[...] = do_ref[...] + dx
        dgpre_ref[...] += dg

    tok = pl.BlockSpec((tm, D_MODEL), lambda i: (i, 0))
    vec = pl.BlockSpec((1, D_MODEL), lambda i: (0, 0))
    res = pl.pallas_call(
        body, name=name, grid=(n_i,),
        in_specs=[pl.BlockSpec((2, N_CHUNK, tm, CHUNK), lambda i: (0, 0, i, 0)),
                  pl.BlockSpec((2, N_CHUNK, D_MODEL, CHUNK), lambda i: (0, 0, 0, 0), pipeline_mode=pl.Buffered(1)),
                  tok, vec, tok] + host.in_specs,
        out_specs=[tok, vec] + host.out_specs,
        out_shape=[jax.ShapeDtypeStruct((t, D_MODEL), F32), jax.ShapeDtypeStruct((1, D_MODEL), F32)] + host.out_shape,
        scratch_shapes=host.scratch, compiler_params=_params(1),
    )(dgu, wgu, x, g_pre, d_out, *host.args)
    return (*res[:2], list(res[2:]))


def _mix_in_fwd(x1, g, w_in):
    t = x1.shape[0]
    tm = _token_tile(t)

    def body(x_ref, g_ref, w_ref, xl_ref, gl_ref, q_ref, kv_ref):
        n = _rms_fwd(x_ref[...], g_ref[...]).astype(BF16)
        proj = _dot(n, w_ref[...])
        xl_ref[...] = proj[:, 0:512]
        gl_ref[...] = proj[:, 512:1024]
        q_ref[...] = proj[:, 1024:1536].astype(BF16)
        kv_ref[...] = proj[:, 1536:2048].astype(BF16)

    tok = pl.BlockSpec((tm, D_MODEL), lambda i: (i, 0))
    half = pl.BlockSpec((tm, 512), lambda i: (i, 0))
    return pl.pallas_call(
        body, name="mix_in_fwd", grid=(t // tm,),
        in_specs=[tok, pl.BlockSpec((1, D_MODEL), lambda i: (0, 0)), pl.BlockSpec((D_MODEL, D_IN_DUP), lambda i: (0, 0))],
        out_specs=[half, half, half, half],
        out_shape=[jax.ShapeDtypeStruct((t, 512), F32), jax.ShapeDtypeStruct((t, 512), F32),
                   jax.ShapeDtypeStruct((t, 512), BF16), jax.ShapeDtypeStruct((t, 512), BF16)],
        compiler_params=_params(1),
    )(x1, g, w_in)


def _shift_down(x, before, s):
    if s == 0:
        return x
    rolled = pltpu.roll(x, s, 0)
    ext = jnp.concatenate([before, x[0:8]], axis=0)
    first8 = pltpu.roll(ext, s, 0)[8:16]
    return jnp.concatenate([first8, rolled[8:]], axis=0)


def _shift_up(x, after, s):
    if s == 0:
        return x
    rows = x.shape[0]
    rolled = pltpu.roll(x, rows - s, 0)
    ext = jnp.concatenate([x[rows - 8:rows], after], axis=0)
    last8 = pltpu.roll(ext, 16 - s, 0)[0:8]
    return jnp.concatenate([rolled[:rows - 8], last8], axis=0)


def _log_sigmoid(x):
    e = jnp.exp(-jnp.abs(x))
    log1p_e = jnp.where(e < 0.01, e * (1.0 - e * (0.5 - e * (1.0 / 3.0))), jnp.log(1.0 + e))
    return jnp.minimum(x, 0.0) - log1p_e


def _lru_gates(xc, p_ref, wrg, wig):
    xcb = xc.astype(BF16)
    r = _sigmoid(_dot(xcb, wrg) + p_ref[1:2, :])
    ig = _sigmoid(_dot(xcb, wig) + p_ref[2:3, :])
    ls = _log_sigmoid(p_ref[3:4, :])
    log_a = LRU_C * r * ls
    a = jnp.exp(log_a)
    z = 2.0 * log_a
    series = z * (1.0 + z * (0.5 + z * (1.0 / 6.0 + z * (1.0 / 24.0 + z * (1.0 / 120.0 + z * (1.0 / 720.0))))))
    expm1 = jnp.where(z > -0.1, series, jnp.exp(z) - 1.0)
    mult = jnp.sqrt(-expm1)
    return xcb, r, ig, ls, a, mult


def _conv_taps(x, before, p_ref):
    xc = x * p_ref[7:8, :]
    for s in (1, 2, 3):
        xc = xc + _shift_down(x, before, s) * p_ref[7 - s:8 - s, :]
    return xc + p_ref[0:1, :]


def _lru_block_rows(t):
    return 512 if t >= 1024 else t // 2


def _lru_fwd(xl, p, wrg2, wig2):
    t = xl.shape[0]
    tb = _lru_block_rows(t)

    def body(xl_ref, p_ref, wrg_ref, wig_ref, h_ref, x_tail, h_carry):
        tt = pl.program_id(1)

        @pl.when(tt == 0)
        def _():
            x_tail[...] = jnp.zeros_like(x_tail)
            h_carry[...] = jnp.zeros_like(h_carry)

        x = xl_ref[...]
        xc = _conv_taps(x, x_tail[...], p_ref)
        x_tail[...] = x[tb - 8:tb]
        _, r, ig, ls, a, mult = _lru_gates(xc, p_ref, wrg_ref[0], wig_ref[0])
        u = mult * ig * xc
        row = lax.broadcasted_iota(jnp.int32, (tb, LRU_GROUP), 0)
        s = 1
        while s < tb:
            keep = row >= s
            u = jnp.where(keep, a * pltpu.roll(u, s, 0) + u, u)
            a = jnp.where(keep, a * pltpu.roll(a, s, 0), a)
            s *= 2
        h = u + a * h_carry[0:1, :]
        h_ref[...] = h
        h_carry[...] = jnp.broadcast_to(h[tb - 1:tb], h_carry.shape)

    blk = pl.BlockSpec((tb, LRU_GROUP), lambda g, tt: (tt, g))
    par = pl.BlockSpec((8, LRU_GROUP), lambda g, tt: (0, g))
    wsp = pl.BlockSpec((1, LRU_GROUP, LRU_GROUP), lambda g, tt: (g, 0, 0))
    return pl.pallas_call(
        body, name="lru_fwd", grid=(N_LRU_GROUP, t // tb), in_specs=[blk, par, wsp, wsp], out_specs=blk,
        out_shape=jax.ShapeDtypeStruct((t, D_LRU), F32),
        scratch_shapes=[pltpu.VMEM((8, LRU_GROUP), F32), pltpu.VMEM((8, LRU_GROUP), F32)],
        compiler_params=_params(2),
    )(xl, p, wrg2, wig2)


def _lru_bwd(dy, h, xl, gl, p, wrg2, wig2):
    t = xl.shape[0]
    tb = _lru_block_rows(t)
    n_tb = t // tb
    tb8 = tb // 8

    def body(dy_ref, h_ref, hprev_ref, xl_ref, xprev_ref, gl_ref, p_ref, wrg_ref, wig_ref,
             dxl_ref, dgl_ref, dp_ref, dwrg_ref, dwig_ref, g_carry, a_carry, dxc_head):
        step = pl.program_id(1)
        tt = n_tb - 1 - step
        first = step == 0

        _zero_at_first(first, g_carry, a_carry, dxc_head, dp_ref, dwrg_ref, dwig_ref)

        has_prev = (tt > 0).astype(F32)
        x = xl_ref[...]
        x_before = xprev_ref[...] * has_prev
        xs = [_shift_down(x, x_before, s) for s in range(4)]
        xc = xs[0] * p_ref[7:8, :] + xs[1] * p_ref[6:7, :] + xs[2] * p_ref[5:6, :] + xs[3] * p_ref[4:5, :] + p_ref[0:1, :]
        wrg = wrg_ref[0]
        wig = wig_ref[0]
        xcb, r, ig, ls, a, mult = _lru_gates(xc, p_ref, wrg, wig)

        hh = h_ref[...]
        h_m1 = _shift_down(hh, hprev_ref[...] * has_prev, 1)
        ge, dge = _gelu(gl_ref[...])
        dy = dy_ref[...]
        dgl_ref[...] = dy * hh * dge
        dh = dy * ge

        b = _shift_up(a, a_carry[...], 1)
        row = lax.broadcasted_iota(jnp.int32, (tb, LRU_GROUP), 0)
        g = dh
        s = 1
        while s < tb:
            keep = row < tb - s
            g = jnp.where(keep, b * pltpu.roll(g, tb - s, 0) + g, g)
            b = jnp.where(keep, b * pltpu.roll(b, tb - s, 0), b)
            s *= 2
        g = g + b * g_carry[0:1, :]
        g_carry[...] = jnp.broadcast_to(g[0:1], g_carry.shape)
        a_carry[...] = jnp.broadcast_to(a[0:1], a_carry.shape)

        da = g * h_m1
        dmult = g * ig * xc
        dig = g * mult * xc
        dxc = g * mult * ig
        dlog_a = da * a - dmult * (a * a) / mult
        dr = dlog_a * (LRU_C * ls)
        dls = jnp.sum(dlog_a * (LRU_C * r), axis=0, keepdims=True)
        dlam = dls * _sigmoid(-p_ref[3:4, :])
        dpre_r = dr * r * (1.0 - r)
        dpre_i = dig * ig * (1.0 - ig)
        dprb = dpre_r.astype(BF16)
        dpib = dpre_i.astype(BF16)
        dxc = dxc + _dot_nt(dprb, wrg) + _dot_nt(dpib, wig)
        dwrg_ref[0] += _dot_tn(xcb, dprb)
        dwig_ref[0] += _dot_tn(xcb, dpib)

        after = dxc_head[...]
        dxl = dxc * p_ref[7:8, :]
        for s in (1, 2, 3):
            dxl = dxl + _shift_up(dxc, after, s) * p_ref[7 - s:8 - s, :]
        dxl_ref[...] = dxl
        dxc_head[...] = dxc[0:8]

        rows = [jnp.sum(dxc, axis=0, keepdims=True), jnp.sum(dpre_r, axis=0, keepdims=True),
                jnp.sum(dpre_i, axis=0, keepdims=True), dlam]
        rows += [jnp.sum(dxc * xs[3 - k], axis=0, keepdims=True) for k in range(4)]
        dp_ref[...] += jnp.concatenate(rows, axis=0)

    blk = pl.BlockSpec((tb, LRU_GROUP), lambda g, s: (n_tb - 1 - s, g))
    prev8 = pl.BlockSpec((8, LRU_GROUP), lambda g, s: (jnp.maximum((n_tb - 1 - s) * tb8 - 1, 0), g))
    par = pl.BlockSpec((8, LRU_GROUP), lambda g, s: (0, g))
    wsp = pl.BlockSpec((1, LRU_GROUP, LRU_GROUP), lambda g, s: (g, 0, 0))
    return pl.pallas_call(
        body, name="lru_bwd", grid=(N_LRU_GROUP, n_tb),
        in_specs=[blk, blk, prev8, blk, prev8, blk, par, wsp, wsp], out_specs=[blk, blk, par, wsp, wsp],
        out_shape=[jax.ShapeDtypeStruct((t, D_LRU), F32), jax.ShapeDtypeStruct((t, D_LRU), F32),
                   jax.ShapeDtypeStruct((8, D_LRU), F32),
                   jax.ShapeDtypeStruct((N_LRU_GROUP, LRU_GROUP, LRU_GROUP), F32),
                   jax.ShapeDtypeStruct((N_LRU_GROUP, LRU_GROUP, LRU_GROUP), F32)],
        scratch_shapes=[pltpu.VMEM((8, LRU_GROUP), F32)] * 3,
        compiler_params=_params(2),
    )(dy, h, h, xl, xl, gl, p, wrg2, wig2)


def _attn_bias(first_block):
    qi = jnp.bitwise_and(lax.broadcasted_iota(jnp.int32, (4 * BLOCK_Q, 2 * BLOCK_Q), 0), BLOCK_Q - 1)
    kj = lax.broadcasted_iota(jnp.int32, (4 * BLOCK_Q, 2 * BLOCK_Q), 1)
    rel = qi + BLOCK_Q - kj
    mask = (rel >= 0) & (rel < BLOCK_Q)
    if first_block:
        mask = mask & (kj >= BLOCK_Q)
    return jnp.where(mask, 0.0, MASK_VALUE)


def _sink_column(sinks):
    hrow = lax.broadcasted_iota(jnp.int32, (4 * BLOCK_Q, 1), 0)
    return jnp.where(hrow < BLOCK_Q, sinks[0],
                     jnp.where(hrow < 2 * BLOCK_Q, sinks[1], jnp.where(hrow < 3 * BLOCK_Q, sinks[2], sinks[3])))


def _attn_scores(qv, kvv, n, bias, sk, lo):
    r0 = pl.multiple_of(n * BLOCK_Q, BLOCK_Q)
    rp = pl.multiple_of(jnp.maximum(n - 1, 0) * BLOCK_Q, BLOCK_Q)
    kvb = jnp.concatenate([kvv[pl.ds(rp, BLOCK_Q), :], kvv[pl.ds(r0, BLOCK_Q), :]], axis=0)
    k2 = kvb[:, 0:128]
    v2 = kvb[:, 128:256]
    qs = _stack_heads(qv[pl.ds(r0, BLOCK_Q), :], lo)
    s = _dot_nt(qs, k2) * ATTN_SCALE + bias
    m = jnp.maximum(jnp.max(s, axis=-1, keepdims=True), sk)
    e = jnp.exp(s - m)
    es = jnp.exp(sk - m)
    inv = 1.0 / (jnp.sum(e, axis=-1, keepdims=True) + es)
    return r0, rp, qs, k2, v2, e * inv, es * inv


def _stack_heads(pair2, lo):
    p0 = pair2[:, 0:128]
    p1 = pair2[:, 128:256]
    z = jnp.zeros_like(p0)
    return jnp.concatenate([jnp.where(lo, p0, z), jnp.where(lo, z, p0), jnp.where(lo, p1, z), jnp.where(lo, z, p1)], axis=0)


def _unstack_heads(st, lo):
    b = BLOCK_Q
    return jnp.concatenate([jnp.where(lo, st[0:b], st[b:2 * b]), jnp.where(lo, st[2 * b:3 * b], st[3 * b:4 * b])], axis=1)


def _attn_fwd(q, kv, sinks):
    t = q.shape[0]
    n_blk = t // BLOCK_Q

    def body(q_hbm, kv_hbm, s_ref, o_hbm, qv, kvv, ov, bias0, bias, sem):
        lo = lax.broadcasted_iota(jnp.int32, (BLOCK_Q, 128), 1) < HEAD_DIM
        bias0[...] = _attn_bias(True)
        bias[...] = _attn_bias(False)
        for g in range(2):
            cols = pl.ds(256 * g, 256)
            loads = [pltpu.make_async_copy(q_hbm.at[:, cols], qv, sem.at[0]),
                     pltpu.make_async_copy(kv_hbm.at[:, cols], kvv, sem.at[1])]
            for cp in loads:
                cp.start()
            for cp in loads:
                cp.wait()
            sk = _sink_column([s_ref[0, 4 * g + i] for i in range(4)])

            def block(n, bias_ref):
                r0, _, _, _, v2, prob, _ = _attn_scores(qv, kvv, n, bias_ref[...], sk, lo)
                ov[pl.ds(r0, BLOCK_Q), :] = _unstack_heads(_dot(prob.astype(BF16), v2), lo)

            block(0, bias0)

            def later(n, carry):
                block(n, bias)
                return carry

            lax.fori_loop(1, n_blk, later, 0, unroll=2)
            store = pltpu.make_async_copy(ov, o_hbm.at[:, cols], sem.at[2])
            store.start()
            store.wait()

    return pl.pallas_call(
        body, name="attn_fwd", in_specs=[ANY, ANY, SMEM], out_specs=ANY,
        out_shape=jax.ShapeDtypeStruct((t, D_ATTN), F32),
        scratch_shapes=[pltpu.VMEM((t, 256), BF16), pltpu.VMEM((t, 256), BF16), pltpu.VMEM((t, 256), F32),
                        pltpu.VMEM((4 * BLOCK_Q, 2 * BLOCK_Q), F32), pltpu.VMEM((4 * BLOCK_Q, 2 * BLOCK_Q), F32),
                        pltpu.SemaphoreType.DMA((3,))],
        compiler_params=_params(),
    )(q, kv, sinks)


def _attn_bwd(q, kv, do, sinks, exchange=None):
    t = q.shape[0]
    n_blk = t // BLOCK_Q
    host = _Host(exchange)

    def body(*refs):
        own, ex_refs = host.split(refs, 4, 3, 9)
        q_hbm, kv_hbm, do_hbm, s_ref, dq_hbm, dkv_hbm, dsink_ref, qv, kvv, dov, dqv, dkvv, ds_acc, bias0, bias, sem = own
        host.phase(0, ex_refs)
        lo = lax.broadcasted_iota(jnp.int32, (BLOCK_Q, 128), 1) < HEAD_DIM
        bias0[...] = _attn_bias(True)
        bias[...] = _attn_bias(False)
        for g in range(2):
            cols = pl.ds(256 * g, 256)
            loads = [pltpu.make_async_copy(q_hbm.at[:, cols], qv, sem.at[0]),
                     pltpu.make_async_copy(kv_hbm.at[:, cols], kvv, sem.at[1]),
                     pltpu.make_async_copy(do_hbm.at[:, cols], dov, sem.at[2])]
            for cp in loads:
                cp.start()
            for cp in loads:
                cp.wait()
            sk = _sink_column([s_ref[0, 4 * g + i] for i in range(4)])
            ds_acc[...] = jnp.zeros_like(ds_acc)

            def block(n, bias_ref, has_prev):
                r0, rp, qs, k2, v2, prob, psink = _attn_scores(qv, kvv, n, bias_ref[...], sk, lo)
                pb = prob.astype(BF16)
                dos = _stack_heads(dov[pl.ds(r0, BLOCK_Q), :], lo)
                dp = _dot_nt(dos, v2)
                dsum = jnp.sum(prob * dp, axis=-1, keepdims=True)
                dsb = (prob * (dp - dsum) * ATTN_SCALE).astype(BF16)
                ds_acc[...] -= psink * dsum
                dqv[pl.ds(r0, BLOCK_Q), :] = _unstack_heads(_dot(dsb, k2), lo).astype(BF16)
                dk2 = _dot_tn(dsb, qs)
                dv2 = _dot_tn(pb, dos)
                dkvv[pl.ds(r0, BLOCK_Q), :] = jnp.concatenate([dk2[BLOCK_Q:], dv2[BLOCK_Q:]], axis=1)
                if has_prev:
                    dkvv[pl.ds(rp, BLOCK_Q), :] += jnp.concatenate([dk2[:BLOCK_Q], dv2[:BLOCK_Q]], axis=1)

            block(0, bias0, False)

            def later(n, carry):
                block(n, bias, True)
                return carry

            lax.fori_loop(1, n_blk, later, 0, unroll=2)
            for i in range(4):
                tot = jnp.sum(ds_acc[BLOCK_Q * i:BLOCK_Q * (i + 1), :], axis=0, keepdims=True)
                dsink_ref[4 * g + i:4 * g + i + 1, :] = jnp.broadcast_to(tot, (1, 128))
            stores = [pltpu.make_async_copy(dqv, dq_hbm.at[:, cols], sem.at[0]),
                      pltpu.make_async_copy(dkvv, dkv_hbm.at[:, cols], sem.at[1])]
            for cp in stores:
                cp.start()
            for cp in stores:
                cp.wait()
        if exchange is not None:
            for p in range(1, exchange.n_phases):
                host.phase(p, ex_refs)

    res = pl.pallas_call(
        body, name="attn_bwd", in_specs=[ANY, ANY, ANY, SMEM] + host.in_specs,
        out_specs=[ANY, ANY, pl.BlockSpec(memory_space=pltpu.VMEM)] + host.out_specs,
        out_shape=[jax.ShapeDtypeStruct((t, D_ATTN), BF16), jax.ShapeDtypeStruct((t, 512), F32),
                   jax.ShapeDtypeStruct((8, 128), F32)] + host.out_shape,
        scratch_shapes=[pltpu.VMEM((t, 256), BF16), pltpu.VMEM((t, 256), BF16), pltpu.VMEM((t, 256), BF16),
                        pltpu.VMEM((t, 256), BF16), pltpu.VMEM((t, 256), F32), pltpu.VMEM((4 * BLOCK_Q, 1), F32),
                        pltpu.VMEM((4 * BLOCK_Q, 2 * BLOCK_Q), F32), pltpu.VMEM((4 * BLOCK_Q, 2 * BLOCK_Q), F32),
                        pltpu.SemaphoreType.DMA((3,))] + host.scratch,
        compiler_params=_params(),
    )(q, kv, do, sinks, *host.args)
    return (*res[:3], list(res[3:]))


def _mix_out_fwd(x1, h, gl, o, g_lru, g_attn, g_post, w_o):
    t = x1.shape[0]
    tm = _token_tile(t)

    def body(x_ref, h_ref, gl_ref, o_ref, g1_ref, g2_ref, gp_ref, w_ref, x2_ref, m_ref):
        y = h_ref[...] * _gelu(gl_ref[...])[0]
        yn1 = _rms_fwd(y, g1_ref[...]).astype(BF16)
        yn2 = _rms_fwd(o_ref[...], g2_ref[...]).astype(BF16)
        m = _dot(yn1, w_ref[0:512, :]) + _dot(yn2, w_ref[512:1024, :])
        m_ref[...] = m
        x2_ref[...] = x_ref[...] + _rms_fwd(m, gp_ref[...])

    tok = pl.BlockSpec((tm, D_MODEL), lambda i: (i, 0))
    half = pl.BlockSpec((tm, 512), lambda i: (i, 0))
    vec = pl.BlockSpec((1, D_MODEL), lambda i: (0, 0))
    hvec = pl.BlockSpec((1, 512), lambda i: (0, 0))
    return pl.pallas_call(
        body, name="mix_out_fwd", grid=(t // tm,),
        in_specs=[tok, half, half, half, hvec, hvec, vec, pl.BlockSpec((D_MODEL, D_MODEL), lambda i: (0, 0))],
        out_specs=[tok, tok],
        out_shape=[jax.ShapeDtypeStruct((t, D_MODEL), F32), jax.ShapeDtypeStruct((t, D_MODEL), F32)],
        compiler_params=_params(1),
    )(x1, h, gl, o, g_lru, g_attn, g_post, w_o)


def _mix_out_bwd(dx2, m, h, gl, o, g_lru, g_attn, g_post, w_o):
    t = dx2.shape[0]
    tm = _token_tile(t)

    def body(dx_ref, m_ref, h_ref, gl_ref, o_ref, g1_ref, g2_ref, gp_ref, w_ref,
             dy_ref, do_ref, dw_ref, dgp_ref, dg1_ref, dg2_ref):
        _zero_at_first(pl.program_id(0) == 0, dw_ref, dgp_ref, dg1_ref, dg2_ref)
        dm, dgp = _rms_bwd(m_ref[...], gp_ref[...], dx_ref[...])
        dmb = dm.astype(BF16)
        y = h_ref[...] * _gelu(gl_ref[...])[0]
        o = o_ref[...]
        yn1 = _rms_fwd(y, g1_ref[...]).astype(BF16)
        yn2 = _rms_fwd(o, g2_ref[...]).astype(BF16)
        dw_ref[0:512, :] += _dot_tn(yn1, dmb)
        dw_ref[512:1024, :] += _dot_tn(yn2, dmb)
        dy, dg1 = _rms_bwd(y, g1_ref[...], _dot_nt(dmb, w_ref[0:512, :]))
        do, dg2 = _rms_bwd(o, g2_ref[...], _dot_nt(dmb, w_ref[512:1024, :]))
        dy_ref[...] = dy
        do_ref[...] = do.astype(BF16)
        dgp_ref[...] += dgp
        dg1_ref[...] += dg1
        dg2_ref[...] += dg2

    tok = pl.BlockSpec((tm, D_MODEL), lambda i: (i, 0))
    half = pl.BlockSpec((tm, 512), lambda i: (i, 0))
    vec = pl.BlockSpec((1, D_MODEL), lambda i: (0, 0))
    hvec = pl.BlockSpec((1, 512), lambda i: (0, 0))
    mat = pl.BlockSpec((D_MODEL, D_MODEL), lambda i: (0, 0))
    return pl.pallas_call(
        body, name="mix_out_bwd", grid=(t // tm,),
        in_specs=[tok, tok, half, half, half, hvec, hvec, vec, mat],
        out_specs=[half, half, mat, vec, hvec, hvec],
        out_shape=[jax.ShapeDtypeStruct((t, 512), F32), jax.ShapeDtypeStruct((t, 512), BF16),
                   jax.ShapeDtypeStruct((D_MODEL, D_MODEL), F32), jax.ShapeDtypeStruct((1, D_MODEL), F32),
                   jax.ShapeDtypeStruct((1, 512), F32), jax.ShapeDtypeStruct((1, 512), F32)],
        compiler_params=_params(1),
    )(dx2, m, h, gl, o, g_lru, g_attn, g_post, w_o)


def _mix_in_bwd(dx2, x1, g, dxl, dgl, dq, dkv, w_in):
    t = x1.shape[0]
    tm = _token_tile(t)

    def body(dx2_ref, x_ref, g_ref, dxl_ref, dgl_ref, dq_ref, dkv_ref, w_ref, dx1_ref, dw_ref, dg_ref):
        _zero_at_first(pl.program_id(0) == 0, dw_ref, dg_ref)
        x = x_ref[...]
        nb = _rms_fwd(x, g_ref[...]).astype(BF16)
        lo = lax.broadcasted_iota(jnp.int32, (tm, 128), 1) < HEAD_DIM
        dkv = dkv_ref[...]
        folded = []
        for k in range(4):
            seg = dkv[:, 128 * k:128 * (k + 1)]
            folded.append(jnp.where(lo, seg + pltpu.roll(seg, HEAD_DIM, 1), 0.0).astype(BF16))
        dproj = jnp.concatenate([dxl_ref[...].astype(BF16), dgl_ref[...].astype(BF16), dq_ref[...]] + folded, axis=1)
        dw_ref[...] += _dot_tn(nb, dproj)
        dx, dg = _rms_bwd(x, g_ref[...], _dot_nt(dproj, w_ref[...]))
        dx1_ref[...] = dx2_ref[...] + dx
        dg_ref[...] += dg

    tok = pl.BlockSpec((tm, D_MODEL), lambda i: (i, 0))
    half = pl.BlockSpec((tm, 512), lambda i: (i, 0))
    vec = pl.BlockSpec((1, D_MODEL), lambda i: (0, 0))
    mat = pl.BlockSpec((D_MODEL, D_IN_DUP), lambda i: (0, 0))
    return pl.pallas_call(
        body, name="mix_in_bwd", grid=(t // tm,),
        in_specs=[tok, tok, vec, half, half, half, half, mat], out_specs=[tok, mat, vec],
        out_shape=[jax.ShapeDtypeStruct((t, D_MODEL), F32), jax.ShapeDtypeStruct((D_MODEL, D_IN_DUP), F32),
                   jax.ShapeDtypeStruct((1, D_MODEL), F32)],
        compiler_params=_params(1),
    )(dx2, x1, g, dxl, dgl, dq, dkv, w_in)


def _row_tile(rows):
    return rows if rows <= 512 else 256


def _chip_sum(grad, from_sibling, core, name):
    _, rows, cols = grad.shape
    tr = _row_tile(rows)

    def body(core_ref, g_ref, s_ref, out_ref):
        out_ref[0] = (g_ref[0, 0] + s_ref[0]).astype(BF16)

    grid_spec = pltpu.PrefetchScalarGridSpec(
        num_scalar_prefetch=1, grid=(4, rows // tr),
        in_specs=[pl.BlockSpec((1, 1, tr, cols), lambda q, i, core: (q, core[0], i, 0)),
                  pl.BlockSpec((1, tr, cols), lambda q, i, core: (q, i, 0))],
        out_specs=pl.BlockSpec((1, tr, cols), lambda q, i, core: (q, i, 0)))
    return pl.pallas_call(
        body, name=name, grid_spec=grid_spec, out_shape=jax.ShapeDtypeStruct((4, rows, cols), BF16),
        compiler_params=_params(2),
    )(core, grad.reshape(4, 2, rows, cols), from_sibling)


def _adamw(w, g, m, v):
    m = ADAM_B1 * m + (1.0 - ADAM_B1) * g
    v = ADAM_B2 * v + (1.0 - ADAM_B2) * (g * g)
    m_hat = m / (1.0 - ADAM_B1 ** ADAM_STEP)
    v_hat = v / (1.0 - ADAM_B2 ** ADAM_STEP)
    delta = -ADAM_LR * (m_hat / (jnp.sqrt(v_hat) + ADAM_EPS) + ADAM_WD * w)
    return delta, m, v


def _shard_update(grad, from_sibling, from_chips, w, m, v, place, name):
    _, rows, cols = grad.shape
    tr = _row_tile(rows)

    def body(place_ref, g_ref, s_ref, c_ref, w_ref, m_ref, v_ref, go_ref, d_ref, mo_ref, vo_ref):
        g = g_ref[0, 0] + s_ref[0]
        g = g + c_ref[0].astype(F32)
        g = g + c_ref[1].astype(F32)
        g = g + c_ref[2].astype(F32)
        go_ref[...] = g
        d_ref[...], mo_ref[...], vo_ref[...] = _adamw(w_ref[...], g, m_ref[...], v_ref[...])

    flat = pl.BlockSpec((tr, cols), lambda i, place: (i, 0))
    grid_spec = pltpu.PrefetchScalarGridSpec(
        num_scalar_prefetch=1, grid=(rows // tr,),
        in_specs=[pl.BlockSpec((1, 1, tr, cols), lambda i, place: (place[0], place[1], i, 0)),
                  pl.BlockSpec((1, tr, cols), lambda i, place: (place[0], i, 0)),
                  pl.BlockSpec((3, tr, cols), lambda i, place: (0, i, 0)), flat, flat, flat],
        out_specs=[flat, flat, flat, flat])
    return pl.pallas_call(
        body, name=name, grid_spec=grid_spec, out_shape=[jax.ShapeDtypeStruct((rows, cols), F32)] * 4,
        compiler_params=_params(1),
    )(place, grad.reshape(4, 2, rows, cols), from_sibling, from_chips, w, m, v)


def _small_update(gathered, w, m, v):
    rows = w.shape[0]

    def body(g_ref, w_ref, m_ref, v_ref, go_ref, d_ref, mo_ref, vo_ref):
        g = g_ref[0]
        for d in range(1, N_DEV):
            g = g + g_ref[d]
        go_ref[...] = g
        d_ref[...], mo_ref[...], vo_ref[...] = _adamw(w_ref[...], g, m_ref[...], v_ref[...])

    return pl.pallas_call(
        body, name="small_update", out_shape=[jax.ShapeDtypeStruct((rows, 128), F32)] * 4,
        compiler_params=_params(),
    )(gathered, w, m, v)


SMALL_VECTORS = ("ffn1_pre_g", "ffn1_post_g", "mix_pre_g", "mix_post_g", "ffn2_pre_g", "ffn2_post_g",
                 "conv_b", "b_rg", "b_ig", "lru_lambda", "g_lru_out", "g_attn_out")


def _pad_rows(a, rows):
    return jnp.concatenate([a, jnp.zeros((rows - a.shape[0], a.shape[1]), a.dtype)], axis=0)


def _pack_small(vals, loss, conv_w_full):
    parts = [_pad_rows(jnp.pad(jnp.reshape(loss, (1, 1)), ((0, 0), (0, 127))), 8)]
    parts += [vals[k].reshape(-1, 128) for k in SMALL_VECTORS]
    parts += [vals["w_rg"].reshape(-1, 128), vals["w_ig"].reshape(-1, 128)]
    parts.append(_pad_rows(jnp.pad(vals["sinks"].reshape(1, 8), ((0, 0), (0, 120))), 8))
    parts.append(conv_w_full.reshape(16, 128))
    return jnp.concatenate(parts, axis=0)


def _unpack_small(blob, like):
    out = {}
    r = 8
    for k in SMALL_VECTORS:
        n = like[k].size // 128
        out[k] = blob[r:r + n].reshape(like[k].shape)
        r += n
    for k in ("w_rg", "w_ig"):
        out[k] = blob[r:r + 256].reshape(like[k].shape)
        r += 256
    out["sinks"] = blob[r:r + 1, 0:8].reshape(like["sinks"].shape)
    r += 8
    conv_w_full = blob[r:r + 16].reshape(4, D_LRU)
    return out, blob[0, 0], conv_w_full


def _dup_in_columns(w):
    k0, k1, v0, v1 = w[:, 1536:1600], w[:, 1600:1664], w[:, 1664:1728], w[:, 1728:1792]
    return jnp.concatenate([w[:, :1536], k0, k0, v0, v0, k1, k1, v1, v1], axis=1)


def _undup_in_columns(dw):
    return jnp.concatenate([dw[:, :1536], dw[:, 1536:1600], dw[:, 1792:1856], dw[:, 1664:1728], dw[:, 1920:1984]], axis=1)


def _pair_block_diag(w):
    w = w.reshape(N_LRU_GROUP, 2, 64, 64)
    z = jnp.zeros((N_LRU_GROUP, 64, 64), w.dtype)
    top = jnp.concatenate([w[:, 0], z], axis=2)
    bot = jnp.concatenate([z, w[:, 1]], axis=2)
    return jnp.concatenate([top, bot], axis=1)


def _pair_block_diag_grad(dw2, shape):
    return jnp.stack([dw2[:, :64, :64], dw2[:, 64:, 64:]], axis=1).reshape(shape)


def kernel(x, ffn1_pre_g, ffn1_w_gu, ffn1_w_down, ffn1_post_g, mix_pre_g, w_in, conv_w, conv_b, w_rg, b_rg, w_ig, b_ig, lru_lambda, sinks, g_lru_out, g_attn_out, w_o, mix_post_g, ffn2_pre_g, ffn2_w_gu, ffn2_w_down, ffn2_post_g, loss_target, m_ffn1_pre_g, m_ffn1_w_gu, m_ffn1_w_down, m_ffn1_post_g, m_mix_pre_g, m_w_in, m_conv_w, m_conv_b, m_w_rg, m_b_rg, m_w_ig, m_b_ig, m_lru_lambda, m_sinks, m_g_lru_out, m_g_attn_out, m_w_o, m_mix_post_g, m_ffn2_pre_g, m_ffn2_w_gu, m_ffn2_w_down, m_ffn2_post_g, v_ffn1_pre_g, v_ffn1_w_gu, v_ffn1_w_down, v_ffn1_post_g, v_mix_pre_g, v_w_in, v_conv_w, v_conv_b, v_w_rg, v_b_rg, v_w_ig, v_b_ig, v_lru_lambda, v_sinks, v_g_lru_out, v_g_attn_out, v_w_o, v_mix_post_g, v_ffn2_pre_g, v_ffn2_w_gu, v_ffn2_w_down, v_ffn2_post_g):
    args = dict(locals())
    names = ["ffn1_pre_g", "ffn1_w_gu", "ffn1_w_down", "ffn1_post_g", "mix_pre_g", "w_in", "conv_w", "conv_b", "w_rg",
             "b_rg", "w_ig", "b_ig", "lru_lambda", "sinks", "g_lru_out", "g_attn_out", "w_o", "mix_post_g",
             "ffn2_pre_g", "ffn2_w_gu", "ffn2_w_down", "ffn2_post_g"]
    big = ["ffn1_w_gu", "ffn1_w_down", "w_in", "w_o", "ffn2_w_gu", "ffn2_w_down"]
    w = {k: args[k] for k in names}
    mom = {k: args["m_" + k] for k in names}
    var = {k: args["v_" + k] for k in names}
    t = x.shape[1]
    xs = x.reshape(t, D_MODEL)
    target = loss_target.reshape(t, D_MODEL)
    cx, cy, cc = _coords()
    me = 4 * cx + 2 * cy + cc
    core = jnp.reshape(cc, (1,)).astype(jnp.int32)
    place = jnp.stack([2 * cx + cy, cc]).astype(jnp.int32)

    shard2d = {k: w[k].reshape(w[k].shape[1:]) for k in big}
    shard_bf = {k: shard2d[k].astype(BF16) for k in big}
    conv_pad = jnp.pad(conv_w.reshape(4, 64), ((0, 4), (0, 64)))
    (first_w,) = _run_exchanges([_Gather([shard_bf["ffn1_w_gu"], shard_bf["ffn1_w_down"]])], "all_gather_ffn1")
    wgu1 = first_w[0].reshape(2, N_CHUNK, D_MODEL, CHUNK)
    wd1 = first_w[1].reshape(N_CHUNK, CHUNK, D_MODEL)
    rest = _Gather([shard_bf["w_in"], shard_bf["w_o"], shard_bf["ffn2_w_gu"], shard_bf["ffn2_w_down"], conv_pad])

    x1, f1, gu1, gathered = _ffn_fwd(xs, ffn1_pre_g, wgu1, wd1, ffn1_post_g, None, "ffn1_fwd", rest)
    w_in_full = _dup_in_columns(jnp.transpose(gathered[0], (1, 0, 2)).reshape(D_MODEL, D_IN))
    w_o_full = gathered[1].reshape(D_MODEL, D_MODEL)
    wgu2 = gathered[2].reshape(2, N_CHUNK, D_MODEL, CHUNK)
    wd2 = gathered[3].reshape(N_CHUNK, CHUNK, D_MODEL)
    conv_w_full = jnp.transpose(gathered[4][:, 0:4, 0:64], (1, 0, 2)).reshape(4, D_LRU)
    p_lru = jnp.concatenate([conv_b, b_rg, b_ig, lru_lambda, conv_w_full], axis=0)
    wrg2 = _pair_block_diag(w_rg[0]).astype(BF16)
    wig2 = _pair_block_diag(w_ig[0]).astype(BF16)
    xl, gl, q, kv = _mix_in_fwd(x1, mix_pre_g, w_in_full)
    h = _lru_fwd(xl, p_lru, wrg2, wig2)
    o = _attn_fwd(q, kv, sinks)
    x2, mo = _mix_out_fwd(x1, h, gl, o, g_lru_out, g_attn_out, mix_post_g, w_o_full)
    dx3, f2, gu2, loss_parts, _ = _ffn_fwd(x2, ffn2_pre_g, wgu2, wd2, ffn2_post_g, target, "ffn2_fwd")
    loss_local = jnp.sum(loss_parts[::8, 0])

    g = {}
    partial, from_sibling, from_chips = {}, {}, {}

    def chip_sums(keys):
        return [_chip_sum(partial[k], from_sibling[k], core, "chip_sum_" + k) for k in keys]

    n2, df2, g["ffn2_post_g"], _ = _ffn_bwd_pre(dx3, x2, f2, ffn2_pre_g, ffn2_post_g, "ffn2_bwd_pre")
    dgu2, dwgu2, dwd2, _ = _ffn_bwd_w(n2, df2, gu2, wd2, "ffn2_bwd_w")
    partial["ffn2_w_gu"] = dwgu2.reshape(N_DEV, D_MODEL, CHUNK)
    partial["ffn2_w_down"] = dwd2.reshape(N_DEV, D_FF // N_DEV, D_MODEL)
    ffn2_keys = ["ffn2_w_gu", "ffn2_w_down"]
    dx2, g["ffn2_pre_g"], got = _ffn_bwd_x(dgu2, wgu2, x2, ffn2_pre_g, dx3, "ffn2_bwd_x",
                                           _SiblingExchange([partial[k] for k in ffn2_keys]))
    from_sibling.update(zip(ffn2_keys, got))
    dy, do, dwo, g["mix_post_g"], g["g_lru_out"], g["g_attn_out"] = _mix_out_bwd(
        dx2, mo, h, gl, o, g_lru_out, g_attn_out, mix_post_g, w_o_full)
    dq, dkv, dsink, got = _attn_bwd(q, kv, do, sinks, _ChipExchange(chip_sums(ffn2_keys)))
    from_chips.update(zip(ffn2_keys, got))
    dxl, dgl, dp, dwrg2, dwig2 = _lru_bwd(dy, h, xl, gl, p_lru, wrg2, wig2)
    dx1, dwin_dup, g["mix_pre_g"] = _mix_in_bwd(dx2, x1, mix_pre_g, dxl, dgl, dq, dkv, w_in_full)
    partial["w_in"] = jnp.transpose(_undup_in_columns(dwin_dup).reshape(D_MODEL, N_DEV, D_IN // N_DEV), (1, 0, 2))
    partial["w_o"] = dwo.reshape(N_DEV, D_MODEL // N_DEV, D_MODEL)
    mix_keys = ["w_in", "w_o"]
    n1, df1, g["ffn1_post_g"], got = _ffn_bwd_pre(dx1, xs, f1, ffn1_pre_g, ffn1_post_g, "ffn1_bwd_pre",
                                                  _SiblingExchange([partial[k] for k in mix_keys]))
    from_sibling.update(zip(mix_keys, got))
    dgu1, dwgu1, dwd1, got = _ffn_bwd_w(n1, df1, gu1, wd1, "ffn1_bwd_w", _ChipExchange(chip_sums(mix_keys)))
    from_chips.update(zip(mix_keys, got))
    partial["ffn1_w_gu"] = dwgu1.reshape(N_DEV, D_MODEL, CHUNK)
    partial["ffn1_w_down"] = dwd1.reshape(N_DEV, D_FF // N_DEV, D_MODEL)
    ffn1_keys = ["ffn1_w_gu", "ffn1_w_down"]
    (got,) = _run_exchanges([_SiblingExchange([partial[k] for k in ffn1_keys])], "ffn1_sibling_exchange")
    from_sibling.update(zip(ffn1_keys, got))
    dx0, g["ffn1_pre_g"], got = _ffn_bwd_x(dgu1, wgu1, xs, ffn1_pre_g, dx1, "ffn1_bwd_x",
                                           _ChipExchange(chip_sums(ffn1_keys)))
    from_chips.update(zip(ffn1_keys, got))
    g["conv_b"], g["b_rg"], g["b_ig"], g["lru_lambda"] = dp[0:1], dp[1:2], dp[2:3], dp[3:4]
    g["w_rg"] = _pair_block_diag_grad(dwrg2, w_rg.shape)
    g["w_ig"] = _pair_block_diag_grad(dwig2, w_ig.shape)
    g["sinks"] = dsink[:, 0].reshape(1, 8)

    col = me * 64

    def conv_rows(a):
        return lax.dynamic_update_slice(jnp.zeros((4, D_LRU), F32), a.reshape(4, 64), (0, col))

    zero = jnp.zeros((), F32)
    g_blob = _pack_small(g, loss_local, dp[4:8])
    ((all_blobs,),) = _run_exchanges([_Gather([g_blob])], "all_gather_small_grads")
    grads, delta, new_m, new_v = {}, {}, {}, {}
    for k in big:
        shape = w[k].shape
        res = _shard_update(partial[k], from_sibling[k], from_chips[k], shard2d[k], mom[k].reshape(shape[1:]),
                            var[k].reshape(shape[1:]), place, "update_" + k)
        grads[k], delta[k], new_m[k], new_v[k] = [r.reshape(shape) for r in res]

    res = _small_update(all_blobs, _pack_small(w, zero, conv_rows(conv_w)), _pack_small(mom, zero, conv_rows(m_conv_w)),
                        _pack_small(var, zero, conv_rows(v_conv_w)))
    loss = None
    for dst, blob in zip((grads, delta, new_m, new_v), res):
        small, first, conv_full = _unpack_small(blob, w)
        dst.update(small)
        dst["conv_w"] = lax.dynamic_slice(conv_full, (0, col), (4, 64)).reshape(conv_w.shape)
        if loss is None:
            loss = first
    return (loss, dx0.reshape(x.shape), *[grads[k] for k in names], *[delta[k] for k in names],
            *[new_m[k] for k in names], *[new_v[k] for k in names])
```

```python
import functools

import jax
import jax.numpy as jnp
from jax import lax
from jax.experimental import pallas as pl
from jax.experimental.pallas import tpu as pltpu

F32 = jnp.float32
BF16 = jnp.bfloat16

D_MODEL = 1024
D_FF = 2816
N_DEV = 8
N_CHUNK = 4
CHUNK = D_FF // N_CHUNK
D_LRU = 512
D_ATTN = 512
LRU_GROUP = 128
N_LRU_GROUP = D_LRU // LRU_GROUP
HEAD_DIM = 64
BLOCK_Q = 128
D_IN = 1792
D_IN_DUP = 2048
RMS_EPS = 1e-6
LRU_C = 8.0
MASK_VALUE = -1e30
ATTN_SCALE = HEAD_DIM ** -0.5

ADAM_LR = 0.001
ADAM_B1 = 0.9
ADAM_B2 = 0.999
ADAM_EPS = 1e-08
ADAM_WD = 0.01
ADAM_STEP = 10

VMEM_LIMIT_V7X = 56 * 2 ** 20

ANY = pl.BlockSpec(memory_space=pl.ANY)
SMEM = pl.BlockSpec(memory_space=pltpu.SMEM)
MESH = pl.DeviceIdType.MESH


def _params(n_grid=0):
    sem = ("arbitrary",) * n_grid if n_grid else None
    return pltpu.CompilerParams(dimension_semantics=sem, vmem_limit_bytes=VMEM_LIMIT_V7X)


def _dot(a, b):
    return lax.dot_general(a, b, (((1,), (0,)), ((), ())), preferred_element_type=F32)


def _dot_nt(a, b):
    return lax.dot_general(a, b, (((1,), (1,)), ((), ())), preferred_element_type=F32)


def _dot_tn(a, b):
    return lax.dot_general(a, b, (((0,), (0,)), ((), ())), preferred_element_type=F32)


def _sigmoid(x):
    return 1.0 / (1.0 + jnp.exp(-x))


def _rms_fwd(x, g):
    r = lax.rsqrt(jnp.mean(x * x, axis=-1, keepdims=True) + RMS_EPS)
    return x * r * g


def _rms_bwd(x, g, dy):
    r = lax.rsqrt(jnp.mean(x * x, axis=-1, keepdims=True) + RMS_EPS)
    xh = x * r
    dg = jnp.sum(dy * xh, axis=0, keepdims=True)
    dxh = dy * g
    dx = r * (dxh - xh * jnp.mean(dxh * xh, axis=-1, keepdims=True))
    return dx, dg


def _gelu(x):
    c = 0.7978845608028654
    inner = c * (x + 0.044715 * x * x * x)
    th = jnp.tanh(inner)
    ge = 0.5 * x * (1.0 + th)
    dge = 0.5 * (1.0 + th) + 0.5 * x * (1.0 - th * th) * c * (1.0 + 3.0 * 0.044715 * x * x)
    return ge, dge


def _zero_at_first(first, *refs):
    @pl.when(first)
    def _():
        for ref in refs:
            ref[...] = jnp.zeros_like(ref)


def _token_tile(t):
    return 512 if t >= 2048 else t // 2


def _ffn_bwd_tile(t):
    return 1024 if t >= 4096 else t // 2


def _coords():
    return lax.axis_index("x"), lax.axis_index("y"), lax.axis_index("c")


class _Gather:
    n_phases = 3
    at = (0.0, 0.8, 1.0)

    def __init__(self, shards):
        k = len(shards)
        self.arrays = list(shards)
        self.out_shape = [jax.ShapeDtypeStruct((N_DEV,) + s.shape, s.dtype) for s in shards]
        self.scratch = [pltpu.SemaphoreType.DMA((7 * k,)), pltpu.SemaphoreType.DMA((7 * k,)), pltpu.SemaphoreType.DMA((k,))]

    def run(self, phase, ins, outs, sems):
        send_sems, recv_sems, local_sems = sems
        k_arr = len(ins)
        x, y, c = _coords()
        me, sibling = (x, y, c), (x, y, 1 - c)
        chips = [(1 - x, y), (x, 1 - y), (1 - x, 1 - y)]

        def rows(k, dev):
            return outs[k].at[4 * dev[0] + 2 * dev[1] + dev[2]]

        def copy(k, slot, block, to, src=None):
            return pltpu.make_async_remote_copy(
                src_ref=rows(k, block) if src is None else src, dst_ref=rows(k, block),
                send_sem=send_sems.at[7 * k + slot], recv_sem=recv_sems.at[7 * k + slot],
                device_id=to, device_id_type=MESH)

        def mine():
            return [pltpu.make_async_copy(ins[k], rows(k, me), local_sems.at[k]) for k in range(k_arr)]

        def first():
            return [copy(k, slot, me, to, src=ins[k]) for k in range(k_arr)
                    for slot, to in enumerate([sibling] + [(*chip, c) for chip in chips])]

        def passed(j, k):
            return copy(k, 4 + j, (*chips[j], c), sibling)

        if phase == 0:
            for cp in mine() + first():
                cp.start()
        elif phase == 1:
            for j, chip in enumerate(chips):
                for k in range(k_arr):
                    copy(k, 1 + j, (*chip, c), me).wait_recv()
                    passed(j, k).start()
        else:
            for k in range(k_arr):
                copy(k, 0, sibling, me).wait_recv()
                for j, chip in enumerate(chips):
                    copy(k, 4 + j, (*chip, 1 - c), me).wait_recv()
            for cp in first() + [passed(j, k) for j in range(3) for k in range(k_arr)]:
                cp.wait_send()
            for cp in mine():
                cp.wait()


class _SiblingExchange:
    n_phases = 2
    at = (0.0, 1.0)

    def __init__(self, grads):
        k = len(grads)
        self.arrays = list(grads)
        self.out_shape = [jax.ShapeDtypeStruct((4,) + g.shape[1:], g.dtype) for g in grads]
        self.scratch = [pltpu.SemaphoreType.DMA((4 * k,)), pltpu.SemaphoreType.DMA((4 * k,))]

    def run(self, phase, ins, outs, sems):
        send_sems, recv_sems = sems
        x, y, c = _coords()
        copies = [pltpu.make_async_remote_copy(
            src_ref=ins[k].at[2 * q + (1 - c)], dst_ref=outs[k].at[q],
            send_sem=send_sems.at[4 * k + q], recv_sem=recv_sems.at[4 * k + q],
            device_id=(x, y, 1 - c), device_id_type=MESH) for k in range(len(ins)) for q in range(4)]
        for cp in copies:
            if phase == 0:
                cp.start()
            else:
                cp.wait_recv()
                cp.wait_send()


class _ChipExchange:
    n_phases = 2
    at = (0.0, 1.0)

    def __init__(self, chip_sums):
        k = len(chip_sums)
        self.arrays = list(chip_sums)
        self.out_shape = [jax.ShapeDtypeStruct((3,) + s.shape[1:], s.dtype) for s in chip_sums]
        self.scratch = [pltpu.SemaphoreType.DMA((3 * k,)), pltpu.SemaphoreType.DMA((3 * k,))]

    def run(self, phase, ins, outs, sems):
        send_sems, recv_sems = sems
        x, y, c = _coords()
        chips = [(1 - x, y), (x, 1 - y), (1 - x, 1 - y)]
        copies = [pltpu.make_async_remote_copy(
            src_ref=ins[k].at[j], dst_ref=outs[k].at[j],
            send_sem=send_sems.at[3 * k + j], recv_sem=recv_sems.at[3 * k + j],
            device_id=(*chip, c), device_id_type=MESH) for k in range(len(ins)) for j, chip in enumerate(chips)]
        for cp in copies:
            if phase == 0:
                cp.start()
            else:
                cp.wait_recv()
                cp.wait_send()


class _Host:
    def __init__(self, exchange):
        self.ex = exchange
        self.args = [] if exchange is None else exchange.arrays
        self.in_specs = [ANY] * len(self.args)
        self.out_shape = [] if exchange is None else exchange.out_shape
        self.out_specs = [ANY] * len(self.out_shape)
        self.scratch = [] if exchange is None else exchange.scratch

    def split(self, refs, n_in, n_out, n_scratch):
        a, b, s = len(self.args), len(self.out_shape), len(self.scratch)
        own_in, ex_in = refs[:n_in], refs[n_in:n_in + a]
        rest = refs[n_in + a:]
        own_out, ex_out = rest[:n_out], rest[n_out:n_out + b]
        rest = rest[n_out + b:]
        own_scratch, ex_sems = rest[:n_scratch], rest[n_scratch:n_scratch + s]
        return list(own_in) + list(own_out) + list(own_scratch), (ex_in, ex_out, ex_sems)

    def at_steps(self, step, n_steps, ex_refs):
        if self.ex is None:
            return
        for p in range(self.ex.n_phases):
            pl.when(step == int(round(self.ex.at[p] * (n_steps - 1))))(functools.partial(self.ex.run, p, *ex_refs))

    def phase(self, p, ex_refs):
        if self.ex is not None:
            self.ex.run(p, *ex_refs)


def _run_exchanges(exchanges, name):
    hosts = [_Host(ex) for ex in exchanges]
    n_in = [len(h.args) for h in hosts]
    n_out = [len(h.out_shape) for h in hosts]
    n_sc = [len(h.scratch) for h in hosts]

    def body(*refs):
        ins, outs, scr = refs[:sum(n_in)], refs[sum(n_in):sum(n_in) + sum(n_out)], refs[sum(n_in) + sum(n_out):]
        parts = []
        for e in range(len(hosts)):
            parts.append((ins[sum(n_in[:e]):sum(n_in[:e + 1])], outs[sum(n_out[:e]):sum(n_out[:e + 1])],
                          scr[sum(n_sc[:e]):sum(n_sc[:e + 1])]))
        for h, part in zip(hosts, parts):
            h.phase(0, part)
        for h, part in zip(hosts, parts):
            for p in range(1, h.ex.n_phases):
                h.phase(p, part)

    res = pl.pallas_call(
        body, name=name, in_specs=[ANY] * sum(n_in), out_specs=[ANY] * sum(n_out),
        out_shape=[s for h in hosts for s in h.out_shape], scratch_shapes=[s for h in hosts for s in h.scratch],
    )(*[a for h in hosts for a in h.args])
    return [res[sum(n_out[:e]):sum(n_out[:e + 1])] for e in range(len(hosts))]


def _ffn_fwd(x, g_pre, wgu, wd, g_post, target, name, exchange=None):
    t = x.shape[0]
    tm = _token_tile(t)
    n_i = t // tm
    with_loss = target is not None
    host = _Host(exchange)
    n_in, n_out = (6, 6) if with_loss else (5, 4)

    def body(*refs):
        own, ex_refs = host.split(refs, n_in, n_out, 0)
        if with_loss:
            x_ref, gpre_ref, wgu_ref, wd_ref, gpost_ref, tgt_ref, xo_ref, n_ref, df_ref, gu_ref, dgpost_ref, loss_ref = own
            _zero_at_first(pl.program_id(0) == 0, dgpost_ref)
        else:
            x_ref, gpre_ref, wgu_ref, wd_ref, gpost_ref, xo_ref, f_ref, n_ref, gu_ref = own
        host.at_steps(pl.program_id(0), n_i, ex_refs)
        x = x_ref[...]
        n = _rms_fwd(x, gpre_ref[...]).astype(BF16)
        n_ref[...] = n
        f = None
        for j in range(N_CHUNK):
            gate = _dot(n, wgu_ref[0, j])
            up = _dot(n, wgu_ref[1, j])
            gu_ref[0, j] = gate.astype(BF16)
            gu_ref[1, j] = up.astype(BF16)
            part = _dot((gate * _sigmoid(gate) * up).astype(BF16), wd_ref[j])
            f = part if f is None else f + part
        xo = x + 0.5 * _rms_fwd(f, gpost_ref[...])
        if with_loss:
            err = xo - tgt_ref[...]
            d_out = err * (1.0 / D_MODEL)
            xo_ref[...] = d_out
            df, dg = _rms_bwd(f, gpost_ref[...], 0.5 * d_out)
            df_ref[...] = df.astype(BF16)
            dgpost_ref[...] += dg
            part = 0.5 * jnp.sum(jnp.sum(err * err, axis=-1, keepdims=True) * (1.0 / D_MODEL), axis=0, keepdims=True)
            loss_ref[...] = jnp.broadcast_to(part, loss_ref.shape)
        else:
            f_ref[...] = f
            xo_ref[...] = xo

    tok = pl.BlockSpec((tm, D_MODEL), lambda i: (i, 0))
    vec = pl.BlockSpec((1, D_MODEL), lambda i: (0, 0))
    act = pl.BlockSpec((2, N_CHUNK, tm, CHUNK), lambda i: (0, 0, i, 0))
    tok_f32 = jax.ShapeDtypeStruct((t, D_MODEL), F32)
    tok_bf16 = jax.ShapeDtypeStruct((t, D_MODEL), BF16)
    act_shape = jax.ShapeDtypeStruct((2, N_CHUNK, t, CHUNK), BF16)
    in_specs = [tok, vec,
                pl.BlockSpec((2, N_CHUNK, D_MODEL, CHUNK), lambda i: (0, 0, 0, 0), pipeline_mode=pl.Buffered(1)),
                pl.BlockSpec((N_CHUNK, CHUNK, D_MODEL), lambda i: (0, 0, 0), pipeline_mode=pl.Buffered(1)),
                vec]
    args = [x, g_pre, wgu, wd, g_post]
    if with_loss:
        in_specs.append(tok)
        args.append(target)
        out_shape = [tok_f32, tok_bf16, tok_bf16, act_shape, jax.ShapeDtypeStruct((1, D_MODEL), F32),
                     jax.ShapeDtypeStruct((n_i * 8, 128), F32)]
        out_specs = [tok, tok, tok, act, vec, pl.BlockSpec((8, 128), lambda i: (i, 0))]
    else:
        out_shape = [tok_f32, tok_f32, tok_bf16, act_shape]
        out_specs = [tok, tok, tok, act]
    res = pl.pallas_call(
        body, name=name, grid=(n_i,), in_specs=in_specs + host.in_specs, out_specs=out_specs + host.out_specs,
        out_shape=out_shape + host.out_shape, scratch_shapes=host.scratch, compiler_params=_params(1),
    )(*args, *host.args)
    return (*res[:n_out], list(res[n_out:]))


def _ffn_bwd_w(n, df, gu, wd, name, exchange=None):
    t = n.shape[0]
    tm = _ffn_bwd_tile(t)
    n_i = t // tm
    host = _Host(exchange)

    def body(*refs):
        (n_ref, df_ref, gu_ref, wd_ref, dgu_ref, dwgu_ref, dwd_ref), ex_refs = host.split(refs, 4, 3, 0)
        i = pl.program_id(1)
        host.at_steps(pl.program_id(0) * n_i + i, N_CHUNK * n_i, ex_refs)
        _zero_at_first(i == 0, dwgu_ref, dwd_ref)
        nb = n_ref[...]
        dfb = df_ref[...]
        gate = gu_ref[0, 0].astype(F32)
        up = gu_ref[1, 0].astype(F32)
        s = _sigmoid(gate)
        silu = gate * s
        a = (silu * up).astype(BF16)
        da = _dot_nt(dfb, wd_ref[0])
        dup = (da * silu).astype(BF16)
        dgate = (da * up * (s * (1.0 + gate * (1.0 - s)))).astype(BF16)
        dgu_ref[0, 0] = dgate
        dgu_ref[1, 0] = dup
        dwgu_ref[0, 0] += _dot_tn(nb, dgate)
        dwgu_ref[1, 0] += _dot_tn(nb, dup)
        dwd_ref[0] += _dot_tn(a, dfb)

    tok = pl.BlockSpec((tm, D_MODEL), lambda j, i: (i, 0))
    act = pl.BlockSpec((2, 1, tm, CHUNK), lambda j, i: (0, j, i, 0))
    wgu_spec = pl.BlockSpec((2, 1, D_MODEL, CHUNK), lambda j, i: (0, j, 0, 0), pipeline_mode=pl.Buffered(1))
    wd_spec = pl.BlockSpec((1, CHUNK, D_MODEL), lambda j, i: (j, 0, 0), pipeline_mode=pl.Buffered(1))
    res = pl.pallas_call(
        body, name=name, grid=(N_CHUNK, n_i),
        in_specs=[tok, tok, act, wd_spec] + host.in_specs,
        out_specs=[act, wgu_spec, wd_spec] + host.out_specs,
        out_shape=[jax.ShapeDtypeStruct((2, N_CHUNK, t, CHUNK), BF16),
                   jax.ShapeDtypeStruct((2, N_CHUNK, D_MODEL, CHUNK), F32),
                   jax.ShapeDtypeStruct((N_CHUNK, CHUNK, D_MODEL), F32)] + host.out_shape,
        scratch_shapes=host.scratch, compiler_params=_params(2),
    )(n, df, gu, wd, *host.args)
    return (*res[:3], list(res[3:]))


def _ffn_bwd_x(dgu, wgu, x, g_pre, d_out, name, exchange=None):
    t = x.shape[0]
    tm = _token_tile(t)
    n_i = t // tm
    host = _Host(exchange)

    def body(*refs):
        (dgu_ref, wgu_ref, x_ref, gpre_ref, do_ref, dx_ref, dgpre_ref), ex_refs = host.split(refs, 5, 2, 0)
        i = pl.program_id(0)
        host.at_steps(i, n_i, ex_refs)
        _zero_at_first(i == 0, dgpre_ref)
        dn = _dot_nt(dgu_ref[0, 0], wgu_ref[0, 0]) + _dot_nt(dgu_ref[1, 0], wgu_ref[1, 0])
        for j in range(1, N_CHUNK):
            dn = dn + _dot_nt(dgu_ref[0, j], wgu_ref[0, j]) + _dot_nt(dgu_ref[1, j], wgu_ref[1, j])
        dx, dg = _rms_bwd(x_ref[...], gpre_ref[...], dn)
        dx_ref[...] = do_ref[...] + dx
        dgpre_ref[...] += dg

    tok = pl.BlockSpec((tm, D_MODEL), lambda i: (i, 0))
    vec = pl.BlockSpec((1, D_MODEL), lambda i: (0, 0))
    res = pl.pallas_call(
        body, name=name, grid=(n_i,),
        in_specs=[pl.BlockSpec((2, N_CHUNK, tm, CHUNK), lambda i: (0, 0, i, 0)),
                  pl.BlockSpec((2, N_CHUNK, D_MODEL, CHUNK), lambda i: (0, 0, 0, 0), pipeline_mode=pl.Buffered(1)),
                  tok, vec, tok] + host.in_specs,
        out_specs=[tok, vec] + host.out_specs,
        out_shape=[jax.ShapeDtypeStruct((t, D_MODEL), F32), jax.ShapeDtypeStruct((1, D_MODEL), F32)] + host.out_shape,
        scratch_shapes=host.scratch, compiler_params=_params(1),
    )(dgu, wgu, x, g_pre, d_out, *host.args)
    return (*res[:2], list(res[2:]))


def _mix_in_fwd(x1, g, w_in):
    t = x1.shape[0]
    tm = _token_tile(t)

    def body(x_ref, g_ref, w_ref, xl_ref, gl_ref, q_ref, kv_ref):
        n = _rms_fwd(x_ref[...], g_ref[...]).astype(BF16)
        proj = _dot(n, w_ref[...])
        xl_ref[...] = proj[:, 0:512]
        gl_ref[...] = proj[:, 512:1024]
        q_ref[...] = proj[:, 1024:1536].astype(BF16)
        kv_ref[...] = proj[:, 1536:2048].astype(BF16)

    tok = pl.BlockSpec((tm, D_MODEL), lambda i: (i, 0))
    half = pl.BlockSpec((tm, 512), lambda i: (i, 0))
    return pl.pallas_call(
        body, name="mix_in_fwd", grid=(t // tm,),
        in_specs=[tok, pl.BlockSpec((1, D_MODEL), lambda i: (0, 0)), pl.BlockSpec((D_MODEL, D_IN_DUP), lambda i: (0, 0))],
        out_specs=[half, half, half, half],
        out_shape=[jax.ShapeDtypeStruct((t, 512), F32), jax.ShapeDtypeStruct((t, 512), F32),
                   jax.ShapeDtypeStruct((t, 512), BF16), jax.ShapeDtypeStruct((t, 512), BF16)],
        compiler_params=_params(1),
    )(x1, g, w_in)


def _shift_down(x, before, s):
    if s == 0:
        return x
    rolled = pltpu.roll(x, s, 0)
    ext = jnp.concatenate([before, x[0:8]], axis=0)
    first8 = pltpu.roll(ext, s, 0)[8:16]
    return jnp.concatenate([first8, rolled[8:]], axis=0)


def _shift_up(x, after, s):
    if s == 0:
        return x
    rows = x.shape[0]
    rolled = pltpu.roll(x, rows - s, 0)
    ext = jnp.concatenate([x[rows - 8:rows], after], axis=0)
    last8 = pltpu.roll(ext, 16 - s, 0)[0:8]
    return jnp.concatenate([rolled[:rows - 8], last8], axis=0)


def _log_sigmoid(x):
    e = jnp.exp(-jnp.abs(x))
    log1p_e = jnp.where(e < 0.01, e * (1.0 - e * (0.5 - e * (1.0 / 3.0))), jnp.log(1.0 + e))
    return jnp.minimum(x, 0.0) - log1p_e


def _lru_gates(xc, p_ref, wrg, wig):
    xcb = xc.astype(BF16)
    r = _sigmoid(_dot(xcb, wrg) + p_ref[1:2, :])
    ig = _sigmoid(_dot(xcb, wig) + p_ref[2:3, :])
    ls = _log_sigmoid(p_ref[3:4, :])
    log_a = LRU_C * r * ls
    a = jnp.exp(log_a)
    z = 2.0 * log_a
    series = z * (1.0 + z * (0.5 + z * (1.0 / 6.0 + z * (1.0 / 24.0 + z * (1.0 / 120.0 + z * (1.0 / 720.0))))))
    expm1 = jnp.where(z > -0.1, series, jnp.exp(z) - 1.0)
    mult = jnp.sqrt(-expm1)
    return xcb, r, ig, ls, a, mult


def _conv_taps(x, before, p_ref):
    xc = x * p_ref[7:8, :]
    for s in (1, 2, 3):
        xc = xc + _shift_down(x, before, s) * p_ref[7 - s:8 - s, :]
    return xc + p_ref[0:1, :]


def _lru_block_rows(t):
    return 512 if t >= 1024 else t // 2


def _lru_fwd(xl, p, wrg2, wig2):
    t = xl.shape[0]
    tb = _lru_block_rows(t)

    def body(xl_ref, p_ref, wrg_ref, wig_ref, h_ref, x_tail, h_carry):
        tt = pl.program_id(1)

        @pl.when(tt == 0)
        def _():
            x_tail[...] = jnp.zeros_like(x_tail)
            h_carry[...] = jnp.zeros_like(h_carry)

        x = xl_ref[...]
        xc = _conv_taps(x, x_tail[...], p_ref)
        x_tail[...] = x[tb - 8:tb]
        _, r, ig, ls, a, mult = _lru_gates(xc, p_ref, wrg_ref[0], wig_ref[0])
        u = mult * ig * xc
        row = lax.broadcasted_iota(jnp.int32, (tb, LRU_GROUP), 0)
        s = 1
        while s < tb:
            keep = row >= s
            u = jnp.where(keep, a * pltpu.roll(u, s, 0) + u, u)
            a = jnp.where(keep, a * pltpu.roll(a, s, 0), a)
            s *= 2
        h = u + a * h_carry[0:1, :]
        h_ref[...] = h
        h_carry[...] = jnp.broadcast_to(h[tb - 1:tb], h_carry.shape)

    blk = pl.BlockSpec((tb, LRU_GROUP), lambda g, tt: (tt, g))
    par = pl.BlockSpec((8, LRU_GROUP), lambda g, tt: (0, g))
    wsp = pl.BlockSpec((1, LRU_GROUP, LRU_GROUP), lambda g, tt: (g, 0, 0))
    return pl.pallas_call(
        body, name="lru_fwd", grid=(N_LRU_GROUP, t // tb), in_specs=[blk, par, wsp, wsp], out_specs=blk,
        out_shape=jax.ShapeDtypeStruct((t, D_LRU), F32),
        scratch_shapes=[pltpu.VMEM((8, LRU_GROUP), F32), pltpu.VMEM((8, LRU_GROUP), F32)],
        compiler_params=_params(2),
    )(xl, p, wrg2, wig2)


def _lru_bwd(dy, h, xl, gl, p, wrg2, wig2):
    t = xl.shape[0]
    tb = _lru_block_rows(t)
    n_tb = t // tb
    tb8 = tb // 8

    def body(dy_ref, h_ref, hprev_ref, xl_ref, xprev_ref, gl_ref, p_ref, wrg_ref, wig_ref,
             dxl_ref, dgl_ref, dp_ref, dwrg_ref, dwig_ref, g_carry, a_carry, dxc_head):
        step = pl.program_id(1)
        tt = n_tb - 1 - step
        first = step == 0

        _zero_at_first(first, g_carry, a_carry, dxc_head, dp_ref, dwrg_ref, dwig_ref)

        has_prev = (tt > 0).astype(F32)
        x = xl_ref[...]
        x_before = xprev_ref[...] * has_prev
        xs = [_shift_down(x, x_before, s) for s in range(4)]
        xc = xs[0] * p_ref[7:8, :] + xs[1] * p_ref[6:7, :] + xs[2] * p_ref[5:6, :] + xs[3] * p_ref[4:5, :] + p_ref[0:1, :]
        wrg = wrg_ref[0]
        wig = wig_ref[0]
        xcb, r, ig, ls, a, mult = _lru_gates(xc, p_ref, wrg, wig)

        hh = h_ref[...]
        h_m1 = _shift_down(hh, hprev_ref[...] * has_prev, 1)
        ge, dge = _gelu(gl_ref[...])
        dy = dy_ref[...]
        dgl_ref[...] = dy * hh * dge
        dh = dy * ge

        b = _shift_up(a, a_carry[...], 1)
        row = lax.broadcasted_iota(jnp.int32, (tb, LRU_GROUP), 0)
        g = dh
        s = 1
        while s < tb:
            keep = row < tb - s
            g = jnp.where(keep, b * pltpu.roll(g, tb - s, 0) + g, g)
            b = jnp.where(keep, b * pltpu.roll(b, tb - s, 0), b)
            s *= 2
        g = g + b * g_carry[0:1, :]
        g_carry[...] = jnp.broadcast_to(g[0:1], g_carry.shape)
        a_carry[...] = jnp.broadcast_to(a[0:1], a_carry.shape)

        da = g * h_m1
        dmult = g * ig * xc
        dig = g * mult * xc
        dxc = g * mult * ig
        dlog_a = da * a - dmult * (a * a) / mult
        dr = dlog_a * (LRU_C * ls)
        dls = jnp.sum(dlog_a * (LRU_C * r), axis=0, keepdims=True)
        dlam = dls * _sigmoid(-p_ref[3:4, :])
        dpre_r = dr * r * (1.0 - r)
        dpre_i = dig * ig * (1.0 - ig)
        dprb = dpre_r.astype(BF16)
        dpib = dpre_i.astype(BF16)
        dxc = dxc + _dot_nt(dprb, wrg) + _dot_nt(dpib, wig)
        dwrg_ref[0] += _dot_tn(xcb, dprb)
        dwig_ref[0] += _dot_tn(xcb, dpib)

        after = dxc_head[...]
        dxl = dxc * p_ref[7:8, :]
        for s in (1, 2, 3):
            dxl = dxl + _shift_up(dxc, after, s) * p_ref[7 - s:8 - s, :]
        dxl_ref[...] = dxl
        dxc_head[...] = dxc[0:8]

        rows = [jnp.sum(dxc, axis=0, keepdims=True), jnp.sum(dpre_r, axis=0, keepdims=True),
                jnp.sum(dpre_i, axis=0, keepdims=True), dlam]
        rows += [jnp.sum(dxc * xs[3 - k], axis=0, keepdims=True) for k in range(4)]
        dp_ref[...] += jnp.concatenate(rows, axis=0)

    blk = pl.BlockSpec((tb, LRU_GROUP), lambda g, s: (n_tb - 1 - s, g))
    prev8 = pl.BlockSpec((8, LRU_GROUP), lambda g, s: (jnp.maximum((n_tb - 1 - s) * tb8 - 1, 0), g))
    par = pl.BlockSpec((8, LRU_GROUP), lambda g, s: (0, g))
    wsp = pl.BlockSpec((1, LRU_GROUP, LRU_GROUP), lambda g, s: (g, 0, 0))
    return pl.pallas_call(
        body, name="lru_bwd", grid=(N_LRU_GROUP, n_tb),
        in_specs=[blk, blk, prev8, blk, prev8, blk, par, wsp, wsp], out_specs=[blk, blk, par, wsp, wsp],
        out_shape=[jax.ShapeDtypeStruct((t, D_LRU), F32), jax.ShapeDtypeStruct((t, D_LRU), F32),
                   jax.ShapeDtypeStruct((8, D_LRU), F32),
                   jax.ShapeDtypeStruct((N_LRU_GROUP, LRU_GROUP, LRU_GROUP), F32),
                   jax.ShapeDtypeStruct((N_LRU_GROUP, LRU_GROUP, LRU_GROUP), F32)],
        scratch_shapes=[pltpu.VMEM((8, LRU_GROUP), F32)] * 3,
        compiler_params=_params(2),
    )(dy, h, h, xl, xl, gl, p, wrg2, wig2)


def _attn_bias(first_block):
    qi = jnp.bitwise_and(lax.broadcasted_iota(jnp.int32, (4 * BLOCK_Q, 2 * BLOCK_Q), 0), BLOCK_Q - 1)
    kj = lax.broadcasted_iota(jnp.int32, (4 * BLOCK_Q, 2 * BLOCK_Q), 1)
    rel = qi + BLOCK_Q - kj
    mask = (rel >= 0) & (rel < BLOCK_Q)
    if first_block:
        mask = mask & (kj >= BLOCK_Q)
    return jnp.where(mask, 0.0, MASK_VALUE)


def _sink_column(sinks):
    hrow = lax.broadcasted_iota(jnp.int32, (4 * BLOCK_Q, 1), 0)
    return jnp.where(hrow < BLOCK_Q, sinks[0],
                     jnp.where(hrow < 2 * BLOCK_Q, sinks[1], jnp.where(hrow < 3 * BLOCK_Q, sinks[2], sinks[3])))


def _attn_scores(qv, kvv, n, bias, sk, lo):
    r0 = pl.multiple_of(n * BLOCK_Q, BLOCK_Q)
    rp = pl.multiple_of(jnp.maximum(n - 1, 0) * BLOCK_Q, BLOCK_Q)
    kvb = jnp.concatenate([kvv[pl.ds(rp, BLOCK_Q), :], kvv[pl.ds(r0, BLOCK_Q), :]], axis=0)
    k2 = kvb[:, 0:128]
    v2 = kvb[:, 128:256]
    qs = _stack_heads(qv[pl.ds(r0, BLOCK_Q), :], lo)
    s = _dot_nt(qs, k2) * ATTN_SCALE + bias
    m = jnp.maximum(jnp.max(s, axis=-1, keepdims=True), sk)
    e = jnp.exp(s - m)
    es = jnp.exp(sk - m)
    inv = 1.0 / (jnp.sum(e, axis=-1, keepdims=True) + es)
    return r0, rp, qs, k2, v2, e * inv, es * inv


def _stack_heads(pair2, lo):
    p0 = pair2[:, 0:128]
    p1 = pair2[:, 128:256]
    z = jnp.zeros_like(p0)
    return jnp.concatenate([jnp.where(lo, p0, z), jnp.where(lo, z, p0), jnp.where(lo, p1, z), jnp.where(lo, z, p1)], axis=0)


def _unstack_heads(st, lo):
    b = BLOCK_Q
    return jnp.concatenate([jnp.where(lo, st[0:b], st[b:2 * b]), jnp.where(lo, st[2 * b:3 * b], st[3 * b:4 * b])], axis=1)


def _attn_fwd(q, kv, sinks):
    t = q.shape[0]
    n_blk = t // BLOCK_Q

    def body(q_hbm, kv_hbm, s_ref, o_hbm, qv, kvv, ov, bias0, bias, sem):
        lo = lax.broadcasted_iota(jnp.int32, (BLOCK_Q, 128), 1) < HEAD_DIM
        bias0[...] = _attn_bias(True)
        bias[...] = _attn_bias(False)
        for g in range(2):
            cols = pl.ds(256 * g, 256)
            loads = [pltpu.make_async_copy(q_hbm.at[:, cols], qv, sem.at[0]),
                     pltpu.make_async_copy(kv_hbm.at[:, cols], kvv, sem.at[1])]
            for cp in loads:
                cp.start()
            for cp in loads:
                cp.wait()
            sk = _sink_column([s_ref[0, 4 * g + i] for i in range(4)])

            def block(n, bias_ref):
                r0, _, _, _, v2, prob, _ = _attn_scores(qv, kvv, n, bias_ref[...], sk, lo)
                ov[pl.ds(r0, BLOCK_Q), :] = _unstack_heads(_dot(prob.astype(BF16), v2), lo)

            block(0, bias0)

            def later(n, carry):
                block(n, bias)
                return carry

            lax.fori_loop(1, n_blk, later, 0, unroll=2)
            store = pltpu.make_async_copy(ov, o_hbm.at[:, cols], sem.at[2])
            store.start()
            store.wait()

    return pl.pallas_call(
        body, name="attn_fwd", in_specs=[ANY, ANY, SMEM], out_specs=ANY,
        out_shape=jax.ShapeDtypeStruct((t, D_ATTN), F32),
        scratch_shapes=[pltpu.VMEM((t, 256), BF16), pltpu.VMEM((t, 256), BF16), pltpu.VMEM((t, 256), F32),
                        pltpu.VMEM((4 * BLOCK_Q, 2 * BLOCK_Q), F32), pltpu.VMEM((4 * BLOCK_Q, 2 * BLOCK_Q), F32),
                        pltpu.SemaphoreType.DMA((3,))],
        compiler_params=_params(),
    )(q, kv, sinks)


def _attn_bwd(q, kv, do, sinks, exchange=None):
    t = q.shape[0]
    n_blk = t // BLOCK_Q
    host = _Host(exchange)

    def body(*refs):
        own, ex_refs = host.split(refs, 4, 3, 9)
        q_hbm, kv_hbm, do_hbm, s_ref, dq_hbm, dkv_hbm, dsink_ref, qv, kvv, dov, dqv, dkvv, ds_acc, bias0, bias, sem = own
        host.phase(0, ex_refs)
        lo = lax.broadcasted_iota(jnp.int32, (BLOCK_Q, 128), 1) < HEAD_DIM
        bias0[...] = _attn_bias(True)
        bias[...] = _attn_bias(False)
        for g in range(2):
            cols = pl.ds(256 * g, 256)
            loads = [pltpu.make_async_copy(q_hbm.at[:, cols], qv, sem.at[0]),
                     pltpu.make_async_copy(kv_hbm.at[:, cols], kvv, sem.at[1]),
                     pltpu.make_async_copy(do_hbm.at[:, cols], dov, sem.at[2])]
            for cp in loads:
                cp.start()
            for cp in loads:
                cp.wait()
            sk = _sink_column([s_ref[0, 4 * g + i] for i in range(4)])
            ds_acc[...] = jnp.zeros_like(ds_acc)

            def block(n, bias_ref, has_prev):
                r0, rp, qs, k2, v2, prob, psink = _attn_scores(qv, kvv, n, bias_ref[...], sk, lo)
                pb = prob.astype(BF16)
                dos = _stack_heads(dov[pl.ds(r0, BLOCK_Q), :], lo)
                dp = _dot_nt(dos, v2)
                dsum = jnp.sum(prob * dp, axis=-1, keepdims=True)
                dsb = (prob * (dp - dsum) * ATTN_SCALE).astype(BF16)
                ds_acc[...] -= psink * dsum
                dqv[pl.ds(r0, BLOCK_Q), :] = _unstack_heads(_dot(dsb, k2), lo).astype(BF16)
                dk2 = _dot_tn(dsb, qs)
                dv2 = _dot_tn(pb, dos)
                dkvv[pl.ds(r0, BLOCK_Q), :] = jnp.concatenate([dk2[BLOCK_Q:], dv2[BLOCK_Q:]], axis=1)
                if has_prev:
                    dkvv[pl.ds(rp, BLOCK_Q), :] += jnp.concatenate([dk2[:BLOCK_Q], dv2[:BLOCK_Q]], axis=1)

            block(0, bias0, False)

            def later(n, carry):
                block(n, bias, True)
                return carry

            lax.fori_loop(1, n_blk, later, 0, unroll=2)
            for i in range(4):
                tot = jnp.sum(ds_acc[BLOCK_Q * i:BLOCK_Q * (i + 1), :], axis=0, keepdims=True)
                dsink_ref[4 * g + i:4 * g + i + 1, :] = jnp.broadcast_to(tot, (1, 128))
            stores = [pltpu.make_async_copy(dqv, dq_hbm.at[:, cols], sem.at[0]),
                      pltpu.make_async_copy(dkvv, dkv_hbm.at[:, cols], sem.at[1])]
            for cp in stores:
                cp.start()
            for cp in stores:
                cp.wait()
        if exchange is not None:
            for p in range(1, exchange.n_phases):
                host.phase(p, ex_refs)

    res = pl.pallas_call(
        body, name="attn_bwd", in_specs=[ANY, ANY, ANY, SMEM] + host.in_specs,
        out_specs=[ANY, ANY, pl.BlockSpec(memory_space=pltpu.VMEM)] + host.out_specs,
        out_shape=[jax.ShapeDtypeStruct((t, D_ATTN), BF16), jax.ShapeDtypeStruct((t, 512), F32),
                   jax.ShapeDtypeStruct((8, 128), F32)] + host.out_shape,
        scratch_shapes=[pltpu.VMEM((t, 256), BF16), pltpu.VMEM((t, 256), BF16), pltpu.VMEM((t, 256), BF16),
                        pltpu.VMEM((t, 256), BF16), pltpu.VMEM((t, 256), F32), pltpu.VMEM((4 * BLOCK_Q, 1), F32),
                        pltpu.VMEM((4 * BLOCK_Q, 2 * BLOCK_Q), F32), pltpu.VMEM((4 * BLOCK_Q, 2 * BLOCK_Q), F32),
                        pltpu.SemaphoreType.DMA((3,))] + host.scratch,
        compiler_params=_params(),
    )(q, kv, do, sinks, *host.args)
    return (*res[:3], list(res[3:]))


def _mix_out_fwd(x1, h, gl, o, g_lru, g_attn, g_post, w_o):
    t = x1.shape[0]
    tm = _token_tile(t)

    def body(x_ref, h_ref, gl_ref, o_ref, g1_ref, g2_ref, gp_ref, w_ref, x2_ref, m_ref):
        y = h_ref[...] * _gelu(gl_ref[...])[0]
        yn1 = _rms_fwd(y, g1_ref[...]).astype(BF16)
        yn2 = _rms_fwd(o_ref[...], g2_ref[...]).astype(BF16)
        m = _dot(yn1, w_ref[0:512, :]) + _dot(yn2, w_ref[512:1024, :])
        m_ref[...] = m
        x2_ref[...] = x_ref[...] + _rms_fwd(m, gp_ref[...])

    tok = pl.BlockSpec((tm, D_MODEL), lambda i: (i, 0))
    half = pl.BlockSpec((tm, 512), lambda i: (i, 0))
    vec = pl.BlockSpec((1, D_MODEL), lambda i: (0, 0))
    hvec = pl.BlockSpec((1, 512), lambda i: (0, 0))
    return pl.pallas_call(
        body, name="mix_out_fwd", grid=(t // tm,),
        in_specs=[tok, half, half, half, hvec, hvec, vec, pl.BlockSpec((D_MODEL, D_MODEL), lambda i: (0, 0))],
        out_specs=[tok, tok],
        out_shape=[jax.ShapeDtypeStruct((t, D_MODEL), F32), jax.ShapeDtypeStruct((t, D_MODEL), F32)],
        compiler_params=_params(1),
    )(x1, h, gl, o, g_lru, g_attn, g_post, w_o)


def _mix_out_bwd(dx2, m, h, gl, o, g_lru, g_attn, g_post, w_o):
    t = dx2.shape[0]
    tm = _token_tile(t)

    def body(dx_ref, m_ref, h_ref, gl_ref, o_ref, g1_ref, g2_ref, gp_ref, w_ref,
             dy_ref, do_ref, dw_ref, dgp_ref, dg1_ref, dg2_ref):
        _zero_at_first(pl.program_id(0) == 0, dw_ref, dgp_ref, dg1_ref, dg2_ref)
        dm, dgp = _rms_bwd(m_ref[...], gp_ref[...], dx_ref[...])
        dmb = dm.astype(BF16)
        y = h_ref[...] * _gelu(gl_ref[...])[0]
        o = o_ref[...]
        yn1 = _rms_fwd(y, g1_ref[...]).astype(BF16)
        yn2 = _rms_fwd(o, g2_ref[...]).astype(BF16)
        dw_ref[0:512, :] += _dot_tn(yn1, dmb)
        dw_ref[512:1024, :] += _dot_tn(yn2, dmb)
        dy, dg1 = _rms_bwd(y, g1_ref[...], _dot_nt(dmb, w_ref[0:512, :]))
        do, dg2 = _rms_bwd(o, g2_ref[...], _dot_nt(dmb, w_ref[512:1024, :]))
        dy_ref[...] = dy
        do_ref[...] = do.astype(BF16)
        dgp_ref[...] += dgp
        dg1_ref[...] += dg1
        dg2_ref[...] += dg2

    tok = pl.BlockSpec((tm, D_MODEL), lambda i: (i, 0))
    half = pl.BlockSpec((tm, 512), lambda i: (i, 0))
    vec = pl.BlockSpec((1, D_MODEL), lambda i: (0, 0))
    hvec = pl.BlockSpec((1, 512), lambda i: (0, 0))
    mat = pl.BlockSpec((D_MODEL, D_MODEL), lambda i: (0, 0))
    return pl.pallas_call(
        body, name="mix_out_bwd", grid=(t // tm,),
        in_specs=[tok, tok, half, half, half, hvec, hvec, vec, mat],
        out_specs=[half, half, mat, vec, hvec, hvec],
        out_shape=[jax.ShapeDtypeStruct((t, 512), F32), jax.ShapeDtypeStruct((t, 512), BF16),
                   jax.ShapeDtypeStruct((D_MODEL, D_MODEL), F32), jax.ShapeDtypeStruct((1, D_MODEL), F32),
                   jax.ShapeDtypeStruct((1, 512), F32), jax.ShapeDtypeStruct((1, 512), F32)],
        compiler_params=_params(1),
    )(dx2, m, h, gl, o, g_lru, g_attn, g_post, w_o)


def _mix_in_bwd(dx2, x1, g, dxl, dgl, dq, dkv, w_in, f1, g_post1):
    t = x1.shape[0]
    tm = _token_tile(t)

    def body(dx2_ref, x_ref, g_ref, dxl_ref, dgl_ref, dq_ref, dkv_ref, w_ref, f1_ref, gp1_ref,
             dx1_ref, dw_ref, dg_ref, df1_ref, dgp1_ref):
        _zero_at_first(pl.program_id(0) == 0, dw_ref, dg_ref, dgp1_ref)
        x = x_ref[...]
        nb = _rms_fwd(x, g_ref[...]).astype(BF16)
        lo = lax.broadcasted_iota(jnp.int32, (tm, 128), 1) < HEAD_DIM
        dkv = dkv_ref[...]
        folded = []
        for k in range(4):
            seg = dkv[:, 128 * k:128 * (k + 1)]
            folded.append(jnp.where(lo, seg + pltpu.roll(seg, HEAD_DIM, 1), 0.0).astype(BF16))
        dproj = jnp.concatenate([dxl_ref[...].astype(BF16), dgl_ref[...].astype(BF16), dq_ref[...]] + folded, axis=1)
        dw_ref[...] += _dot_tn(nb, dproj)
        dx, dg = _rms_bwd(x, g_ref[...], _dot_nt(dproj, w_ref[...]))
        dx1 = dx2_ref[...] + dx
        dx1_ref[...] = dx1
        dg_ref[...] += dg
        df1, dgp1 = _rms_bwd(f1_ref[...], gp1_ref[...], 0.5 * dx1)
        df1_ref[...] = df1.astype(BF16)
        dgp1_ref[...] += dgp1

    tok = pl.BlockSpec((tm, D_MODEL), lambda i: (i, 0))
    half = pl.BlockSpec((tm, 512), lambda i: (i, 0))
    vec = pl.BlockSpec((1, D_MODEL), lambda i: (0, 0))
    mat = pl.BlockSpec((D_MODEL, D_IN_DUP), lambda i: (0, 0))
    return pl.pallas_call(
        body, name="mix_in_bwd", grid=(t // tm,),
        in_specs=[tok, tok, vec, half, half, half, half, mat, tok, vec], out_specs=[tok, mat, vec, tok, vec],
        out_shape=[jax.ShapeDtypeStruct((t, D_MODEL), F32), jax.ShapeDtypeStruct((D_MODEL, D_IN_DUP), F32),
                   jax.ShapeDtypeStruct((1, D_MODEL), F32), jax.ShapeDtypeStruct((t, D_MODEL), BF16),
                   jax.ShapeDtypeStruct((1, D_MODEL), F32)],
        compiler_params=_params(1),
    )(dx2, x1, g, dxl, dgl, dq, dkv, w_in, f1, g_post1)


def _row_tile(rows):
    return rows if rows <= 512 else 256


def _chip_sum(grad, from_sibling, other, name):
    _, rows, cols = grad.shape
    tr = _row_tile(rows)

    def body(other_ref, g_ref, s_ref, out_ref):
        out_ref[0] = (g_ref[0, 0] + s_ref[0]).astype(BF16)

    grid_spec = pltpu.PrefetchScalarGridSpec(
        num_scalar_prefetch=1, grid=(3, rows // tr),
        in_specs=[pl.BlockSpec((1, 1, tr, cols), lambda j, i, other: (other[j], other[3], i, 0)),
                  pl.BlockSpec((1, tr, cols), lambda j, i, other: (other[j], i, 0))],
        out_specs=pl.BlockSpec((1, tr, cols), lambda j, i, other: (j, i, 0)))
    return pl.pallas_call(
        body, name=name, grid_spec=grid_spec, out_shape=jax.ShapeDtypeStruct((3, rows, cols), BF16),
        compiler_params=_params(2),
    )(other, grad.reshape(4, 2, rows, cols), from_sibling)


def _adamw(w, g, m, v):
    m = ADAM_B1 * m + (1.0 - ADAM_B1) * g
    v = ADAM_B2 * v + (1.0 - ADAM_B2) * (g * g)
    m_hat = m / (1.0 - ADAM_B1 ** ADAM_STEP)
    v_hat = v / (1.0 - ADAM_B2 ** ADAM_STEP)
    delta = -ADAM_LR * (m_hat / (jnp.sqrt(v_hat) + ADAM_EPS) + ADAM_WD * w)
    return delta, m, v


def _shard_update(grad, from_sibling, from_chips, w, m, v, place, name):
    _, rows, cols = grad.shape
    tr = _row_tile(rows)

    def body(place_ref, g_ref, s_ref, c_ref, w_ref, m_ref, v_ref, go_ref, d_ref, mo_ref, vo_ref):
        g = g_ref[0, 0] + s_ref[0]
        g = g + c_ref[0].astype(F32)
        g = g + c_ref[1].astype(F32)
        g = g + c_ref[2].astype(F32)
        go_ref[...] = g
        d_ref[...], mo_ref[...], vo_ref[...] = _adamw(w_ref[...], g, m_ref[...], v_ref[...])

    flat = pl.BlockSpec((tr, cols), lambda i, place: (i, 0))
    grid_spec = pltpu.PrefetchScalarGridSpec(
        num_scalar_prefetch=1, grid=(rows // tr,),
        in_specs=[pl.BlockSpec((1, 1, tr, cols), lambda i, place: (place[0], place[1], i, 0)),
                  pl.BlockSpec((1, tr, cols), lambda i, place: (place[0], i, 0)),
                  pl.BlockSpec((3, tr, cols), lambda i, place: (0, i, 0)), flat, flat, flat],
        out_specs=[flat, flat, flat, flat])
    return pl.pallas_call(
        body, name=name, grid_spec=grid_spec, out_shape=[jax.ShapeDtypeStruct((rows, cols), F32)] * 4,
        compiler_params=_params(1),
    )(place, grad.reshape(4, 2, rows, cols), from_sibling, from_chips, w, m, v)


GAINS = ("ffn1_pre_g", "ffn1_post_g", "mix_pre_g", "mix_post_g", "ffn2_pre_g", "ffn2_post_g")
HALVES = ("conv_b", "b_rg", "b_ig", "lru_lambda", "g_lru_out", "g_attn_out")
GATES = ("w_rg", "w_ig")
SMALL = GAINS + HALVES + GATES + ("sinks", "conv_w")


def _small_update(gathered, w, m, v):
    n_small = len(SMALL)

    def body(*refs):
        ga_ref, gb_ref, gc_ref, gd_ref = refs[:4]
        wmv = refs[4:4 + 3 * n_small]
        outs = refs[4 + 3 * n_small:4 + 7 * n_small]
        loss_ref = refs[4 + 7 * n_small]

        def total(ref):
            s = ref[0]
            for d in range(1, N_DEV):
                s = s + ref[d]
            return s

        sa, sb, sc, sd = total(ga_ref), total(gb_ref), total(gc_ref), total(gd_ref)
        grads = {}
        for i, k in enumerate(GAINS):
            grads[k] = sa[i:i + 1]
        for i, k in enumerate(HALVES):
            grads[k] = sb[i:i + 1]
        grads["w_rg"], grads["w_ig"] = sc[0:512], sc[512:1024]
        grads["sinks"] = sd[4:5, 0:8]
        grads["conv_w"] = sd[0:4]
        for i, k in enumerate(SMALL):
            g = grads[k]
            outs[4 * i][...] = g
            outs[4 * i + 1][...], outs[4 * i + 2][...], outs[4 * i + 3][...] = _adamw(
                wmv[3 * i][...], g, wmv[3 * i + 1][...], wmv[3 * i + 2][...])
        loss_ref[...] = jnp.broadcast_to(sd[5:6, 0:128], loss_ref.shape)

    operands = list(gathered)
    out_shape = []
    for k in SMALL:
        operands += [w[k], m[k], v[k]]
        out_shape += [jax.ShapeDtypeStruct(w[k].shape, F32)] * 4
    out_shape.append(jax.ShapeDtypeStruct((8, 128), F32))
    res = pl.pallas_call(body, name="small_update", out_shape=out_shape, compiler_params=_params())(*operands)
    parts = [{k: res[4 * i + j] for i, k in enumerate(SMALL)} for j in range(4)]
    return (*parts, res[-1])


def _dup_in_columns(w):
    k0, k1, v0, v1 = w[:, 1536:1600], w[:, 1600:1664], w[:, 1664:1728], w[:, 1728:1792]
    return jnp.concatenate([w[:, :1536], k0, k0, v0, v0, k1, k1, v1, v1], axis=1)


def _undup_in_columns(dw):
    return jnp.concatenate([dw[:, :1536], dw[:, 1536:1600], dw[:, 1792:1856], dw[:, 1664:1728], dw[:, 1920:1984]], axis=1)


def _pair_block_diag(w):
    w = w.reshape(N_LRU_GROUP, 2, 64, 64)
    z = jnp.zeros((N_LRU_GROUP, 64, 64), w.dtype)
    top = jnp.concatenate([w[:, 0], z], axis=2)
    bot = jnp.concatenate([z, w[:, 1]], axis=2)
    return jnp.concatenate([top, bot], axis=1)


def _pair_block_diag_grad(dw2):
    return jnp.stack([dw2[:, :64, :64], dw2[:, 64:, 64:]], axis=1).reshape(512, 64)


def kernel(x, ffn1_pre_g, ffn1_w_gu, ffn1_w_down, ffn1_post_g, mix_pre_g, w_in, conv_w, conv_b, w_rg, b_rg, w_ig, b_ig, lru_lambda, sinks, g_lru_out, g_attn_out, w_o, mix_post_g, ffn2_pre_g, ffn2_w_gu, ffn2_w_down, ffn2_post_g, loss_target, m_ffn1_pre_g, m_ffn1_w_gu, m_ffn1_w_down, m_ffn1_post_g, m_mix_pre_g, m_w_in, m_conv_w, m_conv_b, m_w_rg, m_b_rg, m_w_ig, m_b_ig, m_lru_lambda, m_sinks, m_g_lru_out, m_g_attn_out, m_w_o, m_mix_post_g, m_ffn2_pre_g, m_ffn2_w_gu, m_ffn2_w_down, m_ffn2_post_g, v_ffn1_pre_g, v_ffn1_w_gu, v_ffn1_w_down, v_ffn1_post_g, v_mix_pre_g, v_w_in, v_conv_w, v_conv_b, v_w_rg, v_b_rg, v_w_ig, v_b_ig, v_lru_lambda, v_sinks, v_g_lru_out, v_g_attn_out, v_w_o, v_mix_post_g, v_ffn2_pre_g, v_ffn2_w_gu, v_ffn2_w_down, v_ffn2_post_g):
    args = dict(locals())
    names = ["ffn1_pre_g", "ffn1_w_gu", "ffn1_w_down", "ffn1_post_g", "mix_pre_g", "w_in", "conv_w", "conv_b", "w_rg",
             "b_rg", "w_ig", "b_ig", "lru_lambda", "sinks", "g_lru_out", "g_attn_out", "w_o", "mix_post_g",
             "ffn2_pre_g", "ffn2_w_gu", "ffn2_w_down", "ffn2_post_g"]
    big = ["ffn1_w_gu", "ffn1_w_down", "w_in", "w_o", "ffn2_w_gu", "ffn2_w_down"]
    w = {k: args[k] for k in names}
    mom = {k: args["m_" + k] for k in names}
    var = {k: args["v_" + k] for k in names}
    t = x.shape[1]
    xs = x.reshape(t, D_MODEL)
    target = loss_target.reshape(t, D_MODEL)
    cx, cy, cc = _coords()
    me = 4 * cx + 2 * cy + cc
    other = jnp.stack([2 * (1 - cx) + cy, 2 * cx + (1 - cy), 2 * (1 - cx) + (1 - cy), cc]).astype(jnp.int32)
    place = jnp.stack([2 * cx + cy, cc]).astype(jnp.int32)

    shard2d = {k: w[k].reshape(w[k].shape[1:]) for k in big}
    shard_bf = {k: shard2d[k].astype(BF16) for k in big}
    conv_pad = jnp.pad(conv_w.reshape(4, 64), ((0, 4), (0, 64)))
    (first_w,) = _run_exchanges([_Gather([shard_bf["ffn1_w_gu"], shard_bf["ffn1_w_down"]])], "all_gather_ffn1")
    wgu1 = first_w[0].reshape(2, N_CHUNK, D_MODEL, CHUNK)
    wd1 = first_w[1].reshape(N_CHUNK, CHUNK, D_MODEL)
    rest = _Gather([shard_bf["w_in"], shard_bf["w_o"], shard_bf["ffn2_w_gu"], shard_bf["ffn2_w_down"], conv_pad])

    x1, f1, n1, gu1, gathered = _ffn_fwd(xs, ffn1_pre_g, wgu1, wd1, ffn1_post_g, None, "ffn1_fwd", rest)
    w_in_full = _dup_in_columns(jnp.transpose(gathered[0], (1, 0, 2)).reshape(D_MODEL, D_IN))
    w_o_full = gathered[1].reshape(D_MODEL, D_MODEL)
    wgu2 = gathered[2].reshape(2, N_CHUNK, D_MODEL, CHUNK)
    wd2 = gathered[3].reshape(N_CHUNK, CHUNK, D_MODEL)
    conv_w_full = jnp.transpose(gathered[4][:, 0:4, 0:64], (1, 0, 2)).reshape(4, D_LRU)
    p_lru = jnp.concatenate([conv_b, b_rg, b_ig, lru_lambda, conv_w_full], axis=0)
    wrg2 = _pair_block_diag(w_rg[0]).astype(BF16)
    wig2 = _pair_block_diag(w_ig[0]).astype(BF16)
    xl, gl, q, kv = _mix_in_fwd(x1, mix_pre_g, w_in_full)
    h = _lru_fwd(xl, p_lru, wrg2, wig2)
    o = _attn_fwd(q, kv, sinks)
    x2, mo = _mix_out_fwd(x1, h, gl, o, g_lru_out, g_attn_out, mix_post_g, w_o_full)
    g = {}
    dx3, n2, df2, gu2, g["ffn2_post_g"], loss_parts, _ = _ffn_fwd(x2, ffn2_pre_g, wgu2, wd2, ffn2_post_g, target, "ffn2_fwd")
    loss_local = jnp.sum(loss_parts[::8, 0])

    partial, from_sibling, from_chips = {}, {}, {}

    def chip_sums(keys):
        return [_chip_sum(partial[k], from_sibling[k], other, "chip_sum_" + k) for k in keys]

    dgu2, dwgu2, dwd2, _ = _ffn_bwd_w(n2, df2, gu2, wd2, "ffn2_bwd_w")
    partial["ffn2_w_gu"] = dwgu2.reshape(N_DEV, D_MODEL, CHUNK)
    partial["ffn2_w_down"] = dwd2.reshape(N_DEV, D_FF // N_DEV, D_MODEL)
    ffn2_keys = ["ffn2_w_gu", "ffn2_w_down"]
    dx2, g["ffn2_pre_g"], got = _ffn_bwd_x(dgu2, wgu2, x2, ffn2_pre_g, dx3, "ffn2_bwd_x",
                                           _SiblingExchange([partial[k] for k in ffn2_keys]))
    from_sibling.update(zip(ffn2_keys, got))
    dy, do, dwo, g["mix_post_g"], g["g_lru_out"], g["g_attn_out"] = _mix_out_bwd(
        dx2, mo, h, gl, o, g_lru_out, g_attn_out, mix_post_g, w_o_full)
    dq, dkv, dsink, got = _attn_bwd(q, kv, do, sinks, _ChipExchange(chip_sums(ffn2_keys)))
    from_chips.update(zip(ffn2_keys, got))
    dxl, dgl, dp, dwrg2, dwig2 = _lru_bwd(dy, h, xl, gl, p_lru, wrg2, wig2)
    dx1, dwin_dup, g["mix_pre_g"], df1, g["ffn1_post_g"] = _mix_in_bwd(
        dx2, x1, mix_pre_g, dxl, dgl, dq, dkv, w_in_full, f1, ffn1_post_g)
    partial["w_in"] = jnp.transpose(_undup_in_columns(dwin_dup).reshape(D_MODEL, N_DEV, D_IN // N_DEV), (1, 0, 2))
    partial["w_o"] = dwo.reshape(N_DEV, D_MODEL // N_DEV, D_MODEL)
    mix_keys = ["w_in", "w_o"]
    (got,) = _run_exchanges([_SiblingExchange([partial[k] for k in mix_keys])], "mix_sibling_exchange")
    from_sibling.update(zip(mix_keys, got))
    dgu1, dwgu1, dwd1, got = _ffn_bwd_w(n1, df1, gu1, wd1, "ffn1_bwd_w", _ChipExchange(chip_sums(mix_keys)))
    from_chips.update(zip(mix_keys, got))
    partial["ffn1_w_gu"] = dwgu1.reshape(N_DEV, D_MODEL, CHUNK)
    partial["ffn1_w_down"] = dwd1.reshape(N_DEV, D_FF // N_DEV, D_MODEL)
    ffn1_keys = ["ffn1_w_gu", "ffn1_w_down"]
    (got,) = _run_exchanges([_SiblingExchange([partial[k] for k in ffn1_keys])], "ffn1_sibling_exchange")
    from_sibling.update(zip(ffn1_keys, got))
    dx0, g["ffn1_pre_g"], got = _ffn_bwd_x(dgu1, wgu1, xs, ffn1_pre_g, dx1, "ffn1_bwd_x",
                                           _ChipExchange(chip_sums(ffn1_keys)))
    from_chips.update(zip(ffn1_keys, got))

    grads, delta, new_m, new_v = {}, {}, {}, {}
    for k in big:
        shape = w[k].shape
        res = _shard_update(partial[k], from_sibling[k], from_chips[k], shard2d[k], mom[k].reshape(shape[1:]),
                            var[k].reshape(shape[1:]), place, "update_" + k)
        grads[k], delta[k], new_m[k], new_v[k] = [r.reshape(shape) for r in res]

    zeros2 = jnp.zeros((2, D_MODEL), F32)
    g_gains = jnp.concatenate([g[k] for k in GAINS] + [zeros2], axis=0)
    g_halves = jnp.concatenate([dp[0:4], g["g_lru_out"], g["g_attn_out"], zeros2[:, :D_LRU]], axis=0)
    g_gates = jnp.concatenate([_pair_block_diag_grad(dwrg2), _pair_block_diag_grad(dwig2)], axis=0)
    g_misc = jnp.concatenate([dp[4:8], jnp.pad(dsink[:, 0].reshape(1, 8), ((0, 0), (0, D_LRU - 8))),
                              jnp.pad(loss_local.reshape(1, 1), ((0, 0), (0, D_LRU - 1))), zeros2[:, :D_LRU]], axis=0)
    (gathered_small,) = _run_exchanges([_Gather([g_gains, g_halves, g_gates, g_misc])], "all_gather_small_grads")
    col = me * 64

    def small_view(vals):
        out = {k: vals[k] for k in GAINS + HALVES + ("sinks",)}
        out.update({k: vals[k].reshape(512, 64) for k in GATES})
        out["conv_w"] = lax.dynamic_update_slice(jnp.zeros((4, D_LRU), F32), vals["conv_w"].reshape(4, 64), (0, col))
        return out

    *small, loss_tile = _small_update(gathered_small, small_view(w), small_view(mom), small_view(var))
    for dst, part in zip((grads, delta, new_m, new_v), small):
        for k in SMALL:
            if k == "conv_w":
                dst[k] = lax.dynamic_slice(part[k], (0, col), (4, 64)).reshape(conv_w.shape)
            else:
                dst[k] = part[k].reshape(w[k].shape)
    return (loss_tile[0, 0], dx0.reshape(x.shape), *[grads[k] for k in names], *[delta[k] for k in names],
            *[new_m[k] for k in names], *[new_v[k] for k in names])
```

```python
import functools

import jax
import jax.numpy as jnp
from jax import lax
from jax.experimental import pallas as pl
from jax.experimental.pallas import tpu as pltpu

F32 = jnp.float32
BF16 = jnp.bfloat16

D_MODEL = 1024
D_FF = 2816
N_DEV = 8
N_CHUNK = 4
CHUNK = D_FF // N_CHUNK
D_LRU = 512
D_ATTN = 512
LRU_GROUP = 128
N_LRU_GROUP = D_LRU // LRU_GROUP
HEAD_DIM = 64
BLOCK_Q = 128
D_IN = 1792
D_IN_DUP = 2048
RMS_EPS = 1e-6
LRU_C = 8.0
MASK_VALUE = -1e30
ATTN_SCALE = HEAD_DIM ** -0.5

ADAM_LR = 0.001
ADAM_B1 = 0.9
ADAM_B2 = 0.999
ADAM_EPS = 1e-08
ADAM_WD = 0.01
ADAM_STEP = 10

VMEM_LIMIT_V7X = 56 * 2 ** 20

ANY = pl.BlockSpec(memory_space=pl.ANY)
SMEM = pl.BlockSpec(memory_space=pltpu.SMEM)
MESH = pl.DeviceIdType.MESH


def _params(n_grid=0):
    sem = ("arbitrary",) * n_grid if n_grid else None
    return pltpu.CompilerParams(dimension_semantics=sem, vmem_limit_bytes=VMEM_LIMIT_V7X)


def _dot(a, b):
    return lax.dot_general(a, b, (((1,), (0,)), ((), ())), preferred_element_type=F32)


def _dot_nt(a, b):
    return lax.dot_general(a, b, (((1,), (1,)), ((), ())), preferred_element_type=F32)


def _dot_tn(a, b):
    return lax.dot_general(a, b, (((0,), (0,)), ((), ())), preferred_element_type=F32)


def _sigmoid(x):
    return 1.0 / (1.0 + jnp.exp(-x))


def _rms_fwd(x, g):
    r = lax.rsqrt(jnp.mean(x * x, axis=-1, keepdims=True) + RMS_EPS)
    return x * r * g


def _rms_bwd(x, g, dy):
    r = lax.rsqrt(jnp.mean(x * x, axis=-1, keepdims=True) + RMS_EPS)
    xh = x * r
    dg = jnp.sum(dy * xh, axis=0, keepdims=True)
    dxh = dy * g
    dx = r * (dxh - xh * jnp.mean(dxh * xh, axis=-1, keepdims=True))
    return dx, dg


def _gelu(x):
    c = 0.7978845608028654
    inner = c * (x + 0.044715 * x * x * x)
    th = jnp.tanh(inner)
    ge = 0.5 * x * (1.0 + th)
    dge = 0.5 * (1.0 + th) + 0.5 * x * (1.0 - th * th) * c * (1.0 + 3.0 * 0.044715 * x * x)
    return ge, dge


def _zero_at_first(first, *refs):
    @pl.when(first)
    def _():
        for ref in refs:
            ref[...] = jnp.zeros_like(ref)


def _token_tile(t):
    return 512 if t >= 2048 else t // 2


def _ffn_bwd_tile(t):
    return 1024 if t >= 4096 else t // 2


def _coords():
    return lax.axis_index("x"), lax.axis_index("y"), lax.axis_index("c")


class _Gather:
    n_phases = 3
    at = (0.0, 0.8, 1.0)

    def __init__(self, shards):
        k = len(shards)
        self.arrays = list(shards)
        self.out_shape = [jax.ShapeDtypeStruct((N_DEV,) + s.shape, s.dtype) for s in shards]
        self.scratch = [pltpu.SemaphoreType.DMA((7 * k,)), pltpu.SemaphoreType.DMA((7 * k,)), pltpu.SemaphoreType.DMA((k,))]

    def run(self, phase, ins, outs, sems):
        send_sems, recv_sems, local_sems = sems
        k_arr = len(ins)
        x, y, c = _coords()
        me, sibling = (x, y, c), (x, y, 1 - c)
        chips = [(1 - x, y), (x, 1 - y), (1 - x, 1 - y)]

        def rows(k, dev):
            return outs[k].at[4 * dev[0] + 2 * dev[1] + dev[2]]

        def copy(k, slot, block, to, src=None):
            return pltpu.make_async_remote_copy(
                src_ref=rows(k, block) if src is None else src, dst_ref=rows(k, block),
                send_sem=send_sems.at[7 * k + slot], recv_sem=recv_sems.at[7 * k + slot],
                device_id=to, device_id_type=MESH)

        def mine():
            return [pltpu.make_async_copy(ins[k], rows(k, me), local_sems.at[k]) for k in range(k_arr)]

        def first():
            return [copy(k, slot, me, to, src=ins[k]) for k in range(k_arr)
                    for slot, to in enumerate([sibling] + [(*chip, c) for chip in chips])]

        def passed(j, k):
            return copy(k, 4 + j, (*chips[j], c), sibling)

        if phase == 0:
            for cp in mine() + first():
                cp.start()
        elif phase == 1:
            for j, chip in enumerate(chips):
                for k in range(k_arr):
                    copy(k, 1 + j, (*chip, c), me).wait_recv()
                    passed(j, k).start()
        else:
            for k in range(k_arr):
                copy(k, 0, sibling, me).wait_recv()
                for j, chip in enumerate(chips):
                    copy(k, 4 + j, (*chip, 1 - c), me).wait_recv()
            for cp in first() + [passed(j, k) for j in range(3) for k in range(k_arr)]:
                cp.wait_send()
            for cp in mine():
                cp.wait()


class _SiblingExchange:
    n_phases = 2
    at = (0.0, 1.0)

    def __init__(self, grads):
        k = len(grads)
        self.arrays = list(grads)
        self.out_shape = [jax.ShapeDtypeStruct((4,) + g.shape[1:], g.dtype) for g in grads]
        self.scratch = [pltpu.SemaphoreType.DMA((4 * k,)), pltpu.SemaphoreType.DMA((4 * k,))]

    def run(self, phase, ins, outs, sems):
        send_sems, recv_sems = sems
        x, y, c = _coords()
        copies = [pltpu.make_async_remote_copy(
            src_ref=ins[k].at[2 * q + (1 - c)], dst_ref=outs[k].at[q],
            send_sem=send_sems.at[4 * k + q], recv_sem=recv_sems.at[4 * k + q],
            device_id=(x, y, 1 - c), device_id_type=MESH) for k in range(len(ins)) for q in range(4)]
        for cp in copies:
            if phase == 0:
                cp.start()
            else:
                cp.wait_recv()
                cp.wait_send()


class _ChipExchange:
    n_phases = 2
    at = (0.0, 1.0)

    def __init__(self, chip_sums):
        k = len(chip_sums)
        self.arrays = list(chip_sums)
        self.out_shape = [jax.ShapeDtypeStruct((3,) + s.shape[1:], s.dtype) for s in chip_sums]
        self.scratch = [pltpu.SemaphoreType.DMA((3 * k,)), pltpu.SemaphoreType.DMA((3 * k,))]

    def run(self, phase, ins, outs, sems):
        send_sems, recv_sems = sems
        x, y, c = _coords()
        chips = [(1 - x, y), (x, 1 - y), (1 - x, 1 - y)]
        copies = [pltpu.make_async_remote_copy(
            src_ref=ins[k].at[j], dst_ref=outs[k].at[j],
            send_sem=send_sems.at[3 * k + j], recv_sem=recv_sems.at[3 * k + j],
            device_id=(*chip, c), device_id_type=MESH) for k in range(len(ins)) for j, chip in enumerate(chips)]
        for cp in copies:
            if phase == 0:
                cp.start()
            else:
                cp.wait_recv()
                cp.wait_send()


class _Host:
    def __init__(self, exchange):
        self.ex = exchange
        self.args = [] if exchange is None else exchange.arrays
        self.in_specs = [ANY] * len(self.args)
        self.out_shape = [] if exchange is None else exchange.out_shape
        self.out_specs = [ANY] * len(self.out_shape)
        self.scratch = [] if exchange is None else exchange.scratch

    def split(self, refs, n_in, n_out, n_scratch):
        a, b, s = len(self.args), len(self.out_shape), len(self.scratch)
        own_in, ex_in = refs[:n_in], refs[n_in:n_in + a]
        rest = refs[n_in + a:]
        own_out, ex_out = rest[:n_out], rest[n_out:n_out + b]
        rest = rest[n_out + b:]
        own_scratch, ex_sems = rest[:n_scratch], rest[n_scratch:n_scratch + s]
        return list(own_in) + list(own_out) + list(own_scratch), (ex_in, ex_out, ex_sems)

    def at_steps(self, step, n_steps, ex_refs):
        if self.ex is None:
            return
        for p in range(self.ex.n_phases):
            pl.when(step == int(round(self.ex.at[p] * (n_steps - 1))))(functools.partial(self.ex.run, p, *ex_refs))

    def phase(self, p, ex_refs):
        if self.ex is not None:
            self.ex.run(p, *ex_refs)


def _run_exchanges(exchanges, name):
    hosts = [_Host(ex) for ex in exchanges]
    n_in = [len(h.args) for h in hosts]
    n_out = [len(h.out_shape) for h in hosts]
    n_sc = [len(h.scratch) for h in hosts]

    def body(*refs):
        ins, outs, scr = refs[:sum(n_in)], refs[sum(n_in):sum(n_in) + sum(n_out)], refs[sum(n_in) + sum(n_out):]
        parts = []
        for e in range(len(hosts)):
            parts.append((ins[sum(n_in[:e]):sum(n_in[:e + 1])], outs[sum(n_out[:e]):sum(n_out[:e + 1])],
                          scr[sum(n_sc[:e]):sum(n_sc[:e + 1])]))
        for h, part in zip(hosts, parts):
            h.phase(0, part)
        for h, part in zip(hosts, parts):
            for p in range(1, h.ex.n_phases):
                h.phase(p, part)

    res = pl.pallas_call(
        body, name=name, in_specs=[ANY] * sum(n_in), out_specs=[ANY] * sum(n_out),
        out_shape=[s for h in hosts for s in h.out_shape], scratch_shapes=[s for h in hosts for s in h.scratch],
    )(*[a for h in hosts for a in h.args])
    return [res[sum(n_out[:e]):sum(n_out[:e + 1])] for e in range(len(hosts))]


def _ffn_fwd(x, g_pre, wgu, wd, g_post, target, name, exchange=None):
    t = x.shape[0]
    tm = _token_tile(t)
    n_i = t // tm
    with_loss = target is not None
    host = _Host(exchange)
    n_in, n_out = (6, 6) if with_loss else (5, 4)

    def body(*refs):
        own, ex_refs = host.split(refs, n_in, n_out, 0)
        if with_loss:
            x_ref, gpre_ref, wgu_ref, wd_ref, gpost_ref, tgt_ref, xo_ref, n_ref, df_ref, gu_ref, dgpost_ref, loss_ref = own
            _zero_at_first(pl.program_id(0) == 0, dgpost_ref)
        else:
            x_ref, gpre_ref, wgu_ref, wd_ref, gpost_ref, xo_ref, f_ref, n_ref, gu_ref = own
        host.at_steps(pl.program_id(0), n_i, ex_refs)
        x = x_ref[...]
        n = _rms_fwd(x, gpre_ref[...]).astype(BF16)
        n_ref[...] = n
        f = None
        for j in range(N_CHUNK):
            gate = _dot_nt(n, wgu_ref[0, j])
            up = _dot_nt(n, wgu_ref[1, j])
            gu_ref[0, j] = gate.astype(BF16)
            gu_ref[1, j] = up.astype(BF16)
            part = _dot((gate * _sigmoid(gate) * up).astype(BF16), wd_ref[j])
            f = part if f is None else f + part
        xo = x + 0.5 * _rms_fwd(f, gpost_ref[...])
        if with_loss:
            err = xo - tgt_ref[...]
            d_out = err * (1.0 / D_MODEL)
            xo_ref[...] = d_out
            df, dg = _rms_bwd(f, gpost_ref[...], 0.5 * d_out)
            df_ref[...] = df.astype(BF16)
            dgpost_ref[...] += dg
            part = 0.5 * jnp.sum(jnp.sum(err * err, axis=-1, keepdims=True) * (1.0 / D_MODEL), axis=0, keepdims=True)
            loss_ref[...] = jnp.broadcast_to(part, loss_ref.shape)
        else:
            f_ref[...] = f
            xo_ref[...] = xo

    tok = pl.BlockSpec((tm, D_MODEL), lambda i: (i, 0))
    vec = pl.BlockSpec((1, D_MODEL), lambda i: (0, 0))
    act = pl.BlockSpec((2, N_CHUNK, tm, CHUNK), lambda i: (0, 0, i, 0))
    tok_f32 = jax.ShapeDtypeStruct((t, D_MODEL), F32)
    tok_bf16 = jax.ShapeDtypeStruct((t, D_MODEL), BF16)
    act_shape = jax.ShapeDtypeStruct((2, N_CHUNK, t, CHUNK), BF16)
    in_specs = [tok, vec,
                pl.BlockSpec((2, N_CHUNK, CHUNK, D_MODEL), lambda i: (0, 0, 0, 0), pipeline_mode=pl.Buffered(1)),
                pl.BlockSpec((N_CHUNK, CHUNK, D_MODEL), lambda i: (0, 0, 0), pipeline_mode=pl.Buffered(1)),
                vec]
    args = [x, g_pre, wgu, wd, g_post]
    if with_loss:
        in_specs.append(tok)
        args.append(target)
        out_shape = [tok_f32, tok_bf16, tok_bf16, act_shape, jax.ShapeDtypeStruct((1, D_MODEL), F32),
                     jax.ShapeDtypeStruct((n_i * 8, 128), F32)]
        out_specs = [tok, tok, tok, act, vec, pl.BlockSpec((8, 128), lambda i: (i, 0))]
    else:
        out_shape = [tok_f32, tok_f32, tok_bf16, act_shape]
        out_specs = [tok, tok, tok, act]
    res = pl.pallas_call(
        body, name=name, grid=(n_i,), in_specs=in_specs + host.in_specs, out_specs=out_specs + host.out_specs,
        out_shape=out_shape + host.out_shape, scratch_shapes=host.scratch, compiler_params=_params(1),
    )(*args, *host.args)
    return (*res[:n_out], list(res[n_out:]))


def _ffn_bwd_w(n, df, gu, wd, name, exchange=None):
    t = n.shape[0]
    tm = _ffn_bwd_tile(t)
    n_i = t // tm
    host = _Host(exchange)

    def body(*refs):
        (n_ref, df_ref, gu_ref, wd_ref, dgu_ref, dwgu_ref, dwd_ref), ex_refs = host.split(refs, 4, 3, 0)
        i = pl.program_id(1)
        host.at_steps(pl.program_id(0) * n_i + i, N_CHUNK * n_i, ex_refs)
        _zero_at_first(i == 0, dwgu_ref, dwd_ref)
        nb = n_ref[...]
        dfb = df_ref[...]
        gate = gu_ref[0, 0].astype(F32)
        up = gu_ref[1, 0].astype(F32)
        s = _sigmoid(gate)
        silu = gate * s
        a = (silu * up).astype(BF16)
        da = _dot_nt(dfb, wd_ref[0])
        dup = (da * silu).astype(BF16)
        dgate = (da * up * (s * (1.0 + gate * (1.0 - s)))).astype(BF16)
        dgu_ref[0, 0] = dgate
        dgu_ref[1, 0] = dup
        dwgu_ref[0, 0] += _dot_tn(nb, dgate)
        dwgu_ref[1, 0] += _dot_tn(nb, dup)
        dwd_ref[0] += _dot_tn(a, dfb)

    tok = pl.BlockSpec((tm, D_MODEL), lambda j, i: (i, 0))
    act = pl.BlockSpec((2, 1, tm, CHUNK), lambda j, i: (0, j, i, 0))
    wgu_spec = pl.BlockSpec((2, 1, D_MODEL, CHUNK), lambda j, i: (0, j, 0, 0), pipeline_mode=pl.Buffered(1))
    wd_spec = pl.BlockSpec((1, CHUNK, D_MODEL), lambda j, i: (j, 0, 0), pipeline_mode=pl.Buffered(1))
    res = pl.pallas_call(
        body, name=name, grid=(N_CHUNK, n_i),
        in_specs=[tok, tok, act, wd_spec] + host.in_specs,
        out_specs=[act, wgu_spec, wd_spec] + host.out_specs,
        out_shape=[jax.ShapeDtypeStruct((2, N_CHUNK, t, CHUNK), BF16),
                   jax.ShapeDtypeStruct((2, N_CHUNK, D_MODEL, CHUNK), F32),
                   jax.ShapeDtypeStruct((N_CHUNK, CHUNK, D_MODEL), F32)] + host.out_shape,
        scratch_shapes=host.scratch, compiler_params=_params(2),
    )(n, df, gu, wd, *host.args)
    return (*res[:3], list(res[3:]))


def _ffn_bwd_x(dgu, wgu, x, g_pre, d_out, name, exchange=None):
    t = x.shape[0]
    tm = _token_tile(t)
    n_i = t // tm
    host = _Host(exchange)

    def body(*refs):
        (dgu_ref, wgu_ref, x_ref, gpre_ref, do_ref, dx_ref, dgpre_ref), ex_refs = host.split(refs, 5, 2, 0)
        i = pl.program_id(0)
        host.at_steps(i, n_i, ex_refs)
        _zero_at_first(i == 0, dgpre_ref)
        dn = _dot(dgu_ref[0, 0], wgu_ref[0, 0]) + _dot(dgu_ref[1, 0], wgu_ref[1, 0])
        for j in range(1, N_CHUNK):
            dn = dn + _dot(dgu_ref[0, j], wgu_ref[0, j]) + _dot(dgu_ref[1, j], wgu_ref[1, j])
        dx, dg = _rms_bwd(x_ref[...], gpre_ref[...], dn)
        dx_ref[...] = do_ref[...] + dx
        dgpre_ref[...] += dg

    tok = pl.BlockSpec((tm, D_MODEL), lambda i: (i, 0))
    vec = pl.BlockSpec((1, D_MODEL), lambda i: (0, 0))
    res = pl.pallas_call(
        body, name=name, grid=(n_i,),
        in_specs=[pl.BlockSpec((2, N_CHUNK, tm, CHUNK), lambda i: (0, 0, i, 0)),
                  pl.BlockSpec((2, N_CHUNK, CHUNK, D_MODEL), lambda i: (0, 0, 0, 0), pipeline_mode=pl.Buffered(1)),
                  tok, vec, tok] + host.in_specs,
        out_specs=[tok, vec] + host.out_specs,
        out_shape=[jax.ShapeDtypeStruct((t, D_MODEL), F32), jax.ShapeDtypeStruct((1, D_MODEL), F32)] + host.out_shape,
        scratch_shapes=host.scratch, compiler_params=_params(1),
    )(dgu, wgu, x, g_pre, d_out, *host.args)
    return (*res[:2], list(res[2:]))


def _mix_in_fwd(x1, g, w_in):
    t = x1.shape[0]
    tm = _token_tile(t)

    def body(x_ref, g_ref, w_ref, xl_ref, gl_ref, q_ref, kv_ref):
        n = _rms_fwd(x_ref[...], g_ref[...]).astype(BF16)
        proj = _dot_nt(n, w_ref[...])
        xl_ref[...] = proj[:, 0:512]
        gl_ref[...] = proj[:, 512:1024]
        q_ref[...] = proj[:, 1024:1536].astype(BF16)
        kv_ref[...] = proj[:, 1536:2048].astype(BF16)

    tok = pl.BlockSpec((tm, D_MODEL), lambda i: (i, 0))
    half = pl.BlockSpec((tm, 512), lambda i: (i, 0))
    return pl.pallas_call(
        body, name="mix_in_fwd", grid=(t // tm,),
        in_specs=[tok, pl.BlockSpec((1, D_MODEL), lambda i: (0, 0)), pl.BlockSpec((D_IN_DUP, D_MODEL), lambda i: (0, 0))],
        out_specs=[half, half, half, half],
        out_shape=[jax.ShapeDtypeStruct((t, 512), F32), jax.ShapeDtypeStruct((t, 512), F32),
                   jax.ShapeDtypeStruct((t, 512), BF16), jax.ShapeDtypeStruct((t, 512), BF16)],
        compiler_params=_params(1),
    )(x1, g, w_in)


def _shift_down(x, before, s):
    if s == 0:
        return x
    rolled = pltpu.roll(x, s, 0)
    ext = jnp.concatenate([before, x[0:8]], axis=0)
    first8 = pltpu.roll(ext, s, 0)[8:16]
    return jnp.concatenate([first8, rolled[8:]], axis=0)


def _shift_up(x, after, s):
    if s == 0:
        return x
    rows = x.shape[0]
    rolled = pltpu.roll(x, rows - s, 0)
    ext = jnp.concatenate([x[rows - 8:rows], after], axis=0)
    last8 = pltpu.roll(ext, 16 - s, 0)[0:8]
    return jnp.concatenate([rolled[:rows - 8], last8], axis=0)


def _log_sigmoid(x):
    e = jnp.exp(-jnp.abs(x))
    log1p_e = jnp.where(e < 0.01, e * (1.0 - e * (0.5 - e * (1.0 / 3.0))), jnp.log(1.0 + e))
    return jnp.minimum(x, 0.0) - log1p_e


def _lru_gates(xc, p_ref, wrg, wig):
    xcb = xc.astype(BF16)
    r = _sigmoid(_dot(xcb, wrg) + p_ref[1:2, :])
    ig = _sigmoid(_dot(xcb, wig) + p_ref[2:3, :])
    ls = _log_sigmoid(p_ref[3:4, :])
    log_a = LRU_C * r * ls
    a = jnp.exp(log_a)
    z = 2.0 * log_a
    series = z * (1.0 + z * (0.5 + z * (1.0 / 6.0 + z * (1.0 / 24.0 + z * (1.0 / 120.0 + z * (1.0 / 720.0))))))
    expm1 = jnp.where(z > -0.1, series, jnp.exp(z) - 1.0)
    mult = jnp.sqrt(-expm1)
    return xcb, r, ig, ls, a, mult


def _conv_taps(x, before, p_ref):
    xc = x * p_ref[7:8, :]
    for s in (1, 2, 3):
        xc = xc + _shift_down(x, before, s) * p_ref[7 - s:8 - s, :]
    return xc + p_ref[0:1, :]


def _lru_block_rows(t):
    return 512 if t >= 1024 else t // 2


def _lru_fwd(xl, p, wrg2, wig2):
    t = xl.shape[0]
    tb = _lru_block_rows(t)

    def body(xl_ref, p_ref, wrg_ref, wig_ref, h_ref, x_tail, h_carry):
        tt = pl.program_id(1)

        @pl.when(tt == 0)
        def _():
            x_tail[...] = jnp.zeros_like(x_tail)
            h_carry[...] = jnp.zeros_like(h_carry)

        x = xl_ref[...]
        xc = _conv_taps(x, x_tail[...], p_ref)
        x_tail[...] = x[tb - 8:tb]
        _, r, ig, ls, a, mult = _lru_gates(xc, p_ref, wrg_ref[0], wig_ref[0])
        u = mult * ig * xc
        row = lax.broadcasted_iota(jnp.int32, (tb, LRU_GROUP), 0)
        s = 1
        while s < tb:
            keep = row >= s
            u = jnp.where(keep, a * pltpu.roll(u, s, 0) + u, u)
            a = jnp.where(keep, a * pltpu.roll(a, s, 0), a)
            s *= 2
        h = u + a * h_carry[0:1, :]
        h_ref[...] = h
        h_carry[...] = jnp.broadcast_to(h[tb - 1:tb], h_carry.shape)

    blk = pl.BlockSpec((tb, LRU_GROUP), lambda g, tt: (tt, g))
    par = pl.BlockSpec((8, LRU_GROUP), lambda g, tt: (0, g))
    wsp = pl.BlockSpec((1, LRU_GROUP, LRU_GROUP), lambda g, tt: (g, 0, 0))
    return pl.pallas_call(
        body, name="lru_fwd", grid=(N_LRU_GROUP, t // tb), in_specs=[blk, par, wsp, wsp], out_specs=blk,
        out_shape=jax.ShapeDtypeStruct((t, D_LRU), F32),
        scratch_shapes=[pltpu.VMEM((8, LRU_GROUP), F32), pltpu.VMEM((8, LRU_GROUP), F32)],
        compiler_params=_params(2),
    )(xl, p, wrg2, wig2)


def _lru_bwd(dy, h, xl, gl, p, wrg2, wig2):
    t = xl.shape[0]
    tb = _lru_block_rows(t)
    n_tb = t // tb
    tb8 = tb // 8

    def body(dy_ref, h_ref, hprev_ref, xl_ref, xprev_ref, gl_ref, p_ref, wrg_ref, wig_ref,
             dxl_ref, dgl_ref, dp_ref, dwrg_ref, dwig_ref, g_carry, a_carry, dxc_head):
        step = pl.program_id(1)
        tt = n_tb - 1 - step
        first = step == 0

        _zero_at_first(first, g_carry, a_carry, dxc_head, dp_ref, dwrg_ref, dwig_ref)

        has_prev = (tt > 0).astype(F32)
        x = xl_ref[...]
        x_before = xprev_ref[...] * has_prev
        xs = [_shift_down(x, x_before, s) for s in range(4)]
        xc = xs[0] * p_ref[7:8, :] + xs[1] * p_ref[6:7, :] + xs[2] * p_ref[5:6, :] + xs[3] * p_ref[4:5, :] + p_ref[0:1, :]
        wrg = wrg_ref[0]
        wig = wig_ref[0]
        xcb, r, ig, ls, a, mult = _lru_gates(xc, p_ref, wrg, wig)

        hh = h_ref[...]
        h_m1 = _shift_down(hh, hprev_ref[...] * has_prev, 1)
        ge, dge = _gelu(gl_ref[...])
        dy = dy_ref[...]
        dgl_ref[...] = dy * hh * dge
        dh = dy * ge

        b = _shift_up(a, a_carry[...], 1)
        row = lax.broadcasted_iota(jnp.int32, (tb, LRU_GROUP), 0)
        g = dh
        s = 1
        while s < tb:
            keep = row < tb - s
            g = jnp.where(keep, b * pltpu.roll(g, tb - s, 0) + g, g)
            b = jnp.where(keep, b * pltpu.roll(b, tb - s, 0), b)
            s *= 2
        g = g + b * g_carry[0:1, :]
        g_carry[...] = jnp.broadcast_to(g[0:1], g_carry.shape)
        a_carry[...] = jnp.broadcast_to(a[0:1], a_carry.shape)

        da = g * h_m1
        dmult = g * ig * xc
        dig = g * mult * xc
        dxc = g * mult * ig
        dlog_a = da * a - dmult * (a * a) / mult
        dr = dlog_a * (LRU_C * ls)
        dls = jnp.sum(dlog_a * (LRU_C * r), axis=0, keepdims=True)
        dlam = dls * _sigmoid(-p_ref[3:4, :])
        dpre_r = dr * r * (1.0 - r)
        dpre_i = dig * ig * (1.0 - ig)
        dprb = dpre_r.astype(BF16)
        dpib = dpre_i.astype(BF16)
        dxc = dxc + _dot_nt(dprb, wrg) + _dot_nt(dpib, wig)
        dwrg_ref[0] += _dot_tn(xcb, dprb)
        dwig_ref[0] += _dot_tn(xcb, dpib)

        after = dxc_head[...]
        dxl = dxc * p_ref[7:8, :]
        for s in (1, 2, 3):
            dxl = dxl + _shift_up(dxc, after, s) * p_ref[7 - s:8 - s, :]
        dxl_ref[...] = dxl
        dxc_head[...] = dxc[0:8]

        rows = [jnp.sum(dxc, axis=0, keepdims=True), jnp.sum(dpre_r, axis=0, keepdims=True),
                jnp.sum(dpre_i, axis=0, keepdims=True), dlam]
        rows += [jnp.sum(dxc * xs[3 - k], axis=0, keepdims=True) for k in range(4)]
        dp_ref[...] += jnp.concatenate(rows, axis=0)

    blk = pl.BlockSpec((tb, LRU_GROUP), lambda g, s: (n_tb - 1 - s, g))
    prev8 = pl.BlockSpec((8, LRU_GROUP), lambda g, s: (jnp.maximum((n_tb - 1 - s) * tb8 - 1, 0), g))
    par = pl.BlockSpec((8, LRU_GROUP), lambda g, s: (0, g))
    wsp = pl.BlockSpec((1, LRU_GROUP, LRU_GROUP), lambda g, s: (g, 0, 0))
    return pl.pallas_call(
        body, name="lru_bwd", grid=(N_LRU_GROUP, n_tb),
        in_specs=[blk, blk, prev8, blk, prev8, blk, par, wsp, wsp], out_specs=[blk, blk, par, wsp, wsp],
        out_shape=[jax.ShapeDtypeStruct((t, D_LRU), F32), jax.ShapeDtypeStruct((t, D_LRU), F32),
                   jax.ShapeDtypeStruct((8, D_LRU), F32),
                   jax.ShapeDtypeStruct((N_LRU_GROUP, LRU_GROUP, LRU_GROUP), F32),
                   jax.ShapeDtypeStruct((N_LRU_GROUP, LRU_GROUP, LRU_GROUP), F32)],
        scratch_shapes=[pltpu.VMEM((8, LRU_GROUP), F32)] * 3,
        compiler_params=_params(2),
    )(dy, h, h, xl, xl, gl, p, wrg2, wig2)


def _attn_bias(first_block):
    qi = jnp.bitwise_and(lax.broadcasted_iota(jnp.int32, (4 * BLOCK_Q, 2 * BLOCK_Q), 0), BLOCK_Q - 1)
    kj = lax.broadcasted_iota(jnp.int32, (4 * BLOCK_Q, 2 * BLOCK_Q), 1)
    rel = qi + BLOCK_Q - kj
    mask = (rel >= 0) & (rel < BLOCK_Q)
    if first_block:
        mask = mask & (kj >= BLOCK_Q)
    return jnp.where(mask, 0.0, MASK_VALUE)


def _sink_column(sinks):
    hrow = lax.broadcasted_iota(jnp.int32, (4 * BLOCK_Q, 1), 0)
    return jnp.where(hrow < BLOCK_Q, sinks[0],
                     jnp.where(hrow < 2 * BLOCK_Q, sinks[1], jnp.where(hrow < 3 * BLOCK_Q, sinks[2], sinks[3])))


def _attn_scores(qv, kvv, n, bias, sk, lo):
    r0 = pl.multiple_of(n * BLOCK_Q, BLOCK_Q)
    rp = pl.multiple_of(jnp.maximum(n - 1, 0) * BLOCK_Q, BLOCK_Q)
    kvb = jnp.concatenate([kvv[pl.ds(rp, BLOCK_Q), :], kvv[pl.ds(r0, BLOCK_Q), :]], axis=0)
    k2 = kvb[:, 0:128]
    v2 = kvb[:, 128:256]
    qs = _stack_heads(qv[pl.ds(r0, BLOCK_Q), :], lo)
    s = _dot_nt(qs, k2) * ATTN_SCALE + bias
    m = jnp.maximum(jnp.max(s, axis=-1, keepdims=True), sk)
    e = jnp.exp(s - m)
    es = jnp.exp(sk - m)
    inv = 1.0 / (jnp.sum(e, axis=-1, keepdims=True) + es)
    return r0, rp, qs, k2, v2, e * inv, es * inv


def _stack_heads(pair2, lo):
    p0 = pair2[:, 0:128]
    p1 = pair2[:, 128:256]
    z = jnp.zeros_like(p0)
    return jnp.concatenate([jnp.where(lo, p0, z), jnp.where(lo, z, p0), jnp.where(lo, p1, z), jnp.where(lo, z, p1)], axis=0)


def _unstack_heads(st, lo):
    b = BLOCK_Q
    return jnp.concatenate([jnp.where(lo, st[0:b], st[b:2 * b]), jnp.where(lo, st[2 * b:3 * b], st[3 * b:4 * b])], axis=1)


def _attn_fwd(q, kv, sinks):
    t = q.shape[0]
    n_blk = t // BLOCK_Q

    def body(q_hbm, kv_hbm, s_ref, o_hbm, qv, kvv, ov, bias0, bias, sem):
        lo = lax.broadcasted_iota(jnp.int32, (BLOCK_Q, 128), 1) < HEAD_DIM
        bias0[...] = _attn_bias(True)
        bias[...] = _attn_bias(False)
        for g in range(2):
            cols = pl.ds(256 * g, 256)
            loads = [pltpu.make_async_copy(q_hbm.at[:, cols], qv, sem.at[0]),
                     pltpu.make_async_copy(kv_hbm.at[:, cols], kvv, sem.at[1])]
            for cp in loads:
                cp.start()
            for cp in loads:
                cp.wait()
            sk = _sink_column([s_ref[0, 4 * g + i] for i in range(4)])

            def block(n, bias_ref):
                r0, _, _, _, v2, prob, _ = _attn_scores(qv, kvv, n, bias_ref[...], sk, lo)
                ov[pl.ds(r0, BLOCK_Q), :] = _unstack_heads(_dot(prob.astype(BF16), v2), lo)

            block(0, bias0)

            def later(n, carry):
                block(n, bias)
                return carry

            lax.fori_loop(1, n_blk, later, 0, unroll=2)
            store = pltpu.make_async_copy(ov, o_hbm.at[:, cols], sem.at[2])
            store.start()
            store.wait()

    return pl.pallas_call(
        body, name="attn_fwd", in_specs=[ANY, ANY, SMEM], out_specs=ANY,
        out_shape=jax.ShapeDtypeStruct((t, D_ATTN), F32),
        scratch_shapes=[pltpu.VMEM((t, 256), BF16), pltpu.VMEM((t, 256), BF16), pltpu.VMEM((t, 256), F32),
                        pltpu.VMEM((4 * BLOCK_Q, 2 * BLOCK_Q), F32), pltpu.VMEM((4 * BLOCK_Q, 2 * BLOCK_Q), F32),
                        pltpu.SemaphoreType.DMA((3,))],
        compiler_params=_params(),
    )(q, kv, sinks)


def _attn_bwd(q, kv, do, sinks, exchange=None):
    t = q.shape[0]
    n_blk = t // BLOCK_Q
    host = _Host(exchange)

    def body(*refs):
        own, ex_refs = host.split(refs, 4, 3, 9)
        q_hbm, kv_hbm, do_hbm, s_ref, dq_hbm, dkv_hbm, dsink_ref, qv, kvv, dov, dqv, dkvv, ds_acc, bias0, bias, sem = own
        host.phase(0, ex_refs)
        lo = lax.broadcasted_iota(jnp.int32, (BLOCK_Q, 128), 1) < HEAD_DIM
        bias0[...] = _attn_bias(True)
        bias[...] = _attn_bias(False)
        for g in range(2):
            cols = pl.ds(256 * g, 256)
            loads = [pltpu.make_async_copy(q_hbm.at[:, cols], qv, sem.at[0]),
                     pltpu.make_async_copy(kv_hbm.at[:, cols], kvv, sem.at[1]),
                     pltpu.make_async_copy(do_hbm.at[:, cols], dov, sem.at[2])]
            for cp in loads:
                cp.start()
            for cp in loads:
                cp.wait()
            sk = _sink_column([s_ref[0, 4 * g + i] for i in range(4)])
            ds_acc[...] = jnp.zeros_like(ds_acc)

            def block(n, bias_ref, has_prev):
                r0, rp, qs, k2, v2, prob, psink = _attn_scores(qv, kvv, n, bias_ref[...], sk, lo)
                pb = prob.astype(BF16)
                dos = _stack_heads(dov[pl.ds(r0, BLOCK_Q), :], lo)
                dp = _dot_nt(dos, v2)
                dsum = jnp.sum(prob * dp, axis=-1, keepdims=True)
                dsb = (prob * (dp - dsum) * ATTN_SCALE).astype(BF16)
                ds_acc[...] -= psink * dsum
                dqv[pl.ds(r0, BLOCK_Q), :] = _unstack_heads(_dot(dsb, k2), lo).astype(BF16)
                dk2 = _dot_tn(dsb, qs)
                dv2 = _dot_tn(pb, dos)
                dkvv[pl.ds(r0, BLOCK_Q), :] = jnp.concatenate([dk2[BLOCK_Q:], dv2[BLOCK_Q:]], axis=1)
                if has_prev:
                    dkvv[pl.ds(rp, BLOCK_Q), :] += jnp.concatenate([dk2[:BLOCK_Q], dv2[:BLOCK_Q]], axis=1)

            block(0, bias0, False)

            def later(n, carry):
                block(n, bias, True)
                return carry

            lax.fori_loop(1, n_blk, later, 0, unroll=2)
            for i in range(4):
                tot = jnp.sum(ds_acc[BLOCK_Q * i:BLOCK_Q * (i + 1), :], axis=0, keepdims=True)
                dsink_ref[4 * g + i:4 * g + i + 1, :] = jnp.broadcast_to(tot, (1, 128))
            stores = [pltpu.make_async_copy(dqv, dq_hbm.at[:, cols], sem.at[0]),
                      pltpu.make_async_copy(dkvv, dkv_hbm.at[:, cols], sem.at[1])]
            for cp in stores:
                cp.start()
            for cp in stores:
                cp.wait()
        if exchange is not None:
            for p in range(1, exchange.n_phases):
                host.phase(p, ex_refs)

    res = pl.pallas_call(
        body, name="attn_bwd", in_specs=[ANY, ANY, ANY, SMEM] + host.in_specs,
        out_specs=[ANY, ANY, pl.BlockSpec(memory_space=pltpu.VMEM)] + host.out_specs,
        out_shape=[jax.ShapeDtypeStruct((t, D_ATTN), BF16), jax.ShapeDtypeStruct((t, 512), F32),
                   jax.ShapeDtypeStruct((8, 128), F32)] + host.out_shape,
        scratch_shapes=[pltpu.VMEM((t, 256), BF16), pltpu.VMEM((t, 256), BF16), pltpu.VMEM((t, 256), BF16),
                        pltpu.VMEM((t, 256), BF16), pltpu.VMEM((t, 256), F32), pltpu.VMEM((4 * BLOCK_Q, 1), F32),
                        pltpu.VMEM((4 * BLOCK_Q, 2 * BLOCK_Q), F32), pltpu.VMEM((4 * BLOCK_Q, 2 * BLOCK_Q), F32),
                        pltpu.SemaphoreType.DMA((3,))] + host.scratch,
        compiler_params=_params(),
    )(q, kv, do, sinks, *host.args)
    return (*res[:3], list(res[3:]))


def _mix_out_fwd(x1, h, gl, o, g_lru, g_attn, g_post, w_o):
    t = x1.shape[0]
    tm = _token_tile(t)

    def body(x_ref, h_ref, gl_ref, o_ref, g1_ref, g2_ref, gp_ref, w_ref, x2_ref, m_ref):
        y = h_ref[...] * _gelu(gl_ref[...])[0]
        yn1 = _rms_fwd(y, g1_ref[...]).astype(BF16)
        yn2 = _rms_fwd(o_ref[...], g2_ref[...]).astype(BF16)
        m = _dot(yn1, w_ref[0:512, :]) + _dot(yn2, w_ref[512:1024, :])
        m_ref[...] = m
        x2_ref[...] = x_ref[...] + _rms_fwd(m, gp_ref[...])

    tok = pl.BlockSpec((tm, D_MODEL), lambda i: (i, 0))
    half = pl.BlockSpec((tm, 512), lambda i: (i, 0))
    vec = pl.BlockSpec((1, D_MODEL), lambda i: (0, 0))
    hvec = pl.BlockSpec((1, 512), lambda i: (0, 0))
    return pl.pallas_call(
        body, name="mix_out_fwd", grid=(t // tm,),
        in_specs=[tok, half, half, half, hvec, hvec, vec, pl.BlockSpec((D_MODEL, D_MODEL), lambda i: (0, 0))],
        out_specs=[tok, tok],
        out_shape=[jax.ShapeDtypeStruct((t, D_MODEL), F32), jax.ShapeDtypeStruct((t, D_MODEL), F32)],
        compiler_params=_params(1),
    )(x1, h, gl, o, g_lru, g_attn, g_post, w_o)


def _mix_out_bwd(dx2, m, h, gl, o, g_lru, g_attn, g_post, w_o):
    t = dx2.shape[0]
    tm = _token_tile(t)

    def body(dx_ref, m_ref, h_ref, gl_ref, o_ref, g1_ref, g2_ref, gp_ref, w_ref,
             dy_ref, do_ref, dw_ref, dgp_ref, dg1_ref, dg2_ref):
        _zero_at_first(pl.program_id(0) == 0, dw_ref, dgp_ref, dg1_ref, dg2_ref)
        dm, dgp = _rms_bwd(m_ref[...], gp_ref[...], dx_ref[...])
        dmb = dm.astype(BF16)
        y = h_ref[...] * _gelu(gl_ref[...])[0]
        o = o_ref[...]
        yn1 = _rms_fwd(y, g1_ref[...]).astype(BF16)
        yn2 = _rms_fwd(o, g2_ref[...]).astype(BF16)
        dw_ref[0:512, :] += _dot_tn(yn1, dmb)
        dw_ref[512:1024, :] += _dot_tn(yn2, dmb)
        dy, dg1 = _rms_bwd(y, g1_ref[...], _dot_nt(dmb, w_ref[0:512, :]))
        do, dg2 = _rms_bwd(o, g2_ref[...], _dot_nt(dmb, w_ref[512:1024, :]))
        dy_ref[...] = dy
        do_ref[...] = do.astype(BF16)
        dgp_ref[...] += dgp
        dg1_ref[...] += dg1
        dg2_ref[...] += dg2

    tok = pl.BlockSpec((tm, D_MODEL), lambda i: (i, 0))
    half = pl.BlockSpec((tm, 512), lambda i: (i, 0))
    vec = pl.BlockSpec((1, D_MODEL), lambda i: (0, 0))
    hvec = pl.BlockSpec((1, 512), lambda i: (0, 0))
    mat = pl.BlockSpec((D_MODEL, D_MODEL), lambda i: (0, 0))
    return pl.pallas_call(
        body, name="mix_out_bwd", grid=(t // tm,),
        in_specs=[tok, tok, half, half, half, hvec, hvec, vec, mat],
        out_specs=[half, half, mat, vec, hvec, hvec],
        out_shape=[jax.ShapeDtypeStruct((t, 512), F32), jax.ShapeDtypeStruct((t, 512), BF16),
                   jax.ShapeDtypeStruct((D_MODEL, D_MODEL), F32), jax.ShapeDtypeStruct((1, D_MODEL), F32),
                   jax.ShapeDtypeStruct((1, 512), F32), jax.ShapeDtypeStruct((1, 512), F32)],
        compiler_params=_params(1),
    )(dx2, m, h, gl, o, g_lru, g_attn, g_post, w_o)


def _mix_in_bwd(dx2, x1, g, dxl, dgl, dq, dkv, w_in, f1, g_post1):
    t = x1.shape[0]
    tm = _token_tile(t)

    def body(dx2_ref, x_ref, g_ref, dxl_ref, dgl_ref, dq_ref, dkv_ref, w_ref, f1_ref, gp1_ref,
             dx1_ref, dw_ref, dg_ref, df1_ref, dgp1_ref):
        _zero_at_first(pl.program_id(0) == 0, dw_ref, dg_ref, dgp1_ref)
        x = x_ref[...]
        nb = _rms_fwd(x, g_ref[...]).astype(BF16)
        lo = lax.broadcasted_iota(jnp.int32, (tm, 128), 1) < HEAD_DIM
        dkv = dkv_ref[...]
        folded = []
        for k in range(4):
            seg = dkv[:, 128 * k:128 * (k + 1)]
            folded.append(jnp.where(lo, seg + pltpu.roll(seg, HEAD_DIM, 1), 0.0).astype(BF16))
        dproj = jnp.concatenate([dxl_ref[...].astype(BF16), dgl_ref[...].astype(BF16), dq_ref[...]] + folded, axis=1)
        dw_ref[...] += _dot_tn(nb, dproj)
        dx, dg = _rms_bwd(x, g_ref[...], _dot(dproj, w_ref[...]))
        dx1 = dx2_ref[...] + dx
        dx1_ref[...] = dx1
        dg_ref[...] += dg
        df1, dgp1 = _rms_bwd(f1_ref[...], gp1_ref[...], 0.5 * dx1)
        df1_ref[...] = df1.astype(BF16)
        dgp1_ref[...] += dgp1

    tok = pl.BlockSpec((tm, D_MODEL), lambda i: (i, 0))
    half = pl.BlockSpec((tm, 512), lambda i: (i, 0))
    vec = pl.BlockSpec((1, D_MODEL), lambda i: (0, 0))
    mat = pl.BlockSpec((D_IN_DUP, D_MODEL), lambda i: (0, 0))
    dmat = pl.BlockSpec((D_MODEL, D_IN_DUP), lambda i: (0, 0))
    return pl.pallas_call(
        body, name="mix_in_bwd", grid=(t // tm,),
        in_specs=[tok, tok, vec, half, half, half, half, mat, tok, vec], out_specs=[tok, dmat, vec, tok, vec],
        out_shape=[jax.ShapeDtypeStruct((t, D_MODEL), F32), jax.ShapeDtypeStruct((D_MODEL, D_IN_DUP), F32),
                   jax.ShapeDtypeStruct((1, D_MODEL), F32), jax.ShapeDtypeStruct((t, D_MODEL), BF16),
                   jax.ShapeDtypeStruct((1, D_MODEL), F32)],
        compiler_params=_params(1),
    )(dx2, x1, g, dxl, dgl, dq, dkv, w_in, f1, g_post1)


def _row_tile(rows):
    return rows if rows <= 512 else rows // 2


def _chip_sum(grad, from_sibling, other, name):
    _, rows, cols = grad.shape
    tr = _row_tile(rows)

    def body(other_ref, g_ref, s_ref, out_ref):
        out_ref[0] = (g_ref[0, 0] + s_ref[0]).astype(BF16)

    grid_spec = pltpu.PrefetchScalarGridSpec(
        num_scalar_prefetch=1, grid=(3, rows // tr),
        in_specs=[pl.BlockSpec((1, 1, tr, cols), lambda j, i, other: (other[j], other[3], i, 0)),
                  pl.BlockSpec((1, tr, cols), lambda j, i, other: (other[j], i, 0))],
        out_specs=pl.BlockSpec((1, tr, cols), lambda j, i, other: (j, i, 0)))
    return pl.pallas_call(
        body, name=name, grid_spec=grid_spec, out_shape=jax.ShapeDtypeStruct((3, rows, cols), BF16),
        compiler_params=_params(2),
    )(other, grad.reshape(4, 2, rows, cols), from_sibling)


def _adamw(w, g, m, v):
    m = ADAM_B1 * m + (1.0 - ADAM_B1) * g
    v = ADAM_B2 * v + (1.0 - ADAM_B2) * (g * g)
    m_hat = m / (1.0 - ADAM_B1 ** ADAM_STEP)
    v_hat = v / (1.0 - ADAM_B2 ** ADAM_STEP)
    delta = -ADAM_LR * (m_hat / (jnp.sqrt(v_hat) + ADAM_EPS) + ADAM_WD * w)
    return delta, m, v


def _shard_update(grad, from_sibling, from_chips, w, m, v, place, name, transposed):
    _, rows, cols = grad.shape
    tr = _row_tile(rows)

    def total(g_ref, s_ref, c_ref):
        g = g_ref[0, 0] + s_ref[0]
        g = g + c_ref[0].astype(F32)
        g = g + c_ref[1].astype(F32)
        return g + c_ref[2].astype(F32)

    part_specs = [pl.BlockSpec((1, 1, tr, cols), lambda i, place: (place[0], place[1], i, 0)),
                  pl.BlockSpec((1, tr, cols), lambda i, place: (place[0], i, 0)),
                  pl.BlockSpec((3, tr, cols), lambda i, place: (0, i, 0))]
    flat = pl.BlockSpec((tr, cols), lambda i, place: (i, 0))
    parts = (place, grad.reshape(4, 2, rows, cols), from_sibling, from_chips)
    if not transposed:
        def body(place_ref, g_ref, s_ref, c_ref, w_ref, m_ref, v_ref, go_ref, d_ref, mo_ref, vo_ref):
            g = total(g_ref, s_ref, c_ref)
            go_ref[...] = g
            d_ref[...], mo_ref[...], vo_ref[...] = _adamw(w_ref[...], g, m_ref[...], v_ref[...])

        grid_spec = pltpu.PrefetchScalarGridSpec(num_scalar_prefetch=1, grid=(rows // tr,),
                                                 in_specs=part_specs + [flat, flat, flat], out_specs=[flat] * 4)
        return pl.pallas_call(body, name=name, grid_spec=grid_spec, out_shape=[jax.ShapeDtypeStruct((rows, cols), F32)] * 4,
                              compiler_params=_params(1))(*parts, w, m, v)

    def sum_body(place_ref, g_ref, s_ref, c_ref, go_ref):
        go_ref[...] = total(g_ref, s_ref, c_ref)

    grid_spec = pltpu.PrefetchScalarGridSpec(num_scalar_prefetch=1, grid=(rows // tr,), in_specs=part_specs, out_specs=flat)
    g = pl.pallas_call(sum_body, name=name + "_sum", grid_spec=grid_spec, out_shape=jax.ShapeDtypeStruct((rows, cols), F32),
                       compiler_params=_params(1))(*parts)
    gt = jnp.transpose(g, (1, 0))
    tc = _row_tile(cols)

    def adam_body(g_ref, w_ref, m_ref, v_ref, d_ref, mo_ref, vo_ref):
        d_ref[...], mo_ref[...], vo_ref[...] = _adamw(w_ref[...], g_ref[...], m_ref[...], v_ref[...])

    blk = pl.BlockSpec((tc, rows), lambda i: (i, 0))
    res = pl.pallas_call(adam_body, name=name + "_adam", grid=(cols // tc,), in_specs=[blk] * 4, out_specs=[blk] * 3,
                         out_shape=[jax.ShapeDtypeStruct((cols, rows), F32)] * 3, compiler_params=_params(1))(gt, w, m, v)
    return (gt, *res)


GAINS = ("ffn1_pre_g", "ffn1_post_g", "mix_pre_g", "mix_post_g", "ffn2_pre_g", "ffn2_post_g")
HALVES = ("conv_b", "b_rg", "b_ig", "lru_lambda", "g_lru_out", "g_attn_out")
GATES = ("w_rg", "w_ig")
SMALL = GAINS + HALVES + GATES + ("sinks", "conv_w")


def _small_update(gathered, w, m, v):
    n_small = len(SMALL)

    def body(*refs):
        ga_ref, gb_ref, gc_ref, gd_ref = refs[:4]
        wmv = refs[4:4 + 3 * n_small]
        outs = refs[4 + 3 * n_small:4 + 7 * n_small]
        loss_ref = refs[4 + 7 * n_small]

        def total(ref):
            s = ref[0]
            for d in range(1, N_DEV):
                s = s + ref[d]
            return s

        sa, sb, sc, sd = total(ga_ref), total(gb_ref), total(gc_ref), total(gd_ref)
        grads = {}
        for i, k in enumerate(GAINS):
            grads[k] = sa[i:i + 1]
        for i, k in enumerate(HALVES):
            grads[k] = sb[i:i + 1]
        grads["w_rg"], grads["w_ig"] = sc[0:512], sc[512:1024]
        grads["sinks"] = sd[4:5, 0:8]
        grads["conv_w"] = sd[0:4]
        for i, k in enumerate(SMALL):
            g = grads[k]
            outs[4 * i][...] = g
            outs[4 * i + 1][...], outs[4 * i + 2][...], outs[4 * i + 3][...] = _adamw(
                wmv[3 * i][...], g, wmv[3 * i + 1][...], wmv[3 * i + 2][...])
        loss_ref[...] = jnp.broadcast_to(sd[5:6, 0:128], loss_ref.shape)

    operands = list(gathered)
    out_shape = []
    for k in SMALL:
        operands += [w[k], m[k], v[k]]
        out_shape += [jax.ShapeDtypeStruct(w[k].shape, F32)] * 4
    out_shape.append(jax.ShapeDtypeStruct((8, 128), F32))
    res = pl.pallas_call(body, name="small_update", out_shape=out_shape, compiler_params=_params())(*operands)
    parts = [{k: res[4 * i + j] for i, k in enumerate(SMALL)} for j in range(4)]
    return (*parts, res[-1])


def _dup_in_rows(wt):
    k0, k1, v0, v1 = wt[1536:1600], wt[1600:1664], wt[1664:1728], wt[1728:1792]
    return jnp.concatenate([wt[:1536], k0, k0, v0, v0, k1, k1, v1, v1], axis=0)


def _undup_in_columns(dw):
    return jnp.concatenate([dw[:, :1536], dw[:, 1536:1600], dw[:, 1792:1856], dw[:, 1664:1728], dw[:, 1920:1984]], axis=1)


def _pair_block_diag(w):
    w = w.reshape(N_LRU_GROUP, 2, 64, 64)
    z = jnp.zeros((N_LRU_GROUP, 64, 64), w.dtype)
    top = jnp.concatenate([w[:, 0], z], axis=2)
    bot = jnp.concatenate([z, w[:, 1]], axis=2)
    return jnp.concatenate([top, bot], axis=1)


def _pair_block_diag_grad(dw2):
    return jnp.stack([dw2[:, :64, :64], dw2[:, 64:, 64:]], axis=1).reshape(512, 64)


def kernel(x, ffn1_pre_g, ffn1_w_gu, ffn1_w_down, ffn1_post_g, mix_pre_g, w_in, conv_w, conv_b, w_rg, b_rg, w_ig, b_ig, lru_lambda, sinks, g_lru_out, g_attn_out, w_o, mix_post_g, ffn2_pre_g, ffn2_w_gu, ffn2_w_down, ffn2_post_g, loss_target, m_ffn1_pre_g, m_ffn1_w_gu, m_ffn1_w_down, m_ffn1_post_g, m_mix_pre_g, m_w_in, m_conv_w, m_conv_b, m_w_rg, m_b_rg, m_w_ig, m_b_ig, m_lru_lambda, m_sinks, m_g_lru_out, m_g_attn_out, m_w_o, m_mix_post_g, m_ffn2_pre_g, m_ffn2_w_gu, m_ffn2_w_down, m_ffn2_post_g, v_ffn1_pre_g, v_ffn1_w_gu, v_ffn1_w_down, v_ffn1_post_g, v_mix_pre_g, v_w_in, v_conv_w, v_conv_b, v_w_rg, v_b_rg, v_w_ig, v_b_ig, v_lru_lambda, v_sinks, v_g_lru_out, v_g_attn_out, v_w_o, v_mix_post_g, v_ffn2_pre_g, v_ffn2_w_gu, v_ffn2_w_down, v_ffn2_post_g):
    args = dict(locals())
    names = ["ffn1_pre_g", "ffn1_w_gu", "ffn1_w_down", "ffn1_post_g", "mix_pre_g", "w_in", "conv_w", "conv_b", "w_rg",
             "b_rg", "w_ig", "b_ig", "lru_lambda", "sinks", "g_lru_out", "g_attn_out", "w_o", "mix_post_g",
             "ffn2_pre_g", "ffn2_w_gu", "ffn2_w_down", "ffn2_post_g"]
    big = ["ffn1_w_gu", "ffn1_w_down", "w_in", "w_o", "ffn2_w_gu", "ffn2_w_down"]
    w = {k: args[k] for k in names}
    mom = {k: args["m_" + k] for k in names}
    var = {k: args["v_" + k] for k in names}
    t = x.shape[1]
    xs = x.reshape(t, D_MODEL)
    target = loss_target.reshape(t, D_MODEL)
    cx, cy, cc = _coords()
    me = 4 * cx + 2 * cy + cc
    other = jnp.stack([2 * (1 - cx) + cy, 2 * cx + (1 - cy), 2 * (1 - cx) + (1 - cy), cc]).astype(jnp.int32)
    place = jnp.stack([2 * cx + cy, cc]).astype(jnp.int32)

    transposed = ("ffn1_w_gu", "w_in", "ffn2_w_gu")

    def shard_view(a, k):
        return jnp.transpose(a[0], (1, 0)) if k in transposed else a[0]

    def shard_unview(a, k):
        return (jnp.transpose(a, (1, 0)) if k in transposed else a)[None]

    shard2d = {k: shard_view(w[k], k) for k in big}
    shard_bf = {k: shard2d[k].astype(BF16) for k in big}
    conv_pad = jnp.pad(conv_w.reshape(4, 64), ((0, 4), (0, 64)))
    (first_w,) = _run_exchanges([_Gather([shard_bf["ffn1_w_gu"], shard_bf["ffn1_w_down"]])], "all_gather_ffn1")
    wgu1 = first_w[0].reshape(2, N_CHUNK, CHUNK, D_MODEL)
    wd1 = first_w[1].reshape(N_CHUNK, CHUNK, D_MODEL)
    rest = _Gather([shard_bf["w_in"], shard_bf["w_o"], shard_bf["ffn2_w_gu"], shard_bf["ffn2_w_down"], conv_pad])

    x1, f1, n1, gu1, gathered = _ffn_fwd(xs, ffn1_pre_g, wgu1, wd1, ffn1_post_g, None, "ffn1_fwd", rest)
    w_in_full = _dup_in_rows(gathered[0].reshape(D_IN, D_MODEL))
    w_o_full = gathered[1].reshape(D_MODEL, D_MODEL)
    wgu2 = gathered[2].reshape(2, N_CHUNK, CHUNK, D_MODEL)
    wd2 = gathered[3].reshape(N_CHUNK, CHUNK, D_MODEL)
    conv_w_full = jnp.transpose(gathered[4][:, 0:4, 0:64], (1, 0, 2)).reshape(4, D_LRU)
    p_lru = jnp.concatenate([conv_b, b_rg, b_ig, lru_lambda, conv_w_full], axis=0)
    wrg2 = _pair_block_diag(w_rg[0]).astype(BF16)
    wig2 = _pair_block_diag(w_ig[0]).astype(BF16)
    xl, gl, q, kv = _mix_in_fwd(x1, mix_pre_g, w_in_full)
    h = _lru_fwd(xl, p_lru, wrg2, wig2)
    o = _attn_fwd(q, kv, sinks)
    x2, mo = _mix_out_fwd(x1, h, gl, o, g_lru_out, g_attn_out, mix_post_g, w_o_full)
    g = {}
    dx3, n2, df2, gu2, g["ffn2_post_g"], loss_parts, _ = _ffn_fwd(x2, ffn2_pre_g, wgu2, wd2, ffn2_post_g, target, "ffn2_fwd")
    loss_local = jnp.sum(loss_parts[::8, 0])

    partial, from_sibling, from_chips = {}, {}, {}

    def chip_sums(keys):
        return [_chip_sum(partial[k], from_sibling[k], other, "chip_sum_" + k) for k in keys]

    dgu2, dwgu2, dwd2, _ = _ffn_bwd_w(n2, df2, gu2, wd2, "ffn2_bwd_w")
    partial["ffn2_w_gu"] = dwgu2.reshape(N_DEV, D_MODEL, CHUNK)
    partial["ffn2_w_down"] = dwd2.reshape(N_DEV, D_FF // N_DEV, D_MODEL)
    ffn2_keys = ["ffn2_w_gu", "ffn2_w_down"]
    dx2, g["ffn2_pre_g"], got = _ffn_bwd_x(dgu2, wgu2, x2, ffn2_pre_g, dx3, "ffn2_bwd_x",
                                           _SiblingExchange([partial[k] for k in ffn2_keys]))
    from_sibling.update(zip(ffn2_keys, got))
    dy, do, dwo, g["mix_post_g"], g["g_lru_out"], g["g_attn_out"] = _mix_out_bwd(
        dx2, mo, h, gl, o, g_lru_out, g_attn_out, mix_post_g, w_o_full)
    dq, dkv, dsink, got = _attn_bwd(q, kv, do, sinks, _ChipExchange(chip_sums(ffn2_keys)))
    from_chips.update(zip(ffn2_keys, got))
    dxl, dgl, dp, dwrg2, dwig2 = _lru_bwd(dy, h, xl, gl, p_lru, wrg2, wig2)
    dx1, dwin_dup, g["mix_pre_g"], df1, g["ffn1_post_g"] = _mix_in_bwd(
        dx2, x1, mix_pre_g, dxl, dgl, dq, dkv, w_in_full, f1, ffn1_post_g)
    partial["w_in"] = jnp.transpose(_undup_in_columns(dwin_dup).reshape(D_MODEL, N_DEV, D_IN // N_DEV), (1, 0, 2))
    partial["w_o"] = dwo.reshape(N_DEV, D_MODEL // N_DEV, D_MODEL)
    mix_keys = ["w_in", "w_o"]
    (got,) = _run_exchanges([_SiblingExchange([partial[k] for k in mix_keys])], "mix_sibling_exchange")
    from_sibling.update(zip(mix_keys, got))
    dgu1, dwgu1, dwd1, got = _ffn_bwd_w(n1, df1, gu1, wd1, "ffn1_bwd_w", _ChipExchange(chip_sums(mix_keys)))
    from_chips.update(zip(mix_keys, got))
    partial["ffn1_w_gu"] = dwgu1.reshape(N_DEV, D_MODEL, CHUNK)
    partial["ffn1_w_down"] = dwd1.reshape(N_DEV, D_FF // N_DEV, D_MODEL)
    ffn1_keys = ["ffn1_w_gu", "ffn1_w_down"]
    (got,) = _run_exchanges([_SiblingExchange([partial[k] for k in ffn1_keys])], "ffn1_sibling_exchange")
    from_sibling.update(zip(ffn1_keys, got))
    dx0, g["ffn1_pre_g"], got = _ffn_bwd_x(dgu1, wgu1, xs, ffn1_pre_g, dx1, "ffn1_bwd_x",
                                           _ChipExchange(chip_sums(ffn1_keys)))
    from_chips.update(zip(ffn1_keys, got))

    grads, delta, new_m, new_v = {}, {}, {}, {}
    for k in big:
        res = _shard_update(partial[k], from_sibling[k], from_chips[k], shard2d[k], shard_view(mom[k], k),
                            shard_view(var[k], k), place, "update_" + k, k in transposed)
        grads[k], delta[k], new_m[k], new_v[k] = [shard_unview(r, k) for r in res]

    zeros2 = jnp.zeros((2, D_MODEL), F32)
    g_gains = jnp.concatenate([g[k] for k in GAINS] + [zeros2], axis=0)
    g_halves = jnp.concatenate([dp[0:4], g["g_lru_out"], g["g_attn_out"], zeros2[:, :D_LRU]], axis=0)
    g_gates = jnp.concatenate([_pair_block_diag_grad(dwrg2), _pair_block_diag_grad(dwig2)], axis=0)
    g_misc = jnp.concatenate([dp[4:8], jnp.pad(dsink[:, 0].reshape(1, 8), ((0, 0), (0, D_LRU - 8))),
                              jnp.pad(loss_local.reshape(1, 1), ((0, 0), (0, D_LRU - 1))), zeros2[:, :D_LRU]], axis=0)
    (gathered_small,) = _run_exchanges([_Gather([g_gains, g_halves, g_gates, g_misc])], "all_gather_small_grads")
    col = me * 64

    def small_view(vals):
        out = {k: vals[k] for k in GAINS + HALVES + ("sinks",)}
        out.update({k: vals[k].reshape(512, 64) for k in GATES})
        out["conv_w"] = lax.dynamic_update_slice(jnp.zeros((4, D_LRU), F32), vals["conv_w"].reshape(4, 64), (0, col))
        return out

    *small, loss_tile = _small_update(gathered_small, small_view(w), small_view(mom), small_view(var))
    for dst, part in zip((grads, delta, new_m, new_v), small):
        for k in SMALL:
            if k == "conv_w":
                dst[k] = lax.dynamic_slice(part[k], (0, col), (4, 64)).reshape(conv_w.shape)
            else:
                dst[k] = part[k].reshape(w[k].shape)
    return (loss_tile[0, 0], dx0.reshape(x.shape), *[grads[k] for k in names], *[delta[k] for k in names],
            *[new_m[k] for k in names], *[new_v[k] for k in names])
```

```python
import functools

import jax
import jax.numpy as jnp
from jax import lax
from jax.experimental import pallas as pl
from jax.experimental.pallas import tpu as pltpu

F32 = jnp.float32
BF16 = jnp.bfloat16

D_MODEL = 1024
D_FF = 2816
N_DEV = 8
N_CHUNK = 4
CHUNK = D_FF // N_CHUNK
D_LRU = 512
D_ATTN = 512
LRU_GROUP = 128
N_LRU_GROUP = D_LRU // LRU_GROUP
HEAD_DIM = 64
BLOCK_Q = 128
D_IN = 1792
D_IN_DUP = 2048
RMS_EPS = 1e-6
LRU_C = 8.0
MASK_VALUE = -1e30
ATTN_SCALE = HEAD_DIM ** -0.5

ADAM_LR = 0.001
ADAM_B1 = 0.9
ADAM_B2 = 0.999
ADAM_EPS = 1e-08
ADAM_WD = 0.01
ADAM_STEP = 10

VMEM_LIMIT_V7X = 56 * 2 ** 20

ANY = pl.BlockSpec(memory_space=pl.ANY)
SMEM = pl.BlockSpec(memory_space=pltpu.SMEM)
MESH = pl.DeviceIdType.MESH


def _params(n_grid=0):
    sem = ("arbitrary",) * n_grid if n_grid else None
    return pltpu.CompilerParams(dimension_semantics=sem, vmem_limit_bytes=VMEM_LIMIT_V7X)


def _dot(a, b):
    return lax.dot_general(a, b, (((1,), (0,)), ((), ())), preferred_element_type=F32)


def _dot_nt(a, b):
    return lax.dot_general(a, b, (((1,), (1,)), ((), ())), preferred_element_type=F32)


def _dot_tn(a, b):
    return lax.dot_general(a, b, (((0,), (0,)), ((), ())), preferred_element_type=F32)


def _sigmoid(x):
    return 1.0 / (1.0 + jnp.exp(-x))


def _rms_fwd(x, g):
    r = lax.rsqrt(jnp.mean(x * x, axis=-1, keepdims=True) + RMS_EPS)
    return x * r * g


def _rms_bwd(x, g, dy):
    r = lax.rsqrt(jnp.mean(x * x, axis=-1, keepdims=True) + RMS_EPS)
    xh = x * r
    dg = jnp.sum(dy * xh, axis=0, keepdims=True)
    dxh = dy * g
    dx = r * (dxh - xh * jnp.mean(dxh * xh, axis=-1, keepdims=True))
    return dx, dg


def _gelu(x):
    c = 0.7978845608028654
    inner = c * (x + 0.044715 * x * x * x)
    th = jnp.tanh(inner)
    ge = 0.5 * x * (1.0 + th)
    dge = 0.5 * (1.0 + th) + 0.5 * x * (1.0 - th * th) * c * (1.0 + 3.0 * 0.044715 * x * x)
    return ge, dge


def _zero_at_first(first, *refs):
    @pl.when(first)
    def _():
        for ref in refs:
            ref[...] = jnp.zeros_like(ref)


def _token_tile(t):
    return 512 if t >= 2048 else t // 2


def _ffn_bwd_tile(t):
    return 1024 if t >= 4096 else t // 2


def _coords():
    return lax.axis_index("x"), lax.axis_index("y"), lax.axis_index("c")


class _Gather:
    n_phases = 3
    at = (0.0, 0.8, 1.0)

    def __init__(self, shards, routed=False):
        k = len(shards)
        self.routed = routed
        self.arrays = list(shards)
        self.out_shape = [jax.ShapeDtypeStruct((N_DEV,) + s.shape, s.dtype) for s in shards]
        self.scratch = [pltpu.SemaphoreType.DMA((7 * k,)), pltpu.SemaphoreType.DMA((7 * k,)), pltpu.SemaphoreType.DMA((k,))]

    def run(self, phase, ins, outs, sems):
        send_sems, recv_sems, local_sems = sems
        k_arr = len(ins)
        x, y, c = _coords()
        me, sibling = (x, y, c), (x, y, 1 - c)
        chips = [(1 - x, y), (x, 1 - y), (1 - x, 1 - y)]
        direct = 2 if self.routed else 3
        relay_from = (x + (1 - c) * (1 - 2 * x), y + c * (1 - 2 * y))
        relay_to = (x + c * (1 - 2 * x), y + (1 - c) * (1 - 2 * y))

        def rows(k, dev):
            return outs[k].at[4 * dev[0] + 2 * dev[1] + dev[2]]

        def copy(k, slot, block, to, src=None):
            return pltpu.make_async_remote_copy(
                src_ref=rows(k, block) if src is None else src, dst_ref=rows(k, block),
                send_sem=send_sems.at[7 * k + slot], recv_sem=recv_sems.at[7 * k + slot],
                device_id=to, device_id_type=MESH)

        def mine():
            return [pltpu.make_async_copy(ins[k], rows(k, me), local_sems.at[k]) for k in range(k_arr)]

        def first():
            return [copy(k, slot, me, to, src=ins[k]) for k in range(k_arr)
                    for slot, to in enumerate([sibling] + [(*chip, c) for chip in chips[:direct]])]

        def relayed(k):
            return copy(k, 3, (*relay_from, c), (*relay_to, c))

        def passed(j, k):
            return copy(k, 4 + j, (*chips[j], c), sibling)

        if phase == 0:
            for cp in mine() + first():
                cp.start()
        elif phase == 1:
            for j in range(direct):
                for k in range(k_arr):
                    copy(k, 1 + j, (*chips[j], c), me).wait_recv()
            for k in range(k_arr):
                if self.routed:
                    relayed(k).start()
                for j in range(direct):
                    passed(j, k).start()
        else:
            for k in range(k_arr):
                if self.routed:
                    copy(k, 3, (*chips[2], c), me).wait_recv()
                    passed(2, k).start()
            for k in range(k_arr):
                copy(k, 0, sibling, me).wait_recv()
                for j, chip in enumerate(chips):
                    copy(k, 4 + j, (*chip, 1 - c), me).wait_recv()
            sent = first() + [passed(j, k) for j in range(3) for k in range(k_arr)]
            if self.routed:
                sent += [relayed(k) for k in range(k_arr)]
            for cp in sent:
                cp.wait_send()
            for cp in mine():
                cp.wait()


class _SiblingExchange:
    n_phases = 2
    at = (0.0, 1.0)

    def __init__(self, grads):
        k = len(grads)
        self.arrays = list(grads)
        self.out_shape = [jax.ShapeDtypeStruct((4,) + g.shape[1:], g.dtype) for g in grads]
        self.scratch = [pltpu.SemaphoreType.DMA((4 * k,)), pltpu.SemaphoreType.DMA((4 * k,))]

    def run(self, phase, ins, outs, sems):
        send_sems, recv_sems = sems
        x, y, c = _coords()
        copies = [pltpu.make_async_remote_copy(
            src_ref=ins[k].at[2 * q + (1 - c)], dst_ref=outs[k].at[q],
            send_sem=send_sems.at[4 * k + q], recv_sem=recv_sems.at[4 * k + q],
            device_id=(x, y, 1 - c), device_id_type=MESH) for k in range(len(ins)) for q in range(4)]
        for cp in copies:
            if phase == 0:
                cp.start()
            else:
                cp.wait_recv()
                cp.wait_send()


class _ChipExchange:
    n_phases = 2
    at = (0.0, 1.0)

    def __init__(self, chip_sums):
        k = len(chip_sums)
        self.arrays = list(chip_sums)
        self.out_shape = [jax.ShapeDtypeStruct((3,) + s.shape[1:], s.dtype) for s in chip_sums]
        self.scratch = [pltpu.SemaphoreType.DMA((3 * k,)), pltpu.SemaphoreType.DMA((3 * k,))]

    def run(self, phase, ins, outs, sems):
        send_sems, recv_sems = sems
        x, y, c = _coords()
        chips = [(1 - x, y), (x, 1 - y), (1 - x, 1 - y)]
        copies = [pltpu.make_async_remote_copy(
            src_ref=ins[k].at[j], dst_ref=outs[k].at[j],
            send_sem=send_sems.at[3 * k + j], recv_sem=recv_sems.at[3 * k + j],
            device_id=(*chip, c), device_id_type=MESH) for k in range(len(ins)) for j, chip in enumerate(chips)]
        for cp in copies:
            if phase == 0:
                cp.start()
            else:
                cp.wait_recv()
                cp.wait_send()


class _Both:
    n_phases = 3
    at = (0.0, 0.8, 1.0)

    def __init__(self, two_phase, gather):
        self.parts = (two_phase, gather)
        self.arrays = two_phase.arrays + gather.arrays
        self.out_shape = two_phase.out_shape + gather.out_shape
        self.scratch = two_phase.scratch + gather.scratch

    def run(self, phase, ins, outs, sems):
        a, b = self.parts
        n_in, n_out, n_sem = len(a.arrays), len(a.out_shape), len(a.scratch)
        refs_a = (ins[:n_in], outs[:n_out], sems[:n_sem])
        refs_b = (ins[n_in:], outs[n_out:], sems[n_sem:])
        if phase == 0:
            a.run(0, *refs_a)
        if phase == 2:
            a.run(1, *refs_a)
        b.run(phase, *refs_b)


class _Host:
    def __init__(self, exchange):
        self.ex = exchange
        self.args = [] if exchange is None else exchange.arrays
        self.in_specs = [ANY] * len(self.args)
        self.out_shape = [] if exchange is None else exchange.out_shape
        self.out_specs = [ANY] * len(self.out_shape)
        self.scratch = [] if exchange is None else exchange.scratch

    def split(self, refs, n_in, n_out, n_scratch):
        a, b, s = len(self.args), len(self.out_shape), len(self.scratch)
        own_in, ex_in = refs[:n_in], refs[n_in:n_in + a]
        rest = refs[n_in + a:]
        own_out, ex_out = rest[:n_out], rest[n_out:n_out + b]
        rest = rest[n_out + b:]
        own_scratch, ex_sems = rest[:n_scratch], rest[n_scratch:n_scratch + s]
        return list(own_in) + list(own_out) + list(own_scratch), (ex_in, ex_out, ex_sems)

    def at_steps(self, step, n_steps, ex_refs):
        if self.ex is None:
            return
        for p in range(self.ex.n_phases):
            pl.when(step == int(round(self.ex.at[p] * (n_steps - 1))))(functools.partial(self.ex.run, p, *ex_refs))

    def phase(self, p, ex_refs):
        if self.ex is not None:
            self.ex.run(p, *ex_refs)


def _run_exchanges(exchanges, name):
    hosts = [_Host(ex) for ex in exchanges]
    n_in = [len(h.args) for h in hosts]
    n_out = [len(h.out_shape) for h in hosts]
    n_sc = [len(h.scratch) for h in hosts]

    def body(*refs):
        ins, outs, scr = refs[:sum(n_in)], refs[sum(n_in):sum(n_in) + sum(n_out)], refs[sum(n_in) + sum(n_out):]
        parts = []
        for e in range(len(hosts)):
            parts.append((ins[sum(n_in[:e]):sum(n_in[:e + 1])], outs[sum(n_out[:e]):sum(n_out[:e + 1])],
                          scr[sum(n_sc[:e]):sum(n_sc[:e + 1])]))
        for h, part in zip(hosts, parts):
            h.phase(0, part)
        for h, part in zip(hosts, parts):
            for p in range(1, h.ex.n_phases):
                h.phase(p, part)

    res = pl.pallas_call(
        body, name=name, in_specs=[ANY] * sum(n_in), out_specs=[ANY] * sum(n_out),
        out_shape=[s for h in hosts for s in h.out_shape], scratch_shapes=[s for h in hosts for s in h.scratch],
    )(*[a for h in hosts for a in h.args])
    return [res[sum(n_out[:e]):sum(n_out[:e + 1])] for e in range(len(hosts))]


def _ffn_fwd(x, g_pre, wgu, wd, g_post, target, name, exchange=None):
    t = x.shape[0]
    tm = _token_tile(t)
    n_i = t // tm
    with_loss = target is not None
    host = _Host(exchange)
    n_in, n_out = (6, 6) if with_loss else (5, 4)

    def body(*refs):
        own, ex_refs = host.split(refs, n_in, n_out, 0)
        if with_loss:
            x_ref, gpre_ref, wgu_ref, wd_ref, gpost_ref, tgt_ref, xo_ref, n_ref, df_ref, gu_ref, dgpost_ref, loss_ref = own
            _zero_at_first(pl.program_id(0) == 0, dgpost_ref)
        else:
            x_ref, gpre_ref, wgu_ref, wd_ref, gpost_ref, xo_ref, f_ref, n_ref, gu_ref = own
        host.at_steps(pl.program_id(0), n_i, ex_refs)
        x = x_ref[...]
        n = _rms_fwd(x, gpre_ref[...]).astype(BF16)
        n_ref[...] = n
        f = None
        for j in range(N_CHUNK):
            gate = _dot_nt(n, wgu_ref[0, j])
            up = _dot_nt(n, wgu_ref[1, j])
            gu_ref[0, j] = gate.astype(BF16)
            gu_ref[1, j] = up.astype(BF16)
            part = _dot((gate * _sigmoid(gate) * up).astype(BF16), wd_ref[j])
            f = part if f is None else f + part
        xo = x + 0.5 * _rms_fwd(f, gpost_ref[...])
        if with_loss:
            err = xo - tgt_ref[...]
            d_out = err * (1.0 / D_MODEL)
            xo_ref[...] = d_out
            df, dg = _rms_bwd(f, gpost_ref[...], 0.5 * d_out)
            df_ref[...] = df.astype(BF16)
            dgpost_ref[...] += dg
            part = 0.5 * jnp.sum(jnp.sum(err * err, axis=-1, keepdims=True) * (1.0 / D_MODEL), axis=0, keepdims=True)
            loss_ref[...] = jnp.broadcast_to(part, loss_ref.shape)
        else:
            f_ref[...] = f
            xo_ref[...] = xo

    tok = pl.BlockSpec((tm, D_MODEL), lambda i: (i, 0))
    vec = pl.BlockSpec((1, D_MODEL), lambda i: (0, 0))
    act = pl.BlockSpec((2, N_CHUNK, tm, CHUNK), lambda i: (0, 0, i, 0))
    tok_f32 = jax.ShapeDtypeStruct((t, D_MODEL), F32)
    tok_bf16 = jax.ShapeDtypeStruct((t, D_MODEL), BF16)
    act_shape = jax.ShapeDtypeStruct((2, N_CHUNK, t, CHUNK), BF16)
    in_specs = [tok, vec,
                pl.BlockSpec((2, N_CHUNK, CHUNK, D_MODEL), lambda i: (0, 0, 0, 0), pipeline_mode=pl.Buffered(1)),
                pl.BlockSpec((N_CHUNK, CHUNK, D_MODEL), lambda i: (0, 0, 0), pipeline_mode=pl.Buffered(1)),
                vec]
    args = [x, g_pre, wgu, wd, g_post]
    if with_loss:
        in_specs.append(tok)
        args.append(target)
        out_shape = [tok_f32, tok_bf16, tok_bf16, act_shape, jax.ShapeDtypeStruct((1, D_MODEL), F32),
                     jax.ShapeDtypeStruct((n_i * 8, 128), F32)]
        out_specs = [tok, tok, tok, act, vec, pl.BlockSpec((8, 128), lambda i: (i, 0))]
    else:
        out_shape = [tok_f32, tok_f32, tok_bf16, act_shape]
        out_specs = [tok, tok, tok, act]
    res = pl.pallas_call(
        body, name=name, grid=(n_i,), in_specs=in_specs + host.in_specs, out_specs=out_specs + host.out_specs,
        out_shape=out_shape + host.out_shape, scratch_shapes=host.scratch, compiler_params=_params(1),
    )(*args, *host.args)
    return (*res[:n_out], list(res[n_out:]))


def _ffn_bwd_w(n, df, gu, wd, name, exchange=None):
    t = n.shape[0]
    tm = _ffn_bwd_tile(t)
    n_i = t // tm
    host = _Host(exchange)

    def body(*refs):
        (n_ref, df_ref, gu_ref, wd_ref, dgu_ref, dwgu_ref, dwd_ref), ex_refs = host.split(refs, 4, 3, 0)
        i = pl.program_id(1)
        host.at_steps(pl.program_id(0) * n_i + i, N_CHUNK * n_i, ex_refs)
        _zero_at_first(i == 0, dwgu_ref, dwd_ref)
        nb = n_ref[...]
        dfb = df_ref[...]
        gate = gu_ref[0, 0].astype(F32)
        up = gu_ref[1, 0].astype(F32)
        s = _sigmoid(gate)
        silu = gate * s
        a = (silu * up).astype(BF16)
        da = _dot_nt(dfb, wd_ref[0])
        dup = (da * silu).astype(BF16)
        dgate = (da * up * (s * (1.0 + gate * (1.0 - s)))).astype(BF16)
        dgu_ref[0, 0] = dgate
        dgu_ref[1, 0] = dup
        dwgu_ref[0, 0] += _dot_tn(nb, dgate)
        dwgu_ref[1, 0] += _dot_tn(nb, dup)
        dwd_ref[0] += _dot_tn(a, dfb)

    tok = pl.BlockSpec((tm, D_MODEL), lambda j, i: (i, 0))
    act = pl.BlockSpec((2, 1, tm, CHUNK), lambda j, i: (0, j, i, 0))
    wgu_spec = pl.BlockSpec((2, 1, D_MODEL, CHUNK), lambda j, i: (0, j, 0, 0), pipeline_mode=pl.Buffered(1))
    wd_spec = pl.BlockSpec((1, CHUNK, D_MODEL), lambda j, i: (j, 0, 0), pipeline_mode=pl.Buffered(1))
    res = pl.pallas_call(
        body, name=name, grid=(N_CHUNK, n_i),
        in_specs=[tok, tok, act, wd_spec] + host.in_specs,
        out_specs=[act, wgu_spec, wd_spec] + host.out_specs,
        out_shape=[jax.ShapeDtypeStruct((2, N_CHUNK, t, CHUNK), BF16),
                   jax.ShapeDtypeStruct((2, N_CHUNK, D_MODEL, CHUNK), F32),
                   jax.ShapeDtypeStruct((N_CHUNK, CHUNK, D_MODEL), F32)] + host.out_shape,
        scratch_shapes=host.scratch, compiler_params=_params(2),
    )(n, df, gu, wd, *host.args)
    return (*res[:3], list(res[3:]))


def _ffn_bwd_x(dgu, wgu, x, g_pre, d_out, name, exchange=None):
    t = x.shape[0]
    tm = _token_tile(t)
    n_i = t // tm
    host = _Host(exchange)

    def body(*refs):
        (dgu_ref, wgu_ref, x_ref, gpre_ref, do_ref, dx_ref, dgpre_ref), ex_refs = host.split(refs, 5, 2, 0)
        i = pl.program_id(0)
        host.at_steps(i, n_i, ex_refs)
        _zero_at_first(i == 0, dgpre_ref)
        dn = _dot(dgu_ref[0, 0], wgu_ref[0, 0]) + _dot(dgu_ref[1, 0], wgu_ref[1, 0])
        for j in range(1, N_CHUNK):
            dn = dn + _dot(dgu_ref[0, j], wgu_ref[0, j]) + _dot(dgu_ref[1, j], wgu_ref[1, j])
        dx, dg = _rms_bwd(x_ref[...], gpre_ref[...], dn)
        dx_ref[...] = do_ref[...] + dx
        dgpre_ref[...] += dg

    tok = pl.BlockSpec((tm, D_MODEL), lambda i: (i, 0))
    vec = pl.BlockSpec((1, D_MODEL), lambda i: (0, 0))
    res = pl.pallas_call(
        body, name=name, grid=(n_i,),
        in_specs=[pl.BlockSpec((2, N_CHUNK, tm, CHUNK), lambda i: (0, 0, i, 0)),
                  pl.BlockSpec((2, N_CHUNK, CHUNK, D_MODEL), lambda i: (0, 0, 0, 0), pipeline_mode=pl.Buffered(1)),
                  tok, vec, tok] + host.in_specs,
        out_specs=[tok, vec] + host.out_specs,
        out_shape=[jax.ShapeDtypeStruct((t, D_MODEL), F32), jax.ShapeDtypeStruct((1, D_MODEL), F32)] + host.out_shape,
        scratch_shapes=host.scratch, compiler_params=_params(1),
    )(dgu, wgu, x, g_pre, d_out, *host.args)
    return (*res[:2], list(res[2:]))


def _mix_in_fwd(x1, g, w_in):
    t = x1.shape[0]
    tm = _token_tile(t)

    def body(x_ref, g_ref, w_ref, xl_ref, gl_ref, q_ref, kv_ref):
        n = _rms_fwd(x_ref[...], g_ref[...]).astype(BF16)
        proj = _dot_nt(n, w_ref[...])
        xl_ref[...] = proj[:, 0:512]
        gl_ref[...] = proj[:, 512:1024]
        q_ref[...] = proj[:, 1024:1536].astype(BF16)
        kv_ref[...] = proj[:, 1536:2048].astype(BF16)

    tok = pl.BlockSpec((tm, D_MODEL), lambda i: (i, 0))
    half = pl.BlockSpec((tm, 512), lambda i: (i, 0))
    return pl.pallas_call(
        body, name="mix_in_fwd", grid=(t // tm,),
        in_specs=[tok, pl.BlockSpec((1, D_MODEL), lambda i: (0, 0)), pl.BlockSpec((D_IN_DUP, D_MODEL), lambda i: (0, 0))],
        out_specs=[half, half, half, half],
        out_shape=[jax.ShapeDtypeStruct((t, 512), F32), jax.ShapeDtypeStruct((t, 512), F32),
                   jax.ShapeDtypeStruct((t, 512), BF16), jax.ShapeDtypeStruct((t, 512), BF16)],
        compiler_params=_params(1),
    )(x1, g, w_in)


def _shift_down(x, before, s):
    if s == 0:
        return x
    rolled = pltpu.roll(x, s, 0)
    ext = jnp.concatenate([before, x[0:8]], axis=0)
    first8 = pltpu.roll(ext, s, 0)[8:16]
    return jnp.concatenate([first8, rolled[8:]], axis=0)


def _shift_up(x, after, s):
    if s == 0:
        return x
    rows = x.shape[0]
    rolled = pltpu.roll(x, rows - s, 0)
    ext = jnp.concatenate([x[rows - 8:rows], after], axis=0)
    last8 = pltpu.roll(ext, 16 - s, 0)[0:8]
    return jnp.concatenate([rolled[:rows - 8], last8], axis=0)


def _log_sigmoid(x):
    e = jnp.exp(-jnp.abs(x))
    log1p_e = jnp.where(e < 0.01, e * (1.0 - e * (0.5 - e * (1.0 / 3.0))), jnp.log(1.0 + e))
    return jnp.minimum(x, 0.0) - log1p_e


def _lru_gates(xc, p_ref, wrg, wig):
    xcb = xc.astype(BF16)
    r = _sigmoid(_dot(xcb, wrg) + p_ref[1:2, :])
    ig = _sigmoid(_dot(xcb, wig) + p_ref[2:3, :])
    ls = _log_sigmoid(p_ref[3:4, :])
    log_a = LRU_C * r * ls
    a = jnp.exp(log_a)
    z = 2.0 * log_a
    series = z * (1.0 + z * (0.5 + z * (1.0 / 6.0 + z * (1.0 / 24.0 + z * (1.0 / 120.0 + z * (1.0 / 720.0))))))
    expm1 = jnp.where(z > -0.1, series, jnp.exp(z) - 1.0)
    mult = jnp.sqrt(-expm1)
    return xcb, r, ig, ls, a, mult


def _conv_taps(x, before, p_ref):
    xc = x * p_ref[7:8, :]
    for s in (1, 2, 3):
        xc = xc + _shift_down(x, before, s) * p_ref[7 - s:8 - s, :]
    return xc + p_ref[0:1, :]


def _lru_block_rows(t):
    return 512 if t >= 1024 else t // 2


def _lru_fwd(xl, p, wrg2, wig2):
    t = xl.shape[0]
    tb = _lru_block_rows(t)

    def body(xl_ref, p_ref, wrg_ref, wig_ref, h_ref, x_tail, h_carry):
        tt = pl.program_id(1)

        @pl.when(tt == 0)
        def _():
            x_tail[...] = jnp.zeros_like(x_tail)
            h_carry[...] = jnp.zeros_like(h_carry)

        x = xl_ref[...]
        xc = _conv_taps(x, x_tail[...], p_ref)
        x_tail[...] = x[tb - 8:tb]
        _, r, ig, ls, a, mult = _lru_gates(xc, p_ref, wrg_ref[0], wig_ref[0])
        u = mult * ig * xc
        row = lax.broadcasted_iota(jnp.int32, (tb, LRU_GROUP), 0)
        s = 1
        while s < tb:
            keep = row >= s
            u = jnp.where(keep, a * pltpu.roll(u, s, 0) + u, u)
            a = jnp.where(keep, a * pltpu.roll(a, s, 0), a)
            s *= 2
        h = u + a * h_carry[0:1, :]
        h_ref[...] = h
        h_carry[...] = jnp.broadcast_to(h[tb - 1:tb], h_carry.shape)

    blk = pl.BlockSpec((tb, LRU_GROUP), lambda g, tt: (tt, g))
    par = pl.BlockSpec((8, LRU_GROUP), lambda g, tt: (0, g))
    wsp = pl.BlockSpec((1, LRU_GROUP, LRU_GROUP), lambda g, tt: (g, 0, 0))
    return pl.pallas_call(
        body, name="lru_fwd", grid=(N_LRU_GROUP, t // tb), in_specs=[blk, par, wsp, wsp], out_specs=blk,
        out_shape=jax.ShapeDtypeStruct((t, D_LRU), F32),
        scratch_shapes=[pltpu.VMEM((8, LRU_GROUP), F32), pltpu.VMEM((8, LRU_GROUP), F32)],
        compiler_params=_params(2),
    )(xl, p, wrg2, wig2)


def _lru_bwd(dy, h, xl, gl, p, wrg2, wig2):
    t = xl.shape[0]
    tb = _lru_block_rows(t)
    n_tb = t // tb
    tb8 = tb // 8

    def body(dy_ref, h_ref, hprev_ref, xl_ref, xprev_ref, gl_ref, p_ref, wrg_ref, wig_ref,
             dxl_ref, dgl_ref, dp_ref, dwrg_ref, dwig_ref, g_carry, a_carry, dxc_head):
        step = pl.program_id(1)
        tt = n_tb - 1 - step
        first = step == 0

        _zero_at_first(first, g_carry, a_carry, dxc_head, dp_ref, dwrg_ref, dwig_ref)

        has_prev = (tt > 0).astype(F32)
        x = xl_ref[...]
        x_before = xprev_ref[...] * has_prev
        xs = [_shift_down(x, x_before, s) for s in range(4)]
        xc = xs[0] * p_ref[7:8, :] + xs[1] * p_ref[6:7, :] + xs[2] * p_ref[5:6, :] + xs[3] * p_ref[4:5, :] + p_ref[0:1, :]
        wrg = wrg_ref[0]
        wig = wig_ref[0]
        xcb, r, ig, ls, a, mult = _lru_gates(xc, p_ref, wrg, wig)

        hh = h_ref[...]
        h_m1 = _shift_down(hh, hprev_ref[...] * has_prev, 1)
        ge, dge = _gelu(gl_ref[...])
        dy = dy_ref[...]
        dgl_ref[...] = dy * hh * dge
        dh = dy * ge

        b = _shift_up(a, a_carry[...], 1)
        row = lax.broadcasted_iota(jnp.int32, (tb, LRU_GROUP), 0)
        g = dh
        s = 1
        while s < tb:
            keep = row < tb - s
            g = jnp.where(keep, b * pltpu.roll(g, tb - s, 0) + g, g)
            b = jnp.where(keep, b * pltpu.roll(b, tb - s, 0), b)
            s *= 2
        g = g + b * g_carry[0:1, :]
        g_carry[...] = jnp.broadcast_to(g[0:1], g_carry.shape)
        a_carry[...] = jnp.broadcast_to(a[0:1], a_carry.shape)

        da = g * h_m1
        dmult = g * ig * xc
        dig = g * mult * xc
        dxc = g * mult * ig
        dlog_a = da * a - dmult * (a * a) / mult
        dr = dlog_a * (LRU_C * ls)
        dls = jnp.sum(dlog_a * (LRU_C * r), axis=0, keepdims=True)
        dlam = dls * _sigmoid(-p_ref[3:4, :])
        dpre_r = dr * r * (1.0 - r)
        dpre_i = dig * ig * (1.0 - ig)
        dprb = dpre_r.astype(BF16)
        dpib = dpre_i.astype(BF16)
        dxc = dxc + _dot_nt(dprb, wrg) + _dot_nt(dpib, wig)
        dwrg_ref[0] += _dot_tn(xcb, dprb)
        dwig_ref[0] += _dot_tn(xcb, dpib)

        after = dxc_head[...]
        dxl = dxc * p_ref[7:8, :]
        for s in (1, 2, 3):
            dxl = dxl + _shift_up(dxc, after, s) * p_ref[7 - s:8 - s, :]
        dxl_ref[...] = dxl
        dxc_head[...] = dxc[0:8]

        rows = [jnp.sum(dxc, axis=0, keepdims=True), jnp.sum(dpre_r, axis=0, keepdims=True),
                jnp.sum(dpre_i, axis=0, keepdims=True), dlam]
        rows += [jnp.sum(dxc * xs[3 - k], axis=0, keepdims=True) for k in range(4)]
        dp_ref[...] += jnp.concatenate(rows, axis=0)

    blk = pl.BlockSpec((tb, LRU_GROUP), lambda g, s: (n_tb - 1 - s, g))
    prev8 = pl.BlockSpec((8, LRU_GROUP), lambda g, s: (jnp.maximum((n_tb - 1 - s) * tb8 - 1, 0), g))
    par = pl.BlockSpec((8, LRU_GROUP), lambda g, s: (0, g))
    wsp = pl.BlockSpec((1, LRU_GROUP, LRU_GROUP), lambda g, s: (g, 0, 0))
    return pl.pallas_call(
        body, name="lru_bwd", grid=(N_LRU_GROUP, n_tb),
        in_specs=[blk, blk, prev8, blk, prev8, blk, par, wsp, wsp], out_specs=[blk, blk, par, wsp, wsp],
        out_shape=[jax.ShapeDtypeStruct((t, D_LRU), F32), jax.ShapeDtypeStruct((t, D_LRU), F32),
                   jax.ShapeDtypeStruct((8, D_LRU), F32),
                   jax.ShapeDtypeStruct((N_LRU_GROUP, LRU_GROUP, LRU_GROUP), F32),
                   jax.ShapeDtypeStruct((N_LRU_GROUP, LRU_GROUP, LRU_GROUP), F32)],
        scratch_shapes=[pltpu.VMEM((8, LRU_GROUP), F32)] * 3,
        compiler_params=_params(2),
    )(dy, h, h, xl, xl, gl, p, wrg2, wig2)


def _attn_bias(first_block):
    qi = jnp.bitwise_and(lax.broadcasted_iota(jnp.int32, (4 * BLOCK_Q, 2 * BLOCK_Q), 0), BLOCK_Q - 1)
    kj = lax.broadcasted_iota(jnp.int32, (4 * BLOCK_Q, 2 * BLOCK_Q), 1)
    rel = qi + BLOCK_Q - kj
    mask = (rel >= 0) & (rel < BLOCK_Q)
    if first_block:
        mask = mask & (kj >= BLOCK_Q)
    return jnp.where(mask, 0.0, MASK_VALUE)


def _sink_column(sinks):
    hrow = lax.broadcasted_iota(jnp.int32, (4 * BLOCK_Q, 1), 0)
    return jnp.where(hrow < BLOCK_Q, sinks[0],
                     jnp.where(hrow < 2 * BLOCK_Q, sinks[1], jnp.where(hrow < 3 * BLOCK_Q, sinks[2], sinks[3])))


def _attn_scores(qv, kvv, n, bias, sk, lo):
    r0 = pl.multiple_of(n * BLOCK_Q, BLOCK_Q)
    rp = pl.multiple_of(jnp.maximum(n - 1, 0) * BLOCK_Q, BLOCK_Q)
    kvb = jnp.concatenate([kvv[pl.ds(rp, BLOCK_Q), :], kvv[pl.ds(r0, BLOCK_Q), :]], axis=0)
    k2 = kvb[:, 0:128]
    v2 = kvb[:, 128:256]
    qs = _stack_heads(qv[pl.ds(r0, BLOCK_Q), :], lo)
    s = _dot_nt(qs, k2) * ATTN_SCALE + bias
    m = jnp.maximum(jnp.max(s, axis=-1, keepdims=True), sk)
    e = jnp.exp(s - m)
    es = jnp.exp(sk - m)
    inv = 1.0 / (jnp.sum(e, axis=-1, keepdims=True) + es)
    return r0, rp, qs, k2, v2, e * inv, es * inv


def _stack_heads(pair2, lo):
    p0 = pair2[:, 0:128]
    p1 = pair2[:, 128:256]
    z = jnp.zeros_like(p0)
    return jnp.concatenate([jnp.where(lo, p0, z), jnp.where(lo, z, p0), jnp.where(lo, p1, z), jnp.where(lo, z, p1)], axis=0)


def _unstack_heads(st, lo):
    b = BLOCK_Q
    return jnp.concatenate([jnp.where(lo, st[0:b], st[b:2 * b]), jnp.where(lo, st[2 * b:3 * b], st[3 * b:4 * b])], axis=1)


def _attn_fwd(q, kv, sinks):
    t = q.shape[0]
    n_blk = t // BLOCK_Q

    def body(q_hbm, kv_hbm, s_ref, o_hbm, qv, kvv, ov, bias0, bias, sem):
        lo = lax.broadcasted_iota(jnp.int32, (BLOCK_Q, 128), 1) < HEAD_DIM
        bias0[...] = _attn_bias(True)
        bias[...] = _attn_bias(False)
        for g in range(2):
            cols = pl.ds(256 * g, 256)
            loads = [pltpu.make_async_copy(q_hbm.at[:, cols], qv, sem.at[0]),
                     pltpu.make_async_copy(kv_hbm.at[:, cols], kvv, sem.at[1])]
            for cp in loads:
                cp.start()
            for cp in loads:
                cp.wait()
            sk = _sink_column([s_ref[0, 4 * g + i] for i in range(4)])

            def block(n, bias_ref):
                r0, _, _, _, v2, prob, _ = _attn_scores(qv, kvv, n, bias_ref[...], sk, lo)
                ov[pl.ds(r0, BLOCK_Q), :] = _unstack_heads(_dot(prob.astype(BF16), v2), lo)

            block(0, bias0)

            def later(n, carry):
                block(n, bias)
                return carry

            lax.fori_loop(1, n_blk, later, 0, unroll=2)
            store = pltpu.make_async_copy(ov, o_hbm.at[:, cols], sem.at[2])
            store.start()
            store.wait()

    return pl.pallas_call(
        body, name="attn_fwd", in_specs=[ANY, ANY, SMEM], out_specs=ANY,
        out_shape=jax.ShapeDtypeStruct((t, D_ATTN), F32),
        scratch_shapes=[pltpu.VMEM((t, 256), BF16), pltpu.VMEM((t, 256), BF16), pltpu.VMEM((t, 256), F32),
                        pltpu.VMEM((4 * BLOCK_Q, 2 * BLOCK_Q), F32), pltpu.VMEM((4 * BLOCK_Q, 2 * BLOCK_Q), F32),
                        pltpu.SemaphoreType.DMA((3,))],
        compiler_params=_params(),
    )(q, kv, sinks)


def _attn_bwd(q, kv, do, sinks, exchange=None):
    t = q.shape[0]
    n_blk = t // BLOCK_Q
    host = _Host(exchange)

    def body(*refs):
        own, ex_refs = host.split(refs, 4, 3, 9)
        q_hbm, kv_hbm, do_hbm, s_ref, dq_hbm, dkv_hbm, dsink_ref, qv, kvv, dov, dqv, dkvv, ds_acc, bias0, bias, sem = own
        host.phase(0, ex_refs)
        lo = lax.broadcasted_iota(jnp.int32, (BLOCK_Q, 128), 1) < HEAD_DIM
        bias0[...] = _attn_bias(True)
        bias[...] = _attn_bias(False)
        for g in range(2):
            cols = pl.ds(256 * g, 256)
            loads = [pltpu.make_async_copy(q_hbm.at[:, cols], qv, sem.at[0]),
                     pltpu.make_async_copy(kv_hbm.at[:, cols], kvv, sem.at[1]),
                     pltpu.make_async_copy(do_hbm.at[:, cols], dov, sem.at[2])]
            for cp in loads:
                cp.start()
            for cp in loads:
                cp.wait()
            sk = _sink_column([s_ref[0, 4 * g + i] for i in range(4)])
            ds_acc[...] = jnp.zeros_like(ds_acc)

            def block(n, bias_ref, has_prev):
                r0, rp, qs, k2, v2, prob, psink = _attn_scores(qv, kvv, n, bias_ref[...], sk, lo)
                pb = prob.astype(BF16)
                dos = _stack_heads(dov[pl.ds(r0, BLOCK_Q), :], lo)
                dp = _dot_nt(dos, v2)
                dsum = jnp.sum(prob * dp, axis=-1, keepdims=True)
                dsb = (prob * (dp - dsum) * ATTN_SCALE).astype(BF16)
                ds_acc[...] -= psink * dsum
                dqv[pl.ds(r0, BLOCK_Q), :] = _unstack_heads(_dot(dsb, k2), lo).astype(BF16)
                dk2 = _dot_tn(dsb, qs)
                dv2 = _dot_tn(pb, dos)
                dkvv[pl.ds(r0, BLOCK_Q), :] = jnp.concatenate([dk2[BLOCK_Q:], dv2[BLOCK_Q:]], axis=1)
                if has_prev:
                    dkvv[pl.ds(rp, BLOCK_Q), :] += jnp.concatenate([dk2[:BLOCK_Q], dv2[:BLOCK_Q]], axis=1)

            block(0, bias0, False)

            def later(n, carry):
                block(n, bias, True)
                return carry

            lax.fori_loop(1, n_blk, later, 0, unroll=2)
            for i in range(4):
                tot = jnp.sum(ds_acc[BLOCK_Q * i:BLOCK_Q * (i + 1), :], axis=0, keepdims=True)
                dsink_ref[4 * g + i:4 * g + i + 1, :] = jnp.broadcast_to(tot, (1, 128))
            stores = [pltpu.make_async_copy(dqv, dq_hbm.at[:, cols], sem.at[0]),
                      pltpu.make_async_copy(dkvv, dkv_hbm.at[:, cols], sem.at[1])]
            for cp in stores:
                cp.start()
            for cp in stores:
                cp.wait()
        if exchange is not None:
            for p in range(1, exchange.n_phases):
                host.phase(p, ex_refs)

    res = pl.pallas_call(
        body, name="attn_bwd", in_specs=[ANY, ANY, ANY, SMEM] + host.in_specs,
        out_specs=[ANY, ANY, pl.BlockSpec(memory_space=pltpu.VMEM)] + host.out_specs,
        out_shape=[jax.ShapeDtypeStruct((t, D_ATTN), BF16), jax.ShapeDtypeStruct((t, 512), F32),
                   jax.ShapeDtypeStruct((8, 128), F32)] + host.out_shape,
        scratch_shapes=[pltpu.VMEM((t, 256), BF16), pltpu.VMEM((t, 256), BF16), pltpu.VMEM((t, 256), BF16),
                        pltpu.VMEM((t, 256), BF16), pltpu.VMEM((t, 256), F32), pltpu.VMEM((4 * BLOCK_Q, 1), F32),
                        pltpu.VMEM((4 * BLOCK_Q, 2 * BLOCK_Q), F32), pltpu.VMEM((4 * BLOCK_Q, 2 * BLOCK_Q), F32),
                        pltpu.SemaphoreType.DMA((3,))] + host.scratch,
        compiler_params=_params(),
    )(q, kv, do, sinks, *host.args)
    return (*res[:3], list(res[3:]))


def _mix_out_fwd(x1, h, gl, o, g_lru, g_attn, g_post, w_o):
    t = x1.shape[0]
    tm = _token_tile(t)

    def body(x_ref, h_ref, gl_ref, o_ref, g1_ref, g2_ref, gp_ref, w_ref, x2_ref, m_ref):
        y = h_ref[...] * _gelu(gl_ref[...])[0]
        yn1 = _rms_fwd(y, g1_ref[...]).astype(BF16)
        yn2 = _rms_fwd(o_ref[...], g2_ref[...]).astype(BF16)
        m = _dot(yn1, w_ref[0:512, :]) + _dot(yn2, w_ref[512:1024, :])
        m_ref[...] = m
        x2_ref[...] = x_ref[...] + _rms_fwd(m, gp_ref[...])

    tok = pl.BlockSpec((tm, D_MODEL), lambda i: (i, 0))
    half = pl.BlockSpec((tm, 512), lambda i: (i, 0))
    vec = pl.BlockSpec((1, D_MODEL), lambda i: (0, 0))
    hvec = pl.BlockSpec((1, 512), lambda i: (0, 0))
    return pl.pallas_call(
        body, name="mix_out_fwd", grid=(t // tm,),
        in_specs=[tok, half, half, half, hvec, hvec, vec, pl.BlockSpec((D_MODEL, D_MODEL), lambda i: (0, 0))],
        out_specs=[tok, tok],
        out_shape=[jax.ShapeDtypeStruct((t, D_MODEL), F32), jax.ShapeDtypeStruct((t, D_MODEL), F32)],
        compiler_params=_params(1),
    )(x1, h, gl, o, g_lru, g_attn, g_post, w_o)


def _mix_out_bwd(dx2, m, h, gl, o, g_lru, g_attn, g_post, w_o):
    t = dx2.shape[0]
    tm = _token_tile(t)

    def body(dx_ref, m_ref, h_ref, gl_ref, o_ref, g1_ref, g2_ref, gp_ref, w_ref,
             dy_ref, do_ref, dw_ref, dgp_ref, dg1_ref, dg2_ref):
        _zero_at_first(pl.program_id(0) == 0, dw_ref, dgp_ref, dg1_ref, dg2_ref)
        dm, dgp = _rms_bwd(m_ref[...], gp_ref[...], dx_ref[...])
        dmb = dm.astype(BF16)
        y = h_ref[...] * _gelu(gl_ref[...])[0]
        o = o_ref[...]
        yn1 = _rms_fwd(y, g1_ref[...]).astype(BF16)
        yn2 = _rms_fwd(o, g2_ref[...]).astype(BF16)
        dw_ref[0:512, :] += _dot_tn(yn1, dmb)
        dw_ref[512:1024, :] += _dot_tn(yn2, dmb)
        dy, dg1 = _rms_bwd(y, g1_ref[...], _dot_nt(dmb, w_ref[0:512, :]))
        do, dg2 = _rms_bwd(o, g2_ref[...], _dot_nt(dmb, w_ref[512:1024, :]))
        dy_ref[...] = dy
        do_ref[...] = do.astype(BF16)
        dgp_ref[...] += dgp
        dg1_ref[...] += dg1
        dg2_ref[...] += dg2

    tok = pl.BlockSpec((tm, D_MODEL), lambda i: (i, 0))
    half = pl.BlockSpec((tm, 512), lambda i: (i, 0))
    vec = pl.BlockSpec((1, D_MODEL), lambda i: (0, 0))
    hvec = pl.BlockSpec((1, 512), lambda i: (0, 0))
    mat = pl.BlockSpec((D_MODEL, D_MODEL), lambda i: (0, 0))
    return pl.pallas_call(
        body, name="mix_out_bwd", grid=(t // tm,),
        in_specs=[tok, tok, half, half, half, hvec, hvec, vec, mat],
        out_specs=[half, half, mat, vec, hvec, hvec],
        out_shape=[jax.ShapeDtypeStruct((t, 512), F32), jax.ShapeDtypeStruct((t, 512), BF16),
                   jax.ShapeDtypeStruct((D_MODEL, D_MODEL), F32), jax.ShapeDtypeStruct((1, D_MODEL), F32),
                   jax.ShapeDtypeStruct((1, 512), F32), jax.ShapeDtypeStruct((1, 512), F32)],
        compiler_params=_params(1),
    )(dx2, m, h, gl, o, g_lru, g_attn, g_post, w_o)


def _mix_in_bwd(dx2, x1, g, dxl, dgl, dq, dkv, w_in, f1, g_post1):
    t = x1.shape[0]
    tm = _token_tile(t)

    def body(dx2_ref, x_ref, g_ref, dxl_ref, dgl_ref, dq_ref, dkv_ref, w_ref, f1_ref, gp1_ref,
             dx1_ref, dw_ref, dg_ref, df1_ref, dgp1_ref):
        _zero_at_first(pl.program_id(0) == 0, dw_ref, dg_ref, dgp1_ref)
        x = x_ref[...]
        nb = _rms_fwd(x, g_ref[...]).astype(BF16)
        lo = lax.broadcasted_iota(jnp.int32, (tm, 128), 1) < HEAD_DIM
        dkv = dkv_ref[...]
        folded = []
        for k in range(4):
            seg = dkv[:, 128 * k:128 * (k + 1)]
            folded.append(jnp.where(lo, seg + pltpu.roll(seg, HEAD_DIM, 1), 0.0).astype(BF16))
        dproj = jnp.concatenate([dxl_ref[...].astype(BF16), dgl_ref[...].astype(BF16), dq_ref[...]] + folded, axis=1)
        dw_ref[...] += _dot_tn(nb, dproj)
        dx, dg = _rms_bwd(x, g_ref[...], _dot(dproj, w_ref[...]))
        dx1 = dx2_ref[...] + dx
        dx1_ref[...] = dx1
        dg_ref[...] += dg
        df1, dgp1 = _rms_bwd(f1_ref[...], gp1_ref[...], 0.5 * dx1)
        df1_ref[...] = df1.astype(BF16)
        dgp1_ref[...] += dgp1

    tok = pl.BlockSpec((tm, D_MODEL), lambda i: (i, 0))
    half = pl.BlockSpec((tm, 512), lambda i: (i, 0))
    vec = pl.BlockSpec((1, D_MODEL), lambda i: (0, 0))
    mat = pl.BlockSpec((D_IN_DUP, D_MODEL), lambda i: (0, 0))
    dmat = pl.BlockSpec((D_MODEL, D_IN_DUP), lambda i: (0, 0))
    return pl.pallas_call(
        body, name="mix_in_bwd", grid=(t // tm,),
        in_specs=[tok, tok, vec, half, half, half, half, mat, tok, vec], out_specs=[tok, dmat, vec, tok, vec],
        out_shape=[jax.ShapeDtypeStruct((t, D_MODEL), F32), jax.ShapeDtypeStruct((D_MODEL, D_IN_DUP), F32),
                   jax.ShapeDtypeStruct((1, D_MODEL), F32), jax.ShapeDtypeStruct((t, D_MODEL), BF16),
                   jax.ShapeDtypeStruct((1, D_MODEL), F32)],
        compiler_params=_params(1),
    )(dx2, x1, g, dxl, dgl, dq, dkv, w_in, f1, g_post1)


def _row_tile(rows):
    return rows if rows <= 512 else rows // 2


def _chip_sum(grad, from_sibling, other, name):
    _, rows, cols = grad.shape
    tr = _row_tile(rows)

    def body(other_ref, g_ref, s_ref, out_ref):
        out_ref[0] = (g_ref[0, 0] + s_ref[0]).astype(BF16)

    grid_spec = pltpu.PrefetchScalarGridSpec(
        num_scalar_prefetch=1, grid=(3, rows // tr),
        in_specs=[pl.BlockSpec((1, 1, tr, cols), lambda j, i, other: (other[j], other[3], i, 0)),
                  pl.BlockSpec((1, tr, cols), lambda j, i, other: (other[j], i, 0))],
        out_specs=pl.BlockSpec((1, tr, cols), lambda j, i, other: (j, i, 0)))
    return pl.pallas_call(
        body, name=name, grid_spec=grid_spec, out_shape=jax.ShapeDtypeStruct((3, rows, cols), BF16),
        compiler_params=_params(2),
    )(other, grad.reshape(4, 2, rows, cols), from_sibling)


def _adamw(w, g, m, v):
    m = ADAM_B1 * m + (1.0 - ADAM_B1) * g
    v = ADAM_B2 * v + (1.0 - ADAM_B2) * (g * g)
    m_hat = m / (1.0 - ADAM_B1 ** ADAM_STEP)
    v_hat = v / (1.0 - ADAM_B2 ** ADAM_STEP)
    delta = -ADAM_LR * (m_hat / (jnp.sqrt(v_hat) + ADAM_EPS) + ADAM_WD * w)
    return delta, m, v


def _shard_update(grad, from_sibling, from_chips, w, m, v, place, name, transposed):
    _, rows, cols = grad.shape
    tr = _row_tile(rows)

    def total(g_ref, s_ref, c_ref):
        g = g_ref[0, 0] + s_ref[0]
        g = g + c_ref[0].astype(F32)
        g = g + c_ref[1].astype(F32)
        return g + c_ref[2].astype(F32)

    part_specs = [pl.BlockSpec((1, 1, tr, cols), lambda i, place: (place[0], place[1], i, 0)),
                  pl.BlockSpec((1, tr, cols), lambda i, place: (place[0], i, 0)),
                  pl.BlockSpec((3, tr, cols), lambda i, place: (0, i, 0))]
    flat = pl.BlockSpec((tr, cols), lambda i, place: (i, 0))
    parts = (place, grad.reshape(4, 2, rows, cols), from_sibling, from_chips)
    if not transposed:
        def body(place_ref, g_ref, s_ref, c_ref, w_ref, m_ref, v_ref, go_ref, d_ref, mo_ref, vo_ref):
            g = total(g_ref, s_ref, c_ref)
            go_ref[...] = g
            d_ref[...], mo_ref[...], vo_ref[...] = _adamw(w_ref[...], g, m_ref[...], v_ref[...])

        grid_spec = pltpu.PrefetchScalarGridSpec(num_scalar_prefetch=1, grid=(rows // tr,),
                                                 in_specs=part_specs + [flat, flat, flat], out_specs=[flat] * 4)
        return pl.pallas_call(body, name=name, grid_spec=grid_spec, out_shape=[jax.ShapeDtypeStruct((rows, cols), F32)] * 4,
                              compiler_params=_params(1))(*parts, w, m, v)

    def sum_body(place_ref, g_ref, s_ref, c_ref, go_ref):
        go_ref[...] = total(g_ref, s_ref, c_ref)

    grid_spec = pltpu.PrefetchScalarGridSpec(num_scalar_prefetch=1, grid=(rows // tr,), in_specs=part_specs, out_specs=flat)
    g = pl.pallas_call(sum_body, name=name + "_sum", grid_spec=grid_spec, out_shape=jax.ShapeDtypeStruct((rows, cols), F32),
                       compiler_params=_params(1))(*parts)
    gt = jnp.transpose(g, (1, 0))
    tc = _row_tile(cols)

    def adam_body(g_ref, w_ref, m_ref, v_ref, d_ref, mo_ref, vo_ref):
        d_ref[...], mo_ref[...], vo_ref[...] = _adamw(w_ref[...], g_ref[...], m_ref[...], v_ref[...])

    blk = pl.BlockSpec((tc, rows), lambda i: (i, 0))
    res = pl.pallas_call(adam_body, name=name + "_adam", grid=(cols // tc,), in_specs=[blk] * 4, out_specs=[blk] * 3,
                         out_shape=[jax.ShapeDtypeStruct((cols, rows), F32)] * 3, compiler_params=_params(1))(gt, w, m, v)
    return (gt, *res)


GAINS = ("ffn1_pre_g", "ffn1_post_g", "mix_pre_g", "mix_post_g", "ffn2_pre_g", "ffn2_post_g")
HALVES = ("conv_b", "b_rg", "b_ig", "lru_lambda", "g_lru_out", "g_attn_out")
GATES = ("w_rg", "w_ig")
SMALL = GAINS + HALVES + GATES + ("sinks", "conv_w")


def _small_update(gathered, w, m, v):
    n_small = len(SMALL)

    def body(*refs):
        ga_ref, gb_ref, gc_ref, gd_ref, g0_ref = refs[:5]
        wmv = refs[5:5 + 3 * n_small]
        outs = refs[5 + 3 * n_small:5 + 7 * n_small]
        loss_ref = refs[5 + 7 * n_small]

        def total(ref):
            s = ref[0]
            for d in range(1, N_DEV):
                s = s + ref[d]
            return s

        sa, sb, sc, sd = total(ga_ref), total(gb_ref), total(gc_ref), total(gd_ref)
        grads = {}
        for i, k in enumerate(GAINS):
            grads[k] = sa[i:i + 1]
        grads[GAINS[0]] = total(g0_ref)
        for i, k in enumerate(HALVES):
            grads[k] = sb[i:i + 1]
        grads["w_rg"], grads["w_ig"] = sc[0:512], sc[512:1024]
        grads["sinks"] = sd[4:5, 0:8]
        grads["conv_w"] = sd[0:4]
        for i, k in enumerate(SMALL):
            g = grads[k]
            outs[4 * i][...] = g
            outs[4 * i + 1][...], outs[4 * i + 2][...], outs[4 * i + 3][...] = _adamw(
                wmv[3 * i][...], g, wmv[3 * i + 1][...], wmv[3 * i + 2][...])
        loss_ref[...] = jnp.broadcast_to(sd[5:6, 0:128], loss_ref.shape)

    operands = list(gathered)
    out_shape = []
    for k in SMALL:
        operands += [w[k], m[k], v[k]]
        out_shape += [jax.ShapeDtypeStruct(w[k].shape, F32)] * 4
    out_shape.append(jax.ShapeDtypeStruct((8, 128), F32))
    res = pl.pallas_call(body, name="small_update", out_shape=out_shape, compiler_params=_params())(*operands)
    parts = [{k: res[4 * i + j] for i, k in enumerate(SMALL)} for j in range(4)]
    return (*parts, res[-1])


def _dup_in_rows(wt):
    k0, k1, v0, v1 = wt[1536:1600], wt[1600:1664], wt[1664:1728], wt[1728:1792]
    return jnp.concatenate([wt[:1536], k0, k0, v0, v0, k1, k1, v1, v1], axis=0)


def _undup_in_columns(dw):
    return jnp.concatenate([dw[:, :1536], dw[:, 1536:1600], dw[:, 1792:1856], dw[:, 1664:1728], dw[:, 1920:1984]], axis=1)


def _pair_block_diag(w):
    w = w.reshape(N_LRU_GROUP, 2, 64, 64)
    z = jnp.zeros((N_LRU_GROUP, 64, 64), w.dtype)
    top = jnp.concatenate([w[:, 0], z], axis=2)
    bot = jnp.concatenate([z, w[:, 1]], axis=2)
    return jnp.concatenate([top, bot], axis=1)


def _pair_block_diag_grad(dw2):
    return jnp.stack([dw2[:, :64, :64], dw2[:, 64:, 64:]], axis=1).reshape(512, 64)


def kernel(x, ffn1_pre_g, ffn1_w_gu, ffn1_w_down, ffn1_post_g, mix_pre_g, w_in, conv_w, conv_b, w_rg, b_rg, w_ig, b_ig, lru_lambda, sinks, g_lru_out, g_attn_out, w_o, mix_post_g, ffn2_pre_g, ffn2_w_gu, ffn2_w_down, ffn2_post_g, loss_target, m_ffn1_pre_g, m_ffn1_w_gu, m_ffn1_w_down, m_ffn1_post_g, m_mix_pre_g, m_w_in, m_conv_w, m_conv_b, m_w_rg, m_b_rg, m_w_ig, m_b_ig, m_lru_lambda, m_sinks, m_g_lru_out, m_g_attn_out, m_w_o, m_mix_post_g, m_ffn2_pre_g, m_ffn2_w_gu, m_ffn2_w_down, m_ffn2_post_g, v_ffn1_pre_g, v_ffn1_w_gu, v_ffn1_w_down, v_ffn1_post_g, v_mix_pre_g, v_w_in, v_conv_w, v_conv_b, v_w_rg, v_b_rg, v_w_ig, v_b_ig, v_lru_lambda, v_sinks, v_g_lru_out, v_g_attn_out, v_w_o, v_mix_post_g, v_ffn2_pre_g, v_ffn2_w_gu, v_ffn2_w_down, v_ffn2_post_g):
    args = dict(locals())
    names = ["ffn1_pre_g", "ffn1_w_gu", "ffn1_w_down", "ffn1_post_g", "mix_pre_g", "w_in", "conv_w", "conv_b", "w_rg",
             "b_rg", "w_ig", "b_ig", "lru_lambda", "sinks", "g_lru_out", "g_attn_out", "w_o", "mix_post_g",
             "ffn2_pre_g", "ffn2_w_gu", "ffn2_w_down", "ffn2_post_g"]
    big = ["ffn1_w_gu", "ffn1_w_down", "w_in", "w_o", "ffn2_w_gu", "ffn2_w_down"]
    w = {k: args[k] for k in names}
    mom = {k: args["m_" + k] for k in names}
    var = {k: args["v_" + k] for k in names}
    t = x.shape[1]
    xs = x.reshape(t, D_MODEL)
    target = loss_target.reshape(t, D_MODEL)
    cx, cy, cc = _coords()
    me = 4 * cx + 2 * cy + cc
    other = jnp.stack([2 * (1 - cx) + cy, 2 * cx + (1 - cy), 2 * (1 - cx) + (1 - cy), cc]).astype(jnp.int32)
    place = jnp.stack([2 * cx + cy, cc]).astype(jnp.int32)

    transposed = ("ffn1_w_gu", "w_in", "ffn2_w_gu")

    def shard_view(a, k):
        return jnp.transpose(a[0], (1, 0)) if k in transposed else a[0]

    def shard_unview(a, k):
        return (jnp.transpose(a, (1, 0)) if k in transposed else a)[None]

    shard2d = {k: shard_view(w[k], k) for k in big}
    shard_bf = {k: shard2d[k].astype(BF16) for k in big}
    conv_pad = jnp.pad(conv_w.reshape(4, 64), ((0, 4), (0, 64)))
    (first_w,) = _run_exchanges([_Gather([shard_bf["ffn1_w_gu"], shard_bf["ffn1_w_down"]], routed=True)], "all_gather_ffn1")
    wgu1 = first_w[0].reshape(2, N_CHUNK, CHUNK, D_MODEL)
    wd1 = first_w[1].reshape(N_CHUNK, CHUNK, D_MODEL)
    rest = _Gather([shard_bf["w_in"], shard_bf["w_o"], shard_bf["ffn2_w_gu"], shard_bf["ffn2_w_down"], conv_pad])

    x1, f1, n1, gu1, gathered = _ffn_fwd(xs, ffn1_pre_g, wgu1, wd1, ffn1_post_g, None, "ffn1_fwd", rest)
    w_in_full = _dup_in_rows(gathered[0].reshape(D_IN, D_MODEL))
    w_o_full = gathered[1].reshape(D_MODEL, D_MODEL)
    wgu2 = gathered[2].reshape(2, N_CHUNK, CHUNK, D_MODEL)
    wd2 = gathered[3].reshape(N_CHUNK, CHUNK, D_MODEL)
    conv_w_full = jnp.transpose(gathered[4][:, 0:4, 0:64], (1, 0, 2)).reshape(4, D_LRU)
    p_lru = jnp.concatenate([conv_b, b_rg, b_ig, lru_lambda, conv_w_full], axis=0)
    wrg2 = _pair_block_diag(w_rg[0]).astype(BF16)
    wig2 = _pair_block_diag(w_ig[0]).astype(BF16)
    xl, gl, q, kv = _mix_in_fwd(x1, mix_pre_g, w_in_full)
    h = _lru_fwd(xl, p_lru, wrg2, wig2)
    o = _attn_fwd(q, kv, sinks)
    x2, mo = _mix_out_fwd(x1, h, gl, o, g_lru_out, g_attn_out, mix_post_g, w_o_full)
    g = {}
    dx3, n2, df2, gu2, g["ffn2_post_g"], loss_parts, _ = _ffn_fwd(x2, ffn2_pre_g, wgu2, wd2, ffn2_post_g, target, "ffn2_fwd")
    loss_local = jnp.sum(loss_parts[::8, 0])

    partial, from_sibling, from_chips = {}, {}, {}

    def chip_sums(keys):
        return [_chip_sum(partial[k], from_sibling[k], other, "chip_sum_" + k) for k in keys]

    dgu2, dwgu2, dwd2, _ = _ffn_bwd_w(n2, df2, gu2, wd2, "ffn2_bwd_w")
    partial["ffn2_w_gu"] = dwgu2.reshape(N_DEV, D_MODEL, CHUNK)
    partial["ffn2_w_down"] = dwd2.reshape(N_DEV, D_FF // N_DEV, D_MODEL)
    ffn2_keys = ["ffn2_w_gu", "ffn2_w_down"]
    dx2, g["ffn2_pre_g"], got = _ffn_bwd_x(dgu2, wgu2, x2, ffn2_pre_g, dx3, "ffn2_bwd_x",
                                           _SiblingExchange([partial[k] for k in ffn2_keys]))
    from_sibling.update(zip(ffn2_keys, got))
    dy, do, dwo, g["mix_post_g"], g["g_lru_out"], g["g_attn_out"] = _mix_out_bwd(
        dx2, mo, h, gl, o, g_lru_out, g_attn_out, mix_post_g, w_o_full)
    dq, dkv, dsink, got = _attn_bwd(q, kv, do, sinks, _ChipExchange(chip_sums(ffn2_keys)))
    from_chips.update(zip(ffn2_keys, got))
    dxl, dgl, dp, dwrg2, dwig2 = _lru_bwd(dy, h, xl, gl, p_lru, wrg2, wig2)
    dx1, dwin_dup, g["mix_pre_g"], df1, g["ffn1_post_g"] = _mix_in_bwd(
        dx2, x1, mix_pre_g, dxl, dgl, dq, dkv, w_in_full, f1, ffn1_post_g)
    partial["w_in"] = jnp.transpose(_undup_in_columns(dwin_dup).reshape(D_MODEL, N_DEV, D_IN // N_DEV), (1, 0, 2))
    partial["w_o"] = dwo.reshape(N_DEV, D_MODEL // N_DEV, D_MODEL)
    mix_keys = ["w_in", "w_o"]
    (got,) = _run_exchanges([_SiblingExchange([partial[k] for k in mix_keys])], "mix_sibling_exchange")
    from_sibling.update(zip(mix_keys, got))
    dgu1, dwgu1, dwd1, got = _ffn_bwd_w(n1, df1, gu1, wd1, "ffn1_bwd_w", _ChipExchange(chip_sums(mix_keys)))
    from_chips.update(zip(mix_keys, got))
    partial["ffn1_w_gu"] = dwgu1.reshape(N_DEV, D_MODEL, CHUNK)
    partial["ffn1_w_down"] = dwd1.reshape(N_DEV, D_FF // N_DEV, D_MODEL)
    ffn1_keys = ["ffn1_w_gu", "ffn1_w_down"]
    (got,) = _run_exchanges([_SiblingExchange([partial[k] for k in ffn1_keys])], "ffn1_sibling_exchange")
    from_sibling.update(zip(ffn1_keys, got))
    zeros2 = jnp.zeros((2, D_MODEL), F32)
    g_gains = jnp.concatenate([zeros2[:1]] + [g[k] for k in GAINS[1:]] + [zeros2], axis=0)
    g_halves = jnp.concatenate([dp[0:4], g["g_lru_out"], g["g_attn_out"], zeros2[:, :D_LRU]], axis=0)
    g_gates = jnp.concatenate([_pair_block_diag_grad(dwrg2), _pair_block_diag_grad(dwig2)], axis=0)
    g_misc = jnp.concatenate([dp[4:8], jnp.pad(dsink[:, 0].reshape(1, 8), ((0, 0), (0, D_LRU - 8))),
                              jnp.pad(loss_local.reshape(1, 1), ((0, 0), (0, D_LRU - 1))), zeros2[:, :D_LRU]], axis=0)
    dx0, g_first, got = _ffn_bwd_x(dgu1, wgu1, xs, ffn1_pre_g, dx1, "ffn1_bwd_x",
                                   _Both(_ChipExchange(chip_sums(ffn1_keys)), _Gather([g_gains, g_halves, g_gates, g_misc])))
    from_chips.update(zip(ffn1_keys, got[:2]))
    gathered_small = got[2:]

    grads, delta, new_m, new_v = {}, {}, {}, {}
    for k in big:
        res = _shard_update(partial[k], from_sibling[k], from_chips[k], shard2d[k], shard_view(mom[k], k),
                            shard_view(var[k], k), place, "update_" + k, k in transposed)
        grads[k], delta[k], new_m[k], new_v[k] = [shard_unview(r, k) for r in res]

    ((gathered_first,),) = _run_exchanges([_Gather([g_first])], "all_gather_first_gain")
    col = me * 64

    def small_view(vals):
        out = {k: vals[k] for k in GAINS + HALVES + ("sinks",)}
        out.update({k: vals[k].reshape(512, 64) for k in GATES})
        out["conv_w"] = lax.dynamic_update_slice(jnp.zeros((4, D_LRU), F32), vals["conv_w"].reshape(4, 64), (0, col))
        return out

    *small, loss_tile = _small_update([*gathered_small, gathered_first], small_view(w), small_view(mom), small_view(var))
    for dst, part in zip((grads, delta, new_m, new_v), small):
        for k in SMALL:
            if k == "conv_w":
                dst[k] = lax.dynamic_slice(part[k], (0, col), (4, 64)).reshape(conv_w.shape)
            else:
                dst[k] = part[k].reshape(w[k].shape)
    return (loss_tile[0, 0], dx0.reshape(x.shape), *[grads[k] for k in names], *[delta[k] for k in names],
            *[new_m[k] for k in names], *[new_v[k] for k in names])
```

```python
import functools

import jax
import jax.numpy as jnp
from jax import lax
from jax.experimental import pallas as pl
from jax.experimental.pallas import tpu as pltpu

F32 = jnp.float32
BF16 = jnp.bfloat16

D_MODEL = 1024
D_FF = 2816
N_DEV = 8
N_CHUNK = 4
CHUNK = D_FF // N_CHUNK
D_LRU = 512
D_ATTN = 512
LRU_GROUP = 128
N_LRU_GROUP = D_LRU // LRU_GROUP
HEAD_DIM = 64
BLOCK_Q = 128
D_IN = 1792
D_IN_DUP = 2048
RMS_EPS = 1e-6
LRU_C = 8.0
MASK_VALUE = -1e30
ATTN_SCALE = HEAD_DIM ** -0.5

ADAM_LR = 0.001
ADAM_B1 = 0.9
ADAM_B2 = 0.999
ADAM_EPS = 1e-08
ADAM_WD = 0.01
ADAM_STEP = 10

VMEM_LIMIT_V7X = 56 * 2 ** 20

ANY = pl.BlockSpec(memory_space=pl.ANY)
SMEM = pl.BlockSpec(memory_space=pltpu.SMEM)
MESH = pl.DeviceIdType.MESH


def _params(n_grid=0):
    sem = ("arbitrary",) * n_grid if n_grid else None
    return pltpu.CompilerParams(dimension_semantics=sem, vmem_limit_bytes=VMEM_LIMIT_V7X)


def _dot(a, b):
    return lax.dot_general(a, b, (((1,), (0,)), ((), ())), preferred_element_type=F32)


def _dot_nt(a, b):
    return lax.dot_general(a, b, (((1,), (1,)), ((), ())), preferred_element_type=F32)


def _dot_tn(a, b):
    return lax.dot_general(a, b, (((0,), (0,)), ((), ())), preferred_element_type=F32)


def _sigmoid(x):
    return 1.0 / (1.0 + jnp.exp(-x))


def _rms_fwd(x, g):
    r = lax.rsqrt(jnp.mean(x * x, axis=-1, keepdims=True) + RMS_EPS)
    return x * r * g


def _rms_bwd(x, g, dy):
    r = lax.rsqrt(jnp.mean(x * x, axis=-1, keepdims=True) + RMS_EPS)
    xh = x * r
    dg = jnp.sum(dy * xh, axis=0, keepdims=True)
    dxh = dy * g
    dx = r * (dxh - xh * jnp.mean(dxh * xh, axis=-1, keepdims=True))
    return dx, dg


def _gelu(x):
    c = 0.7978845608028654
    inner = c * (x + 0.044715 * x * x * x)
    th = jnp.tanh(inner)
    ge = 0.5 * x * (1.0 + th)
    dge = 0.5 * (1.0 + th) + 0.5 * x * (1.0 - th * th) * c * (1.0 + 3.0 * 0.044715 * x * x)
    return ge, dge


def _zero_at_first(first, *refs):
    @pl.when(first)
    def _():
        for ref in refs:
            ref[...] = jnp.zeros_like(ref)


def _token_tile(t):
    return 512 if t >= 2048 else t // 2


def _ffn_bwd_tile(t):
    return 1024 if t >= 4096 else t // 2


def _coords():
    return lax.axis_index("x"), lax.axis_index("y"), lax.axis_index("c")


class _Gather:
    n_phases = 3
    at = (0.0, 0.8, 1.0)

    def __init__(self, shards, routed=False):
        k = len(shards)
        self.routed = routed
        self.arrays = list(shards)
        self.out_shape = [jax.ShapeDtypeStruct((N_DEV,) + s.shape, s.dtype) for s in shards]
        self.scratch = [pltpu.SemaphoreType.DMA((7 * k,)), pltpu.SemaphoreType.DMA((7 * k,)), pltpu.SemaphoreType.DMA((k,))]

    def run(self, phase, ins, outs, sems):
        send_sems, recv_sems, local_sems = sems
        k_arr = len(ins)
        x, y, c = _coords()
        me, sibling = (x, y, c), (x, y, 1 - c)
        chips = [(1 - x, y), (x, 1 - y), (1 - x, 1 - y)]
        direct = 2 if self.routed else 3
        relay_from = (x + (1 - c) * (1 - 2 * x), y + c * (1 - 2 * y))
        relay_to = (x + c * (1 - 2 * x), y + (1 - c) * (1 - 2 * y))

        def rows(k, dev):
            return outs[k].at[4 * dev[0] + 2 * dev[1] + dev[2]]

        def copy(k, slot, block, to, src=None):
            return pltpu.make_async_remote_copy(
                src_ref=rows(k, block) if src is None else src, dst_ref=rows(k, block),
                send_sem=send_sems.at[7 * k + slot], recv_sem=recv_sems.at[7 * k + slot],
                device_id=to, device_id_type=MESH)

        def mine():
            return [pltpu.make_async_copy(ins[k], rows(k, me), local_sems.at[k]) for k in range(k_arr)]

        def first():
            return [copy(k, slot, me, to, src=ins[k]) for k in range(k_arr)
                    for slot, to in enumerate([sibling] + [(*chip, c) for chip in chips[:direct]])]

        def relayed(k):
            return copy(k, 3, (*relay_from, c), (*relay_to, c))

        def passed(j, k):
            return copy(k, 4 + j, (*chips[j], c), sibling)

        if phase == 0:
            for cp in mine() + first():
                cp.start()
        elif phase == 1:
            for j in range(direct):
                for k in range(k_arr):
                    copy(k, 1 + j, (*chips[j], c), me).wait_recv()
            for k in range(k_arr):
                if self.routed:
                    relayed(k).start()
                for j in range(direct):
                    passed(j, k).start()
        else:
            for k in range(k_arr):
                if self.routed:
                    copy(k, 3, (*chips[2], c), me).wait_recv()
                    passed(2, k).start()
            for k in range(k_arr):
                copy(k, 0, sibling, me).wait_recv()
                for j, chip in enumerate(chips):
                    copy(k, 4 + j, (*chip, 1 - c), me).wait_recv()
            sent = first() + [passed(j, k) for j in range(3) for k in range(k_arr)]
            if self.routed:
                sent += [relayed(k) for k in range(k_arr)]
            for cp in sent:
                cp.wait_send()
            for cp in mine():
                cp.wait()


class _SiblingExchange:
    n_phases = 2
    at = (0.0, 1.0)

    def __init__(self, grads):
        k = len(grads)
        self.arrays = list(grads)
        self.out_shape = [jax.ShapeDtypeStruct((4,) + g.shape[1:], g.dtype) for g in grads]
        self.scratch = [pltpu.SemaphoreType.DMA((4 * k,)), pltpu.SemaphoreType.DMA((4 * k,))]

    def run(self, phase, ins, outs, sems):
        send_sems, recv_sems = sems
        x, y, c = _coords()
        copies = [pltpu.make_async_remote_copy(
            src_ref=ins[k].at[2 * q + (1 - c)], dst_ref=outs[k].at[q],
            send_sem=send_sems.at[4 * k + q], recv_sem=recv_sems.at[4 * k + q],
            device_id=(x, y, 1 - c), device_id_type=MESH) for k in range(len(ins)) for q in range(4)]
        for cp in copies:
            if phase == 0:
                cp.start()
            else:
                cp.wait_recv()
                cp.wait_send()


class _ChipExchange:
    n_phases = 2
    at = (0.0, 1.0)

    def __init__(self, chip_sums):
        k = len(chip_sums)
        self.arrays = list(chip_sums)
        self.out_shape = [jax.ShapeDtypeStruct((3,) + s.shape[1:], s.dtype) for s in chip_sums]
        self.scratch = [pltpu.SemaphoreType.DMA((3 * k,)), pltpu.SemaphoreType.DMA((3 * k,))]

    def run(self, phase, ins, outs, sems):
        send_sems, recv_sems = sems
        x, y, c = _coords()
        chips = [(1 - x, y), (x, 1 - y), (1 - x, 1 - y)]
        copies = [pltpu.make_async_remote_copy(
            src_ref=ins[k].at[j], dst_ref=outs[k].at[j],
            send_sem=send_sems.at[3 * k + j], recv_sem=recv_sems.at[3 * k + j],
            device_id=(*chip, c), device_id_type=MESH) for k in range(len(ins)) for j, chip in enumerate(chips)]
        for cp in copies:
            if phase == 0:
                cp.start()
            else:
                cp.wait_recv()
                cp.wait_send()


class _Both:
    n_phases = 3
    at = (0.0, 0.95, 1.0)

    def __init__(self, two_phase, gather):
        self.parts = (two_phase, gather)
        self.arrays = two_phase.arrays + gather.arrays
        self.out_shape = two_phase.out_shape + gather.out_shape
        self.scratch = two_phase.scratch + gather.scratch

    def run(self, phase, ins, outs, sems):
        a, b = self.parts
        n_in, n_out, n_sem = len(a.arrays), len(a.out_shape), len(a.scratch)
        refs_a = (ins[:n_in], outs[:n_out], sems[:n_sem])
        refs_b = (ins[n_in:], outs[n_out:], sems[n_sem:])
        b.run(phase, *refs_b)
        if phase == 0:
            a.run(0, *refs_a)
        if phase == 2:
            a.run(1, *refs_a)


class _Host:
    def __init__(self, exchange):
        self.ex = exchange
        self.args = [] if exchange is None else exchange.arrays
        self.in_specs = [ANY] * len(self.args)
        self.out_shape = [] if exchange is None else exchange.out_shape
        self.out_specs = [ANY] * len(self.out_shape)
        self.scratch = [] if exchange is None else exchange.scratch

    def split(self, refs, n_in, n_out, n_scratch):
        a, b, s = len(self.args), len(self.out_shape), len(self.scratch)
        own_in, ex_in = refs[:n_in], refs[n_in:n_in + a]
        rest = refs[n_in + a:]
        own_out, ex_out = rest[:n_out], rest[n_out:n_out + b]
        rest = rest[n_out + b:]
        own_scratch, ex_sems = rest[:n_scratch], rest[n_scratch:n_scratch + s]
        return list(own_in) + list(own_out) + list(own_scratch), (ex_in, ex_out, ex_sems)

    def at_steps(self, step, n_steps, ex_refs):
        if self.ex is None:
            return
        for p in range(self.ex.n_phases):
            pl.when(step == int(round(self.ex.at[p] * (n_steps - 1))))(functools.partial(self.ex.run, p, *ex_refs))

    def phase(self, p, ex_refs):
        if self.ex is not None:
            self.ex.run(p, *ex_refs)


def _run_exchanges(exchanges, name):
    hosts = [_Host(ex) for ex in exchanges]
    n_in = [len(h.args) for h in hosts]
    n_out = [len(h.out_shape) for h in hosts]
    n_sc = [len(h.scratch) for h in hosts]

    def body(*refs):
        ins, outs, scr = refs[:sum(n_in)], refs[sum(n_in):sum(n_in) + sum(n_out)], refs[sum(n_in) + sum(n_out):]
        parts = []
        for e in range(len(hosts)):
            parts.append((ins[sum(n_in[:e]):sum(n_in[:e + 1])], outs[sum(n_out[:e]):sum(n_out[:e + 1])],
                          scr[sum(n_sc[:e]):sum(n_sc[:e + 1])]))
        for h, part in zip(hosts, parts):
            h.phase(0, part)
        for h, part in zip(hosts, parts):
            for p in range(1, h.ex.n_phases):
                h.phase(p, part)

    res = pl.pallas_call(
        body, name=name, in_specs=[ANY] * sum(n_in), out_specs=[ANY] * sum(n_out),
        out_shape=[s for h in hosts for s in h.out_shape], scratch_shapes=[s for h in hosts for s in h.scratch],
    )(*[a for h in hosts for a in h.args])
    return [res[sum(n_out[:e]):sum(n_out[:e + 1])] for e in range(len(hosts))]


def _ffn_fwd(x, g_pre, wgu, wd, g_post, target, name, exchange=None):
    t = x.shape[0]
    tm = _token_tile(t)
    n_i = t // tm
    with_loss = target is not None
    host = _Host(exchange)
    n_in, n_out = (6, 6) if with_loss else (5, 4)

    def body(*refs):
        own, ex_refs = host.split(refs, n_in, n_out, 0)
        if with_loss:
            x_ref, gpre_ref, wgu_ref, wd_ref, gpost_ref, tgt_ref, xo_ref, n_ref, df_ref, gu_ref, dgpost_ref, loss_ref = own
            _zero_at_first(pl.program_id(0) == 0, dgpost_ref)
        else:
            x_ref, gpre_ref, wgu_ref, wd_ref, gpost_ref, xo_ref, f_ref, n_ref, gu_ref = own
        host.at_steps(pl.program_id(0), n_i, ex_refs)
        x = x_ref[...]
        n = _rms_fwd(x, gpre_ref[...]).astype(BF16)
        n_ref[...] = n
        f = None
        for j in range(N_CHUNK):
            gate = _dot_nt(n, wgu_ref[0, j])
            up = _dot_nt(n, wgu_ref[1, j])
            gu_ref[0, j] = gate.astype(BF16)
            gu_ref[1, j] = up.astype(BF16)
            part = _dot((gate * _sigmoid(gate) * up).astype(BF16), wd_ref[j])
            f = part if f is None else f + part
        xo = x + 0.5 * _rms_fwd(f, gpost_ref[...])
        if with_loss:
            err = xo - tgt_ref[...]
            d_out = err * (1.0 / D_MODEL)
            xo_ref[...] = d_out
            df, dg = _rms_bwd(f, gpost_ref[...], 0.5 * d_out)
            df_ref[...] = df.astype(BF16)
            dgpost_ref[...] += dg
            part = 0.5 * jnp.sum(jnp.sum(err * err, axis=-1, keepdims=True) * (1.0 / D_MODEL), axis=0, keepdims=True)
            loss_ref[...] = jnp.broadcast_to(part, loss_ref.shape)
        else:
            f_ref[...] = f
            xo_ref[...] = xo

    tok = pl.BlockSpec((tm, D_MODEL), lambda i: (i, 0))
    vec = pl.BlockSpec((1, D_MODEL), lambda i: (0, 0))
    act = pl.BlockSpec((2, N_CHUNK, tm, CHUNK), lambda i: (0, 0, i, 0))
    tok_f32 = jax.ShapeDtypeStruct((t, D_MODEL), F32)
    tok_bf16 = jax.ShapeDtypeStruct((t, D_MODEL), BF16)
    act_shape = jax.ShapeDtypeStruct((2, N_CHUNK, t, CHUNK), BF16)
    in_specs = [tok, vec,
                pl.BlockSpec((2, N_CHUNK, CHUNK, D_MODEL), lambda i: (0, 0, 0, 0), pipeline_mode=pl.Buffered(1)),
                pl.BlockSpec((N_CHUNK, CHUNK, D_MODEL), lambda i: (0, 0, 0), pipeline_mode=pl.Buffered(1)),
                vec]
    args = [x, g_pre, wgu, wd, g_post]
    if with_loss:
        in_specs.append(tok)
        args.append(target)
        out_shape = [tok_f32, tok_bf16, tok_bf16, act_shape, jax.ShapeDtypeStruct((1, D_MODEL), F32),
                     jax.ShapeDtypeStruct((n_i * 8, 128), F32)]
        out_specs = [tok, tok, tok, act, vec, pl.BlockSpec((8, 128), lambda i: (i, 0))]
    else:
        out_shape = [tok_f32, tok_f32, tok_bf16, act_shape]
        out_specs = [tok, tok, tok, act]
    res = pl.pallas_call(
        body, name=name, grid=(n_i,), in_specs=in_specs + host.in_specs, out_specs=out_specs + host.out_specs,
        out_shape=out_shape + host.out_shape, scratch_shapes=host.scratch, compiler_params=_params(1),
    )(*args, *host.args)
    return (*res[:n_out], list(res[n_out:]))


def _ffn_bwd_w(n, df, gu, wd, name, exchange=None):
    t = n.shape[0]
    tm = _ffn_bwd_tile(t)
    n_i = t // tm
    host = _Host(exchange)

    def body(*refs):
        (n_ref, df_ref, gu_ref, wd_ref, dgu_ref, dwgu_ref, dwd_ref), ex_refs = host.split(refs, 4, 3, 0)
        i = pl.program_id(1)
        host.at_steps(pl.program_id(0) * n_i + i, N_CHUNK * n_i, ex_refs)
        _zero_at_first(i == 0, dwgu_ref, dwd_ref)
        nb = n_ref[...]
        dfb = df_ref[...]
        gate = gu_ref[0, 0].astype(F32)
        up = gu_ref[1, 0].astype(F32)
        s = _sigmoid(gate)
        silu = gate * s
        a = (silu * up).astype(BF16)
        da = _dot_nt(dfb, wd_ref[0])
        dup = (da * silu).astype(BF16)
        dgate = (da * up * (s * (1.0 + gate * (1.0 - s)))).astype(BF16)
        dgu_ref[0, 0] = dgate
        dgu_ref[1, 0] = dup
        dwgu_ref[0, 0] += _dot_tn(nb, dgate)
        dwgu_ref[1, 0] += _dot_tn(nb, dup)
        dwd_ref[0] += _dot_tn(a, dfb)

    tok = pl.BlockSpec((tm, D_MODEL), lambda j, i: (i, 0))
    act = pl.BlockSpec((2, 1, tm, CHUNK), lambda j, i: (0, j, i, 0))
    wgu_spec = pl.BlockSpec((2, 1, D_MODEL, CHUNK), lambda j, i: (0, j, 0, 0), pipeline_mode=pl.Buffered(1))
    wd_spec = pl.BlockSpec((1, CHUNK, D_MODEL), lambda j, i: (j, 0, 0), pipeline_mode=pl.Buffered(1))
    res = pl.pallas_call(
        body, name=name, grid=(N_CHUNK, n_i),
        in_specs=[tok, tok, act, wd_spec] + host.in_specs,
        out_specs=[act, wgu_spec, wd_spec] + host.out_specs,
        out_shape=[jax.ShapeDtypeStruct((2, N_CHUNK, t, CHUNK), BF16),
                   jax.ShapeDtypeStruct((2, N_CHUNK, D_MODEL, CHUNK), F32),
                   jax.ShapeDtypeStruct((N_CHUNK, CHUNK, D_MODEL), F32)] + host.out_shape,
        scratch_shapes=host.scratch, compiler_params=_params(2),
    )(n, df, gu, wd, *host.args)
    return (*res[:3], list(res[3:]))


def _ffn_bwd_x(dgu, wgu, x, g_pre, d_out, name, exchange=None):
    t = x.shape[0]
    tm = _token_tile(t)
    n_i = t // tm
    host = _Host(exchange)

    def body(*refs):
        (dgu_ref, wgu_ref, x_ref, gpre_ref, do_ref, dx_ref, dgpre_ref), ex_refs = host.split(refs, 5, 2, 0)
        i = pl.program_id(0)
        host.at_steps(i, n_i, ex_refs)
        _zero_at_first(i == 0, dgpre_ref)
        dn = _dot(dgu_ref[0, 0], wgu_ref[0, 0]) + _dot(dgu_ref[1, 0], wgu_ref[1, 0])
        for j in range(1, N_CHUNK):
            dn = dn + _dot(dgu_ref[0, j], wgu_ref[0, j]) + _dot(dgu_ref[1, j], wgu_ref[1, j])
        dx, dg = _rms_bwd(x_ref[...], gpre_ref[...], dn)
        dx_ref[...] = do_ref[...] + dx
        dgpre_ref[...] += dg

    tok = pl.BlockSpec((tm, D_MODEL), lambda i: (i, 0))
    vec = pl.BlockSpec((1, D_MODEL), lambda i: (0, 0))
    res = pl.pallas_call(
        body, name=name, grid=(n_i,),
        in_specs=[pl.BlockSpec((2, N_CHUNK, tm, CHUNK), lambda i: (0, 0, i, 0)),
                  pl.BlockSpec((2, N_CHUNK, CHUNK, D_MODEL), lambda i: (0, 0, 0, 0), pipeline_mode=pl.Buffered(1)),
                  tok, vec, tok] + host.in_specs,
        out_specs=[tok, vec] + host.out_specs,
        out_shape=[jax.ShapeDtypeStruct((t, D_MODEL), F32), jax.ShapeDtypeStruct((1, D_MODEL), F32)] + host.out_shape,
        scratch_shapes=host.scratch, compiler_params=_params(1),
    )(dgu, wgu, x, g_pre, d_out, *host.args)
    return (*res[:2], list(res[2:]))


def _mix_in_fwd(x1, g, w_in):
    t = x1.shape[0]
    tm = _token_tile(t)

    def body(x_ref, g_ref, w_ref, xl_ref, gl_ref, q_ref, kv_ref):
        n = _rms_fwd(x_ref[...], g_ref[...]).astype(BF16)
        proj = _dot_nt(n, w_ref[...])
        xl_ref[...] = proj[:, 0:512]
        gl_ref[...] = proj[:, 512:1024]
        q_ref[...] = proj[:, 1024:1536].astype(BF16)
        kv_ref[...] = proj[:, 1536:2048].astype(BF16)

    tok = pl.BlockSpec((tm, D_MODEL), lambda i: (i, 0))
    half = pl.BlockSpec((tm, 512), lambda i: (i, 0))
    return pl.pallas_call(
        body, name="mix_in_fwd", grid=(t // tm,),
        in_specs=[tok, pl.BlockSpec((1, D_MODEL), lambda i: (0, 0)), pl.BlockSpec((D_IN_DUP, D_MODEL), lambda i: (0, 0))],
        out_specs=[half, half, half, half],
        out_shape=[jax.ShapeDtypeStruct((t, 512), F32), jax.ShapeDtypeStruct((t, 512), F32),
                   jax.ShapeDtypeStruct((t, 512), BF16), jax.ShapeDtypeStruct((t, 512), BF16)],
        compiler_params=_params(1),
    )(x1, g, w_in)


def _shift_down(x, before, s):
    if s == 0:
        return x
    rolled = pltpu.roll(x, s, 0)
    ext = jnp.concatenate([before, x[0:8]], axis=0)
    first8 = pltpu.roll(ext, s, 0)[8:16]
    return jnp.concatenate([first8, rolled[8:]], axis=0)


def _shift_up(x, after, s):
    if s == 0:
        return x
    rows = x.shape[0]
    rolled = pltpu.roll(x, rows - s, 0)
    ext = jnp.concatenate([x[rows - 8:rows], after], axis=0)
    last8 = pltpu.roll(ext, 16 - s, 0)[0:8]
    return jnp.concatenate([rolled[:rows - 8], last8], axis=0)


def _log_sigmoid(x):
    e = jnp.exp(-jnp.abs(x))
    log1p_e = jnp.where(e < 0.01, e * (1.0 - e * (0.5 - e * (1.0 / 3.0))), jnp.log(1.0 + e))
    return jnp.minimum(x, 0.0) - log1p_e


def _lru_gates(xc, p_ref, wrg, wig):
    xcb = xc.astype(BF16)
    r = _sigmoid(_dot(xcb, wrg) + p_ref[1:2, :])
    ig = _sigmoid(_dot(xcb, wig) + p_ref[2:3, :])
    ls = _log_sigmoid(p_ref[3:4, :])
    log_a = LRU_C * r * ls
    a = jnp.exp(log_a)
    mult = jnp.sqrt(-jnp.tanh(log_a) * (a * a + 1.0))
    return xcb, r, ig, ls, a, mult


def _conv_taps(x, before, p_ref):
    xc = x * p_ref[7:8, :]
    for s in (1, 2, 3):
        xc = xc + _shift_down(x, before, s) * p_ref[7 - s:8 - s, :]
    return xc + p_ref[0:1, :]


def _lru_block_rows(t):
    return 512 if t >= 1024 else t // 2


def _lru_fwd(xl, p, wrg2, wig2):
    t = xl.shape[0]
    tb = _lru_block_rows(t)

    def body(xl_ref, p_ref, wrg_ref, wig_ref, h_ref, x_tail, h_carry):
        tt = pl.program_id(1)

        @pl.when(tt == 0)
        def _():
            x_tail[...] = jnp.zeros_like(x_tail)
            h_carry[...] = jnp.zeros_like(h_carry)

        x = xl_ref[...]
        xc = _conv_taps(x, x_tail[...], p_ref)
        x_tail[...] = x[tb - 8:tb]
        _, r, ig, ls, a, mult = _lru_gates(xc, p_ref, wrg_ref[0], wig_ref[0])
        u = mult * ig * xc
        row = lax.broadcasted_iota(jnp.int32, (tb, LRU_GROUP), 0)
        s = 1
        while s < tb:
            keep = row >= s
            u = jnp.where(keep, a * pltpu.roll(u, s, 0) + u, u)
            a = jnp.where(keep, a * pltpu.roll(a, s, 0), a)
            s *= 2
        h = u + a * h_carry[0:1, :]
        h_ref[...] = h
        h_carry[...] = jnp.broadcast_to(h[tb - 1:tb], h_carry.shape)

    blk = pl.BlockSpec((tb, LRU_GROUP), lambda g, tt: (tt, g))
    par = pl.BlockSpec((8, LRU_GROUP), lambda g, tt: (0, g))
    wsp = pl.BlockSpec((1, LRU_GROUP, LRU_GROUP), lambda g, tt: (g, 0, 0))
    return pl.pallas_call(
        body, name="lru_fwd", grid=(N_LRU_GROUP, t // tb), in_specs=[blk, par, wsp, wsp], out_specs=blk,
        out_shape=jax.ShapeDtypeStruct((t, D_LRU), F32),
        scratch_shapes=[pltpu.VMEM((8, LRU_GROUP), F32), pltpu.VMEM((8, LRU_GROUP), F32)],
        compiler_params=_params(2),
    )(xl, p, wrg2, wig2)


def _lru_bwd(dy, h, xl, gl, p, wrg2, wig2):
    t = xl.shape[0]
    tb = _lru_block_rows(t)
    n_tb = t // tb
    tb8 = tb // 8

    def body(dy_ref, h_ref, hprev_ref, xl_ref, xprev_ref, gl_ref, p_ref, wrg_ref, wig_ref,
             dxl_ref, dgl_ref, dp_ref, dwrg_ref, dwig_ref, g_carry, a_carry, dxc_head):
        step = pl.program_id(1)
        tt = n_tb - 1 - step
        first = step == 0

        _zero_at_first(first, g_carry, a_carry, dxc_head, dp_ref, dwrg_ref, dwig_ref)

        has_prev = (tt > 0).astype(F32)
        x = xl_ref[...]
        x_before = xprev_ref[...] * has_prev
        xs = [_shift_down(x, x_before, s) for s in range(4)]
        xc = xs[0] * p_ref[7:8, :] + xs[1] * p_ref[6:7, :] + xs[2] * p_ref[5:6, :] + xs[3] * p_ref[4:5, :] + p_ref[0:1, :]
        wrg = wrg_ref[0]
        wig = wig_ref[0]
        xcb, r, ig, ls, a, mult = _lru_gates(xc, p_ref, wrg, wig)

        hh = h_ref[...]
        h_m1 = _shift_down(hh, hprev_ref[...] * has_prev, 1)
        ge, dge = _gelu(gl_ref[...])
        dy = dy_ref[...]
        dgl_ref[...] = dy * hh * dge
        dh = dy * ge

        b = _shift_up(a, a_carry[...], 1)
        row = lax.broadcasted_iota(jnp.int32, (tb, LRU_GROUP), 0)
        g = dh
        s = 1
        while s < tb:
            keep = row < tb - s
            g = jnp.where(keep, b * pltpu.roll(g, tb - s, 0) + g, g)
            b = jnp.where(keep, b * pltpu.roll(b, tb - s, 0), b)
            s *= 2
        g = g + b * g_carry[0:1, :]
        g_carry[...] = jnp.broadcast_to(g[0:1], g_carry.shape)
        a_carry[...] = jnp.broadcast_to(a[0:1], a_carry.shape)

        da = g * h_m1
        dmult = g * ig * xc
        dig = g * mult * xc
        dxc = g * mult * ig
        dlog_a = da * a - dmult * (a * a) / mult
        dr = dlog_a * (LRU_C * ls)
        dls = jnp.sum(dlog_a * (LRU_C * r), axis=0, keepdims=True)
        dlam = dls * _sigmoid(-p_ref[3:4, :])
        dpre_r = dr * r * (1.0 - r)
        dpre_i = dig * ig * (1.0 - ig)
        dprb = dpre_r.astype(BF16)
        dpib = dpre_i.astype(BF16)
        dxc = dxc + _dot_nt(dprb, wrg) + _dot_nt(dpib, wig)
        dwrg_ref[0] += _dot_tn(xcb, dprb)
        dwig_ref[0] += _dot_tn(xcb, dpib)

        after = dxc_head[...]
        dxl = dxc * p_ref[7:8, :]
        for s in (1, 2, 3):
            dxl = dxl + _shift_up(dxc, after, s) * p_ref[7 - s:8 - s, :]
        dxl_ref[...] = dxl
        dxc_head[...] = dxc[0:8]

        rows = [jnp.sum(dxc, axis=0, keepdims=True), jnp.sum(dpre_r, axis=0, keepdims=True),
                jnp.sum(dpre_i, axis=0, keepdims=True), dlam]
        rows += [jnp.sum(dxc * xs[3 - k], axis=0, keepdims=True) for k in range(4)]
        dp_ref[...] += jnp.concatenate(rows, axis=0)

    blk = pl.BlockSpec((tb, LRU_GROUP), lambda g, s: (n_tb - 1 - s, g))
    prev8 = pl.BlockSpec((8, LRU_GROUP), lambda g, s: (jnp.maximum((n_tb - 1 - s) * tb8 - 1, 0), g))
    par = pl.BlockSpec((8, LRU_GROUP), lambda g, s: (0, g))
    wsp = pl.BlockSpec((1, LRU_GROUP, LRU_GROUP), lambda g, s: (g, 0, 0))
    return pl.pallas_call(
        body, name="lru_bwd", grid=(N_LRU_GROUP, n_tb),
        in_specs=[blk, blk, prev8, blk, prev8, blk, par, wsp, wsp], out_specs=[blk, blk, par, wsp, wsp],
        out_shape=[jax.ShapeDtypeStruct((t, D_LRU), F32), jax.ShapeDtypeStruct((t, D_LRU), F32),
                   jax.ShapeDtypeStruct((8, D_LRU), F32),
                   jax.ShapeDtypeStruct((N_LRU_GROUP, LRU_GROUP, LRU_GROUP), F32),
                   jax.ShapeDtypeStruct((N_LRU_GROUP, LRU_GROUP, LRU_GROUP), F32)],
        scratch_shapes=[pltpu.VMEM((8, LRU_GROUP), F32)] * 3,
        compiler_params=_params(2),
    )(dy, h, h, xl, xl, gl, p, wrg2, wig2)


def _attn_bias(first_block):
    qi = jnp.bitwise_and(lax.broadcasted_iota(jnp.int32, (4 * BLOCK_Q, 2 * BLOCK_Q), 0), BLOCK_Q - 1)
    kj = lax.broadcasted_iota(jnp.int32, (4 * BLOCK_Q, 2 * BLOCK_Q), 1)
    rel = qi + BLOCK_Q - kj
    mask = (rel >= 0) & (rel < BLOCK_Q)
    if first_block:
        mask = mask & (kj >= BLOCK_Q)
    return jnp.where(mask, 0.0, MASK_VALUE)


def _sink_column(sinks):
    hrow = lax.broadcasted_iota(jnp.int32, (4 * BLOCK_Q, 1), 0)
    return jnp.where(hrow < BLOCK_Q, sinks[0],
                     jnp.where(hrow < 2 * BLOCK_Q, sinks[1], jnp.where(hrow < 3 * BLOCK_Q, sinks[2], sinks[3])))


def _attn_scores(qv, kvv, n, bias, sk, lo):
    r0 = pl.multiple_of(n * BLOCK_Q, BLOCK_Q)
    rp = pl.multiple_of(jnp.maximum(n - 1, 0) * BLOCK_Q, BLOCK_Q)
    kvb = jnp.concatenate([kvv[pl.ds(rp, BLOCK_Q), :], kvv[pl.ds(r0, BLOCK_Q), :]], axis=0)
    k2 = kvb[:, 0:128]
    v2 = kvb[:, 128:256]
    qs = _stack_heads(qv[pl.ds(r0, BLOCK_Q), :], lo)
    s = _dot_nt(qs, k2) * ATTN_SCALE + bias
    m = jnp.maximum(jnp.max(s, axis=-1, keepdims=True), sk)
    e = jnp.exp(s - m)
    es = jnp.exp(sk - m)
    inv = 1.0 / (jnp.sum(e, axis=-1, keepdims=True) + es)
    return r0, rp, qs, k2, v2, e * inv, es * inv


def _stack_heads(pair2, lo):
    p0 = pair2[:, 0:128]
    p1 = pair2[:, 128:256]
    z = jnp.zeros_like(p0)
    return jnp.concatenate([jnp.where(lo, p0, z), jnp.where(lo, z, p0), jnp.where(lo, p1, z), jnp.where(lo, z, p1)], axis=0)


def _unstack_heads(st, lo):
    b = BLOCK_Q
    return jnp.concatenate([jnp.where(lo, st[0:b], st[b:2 * b]), jnp.where(lo, st[2 * b:3 * b], st[3 * b:4 * b])], axis=1)


def _attn_fwd(q, kv, sinks):
    t = q.shape[0]
    n_blk = t // BLOCK_Q

    def body(q_hbm, kv_hbm, s_ref, o_hbm, qv, kvv, ov, bias0, bias, sem):
        lo = lax.broadcasted_iota(jnp.int32, (BLOCK_Q, 128), 1) < HEAD_DIM
        bias0[...] = _attn_bias(True)
        bias[...] = _attn_bias(False)
        for g in range(2):
            cols = pl.ds(256 * g, 256)
            loads = [pltpu.make_async_copy(q_hbm.at[:, cols], qv, sem.at[0]),
                     pltpu.make_async_copy(kv_hbm.at[:, cols], kvv, sem.at[1])]
            for cp in loads:
                cp.start()
            for cp in loads:
                cp.wait()
            sk = _sink_column([s_ref[0, 4 * g + i] for i in range(4)])

            def block(n, bias_ref):
                r0, _, _, _, v2, prob, _ = _attn_scores(qv, kvv, n, bias_ref[...], sk, lo)
                ov[pl.ds(r0, BLOCK_Q), :] = _unstack_heads(_dot(prob.astype(BF16), v2), lo)

            block(0, bias0)

            def later(n, carry):
                block(n, bias)
                return carry

            lax.fori_loop(1, n_blk, later, 0, unroll=2)
            store = pltpu.make_async_copy(ov, o_hbm.at[:, cols], sem.at[2])
            store.start()
            store.wait()

    return pl.pallas_call(
        body, name="attn_fwd", in_specs=[ANY, ANY, SMEM], out_specs=ANY,
        out_shape=jax.ShapeDtypeStruct((t, D_ATTN), F32),
        scratch_shapes=[pltpu.VMEM((t, 256), BF16), pltpu.VMEM((t, 256), BF16), pltpu.VMEM((t, 256), F32),
                        pltpu.VMEM((4 * BLOCK_Q, 2 * BLOCK_Q), F32), pltpu.VMEM((4 * BLOCK_Q, 2 * BLOCK_Q), F32),
                        pltpu.SemaphoreType.DMA((3,))],
        compiler_params=_params(),
    )(q, kv, sinks)


def _attn_bwd(q, kv, do, sinks, exchange=None):
    t = q.shape[0]
    n_blk = t // BLOCK_Q
    host = _Host(exchange)

    def body(*refs):
        own, ex_refs = host.split(refs, 4, 3, 9)
        q_hbm, kv_hbm, do_hbm, s_ref, dq_hbm, dkv_hbm, dsink_ref, qv, kvv, dov, dqv, dkvv, ds_acc, bias0, bias, sem = own
        host.phase(0, ex_refs)
        lo = lax.broadcasted_iota(jnp.int32, (BLOCK_Q, 128), 1) < HEAD_DIM
        bias0[...] = _attn_bias(True)
        bias[...] = _attn_bias(False)
        for g in range(2):
            cols = pl.ds(256 * g, 256)
            loads = [pltpu.make_async_copy(q_hbm.at[:, cols], qv, sem.at[0]),
                     pltpu.make_async_copy(kv_hbm.at[:, cols], kvv, sem.at[1]),
                     pltpu.make_async_copy(do_hbm.at[:, cols], dov, sem.at[2])]
            for cp in loads:
                cp.start()
            for cp in loads:
                cp.wait()
            sk = _sink_column([s_ref[0, 4 * g + i] for i in range(4)])
            ds_acc[...] = jnp.zeros_like(ds_acc)

            def block(n, bias_ref, has_prev):
                r0, rp, qs, k2, v2, prob, psink = _attn_scores(qv, kvv, n, bias_ref[...], sk, lo)
                pb = prob.astype(BF16)
                dos = _stack_heads(dov[pl.ds(r0, BLOCK_Q), :], lo)
                dp = _dot_nt(dos, v2)
                dsum = jnp.sum(prob * dp, axis=-1, keepdims=True)
                dsb = (prob * (dp - dsum) * ATTN_SCALE).astype(BF16)
                ds_acc[...] -= psink * dsum
                dqv[pl.ds(r0, BLOCK_Q), :] = _unstack_heads(_dot(dsb, k2), lo).astype(BF16)
                dk2 = _dot_tn(dsb, qs)
                dv2 = _dot_tn(pb, dos)
                dkvv[pl.ds(r0, BLOCK_Q), :] = jnp.concatenate([dk2[BLOCK_Q:], dv2[BLOCK_Q:]], axis=1)
                if has_prev:
                    dkvv[pl.ds(rp, BLOCK_Q), :] += jnp.concatenate([dk2[:BLOCK_Q], dv2[:BLOCK_Q]], axis=1)

            block(0, bias0, False)

            def later(n, carry):
                block(n, bias, True)
                return carry

            lax.fori_loop(1, n_blk, later, 0, unroll=2)
            for i in range(4):
                tot = jnp.sum(ds_acc[BLOCK_Q * i:BLOCK_Q * (i + 1), :], axis=0, keepdims=True)
                dsink_ref[4 * g + i:4 * g + i + 1, :] = jnp.broadcast_to(tot, (1, 128))
            stores = [pltpu.make_async_copy(dqv, dq_hbm.at[:, cols], sem.at[0]),
                      pltpu.make_async_copy(dkvv, dkv_hbm.at[:, cols], sem.at[1])]
            for cp in stores:
                cp.start()
            for cp in stores:
                cp.wait()
        if exchange is not None:
            for p in range(1, exchange.n_phases):
                host.phase(p, ex_refs)

    res = pl.pallas_call(
        body, name="attn_bwd", in_specs=[ANY, ANY, ANY, SMEM] + host.in_specs,
        out_specs=[ANY, ANY, pl.BlockSpec(memory_space=pltpu.VMEM)] + host.out_specs,
        out_shape=[jax.ShapeDtypeStruct((t, D_ATTN), BF16), jax.ShapeDtypeStruct((t, 512), F32),
                   jax.ShapeDtypeStruct((8, 128), F32)] + host.out_shape,
        scratch_shapes=[pltpu.VMEM((t, 256), BF16), pltpu.VMEM((t, 256), BF16), pltpu.VMEM((t, 256), BF16),
                        pltpu.VMEM((t, 256), BF16), pltpu.VMEM((t, 256), F32), pltpu.VMEM((4 * BLOCK_Q, 1), F32),
                        pltpu.VMEM((4 * BLOCK_Q, 2 * BLOCK_Q), F32), pltpu.VMEM((4 * BLOCK_Q, 2 * BLOCK_Q), F32),
                        pltpu.SemaphoreType.DMA((3,))] + host.scratch,
        compiler_params=_params(),
    )(q, kv, do, sinks, *host.args)
    return (*res[:3], list(res[3:]))


def _mix_out_fwd(x1, h, gl, o, g_lru, g_attn, g_post, w_o):
    t = x1.shape[0]
    tm = _token_tile(t)

    def body(x_ref, h_ref, gl_ref, o_ref, g1_ref, g2_ref, gp_ref, w_ref, x2_ref, m_ref):
        y = h_ref[...] * _gelu(gl_ref[...])[0]
        yn1 = _rms_fwd(y, g1_ref[...]).astype(BF16)
        yn2 = _rms_fwd(o_ref[...], g2_ref[...]).astype(BF16)
        m = _dot(yn1, w_ref[0:512, :]) + _dot(yn2, w_ref[512:1024, :])
        m_ref[...] = m
        x2_ref[...] = x_ref[...] + _rms_fwd(m, gp_ref[...])

    tok = pl.BlockSpec((tm, D_MODEL), lambda i: (i, 0))
    half = pl.BlockSpec((tm, 512), lambda i: (i, 0))
    vec = pl.BlockSpec((1, D_MODEL), lambda i: (0, 0))
    hvec = pl.BlockSpec((1, 512), lambda i: (0, 0))
    return pl.pallas_call(
        body, name="mix_out_fwd", grid=(t // tm,),
        in_specs=[tok, half, half, half, hvec, hvec, vec, pl.BlockSpec((D_MODEL, D_MODEL), lambda i: (0, 0))],
        out_specs=[tok, tok],
        out_shape=[jax.ShapeDtypeStruct((t, D_MODEL), F32), jax.ShapeDtypeStruct((t, D_MODEL), F32)],
        compiler_params=_params(1),
    )(x1, h, gl, o, g_lru, g_attn, g_post, w_o)


def _mix_out_bwd(dx2, m, h, gl, o, g_lru, g_attn, g_post, w_o):
    t = dx2.shape[0]
    tm = _token_tile(t)

    def body(dx_ref, m_ref, h_ref, gl_ref, o_ref, g1_ref, g2_ref, gp_ref, w_ref,
             dy_ref, do_ref, dw_ref, dgp_ref, dg1_ref, dg2_ref):
        _zero_at_first(pl.program_id(0) == 0, dw_ref, dgp_ref, dg1_ref, dg2_ref)
        dm, dgp = _rms_bwd(m_ref[...], gp_ref[...], dx_ref[...])
        dmb = dm.astype(BF16)
        y = h_ref[...] * _gelu(gl_ref[...])[0]
        o = o_ref[...]
        yn1 = _rms_fwd(y, g1_ref[...]).astype(BF16)
        yn2 = _rms_fwd(o, g2_ref[...]).astype(BF16)
        dw_ref[0:512, :] += _dot_tn(yn1, dmb)
        dw_ref[512:1024, :] += _dot_tn(yn2, dmb)
        dy, dg1 = _rms_bwd(y, g1_ref[...], _dot_nt(dmb, w_ref[0:512, :]))
        do, dg2 = _rms_bwd(o, g2_ref[...], _dot_nt(dmb, w_ref[512:1024, :]))
        dy_ref[...] = dy
        do_ref[...] = do.astype(BF16)
        dgp_ref[...] += dgp
        dg1_ref[...] += dg1
        dg2_ref[...] += dg2

    tok = pl.BlockSpec((tm, D_MODEL), lambda i: (i, 0))
    half = pl.BlockSpec((tm, 512), lambda i: (i, 0))
    vec = pl.BlockSpec((1, D_MODEL), lambda i: (0, 0))
    hvec = pl.BlockSpec((1, 512), lambda i: (0, 0))
    mat = pl.BlockSpec((D_MODEL, D_MODEL), lambda i: (0, 0))
    return pl.pallas_call(
        body, name="mix_out_bwd", grid=(t // tm,),
        in_specs=[tok, tok, half, half, half, hvec, hvec, vec, mat],
        out_specs=[half, half, mat, vec, hvec, hvec],
        out_shape=[jax.ShapeDtypeStruct((t, 512), F32), jax.ShapeDtypeStruct((t, 512), BF16),
                   jax.ShapeDtypeStruct((D_MODEL, D_MODEL), F32), jax.ShapeDtypeStruct((1, D_MODEL), F32),
                   jax.ShapeDtypeStruct((1, 512), F32), jax.ShapeDtypeStruct((1, 512), F32)],
        compiler_params=_params(1),
    )(dx2, m, h, gl, o, g_lru, g_attn, g_post, w_o)


def _mix_in_bwd(dx2, x1, g, dxl, dgl, dq, dkv, w_in, f1, g_post1):
    t = x1.shape[0]
    tm = _token_tile(t)

    def body(dx2_ref, x_ref, g_ref, dxl_ref, dgl_ref, dq_ref, dkv_ref, w_ref, f1_ref, gp1_ref,
             dx1_ref, dw_ref, dg_ref, df1_ref, dgp1_ref):
        _zero_at_first(pl.program_id(0) == 0, dw_ref, dg_ref, dgp1_ref)
        x = x_ref[...]
        nb = _rms_fwd(x, g_ref[...]).astype(BF16)
        lo = lax.broadcasted_iota(jnp.int32, (tm, 128), 1) < HEAD_DIM
        dkv = dkv_ref[...]
        folded = []
        for k in range(4):
            seg = dkv[:, 128 * k:128 * (k + 1)]
            folded.append(jnp.where(lo, seg + pltpu.roll(seg, HEAD_DIM, 1), 0.0).astype(BF16))
        dproj = jnp.concatenate([dxl_ref[...].astype(BF16), dgl_ref[...].astype(BF16), dq_ref[...]] + folded, axis=1)
        dw_ref[...] += _dot_tn(nb, dproj)
        dx, dg = _rms_bwd(x, g_ref[...], _dot(dproj, w_ref[...]))
        dx1 = dx2_ref[...] + dx
        dx1_ref[...] = dx1
        dg_ref[...] += dg
        df1, dgp1 = _rms_bwd(f1_ref[...], gp1_ref[...], 0.5 * dx1)
        df1_ref[...] = df1.astype(BF16)
        dgp1_ref[...] += dgp1

    tok = pl.BlockSpec((tm, D_MODEL), lambda i: (i, 0))
    half = pl.BlockSpec((tm, 512), lambda i: (i, 0))
    vec = pl.BlockSpec((1, D_MODEL), lambda i: (0, 0))
    mat = pl.BlockSpec((D_IN_DUP, D_MODEL), lambda i: (0, 0))
    dmat = pl.BlockSpec((D_MODEL, D_IN_DUP), lambda i: (0, 0))
    return pl.pallas_call(
        body, name="mix_in_bwd", grid=(t // tm,),
        in_specs=[tok, tok, vec, half, half, half, half, mat, tok, vec], out_specs=[tok, dmat, vec, tok, vec],
        out_shape=[jax.ShapeDtypeStruct((t, D_MODEL), F32), jax.ShapeDtypeStruct((D_MODEL, D_IN_DUP), F32),
                   jax.ShapeDtypeStruct((1, D_MODEL), F32), jax.ShapeDtypeStruct((t, D_MODEL), BF16),
                   jax.ShapeDtypeStruct((1, D_MODEL), F32)],
        compiler_params=_params(1),
    )(dx2, x1, g, dxl, dgl, dq, dkv, w_in, f1, g_post1)


def _row_tile(rows):
    return rows if rows <= 512 else rows // 2


def _chip_sum(grad, from_sibling, other, name):
    _, rows, cols = grad.shape
    tr = _row_tile(rows)

    def body(other_ref, g_ref, s_ref, out_ref):
        out_ref[0] = (g_ref[0, 0] + s_ref[0]).astype(BF16)

    grid_spec = pltpu.PrefetchScalarGridSpec(
        num_scalar_prefetch=1, grid=(3, rows // tr),
        in_specs=[pl.BlockSpec((1, 1, tr, cols), lambda j, i, other: (other[j], other[3], i, 0)),
                  pl.BlockSpec((1, tr, cols), lambda j, i, other: (other[j], i, 0))],
        out_specs=pl.BlockSpec((1, tr, cols), lambda j, i, other: (j, i, 0)))
    return pl.pallas_call(
        body, name=name, grid_spec=grid_spec, out_shape=jax.ShapeDtypeStruct((3, rows, cols), BF16),
        compiler_params=_params(2),
    )(other, grad.reshape(4, 2, rows, cols), from_sibling)


def _adamw(w, g, m, v):
    m = ADAM_B1 * m + (1.0 - ADAM_B1) * g
    v = ADAM_B2 * v + (1.0 - ADAM_B2) * (g * g)
    m_hat = m / (1.0 - ADAM_B1 ** ADAM_STEP)
    v_hat = v / (1.0 - ADAM_B2 ** ADAM_STEP)
    delta = -ADAM_LR * (m_hat / (jnp.sqrt(v_hat) + ADAM_EPS) + ADAM_WD * w)
    return delta, m, v


def _shard_update(grad, from_sibling, from_chips, w, m, v, place, name, transposed):
    _, rows, cols = grad.shape
    tr = _row_tile(rows)

    def total(g_ref, s_ref, c_ref):
        g = g_ref[0, 0] + s_ref[0]
        g = g + c_ref[0].astype(F32)
        g = g + c_ref[1].astype(F32)
        return g + c_ref[2].astype(F32)

    part_specs = [pl.BlockSpec((1, 1, tr, cols), lambda i, place: (place[0], place[1], i, 0)),
                  pl.BlockSpec((1, tr, cols), lambda i, place: (place[0], i, 0)),
                  pl.BlockSpec((3, tr, cols), lambda i, place: (0, i, 0))]
    flat = pl.BlockSpec((tr, cols), lambda i, place: (i, 0))
    parts = (place, grad.reshape(4, 2, rows, cols), from_sibling, from_chips)
    if not transposed:
        def body(place_ref, g_ref, s_ref, c_ref, w_ref, m_ref, v_ref, go_ref, d_ref, mo_ref, vo_ref):
            g = total(g_ref, s_ref, c_ref)
            go_ref[...] = g
            d_ref[...], mo_ref[...], vo_ref[...] = _adamw(w_ref[...], g, m_ref[...], v_ref[...])

        grid_spec = pltpu.PrefetchScalarGridSpec(num_scalar_prefetch=1, grid=(rows // tr,),
                                                 in_specs=part_specs + [flat, flat, flat], out_specs=[flat] * 4)
        return pl.pallas_call(body, name=name, grid_spec=grid_spec, out_shape=[jax.ShapeDtypeStruct((rows, cols), F32)] * 4,
                              compiler_params=_params(1))(*parts, w, m, v)

    def sum_body(place_ref, g_ref, s_ref, c_ref, go_ref):
        go_ref[...] = total(g_ref, s_ref, c_ref)

    grid_spec = pltpu.PrefetchScalarGridSpec(num_scalar_prefetch=1, grid=(rows // tr,), in_specs=part_specs, out_specs=flat)
    g = pl.pallas_call(sum_body, name=name + "_sum", grid_spec=grid_spec, out_shape=jax.ShapeDtypeStruct((rows, cols), F32),
                       compiler_params=_params(1))(*parts)
    gt = jnp.transpose(g, (1, 0))
    tc = _row_tile(cols)

    def adam_body(g_ref, w_ref, m_ref, v_ref, d_ref, mo_ref, vo_ref):
        d_ref[...], mo_ref[...], vo_ref[...] = _adamw(w_ref[...], g_ref[...], m_ref[...], v_ref[...])

    blk = pl.BlockSpec((tc, rows), lambda i: (i, 0))
    res = pl.pallas_call(adam_body, name=name + "_adam", grid=(cols // tc,), in_specs=[blk] * 4, out_specs=[blk] * 3,
                         out_shape=[jax.ShapeDtypeStruct((cols, rows), F32)] * 3, compiler_params=_params(1))(gt, w, m, v)
    return (gt, *res)


GAINS = ("ffn1_pre_g", "ffn1_post_g", "mix_pre_g", "mix_post_g", "ffn2_pre_g", "ffn2_post_g")
HALVES = ("conv_b", "b_rg", "b_ig", "lru_lambda", "g_lru_out", "g_attn_out")
GATES = ("w_rg", "w_ig")
SMALL = GAINS + HALVES + GATES + ("sinks", "conv_w")


def _small_update(gathered, w, m, v):
    n_small = len(SMALL)

    def body(*refs):
        ga_ref, gb_ref, gc_ref, gd_ref, g0_ref = refs[:5]
        wmv = refs[5:5 + 3 * n_small]
        outs = refs[5 + 3 * n_small:5 + 7 * n_small]
        loss_ref = refs[5 + 7 * n_small]

        def total(ref):
            s = ref[0]
            for d in range(1, N_DEV):
                s = s + ref[d]
            return s

        sa, sb, sc, sd = total(ga_ref), total(gb_ref), total(gc_ref), total(gd_ref)
        grads = {}
        for i, k in enumerate(GAINS):
            grads[k] = sa[i:i + 1]
        grads[GAINS[0]] = total(g0_ref)
        for i, k in enumerate(HALVES):
            grads[k] = sb[i:i + 1]
        grads["w_rg"], grads["w_ig"] = sc[0:512], sc[512:1024]
        grads["sinks"] = sd[4:5, 0:8]
        grads["conv_w"] = sd[0:4]
        for i, k in enumerate(SMALL):
            g = grads[k]
            outs[4 * i][...] = g
            outs[4 * i + 1][...], outs[4 * i + 2][...], outs[4 * i + 3][...] = _adamw(
                wmv[3 * i][...], g, wmv[3 * i + 1][...], wmv[3 * i + 2][...])
        loss_ref[...] = jnp.broadcast_to(sd[5:6, 0:128], loss_ref.shape)

    operands = list(gathered)
    out_shape = []
    for k in SMALL:
        operands += [w[k], m[k], v[k]]
        out_shape += [jax.ShapeDtypeStruct(w[k].shape, F32)] * 4
    out_shape.append(jax.ShapeDtypeStruct((8, 128), F32))
    res = pl.pallas_call(body, name="small_update", out_shape=out_shape, compiler_params=_params())(*operands)
    parts = [{k: res[4 * i + j] for i, k in enumerate(SMALL)} for j in range(4)]
    return (*parts, res[-1])


def _dup_in_rows(wt):
    k0, k1, v0, v1 = wt[1536:1600], wt[1600:1664], wt[1664:1728], wt[1728:1792]
    return jnp.concatenate([wt[:1536], k0, k0, v0, v0, k1, k1, v1, v1], axis=0)


def _undup_in_columns(dw):
    return jnp.concatenate([dw[:, :1536], dw[:, 1536:1600], dw[:, 1792:1856], dw[:, 1664:1728], dw[:, 1920:1984]], axis=1)


def _pair_block_diag(w):
    w = w.reshape(N_LRU_GROUP, 2, 64, 64)
    z = jnp.zeros((N_LRU_GROUP, 64, 64), w.dtype)
    top = jnp.concatenate([w[:, 0], z], axis=2)
    bot = jnp.concatenate([z, w[:, 1]], axis=2)
    return jnp.concatenate([top, bot], axis=1)


def _pair_block_diag_grad(dw2):
    return jnp.stack([dw2[:, :64, :64], dw2[:, 64:, 64:]], axis=1).reshape(512, 64)


def kernel(x, ffn1_pre_g, ffn1_w_gu, ffn1_w_down, ffn1_post_g, mix_pre_g, w_in, conv_w, conv_b, w_rg, b_rg, w_ig, b_ig, lru_lambda, sinks, g_lru_out, g_attn_out, w_o, mix_post_g, ffn2_pre_g, ffn2_w_gu, ffn2_w_down, ffn2_post_g, loss_target, m_ffn1_pre_g, m_ffn1_w_gu, m_ffn1_w_down, m_ffn1_post_g, m_mix_pre_g, m_w_in, m_conv_w, m_conv_b, m_w_rg, m_b_rg, m_w_ig, m_b_ig, m_lru_lambda, m_sinks, m_g_lru_out, m_g_attn_out, m_w_o, m_mix_post_g, m_ffn2_pre_g, m_ffn2_w_gu, m_ffn2_w_down, m_ffn2_post_g, v_ffn1_pre_g, v_ffn1_w_gu, v_ffn1_w_down, v_ffn1_post_g, v_mix_pre_g, v_w_in, v_conv_w, v_conv_b, v_w_rg, v_b_rg, v_w_ig, v_b_ig, v_lru_lambda, v_sinks, v_g_lru_out, v_g_attn_out, v_w_o, v_mix_post_g, v_ffn2_pre_g, v_ffn2_w_gu, v_ffn2_w_down, v_ffn2_post_g):
    args = dict(locals())
    names = ["ffn1_pre_g", "ffn1_w_gu", "ffn1_w_down", "ffn1_post_g", "mix_pre_g", "w_in", "conv_w", "conv_b", "w_rg",
             "b_rg", "w_ig", "b_ig", "lru_lambda", "sinks", "g_lru_out", "g_attn_out", "w_o", "mix_post_g",
             "ffn2_pre_g", "ffn2_w_gu", "ffn2_w_down", "ffn2_post_g"]
    big = ["ffn1_w_gu", "ffn1_w_down", "w_in", "w_o", "ffn2_w_gu", "ffn2_w_down"]
    w = {k: args[k] for k in names}
    mom = {k: args["m_" + k] for k in names}
    var = {k: args["v_" + k] for k in names}
    t = x.shape[1]
    xs = x.reshape(t, D_MODEL)
    target = loss_target.reshape(t, D_MODEL)
    cx, cy, cc = _coords()
    me = 4 * cx + 2 * cy + cc
    other = jnp.stack([2 * (1 - cx) + cy, 2 * cx + (1 - cy), 2 * (1 - cx) + (1 - cy), cc]).astype(jnp.int32)
    place = jnp.stack([2 * cx + cy, cc]).astype(jnp.int32)

    transposed = ("ffn1_w_gu", "w_in", "ffn2_w_gu")

    def shard_view(a, k):
        return jnp.transpose(a[0], (1, 0)) if k in transposed else a[0]

    def shard_unview(a, k):
        return (jnp.transpose(a, (1, 0)) if k in transposed else a)[None]

    shard2d = {k: shard_view(w[k], k) for k in big}
    shard_bf = {k: shard2d[k].astype(BF16) for k in big}
    conv_pad = jnp.pad(conv_w.reshape(4, 64), ((0, 4), (0, 64)))
    (first_w,) = _run_exchanges([_Gather([shard_bf["ffn1_w_gu"], shard_bf["ffn1_w_down"]], routed=True)], "all_gather_ffn1")
    wgu1 = first_w[0].reshape(2, N_CHUNK, CHUNK, D_MODEL)
    wd1 = first_w[1].reshape(N_CHUNK, CHUNK, D_MODEL)
    rest = _Gather([shard_bf["w_in"], shard_bf["w_o"], shard_bf["ffn2_w_gu"], shard_bf["ffn2_w_down"], conv_pad])

    x1, f1, n1, gu1, gathered = _ffn_fwd(xs, ffn1_pre_g, wgu1, wd1, ffn1_post_g, None, "ffn1_fwd", rest)
    w_in_full = _dup_in_rows(gathered[0].reshape(D_IN, D_MODEL))
    w_o_full = gathered[1].reshape(D_MODEL, D_MODEL)
    wgu2 = gathered[2].reshape(2, N_CHUNK, CHUNK, D_MODEL)
    wd2 = gathered[3].reshape(N_CHUNK, CHUNK, D_MODEL)
    conv_w_full = jnp.transpose(gathered[4][:, 0:4, 0:64], (1, 0, 2)).reshape(4, D_LRU)
    p_lru = jnp.concatenate([conv_b, b_rg, b_ig, lru_lambda, conv_w_full], axis=0)
    wrg2 = _pair_block_diag(w_rg[0]).astype(BF16)
    wig2 = _pair_block_diag(w_ig[0]).astype(BF16)
    xl, gl, q, kv = _mix_in_fwd(x1, mix_pre_g, w_in_full)
    h = _lru_fwd(xl, p_lru, wrg2, wig2)
    o = _attn_fwd(q, kv, sinks)
    x2, mo = _mix_out_fwd(x1, h, gl, o, g_lru_out, g_attn_out, mix_post_g, w_o_full)
    g = {}
    dx3, n2, df2, gu2, g["ffn2_post_g"], loss_parts, _ = _ffn_fwd(x2, ffn2_pre_g, wgu2, wd2, ffn2_post_g, target, "ffn2_fwd")
    loss_local = jnp.sum(loss_parts[::8, 0])

    partial, from_sibling, from_chips = {}, {}, {}

    def chip_sums(keys):
        return [_chip_sum(partial[k], from_sibling[k], other, "chip_sum_" + k) for k in keys]

    dgu2, dwgu2, dwd2, _ = _ffn_bwd_w(n2, df2, gu2, wd2, "ffn2_bwd_w")
    partial["ffn2_w_gu"] = dwgu2.reshape(N_DEV, D_MODEL, CHUNK)
    partial["ffn2_w_down"] = dwd2.reshape(N_DEV, D_FF // N_DEV, D_MODEL)
    ffn2_keys = ["ffn2_w_gu", "ffn2_w_down"]
    dx2, g["ffn2_pre_g"], got = _ffn_bwd_x(dgu2, wgu2, x2, ffn2_pre_g, dx3, "ffn2_bwd_x",
                                           _SiblingExchange([partial[k] for k in ffn2_keys]))
    from_sibling.update(zip(ffn2_keys, got))
    dy, do, dwo, g["mix_post_g"], g["g_lru_out"], g["g_attn_out"] = _mix_out_bwd(
        dx2, mo, h, gl, o, g_lru_out, g_attn_out, mix_post_g, w_o_full)
    dq, dkv, dsink, got = _attn_bwd(q, kv, do, sinks, _ChipExchange(chip_sums(ffn2_keys)))
    from_chips.update(zip(ffn2_keys, got))
    dxl, dgl, dp, dwrg2, dwig2 = _lru_bwd(dy, h, xl, gl, p_lru, wrg2, wig2)
    dx1, dwin_dup, g["mix_pre_g"], df1, g["ffn1_post_g"] = _mix_in_bwd(
        dx2, x1, mix_pre_g, dxl, dgl, dq, dkv, w_in_full, f1, ffn1_post_g)
    partial["w_in"] = jnp.transpose(_undup_in_columns(dwin_dup).reshape(D_MODEL, N_DEV, D_IN // N_DEV), (1, 0, 2))
    partial["w_o"] = dwo.reshape(N_DEV, D_MODEL // N_DEV, D_MODEL)
    mix_keys = ["w_in", "w_o"]
    (got,) = _run_exchanges([_SiblingExchange([partial[k] for k in mix_keys])], "mix_sibling_exchange")
    from_sibling.update(zip(mix_keys, got))
    dgu1, dwgu1, dwd1, got = _ffn_bwd_w(n1, df1, gu1, wd1, "ffn1_bwd_w", _ChipExchange(chip_sums(mix_keys)))
    from_chips.update(zip(mix_keys, got))
    partial["ffn1_w_gu"] = dwgu1.reshape(N_DEV, D_MODEL, CHUNK)
    partial["ffn1_w_down"] = dwd1.reshape(N_DEV, D_FF // N_DEV, D_MODEL)
    ffn1_keys = ["ffn1_w_gu", "ffn1_w_down"]
    (got,) = _run_exchanges([_SiblingExchange([partial[k] for k in ffn1_keys])], "ffn1_sibling_exchange")
    from_sibling.update(zip(ffn1_keys, got))
    zeros2 = jnp.zeros((2, D_MODEL), F32)
    g_gains = jnp.concatenate([zeros2[:1]] + [g[k] for k in GAINS[1:]] + [zeros2], axis=0)
    g_halves = jnp.concatenate([dp[0:4], g["g_lru_out"], g["g_attn_out"], zeros2[:, :D_LRU]], axis=0)
    g_gates = jnp.concatenate([_pair_block_diag_grad(dwrg2), _pair_block_diag_grad(dwig2)], axis=0)
    g_misc = jnp.concatenate([dp[4:8], jnp.pad(dsink[:, 0].reshape(1, 8), ((0, 0), (0, D_LRU - 8))),
                              jnp.pad(loss_local.reshape(1, 1), ((0, 0), (0, D_LRU - 1))), zeros2[:, :D_LRU]], axis=0)
    dx0, g_first, got = _ffn_bwd_x(dgu1, wgu1, xs, ffn1_pre_g, dx1, "ffn1_bwd_x",
                                   _Both(_ChipExchange(chip_sums(ffn1_keys)), _Gather([g_gains, g_halves, g_gates, g_misc])))
    from_chips.update(zip(ffn1_keys, got[:2]))
    gathered_small = got[2:]

    grads, delta, new_m, new_v = {}, {}, {}, {}
    for k in big:
        res = _shard_update(partial[k], from_sibling[k], from_chips[k], shard2d[k], shard_view(mom[k], k),
                            shard_view(var[k], k), place, "update_" + k, k in transposed)
        grads[k], delta[k], new_m[k], new_v[k] = [shard_unview(r, k) for r in res]

    ((gathered_first,),) = _run_exchanges([_Gather([g_first])], "all_gather_first_gain")
    col = me * 64

    def small_view(vals):
        out = {k: vals[k] for k in GAINS + HALVES + ("sinks",)}
        out.update({k: vals[k].reshape(512, 64) for k in GATES})
        out["conv_w"] = lax.dynamic_update_slice(jnp.zeros((4, D_LRU), F32), vals["conv_w"].reshape(4, 64), (0, col))
        return out

    *small, loss_tile = _small_update([*gathered_small, gathered_first], small_view(w), small_view(mom), small_view(var))
    for dst, part in zip((grads, delta, new_m, new_v), small):
        for k in SMALL:
            if k == "conv_w":
                dst[k] = lax.dynamic_slice(part[k], (0, col), (4, 64)).reshape(conv_w.shape)
            else:
                dst[k] = part[k].reshape(w[k].shape)
    return (loss_tile[0, 0], dx0.reshape(x.shape), *[grads[k] for k in names], *[delta[k] for k in names],
            *[new_m[k] for k in names], *[new_v[k] for k in names])
```

```python
import functools

import jax
import jax.numpy as jnp
from jax import lax
from jax.experimental import pallas as pl
from jax.experimental.pallas import tpu as pltpu

F32 = jnp.float32
BF16 = jnp.bfloat16

D_MODEL = 1024
D_FF = 2816
N_DEV = 8
N_CHUNK = 4
CHUNK = D_FF // N_CHUNK
D_LRU = 512
D_ATTN = 512
LRU_GROUP = 128
N_LRU_GROUP = D_LRU // LRU_GROUP
HEAD_DIM = 64
BLOCK_Q = 128
D_IN = 1792
D_IN_DUP = 2048
RMS_EPS = 1e-6
LRU_C = 8.0
MASK_VALUE = -1e30
ATTN_SCALE = HEAD_DIM ** -0.5

ADAM_LR = 0.001
ADAM_B1 = 0.9
ADAM_B2 = 0.999
ADAM_EPS = 1e-08
ADAM_WD = 0.01
ADAM_STEP = 10

VMEM_LIMIT_V7X = 56 * 2 ** 20

ANY = pl.BlockSpec(memory_space=pl.ANY)
SMEM = pl.BlockSpec(memory_space=pltpu.SMEM)
MESH = pl.DeviceIdType.MESH


def _params(n_grid=0):
    sem = ("arbitrary",) * n_grid if n_grid else None
    return pltpu.CompilerParams(dimension_semantics=sem, vmem_limit_bytes=VMEM_LIMIT_V7X)


def _dot(a, b):
    return lax.dot_general(a, b, (((1,), (0,)), ((), ())), preferred_element_type=F32)


def _dot_nt(a, b):
    return lax.dot_general(a, b, (((1,), (1,)), ((), ())), preferred_element_type=F32)


def _dot_tn(a, b):
    return lax.dot_general(a, b, (((0,), (0,)), ((), ())), preferred_element_type=F32)


def _sigmoid(x):
    return 1.0 / (1.0 + jnp.exp(-x))


def _rms_fwd(x, g):
    r = lax.rsqrt(jnp.mean(x * x, axis=-1, keepdims=True) + RMS_EPS)
    return x * r * g


def _rms_bwd(x, g, dy):
    r = lax.rsqrt(jnp.mean(x * x, axis=-1, keepdims=True) + RMS_EPS)
    xh = x * r
    dg = jnp.sum(dy * xh, axis=0, keepdims=True)
    dxh = dy * g
    dx = r * (dxh - xh * jnp.mean(dxh * xh, axis=-1, keepdims=True))
    return dx, dg


def _gelu(x):
    c = 0.7978845608028654
    inner = c * (x + 0.044715 * x * x * x)
    th = jnp.tanh(inner)
    ge = 0.5 * x * (1.0 + th)
    dge = 0.5 * (1.0 + th) + 0.5 * x * (1.0 - th * th) * c * (1.0 + 3.0 * 0.044715 * x * x)
    return ge, dge


def _zero_at_first(first, *refs):
    @pl.when(first)
    def _():
        for ref in refs:
            ref[...] = jnp.zeros_like(ref)


def _token_tile(t):
    return 512 if t >= 2048 else t // 2


def _ffn_bwd_tile(t):
    return 1024 if t >= 4096 else t // 2


def _coords():
    return lax.axis_index("x"), lax.axis_index("y"), lax.axis_index("c")


class _Gather:
    n_phases = 3
    at = (0.0, 0.8, 1.0)

    def __init__(self, shards, routed=False):
        k = len(shards)
        self.routed = routed
        self.arrays = list(shards)
        self.out_shape = [jax.ShapeDtypeStruct((N_DEV,) + s.shape, s.dtype) for s in shards]
        self.scratch = [pltpu.SemaphoreType.DMA((7 * k,)), pltpu.SemaphoreType.DMA((7 * k,)), pltpu.SemaphoreType.DMA((k,))]

    def run(self, phase, ins, outs, sems):
        send_sems, recv_sems, local_sems = sems
        k_arr = len(ins)
        x, y, c = _coords()
        me, sibling = (x, y, c), (x, y, 1 - c)
        chips = [(1 - x, y), (x, 1 - y), (1 - x, 1 - y)]
        direct = 2 if self.routed else 3
        relay_from = (x + (1 - c) * (1 - 2 * x), y + c * (1 - 2 * y))
        relay_to = (x + c * (1 - 2 * x), y + (1 - c) * (1 - 2 * y))

        def rows(k, dev):
            return outs[k].at[4 * dev[0] + 2 * dev[1] + dev[2]]

        def copy(k, slot, block, to, src=None):
            return pltpu.make_async_remote_copy(
                src_ref=rows(k, block) if src is None else src, dst_ref=rows(k, block),
                send_sem=send_sems.at[7 * k + slot], recv_sem=recv_sems.at[7 * k + slot],
                device_id=to, device_id_type=MESH)

        def mine():
            return [pltpu.make_async_copy(ins[k], rows(k, me), local_sems.at[k]) for k in range(k_arr)]

        def first():
            return [copy(k, slot, me, to, src=ins[k]) for k in range(k_arr)
                    for slot, to in enumerate([sibling] + [(*chip, c) for chip in chips[:direct]])]

        def relayed(k):
            return copy(k, 3, (*relay_from, c), (*relay_to, c))

        def passed(j, k):
            return copy(k, 4 + j, (*chips[j], c), sibling)

        if phase == 0:
            for cp in mine() + first():
                cp.start()
        elif phase == 1:
            for j in range(direct):
                for k in range(k_arr):
                    copy(k, 1 + j, (*chips[j], c), me).wait_recv()
            for k in range(k_arr):
                if self.routed:
                    relayed(k).start()
                for j in range(direct):
                    passed(j, k).start()
        else:
            for k in range(k_arr):
                if self.routed:
                    copy(k, 3, (*chips[2], c), me).wait_recv()
                    passed(2, k).start()
            for k in range(k_arr):
                copy(k, 0, sibling, me).wait_recv()
                for j, chip in enumerate(chips):
                    copy(k, 4 + j, (*chip, 1 - c), me).wait_recv()
            sent = first() + [passed(j, k) for j in range(3) for k in range(k_arr)]
            if self.routed:
                sent += [relayed(k) for k in range(k_arr)]
            for cp in sent:
                cp.wait_send()
            for cp in mine():
                cp.wait()


class _SiblingExchange:
    n_phases = 2
    at = (0.0, 1.0)

    def __init__(self, grads):
        k = len(grads)
        self.arrays = list(grads)
        self.out_shape = [jax.ShapeDtypeStruct((4,) + g.shape[1:], g.dtype) for g in grads]
        self.scratch = [pltpu.SemaphoreType.DMA((4 * k,)), pltpu.SemaphoreType.DMA((4 * k,))]

    def run(self, phase, ins, outs, sems):
        send_sems, recv_sems = sems
        x, y, c = _coords()
        copies = [pltpu.make_async_remote_copy(
            src_ref=ins[k].at[2 * q + (1 - c)], dst_ref=outs[k].at[q],
            send_sem=send_sems.at[4 * k + q], recv_sem=recv_sems.at[4 * k + q],
            device_id=(x, y, 1 - c), device_id_type=MESH) for k in range(len(ins)) for q in range(4)]
        for cp in copies:
            if phase == 0:
                cp.start()
            else:
                cp.wait_recv()
                cp.wait_send()


class _ChipExchange:
    n_phases = 2
    at = (0.0, 1.0)

    def __init__(self, chip_sums):
        k = len(chip_sums)
        self.arrays = list(chip_sums)
        self.out_shape = [jax.ShapeDtypeStruct((3,) + s.shape[1:], s.dtype) for s in chip_sums]
        self.scratch = [pltpu.SemaphoreType.DMA((3 * k,)), pltpu.SemaphoreType.DMA((3 * k,))]

    def run(self, phase, ins, outs, sems):
        send_sems, recv_sems = sems
        x, y, c = _coords()
        chips = [(1 - x, y), (x, 1 - y), (1 - x, 1 - y)]
        copies = [pltpu.make_async_remote_copy(
            src_ref=ins[k].at[j], dst_ref=outs[k].at[j],
            send_sem=send_sems.at[3 * k + j], recv_sem=recv_sems.at[3 * k + j],
            device_id=(*chip, c), device_id_type=MESH) for k in range(len(ins)) for j, chip in enumerate(chips)]
        for cp in copies:
            if phase == 0:
                cp.start()
            else:
                cp.wait_recv()
                cp.wait_send()


class _Both:
    n_phases = 3
    at = (0.0, 0.95, 1.0)

    def __init__(self, two_phase, gather):
        self.parts = (two_phase, gather)
        self.arrays = two_phase.arrays + gather.arrays
        self.out_shape = two_phase.out_shape + gather.out_shape
        self.scratch = two_phase.scratch + gather.scratch

    def run(self, phase, ins, outs, sems):
        a, b = self.parts
        n_in, n_out, n_sem = len(a.arrays), len(a.out_shape), len(a.scratch)
        refs_a = (ins[:n_in], outs[:n_out], sems[:n_sem])
        refs_b = (ins[n_in:], outs[n_out:], sems[n_sem:])
        b.run(phase, *refs_b)
        if phase == 0:
            a.run(0, *refs_a)
        if phase == 2:
            a.run(1, *refs_a)


class _Host:
    def __init__(self, exchange):
        self.ex = exchange
        self.args = [] if exchange is None else exchange.arrays
        self.in_specs = [ANY] * len(self.args)
        self.out_shape = [] if exchange is None else exchange.out_shape
        self.out_specs = [ANY] * len(self.out_shape)
        self.scratch = [] if exchange is None else exchange.scratch

    def split(self, refs, n_in, n_out, n_scratch):
        a, b, s = len(self.args), len(self.out_shape), len(self.scratch)
        own_in, ex_in = refs[:n_in], refs[n_in:n_in + a]
        rest = refs[n_in + a:]
        own_out, ex_out = rest[:n_out], rest[n_out:n_out + b]
        rest = rest[n_out + b:]
        own_scratch, ex_sems = rest[:n_scratch], rest[n_scratch:n_scratch + s]
        return list(own_in) + list(own_out) + list(own_scratch), (ex_in, ex_out, ex_sems)

    def at_steps(self, step, n_steps, ex_refs):
        if self.ex is None:
            return
        for p in range(self.ex.n_phases):
            pl.when(step == int(round(self.ex.at[p] * (n_steps - 1))))(functools.partial(self.ex.run, p, *ex_refs))

    def phase(self, p, ex_refs):
        if self.ex is not None:
            self.ex.run(p, *ex_refs)


def _run_exchanges(exchanges, name):
    hosts = [_Host(ex) for ex in exchanges]
    n_in = [len(h.args) for h in hosts]
    n_out = [len(h.out_shape) for h in hosts]
    n_sc = [len(h.scratch) for h in hosts]

    def body(*refs):
        ins, outs, scr = refs[:sum(n_in)], refs[sum(n_in):sum(n_in) + sum(n_out)], refs[sum(n_in) + sum(n_out):]
        parts = []
        for e in range(len(hosts)):
            parts.append((ins[sum(n_in[:e]):sum(n_in[:e + 1])], outs[sum(n_out[:e]):sum(n_out[:e + 1])],
                          scr[sum(n_sc[:e]):sum(n_sc[:e + 1])]))
        for h, part in zip(hosts, parts):
            h.phase(0, part)
        for h, part in zip(hosts, parts):
            for p in range(1, h.ex.n_phases):
                h.phase(p, part)

    res = pl.pallas_call(
        body, name=name, in_specs=[ANY] * sum(n_in), out_specs=[ANY] * sum(n_out),
        out_shape=[s for h in hosts for s in h.out_shape], scratch_shapes=[s for h in hosts for s in h.scratch],
    )(*[a for h in hosts for a in h.args])
    return [res[sum(n_out[:e]):sum(n_out[:e + 1])] for e in range(len(hosts))]


def _ffn_fwd(x, g_pre, wgu, wd, g_post, target, name, exchange=None):
    t = x.shape[0]
    tm = _token_tile(t)
    n_i = t // tm
    with_loss = target is not None
    host = _Host(exchange)
    n_in, n_out = (6, 6) if with_loss else (5, 4)

    def body(*refs):
        own, ex_refs = host.split(refs, n_in, n_out, 0)
        if with_loss:
            x_ref, gpre_ref, wgu_ref, wd_ref, gpost_ref, tgt_ref, xo_ref, n_ref, df_ref, gu_ref, dgpost_ref, loss_ref = own
            _zero_at_first(pl.program_id(0) == 0, dgpost_ref)
        else:
            x_ref, gpre_ref, wgu_ref, wd_ref, gpost_ref, xo_ref, f_ref, n_ref, gu_ref = own
        host.at_steps(pl.program_id(0), n_i, ex_refs)
        x = x_ref[...]
        n = _rms_fwd(x, gpre_ref[...]).astype(BF16)
        n_ref[...] = n
        f = None
        for j in range(N_CHUNK):
            gate = _dot_nt(n, wgu_ref[0, j])
            up = _dot_nt(n, wgu_ref[1, j])
            gu_ref[0, j] = gate.astype(BF16)
            gu_ref[1, j] = up.astype(BF16)
            part = _dot((gate * _sigmoid(gate) * up).astype(BF16), wd_ref[j])
            f = part if f is None else f + part
        xo = x + 0.5 * _rms_fwd(f, gpost_ref[...])
        if with_loss:
            err = xo - tgt_ref[...]
            d_out = err * (1.0 / D_MODEL)
            xo_ref[...] = d_out
            df, dg = _rms_bwd(f, gpost_ref[...], 0.5 * d_out)
            df_ref[...] = df.astype(BF16)
            dgpost_ref[...] += dg
            part = 0.5 * jnp.sum(jnp.sum(err * err, axis=-1, keepdims=True) * (1.0 / D_MODEL), axis=0, keepdims=True)
            loss_ref[...] = jnp.broadcast_to(part, loss_ref.shape)
        else:
            f_ref[...] = f
            xo_ref[...] = xo

    tok = pl.BlockSpec((tm, D_MODEL), lambda i: (i, 0))
    vec = pl.BlockSpec((1, D_MODEL), lambda i: (0, 0))
    act = pl.BlockSpec((2, N_CHUNK, tm, CHUNK), lambda i: (0, 0, i, 0))
    tok_f32 = jax.ShapeDtypeStruct((t, D_MODEL), F32)
    tok_bf16 = jax.ShapeDtypeStruct((t, D_MODEL), BF16)
    act_shape = jax.ShapeDtypeStruct((2, N_CHUNK, t, CHUNK), BF16)
    in_specs = [tok, vec,
                pl.BlockSpec((2, N_CHUNK, CHUNK, D_MODEL), lambda i: (0, 0, 0, 0), pipeline_mode=pl.Buffered(1)),
                pl.BlockSpec((N_CHUNK, CHUNK, D_MODEL), lambda i: (0, 0, 0), pipeline_mode=pl.Buffered(1)),
                vec]
    args = [x, g_pre, wgu, wd, g_post]
    if with_loss:
        in_specs.append(tok)
        args.append(target)
        out_shape = [tok_f32, tok_bf16, tok_bf16, act_shape, jax.ShapeDtypeStruct((1, D_MODEL), F32),
                     jax.ShapeDtypeStruct((n_i * 8, 128), F32)]
        out_specs = [tok, tok, tok, act, vec, pl.BlockSpec((8, 128), lambda i: (i, 0))]
    else:
        out_shape = [tok_f32, tok_f32, tok_bf16, act_shape]
        out_specs = [tok, tok, tok, act]
    res = pl.pallas_call(
        body, name=name, grid=(n_i,), in_specs=in_specs + host.in_specs, out_specs=out_specs + host.out_specs,
        out_shape=out_shape + host.out_shape, scratch_shapes=host.scratch, compiler_params=_params(1),
    )(*args, *host.args)
    return (*res[:n_out], list(res[n_out:]))


def _ffn_bwd_w(n, df, gu, wd, name, exchange=None):
    t = n.shape[0]
    tm = _ffn_bwd_tile(t)
    n_i = t // tm
    host = _Host(exchange)

    def body(*refs):
        (n_ref, df_ref, gu_ref, wd_ref, dgu_ref, dwgu_ref, dwd_ref), ex_refs = host.split(refs, 4, 3, 0)
        i = pl.program_id(1)
        host.at_steps(pl.program_id(0) * n_i + i, N_CHUNK * n_i, ex_refs)
        _zero_at_first(i == 0, dwgu_ref, dwd_ref)
        nb = n_ref[...]
        dfb = df_ref[...]
        gate = gu_ref[0, 0].astype(F32)
        up = gu_ref[1, 0].astype(F32)
        s = _sigmoid(gate)
        silu = gate * s
        a = (silu * up).astype(BF16)
        da = _dot_nt(dfb, wd_ref[0])
        dup = (da * silu).astype(BF16)
        dgate = (da * up * (s * (1.0 + gate * (1.0 - s)))).astype(BF16)
        dgu_ref[0, 0] = dgate
        dgu_ref[1, 0] = dup
        dwgu_ref[0, 0] += _dot_tn(nb, dgate)
        dwgu_ref[1, 0] += _dot_tn(nb, dup)
        dwd_ref[0] += _dot_tn(a, dfb)

    tok = pl.BlockSpec((tm, D_MODEL), lambda j, i: (i, 0))
    act = pl.BlockSpec((2, 1, tm, CHUNK), lambda j, i: (0, j, i, 0))
    wgu_spec = pl.BlockSpec((2, 1, D_MODEL, CHUNK), lambda j, i: (0, j, 0, 0))
    wd_spec = pl.BlockSpec((1, CHUNK, D_MODEL), lambda j, i: (j, 0, 0))
    res = pl.pallas_call(
        body, name=name, grid=(N_CHUNK, n_i),
        in_specs=[tok, tok, act, wd_spec] + host.in_specs,
        out_specs=[act, wgu_spec, wd_spec] + host.out_specs,
        out_shape=[jax.ShapeDtypeStruct((2, N_CHUNK, t, CHUNK), BF16),
                   jax.ShapeDtypeStruct((2, N_CHUNK, D_MODEL, CHUNK), F32),
                   jax.ShapeDtypeStruct((N_CHUNK, CHUNK, D_MODEL), F32)] + host.out_shape,
        scratch_shapes=host.scratch, compiler_params=_params(2),
    )(n, df, gu, wd, *host.args)
    return (*res[:3], list(res[3:]))


def _ffn_bwd_x(dgu, wgu, x, g_pre, d_out, name, exchange=None):
    t = x.shape[0]
    tm = _token_tile(t)
    n_i = t // tm
    host = _Host(exchange)

    def body(*refs):
        (dgu_ref, wgu_ref, x_ref, gpre_ref, do_ref, dx_ref, dgpre_ref), ex_refs = host.split(refs, 5, 2, 0)
        i = pl.program_id(0)
        host.at_steps(i, n_i, ex_refs)
        _zero_at_first(i == 0, dgpre_ref)
        dn = _dot(dgu_ref[0, 0], wgu_ref[0, 0]) + _dot(dgu_ref[1, 0], wgu_ref[1, 0])
        for j in range(1, N_CHUNK):
            dn = dn + _dot(dgu_ref[0, j], wgu_ref[0, j]) + _dot(dgu_ref[1, j], wgu_ref[1, j])
        dx, dg = _rms_bwd(x_ref[...], gpre_ref[...], dn)
        dx_ref[...] = do_ref[...] + dx
        dgpre_ref[...] += dg

    tok = pl.BlockSpec((tm, D_MODEL), lambda i: (i, 0))
    vec = pl.BlockSpec((1, D_MODEL), lambda i: (0, 0))
    res = pl.pallas_call(
        body, name=name, grid=(n_i,),
        in_specs=[pl.BlockSpec((2, N_CHUNK, tm, CHUNK), lambda i: (0, 0, i, 0)),
                  pl.BlockSpec((2, N_CHUNK, CHUNK, D_MODEL), lambda i: (0, 0, 0, 0), pipeline_mode=pl.Buffered(1)),
                  tok, vec, tok] + host.in_specs,
        out_specs=[tok, vec] + host.out_specs,
        out_shape=[jax.ShapeDtypeStruct((t, D_MODEL), F32), jax.ShapeDtypeStruct((1, D_MODEL), F32)] + host.out_shape,
        scratch_shapes=host.scratch, compiler_params=_params(1),
    )(dgu, wgu, x, g_pre, d_out, *host.args)
    return (*res[:2], list(res[2:]))


def _mix_in_fwd(x1, g, w_in):
    t = x1.shape[0]
    tm = _token_tile(t)

    def body(x_ref, g_ref, w_ref, xl_ref, gl_ref, q_ref, kv_ref):
        n = _rms_fwd(x_ref[...], g_ref[...]).astype(BF16)
        proj = _dot_nt(n, w_ref[...])
        xl_ref[...] = proj[:, 0:512]
        gl_ref[...] = proj[:, 512:1024]
        q_ref[...] = proj[:, 1024:1536].astype(BF16)
        kv_ref[...] = proj[:, 1536:2048].astype(BF16)

    tok = pl.BlockSpec((tm, D_MODEL), lambda i: (i, 0))
    half = pl.BlockSpec((tm, 512), lambda i: (i, 0))
    return pl.pallas_call(
        body, name="mix_in_fwd", grid=(t // tm,),
        in_specs=[tok, pl.BlockSpec((1, D_MODEL), lambda i: (0, 0)), pl.BlockSpec((D_IN_DUP, D_MODEL), lambda i: (0, 0))],
        out_specs=[half, half, half, half],
        out_shape=[jax.ShapeDtypeStruct((t, 512), F32), jax.ShapeDtypeStruct((t, 512), F32),
                   jax.ShapeDtypeStruct((t, 512), BF16), jax.ShapeDtypeStruct((t, 512), BF16)],
        compiler_params=_params(1),
    )(x1, g, w_in)


def _shift_down(x, before, s):
    if s == 0:
        return x
    rolled = pltpu.roll(x, s, 0)
    ext = jnp.concatenate([before, x[0:8]], axis=0)
    first8 = pltpu.roll(ext, s, 0)[8:16]
    return jnp.concatenate([first8, rolled[8:]], axis=0)


def _shift_up(x, after, s):
    if s == 0:
        return x
    rows = x.shape[0]
    rolled = pltpu.roll(x, rows - s, 0)
    ext = jnp.concatenate([x[rows - 8:rows], after], axis=0)
    last8 = pltpu.roll(ext, 16 - s, 0)[0:8]
    return jnp.concatenate([rolled[:rows - 8], last8], axis=0)


def _log_sigmoid(x):
    e = jnp.exp(-jnp.abs(x))
    log1p_e = jnp.where(e < 0.01, e * (1.0 - e * (0.5 - e * (1.0 / 3.0))), jnp.log(1.0 + e))
    return jnp.minimum(x, 0.0) - log1p_e


def _lru_gates(xc, p_ref, wrg, wig):
    xcb = xc.astype(BF16)
    r = _sigmoid(_dot(xcb, wrg) + p_ref[1:2, :])
    ig = _sigmoid(_dot(xcb, wig) + p_ref[2:3, :])
    ls = _log_sigmoid(p_ref[3:4, :])
    log_a = LRU_C * r * ls
    a = jnp.exp(log_a)
    mult = jnp.sqrt(-jnp.tanh(log_a) * (a * a + 1.0))
    return xcb, r, ig, ls, a, mult


def _conv_taps(x, before, p_ref):
    xc = x * p_ref[7:8, :]
    for s in (1, 2, 3):
        xc = xc + _shift_down(x, before, s) * p_ref[7 - s:8 - s, :]
    return xc + p_ref[0:1, :]


def _lru_block_rows(t):
    return 512 if t >= 1024 else t // 2


def _lru_fwd(xl, p, wrg2, wig2):
    t = xl.shape[0]
    tb = _lru_block_rows(t)

    def body(xl_ref, p_ref, wrg_ref, wig_ref, h_ref, x_tail, h_carry):
        tt = pl.program_id(1)

        @pl.when(tt == 0)
        def _():
            x_tail[...] = jnp.zeros_like(x_tail)
            h_carry[...] = jnp.zeros_like(h_carry)

        x = xl_ref[...]
        xc = _conv_taps(x, x_tail[...], p_ref)
        x_tail[...] = x[tb - 8:tb]
        _, r, ig, ls, a, mult = _lru_gates(xc, p_ref, wrg_ref[0], wig_ref[0])
        u = mult * ig * xc
        row = lax.broadcasted_iota(jnp.int32, (tb, LRU_GROUP), 0)
        s = 1
        while s < tb:
            keep = row >= s
            u = jnp.where(keep, a * pltpu.roll(u, s, 0) + u, u)
            a = jnp.where(keep, a * pltpu.roll(a, s, 0), a)
            s *= 2
        h = u + a * h_carry[0:1, :]
        h_ref[...] = h
        h_carry[...] = jnp.broadcast_to(h[tb - 1:tb], h_carry.shape)

    blk = pl.BlockSpec((tb, LRU_GROUP), lambda g, tt: (tt, g))
    par = pl.BlockSpec((8, LRU_GROUP), lambda g, tt: (0, g))
    wsp = pl.BlockSpec((1, LRU_GROUP, LRU_GROUP), lambda g, tt: (g, 0, 0))
    return pl.pallas_call(
        body, name="lru_fwd", grid=(N_LRU_GROUP, t // tb), in_specs=[blk, par, wsp, wsp], out_specs=blk,
        out_shape=jax.ShapeDtypeStruct((t, D_LRU), F32),
        scratch_shapes=[pltpu.VMEM((8, LRU_GROUP), F32), pltpu.VMEM((8, LRU_GROUP), F32)],
        compiler_params=_params(2),
    )(xl, p, wrg2, wig2)


def _lru_bwd(dy, h, xl, gl, p, wrg2, wig2):
    t = xl.shape[0]
    tb = _lru_block_rows(t)
    n_tb = t // tb
    tb8 = tb // 8

    def body(dy_ref, h_ref, hprev_ref, xl_ref, xprev_ref, gl_ref, p_ref, wrg_ref, wig_ref,
             dxl_ref, dgl_ref, dp_ref, dwrg_ref, dwig_ref, g_carry, a_carry, dxc_head):
        step = pl.program_id(1)
        tt = n_tb - 1 - step
        first = step == 0

        _zero_at_first(first, g_carry, a_carry, dxc_head, dp_ref, dwrg_ref, dwig_ref)

        has_prev = (tt > 0).astype(F32)
        x = xl_ref[...]
        x_before = xprev_ref[...] * has_prev
        xs = [_shift_down(x, x_before, s) for s in range(4)]
        xc = xs[0] * p_ref[7:8, :] + xs[1] * p_ref[6:7, :] + xs[2] * p_ref[5:6, :] + xs[3] * p_ref[4:5, :] + p_ref[0:1, :]
        wrg = wrg_ref[0]
        wig = wig_ref[0]
        xcb, r, ig, ls, a, mult = _lru_gates(xc, p_ref, wrg, wig)

        hh = h_ref[...]
        h_m1 = _shift_down(hh, hprev_ref[...] * has_prev, 1)
        ge, dge = _gelu(gl_ref[...])
        dy = dy_ref[...]
        dgl_ref[...] = dy * hh * dge
        dh = dy * ge

        b = _shift_up(a, a_carry[...], 1)
        row = lax.broadcasted_iota(jnp.int32, (tb, LRU_GROUP), 0)
        g = dh
        s = 1
        while s < tb:
            keep = row < tb - s
            g = jnp.where(keep, b * pltpu.roll(g, tb - s, 0) + g, g)
            b = jnp.where(keep, b * pltpu.roll(b, tb - s, 0), b)
            s *= 2
        g = g + b * g_carry[0:1, :]
        g_carry[...] = jnp.broadcast_to(g[0:1], g_carry.shape)
        a_carry[...] = jnp.broadcast_to(a[0:1], a_carry.shape)

        da = g * h_m1
        dmult = g * ig * xc
        dig = g * mult * xc
        dxc = g * mult * ig
        dlog_a = da * a - dmult * (a * a) / mult
        dr = dlog_a * (LRU_C * ls)
        dls = jnp.sum(dlog_a * (LRU_C * r), axis=0, keepdims=True)
        dlam = dls * _sigmoid(-p_ref[3:4, :])
        dpre_r = dr * r * (1.0 - r)
        dpre_i = dig * ig * (1.0 - ig)
        dprb = dpre_r.astype(BF16)
        dpib = dpre_i.astype(BF16)
        dxc = dxc + _dot_nt(dprb, wrg) + _dot_nt(dpib, wig)
        dwrg_ref[0] += _dot_tn(xcb, dprb)
        dwig_ref[0] += _dot_tn(xcb, dpib)

        after = dxc_head[...]
        dxl = dxc * p_ref[7:8, :]
        for s in (1, 2, 3):
            dxl = dxl + _shift_up(dxc, after, s) * p_ref[7 - s:8 - s, :]
        dxl_ref[...] = dxl
        dxc_head[...] = dxc[0:8]

        rows = [jnp.sum(dxc, axis=0, keepdims=True), jnp.sum(dpre_r, axis=0, keepdims=True),
                jnp.sum(dpre_i, axis=0, keepdims=True), dlam]
        rows += [jnp.sum(dxc * xs[3 - k], axis=0, keepdims=True) for k in range(4)]
        dp_ref[...] += jnp.concatenate(rows, axis=0)

    blk = pl.BlockSpec((tb, LRU_GROUP), lambda g, s: (n_tb - 1 - s, g))
    prev8 = pl.BlockSpec((8, LRU_GROUP), lambda g, s: (jnp.maximum((n_tb - 1 - s) * tb8 - 1, 0), g))
    par = pl.BlockSpec((8, LRU_GROUP), lambda g, s: (0, g))
    wsp = pl.BlockSpec((1, LRU_GROUP, LRU_GROUP), lambda g, s: (g, 0, 0))
    return pl.pallas_call(
        body, name="lru_bwd", grid=(N_LRU_GROUP, n_tb),
        in_specs=[blk, blk, prev8, blk, prev8, blk, par, wsp, wsp], out_specs=[blk, blk, par, wsp, wsp],
        out_shape=[jax.ShapeDtypeStruct((t, D_LRU), F32), jax.ShapeDtypeStruct((t, D_LRU), F32),
                   jax.ShapeDtypeStruct((8, D_LRU), F32),
                   jax.ShapeDtypeStruct((N_LRU_GROUP, LRU_GROUP, LRU_GROUP), F32),
                   jax.ShapeDtypeStruct((N_LRU_GROUP, LRU_GROUP, LRU_GROUP), F32)],
        scratch_shapes=[pltpu.VMEM((8, LRU_GROUP), F32)] * 3,
        compiler_params=_params(2),
    )(dy, h, h, xl, xl, gl, p, wrg2, wig2)


def _attn_bias(first_block):
    qi = jnp.bitwise_and(lax.broadcasted_iota(jnp.int32, (4 * BLOCK_Q, 2 * BLOCK_Q), 0), BLOCK_Q - 1)
    kj = lax.broadcasted_iota(jnp.int32, (4 * BLOCK_Q, 2 * BLOCK_Q), 1)
    rel = qi + BLOCK_Q - kj
    mask = (rel >= 0) & (rel < BLOCK_Q)
    if first_block:
        mask = mask & (kj >= BLOCK_Q)
    return jnp.where(mask, 0.0, MASK_VALUE)


def _sink_column(sinks):
    hrow = lax.broadcasted_iota(jnp.int32, (4 * BLOCK_Q, 1), 0)
    return jnp.where(hrow < BLOCK_Q, sinks[0],
                     jnp.where(hrow < 2 * BLOCK_Q, sinks[1], jnp.where(hrow < 3 * BLOCK_Q, sinks[2], sinks[3])))


def _attn_scores(qv, kvv, n, bias, sk, lo):
    r0 = pl.multiple_of(n * BLOCK_Q, BLOCK_Q)
    rp = pl.multiple_of(jnp.maximum(n - 1, 0) * BLOCK_Q, BLOCK_Q)
    kvb = jnp.concatenate([kvv[pl.ds(rp, BLOCK_Q), :], kvv[pl.ds(r0, BLOCK_Q), :]], axis=0)
    k2 = kvb[:, 0:128]
    v2 = kvb[:, 128:256]
    qs = _stack_heads(qv[pl.ds(r0, BLOCK_Q), :], lo)
    s = _dot_nt(qs, k2) * ATTN_SCALE + bias
    m = jnp.maximum(jnp.max(s, axis=-1, keepdims=True), sk)
    e = jnp.exp(s - m)
    es = jnp.exp(sk - m)
    inv = 1.0 / (jnp.sum(e, axis=-1, keepdims=True) + es)
    return r0, rp, qs, k2, v2, e * inv, es * inv


def _stack_heads(pair2, lo):
    p0 = pair2[:, 0:128]
    p1 = pair2[:, 128:256]
    z = jnp.zeros_like(p0)
    return jnp.concatenate([jnp.where(lo, p0, z), jnp.where(lo, z, p0), jnp.where(lo, p1, z), jnp.where(lo, z, p1)], axis=0)


def _unstack_heads(st, lo):
    b = BLOCK_Q
    return jnp.concatenate([jnp.where(lo, st[0:b], st[b:2 * b]), jnp.where(lo, st[2 * b:3 * b], st[3 * b:4 * b])], axis=1)


def _attn_fwd(q, kv, sinks):
    t = q.shape[0]
    n_blk = t // BLOCK_Q

    def body(q_hbm, kv_hbm, s_ref, o_hbm, q2, kv2, o2, bias0, bias, sem):
        lo = lax.broadcasted_iota(jnp.int32, (BLOCK_Q, 128), 1) < HEAD_DIM
        cols = [pl.ds(256 * g, 256) for g in range(2)]
        loads = [[pltpu.make_async_copy(q_hbm.at[:, cols[g]], q2.at[g], sem.at[3 * g]),
                  pltpu.make_async_copy(kv_hbm.at[:, cols[g]], kv2.at[g], sem.at[3 * g + 1])] for g in range(2)]
        stores = [pltpu.make_async_copy(o2.at[g], o_hbm.at[:, cols[g]], sem.at[3 * g + 2]) for g in range(2)]
        for cp in loads[0] + loads[1]:
            cp.start()
        bias0[...] = _attn_bias(True)
        bias[...] = _attn_bias(False)
        for g in range(2):
            for cp in loads[g]:
                cp.wait()
            qv, kvv, ov = q2.at[g], kv2.at[g], o2.at[g]
            sk = _sink_column([s_ref[0, 4 * g + i] for i in range(4)])

            def block(n, bias_ref):
                r0, _, _, _, v2, prob, _ = _attn_scores(qv, kvv, n, bias_ref[...], sk, lo)
                ov[pl.ds(r0, BLOCK_Q), :] = _unstack_heads(_dot(prob.astype(BF16), v2), lo)

            block(0, bias0)

            def later(n, carry):
                block(n, bias)
                return carry

            lax.fori_loop(1, n_blk, later, 0, unroll=2)
            stores[g].start()
        for cp in stores:
            cp.wait()

    return pl.pallas_call(
        body, name="attn_fwd", in_specs=[ANY, ANY, SMEM], out_specs=ANY,
        out_shape=jax.ShapeDtypeStruct((t, D_ATTN), F32),
        scratch_shapes=[pltpu.VMEM((2, t, 256), BF16), pltpu.VMEM((2, t, 256), BF16), pltpu.VMEM((2, t, 256), F32),
                        pltpu.VMEM((4 * BLOCK_Q, 2 * BLOCK_Q), F32), pltpu.VMEM((4 * BLOCK_Q, 2 * BLOCK_Q), F32),
                        pltpu.SemaphoreType.DMA((6,))],
        compiler_params=_params(),
    )(q, kv, sinks)


def _attn_bwd(q, kv, do, sinks, exchange=None):
    t = q.shape[0]
    n_blk = t // BLOCK_Q
    host = _Host(exchange)

    def body(*refs):
        own, ex_refs = host.split(refs, 4, 3, 9)
        q_hbm, kv_hbm, do_hbm, s_ref, dq_hbm, dkv_hbm, dsink_ref, q2, kv2, do2, dqv, dkvv, ds_acc, bias0, bias, sem = own
        host.phase(0, ex_refs)
        lo = lax.broadcasted_iota(jnp.int32, (BLOCK_Q, 128), 1) < HEAD_DIM
        loads = [[pltpu.make_async_copy(src.at[:, pl.ds(256 * g, 256)], dst.at[g], sem.at[3 * g + i])
                  for i, (src, dst) in enumerate(((q_hbm, q2), (kv_hbm, kv2), (do_hbm, do2)))] for g in range(2)]
        for cp in loads[0] + loads[1]:
            cp.start()
        bias0[...] = _attn_bias(True)
        bias[...] = _attn_bias(False)
        for g in range(2):
            cols = pl.ds(256 * g, 256)
            for cp in loads[g]:
                cp.wait()
            qv, kvv, dov = q2.at[g], kv2.at[g], do2.at[g]
            sk = _sink_column([s_ref[0, 4 * g + i] for i in range(4)])
            ds_acc[...] = jnp.zeros_like(ds_acc)

            def block(n, bias_ref, has_prev):
                r0, rp, qs, k2, v2, prob, psink = _attn_scores(qv, kvv, n, bias_ref[...], sk, lo)
                pb = prob.astype(BF16)
                dos = _stack_heads(dov[pl.ds(r0, BLOCK_Q), :], lo)
                dp = _dot_nt(dos, v2)
                dsum = jnp.sum(prob * dp, axis=-1, keepdims=True)
                dsb = (prob * (dp - dsum) * ATTN_SCALE).astype(BF16)
                ds_acc[...] -= psink * dsum
                dqv[pl.ds(r0, BLOCK_Q), :] = _unstack_heads(_dot(dsb, k2), lo).astype(BF16)
                dk2 = _dot_tn(dsb, qs)
                dv2 = _dot_tn(pb, dos)
                dkvv[pl.ds(r0, BLOCK_Q), :] = jnp.concatenate([dk2[BLOCK_Q:], dv2[BLOCK_Q:]], axis=1)
                if has_prev:
                    dkvv[pl.ds(rp, BLOCK_Q), :] += jnp.concatenate([dk2[:BLOCK_Q], dv2[:BLOCK_Q]], axis=1)

            block(0, bias0, False)

            def later(n, carry):
                block(n, bias, True)
                return carry

            lax.fori_loop(1, n_blk, later, 0, unroll=2)
            for i in range(4):
                tot = jnp.sum(ds_acc[BLOCK_Q * i:BLOCK_Q * (i + 1), :], axis=0, keepdims=True)
                dsink_ref[4 * g + i:4 * g + i + 1, :] = jnp.broadcast_to(tot, (1, 128))
            stores = [pltpu.make_async_copy(dqv, dq_hbm.at[:, cols], sem.at[6]),
                      pltpu.make_async_copy(dkvv, dkv_hbm.at[:, cols], sem.at[7])]
            for cp in stores:
                cp.start()
            for cp in stores:
                cp.wait()
        if exchange is not None:
            for p in range(1, exchange.n_phases):
                host.phase(p, ex_refs)

    res = pl.pallas_call(
        body, name="attn_bwd", in_specs=[ANY, ANY, ANY, SMEM] + host.in_specs,
        out_specs=[ANY, ANY, pl.BlockSpec(memory_space=pltpu.VMEM)] + host.out_specs,
        out_shape=[jax.ShapeDtypeStruct((t, D_ATTN), BF16), jax.ShapeDtypeStruct((t, 512), F32),
                   jax.ShapeDtypeStruct((8, 128), F32)] + host.out_shape,
        scratch_shapes=[pltpu.VMEM((2, t, 256), BF16), pltpu.VMEM((2, t, 256), BF16), pltpu.VMEM((2, t, 256), BF16),
                        pltpu.VMEM((t, 256), BF16), pltpu.VMEM((t, 256), F32), pltpu.VMEM((4 * BLOCK_Q, 1), F32),
                        pltpu.VMEM((4 * BLOCK_Q, 2 * BLOCK_Q), F32), pltpu.VMEM((4 * BLOCK_Q, 2 * BLOCK_Q), F32),
                        pltpu.SemaphoreType.DMA((8,))] + host.scratch,
        compiler_params=_params(),
    )(q, kv, do, sinks, *host.args)
    return (*res[:3], list(res[3:]))


def _mix_out_fwd(x1, h, gl, o, g_lru, g_attn, g_post, w_o):
    t = x1.shape[0]
    tm = _token_tile(t)

    def body(x_ref, h_ref, gl_ref, o_ref, g1_ref, g2_ref, gp_ref, w_ref, x2_ref, m_ref):
        y = h_ref[...] * _gelu(gl_ref[...])[0]
        yn1 = _rms_fwd(y, g1_ref[...]).astype(BF16)
        yn2 = _rms_fwd(o_ref[...], g2_ref[...]).astype(BF16)
        m = _dot(yn1, w_ref[0:512, :]) + _dot(yn2, w_ref[512:1024, :])
        m_ref[...] = m
        x2_ref[...] = x_ref[...] + _rms_fwd(m, gp_ref[...])

    tok = pl.BlockSpec((tm, D_MODEL), lambda i: (i, 0))
    half = pl.BlockSpec((tm, 512), lambda i: (i, 0))
    vec = pl.BlockSpec((1, D_MODEL), lambda i: (0, 0))
    hvec = pl.BlockSpec((1, 512), lambda i: (0, 0))
    return pl.pallas_call(
        body, name="mix_out_fwd", grid=(t // tm,),
        in_specs=[tok, half, half, half, hvec, hvec, vec, pl.BlockSpec((D_MODEL, D_MODEL), lambda i: (0, 0))],
        out_specs=[tok, tok],
        out_shape=[jax.ShapeDtypeStruct((t, D_MODEL), F32), jax.ShapeDtypeStruct((t, D_MODEL), F32)],
        compiler_params=_params(1),
    )(x1, h, gl, o, g_lru, g_attn, g_post, w_o)


def _mix_out_bwd(dx2, m, h, gl, o, g_lru, g_attn, g_post, w_o):
    t = dx2.shape[0]
    tm = _token_tile(t)

    def body(dx_ref, m_ref, h_ref, gl_ref, o_ref, g1_ref, g2_ref, gp_ref, w_ref,
             dy_ref, do_ref, dw_ref, dgp_ref, dg1_ref, dg2_ref):
        _zero_at_first(pl.program_id(0) == 0, dw_ref, dgp_ref, dg1_ref, dg2_ref)
        dm, dgp = _rms_bwd(m_ref[...], gp_ref[...], dx_ref[...])
        dmb = dm.astype(BF16)
        y = h_ref[...] * _gelu(gl_ref[...])[0]
        o = o_ref[...]
        yn1 = _rms_fwd(y, g1_ref[...]).astype(BF16)
        yn2 = _rms_fwd(o, g2_ref[...]).astype(BF16)
        dw_ref[0:512, :] += _dot_tn(yn1, dmb)
        dw_ref[512:1024, :] += _dot_tn(yn2, dmb)
        dy, dg1 = _rms_bwd(y, g1_ref[...], _dot_nt(dmb, w_ref[0:512, :]))
        do, dg2 = _rms_bwd(o, g2_ref[...], _dot_nt(dmb, w_ref[512:1024, :]))
        dy_ref[...] = dy
        do_ref[...] = do.astype(BF16)
        dgp_ref[...] += dgp
        dg1_ref[...] += dg1
        dg2_ref[...] += dg2

    tok = pl.BlockSpec((tm, D_MODEL), lambda i: (i, 0))
    half = pl.BlockSpec((tm, 512), lambda i: (i, 0))
    vec = pl.BlockSpec((1, D_MODEL), lambda i: (0, 0))
    hvec = pl.BlockSpec((1, 512), lambda i: (0, 0))
    mat = pl.BlockSpec((D_MODEL, D_MODEL), lambda i: (0, 0))
    return pl.pallas_call(
        body, name="mix_out_bwd", grid=(t // tm,),
        in_specs=[tok, tok, half, half, half, hvec, hvec, vec, mat],
        out_specs=[half, half, mat, vec, hvec, hvec],
        out_shape=[jax.ShapeDtypeStruct((t, 512), F32), jax.ShapeDtypeStruct((t, 512), BF16),
                   jax.ShapeDtypeStruct((D_MODEL, D_MODEL), F32), jax.ShapeDtypeStruct((1, D_MODEL), F32),
                   jax.ShapeDtypeStruct((1, 512), F32), jax.ShapeDtypeStruct((1, 512), F32)],
        compiler_params=_params(1),
    )(dx2, m, h, gl, o, g_lru, g_attn, g_post, w_o)


def _mix_in_bwd(dx2, x1, g, dxl, dgl, dq, dkv, w_in, f1, g_post1):
    t = x1.shape[0]
    tm = _token_tile(t)

    def body(dx2_ref, x_ref, g_ref, dxl_ref, dgl_ref, dq_ref, dkv_ref, w_ref, f1_ref, gp1_ref,
             dx1_ref, dw_ref, dg_ref, df1_ref, dgp1_ref):
        _zero_at_first(pl.program_id(0) == 0, dw_ref, dg_ref, dgp1_ref)
        x = x_ref[...]
        nb = _rms_fwd(x, g_ref[...]).astype(BF16)
        lo = lax.broadcasted_iota(jnp.int32, (tm, 128), 1) < HEAD_DIM
        dkv = dkv_ref[...]
        folded = []
        for k in range(4):
            seg = dkv[:, 128 * k:128 * (k + 1)]
            folded.append(jnp.where(lo, seg + pltpu.roll(seg, HEAD_DIM, 1), 0.0).astype(BF16))
        dproj = jnp.concatenate([dxl_ref[...].astype(BF16), dgl_ref[...].astype(BF16), dq_ref[...]] + folded, axis=1)
        dw_ref[...] += _dot_tn(nb, dproj)
        dx, dg = _rms_bwd(x, g_ref[...], _dot(dproj, w_ref[...]))
        dx1 = dx2_ref[...] + dx
        dx1_ref[...] = dx1
        dg_ref[...] += dg
        df1, dgp1 = _rms_bwd(f1_ref[...], gp1_ref[...], 0.5 * dx1)
        df1_ref[...] = df1.astype(BF16)
        dgp1_ref[...] += dgp1

    tok = pl.BlockSpec((tm, D_MODEL), lambda i: (i, 0))
    half = pl.BlockSpec((tm, 512), lambda i: (i, 0))
    vec = pl.BlockSpec((1, D_MODEL), lambda i: (0, 0))
    mat = pl.BlockSpec((D_IN_DUP, D_MODEL), lambda i: (0, 0))
    dmat = pl.BlockSpec((D_MODEL, D_IN_DUP), lambda i: (0, 0))
    return pl.pallas_call(
        body, name="mix_in_bwd", grid=(t // tm,),
        in_specs=[tok, tok, vec, half, half, half, half, mat, tok, vec], out_specs=[tok, dmat, vec, tok, vec],
        out_shape=[jax.ShapeDtypeStruct((t, D_MODEL), F32), jax.ShapeDtypeStruct((D_MODEL, D_IN_DUP), F32),
                   jax.ShapeDtypeStruct((1, D_MODEL), F32), jax.ShapeDtypeStruct((t, D_MODEL), BF16),
                   jax.ShapeDtypeStruct((1, D_MODEL), F32)],
        compiler_params=_params(1),
    )(dx2, x1, g, dxl, dgl, dq, dkv, w_in, f1, g_post1)


def _row_tile(rows):
    return rows if rows <= 512 else rows // 2


def _chip_sum(grad, from_sibling, other, name):
    _, rows, cols = grad.shape
    tr = _row_tile(rows)

    def body(other_ref, g_ref, s_ref, out_ref):
        out_ref[0] = (g_ref[0, 0] + s_ref[0]).astype(BF16)

    grid_spec = pltpu.PrefetchScalarGridSpec(
        num_scalar_prefetch=1, grid=(3, rows // tr),
        in_specs=[pl.BlockSpec((1, 1, tr, cols), lambda j, i, other: (other[j], other[3], i, 0)),
                  pl.BlockSpec((1, tr, cols), lambda j, i, other: (other[j], i, 0))],
        out_specs=pl.BlockSpec((1, tr, cols), lambda j, i, other: (j, i, 0)))
    return pl.pallas_call(
        body, name=name, grid_spec=grid_spec, out_shape=jax.ShapeDtypeStruct((3, rows, cols), BF16),
        compiler_params=_params(2),
    )(other, grad.reshape(4, 2, rows, cols), from_sibling)


def _adamw(w, g, m, v):
    m = ADAM_B1 * m + (1.0 - ADAM_B1) * g
    v = ADAM_B2 * v + (1.0 - ADAM_B2) * (g * g)
    m_hat = m / (1.0 - ADAM_B1 ** ADAM_STEP)
    v_hat = v / (1.0 - ADAM_B2 ** ADAM_STEP)
    delta = -ADAM_LR * (m_hat / (jnp.sqrt(v_hat) + ADAM_EPS) + ADAM_WD * w)
    return delta, m, v


def _shard_update(grad, from_sibling, from_chips, w, m, v, place, name, transposed):
    _, rows, cols = grad.shape
    tr = _row_tile(rows)

    def total(g_ref, s_ref, c_ref):
        g = g_ref[0, 0] + s_ref[0]
        g = g + c_ref[0].astype(F32)
        g = g + c_ref[1].astype(F32)
        return g + c_ref[2].astype(F32)

    part_specs = [pl.BlockSpec((1, 1, tr, cols), lambda i, place: (place[0], place[1], i, 0)),
                  pl.BlockSpec((1, tr, cols), lambda i, place: (place[0], i, 0)),
                  pl.BlockSpec((3, tr, cols), lambda i, place: (0, i, 0))]
    flat = pl.BlockSpec((tr, cols), lambda i, place: (i, 0))
    parts = (place, grad.reshape(4, 2, rows, cols), from_sibling, from_chips)
    if not transposed:
        def body(place_ref, g_ref, s_ref, c_ref, w_ref, m_ref, v_ref, go_ref, d_ref, mo_ref, vo_ref):
            g = total(g_ref, s_ref, c_ref)
            go_ref[...] = g
            d_ref[...], mo_ref[...], vo_ref[...] = _adamw(w_ref[...], g, m_ref[...], v_ref[...])

        grid_spec = pltpu.PrefetchScalarGridSpec(num_scalar_prefetch=1, grid=(rows // tr,),
                                                 in_specs=part_specs + [flat, flat, flat], out_specs=[flat] * 4)
        return pl.pallas_call(body, name=name, grid_spec=grid_spec, out_shape=[jax.ShapeDtypeStruct((rows, cols), F32)] * 4,
                              compiler_params=_params(1))(*parts, w, m, v)

    def sum_body(place_ref, g_ref, s_ref, c_ref, go_ref):
        go_ref[...] = total(g_ref, s_ref, c_ref)

    grid_spec = pltpu.PrefetchScalarGridSpec(num_scalar_prefetch=1, grid=(rows // tr,), in_specs=part_specs, out_specs=flat)
    g = pl.pallas_call(sum_body, name=name + "_sum", grid_spec=grid_spec, out_shape=jax.ShapeDtypeStruct((rows, cols), F32),
                       compiler_params=_params(1))(*parts)
    gt = jnp.transpose(g, (1, 0))
    tc = _row_tile(cols)

    def adam_body(g_ref, w_ref, m_ref, v_ref, d_ref, mo_ref, vo_ref):
        d_ref[...], mo_ref[...], vo_ref[...] = _adamw(w_ref[...], g_ref[...], m_ref[...], v_ref[...])

    blk = pl.BlockSpec((tc, rows), lambda i: (i, 0))
    res = pl.pallas_call(adam_body, name=name + "_adam", grid=(cols // tc,), in_specs=[blk] * 4, out_specs=[blk] * 3,
                         out_shape=[jax.ShapeDtypeStruct((cols, rows), F32)] * 3, compiler_params=_params(1))(gt, w, m, v)
    return (gt, *res)


GAINS = ("ffn1_pre_g", "ffn1_post_g", "mix_pre_g", "mix_post_g", "ffn2_pre_g", "ffn2_post_g")
HALVES = ("conv_b", "b_rg", "b_ig", "lru_lambda", "g_lru_out", "g_attn_out")
GATES = ("w_rg", "w_ig")
SMALL = GAINS + HALVES + GATES + ("sinks", "conv_w")


def _small_update(gathered, w, m, v):
    n_small = len(SMALL)

    def body(*refs):
        ga_ref, gb_ref, gc_ref, gd_ref, g0_ref = refs[:5]
        wmv = refs[5:5 + 3 * n_small]
        outs = refs[5 + 3 * n_small:5 + 7 * n_small]
        loss_ref = refs[5 + 7 * n_small]

        def total(ref):
            s = ref[0]
            for d in range(1, N_DEV):
                s = s + ref[d]
            return s

        sa, sb, sc, sd = total(ga_ref), total(gb_ref), total(gc_ref), total(gd_ref)
        grads = {}
        for i, k in enumerate(GAINS):
            grads[k] = sa[i:i + 1]
        grads[GAINS[0]] = total(g0_ref)
        for i, k in enumerate(HALVES):
            grads[k] = sb[i:i + 1]
        grads["w_rg"], grads["w_ig"] = sc[0:512], sc[512:1024]
        grads["sinks"] = sd[4:5, 0:8]
        grads["conv_w"] = sd[0:4]
        for i, k in enumerate(SMALL):
            g = grads[k]
            outs[4 * i][...] = g
            outs[4 * i + 1][...], outs[4 * i + 2][...], outs[4 * i + 3][...] = _adamw(
                wmv[3 * i][...], g, wmv[3 * i + 1][...], wmv[3 * i + 2][...])
        loss_ref[...] = jnp.broadcast_to(sd[5:6, 0:128], loss_ref.shape)

    operands = list(gathered)
    out_shape = []
    for k in SMALL:
        operands += [w[k], m[k], v[k]]
        out_shape += [jax.ShapeDtypeStruct(w[k].shape, F32)] * 4
    out_shape.append(jax.ShapeDtypeStruct((8, 128), F32))
    res = pl.pallas_call(body, name="small_update", out_shape=out_shape, compiler_params=_params())(*operands)
    parts = [{k: res[4 * i + j] for i, k in enumerate(SMALL)} for j in range(4)]
    return (*parts, res[-1])


def _dup_in_rows(wt):
    k0, k1, v0, v1 = wt[1536:1600], wt[1600:1664], wt[1664:1728], wt[1728:1792]
    return jnp.concatenate([wt[:1536], k0, k0, v0, v0, k1, k1, v1, v1], axis=0)


def _undup_in_columns(dw):
    return jnp.concatenate([dw[:, :1536], dw[:, 1536:1600], dw[:, 1792:1856], dw[:, 1664:1728], dw[:, 1920:1984]], axis=1)


def _pair_block_diag(w):
    w = w.reshape(N_LRU_GROUP, 2, 64, 64)
    z = jnp.zeros((N_LRU_GROUP, 64, 64), w.dtype)
    top = jnp.concatenate([w[:, 0], z], axis=2)
    bot = jnp.concatenate([z, w[:, 1]], axis=2)
    return jnp.concatenate([top, bot], axis=1)


def _pair_block_diag_grad(dw2):
    return jnp.stack([dw2[:, :64, :64], dw2[:, 64:, 64:]], axis=1).reshape(512, 64)


def kernel(x, ffn1_pre_g, ffn1_w_gu, ffn1_w_down, ffn1_post_g, mix_pre_g, w_in, conv_w, conv_b, w_rg, b_rg, w_ig, b_ig, lru_lambda, sinks, g_lru_out, g_attn_out, w_o, mix_post_g, ffn2_pre_g, ffn2_w_gu, ffn2_w_down, ffn2_post_g, loss_target, m_ffn1_pre_g, m_ffn1_w_gu, m_ffn1_w_down, m_ffn1_post_g, m_mix_pre_g, m_w_in, m_conv_w, m_conv_b, m_w_rg, m_b_rg, m_w_ig, m_b_ig, m_lru_lambda, m_sinks, m_g_lru_out, m_g_attn_out, m_w_o, m_mix_post_g, m_ffn2_pre_g, m_ffn2_w_gu, m_ffn2_w_down, m_ffn2_post_g, v_ffn1_pre_g, v_ffn1_w_gu, v_ffn1_w_down, v_ffn1_post_g, v_mix_pre_g, v_w_in, v_conv_w, v_conv_b, v_w_rg, v_b_rg, v_w_ig, v_b_ig, v_lru_lambda, v_sinks, v_g_lru_out, v_g_attn_out, v_w_o, v_mix_post_g, v_ffn2_pre_g, v_ffn2_w_gu, v_ffn2_w_down, v_ffn2_post_g):
    args = dict(locals())
    names = ["ffn1_pre_g", "ffn1_w_gu", "ffn1_w_down", "ffn1_post_g", "mix_pre_g", "w_in", "conv_w", "conv_b", "w_rg",
             "b_rg", "w_ig", "b_ig", "lru_lambda", "sinks", "g_lru_out", "g_attn_out", "w_o", "mix_post_g",
             "ffn2_pre_g", "ffn2_w_gu", "ffn2_w_down", "ffn2_post_g"]
    big = ["ffn1_w_gu", "ffn1_w_down", "w_in", "w_o", "ffn2_w_gu", "ffn2_w_down"]
    w = {k: args[k] for k in names}
    mom = {k: args["m_" + k] for k in names}
    var = {k: args["v_" + k] for k in names}
    t = x.shape[1]
    xs = x.reshape(t, D_MODEL)
    target = loss_target.reshape(t, D_MODEL)
    cx, cy, cc = _coords()
    me = 4 * cx + 2 * cy + cc
    other = jnp.stack([2 * (1 - cx) + cy, 2 * cx + (1 - cy), 2 * (1 - cx) + (1 - cy), cc]).astype(jnp.int32)
    place = jnp.stack([2 * cx + cy, cc]).astype(jnp.int32)

    transposed = ("ffn1_w_gu", "w_in", "ffn2_w_gu")

    def shard_view(a, k):
        return jnp.transpose(a[0], (1, 0)) if k in transposed else a[0]

    def shard_unview(a, k):
        return (jnp.transpose(a, (1, 0)) if k in transposed else a)[None]

    shard2d = {k: shard_view(w[k], k) for k in big}
    shard_bf = {k: shard2d[k].astype(BF16) for k in big}
    conv_pad = jnp.pad(conv_w.reshape(4, 64), ((0, 4), (0, 64)))
    (first_w,) = _run_exchanges([_Gather([shard_bf["ffn1_w_gu"], shard_bf["ffn1_w_down"]], routed=True)], "all_gather_ffn1")
    wgu1 = first_w[0].reshape(2, N_CHUNK, CHUNK, D_MODEL)
    wd1 = first_w[1].reshape(N_CHUNK, CHUNK, D_MODEL)
    rest = _Gather([shard_bf["w_in"], shard_bf["w_o"], shard_bf["ffn2_w_gu"], shard_bf["ffn2_w_down"], conv_pad])

    x1, f1, n1, gu1, gathered = _ffn_fwd(xs, ffn1_pre_g, wgu1, wd1, ffn1_post_g, None, "ffn1_fwd", rest)
    w_in_full = _dup_in_rows(gathered[0].reshape(D_IN, D_MODEL))
    w_o_full = gathered[1].reshape(D_MODEL, D_MODEL)
    wgu2 = gathered[2].reshape(2, N_CHUNK, CHUNK, D_MODEL)
    wd2 = gathered[3].reshape(N_CHUNK, CHUNK, D_MODEL)
    conv_w_full = jnp.transpose(gathered[4][:, 0:4, 0:64], (1, 0, 2)).reshape(4, D_LRU)
    p_lru = jnp.concatenate([conv_b, b_rg, b_ig, lru_lambda, conv_w_full], axis=0)
    wrg2 = _pair_block_diag(w_rg[0]).astype(BF16)
    wig2 = _pair_block_diag(w_ig[0]).astype(BF16)
    xl, gl, q, kv = _mix_in_fwd(x1, mix_pre_g, w_in_full)
    h = _lru_fwd(xl, p_lru, wrg2, wig2)
    o = _attn_fwd(q, kv, sinks)
    x2, mo = _mix_out_fwd(x1, h, gl, o, g_lru_out, g_attn_out, mix_post_g, w_o_full)
    g = {}
    dx3, n2, df2, gu2, g["ffn2_post_g"], loss_parts, _ = _ffn_fwd(x2, ffn2_pre_g, wgu2, wd2, ffn2_post_g, target, "ffn2_fwd")
    loss_local = jnp.sum(loss_parts[::8, 0])

    partial, from_sibling, from_chips = {}, {}, {}

    def chip_sums(keys):
        return [_chip_sum(partial[k], from_sibling[k], other, "chip_sum_" + k) for k in keys]

    dgu2, dwgu2, dwd2, _ = _ffn_bwd_w(n2, df2, gu2, wd2, "ffn2_bwd_w")
    partial["ffn2_w_gu"] = dwgu2.reshape(N_DEV, D_MODEL, CHUNK)
    partial["ffn2_w_down"] = dwd2.reshape(N_DEV, D_FF // N_DEV, D_MODEL)
    ffn2_keys = ["ffn2_w_gu", "ffn2_w_down"]
    dx2, g["ffn2_pre_g"], got = _ffn_bwd_x(dgu2, wgu2, x2, ffn2_pre_g, dx3, "ffn2_bwd_x",
                                           _SiblingExchange([partial[k] for k in ffn2_keys]))
    from_sibling.update(zip(ffn2_keys, got))
    dy, do, dwo, g["mix_post_g"], g["g_lru_out"], g["g_attn_out"] = _mix_out_bwd(
        dx2, mo, h, gl, o, g_lru_out, g_attn_out, mix_post_g, w_o_full)
    dq, dkv, dsink, got = _attn_bwd(q, kv, do, sinks, _ChipExchange(chip_sums(ffn2_keys)))
    from_chips.update(zip(ffn2_keys, got))
    dxl, dgl, dp, dwrg2, dwig2 = _lru_bwd(dy, h, xl, gl, p_lru, wrg2, wig2)
    dx1, dwin_dup, g["mix_pre_g"], df1, g["ffn1_post_g"] = _mix_in_bwd(
        dx2, x1, mix_pre_g, dxl, dgl, dq, dkv, w_in_full, f1, ffn1_post_g)
    partial["w_in"] = jnp.transpose(_undup_in_columns(dwin_dup).reshape(D_MODEL, N_DEV, D_IN // N_DEV), (1, 0, 2))
    partial["w_o"] = dwo.reshape(N_DEV, D_MODEL // N_DEV, D_MODEL)
    mix_keys = ["w_in", "w_o"]
    (got,) = _run_exchanges([_SiblingExchange([partial[k] for k in mix_keys])], "mix_sibling_exchange")
    from_sibling.update(zip(mix_keys, got))
    dgu1, dwgu1, dwd1, got = _ffn_bwd_w(n1, df1, gu1, wd1, "ffn1_bwd_w", _ChipExchange(chip_sums(mix_keys)))
    from_chips.update(zip(mix_keys, got))
    partial["ffn1_w_gu"] = dwgu1.reshape(N_DEV, D_MODEL, CHUNK)
    partial["ffn1_w_down"] = dwd1.reshape(N_DEV, D_FF // N_DEV, D_MODEL)
    ffn1_keys = ["ffn1_w_gu", "ffn1_w_down"]
    (got,) = _run_exchanges([_SiblingExchange([partial[k] for k in ffn1_keys])], "ffn1_sibling_exchange")
    from_sibling.update(zip(ffn1_keys, got))
    zeros2 = jnp.zeros((2, D_MODEL), F32)
    g_gains = jnp.concatenate([zeros2[:1]] + [g[k] for k in GAINS[1:]] + [zeros2], axis=0)
    g_halves = jnp.concatenate([dp[0:4], g["g_lru_out"], g["g_attn_out"], zeros2[:, :D_LRU]], axis=0)
    g_gates = jnp.concatenate([_pair_block_diag_grad(dwrg2), _pair_block_diag_grad(dwig2)], axis=0)
    g_misc = jnp.concatenate([dp[4:8], jnp.pad(dsink[:, 0].reshape(1, 8), ((0, 0), (0, D_LRU - 8))),
                              jnp.pad(loss_local.reshape(1, 1), ((0, 0), (0, D_LRU - 1))), zeros2[:, :D_LRU]], axis=0)
    dx0, g_first, got = _ffn_bwd_x(dgu1, wgu1, xs, ffn1_pre_g, dx1, "ffn1_bwd_x",
                                   _Both(_ChipExchange(chip_sums(ffn1_keys)), _Gather([g_gains, g_halves, g_gates, g_misc])))
    from_chips.update(zip(ffn1_keys, got[:2]))
    gathered_small = got[2:]

    grads, delta, new_m, new_v = {}, {}, {}, {}
    for k in big:
        res = _shard_update(partial[k], from_sibling[k], from_chips[k], shard2d[k], shard_view(mom[k], k),
                            shard_view(var[k], k), place, "update_" + k, k in transposed)
        grads[k], delta[k], new_m[k], new_v[k] = [shard_unview(r, k) for r in res]

    ((gathered_first,),) = _run_exchanges([_Gather([g_first])], "all_gather_first_gain")
    col = me * 64

    def small_view(vals):
        out = {k: vals[k] for k in GAINS + HALVES + ("sinks",)}
        out.update({k: vals[k].reshape(512, 64) for k in GATES})
        out["conv_w"] = lax.dynamic_update_slice(jnp.zeros((4, D_LRU), F32), vals["conv_w"].reshape(4, 64), (0, col))
        return out

    *small, loss_tile = _small_update([*gathered_small, gathered_first], small_view(w), small_view(mom), small_view(var))
    for dst, part in zip((grads, delta, new_m, new_v), small):
        for k in SMALL:
            if k == "conv_w":
                dst[k] = lax.dynamic_slice(part[k], (0, col), (4, 64)).reshape(conv_w.shape)
            else:
                dst[k] = part[k].reshape(w[k].shape)
    return (loss_tile[0, 0], dx0.reshape(x.shape), *[grads[k] for k in names], *[delta[k] for k in names],
            *[new_m[k] for k in names], *[new_v[k] for k in names])
```

```python
import functools

import jax
import jax.numpy as jnp
from jax import lax
from jax.experimental import pallas as pl
from jax.experimental.pallas import tpu as pltpu

F32 = jnp.float32
BF16 = jnp.bfloat16

D_MODEL = 1024
D_FF = 2816
N_DEV = 8
N_CHUNK = 4
CHUNK = D_FF // N_CHUNK
D_LRU = 512
D_ATTN = 512
LRU_GROUP = 128
N_LRU_GROUP = D_LRU // LRU_GROUP
HEAD_DIM = 64
BLOCK_Q = 128
D_IN = 1792
D_IN_DUP = 2048
RMS_EPS = 1e-6
LRU_C = 8.0
MASK_VALUE = -1e30
ATTN_SCALE = HEAD_DIM ** -0.5

ADAM_LR = 0.001
ADAM_B1 = 0.9
ADAM_B2 = 0.999
ADAM_EPS = 1e-08
ADAM_WD = 0.01
ADAM_STEP = 10

VMEM_LIMIT_V7X = 56 * 2 ** 20

ANY = pl.BlockSpec(memory_space=pl.ANY)
SMEM = pl.BlockSpec(memory_space=pltpu.SMEM)
MESH = pl.DeviceIdType.MESH


def _params(n_grid=0):
    sem = ("arbitrary",) * n_grid if n_grid else None
    return pltpu.CompilerParams(dimension_semantics=sem, vmem_limit_bytes=VMEM_LIMIT_V7X)


def _dot(a, b):
    return lax.dot_general(a, b, (((1,), (0,)), ((), ())), preferred_element_type=F32)


def _dot_nt(a, b):
    return lax.dot_general(a, b, (((1,), (1,)), ((), ())), preferred_element_type=F32)


def _dot_tn(a, b):
    return lax.dot_general(a, b, (((0,), (0,)), ((), ())), preferred_element_type=F32)


def _sigmoid(x):
    return 1.0 / (1.0 + jnp.exp(-x))


def _rms_fwd(x, g):
    r = lax.rsqrt(jnp.mean(x * x, axis=-1, keepdims=True) + RMS_EPS)
    return x * r * g


def _rms_bwd(x, g, dy):
    r = lax.rsqrt(jnp.mean(x * x, axis=-1, keepdims=True) + RMS_EPS)
    xh = x * r
    dg = jnp.sum(dy * xh, axis=0, keepdims=True)
    dxh = dy * g
    dx = r * (dxh - xh * jnp.mean(dxh * xh, axis=-1, keepdims=True))
    return dx, dg


def _gelu(x):
    c = 0.7978845608028654
    inner = c * (x + 0.044715 * x * x * x)
    th = jnp.tanh(inner)
    ge = 0.5 * x * (1.0 + th)
    dge = 0.5 * (1.0 + th) + 0.5 * x * (1.0 - th * th) * c * (1.0 + 3.0 * 0.044715 * x * x)
    return ge, dge


def _zero_at_first(first, *refs):
    @pl.when(first)
    def _():
        for ref in refs:
            ref[...] = jnp.zeros_like(ref)


def _token_tile(t):
    return 512 if t >= 2048 else t // 2


def _ffn_bwd_tile(t):
    return 1024 if t >= 4096 else t // 2


def _coords():
    return lax.axis_index("x"), lax.axis_index("y"), lax.axis_index("c")


class _Gather:
    n_phases = 3
    at = (0.0, 0.8, 1.0)

    def __init__(self, shards, routed=False):
        k = len(shards)
        self.routed = routed
        self.arrays = list(shards)
        self.out_shape = [jax.ShapeDtypeStruct((N_DEV,) + s.shape, s.dtype) for s in shards]
        self.scratch = [pltpu.SemaphoreType.DMA((7 * k,)), pltpu.SemaphoreType.DMA((7 * k,)), pltpu.SemaphoreType.DMA((k,))]

    def run(self, phase, ins, outs, sems):
        send_sems, recv_sems, local_sems = sems
        k_arr = len(ins)
        x, y, c = _coords()
        me, sibling = (x, y, c), (x, y, 1 - c)
        chips = [(1 - x, y), (x, 1 - y), (1 - x, 1 - y)]
        direct = 2 if self.routed else 3
        relay_from = (x + (1 - c) * (1 - 2 * x), y + c * (1 - 2 * y))
        relay_to = (x + c * (1 - 2 * x), y + (1 - c) * (1 - 2 * y))

        def rows(k, dev):
            return outs[k].at[4 * dev[0] + 2 * dev[1] + dev[2]]

        def copy(k, slot, block, to, src=None):
            return pltpu.make_async_remote_copy(
                src_ref=rows(k, block) if src is None else src, dst_ref=rows(k, block),
                send_sem=send_sems.at[7 * k + slot], recv_sem=recv_sems.at[7 * k + slot],
                device_id=to, device_id_type=MESH)

        def mine():
            return [pltpu.make_async_copy(ins[k], rows(k, me), local_sems.at[k]) for k in range(k_arr)]

        def first():
            return [copy(k, slot, me, to, src=ins[k]) for k in range(k_arr)
                    for slot, to in enumerate([sibling] + [(*chip, c) for chip in chips[:direct]])]

        def relayed(k):
            return copy(k, 3, (*relay_from, c), (*relay_to, c))

        def passed(j, k):
            return copy(k, 4 + j, (*chips[j], c), sibling)

        if phase == 0:
            for cp in mine() + first():
                cp.start()
        elif phase == 1:
            for j in range(direct):
                for k in range(k_arr):
                    copy(k, 1 + j, (*chips[j], c), me).wait_recv()
            for k in range(k_arr):
                if self.routed:
                    relayed(k).start()
                for j in range(direct):
                    passed(j, k).start()
        else:
            for k in range(k_arr):
                if self.routed:
                    copy(k, 3, (*chips[2], c), me).wait_recv()
                    passed(2, k).start()
            for k in range(k_arr):
                copy(k, 0, sibling, me).wait_recv()
                for j, chip in enumerate(chips):
                    copy(k, 4 + j, (*chip, 1 - c), me).wait_recv()
            sent = first() + [passed(j, k) for j in range(3) for k in range(k_arr)]
            if self.routed:
                sent += [relayed(k) for k in range(k_arr)]
            for cp in sent:
                cp.wait_send()
            for cp in mine():
                cp.wait()


class _SiblingExchange:
    n_phases = 2
    at = (0.0, 1.0)

    def __init__(self, grads):
        k = len(grads)
        self.arrays = list(grads)
        self.out_shape = [jax.ShapeDtypeStruct((4,) + g.shape[1:], g.dtype) for g in grads]
        self.scratch = [pltpu.SemaphoreType.DMA((4 * k,)), pltpu.SemaphoreType.DMA((4 * k,))]

    def run(self, phase, ins, outs, sems):
        send_sems, recv_sems = sems
        x, y, c = _coords()
        copies = [pltpu.make_async_remote_copy(
            src_ref=ins[k].at[2 * q + (1 - c)], dst_ref=outs[k].at[q],
            send_sem=send_sems.at[4 * k + q], recv_sem=recv_sems.at[4 * k + q],
            device_id=(x, y, 1 - c), device_id_type=MESH) for k in range(len(ins)) for q in range(4)]
        for cp in copies:
            if phase == 0:
                cp.start()
            else:
                cp.wait_recv()
                cp.wait_send()


class _ChipExchange:
    n_phases = 2
    at = (0.0, 1.0)

    def __init__(self, chip_sums):
        k = len(chip_sums)
        self.arrays = list(chip_sums)
        self.out_shape = [jax.ShapeDtypeStruct((3,) + s.shape[1:], s.dtype) for s in chip_sums]
        self.scratch = [pltpu.SemaphoreType.DMA((3 * k,)), pltpu.SemaphoreType.DMA((3 * k,))]

    def run(self, phase, ins, outs, sems):
        send_sems, recv_sems = sems
        x, y, c = _coords()
        chips = [(1 - x, y), (x, 1 - y), (1 - x, 1 - y)]
        copies = [pltpu.make_async_remote_copy(
            src_ref=ins[k].at[j], dst_ref=outs[k].at[j],
            send_sem=send_sems.at[3 * k + j], recv_sem=recv_sems.at[3 * k + j],
            device_id=(*chip, c), device_id_type=MESH) for k in range(len(ins)) for j, chip in enumerate(chips)]
        for cp in copies:
            if phase == 0:
                cp.start()
            else:
                cp.wait_recv()
                cp.wait_send()


class _Both:
    n_phases = 3
    at = (0.0, 0.95, 1.0)

    def __init__(self, two_phase, gather):
        self.parts = (two_phase, gather)
        self.arrays = two_phase.arrays + gather.arrays
        self.out_shape = two_phase.out_shape + gather.out_shape
        self.scratch = two_phase.scratch + gather.scratch

    def run(self, phase, ins, outs, sems):
        a, b = self.parts
        n_in, n_out, n_sem = len(a.arrays), len(a.out_shape), len(a.scratch)
        refs_a = (ins[:n_in], outs[:n_out], sems[:n_sem])
        refs_b = (ins[n_in:], outs[n_out:], sems[n_sem:])
        b.run(phase, *refs_b)
        if phase == 0:
            a.run(0, *refs_a)
        if phase == 2:
            a.run(1, *refs_a)


class _Host:
    def __init__(self, exchange):
        self.ex = exchange
        self.args = [] if exchange is None else exchange.arrays
        self.in_specs = [ANY] * len(self.args)
        self.out_shape = [] if exchange is None else exchange.out_shape
        self.out_specs = [ANY] * len(self.out_shape)
        self.scratch = [] if exchange is None else exchange.scratch

    def split(self, refs, n_in, n_out, n_scratch):
        a, b, s = len(self.args), len(self.out_shape), len(self.scratch)
        own_in, ex_in = refs[:n_in], refs[n_in:n_in + a]
        rest = refs[n_in + a:]
        own_out, ex_out = rest[:n_out], rest[n_out:n_out + b]
        rest = rest[n_out + b:]
        own_scratch, ex_sems = rest[:n_scratch], rest[n_scratch:n_scratch + s]
        return list(own_in) + list(own_out) + list(own_scratch), (ex_in, ex_out, ex_sems)

    def at_steps(self, step, n_steps, ex_refs):
        if self.ex is None:
            return
        for p in range(self.ex.n_phases):
            pl.when(step == int(round(self.ex.at[p] * (n_steps - 1))))(functools.partial(self.ex.run, p, *ex_refs))

    def phase(self, p, ex_refs):
        if self.ex is not None:
            self.ex.run(p, *ex_refs)


def _run_exchanges(exchanges, name):
    hosts = [_Host(ex) for ex in exchanges]
    n_in = [len(h.args) for h in hosts]
    n_out = [len(h.out_shape) for h in hosts]
    n_sc = [len(h.scratch) for h in hosts]

    def body(*refs):
        ins, outs, scr = refs[:sum(n_in)], refs[sum(n_in):sum(n_in) + sum(n_out)], refs[sum(n_in) + sum(n_out):]
        parts = []
        for e in range(len(hosts)):
            parts.append((ins[sum(n_in[:e]):sum(n_in[:e + 1])], outs[sum(n_out[:e]):sum(n_out[:e + 1])],
                          scr[sum(n_sc[:e]):sum(n_sc[:e + 1])]))
        for h, part in zip(hosts, parts):
            h.phase(0, part)
        for h, part in zip(hosts, parts):
            for p in range(1, h.ex.n_phases):
                h.phase(p, part)

    res = pl.pallas_call(
        body, name=name, in_specs=[ANY] * sum(n_in), out_specs=[ANY] * sum(n_out),
        out_shape=[s for h in hosts for s in h.out_shape], scratch_shapes=[s for h in hosts for s in h.scratch],
    )(*[a for h in hosts for a in h.args])
    return [res[sum(n_out[:e]):sum(n_out[:e + 1])] for e in range(len(hosts))]


def _ffn_fwd(x, g_pre, wgu, wd, g_post, target, name, exchange=None):
    t = x.shape[0]
    tm = _token_tile(t)
    n_i = t // tm
    with_loss = target is not None
    host = _Host(exchange)
    n_in, n_out = (6, 6) if with_loss else (5, 4)

    def body(*refs):
        own, ex_refs = host.split(refs, n_in, n_out, 0)
        if with_loss:
            x_ref, gpre_ref, wgu_ref, wd_ref, gpost_ref, tgt_ref, xo_ref, n_ref, df_ref, gu_ref, dgpost_ref, loss_ref = own
            _zero_at_first(pl.program_id(0) == 0, dgpost_ref)
        else:
            x_ref, gpre_ref, wgu_ref, wd_ref, gpost_ref, xo_ref, f_ref, n_ref, gu_ref = own
        host.at_steps(pl.program_id(0), n_i, ex_refs)
        x = x_ref[...]
        n = _rms_fwd(x, gpre_ref[...]).astype(BF16)
        n_ref[...] = n
        f = None
        for j in range(N_CHUNK):
            gate = _dot_nt(n, wgu_ref[0, j])
            up = _dot_nt(n, wgu_ref[1, j])
            gu_ref[0, j] = gate.astype(BF16)
            gu_ref[1, j] = up.astype(BF16)
            part = _dot((gate * _sigmoid(gate) * up).astype(BF16), wd_ref[j])
            f = part if f is None else f + part
        xo = x + 0.5 * _rms_fwd(f, gpost_ref[...])
        if with_loss:
            err = xo - tgt_ref[...]
            d_out = err * (1.0 / D_MODEL)
            xo_ref[...] = d_out
            df, dg = _rms_bwd(f, gpost_ref[...], 0.5 * d_out)
            df_ref[...] = df.astype(BF16)
            dgpost_ref[...] += dg
            part = 0.5 * jnp.sum(jnp.sum(err * err, axis=-1, keepdims=True) * (1.0 / D_MODEL), axis=0, keepdims=True)
            loss_ref[...] = jnp.broadcast_to(part, loss_ref.shape)
        else:
            f_ref[...] = f
            xo_ref[...] = xo

    tok = pl.BlockSpec((tm, D_MODEL), lambda i: (i, 0))
    vec = pl.BlockSpec((1, D_MODEL), lambda i: (0, 0))
    act = pl.BlockSpec((2, N_CHUNK, tm, CHUNK), lambda i: (0, 0, i, 0))
    tok_f32 = jax.ShapeDtypeStruct((t, D_MODEL), F32)
    tok_bf16 = jax.ShapeDtypeStruct((t, D_MODEL), BF16)
    act_shape = jax.ShapeDtypeStruct((2, N_CHUNK, t, CHUNK), BF16)
    in_specs = [tok, vec,
                pl.BlockSpec((2, N_CHUNK, CHUNK, D_MODEL), lambda i: (0, 0, 0, 0), pipeline_mode=pl.Buffered(1)),
                pl.BlockSpec((N_CHUNK, CHUNK, D_MODEL), lambda i: (0, 0, 0), pipeline_mode=pl.Buffered(1)),
                vec]
    args = [x, g_pre, wgu, wd, g_post]
    if with_loss:
        in_specs.append(tok)
        args.append(target)
        out_shape = [tok_f32, tok_bf16, tok_bf16, act_shape, jax.ShapeDtypeStruct((1, D_MODEL), F32),
                     jax.ShapeDtypeStruct((n_i * 8, 128), F32)]
        out_specs = [tok, tok, tok, act, vec, pl.BlockSpec((8, 128), lambda i: (i, 0))]
    else:
        out_shape = [tok_f32, tok_f32, tok_bf16, act_shape]
        out_specs = [tok, tok, tok, act]
    res = pl.pallas_call(
        body, name=name, grid=(n_i,), in_specs=in_specs + host.in_specs, out_specs=out_specs + host.out_specs,
        out_shape=out_shape + host.out_shape, scratch_shapes=host.scratch, compiler_params=_params(1),
    )(*args, *host.args)
    return (*res[:n_out], list(res[n_out:]))


def _ffn_bwd_w(n, df, gu, wd, name, exchange=None):
    t = n.shape[0]
    tm = _ffn_bwd_tile(t)
    n_i = t // tm
    host = _Host(exchange)

    def body(*refs):
        (n_ref, df_ref, gu_ref, wd_ref, dgu_ref, dwgu_ref, dwd_ref), ex_refs = host.split(refs, 4, 3, 0)
        i = pl.program_id(1)
        host.at_steps(pl.program_id(0) * n_i + i, N_CHUNK * n_i, ex_refs)
        _zero_at_first(i == 0, dwgu_ref, dwd_ref)
        nb = n_ref[...]
        dfb = df_ref[...]
        gate = gu_ref[0, 0].astype(F32)
        up = gu_ref[1, 0].astype(F32)
        s = _sigmoid(gate)
        silu = gate * s
        a = (silu * up).astype(BF16)
        da = _dot_nt(dfb, wd_ref[0])
        dup = (da * silu).astype(BF16)
        dgate = (da * up * (s * (1.0 + gate * (1.0 - s)))).astype(BF16)
        dgu_ref[0, 0] = dgate
        dgu_ref[1, 0] = dup
        dwgu_ref[0, 0] += _dot_tn(nb, dgate)
        dwgu_ref[1, 0] += _dot_tn(nb, dup)
        dwd_ref[0] += _dot_tn(a, dfb)

    tok = pl.BlockSpec((tm, D_MODEL), lambda j, i: (i, 0))
    act = pl.BlockSpec((2, 1, tm, CHUNK), lambda j, i: (0, j, i, 0))
    wgu_spec = pl.BlockSpec((2, 1, D_MODEL, CHUNK), lambda j, i: (0, j, 0, 0))
    wd_spec = pl.BlockSpec((1, CHUNK, D_MODEL), lambda j, i: (j, 0, 0))
    res = pl.pallas_call(
        body, name=name, grid=(N_CHUNK, n_i),
        in_specs=[tok, tok, act, wd_spec] + host.in_specs,
        out_specs=[act, wgu_spec, wd_spec] + host.out_specs,
        out_shape=[jax.ShapeDtypeStruct((2, N_CHUNK, t, CHUNK), BF16),
                   jax.ShapeDtypeStruct((2, N_CHUNK, D_MODEL, CHUNK), F32),
                   jax.ShapeDtypeStruct((N_CHUNK, CHUNK, D_MODEL), F32)] + host.out_shape,
        scratch_shapes=host.scratch, compiler_params=_params(2),
    )(n, df, gu, wd, *host.args)
    return (*res[:3], list(res[3:]))


def _ffn_bwd_x(dgu, wgu, x, g_pre, d_out, name, exchange=None):
    t = x.shape[0]
    tm = _token_tile(t)
    n_i = t // tm
    host = _Host(exchange)

    def body(*refs):
        (dgu_ref, wgu_ref, x_ref, gpre_ref, do_ref, dx_ref, dgpre_ref), ex_refs = host.split(refs, 5, 2, 0)
        i = pl.program_id(0)
        host.at_steps(i, n_i, ex_refs)
        _zero_at_first(i == 0, dgpre_ref)
        dn = _dot(dgu_ref[0, 0], wgu_ref[0, 0]) + _dot(dgu_ref[1, 0], wgu_ref[1, 0])
        for j in range(1, N_CHUNK):
            dn = dn + _dot(dgu_ref[0, j], wgu_ref[0, j]) + _dot(dgu_ref[1, j], wgu_ref[1, j])
        dx, dg = _rms_bwd(x_ref[...], gpre_ref[...], dn)
        dx_ref[...] = do_ref[...] + dx
        dgpre_ref[...] += dg

    tok = pl.BlockSpec((tm, D_MODEL), lambda i: (i, 0))
    vec = pl.BlockSpec((1, D_MODEL), lambda i: (0, 0))
    res = pl.pallas_call(
        body, name=name, grid=(n_i,),
        in_specs=[pl.BlockSpec((2, N_CHUNK, tm, CHUNK), lambda i: (0, 0, i, 0)),
                  pl.BlockSpec((2, N_CHUNK, CHUNK, D_MODEL), lambda i: (0, 0, 0, 0), pipeline_mode=pl.Buffered(1)),
                  tok, vec, tok] + host.in_specs,
        out_specs=[tok, vec] + host.out_specs,
        out_shape=[jax.ShapeDtypeStruct((t, D_MODEL), F32), jax.ShapeDtypeStruct((1, D_MODEL), F32)] + host.out_shape,
        scratch_shapes=host.scratch, compiler_params=_params(1),
    )(dgu, wgu, x, g_pre, d_out, *host.args)
    return (*res[:2], list(res[2:]))


def _mix_in_fwd(x1, g, w_in):
    t = x1.shape[0]
    tm = _token_tile(t)

    def body(x_ref, g_ref, w_ref, xl_ref, gl_ref, q_ref, kv_ref):
        n = _rms_fwd(x_ref[...], g_ref[...]).astype(BF16)
        proj = _dot_nt(n, w_ref[...])
        xl_ref[...] = proj[:, 0:512]
        gl_ref[...] = proj[:, 512:1024]
        q_ref[...] = proj[:, 1024:1536].astype(BF16)
        kv_ref[...] = proj[:, 1536:2048].astype(BF16)

    tok = pl.BlockSpec((tm, D_MODEL), lambda i: (i, 0))
    half = pl.BlockSpec((tm, 512), lambda i: (i, 0))
    return pl.pallas_call(
        body, name="mix_in_fwd", grid=(t // tm,),
        in_specs=[tok, pl.BlockSpec((1, D_MODEL), lambda i: (0, 0)), pl.BlockSpec((D_IN_DUP, D_MODEL), lambda i: (0, 0))],
        out_specs=[half, half, half, half],
        out_shape=[jax.ShapeDtypeStruct((t, 512), F32), jax.ShapeDtypeStruct((t, 512), F32),
                   jax.ShapeDtypeStruct((t, 512), BF16), jax.ShapeDtypeStruct((t, 512), BF16)],
        compiler_params=_params(1),
    )(x1, g, w_in)


def _shift_down(x, before, s):
    if s == 0:
        return x
    rolled = pltpu.roll(x, s, 0)
    ext = jnp.concatenate([before, x[0:8]], axis=0)
    first8 = pltpu.roll(ext, s, 0)[8:16]
    return jnp.concatenate([first8, rolled[8:]], axis=0)


def _shift_up(x, after, s):
    if s == 0:
        return x
    rows = x.shape[0]
    rolled = pltpu.roll(x, rows - s, 0)
    ext = jnp.concatenate([x[rows - 8:rows], after], axis=0)
    last8 = pltpu.roll(ext, 16 - s, 0)[0:8]
    return jnp.concatenate([rolled[:rows - 8], last8], axis=0)


def _log_sigmoid(x):
    e = jnp.exp(-jnp.abs(x))
    log1p_e = jnp.where(e < 0.01, e * (1.0 - e * (0.5 - e * (1.0 / 3.0))), jnp.log(1.0 + e))
    return jnp.minimum(x, 0.0) - log1p_e


def _lru_gates(xc, p_ref, wrg, wig):
    xcb = xc.astype(BF16)
    r = _sigmoid(_dot(xcb, wrg) + p_ref[1:2, :])
    ig = _sigmoid(_dot(xcb, wig) + p_ref[2:3, :])
    ls = _log_sigmoid(p_ref[3:4, :])
    log_a = LRU_C * r * ls
    a = jnp.exp(log_a)
    mult = jnp.sqrt(-jnp.tanh(log_a) * (a * a + 1.0))
    return xcb, r, ig, ls, a, mult


def _conv_taps(x, before, p_ref):
    xc = x * p_ref[7:8, :]
    for s in (1, 2, 3):
        xc = xc + _shift_down(x, before, s) * p_ref[7 - s:8 - s, :]
    return xc + p_ref[0:1, :]


def _lru_block_rows(t):
    return 512 if t >= 1024 else t // 2


def _lru_fwd(xl, p, wrg2, wig2):
    t = xl.shape[0]
    tb = _lru_block_rows(t)

    def body(xl_ref, p_ref, wrg_ref, wig_ref, h_ref, x_tail, h_carry):
        tt = pl.program_id(1)

        @pl.when(tt == 0)
        def _():
            x_tail[...] = jnp.zeros_like(x_tail)
            h_carry[...] = jnp.zeros_like(h_carry)

        x = xl_ref[...]
        xc = _conv_taps(x, x_tail[...], p_ref)
        x_tail[...] = x[tb - 8:tb]
        _, r, ig, ls, a, mult = _lru_gates(xc, p_ref, wrg_ref[0], wig_ref[0])
        u = mult * ig * xc
        row = lax.broadcasted_iota(jnp.int32, (tb, LRU_GROUP), 0)
        s = 1
        while s < tb:
            keep = row >= s
            u = jnp.where(keep, a * pltpu.roll(u, s, 0) + u, u)
            a = jnp.where(keep, a * pltpu.roll(a, s, 0), a)
            s *= 2
        h = u + a * h_carry[0:1, :]
        h_ref[...] = h
        h_carry[...] = jnp.broadcast_to(h[tb - 1:tb], h_carry.shape)

    blk = pl.BlockSpec((tb, LRU_GROUP), lambda g, tt: (tt, g))
    par = pl.BlockSpec((8, LRU_GROUP), lambda g, tt: (0, g))
    wsp = pl.BlockSpec((1, LRU_GROUP, LRU_GROUP), lambda g, tt: (g, 0, 0))
    return pl.pallas_call(
        body, name="lru_fwd", grid=(N_LRU_GROUP, t // tb), in_specs=[blk, par, wsp, wsp], out_specs=blk,
        out_shape=jax.ShapeDtypeStruct((t, D_LRU), F32),
        scratch_shapes=[pltpu.VMEM((8, LRU_GROUP), F32), pltpu.VMEM((8, LRU_GROUP), F32)],
        compiler_params=_params(2),
    )(xl, p, wrg2, wig2)


def _lru_bwd(dy, h, xl, gl, p, wrg2, wig2):
    t = xl.shape[0]
    tb = _lru_block_rows(t)
    n_tb = t // tb
    tb8 = tb // 8

    def body(dy_ref, h_ref, hprev_ref, xl_ref, xprev_ref, gl_ref, p_ref, wrg_ref, wig_ref,
             dxl_ref, dgl_ref, dp_ref, dwrg_ref, dwig_ref, g_carry, a_carry, dxc_head):
        step = pl.program_id(1)
        tt = n_tb - 1 - step
        first = step == 0

        _zero_at_first(first, g_carry, a_carry, dxc_head, dp_ref, dwrg_ref, dwig_ref)

        has_prev = (tt > 0).astype(F32)
        x = xl_ref[...]
        x_before = xprev_ref[...] * has_prev
        xs = [_shift_down(x, x_before, s) for s in range(4)]
        xc = xs[0] * p_ref[7:8, :] + xs[1] * p_ref[6:7, :] + xs[2] * p_ref[5:6, :] + xs[3] * p_ref[4:5, :] + p_ref[0:1, :]
        wrg = wrg_ref[0]
        wig = wig_ref[0]
        xcb, r, ig, ls, a, mult = _lru_gates(xc, p_ref, wrg, wig)

        hh = h_ref[...]
        h_m1 = _shift_down(hh, hprev_ref[...] * has_prev, 1)
        ge, dge = _gelu(gl_ref[...])
        dy = dy_ref[...]
        dgl_ref[...] = dy * hh * dge
        dh = dy * ge

        b = _shift_up(a, a_carry[...], 1)
        row = lax.broadcasted_iota(jnp.int32, (tb, LRU_GROUP), 0)
        g = dh
        s = 1
        while s < tb:
            keep = row < tb - s
            g = jnp.where(keep, b * pltpu.roll(g, tb - s, 0) + g, g)
            b = jnp.where(keep, b * pltpu.roll(b, tb - s, 0), b)
            s *= 2
        g = g + b * g_carry[0:1, :]
        g_carry[...] = jnp.broadcast_to(g[0:1], g_carry.shape)
        a_carry[...] = jnp.broadcast_to(a[0:1], a_carry.shape)

        da = g * h_m1
        dmult = g * ig * xc
        dig = g * mult * xc
        dxc = g * mult * ig
        dlog_a = da * a - dmult * (a * a) / mult
        dr = dlog_a * (LRU_C * ls)
        dls = jnp.sum(dlog_a * (LRU_C * r), axis=0, keepdims=True)
        dlam = dls * _sigmoid(-p_ref[3:4, :])
        dpre_r = dr * r * (1.0 - r)
        dpre_i = dig * ig * (1.0 - ig)
        dprb = dpre_r.astype(BF16)
        dpib = dpre_i.astype(BF16)
        dxc = dxc + _dot_nt(dprb, wrg) + _dot_nt(dpib, wig)
        dwrg_ref[0] += _dot_tn(xcb, dprb)
        dwig_ref[0] += _dot_tn(xcb, dpib)

        after = dxc_head[...]
        dxl = dxc * p_ref[7:8, :]
        for s in (1, 2, 3):
            dxl = dxl + _shift_up(dxc, after, s) * p_ref[7 - s:8 - s, :]
        dxl_ref[...] = dxl
        dxc_head[...] = dxc[0:8]

        rows = [jnp.sum(dxc, axis=0, keepdims=True), jnp.sum(dpre_r, axis=0, keepdims=True),
                jnp.sum(dpre_i, axis=0, keepdims=True), dlam]
        rows += [jnp.sum(dxc * xs[3 - k], axis=0, keepdims=True) for k in range(4)]
        dp_ref[...] += jnp.concatenate(rows, axis=0)

    blk = pl.BlockSpec((tb, LRU_GROUP), lambda g, s: (n_tb - 1 - s, g))
    prev8 = pl.BlockSpec((8, LRU_GROUP), lambda g, s: (jnp.maximum((n_tb - 1 - s) * tb8 - 1, 0), g))
    par = pl.BlockSpec((8, LRU_GROUP), lambda g, s: (0, g))
    wsp = pl.BlockSpec((1, LRU_GROUP, LRU_GROUP), lambda g, s: (g, 0, 0))
    return pl.pallas_call(
        body, name="lru_bwd", grid=(N_LRU_GROUP, n_tb),
        in_specs=[blk, blk, prev8, blk, prev8, blk, par, wsp, wsp], out_specs=[blk, blk, par, wsp, wsp],
        out_shape=[jax.ShapeDtypeStruct((t, D_LRU), F32), jax.ShapeDtypeStruct((t, D_LRU), F32),
                   jax.ShapeDtypeStruct((8, D_LRU), F32),
                   jax.ShapeDtypeStruct((N_LRU_GROUP, LRU_GROUP, LRU_GROUP), F32),
                   jax.ShapeDtypeStruct((N_LRU_GROUP, LRU_GROUP, LRU_GROUP), F32)],
        scratch_shapes=[pltpu.VMEM((8, LRU_GROUP), F32)] * 3,
        compiler_params=_params(2),
    )(dy, h, h, xl, xl, gl, p, wrg2, wig2)


def _attn_bias(first_block):
    qi = jnp.bitwise_and(lax.broadcasted_iota(jnp.int32, (4 * BLOCK_Q, 2 * BLOCK_Q), 0), BLOCK_Q - 1)
    kj = lax.broadcasted_iota(jnp.int32, (4 * BLOCK_Q, 2 * BLOCK_Q), 1)
    rel = qi + BLOCK_Q - kj
    mask = (rel >= 0) & (rel < BLOCK_Q)
    if first_block:
        mask = mask & (kj >= BLOCK_Q)
    return jnp.where(mask, 0.0, MASK_VALUE)


def _sink_column(sinks):
    hrow = lax.broadcasted_iota(jnp.int32, (4 * BLOCK_Q, 1), 0)
    return jnp.where(hrow < BLOCK_Q, sinks[0],
                     jnp.where(hrow < 2 * BLOCK_Q, sinks[1], jnp.where(hrow < 3 * BLOCK_Q, sinks[2], sinks[3])))


def _attn_scores(qv, kvv, n, bias, sk, lo):
    r0 = pl.multiple_of(n * BLOCK_Q, BLOCK_Q)
    rp = pl.multiple_of(jnp.maximum(n - 1, 0) * BLOCK_Q, BLOCK_Q)
    kvb = jnp.concatenate([kvv[pl.ds(rp, BLOCK_Q), :], kvv[pl.ds(r0, BLOCK_Q), :]], axis=0)
    k2 = kvb[:, 0:128]
    v2 = kvb[:, 128:256]
    qs = _stack_heads(qv[pl.ds(r0, BLOCK_Q), :], lo)
    s = _dot_nt(qs, k2) * ATTN_SCALE + bias
    m = jnp.maximum(jnp.max(s, axis=-1, keepdims=True), sk)
    e = jnp.exp(s - m)
    es = jnp.exp(sk - m)
    inv = 1.0 / (jnp.sum(e, axis=-1, keepdims=True) + es)
    return r0, rp, qs, k2, v2, e * inv, es * inv


def _stack_heads(pair2, lo):
    p0 = pair2[:, 0:128]
    p1 = pair2[:, 128:256]
    z = jnp.zeros_like(p0)
    return jnp.concatenate([jnp.where(lo, p0, z), jnp.where(lo, z, p0), jnp.where(lo, p1, z), jnp.where(lo, z, p1)], axis=0)


def _unstack_heads(st, lo):
    b = BLOCK_Q
    return jnp.concatenate([jnp.where(lo, st[0:b], st[b:2 * b]), jnp.where(lo, st[2 * b:3 * b], st[3 * b:4 * b])], axis=1)


def _attn_fwd(q, kv, sinks):
    t = q.shape[0]
    n_blk = t // BLOCK_Q

    def body(q_hbm, kv_hbm, s_ref, o_hbm, q2, kv2, o2, bias0, bias, sem):
        lo = lax.broadcasted_iota(jnp.int32, (BLOCK_Q, 128), 1) < HEAD_DIM
        cols = [pl.ds(256 * g, 256) for g in range(2)]
        loads = [[pltpu.make_async_copy(q_hbm.at[:, cols[g]], q2.at[g], sem.at[3 * g]),
                  pltpu.make_async_copy(kv_hbm.at[:, cols[g]], kv2.at[g], sem.at[3 * g + 1])] for g in range(2)]
        stores = [pltpu.make_async_copy(o2.at[g], o_hbm.at[:, cols[g]], sem.at[3 * g + 2]) for g in range(2)]
        for cp in loads[0] + loads[1]:
            cp.start()
        bias0[...] = _attn_bias(True)
        bias[...] = _attn_bias(False)
        for g in range(2):
            for cp in loads[g]:
                cp.wait()
            qv, kvv, ov = q2.at[g], kv2.at[g], o2.at[g]
            sk = _sink_column([s_ref[0, 4 * g + i] for i in range(4)])

            def block(n, bias_ref):
                r0, _, _, _, v2, prob, _ = _attn_scores(qv, kvv, n, bias_ref[...], sk, lo)
                ov[pl.ds(r0, BLOCK_Q), :] = _unstack_heads(_dot(prob.astype(BF16), v2), lo).astype(BF16)

            block(0, bias0)

            def later(n, carry):
                block(n, bias)
                return carry

            lax.fori_loop(1, n_blk, later, 0, unroll=2)
            stores[g].start()
        for cp in stores:
            cp.wait()

    return pl.pallas_call(
        body, name="attn_fwd", in_specs=[ANY, ANY, SMEM], out_specs=ANY,
        out_shape=jax.ShapeDtypeStruct((t, D_ATTN), BF16),
        scratch_shapes=[pltpu.VMEM((2, t, 256), BF16), pltpu.VMEM((2, t, 256), BF16), pltpu.VMEM((2, t, 256), BF16),
                        pltpu.VMEM((4 * BLOCK_Q, 2 * BLOCK_Q), F32), pltpu.VMEM((4 * BLOCK_Q, 2 * BLOCK_Q), F32),
                        pltpu.SemaphoreType.DMA((6,))],
        compiler_params=_params(),
    )(q, kv, sinks)


def _attn_bwd(q, kv, do, sinks, exchange=None):
    t = q.shape[0]
    n_blk = t // BLOCK_Q
    host = _Host(exchange)

    def body(*refs):
        own, ex_refs = host.split(refs, 4, 3, 9)
        q_hbm, kv_hbm, do_hbm, s_ref, dq_hbm, dkv_hbm, dsink_ref, q2, kv2, do2, dqv, dkvv, ds_acc, bias0, bias, sem = own
        host.phase(0, ex_refs)
        lo = lax.broadcasted_iota(jnp.int32, (BLOCK_Q, 128), 1) < HEAD_DIM
        loads = [[pltpu.make_async_copy(src.at[:, pl.ds(256 * g, 256)], dst.at[g], sem.at[3 * g + i])
                  for i, (src, dst) in enumerate(((q_hbm, q2), (kv_hbm, kv2), (do_hbm, do2)))] for g in range(2)]
        for cp in loads[0] + loads[1]:
            cp.start()
        bias0[...] = _attn_bias(True)
        bias[...] = _attn_bias(False)
        for g in range(2):
            cols = pl.ds(256 * g, 256)
            for cp in loads[g]:
                cp.wait()
            qv, kvv, dov = q2.at[g], kv2.at[g], do2.at[g]
            sk = _sink_column([s_ref[0, 4 * g + i] for i in range(4)])
            ds_acc[...] = jnp.zeros_like(ds_acc)

            def block(n, bias_ref, has_prev):
                r0, rp, qs, k2, v2, prob, psink = _attn_scores(qv, kvv, n, bias_ref[...], sk, lo)
                pb = prob.astype(BF16)
                dos = _stack_heads(dov[pl.ds(r0, BLOCK_Q), :], lo)
                dp = _dot_nt(dos, v2)
                dsum = jnp.sum(prob * dp, axis=-1, keepdims=True)
                dsb = (prob * (dp - dsum) * ATTN_SCALE).astype(BF16)
                ds_acc[...] -= psink * dsum
                dqv[pl.ds(r0, BLOCK_Q), :] = _unstack_heads(_dot(dsb, k2), lo).astype(BF16)
                dk2 = _dot_tn(dsb, qs)
                dv2 = _dot_tn(pb, dos)
                dkvv[pl.ds(r0, BLOCK_Q), :] = jnp.concatenate([dk2[BLOCK_Q:], dv2[BLOCK_Q:]], axis=1)
                if has_prev:
                    dkvv[pl.ds(rp, BLOCK_Q), :] += jnp.concatenate([dk2[:BLOCK_Q], dv2[:BLOCK_Q]], axis=1)

            block(0, bias0, False)

            def later(n, carry):
                block(n, bias, True)
                return carry

            lax.fori_loop(1, n_blk, later, 0, unroll=2)
            for i in range(4):
                tot = jnp.sum(ds_acc[BLOCK_Q * i:BLOCK_Q * (i + 1), :], axis=0, keepdims=True)
                dsink_ref[4 * g + i:4 * g + i + 1, :] = jnp.broadcast_to(tot, (1, 128))
            stores = [pltpu.make_async_copy(dqv, dq_hbm.at[:, cols], sem.at[6]),
                      pltpu.make_async_copy(dkvv, dkv_hbm.at[:, cols], sem.at[7])]
            for cp in stores:
                cp.start()
            for cp in stores:
                cp.wait()
        if exchange is not None:
            for p in range(1, exchange.n_phases):
                host.phase(p, ex_refs)

    res = pl.pallas_call(
        body, name="attn_bwd", in_specs=[ANY, ANY, ANY, SMEM] + host.in_specs,
        out_specs=[ANY, ANY, pl.BlockSpec(memory_space=pltpu.VMEM)] + host.out_specs,
        out_shape=[jax.ShapeDtypeStruct((t, D_ATTN), BF16), jax.ShapeDtypeStruct((t, 512), F32),
                   jax.ShapeDtypeStruct((8, 128), F32)] + host.out_shape,
        scratch_shapes=[pltpu.VMEM((2, t, 256), BF16), pltpu.VMEM((2, t, 256), BF16), pltpu.VMEM((2, t, 256), BF16),
                        pltpu.VMEM((t, 256), BF16), pltpu.VMEM((t, 256), F32), pltpu.VMEM((4 * BLOCK_Q, 1), F32),
                        pltpu.VMEM((4 * BLOCK_Q, 2 * BLOCK_Q), F32), pltpu.VMEM((4 * BLOCK_Q, 2 * BLOCK_Q), F32),
                        pltpu.SemaphoreType.DMA((8,))] + host.scratch,
        compiler_params=_params(),
    )(q, kv, do, sinks, *host.args)
    return (*res[:3], list(res[3:]))


def _mix_out_fwd(x1, h, gl, o, g_lru, g_attn, g_post, w_o):
    t = x1.shape[0]
    tm = _token_tile(t)

    def body(x_ref, h_ref, gl_ref, o_ref, g1_ref, g2_ref, gp_ref, w_ref, x2_ref, m_ref):
        y = h_ref[...] * _gelu(gl_ref[...])[0]
        yn1 = _rms_fwd(y, g1_ref[...]).astype(BF16)
        yn2 = _rms_fwd(o_ref[...].astype(F32), g2_ref[...]).astype(BF16)
        m = _dot(yn1, w_ref[0:512, :]) + _dot(yn2, w_ref[512:1024, :])
        m_ref[...] = m.astype(BF16)
        x2_ref[...] = x_ref[...] + _rms_fwd(m, gp_ref[...])

    tok = pl.BlockSpec((tm, D_MODEL), lambda i: (i, 0))
    half = pl.BlockSpec((tm, 512), lambda i: (i, 0))
    vec = pl.BlockSpec((1, D_MODEL), lambda i: (0, 0))
    hvec = pl.BlockSpec((1, 512), lambda i: (0, 0))
    return pl.pallas_call(
        body, name="mix_out_fwd", grid=(t // tm,),
        in_specs=[tok, half, half, half, hvec, hvec, vec, pl.BlockSpec((D_MODEL, D_MODEL), lambda i: (0, 0))],
        out_specs=[tok, tok],
        out_shape=[jax.ShapeDtypeStruct((t, D_MODEL), F32), jax.ShapeDtypeStruct((t, D_MODEL), BF16)],
        compiler_params=_params(1),
    )(x1, h, gl, o, g_lru, g_attn, g_post, w_o)


def _mix_out_bwd(dx2, m, h, gl, o, g_lru, g_attn, g_post, w_o):
    t = dx2.shape[0]
    tm = _token_tile(t)

    def body(dx_ref, m_ref, h_ref, gl_ref, o_ref, g1_ref, g2_ref, gp_ref, w_ref,
             dy_ref, do_ref, dw_ref, dgp_ref, dg1_ref, dg2_ref):
        _zero_at_first(pl.program_id(0) == 0, dw_ref, dgp_ref, dg1_ref, dg2_ref)
        dm, dgp = _rms_bwd(m_ref[...].astype(F32), gp_ref[...], dx_ref[...])
        dmb = dm.astype(BF16)
        y = h_ref[...] * _gelu(gl_ref[...])[0]
        o = o_ref[...].astype(F32)
        yn1 = _rms_fwd(y, g1_ref[...]).astype(BF16)
        yn2 = _rms_fwd(o, g2_ref[...]).astype(BF16)
        dw_ref[0:512, :] += _dot_tn(yn1, dmb)
        dw_ref[512:1024, :] += _dot_tn(yn2, dmb)
        dy, dg1 = _rms_bwd(y, g1_ref[...], _dot_nt(dmb, w_ref[0:512, :]))
        do, dg2 = _rms_bwd(o, g2_ref[...], _dot_nt(dmb, w_ref[512:1024, :]))
        dy_ref[...] = dy
        do_ref[...] = do.astype(BF16)
        dgp_ref[...] += dgp
        dg1_ref[...] += dg1
        dg2_ref[...] += dg2

    tok = pl.BlockSpec((tm, D_MODEL), lambda i: (i, 0))
    half = pl.BlockSpec((tm, 512), lambda i: (i, 0))
    vec = pl.BlockSpec((1, D_MODEL), lambda i: (0, 0))
    hvec = pl.BlockSpec((1, 512), lambda i: (0, 0))
    mat = pl.BlockSpec((D_MODEL, D_MODEL), lambda i: (0, 0))
    return pl.pallas_call(
        body, name="mix_out_bwd", grid=(t // tm,),
        in_specs=[tok, tok, half, half, half, hvec, hvec, vec, mat],
        out_specs=[half, half, mat, vec, hvec, hvec],
        out_shape=[jax.ShapeDtypeStruct((t, 512), F32), jax.ShapeDtypeStruct((t, 512), BF16),
                   jax.ShapeDtypeStruct((D_MODEL, D_MODEL), F32), jax.ShapeDtypeStruct((1, D_MODEL), F32),
                   jax.ShapeDtypeStruct((1, 512), F32), jax.ShapeDtypeStruct((1, 512), F32)],
        compiler_params=_params(1),
    )(dx2, m, h, gl, o, g_lru, g_attn, g_post, w_o)


def _mix_in_bwd(dx2, x1, g, dxl, dgl, dq, dkv, w_in, f1, g_post1):
    t = x1.shape[0]
    tm = _token_tile(t)

    def body(dx2_ref, x_ref, g_ref, dxl_ref, dgl_ref, dq_ref, dkv_ref, w_ref, f1_ref, gp1_ref,
             dx1_ref, dw_ref, dg_ref, df1_ref, dgp1_ref):
        _zero_at_first(pl.program_id(0) == 0, dw_ref, dg_ref, dgp1_ref)
        x = x_ref[...]
        nb = _rms_fwd(x, g_ref[...]).astype(BF16)
        lo = lax.broadcasted_iota(jnp.int32, (tm, 128), 1) < HEAD_DIM
        dkv = dkv_ref[...]
        folded = []
        for k in range(4):
            seg = dkv[:, 128 * k:128 * (k + 1)]
            folded.append(jnp.where(lo, seg + pltpu.roll(seg, HEAD_DIM, 1), 0.0).astype(BF16))
        dproj = jnp.concatenate([dxl_ref[...].astype(BF16), dgl_ref[...].astype(BF16), dq_ref[...]] + folded, axis=1)
        dw_ref[...] += _dot_tn(nb, dproj)
        dx, dg = _rms_bwd(x, g_ref[...], _dot(dproj, w_ref[...]))
        dx1 = dx2_ref[...] + dx
        dx1_ref[...] = dx1
        dg_ref[...] += dg
        df1, dgp1 = _rms_bwd(f1_ref[...], gp1_ref[...], 0.5 * dx1)
        df1_ref[...] = df1.astype(BF16)
        dgp1_ref[...] += dgp1

    tok = pl.BlockSpec((tm, D_MODEL), lambda i: (i, 0))
    half = pl.BlockSpec((tm, 512), lambda i: (i, 0))
    vec = pl.BlockSpec((1, D_MODEL), lambda i: (0, 0))
    mat = pl.BlockSpec((D_IN_DUP, D_MODEL), lambda i: (0, 0))
    dmat = pl.BlockSpec((D_MODEL, D_IN_DUP), lambda i: (0, 0))
    return pl.pallas_call(
        body, name="mix_in_bwd", grid=(t // tm,),
        in_specs=[tok, tok, vec, half, half, half, half, mat, tok, vec], out_specs=[tok, dmat, vec, tok, vec],
        out_shape=[jax.ShapeDtypeStruct((t, D_MODEL), F32), jax.ShapeDtypeStruct((D_MODEL, D_IN_DUP), F32),
                   jax.ShapeDtypeStruct((1, D_MODEL), F32), jax.ShapeDtypeStruct((t, D_MODEL), BF16),
                   jax.ShapeDtypeStruct((1, D_MODEL), F32)],
        compiler_params=_params(1),
    )(dx2, x1, g, dxl, dgl, dq, dkv, w_in, f1, g_post1)


def _row_tile(rows):
    return rows if rows <= 512 else rows // 2


def _chip_sum(grad, from_sibling, other, name):
    _, rows, cols = grad.shape
    tr = _row_tile(rows)

    def body(other_ref, g_ref, s_ref, out_ref):
        out_ref[0] = (g_ref[0, 0] + s_ref[0]).astype(BF16)

    grid_spec = pltpu.PrefetchScalarGridSpec(
        num_scalar_prefetch=1, grid=(3, rows // tr),
        in_specs=[pl.BlockSpec((1, 1, tr, cols), lambda j, i, other: (other[j], other[3], i, 0)),
                  pl.BlockSpec((1, tr, cols), lambda j, i, other: (other[j], i, 0))],
        out_specs=pl.BlockSpec((1, tr, cols), lambda j, i, other: (j, i, 0)))
    return pl.pallas_call(
        body, name=name, grid_spec=grid_spec, out_shape=jax.ShapeDtypeStruct((3, rows, cols), BF16),
        compiler_params=_params(2),
    )(other, grad.reshape(4, 2, rows, cols), from_sibling)


def _adamw(w, g, m, v):
    m = ADAM_B1 * m + (1.0 - ADAM_B1) * g
    v = ADAM_B2 * v + (1.0 - ADAM_B2) * (g * g)
    m_hat = m / (1.0 - ADAM_B1 ** ADAM_STEP)
    v_hat = v / (1.0 - ADAM_B2 ** ADAM_STEP)
    delta = -ADAM_LR * (m_hat / (jnp.sqrt(v_hat) + ADAM_EPS) + ADAM_WD * w)
    return delta, m, v


def _shard_update(grad, from_sibling, from_chips, w, m, v, place, name, transposed):
    _, rows, cols = grad.shape
    tr = _row_tile(rows)

    def total(g_ref, s_ref, c_ref):
        g = g_ref[0, 0] + s_ref[0]
        g = g + c_ref[0].astype(F32)
        g = g + c_ref[1].astype(F32)
        return g + c_ref[2].astype(F32)

    part_specs = [pl.BlockSpec((1, 1, tr, cols), lambda i, place: (place[0], place[1], i, 0)),
                  pl.BlockSpec((1, tr, cols), lambda i, place: (place[0], i, 0)),
                  pl.BlockSpec((3, tr, cols), lambda i, place: (0, i, 0))]
    flat = pl.BlockSpec((tr, cols), lambda i, place: (i, 0))
    parts = (place, grad.reshape(4, 2, rows, cols), from_sibling, from_chips)
    if not transposed:
        def body(place_ref, g_ref, s_ref, c_ref, w_ref, m_ref, v_ref, go_ref, d_ref, mo_ref, vo_ref):
            g = total(g_ref, s_ref, c_ref)
            go_ref[...] = g
            d_ref[...], mo_ref[...], vo_ref[...] = _adamw(w_ref[...], g, m_ref[...], v_ref[...])

        grid_spec = pltpu.PrefetchScalarGridSpec(num_scalar_prefetch=1, grid=(rows // tr,),
                                                 in_specs=part_specs + [flat, flat, flat], out_specs=[flat] * 4)
        return pl.pallas_call(body, name=name, grid_spec=grid_spec, out_shape=[jax.ShapeDtypeStruct((rows, cols), F32)] * 4,
                              compiler_params=_params(1))(*parts, w, m, v)

    def sum_body(place_ref, g_ref, s_ref, c_ref, go_ref):
        go_ref[...] = total(g_ref, s_ref, c_ref)

    grid_spec = pltpu.PrefetchScalarGridSpec(num_scalar_prefetch=1, grid=(rows // tr,), in_specs=part_specs, out_specs=flat)
    g = pl.pallas_call(sum_body, name=name + "_sum", grid_spec=grid_spec, out_shape=jax.ShapeDtypeStruct((rows, cols), F32),
                       compiler_params=_params(1))(*parts)
    gt = jnp.transpose(g, (1, 0))
    tc = _row_tile(cols)

    def adam_body(g_ref, w_ref, m_ref, v_ref, d_ref, mo_ref, vo_ref):
        d_ref[...], mo_ref[...], vo_ref[...] = _adamw(w_ref[...], g_ref[...], m_ref[...], v_ref[...])

    blk = pl.BlockSpec((tc, rows), lambda i: (i, 0))
    res = pl.pallas_call(adam_body, name=name + "_adam", grid=(cols // tc,), in_specs=[blk] * 4, out_specs=[blk] * 3,
                         out_shape=[jax.ShapeDtypeStruct((cols, rows), F32)] * 3, compiler_params=_params(1))(gt, w, m, v)
    return (gt, *res)


GAINS = ("ffn1_pre_g", "ffn1_post_g", "mix_pre_g", "mix_post_g", "ffn2_pre_g", "ffn2_post_g")
HALVES = ("conv_b", "b_rg", "b_ig", "lru_lambda", "g_lru_out", "g_attn_out")
GATES = ("w_rg", "w_ig")
SMALL = GAINS + HALVES + GATES + ("sinks", "conv_w")


def _small_update(gathered, w, m, v):
    n_small = len(SMALL)

    def body(*refs):
        ga_ref, gb_ref, gc_ref, gd_ref, g0_ref = refs[:5]
        wmv = refs[5:5 + 3 * n_small]
        outs = refs[5 + 3 * n_small:5 + 7 * n_small]
        loss_ref = refs[5 + 7 * n_small]

        def total(ref):
            s = ref[0]
            for d in range(1, N_DEV):
                s = s + ref[d]
            return s

        sa, sb, sc, sd = total(ga_ref), total(gb_ref), total(gc_ref), total(gd_ref)
        grads = {}
        for i, k in enumerate(GAINS):
            grads[k] = sa[i:i + 1]
        grads[GAINS[0]] = total(g0_ref)
        for i, k in enumerate(HALVES):
            grads[k] = sb[i:i + 1]
        grads["w_rg"], grads["w_ig"] = sc[0:512], sc[512:1024]
        grads["sinks"] = sd[4:5, 0:8]
        grads["conv_w"] = sd[0:4]
        for i, k in enumerate(SMALL):
            g = grads[k]
            outs[4 * i][...] = g
            outs[4 * i + 1][...], outs[4 * i + 2][...], outs[4 * i + 3][...] = _adamw(
                wmv[3 * i][...], g, wmv[3 * i + 1][...], wmv[3 * i + 2][...])
        loss_ref[...] = jnp.broadcast_to(sd[5:6, 0:128], loss_ref.shape)

    operands = list(gathered)
    out_shape = []
    for k in SMALL:
        operands += [w[k], m[k], v[k]]
        out_shape += [jax.ShapeDtypeStruct(w[k].shape, F32)] * 4
    out_shape.append(jax.ShapeDtypeStruct((8, 128), F32))
    res = pl.pallas_call(body, name="small_update", out_shape=out_shape, compiler_params=_params())(*operands)
    parts = [{k: res[4 * i + j] for i, k in enumerate(SMALL)} for j in range(4)]
    return (*parts, res[-1])


def _dup_in_rows(wt):
    k0, k1, v0, v1 = wt[1536:1600], wt[1600:1664], wt[1664:1728], wt[1728:1792]
    return jnp.concatenate([wt[:1536], k0, k0, v0, v0, k1, k1, v1, v1], axis=0)


def _undup_in_columns(dw):
    return jnp.concatenate([dw[:, :1536], dw[:, 1536:1600], dw[:, 1792:1856], dw[:, 1664:1728], dw[:, 1920:1984]], axis=1)


def _pair_block_diag(w):
    w = w.reshape(N_LRU_GROUP, 2, 64, 64)
    z = jnp.zeros((N_LRU_GROUP, 64, 64), w.dtype)
    top = jnp.concatenate([w[:, 0], z], axis=2)
    bot = jnp.concatenate([z, w[:, 1]], axis=2)
    return jnp.concatenate([top, bot], axis=1)


def _pair_block_diag_grad(dw2):
    return jnp.stack([dw2[:, :64, :64], dw2[:, 64:, 64:]], axis=1).reshape(512, 64)


def kernel(x, ffn1_pre_g, ffn1_w_gu, ffn1_w_down, ffn1_post_g, mix_pre_g, w_in, conv_w, conv_b, w_rg, b_rg, w_ig, b_ig, lru_lambda, sinks, g_lru_out, g_attn_out, w_o, mix_post_g, ffn2_pre_g, ffn2_w_gu, ffn2_w_down, ffn2_post_g, loss_target, m_ffn1_pre_g, m_ffn1_w_gu, m_ffn1_w_down, m_ffn1_post_g, m_mix_pre_g, m_w_in, m_conv_w, m_conv_b, m_w_rg, m_b_rg, m_w_ig, m_b_ig, m_lru_lambda, m_sinks, m_g_lru_out, m_g_attn_out, m_w_o, m_mix_post_g, m_ffn2_pre_g, m_ffn2_w_gu, m_ffn2_w_down, m_ffn2_post_g, v_ffn1_pre_g, v_ffn1_w_gu, v_ffn1_w_down, v_ffn1_post_g, v_mix_pre_g, v_w_in, v_conv_w, v_conv_b, v_w_rg, v_b_rg, v_w_ig, v_b_ig, v_lru_lambda, v_sinks, v_g_lru_out, v_g_attn_out, v_w_o, v_mix_post_g, v_ffn2_pre_g, v_ffn2_w_gu, v_ffn2_w_down, v_ffn2_post_g):
    args = dict(locals())
    names = ["ffn1_pre_g", "ffn1_w_gu", "ffn1_w_down", "ffn1_post_g", "mix_pre_g", "w_in", "conv_w", "conv_b", "w_rg",
             "b_rg", "w_ig", "b_ig", "lru_lambda", "sinks", "g_lru_out", "g_attn_out", "w_o", "mix_post_g",
             "ffn2_pre_g", "ffn2_w_gu", "ffn2_w_down", "ffn2_post_g"]
    big = ["ffn1_w_gu", "ffn1_w_down", "w_in", "w_o", "ffn2_w_gu", "ffn2_w_down"]
    w = {k: args[k] for k in names}
    mom = {k: args["m_" + k] for k in names}
    var = {k: args["v_" + k] for k in names}
    t = x.shape[1]
    xs = x.reshape(t, D_MODEL)
    target = loss_target.reshape(t, D_MODEL)
    cx, cy, cc = _coords()
    me = 4 * cx + 2 * cy + cc
    other = jnp.stack([2 * (1 - cx) + cy, 2 * cx + (1 - cy), 2 * (1 - cx) + (1 - cy), cc]).astype(jnp.int32)
    place = jnp.stack([2 * cx + cy, cc]).astype(jnp.int32)

    transposed = ("ffn1_w_gu", "w_in", "ffn2_w_gu")

    def shard_view(a, k):
        return jnp.transpose(a[0], (1, 0)) if k in transposed else a[0]

    def shard_unview(a, k):
        return (jnp.transpose(a, (1, 0)) if k in transposed else a)[None]

    shard2d = {k: shard_view(w[k], k) for k in big}
    shard_bf = {k: shard2d[k].astype(BF16) for k in big}
    conv_pad = jnp.pad(conv_w.reshape(4, 64), ((0, 4), (0, 64)))
    (first_w,) = _run_exchanges([_Gather([shard_bf["ffn1_w_gu"], shard_bf["ffn1_w_down"]], routed=True)], "all_gather_ffn1")
    wgu1 = first_w[0].reshape(2, N_CHUNK, CHUNK, D_MODEL)
    wd1 = first_w[1].reshape(N_CHUNK, CHUNK, D_MODEL)
    rest = _Gather([shard_bf["w_in"], shard_bf["w_o"], shard_bf["ffn2_w_gu"], shard_bf["ffn2_w_down"], conv_pad])

    x1, f1, n1, gu1, gathered = _ffn_fwd(xs, ffn1_pre_g, wgu1, wd1, ffn1_post_g, None, "ffn1_fwd", rest)
    w_in_full = _dup_in_rows(gathered[0].reshape(D_IN, D_MODEL))
    w_o_full = gathered[1].reshape(D_MODEL, D_MODEL)
    wgu2 = gathered[2].reshape(2, N_CHUNK, CHUNK, D_MODEL)
    wd2 = gathered[3].reshape(N_CHUNK, CHUNK, D_MODEL)
    conv_w_full = jnp.transpose(gathered[4][:, 0:4, 0:64], (1, 0, 2)).reshape(4, D_LRU)
    p_lru = jnp.concatenate([conv_b, b_rg, b_ig, lru_lambda, conv_w_full], axis=0)
    wrg2 = _pair_block_diag(w_rg[0]).astype(BF16)
    wig2 = _pair_block_diag(w_ig[0]).astype(BF16)
    xl, gl, q, kv = _mix_in_fwd(x1, mix_pre_g, w_in_full)
    h = _lru_fwd(xl, p_lru, wrg2, wig2)
    o = _attn_fwd(q, kv, sinks)
    x2, mo = _mix_out_fwd(x1, h, gl, o, g_lru_out, g_attn_out, mix_post_g, w_o_full)
    g = {}
    dx3, n2, df2, gu2, g["ffn2_post_g"], loss_parts, _ = _ffn_fwd(x2, ffn2_pre_g, wgu2, wd2, ffn2_post_g, target, "ffn2_fwd")
    loss_local = jnp.sum(loss_parts[::8, 0])

    partial, from_sibling, from_chips = {}, {}, {}

    def chip_sums(keys):
        return [_chip_sum(partial[k], from_sibling[k], other, "chip_sum_" + k) for k in keys]

    dgu2, dwgu2, dwd2, _ = _ffn_bwd_w(n2, df2, gu2, wd2, "ffn2_bwd_w")
    partial["ffn2_w_gu"] = dwgu2.reshape(N_DEV, D_MODEL, CHUNK)
    partial["ffn2_w_down"] = dwd2.reshape(N_DEV, D_FF // N_DEV, D_MODEL)
    ffn2_keys = ["ffn2_w_gu", "ffn2_w_down"]
    dx2, g["ffn2_pre_g"], got = _ffn_bwd_x(dgu2, wgu2, x2, ffn2_pre_g, dx3, "ffn2_bwd_x",
                                           _SiblingExchange([partial[k] for k in ffn2_keys]))
    from_sibling.update(zip(ffn2_keys, got))
    dy, do, dwo, g["mix_post_g"], g["g_lru_out"], g["g_attn_out"] = _mix_out_bwd(
        dx2, mo, h, gl, o, g_lru_out, g_attn_out, mix_post_g, w_o_full)
    dq, dkv, dsink, got = _attn_bwd(q, kv, do, sinks, _ChipExchange(chip_sums(ffn2_keys)))
    from_chips.update(zip(ffn2_keys, got))
    dxl, dgl, dp, dwrg2, dwig2 = _lru_bwd(dy, h, xl, gl, p_lru, wrg2, wig2)
    dx1, dwin_dup, g["mix_pre_g"], df1, g["ffn1_post_g"] = _mix_in_bwd(
        dx2, x1, mix_pre_g, dxl, dgl, dq, dkv, w_in_full, f1, ffn1_post_g)
    partial["w_in"] = jnp.transpose(_undup_in_columns(dwin_dup).reshape(D_MODEL, N_DEV, D_IN // N_DEV), (1, 0, 2))
    partial["w_o"] = dwo.reshape(N_DEV, D_MODEL // N_DEV, D_MODEL)
    mix_keys = ["w_in", "w_o"]
    (got,) = _run_exchanges([_SiblingExchange([partial[k] for k in mix_keys])], "mix_sibling_exchange")
    from_sibling.update(zip(mix_keys, got))
    dgu1, dwgu1, dwd1, got = _ffn_bwd_w(n1, df1, gu1, wd1, "ffn1_bwd_w", _ChipExchange(chip_sums(mix_keys)))
    from_chips.update(zip(mix_keys, got))
    partial["ffn1_w_gu"] = dwgu1.reshape(N_DEV, D_MODEL, CHUNK)
    partial["ffn1_w_down"] = dwd1.reshape(N_DEV, D_FF // N_DEV, D_MODEL)
    ffn1_keys = ["ffn1_w_gu", "ffn1_w_down"]
    (got,) = _run_exchanges([_SiblingExchange([partial[k] for k in ffn1_keys])], "ffn1_sibling_exchange")
    from_sibling.update(zip(ffn1_keys, got))
    zeros2 = jnp.zeros((2, D_MODEL), F32)
    g_gains = jnp.concatenate([zeros2[:1]] + [g[k] for k in GAINS[1:]] + [zeros2], axis=0)
    g_halves = jnp.concatenate([dp[0:4], g["g_lru_out"], g["g_attn_out"], zeros2[:, :D_LRU]], axis=0)
    g_gates = jnp.concatenate([_pair_block_diag_grad(dwrg2), _pair_block_diag_grad(dwig2)], axis=0)
    g_misc = jnp.concatenate([dp[4:8], jnp.pad(dsink[:, 0].reshape(1, 8), ((0, 0), (0, D_LRU - 8))),
                              jnp.pad(loss_local.reshape(1, 1), ((0, 0), (0, D_LRU - 1))), zeros2[:, :D_LRU]], axis=0)
    dx0, g_first, got = _ffn_bwd_x(dgu1, wgu1, xs, ffn1_pre_g, dx1, "ffn1_bwd_x",
                                   _Both(_ChipExchange(chip_sums(ffn1_keys)), _Gather([g_gains, g_halves, g_gates, g_misc])))
    from_chips.update(zip(ffn1_keys, got[:2]))
    gathered_small = got[2:]

    grads, delta, new_m, new_v = {}, {}, {}, {}
    for k in big:
        res = _shard_update(partial[k], from_sibling[k], from_chips[k], shard2d[k], shard_view(mom[k], k),
                            shard_view(var[k], k), place, "update_" + k, k in transposed)
        grads[k], delta[k], new_m[k], new_v[k] = [shard_unview(r, k) for r in res]

    ((gathered_first,),) = _run_exchanges([_Gather([g_first])], "all_gather_first_gain")
    col = me * 64

    def small_view(vals):
        out = {k: vals[k] for k in GAINS + HALVES + ("sinks",)}
        out.update({k: vals[k].reshape(512, 64) for k in GATES})
        out["conv_w"] = lax.dynamic_update_slice(jnp.zeros((4, D_LRU), F32), vals["conv_w"].reshape(4, 64), (0, col))
        return out

    *small, loss_tile = _small_update([*gathered_small, gathered_first], small_view(w), small_view(mom), small_view(var))
    for dst, part in zip((grads, delta, new_m, new_v), small):
        for k in SMALL:
            if k == "conv_w":
                dst[k] = lax.dynamic_slice(part[k], (0, col), (4, 64)).reshape(conv_w.shape)
            else:
                dst[k] = part[k].reshape(w[k].shape)
    return (loss_tile[0, 0], dx0.reshape(x.shape), *[grads[k] for k in names], *[delta[k] for k in names],
            *[new_m[k] for k in names], *[new_v[k] for k in names])
```

```python
import functools

import jax
import jax.numpy as jnp
from jax import lax
from jax.experimental import pallas as pl
from jax.experimental.pallas import tpu as pltpu

F32 = jnp.float32
BF16 = jnp.bfloat16

D_MODEL = 1024
D_FF = 2816
N_DEV = 8
N_CHUNK = 4
CHUNK = D_FF // N_CHUNK
D_LRU = 512
D_ATTN = 512
LRU_GROUP = 128
N_LRU_GROUP = D_LRU // LRU_GROUP
HEAD_DIM = 64
BLOCK_Q = 128
D_IN = 1792
HEAD_ORDER = (0, 4, 1, 5, 2, 6, 3, 7)
RMS_EPS = 1e-6
LRU_C = 8.0
MASK_VALUE = -1e30
ATTN_SCALE = HEAD_DIM ** -0.5

ADAM_LR = 0.001
ADAM_B1 = 0.9
ADAM_B2 = 0.999
ADAM_EPS = 1e-08
ADAM_WD = 0.01
ADAM_STEP = 10

VMEM_LIMIT_V7X = 56 * 2 ** 20

ANY = pl.BlockSpec(memory_space=pl.ANY)
SMEM = pl.BlockSpec(memory_space=pltpu.SMEM)
MESH = pl.DeviceIdType.MESH


def _params(n_grid=0):
    sem = ("arbitrary",) * n_grid if n_grid else None
    return pltpu.CompilerParams(dimension_semantics=sem, vmem_limit_bytes=VMEM_LIMIT_V7X)


def _dot(a, b):
    return lax.dot_general(a, b, (((1,), (0,)), ((), ())), preferred_element_type=F32)


def _dot_nt(a, b):
    return lax.dot_general(a, b, (((1,), (1,)), ((), ())), preferred_element_type=F32)


def _dot_tn(a, b):
    return lax.dot_general(a, b, (((0,), (0,)), ((), ())), preferred_element_type=F32)


def _sigmoid(x):
    return 1.0 / (1.0 + jnp.exp(-x))


def _rms_fwd(x, g):
    r = lax.rsqrt(jnp.mean(x * x, axis=-1, keepdims=True) + RMS_EPS)
    return x * r * g


def _rms_bwd(x, g, dy):
    r = lax.rsqrt(jnp.mean(x * x, axis=-1, keepdims=True) + RMS_EPS)
    xh = x * r
    dg = jnp.sum(dy * xh, axis=0, keepdims=True)
    dxh = dy * g
    dx = r * (dxh - xh * jnp.mean(dxh * xh, axis=-1, keepdims=True))
    return dx, dg


def _gelu(x):
    c = 0.7978845608028654
    inner = c * (x + 0.044715 * x * x * x)
    th = jnp.tanh(inner)
    ge = 0.5 * x * (1.0 + th)
    dge = 0.5 * (1.0 + th) + 0.5 * x * (1.0 - th * th) * c * (1.0 + 3.0 * 0.044715 * x * x)
    return ge, dge


def _zero_at_first(first, *refs):
    @pl.when(first)
    def _():
        for ref in refs:
            ref[...] = jnp.zeros_like(ref)


def _token_tile(t):
    return 512 if t >= 2048 else t // 2


def _ffn_bwd_tile(t):
    return 1024 if t >= 4096 else t // 2


def _coords():
    return lax.axis_index("x"), lax.axis_index("y"), lax.axis_index("c")


class _Gather:
    n_phases = 3
    at = (0.0, 0.8, 1.0)

    def __init__(self, shards, routed=False):
        k = len(shards)
        self.routed = routed
        self.arrays = list(shards)
        self.out_shape = [jax.ShapeDtypeStruct((N_DEV,) + s.shape, s.dtype) for s in shards]
        self.scratch = [pltpu.SemaphoreType.DMA((7 * k,)), pltpu.SemaphoreType.DMA((7 * k,)), pltpu.SemaphoreType.DMA((k,))]

    def run(self, phase, ins, outs, sems):
        send_sems, recv_sems, local_sems = sems
        k_arr = len(ins)
        x, y, c = _coords()
        me, sibling = (x, y, c), (x, y, 1 - c)
        chips = [(1 - x, y), (x, 1 - y), (1 - x, 1 - y)]
        direct = 2 if self.routed else 3
        relay_from = (x + (1 - c) * (1 - 2 * x), y + c * (1 - 2 * y))
        relay_to = (x + c * (1 - 2 * x), y + (1 - c) * (1 - 2 * y))

        def rows(k, dev):
            return outs[k].at[4 * dev[0] + 2 * dev[1] + dev[2]]

        def copy(k, slot, block, to, src=None):
            return pltpu.make_async_remote_copy(
                src_ref=rows(k, block) if src is None else src, dst_ref=rows(k, block),
                send_sem=send_sems.at[7 * k + slot], recv_sem=recv_sems.at[7 * k + slot],
                device_id=to, device_id_type=MESH)

        def mine():
            return [pltpu.make_async_copy(ins[k], rows(k, me), local_sems.at[k]) for k in range(k_arr)]

        def first():
            return [copy(k, slot, me, to, src=ins[k]) for k in range(k_arr)
                    for slot, to in enumerate([sibling] + [(*chip, c) for chip in chips[:direct]])]

        def relayed(k):
            return copy(k, 3, (*relay_from, c), (*relay_to, c))

        def passed(j, k):
            return copy(k, 4 + j, (*chips[j], c), sibling)

        if phase == 0:
            for cp in mine() + first():
                cp.start()
        elif phase == 1:
            for j in range(direct):
                for k in range(k_arr):
                    copy(k, 1 + j, (*chips[j], c), me).wait_recv()
            for k in range(k_arr):
                if self.routed:
                    relayed(k).start()
                for j in range(direct):
                    passed(j, k).start()
        else:
            for k in range(k_arr):
                if self.routed:
                    copy(k, 3, (*chips[2], c), me).wait_recv()
                    passed(2, k).start()
            for k in range(k_arr):
                copy(k, 0, sibling, me).wait_recv()
                for j, chip in enumerate(chips):
                    copy(k, 4 + j, (*chip, 1 - c), me).wait_recv()
            sent = first() + [passed(j, k) for j in range(3) for k in range(k_arr)]
            if self.routed:
                sent += [relayed(k) for k in range(k_arr)]
            for cp in sent:
                cp.wait_send()
            for cp in mine():
                cp.wait()


class _SiblingExchange:
    n_phases = 2
    at = (0.0, 1.0)

    def __init__(self, grads):
        k = len(grads)
        self.arrays = list(grads)
        self.out_shape = [jax.ShapeDtypeStruct((4,) + g.shape[1:], g.dtype) for g in grads]
        self.scratch = [pltpu.SemaphoreType.DMA((4 * k,)), pltpu.SemaphoreType.DMA((4 * k,))]

    def run(self, phase, ins, outs, sems):
        send_sems, recv_sems = sems
        x, y, c = _coords()
        copies = [pltpu.make_async_remote_copy(
            src_ref=ins[k].at[2 * q + (1 - c)], dst_ref=outs[k].at[q],
            send_sem=send_sems.at[4 * k + q], recv_sem=recv_sems.at[4 * k + q],
            device_id=(x, y, 1 - c), device_id_type=MESH) for k in range(len(ins)) for q in range(4)]
        for cp in copies:
            if phase == 0:
                cp.start()
            else:
                cp.wait_recv()
                cp.wait_send()


class _ChipExchange:
    n_phases = 2
    at = (0.0, 1.0)

    def __init__(self, chip_sums):
        k = len(chip_sums)
        self.arrays = list(chip_sums)
        self.out_shape = [jax.ShapeDtypeStruct((3,) + s.shape[1:], s.dtype) for s in chip_sums]
        self.scratch = [pltpu.SemaphoreType.DMA((3 * k,)), pltpu.SemaphoreType.DMA((3 * k,))]

    def run(self, phase, ins, outs, sems):
        send_sems, recv_sems = sems
        x, y, c = _coords()
        chips = [(1 - x, y), (x, 1 - y), (1 - x, 1 - y)]
        copies = [pltpu.make_async_remote_copy(
            src_ref=ins[k].at[j], dst_ref=outs[k].at[j],
            send_sem=send_sems.at[3 * k + j], recv_sem=recv_sems.at[3 * k + j],
            device_id=(*chip, c), device_id_type=MESH) for k in range(len(ins)) for j, chip in enumerate(chips)]
        for cp in copies:
            if phase == 0:
                cp.start()
            else:
                cp.wait_recv()
                cp.wait_send()


class _Both:
    n_phases = 3
    at = (0.0, 0.95, 1.0)

    def __init__(self, two_phase, gather):
        self.parts = (two_phase, gather)
        self.arrays = two_phase.arrays + gather.arrays
        self.out_shape = two_phase.out_shape + gather.out_shape
        self.scratch = two_phase.scratch + gather.scratch

    def run(self, phase, ins, outs, sems):
        a, b = self.parts
        n_in, n_out, n_sem = len(a.arrays), len(a.out_shape), len(a.scratch)
        refs_a = (ins[:n_in], outs[:n_out], sems[:n_sem])
        refs_b = (ins[n_in:], outs[n_out:], sems[n_sem:])
        b.run(phase, *refs_b)
        if phase == 0:
            a.run(0, *refs_a)
        if phase == 2:
            a.run(1, *refs_a)


class _Host:
    def __init__(self, exchange):
        self.ex = exchange
        self.args = [] if exchange is None else exchange.arrays
        self.in_specs = [ANY] * len(self.args)
        self.out_shape = [] if exchange is None else exchange.out_shape
        self.out_specs = [ANY] * len(self.out_shape)
        self.scratch = [] if exchange is None else exchange.scratch

    def split(self, refs, n_in, n_out, n_scratch):
        a, b, s = len(self.args), len(self.out_shape), len(self.scratch)
        own_in, ex_in = refs[:n_in], refs[n_in:n_in + a]
        rest = refs[n_in + a:]
        own_out, ex_out = rest[:n_out], rest[n_out:n_out + b]
        rest = rest[n_out + b:]
        own_scratch, ex_sems = rest[:n_scratch], rest[n_scratch:n_scratch + s]
        return list(own_in) + list(own_out) + list(own_scratch), (ex_in, ex_out, ex_sems)

    def at_steps(self, step, n_steps, ex_refs):
        if self.ex is None:
            return
        for p in range(self.ex.n_phases):
            pl.when(step == int(round(self.ex.at[p] * (n_steps - 1))))(functools.partial(self.ex.run, p, *ex_refs))

    def phase(self, p, ex_refs):
        if self.ex is not None:
            self.ex.run(p, *ex_refs)


def _run_exchanges(exchanges, name):
    hosts = [_Host(ex) for ex in exchanges]
    n_in = [len(h.args) for h in hosts]
    n_out = [len(h.out_shape) for h in hosts]
    n_sc = [len(h.scratch) for h in hosts]

    def body(*refs):
        ins, outs, scr = refs[:sum(n_in)], refs[sum(n_in):sum(n_in) + sum(n_out)], refs[sum(n_in) + sum(n_out):]
        parts = []
        for e in range(len(hosts)):
            parts.append((ins[sum(n_in[:e]):sum(n_in[:e + 1])], outs[sum(n_out[:e]):sum(n_out[:e + 1])],
                          scr[sum(n_sc[:e]):sum(n_sc[:e + 1])]))
        for h, part in zip(hosts, parts):
            h.phase(0, part)
        for h, part in zip(hosts, parts):
            for p in range(1, h.ex.n_phases):
                h.phase(p, part)

    res = pl.pallas_call(
        body, name=name, in_specs=[ANY] * sum(n_in), out_specs=[ANY] * sum(n_out),
        out_shape=[s for h in hosts for s in h.out_shape], scratch_shapes=[s for h in hosts for s in h.scratch],
    )(*[a for h in hosts for a in h.args])
    return [res[sum(n_out[:e]):sum(n_out[:e + 1])] for e in range(len(hosts))]


def _ffn_fwd(x, g_pre, wgu, wd, g_post, target, name, exchange=None):
    t = x.shape[0]
    tm = _token_tile(t)
    n_i = t // tm
    with_loss = target is not None
    host = _Host(exchange)
    n_in, n_out = (6, 6) if with_loss else (5, 4)

    def body(*refs):
        own, ex_refs = host.split(refs, n_in, n_out, 0)
        if with_loss:
            x_ref, gpre_ref, wgu_ref, wd_ref, gpost_ref, tgt_ref, xo_ref, n_ref, df_ref, gu_ref, dgpost_ref, loss_ref = own
            _zero_at_first(pl.program_id(0) == 0, dgpost_ref)
        else:
            x_ref, gpre_ref, wgu_ref, wd_ref, gpost_ref, xo_ref, f_ref, n_ref, gu_ref = own
        host.at_steps(pl.program_id(0), n_i, ex_refs)
        x = x_ref[...]
        n = _rms_fwd(x, gpre_ref[...]).astype(BF16)
        n_ref[...] = n
        f = None
        for j in range(N_CHUNK):
            gate = _dot_nt(n, wgu_ref[0, j])
            up = _dot_nt(n, wgu_ref[1, j])
            gu_ref[0, j] = gate.astype(BF16)
            gu_ref[1, j] = up.astype(BF16)
            part = _dot((gate * _sigmoid(gate) * up).astype(BF16), wd_ref[j])
            f = part if f is None else f + part
        xo = x + 0.5 * _rms_fwd(f, gpost_ref[...])
        if with_loss:
            err = xo - tgt_ref[...]
            d_out = err * (1.0 / D_MODEL)
            xo_ref[...] = d_out
            df, dg = _rms_bwd(f, gpost_ref[...], 0.5 * d_out)
            df_ref[...] = df.astype(BF16)
            dgpost_ref[...] += dg
            part = 0.5 * jnp.sum(jnp.sum(err * err, axis=-1, keepdims=True) * (1.0 / D_MODEL), axis=0, keepdims=True)
            loss_ref[...] = jnp.broadcast_to(part, loss_ref.shape)
        else:
            f_ref[...] = f
            xo_ref[...] = xo

    tok = pl.BlockSpec((tm, D_MODEL), lambda i: (i, 0))
    vec = pl.BlockSpec((1, D_MODEL), lambda i: (0, 0))
    act = pl.BlockSpec((2, N_CHUNK, tm, CHUNK), lambda i: (0, 0, i, 0))
    tok_f32 = jax.ShapeDtypeStruct((t, D_MODEL), F32)
    tok_bf16 = jax.ShapeDtypeStruct((t, D_MODEL), BF16)
    act_shape = jax.ShapeDtypeStruct((2, N_CHUNK, t, CHUNK), BF16)
    in_specs = [tok, vec,
                pl.BlockSpec((2, N_CHUNK, CHUNK, D_MODEL), lambda i: (0, 0, 0, 0), pipeline_mode=pl.Buffered(1)),
                pl.BlockSpec((N_CHUNK, CHUNK, D_MODEL), lambda i: (0, 0, 0), pipeline_mode=pl.Buffered(1)),
                vec]
    args = [x, g_pre, wgu, wd, g_post]
    if with_loss:
        in_specs.append(tok)
        args.append(target)
        out_shape = [tok_f32, tok_bf16, tok_bf16, act_shape, jax.ShapeDtypeStruct((1, D_MODEL), F32),
                     jax.ShapeDtypeStruct((n_i * 8, 128), F32)]
        out_specs = [tok, tok, tok, act, vec, pl.BlockSpec((8, 128), lambda i: (i, 0))]
    else:
        out_shape = [tok_f32, tok_f32, tok_bf16, act_shape]
        out_specs = [tok, tok, tok, act]
    res = pl.pallas_call(
        body, name=name, grid=(n_i,), in_specs=in_specs + host.in_specs, out_specs=out_specs + host.out_specs,
        out_shape=out_shape + host.out_shape, scratch_shapes=host.scratch, compiler_params=_params(1),
    )(*args, *host.args)
    return (*res[:n_out], list(res[n_out:]))


def _ffn_bwd_w(n, df, gu, wd, name, exchange=None):
    t = n.shape[0]
    tm = _ffn_bwd_tile(t)
    n_i = t // tm
    host = _Host(exchange)

    def body(*refs):
        (n_ref, df_ref, gu_ref, wd_ref, dgu_ref, dwgu_ref, dwd_ref), ex_refs = host.split(refs, 4, 3, 0)
        i = pl.program_id(1)
        host.at_steps(pl.program_id(0) * n_i + i, N_CHUNK * n_i, ex_refs)
        _zero_at_first(i == 0, dwgu_ref, dwd_ref)
        nb = n_ref[...]
        dfb = df_ref[...]
        gate = gu_ref[0, 0].astype(F32)
        up = gu_ref[1, 0].astype(F32)
        s = _sigmoid(gate)
        silu = gate * s
        a = (silu * up).astype(BF16)
        da = _dot_nt(dfb, wd_ref[0])
        dup = (da * silu).astype(BF16)
        dgate = (da * up * (s * (1.0 + gate * (1.0 - s)))).astype(BF16)
        dgu_ref[0, 0] = dgate
        dgu_ref[1, 0] = dup
        dwgu_ref[0, 0] += _dot_tn(nb, dgate)
        dwgu_ref[1, 0] += _dot_tn(nb, dup)
        dwd_ref[0] += _dot_tn(a, dfb)

    tok = pl.BlockSpec((tm, D_MODEL), lambda j, i: (i, 0))
    act = pl.BlockSpec((2, 1, tm, CHUNK), lambda j, i: (0, j, i, 0))
    wgu_spec = pl.BlockSpec((2, 1, D_MODEL, CHUNK), lambda j, i: (0, j, 0, 0))
    wd_spec = pl.BlockSpec((1, CHUNK, D_MODEL), lambda j, i: (j, 0, 0))
    res = pl.pallas_call(
        body, name=name, grid=(N_CHUNK, n_i),
        in_specs=[tok, tok, act, wd_spec] + host.in_specs,
        out_specs=[act, wgu_spec, wd_spec] + host.out_specs,
        out_shape=[jax.ShapeDtypeStruct((2, N_CHUNK, t, CHUNK), BF16),
                   jax.ShapeDtypeStruct((2, N_CHUNK, D_MODEL, CHUNK), F32),
                   jax.ShapeDtypeStruct((N_CHUNK, CHUNK, D_MODEL), F32)] + host.out_shape,
        scratch_shapes=host.scratch, compiler_params=_params(2),
    )(n, df, gu, wd, *host.args)
    return (*res[:3], list(res[3:]))


def _ffn_bwd_x(dgu, wgu, x, g_pre, d_out, name, exchange=None):
    t = x.shape[0]
    tm = _token_tile(t)
    n_i = t // tm
    host = _Host(exchange)

    def body(*refs):
        (dgu_ref, wgu_ref, x_ref, gpre_ref, do_ref, dx_ref, dgpre_ref), ex_refs = host.split(refs, 5, 2, 0)
        i = pl.program_id(0)
        host.at_steps(i, n_i, ex_refs)
        _zero_at_first(i == 0, dgpre_ref)
        dn = _dot(dgu_ref[0, 0], wgu_ref[0, 0]) + _dot(dgu_ref[1, 0], wgu_ref[1, 0])
        for j in range(1, N_CHUNK):
            dn = dn + _dot(dgu_ref[0, j], wgu_ref[0, j]) + _dot(dgu_ref[1, j], wgu_ref[1, j])
        dx, dg = _rms_bwd(x_ref[...], gpre_ref[...], dn)
        dx_ref[...] = do_ref[...] + dx
        dgpre_ref[...] += dg

    tok = pl.BlockSpec((tm, D_MODEL), lambda i: (i, 0))
    vec = pl.BlockSpec((1, D_MODEL), lambda i: (0, 0))
    res = pl.pallas_call(
        body, name=name, grid=(n_i,),
        in_specs=[pl.BlockSpec((2, N_CHUNK, tm, CHUNK), lambda i: (0, 0, i, 0)),
                  pl.BlockSpec((2, N_CHUNK, CHUNK, D_MODEL), lambda i: (0, 0, 0, 0), pipeline_mode=pl.Buffered(1)),
                  tok, vec, tok] + host.in_specs,
        out_specs=[tok, vec] + host.out_specs,
        out_shape=[jax.ShapeDtypeStruct((t, D_MODEL), F32), jax.ShapeDtypeStruct((1, D_MODEL), F32)] + host.out_shape,
        scratch_shapes=host.scratch, compiler_params=_params(1),
    )(dgu, wgu, x, g_pre, d_out, *host.args)
    return (*res[:2], list(res[2:]))


def _mix_in_fwd(x1, g, w_in):
    t = x1.shape[0]
    tm = _token_tile(t)

    def body(x_ref, g_ref, w_ref, xl_ref, gl_ref, q_ref, kv_ref):
        n = _rms_fwd(x_ref[...], g_ref[...]).astype(BF16)
        proj = _dot_nt(n, w_ref[...])
        xl_ref[...] = proj[:, 0:512]
        gl_ref[...] = proj[:, 512:1024]
        q_ref[...] = proj[:, 1024:1536].astype(BF16)
        kv_ref[...] = proj[:, 1536:1792].astype(BF16)

    tok = pl.BlockSpec((tm, D_MODEL), lambda i: (i, 0))
    half = pl.BlockSpec((tm, 512), lambda i: (i, 0))
    return pl.pallas_call(
        body, name="mix_in_fwd", grid=(t // tm,),
        in_specs=[tok, pl.BlockSpec((1, D_MODEL), lambda i: (0, 0)), pl.BlockSpec((D_IN, D_MODEL), lambda i: (0, 0))],
        out_specs=[half, half, half, pl.BlockSpec((tm, 256), lambda i: (i, 0))],
        out_shape=[jax.ShapeDtypeStruct((t, 512), F32), jax.ShapeDtypeStruct((t, 512), F32),
                   jax.ShapeDtypeStruct((t, 512), BF16), jax.ShapeDtypeStruct((t, 256), BF16)],
        compiler_params=_params(1),
    )(x1, g, w_in)


def _shift_down(x, before, s):
    if s == 0:
        return x
    rolled = pltpu.roll(x, s, 0)
    ext = jnp.concatenate([before, x[0:8]], axis=0)
    first8 = pltpu.roll(ext, s, 0)[8:16]
    return jnp.concatenate([first8, rolled[8:]], axis=0)


def _shift_up(x, after, s):
    if s == 0:
        return x
    rows = x.shape[0]
    rolled = pltpu.roll(x, rows - s, 0)
    ext = jnp.concatenate([x[rows - 8:rows], after], axis=0)
    last8 = pltpu.roll(ext, 16 - s, 0)[0:8]
    return jnp.concatenate([rolled[:rows - 8], last8], axis=0)


def _log_sigmoid(x):
    e = jnp.exp(-jnp.abs(x))
    log1p_e = jnp.where(e < 0.01, e * (1.0 - e * (0.5 - e * (1.0 / 3.0))), jnp.log(1.0 + e))
    return jnp.minimum(x, 0.0) - log1p_e


def _lru_gates(xc, p_ref, wrg, wig):
    xcb = xc.astype(BF16)
    r = _sigmoid(_dot(xcb, wrg) + p_ref[1:2, :])
    ig = _sigmoid(_dot(xcb, wig) + p_ref[2:3, :])
    ls = _log_sigmoid(p_ref[3:4, :])
    log_a = LRU_C * r * ls
    a = jnp.exp(log_a)
    mult = jnp.sqrt(-jnp.tanh(log_a) * (a * a + 1.0))
    return xcb, r, ig, ls, a, mult


def _conv_taps(x, before, p_ref):
    xc = x * p_ref[7:8, :]
    for s in (1, 2, 3):
        xc = xc + _shift_down(x, before, s) * p_ref[7 - s:8 - s, :]
    return xc + p_ref[0:1, :]


def _lru_block_rows(t):
    return 512 if t >= 1024 else t // 2


def _lru_fwd(xl, p, wrg2, wig2):
    t = xl.shape[0]
    tb = _lru_block_rows(t)

    def body(xl_ref, p_ref, wrg_ref, wig_ref, h_ref, x_tail, h_carry):
        tt = pl.program_id(1)

        @pl.when(tt == 0)
        def _():
            x_tail[...] = jnp.zeros_like(x_tail)
            h_carry[...] = jnp.zeros_like(h_carry)

        x = xl_ref[...]
        xc = _conv_taps(x, x_tail[...], p_ref)
        x_tail[...] = x[tb - 8:tb]
        _, r, ig, ls, a, mult = _lru_gates(xc, p_ref, wrg_ref[0], wig_ref[0])
        u = mult * ig * xc
        row = lax.broadcasted_iota(jnp.int32, (tb, LRU_GROUP), 0)
        s = 1
        while s < tb:
            keep = row >= s
            u = jnp.where(keep, a * pltpu.roll(u, s, 0) + u, u)
            a = jnp.where(keep, a * pltpu.roll(a, s, 0), a)
            s *= 2
        h = u + a * h_carry[0:1, :]
        h_ref[...] = h
        h_carry[...] = jnp.broadcast_to(h[tb - 1:tb], h_carry.shape)

    blk = pl.BlockSpec((tb, LRU_GROUP), lambda g, tt: (tt, g))
    par = pl.BlockSpec((8, LRU_GROUP), lambda g, tt: (0, g))
    wsp = pl.BlockSpec((1, LRU_GROUP, LRU_GROUP), lambda g, tt: (g, 0, 0))
    return pl.pallas_call(
        body, name="lru_fwd", grid=(N_LRU_GROUP, t // tb), in_specs=[blk, par, wsp, wsp], out_specs=blk,
        out_shape=jax.ShapeDtypeStruct((t, D_LRU), F32),
        scratch_shapes=[pltpu.VMEM((8, LRU_GROUP), F32), pltpu.VMEM((8, LRU_GROUP), F32)],
        compiler_params=_params(2),
    )(xl, p, wrg2, wig2)


def _lru_bwd(dy, h, xl, gl, p, wrg2, wig2):
    t = xl.shape[0]
    tb = _lru_block_rows(t)
    n_tb = t // tb
    tb8 = tb // 8

    def body(dy_ref, h_ref, hprev_ref, xl_ref, xprev_ref, gl_ref, p_ref, wrg_ref, wig_ref,
             dxl_ref, dgl_ref, dp_ref, dwrg_ref, dwig_ref, g_carry, a_carry, dxc_head):
        step = pl.program_id(1)
        tt = n_tb - 1 - step
        first = step == 0

        _zero_at_first(first, g_carry, a_carry, dxc_head, dp_ref, dwrg_ref, dwig_ref)

        has_prev = (tt > 0).astype(F32)
        x = xl_ref[...]
        x_before = xprev_ref[...] * has_prev
        xs = [_shift_down(x, x_before, s) for s in range(4)]
        xc = xs[0] * p_ref[7:8, :] + xs[1] * p_ref[6:7, :] + xs[2] * p_ref[5:6, :] + xs[3] * p_ref[4:5, :] + p_ref[0:1, :]
        wrg = wrg_ref[0]
        wig = wig_ref[0]
        xcb, r, ig, ls, a, mult = _lru_gates(xc, p_ref, wrg, wig)

        hh = h_ref[...]
        h_m1 = _shift_down(hh, hprev_ref[...] * has_prev, 1)
        ge, dge = _gelu(gl_ref[...])
        dy = dy_ref[...]
        dgl_ref[...] = dy * hh * dge
        dh = dy * ge

        b = _shift_up(a, a_carry[...], 1)
        row = lax.broadcasted_iota(jnp.int32, (tb, LRU_GROUP), 0)
        g = dh
        s = 1
        while s < tb:
            keep = row < tb - s
            g = jnp.where(keep, b * pltpu.roll(g, tb - s, 0) + g, g)
            b = jnp.where(keep, b * pltpu.roll(b, tb - s, 0), b)
            s *= 2
        g = g + b * g_carry[0:1, :]
        g_carry[...] = jnp.broadcast_to(g[0:1], g_carry.shape)
        a_carry[...] = jnp.broadcast_to(a[0:1], a_carry.shape)

        da = g * h_m1
        dmult = g * ig * xc
        dig = g * mult * xc
        dxc = g * mult * ig
        dlog_a = da * a - dmult * (a * a) / mult
        dr = dlog_a * (LRU_C * ls)
        dls = jnp.sum(dlog_a * (LRU_C * r), axis=0, keepdims=True)
        dlam = dls * _sigmoid(-p_ref[3:4, :])
        dpre_r = dr * r * (1.0 - r)
        dpre_i = dig * ig * (1.0 - ig)
        dprb = dpre_r.astype(BF16)
        dpib = dpre_i.astype(BF16)
        dxc = dxc + _dot_nt(dprb, wrg) + _dot_nt(dpib, wig)
        dwrg_ref[0] += _dot_tn(xcb, dprb)
        dwig_ref[0] += _dot_tn(xcb, dpib)

        after = dxc_head[...]
        dxl = dxc * p_ref[7:8, :]
        for s in (1, 2, 3):
            dxl = dxl + _shift_up(dxc, after, s) * p_ref[7 - s:8 - s, :]
        dxl_ref[...] = dxl
        dxc_head[...] = dxc[0:8]

        rows = [jnp.sum(dxc, axis=0, keepdims=True), jnp.sum(dpre_r, axis=0, keepdims=True),
                jnp.sum(dpre_i, axis=0, keepdims=True), dlam]
        rows += [jnp.sum(dxc * xs[3 - k], axis=0, keepdims=True) for k in range(4)]
        dp_ref[...] += jnp.concatenate(rows, axis=0)

    blk = pl.BlockSpec((tb, LRU_GROUP), lambda g, s: (n_tb - 1 - s, g))
    prev8 = pl.BlockSpec((8, LRU_GROUP), lambda g, s: (jnp.maximum((n_tb - 1 - s) * tb8 - 1, 0), g))
    par = pl.BlockSpec((8, LRU_GROUP), lambda g, s: (0, g))
    wsp = pl.BlockSpec((1, LRU_GROUP, LRU_GROUP), lambda g, s: (g, 0, 0))
    return pl.pallas_call(
        body, name="lru_bwd", grid=(N_LRU_GROUP, n_tb),
        in_specs=[blk, blk, prev8, blk, prev8, blk, par, wsp, wsp], out_specs=[blk, blk, par, wsp, wsp],
        out_shape=[jax.ShapeDtypeStruct((t, D_LRU), F32), jax.ShapeDtypeStruct((t, D_LRU), F32),
                   jax.ShapeDtypeStruct((8, D_LRU), F32),
                   jax.ShapeDtypeStruct((N_LRU_GROUP, LRU_GROUP, LRU_GROUP), F32),
                   jax.ShapeDtypeStruct((N_LRU_GROUP, LRU_GROUP, LRU_GROUP), F32)],
        scratch_shapes=[pltpu.VMEM((8, LRU_GROUP), F32)] * 3,
        compiler_params=_params(2),
    )(dy, h, h, xl, xl, gl, p, wrg2, wig2)


def _attn_bias(first_block):
    qi = jnp.bitwise_and(lax.broadcasted_iota(jnp.int32, (4 * BLOCK_Q, 2 * BLOCK_Q), 0), BLOCK_Q - 1)
    kj = lax.broadcasted_iota(jnp.int32, (4 * BLOCK_Q, 2 * BLOCK_Q), 1)
    rel = qi + BLOCK_Q - kj
    mask = (rel >= 0) & (rel < BLOCK_Q)
    if first_block:
        mask = mask & (kj >= BLOCK_Q)
    return jnp.where(mask, 0.0, MASK_VALUE)


def _sink_column(sinks):
    hrow = lax.broadcasted_iota(jnp.int32, (4 * BLOCK_Q, 1), 0)
    return jnp.where(hrow < BLOCK_Q, sinks[0],
                     jnp.where(hrow < 2 * BLOCK_Q, sinks[1], jnp.where(hrow < 3 * BLOCK_Q, sinks[2], sinks[3])))


def _attn_scores(qv, kvv, n, bias, sk, lo):
    r0 = pl.multiple_of(n * BLOCK_Q, BLOCK_Q)
    rp = pl.multiple_of(jnp.maximum(n - 1, 0) * BLOCK_Q, BLOCK_Q)
    kvb = jnp.concatenate([kvv[pl.ds(rp, BLOCK_Q), :], kvv[pl.ds(r0, BLOCK_Q), :]], axis=0)
    k2 = kvb[:, 0:128]
    v2 = kvb[:, 128:256]
    qs = _stack_heads(qv[pl.ds(r0, BLOCK_Q), :], lo)
    s = _dot_nt(qs, k2) * ATTN_SCALE + bias
    m = jnp.maximum(jnp.max(s, axis=-1, keepdims=True), sk)
    e = jnp.exp(s - m)
    es = jnp.exp(sk - m)
    inv = 1.0 / (jnp.sum(e, axis=-1, keepdims=True) + es)
    return r0, rp, qs, k2, v2, e * inv, es * inv


def _stack_heads(pair2, lo):
    p0 = pair2[:, 0:128]
    p1 = pair2[:, 128:256]
    z = jnp.zeros_like(p0)
    return jnp.concatenate([jnp.where(lo, p0, z), jnp.where(lo, z, p0), jnp.where(lo, p1, z), jnp.where(lo, z, p1)], axis=0)


def _unstack_heads(st, lo):
    b = BLOCK_Q
    return jnp.concatenate([jnp.where(lo, st[0:b], st[b:2 * b]), jnp.where(lo, st[2 * b:3 * b], st[3 * b:4 * b])], axis=1)


def _attn_fwd(q, kv, sinks):
    t = q.shape[0]
    n_blk = t // BLOCK_Q

    def body(q_hbm, kv_hbm, s_ref, o_hbm, q2, kvv, o2, bias0, bias, sem):
        lo = lax.broadcasted_iota(jnp.int32, (BLOCK_Q, 128), 1) < HEAD_DIM
        cols = [pl.ds(256 * g, 256) for g in range(2)]
        loads = [pltpu.make_async_copy(kv_hbm, kvv, sem.at[0])]
        loads += [pltpu.make_async_copy(q_hbm.at[:, cols[g]], q2.at[g], sem.at[1 + g]) for g in range(2)]
        stores = [pltpu.make_async_copy(o2.at[g], o_hbm.at[:, cols[g]], sem.at[3 + g]) for g in range(2)]
        for cp in loads:
            cp.start()
        bias0[...] = _attn_bias(True)
        bias[...] = _attn_bias(False)
        loads[0].wait()
        for g in range(2):
            loads[1 + g].wait()
            qv, ov = q2.at[g], o2.at[g]
            sk = _sink_column([s_ref[0, HEAD_ORDER[4 * g + i]] for i in range(4)])

            def block(n, bias_ref):
                r0, _, _, _, v2, prob, _ = _attn_scores(qv, kvv, n, bias_ref[...], sk, lo)
                ov[pl.ds(r0, BLOCK_Q), :] = _unstack_heads(_dot(prob.astype(BF16), v2), lo)

            block(0, bias0)

            def later(n, carry):
                block(n, bias)
                return carry

            lax.fori_loop(1, n_blk, later, 0, unroll=2)
            stores[g].start()
        for cp in stores:
            cp.wait()

    return pl.pallas_call(
        body, name="attn_fwd", in_specs=[ANY, ANY, SMEM], out_specs=ANY,
        out_shape=jax.ShapeDtypeStruct((t, D_ATTN), F32),
        scratch_shapes=[pltpu.VMEM((2, t, 256), BF16), pltpu.VMEM((t, 256), BF16), pltpu.VMEM((2, t, 256), F32),
                        pltpu.VMEM((4 * BLOCK_Q, 2 * BLOCK_Q), F32), pltpu.VMEM((4 * BLOCK_Q, 2 * BLOCK_Q), F32),
                        pltpu.SemaphoreType.DMA((5,))],
        compiler_params=_params(),
    )(q, kv, sinks)


def _attn_bwd(q, kv, do, sinks, exchange=None):
    t = q.shape[0]
    n_blk = t // BLOCK_Q
    host = _Host(exchange)

    def body(*refs):
        own, ex_refs = host.split(refs, 4, 3, 9)
        q_hbm, kv_hbm, do_hbm, s_ref, dq_hbm, dkv_hbm, dsink_ref, q2, kvv, do2, dqv, dkvv, ds_acc, bias0, bias, sem = own
        host.phase(0, ex_refs)
        lo = lax.broadcasted_iota(jnp.int32, (BLOCK_Q, 128), 1) < HEAD_DIM
        loads = [pltpu.make_async_copy(kv_hbm, kvv, sem.at[0])]
        for g in range(2):
            loads += [pltpu.make_async_copy(src.at[:, pl.ds(256 * g, 256)], dst.at[g], sem.at[1 + 2 * g + i])
                      for i, (src, dst) in enumerate(((q_hbm, q2), (do_hbm, do2)))]
        for cp in loads:
            cp.start()
        bias0[...] = _attn_bias(True)
        bias[...] = _attn_bias(False)
        loads[0].wait()
        for g in range(2):
            cols = pl.ds(256 * g, 256)
            for cp in loads[1 + 2 * g:3 + 2 * g]:
                cp.wait()
            qv, dov = q2.at[g], do2.at[g]
            heads = [HEAD_ORDER[4 * g + i] for i in range(4)]
            sk = _sink_column([s_ref[0, h] for h in heads])
            ds_acc[...] = jnp.zeros_like(ds_acc)

            def block(n, bias_ref, has_prev):
                r0, rp, qs, k2, v2, prob, psink = _attn_scores(qv, kvv, n, bias_ref[...], sk, lo)
                pb = prob.astype(BF16)
                dos = _stack_heads(dov[pl.ds(r0, BLOCK_Q), :], lo)
                dp = _dot_nt(dos, v2)
                dsum = jnp.sum(prob * dp, axis=-1, keepdims=True)
                dsb = (prob * (dp - dsum) * ATTN_SCALE).astype(BF16)
                ds_acc[...] -= psink * dsum
                dqv[pl.ds(r0, BLOCK_Q), :] = _unstack_heads(_dot(dsb, k2), lo).astype(BF16)
                dk2 = _dot_tn(dsb, qs)
                dv2 = _dot_tn(pb, dos)
                cur = jnp.concatenate([dk2[BLOCK_Q:], dv2[BLOCK_Q:]], axis=1)
                if g == 0:
                    dkvv[pl.ds(r0, BLOCK_Q), :] = cur
                else:
                    dkvv[pl.ds(r0, BLOCK_Q), :] += cur
                if has_prev:
                    dkvv[pl.ds(rp, BLOCK_Q), :] += jnp.concatenate([dk2[:BLOCK_Q], dv2[:BLOCK_Q]], axis=1)

            block(0, bias0, False)

            def later(n, carry):
                block(n, bias, True)
                return carry

            lax.fori_loop(1, n_blk, later, 0, unroll=2)
            for i, h in enumerate(heads):
                tot = jnp.sum(ds_acc[BLOCK_Q * i:BLOCK_Q * (i + 1), :], axis=0, keepdims=True)
                dsink_ref[h:h + 1, :] = jnp.broadcast_to(tot, (1, 128))
            store = pltpu.make_async_copy(dqv, dq_hbm.at[:, cols], sem.at[5])
            store.start()
            store.wait()
        store = pltpu.make_async_copy(dkvv, dkv_hbm, sem.at[6])
        store.start()
        store.wait()
        if exchange is not None:
            for p in range(1, exchange.n_phases):
                host.phase(p, ex_refs)

    res = pl.pallas_call(
        body, name="attn_bwd", in_specs=[ANY, ANY, ANY, SMEM] + host.in_specs,
        out_specs=[ANY, ANY, pl.BlockSpec(memory_space=pltpu.VMEM)] + host.out_specs,
        out_shape=[jax.ShapeDtypeStruct((t, D_ATTN), BF16), jax.ShapeDtypeStruct((t, 256), F32),
                   jax.ShapeDtypeStruct((8, 128), F32)] + host.out_shape,
        scratch_shapes=[pltpu.VMEM((2, t, 256), BF16), pltpu.VMEM((t, 256), BF16), pltpu.VMEM((2, t, 256), BF16),
                        pltpu.VMEM((t, 256), BF16), pltpu.VMEM((t, 256), F32), pltpu.VMEM((4 * BLOCK_Q, 1), F32),
                        pltpu.VMEM((4 * BLOCK_Q, 2 * BLOCK_Q), F32), pltpu.VMEM((4 * BLOCK_Q, 2 * BLOCK_Q), F32),
                        pltpu.SemaphoreType.DMA((7,))] + host.scratch,
        compiler_params=_params(),
    )(q, kv, do, sinks, *host.args)
    return (*res[:3], list(res[3:]))


def _mix_out_fwd(x1, h, gl, o, g_lru, g_attn, g_post, w_o):
    t = x1.shape[0]
    tm = _token_tile(t)

    def body(x_ref, h_ref, gl_ref, o_ref, g1_ref, g2_ref, gp_ref, w_ref, x2_ref, m_ref):
        y = h_ref[...] * _gelu(gl_ref[...])[0]
        yn1 = _rms_fwd(y, g1_ref[...]).astype(BF16)
        yn2 = _rms_fwd(o_ref[...], g2_ref[...]).astype(BF16)
        m = _dot(yn1, w_ref[0:512, :]) + _dot(yn2, w_ref[512:1024, :])
        m_ref[...] = m
        x2_ref[...] = x_ref[...] + _rms_fwd(m, gp_ref[...])

    tok = pl.BlockSpec((tm, D_MODEL), lambda i: (i, 0))
    half = pl.BlockSpec((tm, 512), lambda i: (i, 0))
    vec = pl.BlockSpec((1, D_MODEL), lambda i: (0, 0))
    hvec = pl.BlockSpec((1, 512), lambda i: (0, 0))
    return pl.pallas_call(
        body, name="mix_out_fwd", grid=(t // tm,),
        in_specs=[tok, half, half, half, hvec, hvec, vec, pl.BlockSpec((D_MODEL, D_MODEL), lambda i: (0, 0))],
        out_specs=[tok, tok],
        out_shape=[jax.ShapeDtypeStruct((t, D_MODEL), F32), jax.ShapeDtypeStruct((t, D_MODEL), F32)],
        compiler_params=_params(1),
    )(x1, h, gl, o, g_lru, g_attn, g_post, w_o)


def _mix_out_bwd(dx2, m, h, gl, o, g_lru, g_attn, g_post, w_o):
    t = dx2.shape[0]
    tm = _token_tile(t)

    def body(dx_ref, m_ref, h_ref, gl_ref, o_ref, g1_ref, g2_ref, gp_ref, w_ref,
             dy_ref, do_ref, dw_ref, dgp_ref, dg1_ref, dg2_ref):
        _zero_at_first(pl.program_id(0) == 0, dw_ref, dgp_ref, dg1_ref, dg2_ref)
        dm, dgp = _rms_bwd(m_ref[...], gp_ref[...], dx_ref[...])
        dmb = dm.astype(BF16)
        y = h_ref[...] * _gelu(gl_ref[...])[0]
        o = o_ref[...]
        yn1 = _rms_fwd(y, g1_ref[...]).astype(BF16)
        yn2 = _rms_fwd(o, g2_ref[...]).astype(BF16)
        dw_ref[0:512, :] += _dot_tn(yn1, dmb)
        dw_ref[512:1024, :] += _dot_tn(yn2, dmb)
        dy, dg1 = _rms_bwd(y, g1_ref[...], _dot_nt(dmb, w_ref[0:512, :]))
        do, dg2 = _rms_bwd(o, g2_ref[...], _dot_nt(dmb, w_ref[512:1024, :]))
        dy_ref[...] = dy
        do_ref[...] = do.astype(BF16)
        dgp_ref[...] += dgp
        dg1_ref[...] += dg1
        dg2_ref[...] += dg2

    tok = pl.BlockSpec((tm, D_MODEL), lambda i: (i, 0))
    half = pl.BlockSpec((tm, 512), lambda i: (i, 0))
    vec = pl.BlockSpec((1, D_MODEL), lambda i: (0, 0))
    hvec = pl.BlockSpec((1, 512), lambda i: (0, 0))
    mat = pl.BlockSpec((D_MODEL, D_MODEL), lambda i: (0, 0))
    return pl.pallas_call(
        body, name="mix_out_bwd", grid=(t // tm,),
        in_specs=[tok, tok, half, half, half, hvec, hvec, vec, mat],
        out_specs=[half, half, mat, vec, hvec, hvec],
        out_shape=[jax.ShapeDtypeStruct((t, 512), F32), jax.ShapeDtypeStruct((t, 512), BF16),
                   jax.ShapeDtypeStruct((D_MODEL, D_MODEL), F32), jax.ShapeDtypeStruct((1, D_MODEL), F32),
                   jax.ShapeDtypeStruct((1, 512), F32), jax.ShapeDtypeStruct((1, 512), F32)],
        compiler_params=_params(1),
    )(dx2, m, h, gl, o, g_lru, g_attn, g_post, w_o)


def _mix_in_bwd(dx2, x1, g, dxl, dgl, dq, dkv, w_in, f1, g_post1):
    t = x1.shape[0]
    tm = _token_tile(t)

    def body(dx2_ref, x_ref, g_ref, dxl_ref, dgl_ref, dq_ref, dkv_ref, w_ref, f1_ref, gp1_ref,
             dx1_ref, dw_ref, dg_ref, df1_ref, dgp1_ref):
        _zero_at_first(pl.program_id(0) == 0, dw_ref, dg_ref, dgp1_ref)
        x = x_ref[...]
        nb = _rms_fwd(x, g_ref[...]).astype(BF16)
        dproj = jnp.concatenate([dxl_ref[...].astype(BF16), dgl_ref[...].astype(BF16), dq_ref[...],
                                 dkv_ref[...].astype(BF16)], axis=1)
        dw_ref[...] += _dot_tn(nb, dproj)
        dx, dg = _rms_bwd(x, g_ref[...], _dot(dproj, w_ref[...]))
        dx1 = dx2_ref[...] + dx
        dx1_ref[...] = dx1
        dg_ref[...] += dg
        df1, dgp1 = _rms_bwd(f1_ref[...], gp1_ref[...], 0.5 * dx1)
        df1_ref[...] = df1.astype(BF16)
        dgp1_ref[...] += dgp1

    tok = pl.BlockSpec((tm, D_MODEL), lambda i: (i, 0))
    half = pl.BlockSpec((tm, 512), lambda i: (i, 0))
    vec = pl.BlockSpec((1, D_MODEL), lambda i: (0, 0))
    mat = pl.BlockSpec((D_IN, D_MODEL), lambda i: (0, 0))
    dmat = pl.BlockSpec((D_MODEL, D_IN), lambda i: (0, 0))
    quarter = pl.BlockSpec((tm, 256), lambda i: (i, 0))
    return pl.pallas_call(
        body, name="mix_in_bwd", grid=(t // tm,),
        in_specs=[tok, tok, vec, half, half, half, quarter, mat, tok, vec], out_specs=[tok, dmat, vec, tok, vec],
        out_shape=[jax.ShapeDtypeStruct((t, D_MODEL), F32), jax.ShapeDtypeStruct((D_MODEL, D_IN), F32),
                   jax.ShapeDtypeStruct((1, D_MODEL), F32), jax.ShapeDtypeStruct((t, D_MODEL), BF16),
                   jax.ShapeDtypeStruct((1, D_MODEL), F32)],
        compiler_params=_params(1),
    )(dx2, x1, g, dxl, dgl, dq, dkv, w_in, f1, g_post1)


def _row_tile(rows):
    return rows if rows <= 512 else rows // 2


def _chip_sum(grad, from_sibling, other, name):
    _, rows, cols = grad.shape
    tr = _row_tile(rows)

    def body(other_ref, g_ref, s_ref, out_ref):
        out_ref[0] = (g_ref[0, 0] + s_ref[0]).astype(BF16)

    grid_spec = pltpu.PrefetchScalarGridSpec(
        num_scalar_prefetch=1, grid=(3, rows // tr),
        in_specs=[pl.BlockSpec((1, 1, tr, cols), lambda j, i, other: (other[j], other[3], i, 0)),
                  pl.BlockSpec((1, tr, cols), lambda j, i, other: (other[j], i, 0))],
        out_specs=pl.BlockSpec((1, tr, cols), lambda j, i, other: (j, i, 0)))
    return pl.pallas_call(
        body, name=name, grid_spec=grid_spec, out_shape=jax.ShapeDtypeStruct((3, rows, cols), BF16),
        compiler_params=_params(2),
    )(other, grad.reshape(4, 2, rows, cols), from_sibling)


def _adamw(w, g, m, v):
    m = ADAM_B1 * m + (1.0 - ADAM_B1) * g
    v = ADAM_B2 * v + (1.0 - ADAM_B2) * (g * g)
    m_hat = m / (1.0 - ADAM_B1 ** ADAM_STEP)
    v_hat = v / (1.0 - ADAM_B2 ** ADAM_STEP)
    delta = -ADAM_LR * (m_hat / (jnp.sqrt(v_hat) + ADAM_EPS) + ADAM_WD * w)
    return delta, m, v


def _shard_update(grad, from_sibling, from_chips, w, m, v, place, name, transposed):
    _, rows, cols = grad.shape
    tr = _row_tile(rows)

    def total(g_ref, s_ref, c_ref):
        g = g_ref[0, 0] + s_ref[0]
        g = g + c_ref[0].astype(F32)
        g = g + c_ref[1].astype(F32)
        return g + c_ref[2].astype(F32)

    part_specs = [pl.BlockSpec((1, 1, tr, cols), lambda i, place: (place[0], place[1], i, 0)),
                  pl.BlockSpec((1, tr, cols), lambda i, place: (place[0], i, 0)),
                  pl.BlockSpec((3, tr, cols), lambda i, place: (0, i, 0))]
    flat = pl.BlockSpec((tr, cols), lambda i, place: (i, 0))
    parts = (place, grad.reshape(4, 2, rows, cols), from_sibling, from_chips)
    if not transposed:
        def body(place_ref, g_ref, s_ref, c_ref, w_ref, m_ref, v_ref, go_ref, d_ref, mo_ref, vo_ref):
            g = total(g_ref, s_ref, c_ref)
            go_ref[...] = g
            d_ref[...], mo_ref[...], vo_ref[...] = _adamw(w_ref[...], g, m_ref[...], v_ref[...])

        grid_spec = pltpu.PrefetchScalarGridSpec(num_scalar_prefetch=1, grid=(rows // tr,),
                                                 in_specs=part_specs + [flat, flat, flat], out_specs=[flat] * 4)
        return pl.pallas_call(body, name=name, grid_spec=grid_spec, out_shape=[jax.ShapeDtypeStruct((rows, cols), F32)] * 4,
                              compiler_params=_params(1))(*parts, w, m, v)

    def sum_body(place_ref, g_ref, s_ref, c_ref, go_ref):
        go_ref[...] = total(g_ref, s_ref, c_ref)

    grid_spec = pltpu.PrefetchScalarGridSpec(num_scalar_prefetch=1, grid=(rows // tr,), in_specs=part_specs, out_specs=flat)
    g = pl.pallas_call(sum_body, name=name + "_sum", grid_spec=grid_spec, out_shape=jax.ShapeDtypeStruct((rows, cols), F32),
                       compiler_params=_params(1))(*parts)
    gt = jnp.transpose(g, (1, 0))
    tc = _row_tile(cols)

    def adam_body(g_ref, w_ref, m_ref, v_ref, d_ref, mo_ref, vo_ref):
        d_ref[...], mo_ref[...], vo_ref[...] = _adamw(w_ref[...], g_ref[...], m_ref[...], v_ref[...])

    blk = pl.BlockSpec((tc, rows), lambda i: (i, 0))
    res = pl.pallas_call(adam_body, name=name + "_adam", grid=(cols // tc,), in_specs=[blk] * 4, out_specs=[blk] * 3,
                         out_shape=[jax.ShapeDtypeStruct((cols, rows), F32)] * 3, compiler_params=_params(1))(gt, w, m, v)
    return (gt, *res)


GAINS = ("ffn1_pre_g", "ffn1_post_g", "mix_pre_g", "mix_post_g", "ffn2_pre_g", "ffn2_post_g")
HALVES = ("conv_b", "b_rg", "b_ig", "lru_lambda", "g_lru_out", "g_attn_out")
GATES = ("w_rg", "w_ig")
SMALL = GAINS + HALVES + GATES + ("sinks", "conv_w")


def _small_update(gathered, w, m, v):
    n_small = len(SMALL)

    def body(*refs):
        ga_ref, gb_ref, gc_ref, gd_ref, g0_ref = refs[:5]
        wmv = refs[5:5 + 3 * n_small]
        outs = refs[5 + 3 * n_small:5 + 7 * n_small]
        loss_ref = refs[5 + 7 * n_small]

        def total(ref):
            s = ref[0]
            for d in range(1, N_DEV):
                s = s + ref[d]
            return s

        sa, sb, sc, sd = total(ga_ref), total(gb_ref), total(gc_ref), total(gd_ref)
        grads = {}
        for i, k in enumerate(GAINS):
            grads[k] = sa[i:i + 1]
        grads[GAINS[0]] = total(g0_ref)
        for i, k in enumerate(HALVES):
            grads[k] = sb[i:i + 1]
        grads["w_rg"], grads["w_ig"] = sc[0:512], sc[512:1024]
        grads["sinks"] = sd[4:5, 0:8]
        grads["conv_w"] = sd[0:4]
        for i, k in enumerate(SMALL):
            g = grads[k]
            outs[4 * i][...] = g
            outs[4 * i + 1][...], outs[4 * i + 2][...], outs[4 * i + 3][...] = _adamw(
                wmv[3 * i][...], g, wmv[3 * i + 1][...], wmv[3 * i + 2][...])
        loss_ref[...] = jnp.broadcast_to(sd[5:6, 0:128], loss_ref.shape)

    operands = list(gathered)
    out_shape = []
    for k in SMALL:
        operands += [w[k], m[k], v[k]]
        out_shape += [jax.ShapeDtypeStruct(w[k].shape, F32)] * 4
    out_shape.append(jax.ShapeDtypeStruct((8, 128), F32))
    res = pl.pallas_call(body, name="small_update", out_shape=out_shape, compiler_params=_params())(*operands)
    parts = [{k: res[4 * i + j] for i, k in enumerate(SMALL)} for j in range(4)]
    return (*parts, res[-1])


def _reorder_heads(a, axis, start, order):
    def slab(h):
        return lax.slice_in_dim(a, start + HEAD_DIM * h, start + HEAD_DIM * (h + 1), axis=axis)

    parts = [lax.slice_in_dim(a, 0, start, axis=axis)] + [slab(h) for h in order]
    parts.append(lax.slice_in_dim(a, start + 8 * HEAD_DIM, a.shape[axis], axis=axis))
    return jnp.concatenate(parts, axis=axis)


HEAD_ORDER_INVERSE = tuple(HEAD_ORDER.index(h) for h in range(8))


def _pair_block_diag(w):
    w = w.reshape(N_LRU_GROUP, 2, 64, 64)
    z = jnp.zeros((N_LRU_GROUP, 64, 64), w.dtype)
    top = jnp.concatenate([w[:, 0], z], axis=2)
    bot = jnp.concatenate([z, w[:, 1]], axis=2)
    return jnp.concatenate([top, bot], axis=1)


def _pair_block_diag_grad(dw2):
    return jnp.stack([dw2[:, :64, :64], dw2[:, 64:, 64:]], axis=1).reshape(512, 64)


def kernel(x, ffn1_pre_g, ffn1_w_gu, ffn1_w_down, ffn1_post_g, mix_pre_g, w_in, conv_w, conv_b, w_rg, b_rg, w_ig, b_ig, lru_lambda, sinks, g_lru_out, g_attn_out, w_o, mix_post_g, ffn2_pre_g, ffn2_w_gu, ffn2_w_down, ffn2_post_g, loss_target, m_ffn1_pre_g, m_ffn1_w_gu, m_ffn1_w_down, m_ffn1_post_g, m_mix_pre_g, m_w_in, m_conv_w, m_conv_b, m_w_rg, m_b_rg, m_w_ig, m_b_ig, m_lru_lambda, m_sinks, m_g_lru_out, m_g_attn_out, m_w_o, m_mix_post_g, m_ffn2_pre_g, m_ffn2_w_gu, m_ffn2_w_down, m_ffn2_post_g, v_ffn1_pre_g, v_ffn1_w_gu, v_ffn1_w_down, v_ffn1_post_g, v_mix_pre_g, v_w_in, v_conv_w, v_conv_b, v_w_rg, v_b_rg, v_w_ig, v_b_ig, v_lru_lambda, v_sinks, v_g_lru_out, v_g_attn_out, v_w_o, v_mix_post_g, v_ffn2_pre_g, v_ffn2_w_gu, v_ffn2_w_down, v_ffn2_post_g):
    args = dict(locals())
    names = ["ffn1_pre_g", "ffn1_w_gu", "ffn1_w_down", "ffn1_post_g", "mix_pre_g", "w_in", "conv_w", "conv_b", "w_rg",
             "b_rg", "w_ig", "b_ig", "lru_lambda", "sinks", "g_lru_out", "g_attn_out", "w_o", "mix_post_g",
             "ffn2_pre_g", "ffn2_w_gu", "ffn2_w_down", "ffn2_post_g"]
    big = ["ffn1_w_gu", "ffn1_w_down", "w_in", "w_o", "ffn2_w_gu", "ffn2_w_down"]
    w = {k: args[k] for k in names}
    mom = {k: args["m_" + k] for k in names}
    var = {k: args["v_" + k] for k in names}
    t = x.shape[1]
    xs = x.reshape(t, D_MODEL)
    target = loss_target.reshape(t, D_MODEL)
    cx, cy, cc = _coords()
    me = 4 * cx + 2 * cy + cc
    other = jnp.stack([2 * (1 - cx) + cy, 2 * cx + (1 - cy), 2 * (1 - cx) + (1 - cy), cc]).astype(jnp.int32)
    place = jnp.stack([2 * cx + cy, cc]).astype(jnp.int32)

    transposed = ("ffn1_w_gu", "w_in", "ffn2_w_gu")

    def shard_view(a, k):
        return jnp.transpose(a[0], (1, 0)) if k in transposed else a[0]

    def shard_unview(a, k):
        return (jnp.transpose(a, (1, 0)) if k in transposed else a)[None]

    shard2d = {k: shard_view(w[k], k) for k in big}
    shard_bf = {k: shard2d[k].astype(BF16) for k in big}
    conv_pad = jnp.pad(conv_w.reshape(4, 64), ((0, 4), (0, 64)))
    (first_w,) = _run_exchanges([_Gather([shard_bf["ffn1_w_gu"], shard_bf["ffn1_w_down"]], routed=True)], "all_gather_ffn1")
    wgu1 = first_w[0].reshape(2, N_CHUNK, CHUNK, D_MODEL)
    wd1 = first_w[1].reshape(N_CHUNK, CHUNK, D_MODEL)
    rest = _Gather([shard_bf["w_in"], shard_bf["w_o"], shard_bf["ffn2_w_gu"], shard_bf["ffn2_w_down"], conv_pad])

    x1, f1, n1, gu1, gathered = _ffn_fwd(xs, ffn1_pre_g, wgu1, wd1, ffn1_post_g, None, "ffn1_fwd", rest)
    w_in_full = _reorder_heads(gathered[0].reshape(D_IN, D_MODEL), 0, 2 * D_LRU, HEAD_ORDER)
    w_o_full = _reorder_heads(gathered[1].reshape(D_MODEL, D_MODEL), 0, D_LRU, HEAD_ORDER)
    g_attn_heads = _reorder_heads(g_attn_out, 1, 0, HEAD_ORDER)
    wgu2 = gathered[2].reshape(2, N_CHUNK, CHUNK, D_MODEL)
    wd2 = gathered[3].reshape(N_CHUNK, CHUNK, D_MODEL)
    conv_w_full = jnp.transpose(gathered[4][:, 0:4, 0:64], (1, 0, 2)).reshape(4, D_LRU)
    p_lru = jnp.concatenate([conv_b, b_rg, b_ig, lru_lambda, conv_w_full], axis=0)
    wrg2 = _pair_block_diag(w_rg[0]).astype(BF16)
    wig2 = _pair_block_diag(w_ig[0]).astype(BF16)
    xl, gl, q, kv = _mix_in_fwd(x1, mix_pre_g, w_in_full)
    h = _lru_fwd(xl, p_lru, wrg2, wig2)
    o = _attn_fwd(q, kv, sinks)
    x2, mo = _mix_out_fwd(x1, h, gl, o, g_lru_out, g_attn_heads, mix_post_g, w_o_full)
    g = {}
    dx3, n2, df2, gu2, g["ffn2_post_g"], loss_parts, _ = _ffn_fwd(x2, ffn2_pre_g, wgu2, wd2, ffn2_post_g, target, "ffn2_fwd")
    loss_local = jnp.sum(loss_parts[::8, 0])

    partial, from_sibling, from_chips = {}, {}, {}

    def chip_sums(keys):
        return [_chip_sum(partial[k], from_sibling[k], other, "chip_sum_" + k) for k in keys]

    dgu2, dwgu2, dwd2, _ = _ffn_bwd_w(n2, df2, gu2, wd2, "ffn2_bwd_w")
    partial["ffn2_w_gu"] = dwgu2.reshape(N_DEV, D_MODEL, CHUNK)
    partial["ffn2_w_down"] = dwd2.reshape(N_DEV, D_FF // N_DEV, D_MODEL)
    ffn2_keys = ["ffn2_w_gu", "ffn2_w_down"]
    dx2, g["ffn2_pre_g"], got = _ffn_bwd_x(dgu2, wgu2, x2, ffn2_pre_g, dx3, "ffn2_bwd_x",
                                           _SiblingExchange([partial[k] for k in ffn2_keys]))
    from_sibling.update(zip(ffn2_keys, got))
    dy, do, dwo, g["mix_post_g"], g["g_lru_out"], dg_attn_heads = _mix_out_bwd(
        dx2, mo, h, gl, o, g_lru_out, g_attn_heads, mix_post_g, w_o_full)
    g["g_attn_out"] = _reorder_heads(dg_attn_heads, 1, 0, HEAD_ORDER_INVERSE)
    dwo = _reorder_heads(dwo, 0, D_LRU, HEAD_ORDER_INVERSE)
    dq, dkv, dsink, got = _attn_bwd(q, kv, do, sinks, _ChipExchange(chip_sums(ffn2_keys)))
    from_chips.update(zip(ffn2_keys, got))
    dxl, dgl, dp, dwrg2, dwig2 = _lru_bwd(dy, h, xl, gl, p_lru, wrg2, wig2)
    dx1, dwin, g["mix_pre_g"], df1, g["ffn1_post_g"] = _mix_in_bwd(
        dx2, x1, mix_pre_g, dxl, dgl, dq, dkv, w_in_full, f1, ffn1_post_g)
    dwin = _reorder_heads(dwin, 1, 2 * D_LRU, HEAD_ORDER_INVERSE)
    partial["w_in"] = jnp.transpose(dwin.reshape(D_MODEL, N_DEV, D_IN // N_DEV), (1, 0, 2))
    partial["w_o"] = dwo.reshape(N_DEV, D_MODEL // N_DEV, D_MODEL)
    mix_keys = ["w_in", "w_o"]
    (got,) = _run_exchanges([_SiblingExchange([partial[k] for k in mix_keys])], "mix_sibling_exchange")
    from_sibling.update(zip(mix_keys, got))
    dgu1, dwgu1, dwd1, got = _ffn_bwd_w(n1, df1, gu1, wd1, "ffn1_bwd_w", _ChipExchange(chip_sums(mix_keys)))
    from_chips.update(zip(mix_keys, got))
    partial["ffn1_w_gu"] = dwgu1.reshape(N_DEV, D_MODEL, CHUNK)
    partial["ffn1_w_down"] = dwd1.reshape(N_DEV, D_FF // N_DEV, D_MODEL)
    ffn1_keys = ["ffn1_w_gu", "ffn1_w_down"]
    (got,) = _run_exchanges([_SiblingExchange([partial[k] for k in ffn1_keys])], "ffn1_sibling_exchange")
    from_sibling.update(zip(ffn1_keys, got))
    zeros2 = jnp.zeros((2, D_MODEL), F32)
    g_gains = jnp.concatenate([zeros2[:1]] + [g[k] for k in GAINS[1:]] + [zeros2], axis=0)
    g_halves = jnp.concatenate([dp[0:4], g["g_lru_out"], g["g_attn_out"], zeros2[:, :D_LRU]], axis=0)
    g_gates = jnp.concatenate([_pair_block_diag_grad(dwrg2), _pair_block_diag_grad(dwig2)], axis=0)
    g_misc = jnp.concatenate([dp[4:8], jnp.pad(dsink[:, 0].reshape(1, 8), ((0, 0), (0, D_LRU - 8))),
                              jnp.pad(loss_local.reshape(1, 1), ((0, 0), (0, D_LRU - 1))), zeros2[:, :D_LRU]], axis=0)
    dx0, g_first, got = _ffn_bwd_x(dgu1, wgu1, xs, ffn1_pre_g, dx1, "ffn1_bwd_x",
                                   _Both(_ChipExchange(chip_sums(ffn1_keys)), _Gather([g_gains, g_halves, g_gates, g_misc])))
    from_chips.update(zip(ffn1_keys, got[:2]))
    gathered_small = got[2:]

    grads, delta, new_m, new_v = {}, {}, {}, {}
    for k in big:
        res = _shard_update(partial[k], from_sibling[k], from_chips[k], shard2d[k], shard_view(mom[k], k),
                            shard_view(var[k], k), place, "update_" + k, k in transposed)
        grads[k], delta[k], new_m[k], new_v[k] = [shard_unview(r, k) for r in res]

    ((gathered_first,),) = _run_exchanges([_Gather([g_first])], "all_gather_first_gain")
    col = me * 64

    def small_view(vals):
        out = {k: vals[k] for k in GAINS + HALVES + ("sinks",)}
        out.update({k: vals[k].reshape(512, 64) for k in GATES})
        out["conv_w"] = lax.dynamic_update_slice(jnp.zeros((4, D_LRU), F32), vals["conv_w"].reshape(4, 64), (0, col))
        return out

    *small, loss_tile = _small_update([*gathered_small, gathered_first], small_view(w), small_view(mom), small_view(var))
    for dst, part in zip((grads, delta, new_m, new_v), small):
        for k in SMALL:
            if k == "conv_w":
                dst[k] = lax.dynamic_slice(part[k], (0, col), (4, 64)).reshape(conv_w.shape)
            else:
                dst[k] = part[k].reshape(w[k].shape)
    return (loss_tile[0, 0], dx0.reshape(x.shape), *[grads[k] for k in names], *[delta[k] for k in names],
            *[new_m[k] for k in names], *[new_v[k] for k in names])
```

```python
import functools

import jax
import jax.numpy as jnp
from jax import lax
from jax.experimental import pallas as pl
from jax.experimental.pallas import tpu as pltpu

F32 = jnp.float32
BF16 = jnp.bfloat16

D_MODEL = 1024
D_FF = 2816
N_DEV = 8
N_CHUNK = 4
CHUNK = D_FF // N_CHUNK
D_LRU = 512
D_ATTN = 512
LRU_GROUP = 128
N_LRU_GROUP = D_LRU // LRU_GROUP
HEAD_DIM = 64
BLOCK_Q = 128
D_IN = 1792
HEAD_ORDER = (0, 4, 1, 5, 2, 6, 3, 7)
RMS_EPS = 1e-6
LRU_C = 8.0
MASK_VALUE = -1e30
ATTN_SCALE = HEAD_DIM ** -0.5

ADAM_LR = 0.001
ADAM_B1 = 0.9
ADAM_B2 = 0.999
ADAM_EPS = 1e-08
ADAM_WD = 0.01
ADAM_STEP = 10

VMEM_LIMIT_V7X = 56 * 2 ** 20

ANY = pl.BlockSpec(memory_space=pl.ANY)
SMEM = pl.BlockSpec(memory_space=pltpu.SMEM)
MESH = pl.DeviceIdType.MESH


def _params(n_grid=0):
    sem = ("arbitrary",) * n_grid if n_grid else None
    return pltpu.CompilerParams(dimension_semantics=sem, vmem_limit_bytes=VMEM_LIMIT_V7X)


def _dot(a, b):
    return lax.dot_general(a, b, (((1,), (0,)), ((), ())), preferred_element_type=F32)


def _dot_nt(a, b):
    return lax.dot_general(a, b, (((1,), (1,)), ((), ())), preferred_element_type=F32)


def _dot_tn(a, b):
    return lax.dot_general(a, b, (((0,), (0,)), ((), ())), preferred_element_type=F32)


def _sigmoid(x):
    return 1.0 / (1.0 + jnp.exp(-x))


def _rms_fwd(x, g):
    r = lax.rsqrt(jnp.mean(x * x, axis=-1, keepdims=True) + RMS_EPS)
    return x * r * g


def _rms_bwd(x, g, dy):
    r = lax.rsqrt(jnp.mean(x * x, axis=-1, keepdims=True) + RMS_EPS)
    xh = x * r
    dg = jnp.sum(dy * xh, axis=0, keepdims=True)
    dxh = dy * g
    dx = r * (dxh - xh * jnp.mean(dxh * xh, axis=-1, keepdims=True))
    return dx, dg


def _gelu(x):
    c = 0.7978845608028654
    inner = c * (x + 0.044715 * x * x * x)
    th = jnp.tanh(inner)
    ge = 0.5 * x * (1.0 + th)
    dge = 0.5 * (1.0 + th) + 0.5 * x * (1.0 - th * th) * c * (1.0 + 3.0 * 0.044715 * x * x)
    return ge, dge


def _zero_at_first(first, *refs):
    @pl.when(first)
    def _():
        for ref in refs:
            ref[...] = jnp.zeros_like(ref)


def _token_tile(t):
    return 512 if t >= 2048 else t // 2


def _ffn_bwd_tile(t):
    return 1024 if t >= 4096 else t // 2


def _coords():
    return lax.axis_index("x"), lax.axis_index("y"), lax.axis_index("c")


class _Gather:
    n_phases = 3
    at = (0.0, 0.8, 1.0)

    def __init__(self, shards, routed=False):
        k = len(shards)
        self.routed = routed
        self.arrays = list(shards)
        self.out_shape = [jax.ShapeDtypeStruct((N_DEV,) + s.shape, s.dtype) for s in shards]
        self.scratch = [pltpu.SemaphoreType.DMA((7 * k,)), pltpu.SemaphoreType.DMA((7 * k,)), pltpu.SemaphoreType.DMA((k,))]

    def run(self, phase, ins, outs, sems):
        send_sems, recv_sems, local_sems = sems
        k_arr = len(ins)
        x, y, c = _coords()
        me, sibling = (x, y, c), (x, y, 1 - c)
        chips = [(1 - x, y), (x, 1 - y), (1 - x, 1 - y)]
        direct = 2 if self.routed else 3
        relay_from = (x + (1 - c) * (1 - 2 * x), y + c * (1 - 2 * y))
        relay_to = (x + c * (1 - 2 * x), y + (1 - c) * (1 - 2 * y))

        def rows(k, dev):
            return outs[k].at[4 * dev[0] + 2 * dev[1] + dev[2]]

        def copy(k, slot, block, to, src=None):
            return pltpu.make_async_remote_copy(
                src_ref=rows(k, block) if src is None else src, dst_ref=rows(k, block),
                send_sem=send_sems.at[7 * k + slot], recv_sem=recv_sems.at[7 * k + slot],
                device_id=to, device_id_type=MESH)

        def mine():
            return [pltpu.make_async_copy(ins[k], rows(k, me), local_sems.at[k]) for k in range(k_arr)]

        def first():
            return [copy(k, slot, me, to, src=ins[k]) for k in range(k_arr)
                    for slot, to in enumerate([sibling] + [(*chip, c) for chip in chips[:direct]])]

        def relayed(k):
            return copy(k, 3, (*relay_from, c), (*relay_to, c))

        def passed(j, k):
            return copy(k, 4 + j, (*chips[j], c), sibling)

        if phase == 0:
            for cp in mine() + first():
                cp.start()
        elif phase == 1:
            for j in range(direct):
                for k in range(k_arr):
                    copy(k, 1 + j, (*chips[j], c), me).wait_recv()
            for k in range(k_arr):
                if self.routed:
                    relayed(k).start()
                for j in range(direct):
                    passed(j, k).start()
        else:
            for k in range(k_arr):
                if self.routed:
                    copy(k, 3, (*chips[2], c), me).wait_recv()
                    passed(2, k).start()
            for k in range(k_arr):
                copy(k, 0, sibling, me).wait_recv()
                for j, chip in enumerate(chips):
                    copy(k, 4 + j, (*chip, 1 - c), me).wait_recv()
            sent = first() + [passed(j, k) for j in range(3) for k in range(k_arr)]
            if self.routed:
                sent += [relayed(k) for k in range(k_arr)]
            for cp in sent:
                cp.wait_send()
            for cp in mine():
                cp.wait()


class _SiblingExchange:
    n_phases = 2
    at = (0.0, 1.0)

    def __init__(self, grads):
        k = len(grads)
        self.arrays = list(grads)
        self.out_shape = [jax.ShapeDtypeStruct((4,) + g.shape[1:], g.dtype) for g in grads]
        self.scratch = [pltpu.SemaphoreType.DMA((4 * k,)), pltpu.SemaphoreType.DMA((4 * k,))]

    def run(self, phase, ins, outs, sems):
        send_sems, recv_sems = sems
        x, y, c = _coords()
        copies = [pltpu.make_async_remote_copy(
            src_ref=ins[k].at[2 * q + (1 - c)], dst_ref=outs[k].at[q],
            send_sem=send_sems.at[4 * k + q], recv_sem=recv_sems.at[4 * k + q],
            device_id=(x, y, 1 - c), device_id_type=MESH) for k in range(len(ins)) for q in range(4)]
        for cp in copies:
            if phase == 0:
                cp.start()
            else:
                cp.wait_recv()
                cp.wait_send()


class _ChipExchange:
    n_phases = 2
    at = (0.0, 1.0)

    def __init__(self, chip_sums):
        k = len(chip_sums)
        self.arrays = list(chip_sums)
        self.out_shape = [jax.ShapeDtypeStruct((3,) + s.shape[1:], s.dtype) for s in chip_sums]
        self.scratch = [pltpu.SemaphoreType.DMA((3 * k,)), pltpu.SemaphoreType.DMA((3 * k,))]

    def run(self, phase, ins, outs, sems):
        send_sems, recv_sems = sems
        x, y, c = _coords()
        chips = [(1 - x, y), (x, 1 - y), (1 - x, 1 - y)]
        copies = [pltpu.make_async_remote_copy(
            src_ref=ins[k].at[j], dst_ref=outs[k].at[j],
            send_sem=send_sems.at[3 * k + j], recv_sem=recv_sems.at[3 * k + j],
            device_id=(*chip, c), device_id_type=MESH) for k in range(len(ins)) for j, chip in enumerate(chips)]
        for cp in copies:
            if phase == 0:
                cp.start()
            else:
                cp.wait_recv()
                cp.wait_send()


class _Both:
    n_phases = 3
    at = (0.0, 0.95, 1.0)

    def __init__(self, two_phase, gather):
        self.parts = (two_phase, gather)
        self.arrays = two_phase.arrays + gather.arrays
        self.out_shape = two_phase.out_shape + gather.out_shape
        self.scratch = two_phase.scratch + gather.scratch

    def run(self, phase, ins, outs, sems):
        a, b = self.parts
        n_in, n_out, n_sem = len(a.arrays), len(a.out_shape), len(a.scratch)
        refs_a = (ins[:n_in], outs[:n_out], sems[:n_sem])
        refs_b = (ins[n_in:], outs[n_out:], sems[n_sem:])
        b.run(phase, *refs_b)
        if phase == 0:
            a.run(0, *refs_a)
        if phase == 2:
            a.run(1, *refs_a)


class _Host:
    def __init__(self, exchange):
        self.ex = exchange
        self.args = [] if exchange is None else exchange.arrays
        self.in_specs = [ANY] * len(self.args)
        self.out_shape = [] if exchange is None else exchange.out_shape
        self.out_specs = [ANY] * len(self.out_shape)
        self.scratch = [] if exchange is None else exchange.scratch

    def split(self, refs, n_in, n_out, n_scratch):
        a, b, s = len(self.args), len(self.out_shape), len(self.scratch)
        own_in, ex_in = refs[:n_in], refs[n_in:n_in + a]
        rest = refs[n_in + a:]
        own_out, ex_out = rest[:n_out], rest[n_out:n_out + b]
        rest = rest[n_out + b:]
        own_scratch, ex_sems = rest[:n_scratch], rest[n_scratch:n_scratch + s]
        return list(own_in) + list(own_out) + list(own_scratch), (ex_in, ex_out, ex_sems)

    def at_steps(self, step, n_steps, ex_refs):
        if self.ex is None:
            return
        for p in range(self.ex.n_phases):
            pl.when(step == int(round(self.ex.at[p] * (n_steps - 1))))(functools.partial(self.ex.run, p, *ex_refs))

    def phase(self, p, ex_refs):
        if self.ex is not None:
            self.ex.run(p, *ex_refs)


def _run_exchanges(exchanges, name):
    hosts = [_Host(ex) for ex in exchanges]
    n_in = [len(h.args) for h in hosts]
    n_out = [len(h.out_shape) for h in hosts]
    n_sc = [len(h.scratch) for h in hosts]

    def body(*refs):
        ins, outs, scr = refs[:sum(n_in)], refs[sum(n_in):sum(n_in) + sum(n_out)], refs[sum(n_in) + sum(n_out):]
        parts = []
        for e in range(len(hosts)):
            parts.append((ins[sum(n_in[:e]):sum(n_in[:e + 1])], outs[sum(n_out[:e]):sum(n_out[:e + 1])],
                          scr[sum(n_sc[:e]):sum(n_sc[:e + 1])]))
        for h, part in zip(hosts, parts):
            h.phase(0, part)
        for h, part in zip(hosts, parts):
            for p in range(1, h.ex.n_phases):
                h.phase(p, part)

    res = pl.pallas_call(
        body, name=name, in_specs=[ANY] * sum(n_in), out_specs=[ANY] * sum(n_out),
        out_shape=[s for h in hosts for s in h.out_shape], scratch_shapes=[s for h in hosts for s in h.scratch],
    )(*[a for h in hosts for a in h.args])
    return [res[sum(n_out[:e]):sum(n_out[:e + 1])] for e in range(len(hosts))]


def _ffn_fwd(x, g_pre, wgu, wd, g_post, target, name, exchange=None):
    t = x.shape[0]
    tm = _token_tile(t)
    n_i = t // tm
    with_loss = target is not None
    host = _Host(exchange)
    n_in, n_out = (6, 6) if with_loss else (5, 4)

    def body(*refs):
        own, ex_refs = host.split(refs, n_in, n_out, 0)
        if with_loss:
            x_ref, gpre_ref, wgu_ref, wd_ref, gpost_ref, tgt_ref, xo_ref, n_ref, df_ref, gu_ref, dgpost_ref, loss_ref = own
            _zero_at_first(pl.program_id(0) == 0, dgpost_ref)
        else:
            x_ref, gpre_ref, wgu_ref, wd_ref, gpost_ref, xo_ref, f_ref, n_ref, gu_ref = own
        host.at_steps(pl.program_id(0), n_i, ex_refs)
        x = x_ref[...]
        n = _rms_fwd(x, gpre_ref[...]).astype(BF16)
        n_ref[...] = n
        f = None
        for j in range(N_CHUNK):
            gate = _dot_nt(n, wgu_ref[0, j])
            up = _dot_nt(n, wgu_ref[1, j])
            gu_ref[0, j] = gate.astype(BF16)
            gu_ref[1, j] = up.astype(BF16)
            part = _dot((gate * _sigmoid(gate) * up).astype(BF16), wd_ref[j])
            f = part if f is None else f + part
        xo = x + 0.5 * _rms_fwd(f, gpost_ref[...])
        if with_loss:
            err = xo - tgt_ref[...]
            d_out = err * (1.0 / D_MODEL)
            xo_ref[...] = d_out
            df, dg = _rms_bwd(f, gpost_ref[...], 0.5 * d_out)
            df_ref[...] = df.astype(BF16)
            dgpost_ref[...] += dg
            part = 0.5 * jnp.sum(jnp.sum(err * err, axis=-1, keepdims=True) * (1.0 / D_MODEL), axis=0, keepdims=True)
            loss_ref[...] = jnp.broadcast_to(part, loss_ref.shape)
        else:
            f_ref[...] = f
            xo_ref[...] = xo

    tok = pl.BlockSpec((tm, D_MODEL), lambda i: (i, 0))
    vec = pl.BlockSpec((1, D_MODEL), lambda i: (0, 0))
    act = pl.BlockSpec((2, N_CHUNK, tm, CHUNK), lambda i: (0, 0, i, 0))
    tok_f32 = jax.ShapeDtypeStruct((t, D_MODEL), F32)
    tok_bf16 = jax.ShapeDtypeStruct((t, D_MODEL), BF16)
    act_shape = jax.ShapeDtypeStruct((2, N_CHUNK, t, CHUNK), BF16)
    in_specs = [tok, vec,
                pl.BlockSpec((2, N_CHUNK, CHUNK, D_MODEL), lambda i: (0, 0, 0, 0), pipeline_mode=pl.Buffered(1)),
                pl.BlockSpec((N_CHUNK, CHUNK, D_MODEL), lambda i: (0, 0, 0), pipeline_mode=pl.Buffered(1)),
                vec]
    args = [x, g_pre, wgu, wd, g_post]
    if with_loss:
        in_specs.append(tok)
        args.append(target)
        out_shape = [tok_f32, tok_bf16, tok_bf16, act_shape, jax.ShapeDtypeStruct((1, D_MODEL), F32),
                     jax.ShapeDtypeStruct((n_i * 8, 128), F32)]
        out_specs = [tok, tok, tok, act, vec, pl.BlockSpec((8, 128), lambda i: (i, 0))]
    else:
        out_shape = [tok_f32, tok_f32, tok_bf16, act_shape]
        out_specs = [tok, tok, tok, act]
    res = pl.pallas_call(
        body, name=name, grid=(n_i,), in_specs=in_specs + host.in_specs, out_specs=out_specs + host.out_specs,
        out_shape=out_shape + host.out_shape, scratch_shapes=host.scratch, compiler_params=_params(1),
    )(*args, *host.args)
    return (*res[:n_out], list(res[n_out:]))


def _ffn_bwd_w(n, df, gu, wd, name, exchange=None):
    t = n.shape[0]
    tm = _ffn_bwd_tile(t)
    n_i = t // tm
    host = _Host(exchange)

    def body(*refs):
        (n_ref, df_ref, gu_ref, wd_ref, dgu_ref, dwgu_ref, dwd_ref), ex_refs = host.split(refs, 4, 3, 0)
        i = pl.program_id(1)
        host.at_steps(pl.program_id(0) * n_i + i, N_CHUNK * n_i, ex_refs)
        _zero_at_first(i == 0, dwgu_ref, dwd_ref)
        nb = n_ref[...]
        dfb = df_ref[...]
        gate = gu_ref[0, 0].astype(F32)
        up = gu_ref[1, 0].astype(F32)
        s = _sigmoid(gate)
        silu = gate * s
        a = (silu * up).astype(BF16)
        da = _dot_nt(dfb, wd_ref[0])
        dup = (da * silu).astype(BF16)
        dgate = (da * up * (s * (1.0 + gate * (1.0 - s)))).astype(BF16)
        dgu_ref[0, 0] = dgate
        dgu_ref[1, 0] = dup
        dwgu_ref[0, 0] += _dot_tn(nb, dgate)
        dwgu_ref[1, 0] += _dot_tn(nb, dup)
        dwd_ref[0] += _dot_tn(a, dfb)

    tok = pl.BlockSpec((tm, D_MODEL), lambda j, i: (i, 0))
    act = pl.BlockSpec((2, 1, tm, CHUNK), lambda j, i: (0, j, i, 0))
    wgu_spec = pl.BlockSpec((2, 1, D_MODEL, CHUNK), lambda j, i: (0, j, 0, 0))
    wd_spec = pl.BlockSpec((1, CHUNK, D_MODEL), lambda j, i: (j, 0, 0))
    res = pl.pallas_call(
        body, name=name, grid=(N_CHUNK, n_i),
        in_specs=[tok, tok, act, wd_spec] + host.in_specs,
        out_specs=[act, wgu_spec, wd_spec] + host.out_specs,
        out_shape=[jax.ShapeDtypeStruct((2, N_CHUNK, t, CHUNK), BF16),
                   jax.ShapeDtypeStruct((2, N_CHUNK, D_MODEL, CHUNK), F32),
                   jax.ShapeDtypeStruct((N_CHUNK, CHUNK, D_MODEL), F32)] + host.out_shape,
        scratch_shapes=host.scratch, compiler_params=_params(2),
    )(n, df, gu, wd, *host.args)
    return (*res[:3], list(res[3:]))


def _ffn_bwd_x(dgu, wgu, x, g_pre, d_out, name, exchange=None):
    t = x.shape[0]
    tm = _token_tile(t)
    n_i = t // tm
    host = _Host(exchange)

    def body(*refs):
        (dgu_ref, wgu_ref, x_ref, gpre_ref, do_ref, dx_ref, dgpre_ref), ex_refs = host.split(refs, 5, 2, 0)
        i = pl.program_id(0)
        host.at_steps(i, n_i, ex_refs)
        _zero_at_first(i == 0, dgpre_ref)
        dn = _dot(dgu_ref[0, 0], wgu_ref[0, 0]) + _dot(dgu_ref[1, 0], wgu_ref[1, 0])
        for j in range(1, N_CHUNK):
            dn = dn + _dot(dgu_ref[0, j], wgu_ref[0, j]) + _dot(dgu_ref[1, j], wgu_ref[1, j])
        dx, dg = _rms_bwd(x_ref[...], gpre_ref[...], dn)
        dx_ref[...] = do_ref[...] + dx
        dgpre_ref[...] += dg

    tok = pl.BlockSpec((tm, D_MODEL), lambda i: (i, 0))
    vec = pl.BlockSpec((1, D_MODEL), lambda i: (0, 0))
    res = pl.pallas_call(
        body, name=name, grid=(n_i,),
        in_specs=[pl.BlockSpec((2, N_CHUNK, tm, CHUNK), lambda i: (0, 0, i, 0)),
                  pl.BlockSpec((2, N_CHUNK, CHUNK, D_MODEL), lambda i: (0, 0, 0, 0), pipeline_mode=pl.Buffered(1)),
                  tok, vec, tok] + host.in_specs,
        out_specs=[tok, vec] + host.out_specs,
        out_shape=[jax.ShapeDtypeStruct((t, D_MODEL), F32), jax.ShapeDtypeStruct((1, D_MODEL), F32)] + host.out_shape,
        scratch_shapes=host.scratch, compiler_params=_params(1),
    )(dgu, wgu, x, g_pre, d_out, *host.args)
    return (*res[:2], list(res[2:]))


def _mix_in_fwd(x1, g, w_in):
    t = x1.shape[0]
    tm = _token_tile(t)

    def body(x_ref, g_ref, w_ref, xl_ref, gl_ref, q_ref, kv_ref):
        n = _rms_fwd(x_ref[...], g_ref[...]).astype(BF16)
        proj = _dot_nt(n, w_ref[...])
        xl_ref[...] = proj[:, 0:512]
        gl_ref[...] = proj[:, 512:1024]
        q_ref[...] = proj[:, 1024:1536].astype(BF16)
        kv_ref[...] = proj[:, 1536:1792].astype(BF16)

    tok = pl.BlockSpec((tm, D_MODEL), lambda i: (i, 0))
    half = pl.BlockSpec((tm, 512), lambda i: (i, 0))
    return pl.pallas_call(
        body, name="mix_in_fwd", grid=(t // tm,),
        in_specs=[tok, pl.BlockSpec((1, D_MODEL), lambda i: (0, 0)), pl.BlockSpec((D_IN, D_MODEL), lambda i: (0, 0))],
        out_specs=[half, half, half, pl.BlockSpec((tm, 256), lambda i: (i, 0))],
        out_shape=[jax.ShapeDtypeStruct((t, 512), F32), jax.ShapeDtypeStruct((t, 512), F32),
                   jax.ShapeDtypeStruct((t, 512), BF16), jax.ShapeDtypeStruct((t, 256), BF16)],
        compiler_params=_params(1),
    )(x1, g, w_in)


def _shift_down(x, before, s):
    if s == 0:
        return x
    rolled = pltpu.roll(x, s, 0)
    ext = jnp.concatenate([before, x[0:8]], axis=0)
    first8 = pltpu.roll(ext, s, 0)[8:16]
    return jnp.concatenate([first8, rolled[8:]], axis=0)


def _shift_up(x, after, s):
    if s == 0:
        return x
    rows = x.shape[0]
    rolled = pltpu.roll(x, rows - s, 0)
    ext = jnp.concatenate([x[rows - 8:rows], after], axis=0)
    last8 = pltpu.roll(ext, 16 - s, 0)[0:8]
    return jnp.concatenate([rolled[:rows - 8], last8], axis=0)


def _log_sigmoid(x):
    e = jnp.exp(-jnp.abs(x))
    log1p_e = jnp.where(e < 0.01, e * (1.0 - e * (0.5 - e * (1.0 / 3.0))), jnp.log(1.0 + e))
    return jnp.minimum(x, 0.0) - log1p_e


def _lru_gates(xc, p_ref, wrg, wig):
    xcb = xc.astype(BF16)
    r = _sigmoid(_dot(xcb, wrg) + p_ref[1:2, :])
    ig = _sigmoid(_dot(xcb, wig) + p_ref[2:3, :])
    ls = _log_sigmoid(p_ref[3:4, :])
    log_a = LRU_C * r * ls
    a = jnp.exp(log_a)
    mult = jnp.sqrt(-jnp.tanh(log_a) * (a * a + 1.0))
    return xcb, r, ig, ls, a, mult


def _conv_taps(x, before, p_ref):
    xc = x * p_ref[7:8, :]
    for s in (1, 2, 3):
        xc = xc + _shift_down(x, before, s) * p_ref[7 - s:8 - s, :]
    return xc + p_ref[0:1, :]


def _lru_block_rows(t):
    return 512 if t >= 1024 else t // 2


def _lru_fwd(xl, p, wrg2, wig2):
    t = xl.shape[0]
    tb = _lru_block_rows(t)

    def body(xl_ref, p_ref, wrg_ref, wig_ref, h_ref, x_tail, h_carry):
        tt = pl.program_id(1)

        @pl.when(tt == 0)
        def _():
            x_tail[...] = jnp.zeros_like(x_tail)
            h_carry[...] = jnp.zeros_like(h_carry)

        x = xl_ref[...]
        xc = _conv_taps(x, x_tail[...], p_ref)
        x_tail[...] = x[tb - 8:tb]
        _, r, ig, ls, a, mult = _lru_gates(xc, p_ref, wrg_ref[0], wig_ref[0])
        u = mult * ig * xc
        row = lax.broadcasted_iota(jnp.int32, (tb, LRU_GROUP), 0)
        s = 1
        while s < tb:
            keep = row >= s
            u = jnp.where(keep, a * pltpu.roll(u, s, 0) + u, u)
            a = jnp.where(keep, a * pltpu.roll(a, s, 0), a)
            s *= 2
        h = u + a * h_carry[0:1, :]
        h_ref[...] = h
        h_carry[...] = jnp.broadcast_to(h[tb - 1:tb], h_carry.shape)

    blk = pl.BlockSpec((tb, LRU_GROUP), lambda g, tt: (tt, g))
    par = pl.BlockSpec((8, LRU_GROUP), lambda g, tt: (0, g))
    wsp = pl.BlockSpec((1, LRU_GROUP, LRU_GROUP), lambda g, tt: (g, 0, 0))
    return pl.pallas_call(
        body, name="lru_fwd", grid=(N_LRU_GROUP, t // tb), in_specs=[blk, par, wsp, wsp], out_specs=blk,
        out_shape=jax.ShapeDtypeStruct((t, D_LRU), F32),
        scratch_shapes=[pltpu.VMEM((8, LRU_GROUP), F32), pltpu.VMEM((8, LRU_GROUP), F32)],
        compiler_params=_params(2),
    )(xl, p, wrg2, wig2)


def _lru_bwd(dy, h, xl, gl, p, wrg2, wig2):
    t = xl.shape[0]
    tb = _lru_block_rows(t)
    n_tb = t // tb
    tb8 = tb // 8

    def body(dy_ref, h_ref, hprev_ref, xl_ref, xprev_ref, gl_ref, p_ref, wrg_ref, wig_ref,
             dxl_ref, dgl_ref, dp_ref, dwrg_ref, dwig_ref, g_carry, a_carry, dxc_head):
        step = pl.program_id(1)
        tt = n_tb - 1 - step
        first = step == 0

        _zero_at_first(first, g_carry, a_carry, dxc_head, dp_ref, dwrg_ref, dwig_ref)

        has_prev = (tt > 0).astype(F32)
        x = xl_ref[...]
        x_before = xprev_ref[...] * has_prev
        xs = [_shift_down(x, x_before, s) for s in range(4)]
        xc = xs[0] * p_ref[7:8, :] + xs[1] * p_ref[6:7, :] + xs[2] * p_ref[5:6, :] + xs[3] * p_ref[4:5, :] + p_ref[0:1, :]
        wrg = wrg_ref[0]
        wig = wig_ref[0]
        xcb, r, ig, ls, a, mult = _lru_gates(xc, p_ref, wrg, wig)

        hh = h_ref[...]
        h_m1 = _shift_down(hh, hprev_ref[...] * has_prev, 1)
        ge, dge = _gelu(gl_ref[...])
        dy = dy_ref[...]
        dgl_ref[...] = dy * hh * dge
        dh = dy * ge

        b = _shift_up(a, a_carry[...], 1)
        row = lax.broadcasted_iota(jnp.int32, (tb, LRU_GROUP), 0)
        g = dh
        s = 1
        while s < tb:
            keep = row < tb - s
            g = jnp.where(keep, b * pltpu.roll(g, tb - s, 0) + g, g)
            b = jnp.where(keep, b * pltpu.roll(b, tb - s, 0), b)
            s *= 2
        g = g + b * g_carry[0:1, :]
        g_carry[...] = jnp.broadcast_to(g[0:1], g_carry.shape)
        a_carry[...] = jnp.broadcast_to(a[0:1], a_carry.shape)

        da = g * h_m1
        dmult = g * ig * xc
        dig = g * mult * xc
        dxc = g * mult * ig
        dlog_a = da * a - dmult * (a * a) / mult
        dr = dlog_a * (LRU_C * ls)
        dls = jnp.sum(dlog_a * (LRU_C * r), axis=0, keepdims=True)
        dlam = dls * _sigmoid(-p_ref[3:4, :])
        dpre_r = dr * r * (1.0 - r)
        dpre_i = dig * ig * (1.0 - ig)
        dprb = dpre_r.astype(BF16)
        dpib = dpre_i.astype(BF16)
        dxc = dxc + _dot_nt(dprb, wrg) + _dot_nt(dpib, wig)
        dwrg_ref[0] += _dot_tn(xcb, dprb)
        dwig_ref[0] += _dot_tn(xcb, dpib)

        after = dxc_head[...]
        dxl = dxc * p_ref[7:8, :]
        for s in (1, 2, 3):
            dxl = dxl + _shift_up(dxc, after, s) * p_ref[7 - s:8 - s, :]
        dxl_ref[...] = dxl
        dxc_head[...] = dxc[0:8]

        rows = [jnp.sum(dxc, axis=0, keepdims=True), jnp.sum(dpre_r, axis=0, keepdims=True),
                jnp.sum(dpre_i, axis=0, keepdims=True), dlam]
        rows += [jnp.sum(dxc * xs[3 - k], axis=0, keepdims=True) for k in range(4)]
        dp_ref[...] += jnp.concatenate(rows, axis=0)

    blk = pl.BlockSpec((tb, LRU_GROUP), lambda g, s: (n_tb - 1 - s, g))
    prev8 = pl.BlockSpec((8, LRU_GROUP), lambda g, s: (jnp.maximum((n_tb - 1 - s) * tb8 - 1, 0), g))
    par = pl.BlockSpec((8, LRU_GROUP), lambda g, s: (0, g))
    wsp = pl.BlockSpec((1, LRU_GROUP, LRU_GROUP), lambda g, s: (g, 0, 0))
    return pl.pallas_call(
        body, name="lru_bwd", grid=(N_LRU_GROUP, n_tb),
        in_specs=[blk, blk, prev8, blk, prev8, blk, par, wsp, wsp], out_specs=[blk, blk, par, wsp, wsp],
        out_shape=[jax.ShapeDtypeStruct((t, D_LRU), F32), jax.ShapeDtypeStruct((t, D_LRU), F32),
                   jax.ShapeDtypeStruct((8, D_LRU), F32),
                   jax.ShapeDtypeStruct((N_LRU_GROUP, LRU_GROUP, LRU_GROUP), F32),
                   jax.ShapeDtypeStruct((N_LRU_GROUP, LRU_GROUP, LRU_GROUP), F32)],
        scratch_shapes=[pltpu.VMEM((8, LRU_GROUP), F32)] * 3,
        compiler_params=_params(2),
    )(dy, h, h, xl, xl, gl, p, wrg2, wig2)


def _attn_bias(first_block):
    qi = jnp.bitwise_and(lax.broadcasted_iota(jnp.int32, (4 * BLOCK_Q, 2 * BLOCK_Q), 0), BLOCK_Q - 1)
    kj = lax.broadcasted_iota(jnp.int32, (4 * BLOCK_Q, 2 * BLOCK_Q), 1)
    rel = qi + BLOCK_Q - kj
    mask = (rel >= 0) & (rel < BLOCK_Q)
    if first_block:
        mask = mask & (kj >= BLOCK_Q)
    return jnp.where(mask, 0.0, MASK_VALUE)


def _sink_column(sinks):
    hrow = lax.broadcasted_iota(jnp.int32, (4 * BLOCK_Q, 1), 0)
    return jnp.where(hrow < BLOCK_Q, sinks[0],
                     jnp.where(hrow < 2 * BLOCK_Q, sinks[1], jnp.where(hrow < 3 * BLOCK_Q, sinks[2], sinks[3])))


def _attn_scores(qv, kvv, n, bias, sk, lo):
    r0 = pl.multiple_of(n * BLOCK_Q, BLOCK_Q)
    rp = pl.multiple_of(jnp.maximum(n - 1, 0) * BLOCK_Q, BLOCK_Q)
    kvb = jnp.concatenate([kvv[pl.ds(rp, BLOCK_Q), :], kvv[pl.ds(r0, BLOCK_Q), :]], axis=0)
    k2 = kvb[:, 0:128]
    v2 = kvb[:, 128:256]
    qs = _stack_heads(qv[pl.ds(r0, BLOCK_Q), :], lo)
    s = _dot_nt(qs, k2) * ATTN_SCALE + bias
    m = jnp.maximum(jnp.max(s, axis=-1, keepdims=True), sk)
    e = jnp.exp(s - m)
    es = jnp.exp(sk - m)
    inv = 1.0 / (jnp.sum(e, axis=-1, keepdims=True) + es)
    return r0, rp, qs, k2, v2, e * inv, es * inv


def _stack_heads(pair2, lo):
    p0 = pair2[:, 0:128]
    p1 = pair2[:, 128:256]
    z = jnp.zeros_like(p0)
    return jnp.concatenate([jnp.where(lo, p0, z), jnp.where(lo, z, p0), jnp.where(lo, p1, z), jnp.where(lo, z, p1)], axis=0)


def _unstack_heads(st, lo):
    b = BLOCK_Q
    return jnp.concatenate([jnp.where(lo, st[0:b], st[b:2 * b]), jnp.where(lo, st[2 * b:3 * b], st[3 * b:4 * b])], axis=1)


def _attn_fwd(q, kv, sinks):
    t = q.shape[0]
    n_blk = t // BLOCK_Q

    def body(q_hbm, kv_hbm, s_ref, o_hbm, q2, kvv, o2, bias0, bias, sem):
        lo = lax.broadcasted_iota(jnp.int32, (BLOCK_Q, 128), 1) < HEAD_DIM
        cols = [pl.ds(256 * g, 256) for g in range(2)]
        loads = [pltpu.make_async_copy(kv_hbm, kvv, sem.at[0])]
        loads += [pltpu.make_async_copy(q_hbm.at[:, cols[g]], q2.at[g], sem.at[1 + g]) for g in range(2)]
        stores = [pltpu.make_async_copy(o2.at[g], o_hbm.at[:, cols[g]], sem.at[3 + g]) for g in range(2)]
        for cp in loads:
            cp.start()
        bias0[...] = _attn_bias(True)
        bias[...] = _attn_bias(False)
        loads[0].wait()
        for g in range(2):
            loads[1 + g].wait()
            qv, ov = q2.at[g], o2.at[g]
            sk = _sink_column([s_ref[0, HEAD_ORDER[4 * g + i]] for i in range(4)])

            def block(n, bias_ref):
                r0, _, _, _, v2, prob, _ = _attn_scores(qv, kvv, n, bias_ref[...], sk, lo)
                ov[pl.ds(r0, BLOCK_Q), :] = _unstack_heads(_dot(prob.astype(BF16), v2), lo)

            block(0, bias0)

            def later(n, carry):
                block(n, bias)
                return carry

            lax.fori_loop(1, n_blk, later, 0, unroll=2)
            stores[g].start()
        for cp in stores:
            cp.wait()

    return pl.pallas_call(
        body, name="attn_fwd", in_specs=[ANY, ANY, SMEM], out_specs=ANY,
        out_shape=jax.ShapeDtypeStruct((t, D_ATTN), F32),
        scratch_shapes=[pltpu.VMEM((2, t, 256), BF16), pltpu.VMEM((t, 256), BF16), pltpu.VMEM((2, t, 256), F32),
                        pltpu.VMEM((4 * BLOCK_Q, 2 * BLOCK_Q), F32), pltpu.VMEM((4 * BLOCK_Q, 2 * BLOCK_Q), F32),
                        pltpu.SemaphoreType.DMA((5,))],
        compiler_params=_params(),
    )(q, kv, sinks)


def _attn_bwd(q, kv, do, sinks, exchange=None):
    t = q.shape[0]
    n_blk = t // BLOCK_Q
    host = _Host(exchange)

    def body(*refs):
        own, ex_refs = host.split(refs, 4, 3, 9)
        q_hbm, kv_hbm, do_hbm, s_ref, dq_hbm, dkv_hbm, dsink_ref, q2, kvv, do2, dqv, dkvv, ds_acc, bias0, bias, sem = own
        host.phase(0, ex_refs)
        lo = lax.broadcasted_iota(jnp.int32, (BLOCK_Q, 128), 1) < HEAD_DIM
        loads = [pltpu.make_async_copy(kv_hbm, kvv, sem.at[0])]
        for g in range(2):
            loads += [pltpu.make_async_copy(src.at[:, pl.ds(256 * g, 256)], dst.at[g], sem.at[1 + 2 * g + i])
                      for i, (src, dst) in enumerate(((q_hbm, q2), (do_hbm, do2)))]
        for cp in loads:
            cp.start()
        bias0[...] = _attn_bias(True)
        bias[...] = _attn_bias(False)
        loads[0].wait()
        for g in range(2):
            cols = pl.ds(256 * g, 256)
            for cp in loads[1 + 2 * g:3 + 2 * g]:
                cp.wait()
            qv, dov = q2.at[g], do2.at[g]
            heads = [HEAD_ORDER[4 * g + i] for i in range(4)]
            sk = _sink_column([s_ref[0, h] for h in heads])
            ds_acc[...] = jnp.zeros_like(ds_acc)

            def block(n, bias_ref, has_prev):
                r0, rp, qs, k2, v2, prob, psink = _attn_scores(qv, kvv, n, bias_ref[...], sk, lo)
                pb = prob.astype(BF16)
                dos = _stack_heads(dov[pl.ds(r0, BLOCK_Q), :], lo)
                dp = _dot_nt(dos, v2)
                dsum = jnp.sum(prob * dp, axis=-1, keepdims=True)
                dsb = (prob * (dp - dsum) * ATTN_SCALE).astype(BF16)
                ds_acc[...] -= psink * dsum
                dqv[pl.ds(r0, BLOCK_Q), :] = _unstack_heads(_dot(dsb, k2), lo).astype(BF16)
                dk2 = _dot_tn(dsb, qs)
                dv2 = _dot_tn(pb, dos)
                cur = jnp.concatenate([dk2[BLOCK_Q:], dv2[BLOCK_Q:]], axis=1)
                if g == 0:
                    dkvv[pl.ds(r0, BLOCK_Q), :] = cur
                else:
                    dkvv[pl.ds(r0, BLOCK_Q), :] += cur
                if has_prev:
                    dkvv[pl.ds(rp, BLOCK_Q), :] += jnp.concatenate([dk2[:BLOCK_Q], dv2[:BLOCK_Q]], axis=1)

            block(0, bias0, False)

            def later(n, carry):
                block(n, bias, True)
                return carry

            lax.fori_loop(1, n_blk, later, 0, unroll=2)
            for i, h in enumerate(heads):
                tot = jnp.sum(ds_acc[BLOCK_Q * i:BLOCK_Q * (i + 1), :], axis=0, keepdims=True)
                dsink_ref[h:h + 1, :] = jnp.broadcast_to(tot, (1, 128))
            store = pltpu.make_async_copy(dqv, dq_hbm.at[:, cols], sem.at[5])
            store.start()
            store.wait()
        store = pltpu.make_async_copy(dkvv, dkv_hbm, sem.at[6])
        store.start()
        store.wait()
        if exchange is not None:
            for p in range(1, exchange.n_phases):
                host.phase(p, ex_refs)

    res = pl.pallas_call(
        body, name="attn_bwd", in_specs=[ANY, ANY, ANY, SMEM] + host.in_specs,
        out_specs=[ANY, ANY, pl.BlockSpec(memory_space=pltpu.VMEM)] + host.out_specs,
        out_shape=[jax.ShapeDtypeStruct((t, D_ATTN), BF16), jax.ShapeDtypeStruct((t, 256), F32),
                   jax.ShapeDtypeStruct((8, 128), F32)] + host.out_shape,
        scratch_shapes=[pltpu.VMEM((2, t, 256), BF16), pltpu.VMEM((t, 256), BF16), pltpu.VMEM((2, t, 256), BF16),
                        pltpu.VMEM((t, 256), BF16), pltpu.VMEM((t, 256), F32), pltpu.VMEM((4 * BLOCK_Q, 1), F32),
                        pltpu.VMEM((4 * BLOCK_Q, 2 * BLOCK_Q), F32), pltpu.VMEM((4 * BLOCK_Q, 2 * BLOCK_Q), F32),
                        pltpu.SemaphoreType.DMA((7,))] + host.scratch,
        compiler_params=_params(),
    )(q, kv, do, sinks, *host.args)
    return (*res[:3], list(res[3:]))


def _mix_out_fwd(x1, h, gl, o, g_lru, g_attn, g_post, w_o):
    t = x1.shape[0]
    tm = _token_tile(t)

    def body(x_ref, h_ref, gl_ref, o_ref, g1_ref, g2_ref, gp_ref, w_ref, x2_ref, m_ref):
        y = h_ref[...] * _gelu(gl_ref[...])[0]
        yn1 = _rms_fwd(y, g1_ref[...]).astype(BF16)
        yn2 = _rms_fwd(o_ref[...], g2_ref[...]).astype(BF16)
        m = _dot(yn1, w_ref[0:512, :]) + _dot(yn2, w_ref[512:1024, :])
        m_ref[...] = m
        x2_ref[...] = x_ref[...] + _rms_fwd(m, gp_ref[...])

    tok = pl.BlockSpec((tm, D_MODEL), lambda i: (i, 0))
    half = pl.BlockSpec((tm, 512), lambda i: (i, 0))
    vec = pl.BlockSpec((1, D_MODEL), lambda i: (0, 0))
    hvec = pl.BlockSpec((1, 512), lambda i: (0, 0))
    return pl.pallas_call(
        body, name="mix_out_fwd", grid=(t // tm,),
        in_specs=[tok, half, half, half, hvec, hvec, vec, pl.BlockSpec((D_MODEL, D_MODEL), lambda i: (0, 0))],
        out_specs=[tok, tok],
        out_shape=[jax.ShapeDtypeStruct((t, D_MODEL), F32), jax.ShapeDtypeStruct((t, D_MODEL), F32)],
        compiler_params=_params(1),
    )(x1, h, gl, o, g_lru, g_attn, g_post, w_o)


def _mix_out_bwd(dx2, m, h, gl, o, g_lru, g_attn, g_post, w_o):
    t = dx2.shape[0]
    tm = _token_tile(t)

    def body(dx_ref, m_ref, h_ref, gl_ref, o_ref, g1_ref, g2_ref, gp_ref, w_ref,
             dy_ref, do_ref, dw_ref, dgp_ref, dg1_ref, dg2_ref):
        _zero_at_first(pl.program_id(0) == 0, dw_ref, dgp_ref, dg1_ref, dg2_ref)
        dm, dgp = _rms_bwd(m_ref[...], gp_ref[...], dx_ref[...])
        dmb = dm.astype(BF16)
        y = h_ref[...] * _gelu(gl_ref[...])[0]
        o = o_ref[...]
        yn1 = _rms_fwd(y, g1_ref[...]).astype(BF16)
        yn2 = _rms_fwd(o, g2_ref[...]).astype(BF16)
        dw_ref[0:512, :] += _dot_tn(yn1, dmb)
        dw_ref[512:1024, :] += _dot_tn(yn2, dmb)
        dy, dg1 = _rms_bwd(y, g1_ref[...], _dot_nt(dmb, w_ref[0:512, :]))
        do, dg2 = _rms_bwd(o, g2_ref[...], _dot_nt(dmb, w_ref[512:1024, :]))
        dy_ref[...] = dy
        do_ref[...] = do.astype(BF16)
        dgp_ref[...] += dgp
        dg1_ref[...] += dg1
        dg2_ref[...] += dg2

    tok = pl.BlockSpec((tm, D_MODEL), lambda i: (i, 0))
    half = pl.BlockSpec((tm, 512), lambda i: (i, 0))
    vec = pl.BlockSpec((1, D_MODEL), lambda i: (0, 0))
    hvec = pl.BlockSpec((1, 512), lambda i: (0, 0))
    mat = pl.BlockSpec((D_MODEL, D_MODEL), lambda i: (0, 0))
    return pl.pallas_call(
        body, name="mix_out_bwd", grid=(t // tm,),
        in_specs=[tok, tok, half, half, half, hvec, hvec, vec, mat],
        out_specs=[half, half, mat, vec, hvec, hvec],
        out_shape=[jax.ShapeDtypeStruct((t, 512), F32), jax.ShapeDtypeStruct((t, 512), BF16),
                   jax.ShapeDtypeStruct((D_MODEL, D_MODEL), F32), jax.ShapeDtypeStruct((1, D_MODEL), F32),
                   jax.ShapeDtypeStruct((1, 512), F32), jax.ShapeDtypeStruct((1, 512), F32)],
        compiler_params=_params(1),
    )(dx2, m, h, gl, o, g_lru, g_attn, g_post, w_o)


def _mix_in_bwd(dx2, x1, g, dxl, dgl, dq, dkv, w_in, f1, g_post1):
    t = x1.shape[0]
    tm = _token_tile(t)

    def body(dx2_ref, x_ref, g_ref, dxl_ref, dgl_ref, dq_ref, dkv_ref, w_ref, f1_ref, gp1_ref,
             dx1_ref, dw_ref, dg_ref, df1_ref, dgp1_ref):
        _zero_at_first(pl.program_id(0) == 0, dw_ref, dg_ref, dgp1_ref)
        x = x_ref[...]
        nb = _rms_fwd(x, g_ref[...]).astype(BF16)
        dproj = jnp.concatenate([dxl_ref[...].astype(BF16), dgl_ref[...].astype(BF16), dq_ref[...],
                                 dkv_ref[...].astype(BF16)], axis=1)
        dw_ref[...] += _dot_tn(nb, dproj)
        dx, dg = _rms_bwd(x, g_ref[...], _dot(dproj, w_ref[...]))
        dx1 = dx2_ref[...] + dx
        dx1_ref[...] = dx1
        dg_ref[...] += dg
        df1, dgp1 = _rms_bwd(f1_ref[...], gp1_ref[...], 0.5 * dx1)
        df1_ref[...] = df1.astype(BF16)
        dgp1_ref[...] += dgp1

    tok = pl.BlockSpec((tm, D_MODEL), lambda i: (i, 0))
    half = pl.BlockSpec((tm, 512), lambda i: (i, 0))
    vec = pl.BlockSpec((1, D_MODEL), lambda i: (0, 0))
    mat = pl.BlockSpec((D_IN, D_MODEL), lambda i: (0, 0))
    dmat = pl.BlockSpec((D_MODEL, D_IN), lambda i: (0, 0))
    quarter = pl.BlockSpec((tm, 256), lambda i: (i, 0))
    return pl.pallas_call(
        body, name="mix_in_bwd", grid=(t // tm,),
        in_specs=[tok, tok, vec, half, half, half, quarter, mat, tok, vec], out_specs=[tok, dmat, vec, tok, vec],
        out_shape=[jax.ShapeDtypeStruct((t, D_MODEL), F32), jax.ShapeDtypeStruct((D_MODEL, D_IN), F32),
                   jax.ShapeDtypeStruct((1, D_MODEL), F32), jax.ShapeDtypeStruct((t, D_MODEL), BF16),
                   jax.ShapeDtypeStruct((1, D_MODEL), F32)],
        compiler_params=_params(1),
    )(dx2, x1, g, dxl, dgl, dq, dkv, w_in, f1, g_post1)


def _half(rows):
    return rows // 2


def _chip_sums(grads, from_sibling, other, name):
    n_arr = len(grads)

    def body(other_ref, *refs):
        for a in range(n_arr):
            refs[2 * n_arr + a][0] = (refs[2 * a][0, 0] + refs[2 * a + 1][0]).astype(BF16)

    in_specs, out_specs, out_shape, args = [], [], [], []
    for g, s in zip(grads, from_sibling):
        _, rows, cols = g.shape
        tr = _half(rows)
        in_specs += [pl.BlockSpec((1, 1, tr, cols), lambda j, i, other: (other[j], other[3], i, 0)),
                     pl.BlockSpec((1, tr, cols), lambda j, i, other: (other[j], i, 0))]
        out_specs.append(pl.BlockSpec((1, tr, cols), lambda j, i, other: (j, i, 0)))
        out_shape.append(jax.ShapeDtypeStruct((3, rows, cols), BF16))
        args += [g.reshape(4, 2, rows, cols), s]
    grid_spec = pltpu.PrefetchScalarGridSpec(num_scalar_prefetch=1, grid=(3, 2), in_specs=in_specs, out_specs=out_specs)
    return pl.pallas_call(body, name=name, grid_spec=grid_spec, out_shape=out_shape, compiler_params=_params(2))(other, *args)


def _adamw(w, g, m, v):
    m = ADAM_B1 * m + (1.0 - ADAM_B1) * g
    v = ADAM_B2 * v + (1.0 - ADAM_B2) * (g * g)
    m_hat = m / (1.0 - ADAM_B1 ** ADAM_STEP)
    v_hat = v / (1.0 - ADAM_B2 ** ADAM_STEP)
    delta = -ADAM_LR * (m_hat / (jnp.sqrt(v_hat) + ADAM_EPS) + ADAM_WD * w)
    return delta, m, v


def _shard_updates(grads, from_sibling, from_chips, w, m, v, place, name, transposed):
    n_arr = len(grads)

    def total(g_ref, s_ref, c_ref):
        g = g_ref[0, 0] + s_ref[0]
        g = g + c_ref[0].astype(F32)
        g = g + c_ref[1].astype(F32)
        return g + c_ref[2].astype(F32)

    part_specs, parts, flat, shapes = [], [], [], []
    for g in grads:
        _, rows, cols = g.shape
        tr = _half(rows)
        part_specs.append([pl.BlockSpec((1, 1, tr, cols), lambda i, place: (place[0], place[1], i, 0)),
                           pl.BlockSpec((1, tr, cols), lambda i, place: (place[0], i, 0)),
                           pl.BlockSpec((3, tr, cols), lambda i, place: (0, i, 0))])
        flat.append(pl.BlockSpec((tr, cols), lambda i, place: (i, 0)))
        shapes.append(jax.ShapeDtypeStruct((rows, cols), F32))
    for g, s, c in zip(grads, from_sibling, from_chips):
        parts += [g.reshape(4, 2, *g.shape[1:]), s, c]

    if not transposed:
        def body(place_ref, *refs):
            ins, wmv, outs = refs[:3 * n_arr], refs[3 * n_arr:6 * n_arr], refs[6 * n_arr:]
            for a in range(n_arr):
                g = total(*ins[3 * a:3 * a + 3])
                outs[4 * a][...] = g
                outs[4 * a + 1][...], outs[4 * a + 2][...], outs[4 * a + 3][...] = _adamw(
                    wmv[3 * a][...], g, wmv[3 * a + 1][...], wmv[3 * a + 2][...])

        grid_spec = pltpu.PrefetchScalarGridSpec(
            num_scalar_prefetch=1, grid=(2,),
            in_specs=[sp for specs in part_specs for sp in specs] + [f for f in flat for _ in range(3)],
            out_specs=[f for f in flat for _ in range(4)])
        res = pl.pallas_call(body, name=name, grid_spec=grid_spec, out_shape=[sh for sh in shapes for _ in range(4)],
                             compiler_params=_params(1))(place, *parts, *[x for wmv in zip(w, m, v) for x in wmv])
        return [tuple(res[4 * a:4 * a + 4]) for a in range(n_arr)]

    def sum_body(place_ref, *refs):
        for a in range(n_arr):
            refs[3 * n_arr + a][...] = total(*refs[3 * a:3 * a + 3])

    grid_spec = pltpu.PrefetchScalarGridSpec(num_scalar_prefetch=1, grid=(2,),
                                             in_specs=[sp for specs in part_specs for sp in specs], out_specs=flat)
    sums = pl.pallas_call(sum_body, name=name + "_sum", grid_spec=grid_spec, out_shape=shapes,
                          compiler_params=_params(1))(place, *parts)
    turned = [jnp.transpose(g, (1, 0)) for g in sums]

    def adam_body(*refs):
        ins, outs = refs[:4 * n_arr], refs[4 * n_arr:]
        for a in range(n_arr):
            outs[3 * a][...], outs[3 * a + 1][...], outs[3 * a + 2][...] = _adamw(
                ins[4 * a + 1][...], ins[4 * a][...], ins[4 * a + 2][...], ins[4 * a + 3][...])

    blks = [pl.BlockSpec((_half(g.shape[0]), g.shape[1]), lambda i: (i, 0)) for g in turned]
    res = pl.pallas_call(
        adam_body, name=name + "_adam", grid=(2,), in_specs=[b for b in blks for _ in range(4)],
        out_specs=[b for b in blks for _ in range(3)],
        out_shape=[jax.ShapeDtypeStruct(g.shape, F32) for g in turned for _ in range(3)], compiler_params=_params(1),
    )(*[x for gwmv in zip(turned, w, m, v) for x in gwmv])
    return [(turned[a], *res[3 * a:3 * a + 3]) for a in range(n_arr)]


GAINS = ("ffn1_pre_g", "ffn1_post_g", "mix_pre_g", "mix_post_g", "ffn2_pre_g", "ffn2_post_g")
HALVES = ("conv_b", "b_rg", "b_ig", "lru_lambda", "g_lru_out", "g_attn_out")
GATES = ("w_rg", "w_ig")
SMALL = GAINS + HALVES + GATES + ("sinks", "conv_w")


def _small_update(gathered, w, m, v):
    n_small = len(SMALL)

    def body(*refs):
        ga_ref, gb_ref, gc_ref, gd_ref, g0_ref = refs[:5]
        wmv = refs[5:5 + 3 * n_small]
        outs = refs[5 + 3 * n_small:5 + 7 * n_small]
        loss_ref = refs[5 + 7 * n_small]

        def total(ref):
            s = ref[0]
            for d in range(1, N_DEV):
                s = s + ref[d]
            return s

        sa, sb, sc, sd = total(ga_ref), total(gb_ref), total(gc_ref), total(gd_ref)
        grads = {}
        for i, k in enumerate(GAINS):
            grads[k] = sa[i:i + 1]
        grads[GAINS[0]] = total(g0_ref)
        for i, k in enumerate(HALVES):
            grads[k] = sb[i:i + 1]
        grads["w_rg"], grads["w_ig"] = sc[0:512], sc[512:1024]
        grads["sinks"] = sd[4:5, 0:8]
        grads["conv_w"] = sd[0:4]
        for i, k in enumerate(SMALL):
            g = grads[k]
            outs[4 * i][...] = g
            outs[4 * i + 1][...], outs[4 * i + 2][...], outs[4 * i + 3][...] = _adamw(
                wmv[3 * i][...], g, wmv[3 * i + 1][...], wmv[3 * i + 2][...])
        loss_ref[...] = jnp.broadcast_to(sd[5:6, 0:128], loss_ref.shape)

    operands = list(gathered)
    out_shape = []
    for k in SMALL:
        operands += [w[k], m[k], v[k]]
        out_shape += [jax.ShapeDtypeStruct(w[k].shape, F32)] * 4
    out_shape.append(jax.ShapeDtypeStruct((8, 128), F32))
    res = pl.pallas_call(body, name="small_update", out_shape=out_shape, compiler_params=_params())(*operands)
    parts = [{k: res[4 * i + j] for i, k in enumerate(SMALL)} for j in range(4)]
    return (*parts, res[-1])


def _reorder_heads(a, axis, start, order):
    def slab(h):
        return lax.slice_in_dim(a, start + HEAD_DIM * h, start + HEAD_DIM * (h + 1), axis=axis)

    parts = [lax.slice_in_dim(a, 0, start, axis=axis)] + [slab(h) for h in order]
    parts.append(lax.slice_in_dim(a, start + 8 * HEAD_DIM, a.shape[axis], axis=axis))
    return jnp.concatenate(parts, axis=axis)


HEAD_ORDER_INVERSE = tuple(HEAD_ORDER.index(h) for h in range(8))


def _pair_block_diag(w):
    w = w.reshape(N_LRU_GROUP, 2, 64, 64)
    z = jnp.zeros((N_LRU_GROUP, 64, 64), w.dtype)
    top = jnp.concatenate([w[:, 0], z], axis=2)
    bot = jnp.concatenate([z, w[:, 1]], axis=2)
    return jnp.concatenate([top, bot], axis=1)


def _pair_block_diag_grad(dw2):
    return jnp.stack([dw2[:, :64, :64], dw2[:, 64:, 64:]], axis=1).reshape(512, 64)


def kernel(x, ffn1_pre_g, ffn1_w_gu, ffn1_w_down, ffn1_post_g, mix_pre_g, w_in, conv_w, conv_b, w_rg, b_rg, w_ig, b_ig, lru_lambda, sinks, g_lru_out, g_attn_out, w_o, mix_post_g, ffn2_pre_g, ffn2_w_gu, ffn2_w_down, ffn2_post_g, loss_target, m_ffn1_pre_g, m_ffn1_w_gu, m_ffn1_w_down, m_ffn1_post_g, m_mix_pre_g, m_w_in, m_conv_w, m_conv_b, m_w_rg, m_b_rg, m_w_ig, m_b_ig, m_lru_lambda, m_sinks, m_g_lru_out, m_g_attn_out, m_w_o, m_mix_post_g, m_ffn2_pre_g, m_ffn2_w_gu, m_ffn2_w_down, m_ffn2_post_g, v_ffn1_pre_g, v_ffn1_w_gu, v_ffn1_w_down, v_ffn1_post_g, v_mix_pre_g, v_w_in, v_conv_w, v_conv_b, v_w_rg, v_b_rg, v_w_ig, v_b_ig, v_lru_lambda, v_sinks, v_g_lru_out, v_g_attn_out, v_w_o, v_mix_post_g, v_ffn2_pre_g, v_ffn2_w_gu, v_ffn2_w_down, v_ffn2_post_g):
    args = dict(locals())
    names = ["ffn1_pre_g", "ffn1_w_gu", "ffn1_w_down", "ffn1_post_g", "mix_pre_g", "w_in", "conv_w", "conv_b", "w_rg",
             "b_rg", "w_ig", "b_ig", "lru_lambda", "sinks", "g_lru_out", "g_attn_out", "w_o", "mix_post_g",
             "ffn2_pre_g", "ffn2_w_gu", "ffn2_w_down", "ffn2_post_g"]
    big = ["ffn1_w_gu", "ffn1_w_down", "w_in", "w_o", "ffn2_w_gu", "ffn2_w_down"]
    w = {k: args[k] for k in names}
    mom = {k: args["m_" + k] for k in names}
    var = {k: args["v_" + k] for k in names}
    t = x.shape[1]
    xs = x.reshape(t, D_MODEL)
    target = loss_target.reshape(t, D_MODEL)
    cx, cy, cc = _coords()
    me = 4 * cx + 2 * cy + cc
    other = jnp.stack([2 * (1 - cx) + cy, 2 * cx + (1 - cy), 2 * (1 - cx) + (1 - cy), cc]).astype(jnp.int32)
    place = jnp.stack([2 * cx + cy, cc]).astype(jnp.int32)

    transposed = ("ffn1_w_gu", "w_in", "ffn2_w_gu")

    def shard_view(a, k):
        return jnp.transpose(a[0], (1, 0)) if k in transposed else a[0]

    def shard_unview(a, k):
        return (jnp.transpose(a, (1, 0)) if k in transposed else a)[None]

    shard2d = {k: shard_view(w[k], k) for k in big}
    shard_bf = {k: shard2d[k].astype(BF16) for k in big}
    conv_pad = jnp.pad(conv_w.reshape(4, 64), ((0, 4), (0, 64)))
    (first_w,) = _run_exchanges([_Gather([shard_bf["ffn1_w_gu"], shard_bf["ffn1_w_down"]], routed=True)], "all_gather_ffn1")
    wgu1 = first_w[0].reshape(2, N_CHUNK, CHUNK, D_MODEL)
    wd1 = first_w[1].reshape(N_CHUNK, CHUNK, D_MODEL)
    rest = _Gather([shard_bf["w_in"], shard_bf["w_o"], shard_bf["ffn2_w_gu"], shard_bf["ffn2_w_down"], conv_pad])

    x1, f1, n1, gu1, gathered = _ffn_fwd(xs, ffn1_pre_g, wgu1, wd1, ffn1_post_g, None, "ffn1_fwd", rest)
    w_in_full = _reorder_heads(gathered[0].reshape(D_IN, D_MODEL), 0, 2 * D_LRU, HEAD_ORDER)
    w_o_full = _reorder_heads(gathered[1].reshape(D_MODEL, D_MODEL), 0, D_LRU, HEAD_ORDER)
    g_attn_heads = _reorder_heads(g_attn_out, 1, 0, HEAD_ORDER)
    wgu2 = gathered[2].reshape(2, N_CHUNK, CHUNK, D_MODEL)
    wd2 = gathered[3].reshape(N_CHUNK, CHUNK, D_MODEL)
    conv_w_full = jnp.transpose(gathered[4][:, 0:4, 0:64], (1, 0, 2)).reshape(4, D_LRU)
    p_lru = jnp.concatenate([conv_b, b_rg, b_ig, lru_lambda, conv_w_full], axis=0)
    wrg2 = _pair_block_diag(w_rg[0]).astype(BF16)
    wig2 = _pair_block_diag(w_ig[0]).astype(BF16)
    xl, gl, q, kv = _mix_in_fwd(x1, mix_pre_g, w_in_full)
    h = _lru_fwd(xl, p_lru, wrg2, wig2)
    o = _attn_fwd(q, kv, sinks)
    x2, mo = _mix_out_fwd(x1, h, gl, o, g_lru_out, g_attn_heads, mix_post_g, w_o_full)
    g = {}
    dx3, n2, df2, gu2, g["ffn2_post_g"], loss_parts, _ = _ffn_fwd(x2, ffn2_pre_g, wgu2, wd2, ffn2_post_g, target, "ffn2_fwd")
    loss_local = jnp.sum(loss_parts[::8, 0])

    partial, from_sibling, from_chips = {}, {}, {}

    def chip_sums(keys):
        return _chip_sums([partial[k] for k in keys], [from_sibling[k] for k in keys], other, "chip_sum_" + keys[0])

    dgu2, dwgu2, dwd2, _ = _ffn_bwd_w(n2, df2, gu2, wd2, "ffn2_bwd_w")
    partial["ffn2_w_gu"] = dwgu2.reshape(N_DEV, D_MODEL, CHUNK)
    partial["ffn2_w_down"] = dwd2.reshape(N_DEV, D_FF // N_DEV, D_MODEL)
    ffn2_keys = ["ffn2_w_gu", "ffn2_w_down"]
    dx2, g["ffn2_pre_g"], got = _ffn_bwd_x(dgu2, wgu2, x2, ffn2_pre_g, dx3, "ffn2_bwd_x",
                                           _SiblingExchange([partial[k] for k in ffn2_keys]))
    from_sibling.update(zip(ffn2_keys, got))
    dy, do, dwo, g["mix_post_g"], g["g_lru_out"], dg_attn_heads = _mix_out_bwd(
        dx2, mo, h, gl, o, g_lru_out, g_attn_heads, mix_post_g, w_o_full)
    g["g_attn_out"] = _reorder_heads(dg_attn_heads, 1, 0, HEAD_ORDER_INVERSE)
    dwo = _reorder_heads(dwo, 0, D_LRU, HEAD_ORDER_INVERSE)
    dq, dkv, dsink, got = _attn_bwd(q, kv, do, sinks, _ChipExchange(chip_sums(ffn2_keys)))
    from_chips.update(zip(ffn2_keys, got))
    dxl, dgl, dp, dwrg2, dwig2 = _lru_bwd(dy, h, xl, gl, p_lru, wrg2, wig2)
    dx1, dwin, g["mix_pre_g"], df1, g["ffn1_post_g"] = _mix_in_bwd(
        dx2, x1, mix_pre_g, dxl, dgl, dq, dkv, w_in_full, f1, ffn1_post_g)
    dwin = _reorder_heads(dwin, 1, 2 * D_LRU, HEAD_ORDER_INVERSE)
    partial["w_in"] = jnp.transpose(dwin.reshape(D_MODEL, N_DEV, D_IN // N_DEV), (1, 0, 2))
    partial["w_o"] = dwo.reshape(N_DEV, D_MODEL // N_DEV, D_MODEL)
    mix_keys = ["w_in", "w_o"]
    (got,) = _run_exchanges([_SiblingExchange([partial[k] for k in mix_keys])], "mix_sibling_exchange")
    from_sibling.update(zip(mix_keys, got))
    dgu1, dwgu1, dwd1, got = _ffn_bwd_w(n1, df1, gu1, wd1, "ffn1_bwd_w", _ChipExchange(chip_sums(mix_keys)))
    from_chips.update(zip(mix_keys, got))
    partial["ffn1_w_gu"] = dwgu1.reshape(N_DEV, D_MODEL, CHUNK)
    partial["ffn1_w_down"] = dwd1.reshape(N_DEV, D_FF // N_DEV, D_MODEL)
    ffn1_keys = ["ffn1_w_gu", "ffn1_w_down"]
    (got,) = _run_exchanges([_SiblingExchange([partial[k] for k in ffn1_keys])], "ffn1_sibling_exchange")
    from_sibling.update(zip(ffn1_keys, got))
    zeros2 = jnp.zeros((2, D_MODEL), F32)
    g_gains = jnp.concatenate([zeros2[:1]] + [g[k] for k in GAINS[1:]] + [zeros2], axis=0)
    g_halves = jnp.concatenate([dp[0:4], g["g_lru_out"], g["g_attn_out"], zeros2[:, :D_LRU]], axis=0)
    g_gates = jnp.concatenate([_pair_block_diag_grad(dwrg2), _pair_block_diag_grad(dwig2)], axis=0)
    g_misc = jnp.concatenate([dp[4:8], jnp.pad(dsink[:, 0].reshape(1, 8), ((0, 0), (0, D_LRU - 8))),
                              jnp.pad(loss_local.reshape(1, 1), ((0, 0), (0, D_LRU - 1))), zeros2[:, :D_LRU]], axis=0)
    dx0, g_first, got = _ffn_bwd_x(dgu1, wgu1, xs, ffn1_pre_g, dx1, "ffn1_bwd_x",
                                   _Both(_ChipExchange(chip_sums(ffn1_keys)), _Gather([g_gains, g_halves, g_gates, g_misc])))
    from_chips.update(zip(ffn1_keys, got[:2]))
    gathered_small = got[2:]

    grads, delta, new_m, new_v = {}, {}, {}, {}
    for name, keys, turned in (("update_column_sharded", transposed, True),
                               ("update_row_sharded", tuple(k for k in big if k not in transposed), False)):
        res = _shard_updates([partial[k] for k in keys], [from_sibling[k] for k in keys], [from_chips[k] for k in keys],
                             [shard2d[k] for k in keys], [shard_view(mom[k], k) for k in keys],
                             [shard_view(var[k], k) for k in keys], place, name, turned)
        for k, out in zip(keys, res):
            grads[k], delta[k], new_m[k], new_v[k] = [shard_unview(r, k) for r in out]

    ((gathered_first,),) = _run_exchanges([_Gather([g_first])], "all_gather_first_gain")
    col = me * 64

    def small_view(vals):
        out = {k: vals[k] for k in GAINS + HALVES + ("sinks",)}
        out.update({k: vals[k].reshape(512, 64) for k in GATES})
        out["conv_w"] = lax.dynamic_update_slice(jnp.zeros((4, D_LRU), F32), vals["conv_w"].reshape(4, 64), (0, col))
        return out

    *small, loss_tile = _small_update([*gathered_small, gathered_first], small_view(w), small_view(mom), small_view(var))
    for dst, part in zip((grads, delta, new_m, new_v), small):
        for k in SMALL:
            if k == "conv_w":
                dst[k] = lax.dynamic_slice(part[k], (0, col), (4, 64)).reshape(conv_w.shape)
            else:
                dst[k] = part[k].reshape(w[k].shape)
    return (loss_tile[0, 0], dx0.reshape(x.shape), *[grads[k] for k in names], *[delta[k] for k in names],
            *[new_m[k] for k in names], *[new_v[k] for k in names])
```

```python
import functools

import jax
import jax.numpy as jnp
from jax import lax
from jax.experimental import pallas as pl
from jax.experimental.pallas import tpu as pltpu

F32 = jnp.float32
BF16 = jnp.bfloat16

D_MODEL = 1024
D_FF = 2816
N_DEV = 8
N_CHUNK = 4
CHUNK = D_FF // N_CHUNK
D_LRU = 512
D_ATTN = 512
LRU_GROUP = 128
N_LRU_GROUP = D_LRU // LRU_GROUP
HEAD_DIM = 64
BLOCK_Q = 128
D_IN = 1792
HEAD_ORDER = (0, 4, 1, 5, 2, 6, 3, 7)
RMS_EPS = 1e-6
LRU_C = 8.0
MASK_VALUE = -1e30
ATTN_SCALE = HEAD_DIM ** -0.5

ADAM_LR = 0.001
ADAM_B1 = 0.9
ADAM_B2 = 0.999
ADAM_EPS = 1e-08
ADAM_WD = 0.01
ADAM_STEP = 10

VMEM_LIMIT_V7X = 56 * 2 ** 20

ANY = pl.BlockSpec(memory_space=pl.ANY)
SMEM = pl.BlockSpec(memory_space=pltpu.SMEM)
MESH = pl.DeviceIdType.MESH


def _params(n_grid=0):
    sem = ("arbitrary",) * n_grid if n_grid else None
    return pltpu.CompilerParams(dimension_semantics=sem, vmem_limit_bytes=VMEM_LIMIT_V7X)


def _dot(a, b):
    return lax.dot_general(a, b, (((1,), (0,)), ((), ())), preferred_element_type=F32)


def _dot_nt(a, b):
    return lax.dot_general(a, b, (((1,), (1,)), ((), ())), preferred_element_type=F32)


def _dot_tn(a, b):
    return lax.dot_general(a, b, (((0,), (0,)), ((), ())), preferred_element_type=F32)


def _sigmoid(x):
    return 1.0 / (1.0 + jnp.exp(-x))


def _rms_fwd(x, g):
    r = lax.rsqrt(jnp.mean(x * x, axis=-1, keepdims=True) + RMS_EPS)
    return x * r * g


def _rms_bwd(x, g, dy):
    r = lax.rsqrt(jnp.mean(x * x, axis=-1, keepdims=True) + RMS_EPS)
    xh = x * r
    dg = jnp.sum(dy * xh, axis=0, keepdims=True)
    dxh = dy * g
    dx = r * (dxh - xh * jnp.mean(dxh * xh, axis=-1, keepdims=True))
    return dx, dg


def _gelu(x):
    c = 0.7978845608028654
    inner = c * (x + 0.044715 * x * x * x)
    th = jnp.tanh(inner)
    ge = 0.5 * x * (1.0 + th)
    dge = 0.5 * (1.0 + th) + 0.5 * x * (1.0 - th * th) * c * (1.0 + 3.0 * 0.044715 * x * x)
    return ge, dge


def _zero_at_first(first, *refs):
    @pl.when(first)
    def _():
        for ref in refs:
            ref[...] = jnp.zeros_like(ref)


def _token_tile(t):
    return 512 if t >= 2048 else t // 2


def _ffn_bwd_tile(t):
    return 1024 if t >= 4096 else t // 2


def _coords():
    return lax.axis_index("x"), lax.axis_index("y"), lax.axis_index("c")


class _Gather:
    n_phases = 3
    at = (0.0, 0.8, 1.0)

    def __init__(self, shards, routed=False):
        k = len(shards)
        self.routed = routed
        self.arrays = list(shards)
        self.out_shape = [jax.ShapeDtypeStruct((N_DEV,) + s.shape, s.dtype) for s in shards]
        self.scratch = [pltpu.SemaphoreType.DMA((7 * k,)), pltpu.SemaphoreType.DMA((7 * k,)), pltpu.SemaphoreType.DMA((k,))]

    def run(self, phase, ins, outs, sems):
        send_sems, recv_sems, local_sems = sems
        k_arr = len(ins)
        x, y, c = _coords()
        me, sibling = (x, y, c), (x, y, 1 - c)
        chips = [(1 - x, y), (x, 1 - y), (1 - x, 1 - y)]
        direct = 2 if self.routed else 3
        relay_from = (x + (1 - c) * (1 - 2 * x), y + c * (1 - 2 * y))
        relay_to = (x + c * (1 - 2 * x), y + (1 - c) * (1 - 2 * y))

        def rows(k, dev):
            return outs[k].at[4 * dev[0] + 2 * dev[1] + dev[2]]

        def copy(k, slot, block, to, src=None):
            return pltpu.make_async_remote_copy(
                src_ref=rows(k, block) if src is None else src, dst_ref=rows(k, block),
                send_sem=send_sems.at[7 * k + slot], recv_sem=recv_sems.at[7 * k + slot],
                device_id=to, device_id_type=MESH)

        def mine():
            return [pltpu.make_async_copy(ins[k], rows(k, me), local_sems.at[k]) for k in range(k_arr)]

        def first():
            return [copy(k, slot, me, to, src=ins[k]) for k in range(k_arr)
                    for slot, to in enumerate([sibling] + [(*chip, c) for chip in chips[:direct]])]

        def relayed(k):
            return copy(k, 3, (*relay_from, c), (*relay_to, c))

        def passed(j, k):
            return copy(k, 4 + j, (*chips[j], c), sibling)

        if phase == 0:
            for cp in mine() + first():
                cp.start()
        elif phase == 1:
            for j in range(direct):
                for k in range(k_arr):
                    copy(k, 1 + j, (*chips[j], c), me).wait_recv()
            for k in range(k_arr):
                if self.routed:
                    relayed(k).start()
                for j in range(direct):
                    passed(j, k).start()
        else:
            for k in range(k_arr):
                if self.routed:
                    copy(k, 3, (*chips[2], c), me).wait_recv()
                    passed(2, k).start()
            for k in range(k_arr):
                copy(k, 0, sibling, me).wait_recv()
                for j, chip in enumerate(chips):
                    copy(k, 4 + j, (*chip, 1 - c), me).wait_recv()
            sent = first() + [passed(j, k) for j in range(3) for k in range(k_arr)]
            if self.routed:
                sent += [relayed(k) for k in range(k_arr)]
            for cp in sent:
                cp.wait_send()
            for cp in mine():
                cp.wait()


class _SiblingExchange:
    n_phases = 2
    at = (0.0, 1.0)

    def __init__(self, grads):
        k = len(grads)
        self.arrays = list(grads)
        self.out_shape = [jax.ShapeDtypeStruct((4,) + g.shape[1:], g.dtype) for g in grads]
        self.scratch = [pltpu.SemaphoreType.DMA((4 * k,)), pltpu.SemaphoreType.DMA((4 * k,))]

    def run(self, phase, ins, outs, sems):
        send_sems, recv_sems = sems
        x, y, c = _coords()
        copies = [pltpu.make_async_remote_copy(
            src_ref=ins[k].at[2 * q + (1 - c)], dst_ref=outs[k].at[q],
            send_sem=send_sems.at[4 * k + q], recv_sem=recv_sems.at[4 * k + q],
            device_id=(x, y, 1 - c), device_id_type=MESH) for k in range(len(ins)) for q in range(4)]
        for cp in copies:
            if phase == 0:
                cp.start()
            else:
                cp.wait_recv()
                cp.wait_send()


class _ChipExchange:
    n_phases = 2
    at = (0.0, 1.0)

    def __init__(self, chip_sums):
        k = len(chip_sums)
        self.arrays = list(chip_sums)
        self.out_shape = [jax.ShapeDtypeStruct((3,) + s.shape[1:], s.dtype) for s in chip_sums]
        self.scratch = [pltpu.SemaphoreType.DMA((3 * k,)), pltpu.SemaphoreType.DMA((3 * k,))]

    def run(self, phase, ins, outs, sems):
        send_sems, recv_sems = sems
        x, y, c = _coords()
        chips = [(1 - x, y), (x, 1 - y), (1 - x, 1 - y)]
        copies = [pltpu.make_async_remote_copy(
            src_ref=ins[k].at[j], dst_ref=outs[k].at[j],
            send_sem=send_sems.at[3 * k + j], recv_sem=recv_sems.at[3 * k + j],
            device_id=(*chip, c), device_id_type=MESH) for k in range(len(ins)) for j, chip in enumerate(chips)]
        for cp in copies:
            if phase == 0:
                cp.start()
            else:
                cp.wait_recv()
                cp.wait_send()


class _Both:
    n_phases = 3
    at = (0.0, 0.95, 1.0)

    def __init__(self, two_phase, gather):
        self.parts = (two_phase, gather)
        self.arrays = two_phase.arrays + gather.arrays
        self.out_shape = two_phase.out_shape + gather.out_shape
        self.scratch = two_phase.scratch + gather.scratch

    def run(self, phase, ins, outs, sems):
        a, b = self.parts
        n_in, n_out, n_sem = len(a.arrays), len(a.out_shape), len(a.scratch)
        refs_a = (ins[:n_in], outs[:n_out], sems[:n_sem])
        refs_b = (ins[n_in:], outs[n_out:], sems[n_sem:])
        b.run(phase, *refs_b)
        if phase == 0:
            a.run(0, *refs_a)
        if phase == 2:
            a.run(1, *refs_a)


class _Host:
    def __init__(self, exchange):
        self.ex = exchange
        self.args = [] if exchange is None else exchange.arrays
        self.in_specs = [ANY] * len(self.args)
        self.out_shape = [] if exchange is None else exchange.out_shape
        self.out_specs = [ANY] * len(self.out_shape)
        self.scratch = [] if exchange is None else exchange.scratch

    def split(self, refs, n_in, n_out, n_scratch):
        a, b, s = len(self.args), len(self.out_shape), len(self.scratch)
        own_in, ex_in = refs[:n_in], refs[n_in:n_in + a]
        rest = refs[n_in + a:]
        own_out, ex_out = rest[:n_out], rest[n_out:n_out + b]
        rest = rest[n_out + b:]
        own_scratch, ex_sems = rest[:n_scratch], rest[n_scratch:n_scratch + s]
        return list(own_in) + list(own_out) + list(own_scratch), (ex_in, ex_out, ex_sems)

    def at_steps(self, step, n_steps, ex_refs):
        if self.ex is None:
            return
        for p in range(self.ex.n_phases):
            pl.when(step == int(round(self.ex.at[p] * (n_steps - 1))))(functools.partial(self.ex.run, p, *ex_refs))

    def phase(self, p, ex_refs):
        if self.ex is not None:
            self.ex.run(p, *ex_refs)


def _run_exchanges(exchanges, name):
    hosts = [_Host(ex) for ex in exchanges]
    n_in = [len(h.args) for h in hosts]
    n_out = [len(h.out_shape) for h in hosts]
    n_sc = [len(h.scratch) for h in hosts]

    def body(*refs):
        ins, outs, scr = refs[:sum(n_in)], refs[sum(n_in):sum(n_in) + sum(n_out)], refs[sum(n_in) + sum(n_out):]
        parts = []
        for e in range(len(hosts)):
            parts.append((ins[sum(n_in[:e]):sum(n_in[:e + 1])], outs[sum(n_out[:e]):sum(n_out[:e + 1])],
                          scr[sum(n_sc[:e]):sum(n_sc[:e + 1])]))
        for h, part in zip(hosts, parts):
            h.phase(0, part)
        for h, part in zip(hosts, parts):
            for p in range(1, h.ex.n_phases):
                h.phase(p, part)

    res = pl.pallas_call(
        body, name=name, in_specs=[ANY] * sum(n_in), out_specs=[ANY] * sum(n_out),
        out_shape=[s for h in hosts for s in h.out_shape], scratch_shapes=[s for h in hosts for s in h.scratch],
    )(*[a for h in hosts for a in h.args])
    return [res[sum(n_out[:e]):sum(n_out[:e + 1])] for e in range(len(hosts))]


def _ffn_fwd(x, g_pre, wgu, wd, g_post, target, name, exchange=None):
    t = x.shape[0]
    tm = _token_tile(t)
    n_i = t // tm
    with_loss = target is not None
    host = _Host(exchange)
    n_in, n_out = (6, 6) if with_loss else (5, 4)

    def body(*refs):
        own, ex_refs = host.split(refs, n_in, n_out, 0)
        if with_loss:
            x_ref, gpre_ref, wgu_ref, wd_ref, gpost_ref, tgt_ref, xo_ref, n_ref, df_ref, gu_ref, dgpost_ref, loss_ref = own
            _zero_at_first(pl.program_id(0) == 0, dgpost_ref)
        else:
            x_ref, gpre_ref, wgu_ref, wd_ref, gpost_ref, xo_ref, f_ref, n_ref, gu_ref = own
        host.at_steps(pl.program_id(0), n_i, ex_refs)
        x = x_ref[...]
        n = _rms_fwd(x, gpre_ref[...]).astype(BF16)
        n_ref[...] = n
        f = None
        for j in range(N_CHUNK):
            gate = _dot_nt(n, wgu_ref[0, j])
            up = _dot_nt(n, wgu_ref[1, j])
            gu_ref[0, j] = gate.astype(BF16)
            gu_ref[1, j] = up.astype(BF16)
            part = _dot((gate * _sigmoid(gate) * up).astype(BF16), wd_ref[j])
            f = part if f is None else f + part
        xo = x + 0.5 * _rms_fwd(f, gpost_ref[...])
        if with_loss:
            err = xo - tgt_ref[...]
            d_out = err * (1.0 / D_MODEL)
            xo_ref[...] = d_out
            df, dg = _rms_bwd(f, gpost_ref[...], 0.5 * d_out)
            df_ref[...] = df.astype(BF16)
            dgpost_ref[...] += dg
            part = 0.5 * jnp.sum(jnp.sum(err * err, axis=-1, keepdims=True) * (1.0 / D_MODEL), axis=0, keepdims=True)
            loss_ref[...] = jnp.broadcast_to(part, loss_ref.shape)
        else:
            f_ref[...] = f
            xo_ref[...] = xo

    tok = pl.BlockSpec((tm, D_MODEL), lambda i: (i, 0))
    vec = pl.BlockSpec((1, D_MODEL), lambda i: (0, 0))
    act = pl.BlockSpec((2, N_CHUNK, tm, CHUNK), lambda i: (0, 0, i, 0))
    tok_f32 = jax.ShapeDtypeStruct((t, D_MODEL), F32)
    tok_bf16 = jax.ShapeDtypeStruct((t, D_MODEL), BF16)
    act_shape = jax.ShapeDtypeStruct((2, N_CHUNK, t, CHUNK), BF16)
    in_specs = [tok, vec,
                pl.BlockSpec((2, N_CHUNK, CHUNK, D_MODEL), lambda i: (0, 0, 0, 0), pipeline_mode=pl.Buffered(1)),
                pl.BlockSpec((N_CHUNK, CHUNK, D_MODEL), lambda i: (0, 0, 0), pipeline_mode=pl.Buffered(1)),
                vec]
    args = [x, g_pre, wgu, wd, g_post]
    if with_loss:
        in_specs.append(tok)
        args.append(target)
        out_shape = [tok_f32, tok_bf16, tok_bf16, act_shape, jax.ShapeDtypeStruct((1, D_MODEL), F32),
                     jax.ShapeDtypeStruct((n_i * 8, 128), F32)]
        out_specs = [tok, tok, tok, act, vec, pl.BlockSpec((8, 128), lambda i: (i, 0))]
    else:
        out_shape = [tok_f32, tok_f32, tok_bf16, act_shape]
        out_specs = [tok, tok, tok, act]
    res = pl.pallas_call(
        body, name=name, grid=(n_i,), in_specs=in_specs + host.in_specs, out_specs=out_specs + host.out_specs,
        out_shape=out_shape + host.out_shape, scratch_shapes=host.scratch, compiler_params=_params(1),
    )(*args, *host.args)
    return (*res[:n_out], list(res[n_out:]))


def _ffn_bwd_w(n, df, gu, wd, name, exchange=None):
    t = n.shape[0]
    tm = _ffn_bwd_tile(t)
    n_i = t // tm
    host = _Host(exchange)

    def body(*refs):
        (n_ref, df_ref, gu_ref, wd_ref, dgu_ref, dwgu_ref, dwd_ref), ex_refs = host.split(refs, 4, 3, 0)
        i = pl.program_id(1)
        host.at_steps(pl.program_id(0) * n_i + i, N_CHUNK * n_i, ex_refs)
        _zero_at_first(i == 0, dwgu_ref, dwd_ref)
        nb = n_ref[...]
        dfb = df_ref[...]
        gate = gu_ref[0, 0].astype(F32)
        up = gu_ref[1, 0].astype(F32)
        s = _sigmoid(gate)
        silu = gate * s
        a = (silu * up).astype(BF16)
        da = _dot_nt(dfb, wd_ref[0])
        dup = (da * silu).astype(BF16)
        dgate = (da * up * (s * (1.0 + gate * (1.0 - s)))).astype(BF16)
        dgu_ref[0, 0] = dgate
        dgu_ref[1, 0] = dup
        dwgu_ref[0, 0] += _dot_tn(nb, dgate)
        dwgu_ref[1, 0] += _dot_tn(nb, dup)
        dwd_ref[0] += _dot_tn(a, dfb)

    tok = pl.BlockSpec((tm, D_MODEL), lambda j, i: (i, 0))
    act = pl.BlockSpec((2, 1, tm, CHUNK), lambda j, i: (0, j, i, 0))
    wgu_spec = pl.BlockSpec((2, 1, D_MODEL, CHUNK), lambda j, i: (0, j, 0, 0))
    wd_spec = pl.BlockSpec((1, CHUNK, D_MODEL), lambda j, i: (j, 0, 0))
    res = pl.pallas_call(
        body, name=name, grid=(N_CHUNK, n_i),
        in_specs=[tok, tok, act, wd_spec] + host.in_specs,
        out_specs=[act, wgu_spec, wd_spec] + host.out_specs,
        out_shape=[jax.ShapeDtypeStruct((2, N_CHUNK, t, CHUNK), BF16),
                   jax.ShapeDtypeStruct((2, N_CHUNK, D_MODEL, CHUNK), F32),
                   jax.ShapeDtypeStruct((N_CHUNK, CHUNK, D_MODEL), F32)] + host.out_shape,
        scratch_shapes=host.scratch, compiler_params=_params(2),
    )(n, df, gu, wd, *host.args)
    return (*res[:3], list(res[3:]))


def _ffn_bwd_x(dgu, wgu, x, g_pre, d_out, name, exchange=None):
    t = x.shape[0]
    tm = _token_tile(t)
    n_i = t // tm
    host = _Host(exchange)

    def body(*refs):
        (dgu_ref, wgu_ref, x_ref, gpre_ref, do_ref, dx_ref, dgpre_ref), ex_refs = host.split(refs, 5, 2, 0)
        i = pl.program_id(0)
        host.at_steps(i, n_i, ex_refs)
        _zero_at_first(i == 0, dgpre_ref)
        dn = _dot(dgu_ref[0, 0], wgu_ref[0, 0]) + _dot(dgu_ref[1, 0], wgu_ref[1, 0])
        for j in range(1, N_CHUNK):
            dn = dn + _dot(dgu_ref[0, j], wgu_ref[0, j]) + _dot(dgu_ref[1, j], wgu_ref[1, j])
        dx, dg = _rms_bwd(x_ref[...], gpre_ref[...], dn)
        dx_ref[...] = do_ref[...] + dx
        dgpre_ref[...] += dg

    tok = pl.BlockSpec((tm, D_MODEL), lambda i: (i, 0))
    vec = pl.BlockSpec((1, D_MODEL), lambda i: (0, 0))
    res = pl.pallas_call(
        body, name=name, grid=(n_i,),
        in_specs=[pl.BlockSpec((2, N_CHUNK, tm, CHUNK), lambda i: (0, 0, i, 0)),
                  pl.BlockSpec((2, N_CHUNK, CHUNK, D_MODEL), lambda i: (0, 0, 0, 0), pipeline_mode=pl.Buffered(1)),
                  tok, vec, tok] + host.in_specs,
        out_specs=[tok, vec] + host.out_specs,
        out_shape=[jax.ShapeDtypeStruct((t, D_MODEL), F32), jax.ShapeDtypeStruct((1, D_MODEL), F32)] + host.out_shape,
        scratch_shapes=host.scratch, compiler_params=_params(1),
    )(dgu, wgu, x, g_pre, d_out, *host.args)
    return (*res[:2], list(res[2:]))


def _mix_in_fwd(x1, g, w_in):
    t = x1.shape[0]
    tm = _token_tile(t)

    def body(x_ref, g_ref, w_ref, xl_ref, gl_ref, q_ref, kv_ref):
        n = _rms_fwd(x_ref[...], g_ref[...]).astype(BF16)
        proj = _dot_nt(n, w_ref[...])
        xl_ref[...] = proj[:, 0:512]
        gl_ref[...] = proj[:, 512:1024]
        q_ref[...] = proj[:, 1024:1536].astype(BF16)
        kv_ref[...] = proj[:, 1536:1792].astype(BF16)

    tok = pl.BlockSpec((tm, D_MODEL), lambda i: (i, 0))
    half = pl.BlockSpec((tm, 512), lambda i: (i, 0))
    return pl.pallas_call(
        body, name="mix_in_fwd", grid=(t // tm,),
        in_specs=[tok, pl.BlockSpec((1, D_MODEL), lambda i: (0, 0)), pl.BlockSpec((D_IN, D_MODEL), lambda i: (0, 0))],
        out_specs=[half, half, half, pl.BlockSpec((tm, 256), lambda i: (i, 0))],
        out_shape=[jax.ShapeDtypeStruct((t, 512), F32), jax.ShapeDtypeStruct((t, 512), F32),
                   jax.ShapeDtypeStruct((t, 512), BF16), jax.ShapeDtypeStruct((t, 256), BF16)],
        compiler_params=_params(1),
    )(x1, g, w_in)


def _shift_down(x, before, s):
    if s == 0:
        return x
    rolled = pltpu.roll(x, s, 0)
    ext = jnp.concatenate([before, x[0:8]], axis=0)
    first8 = pltpu.roll(ext, s, 0)[8:16]
    return jnp.concatenate([first8, rolled[8:]], axis=0)


def _shift_up(x, after, s):
    if s == 0:
        return x
    rows = x.shape[0]
    rolled = pltpu.roll(x, rows - s, 0)
    ext = jnp.concatenate([x[rows - 8:rows], after], axis=0)
    last8 = pltpu.roll(ext, 16 - s, 0)[0:8]
    return jnp.concatenate([rolled[:rows - 8], last8], axis=0)


def _log_sigmoid(x):
    e = jnp.exp(-jnp.abs(x))
    log1p_e = jnp.where(e < 0.01, e * (1.0 - e * (0.5 - e * (1.0 / 3.0))), jnp.log(1.0 + e))
    return jnp.minimum(x, 0.0) - log1p_e


def _lru_gates(xc, p_ref, wrg, wig):
    xcb = xc.astype(BF16)
    r = _sigmoid(_dot(xcb, wrg) + p_ref[1:2, :])
    ig = _sigmoid(_dot(xcb, wig) + p_ref[2:3, :])
    ls = _log_sigmoid(p_ref[3:4, :])
    log_a = LRU_C * r * ls
    a = jnp.exp(log_a)
    mult = jnp.sqrt(-jnp.tanh(log_a) * (a * a + 1.0))
    return xcb, r, ig, ls, a, mult


def _conv_taps(x, before, p_ref):
    xc = x * p_ref[7:8, :]
    for s in (1, 2, 3):
        xc = xc + _shift_down(x, before, s) * p_ref[7 - s:8 - s, :]
    return xc + p_ref[0:1, :]


def _lru_block_rows(t):
    return 512 if t >= 1024 else t // 2


def _lru_fwd(xl, p, wrg2, wig2):
    t = xl.shape[0]
    tb = _lru_block_rows(t)

    def body(xl_ref, p_ref, wrg_ref, wig_ref, h_ref, x_tail, h_carry):
        tt = pl.program_id(1)

        @pl.when(tt == 0)
        def _():
            x_tail[...] = jnp.zeros_like(x_tail)
            h_carry[...] = jnp.zeros_like(h_carry)

        x = xl_ref[...]
        xc = _conv_taps(x, x_tail[...], p_ref)
        x_tail[...] = x[tb - 8:tb]
        _, r, ig, ls, a, mult = _lru_gates(xc, p_ref, wrg_ref[0], wig_ref[0])
        u = mult * ig * xc
        row = lax.broadcasted_iota(jnp.int32, (tb, LRU_GROUP), 0)
        s = 1
        while s < tb:
            keep = row >= s
            u = jnp.where(keep, a * pltpu.roll(u, s, 0) + u, u)
            a = jnp.where(keep, a * pltpu.roll(a, s, 0), a)
            s *= 2
        h = u + a * h_carry[0:1, :]
        h_ref[...] = h
        h_carry[...] = jnp.broadcast_to(h[tb - 1:tb], h_carry.shape)

    blk = pl.BlockSpec((tb, LRU_GROUP), lambda g, tt: (tt, g))
    par = pl.BlockSpec((8, LRU_GROUP), lambda g, tt: (0, g))
    wsp = pl.BlockSpec((1, LRU_GROUP, LRU_GROUP), lambda g, tt: (g, 0, 0))
    return pl.pallas_call(
        body, name="lru_fwd", grid=(N_LRU_GROUP, t // tb), in_specs=[blk, par, wsp, wsp], out_specs=blk,
        out_shape=jax.ShapeDtypeStruct((t, D_LRU), F32),
        scratch_shapes=[pltpu.VMEM((8, LRU_GROUP), F32), pltpu.VMEM((8, LRU_GROUP), F32)],
        compiler_params=_params(2),
    )(xl, p, wrg2, wig2)


def _lru_bwd(dy, h, xl, gl, p, wrg2, wig2):
    t = xl.shape[0]
    tb = _lru_block_rows(t)
    n_tb = t // tb
    tb8 = tb // 8

    def body(dy_ref, h_ref, hprev_ref, xl_ref, xprev_ref, gl_ref, p_ref, wrg_ref, wig_ref,
             dxl_ref, dgl_ref, dp_ref, dwrg_ref, dwig_ref, g_carry, a_carry, dxc_head):
        step = pl.program_id(1)
        tt = n_tb - 1 - step
        first = step == 0

        _zero_at_first(first, g_carry, a_carry, dxc_head, dp_ref, dwrg_ref, dwig_ref)

        has_prev = (tt > 0).astype(F32)
        x = xl_ref[...]
        x_before = xprev_ref[...] * has_prev
        xs = [_shift_down(x, x_before, s) for s in range(4)]
        xc = xs[0] * p_ref[7:8, :] + xs[1] * p_ref[6:7, :] + xs[2] * p_ref[5:6, :] + xs[3] * p_ref[4:5, :] + p_ref[0:1, :]
        wrg = wrg_ref[0]
        wig = wig_ref[0]
        xcb, r, ig, ls, a, mult = _lru_gates(xc, p_ref, wrg, wig)

        hh = h_ref[...]
        h_m1 = _shift_down(hh, hprev_ref[...] * has_prev, 1)
        ge, dge = _gelu(gl_ref[...])
        dy = dy_ref[...]
        dgl_ref[...] = dy * hh * dge
        dh = dy * ge

        b = _shift_up(a, a_carry[...], 1)
        row = lax.broadcasted_iota(jnp.int32, (tb, LRU_GROUP), 0)
        g = dh
        s = 1
        while s < tb:
            keep = row < tb - s
            g = jnp.where(keep, b * pltpu.roll(g, tb - s, 0) + g, g)
            b = jnp.where(keep, b * pltpu.roll(b, tb - s, 0), b)
            s *= 2
        g = g + b * g_carry[0:1, :]
        g_carry[...] = jnp.broadcast_to(g[0:1], g_carry.shape)
        a_carry[...] = jnp.broadcast_to(a[0:1], a_carry.shape)

        da = g * h_m1
        dmult = g * ig * xc
        dig = g * mult * xc
        dxc = g * mult * ig
        dlog_a = da * a - dmult * (a * a) / mult
        dr = dlog_a * (LRU_C * ls)
        dls = jnp.sum(dlog_a * (LRU_C * r), axis=0, keepdims=True)
        dlam = dls * _sigmoid(-p_ref[3:4, :])
        dpre_r = dr * r * (1.0 - r)
        dpre_i = dig * ig * (1.0 - ig)
        dprb = dpre_r.astype(BF16)
        dpib = dpre_i.astype(BF16)
        dxc = dxc + _dot_nt(dprb, wrg) + _dot_nt(dpib, wig)
        dwrg_ref[0] += _dot_tn(xcb, dprb)
        dwig_ref[0] += _dot_tn(xcb, dpib)

        after = dxc_head[...]
        dxl = dxc * p_ref[7:8, :]
        for s in (1, 2, 3):
            dxl = dxl + _shift_up(dxc, after, s) * p_ref[7 - s:8 - s, :]
        dxl_ref[...] = dxl
        dxc_head[...] = dxc[0:8]

        rows = [jnp.sum(dxc, axis=0, keepdims=True), jnp.sum(dpre_r, axis=0, keepdims=True),
                jnp.sum(dpre_i, axis=0, keepdims=True), dlam]
        rows += [jnp.sum(dxc * xs[3 - k], axis=0, keepdims=True) for k in range(4)]
        dp_ref[...] += jnp.concatenate(rows, axis=0)

    blk = pl.BlockSpec((tb, LRU_GROUP), lambda g, s: (n_tb - 1 - s, g))
    prev8 = pl.BlockSpec((8, LRU_GROUP), lambda g, s: (jnp.maximum((n_tb - 1 - s) * tb8 - 1, 0), g))
    par = pl.BlockSpec((8, LRU_GROUP), lambda g, s: (0, g))
    wsp = pl.BlockSpec((1, LRU_GROUP, LRU_GROUP), lambda g, s: (g, 0, 0))
    return pl.pallas_call(
        body, name="lru_bwd", grid=(N_LRU_GROUP, n_tb),
        in_specs=[blk, blk, prev8, blk, prev8, blk, par, wsp, wsp], out_specs=[blk, blk, par, wsp, wsp],
        out_shape=[jax.ShapeDtypeStruct((t, D_LRU), F32), jax.ShapeDtypeStruct((t, D_LRU), F32),
                   jax.ShapeDtypeStruct((8, D_LRU), F32),
                   jax.ShapeDtypeStruct((N_LRU_GROUP, LRU_GROUP, LRU_GROUP), F32),
                   jax.ShapeDtypeStruct((N_LRU_GROUP, LRU_GROUP, LRU_GROUP), F32)],
        scratch_shapes=[pltpu.VMEM((8, LRU_GROUP), F32)] * 3,
        compiler_params=_params(2),
    )(dy, h, h, xl, xl, gl, p, wrg2, wig2)


def _attn_bias(first_block):
    qi = jnp.bitwise_and(lax.broadcasted_iota(jnp.int32, (4 * BLOCK_Q, 2 * BLOCK_Q), 0), BLOCK_Q - 1)
    kj = lax.broadcasted_iota(jnp.int32, (4 * BLOCK_Q, 2 * BLOCK_Q), 1)
    rel = qi + BLOCK_Q - kj
    mask = (rel >= 0) & (rel < BLOCK_Q)
    if first_block:
        mask = mask & (kj >= BLOCK_Q)
    return jnp.where(mask, 0.0, MASK_VALUE)


def _sink_column(sinks):
    hrow = lax.broadcasted_iota(jnp.int32, (4 * BLOCK_Q, 1), 0)
    return jnp.where(hrow < BLOCK_Q, sinks[0],
                     jnp.where(hrow < 2 * BLOCK_Q, sinks[1], jnp.where(hrow < 3 * BLOCK_Q, sinks[2], sinks[3])))


def _attn_scores(qv, kvv, n, bias, sk, lo):
    r0 = pl.multiple_of(n * BLOCK_Q, BLOCK_Q)
    rp = pl.multiple_of(jnp.maximum(n - 1, 0) * BLOCK_Q, BLOCK_Q)
    kvb = jnp.concatenate([kvv[pl.ds(rp, BLOCK_Q), :], kvv[pl.ds(r0, BLOCK_Q), :]], axis=0)
    k2 = kvb[:, 0:128]
    v2 = kvb[:, 128:256]
    qs = _stack_heads(qv[pl.ds(r0, BLOCK_Q), :], lo)
    s = _dot_nt(qs, k2) * ATTN_SCALE + bias
    m = jnp.maximum(jnp.max(s, axis=-1, keepdims=True), sk)
    e = jnp.exp(s - m)
    es = jnp.exp(sk - m)
    inv = 1.0 / (jnp.sum(e, axis=-1, keepdims=True) + es)
    return r0, rp, qs, k2, v2, e * inv, es * inv


def _stack_heads(pair2, lo):
    p0 = pair2[:, 0:128]
    p1 = pair2[:, 128:256]
    z = jnp.zeros_like(p0)
    return jnp.concatenate([jnp.where(lo, p0, z), jnp.where(lo, z, p0), jnp.where(lo, p1, z), jnp.where(lo, z, p1)], axis=0)


def _unstack_heads(st, lo):
    b = BLOCK_Q
    return jnp.concatenate([jnp.where(lo, st[0:b], st[b:2 * b]), jnp.where(lo, st[2 * b:3 * b], st[3 * b:4 * b])], axis=1)


def _attn_fwd(q, kv, sinks):
    t = q.shape[0]
    n_blk = t // BLOCK_Q

    def body(q_hbm, kv_hbm, s_ref, o_hbm, q2, kvv, o2, bias0, bias, sem):
        lo = lax.broadcasted_iota(jnp.int32, (BLOCK_Q, 128), 1) < HEAD_DIM
        cols = [pl.ds(256 * g, 256) for g in range(2)]
        loads = [pltpu.make_async_copy(kv_hbm, kvv, sem.at[0])]
        loads += [pltpu.make_async_copy(q_hbm.at[:, cols[g]], q2.at[g], sem.at[1 + g]) for g in range(2)]
        stores = [pltpu.make_async_copy(o2.at[g], o_hbm.at[:, cols[g]], sem.at[3 + g]) for g in range(2)]
        for cp in loads:
            cp.start()
        bias0[...] = _attn_bias(True)
        bias[...] = _attn_bias(False)
        loads[0].wait()
        for g in range(2):
            loads[1 + g].wait()
            qv, ov = q2.at[g], o2.at[g]
            sk = _sink_column([s_ref[0, HEAD_ORDER[4 * g + i]] for i in range(4)])

            def block(n, bias_ref):
                r0, _, _, _, v2, prob, _ = _attn_scores(qv, kvv, n, bias_ref[...], sk, lo)
                ov[pl.ds(r0, BLOCK_Q), :] = _unstack_heads(_dot(prob.astype(BF16), v2), lo)

            block(0, bias0)

            def later(n, carry):
                block(n, bias)
                return carry

            lax.fori_loop(1, n_blk, later, 0, unroll=2)
            stores[g].start()
        for cp in stores:
            cp.wait()

    return pl.pallas_call(
        body, name="attn_fwd", in_specs=[ANY, ANY, SMEM], out_specs=ANY,
        out_shape=jax.ShapeDtypeStruct((t, D_ATTN), F32),
        scratch_shapes=[pltpu.VMEM((2, t, 256), BF16), pltpu.VMEM((t, 256), BF16), pltpu.VMEM((2, t, 256), F32),
                        pltpu.VMEM((4 * BLOCK_Q, 2 * BLOCK_Q), F32), pltpu.VMEM((4 * BLOCK_Q, 2 * BLOCK_Q), F32),
                        pltpu.SemaphoreType.DMA((5,))],
        compiler_params=_params(),
    )(q, kv, sinks)


def _attn_bwd(q, kv, do, sinks, exchange=None):
    t = q.shape[0]
    n_blk = t // BLOCK_Q
    host = _Host(exchange)

    def body(*refs):
        own, ex_refs = host.split(refs, 4, 3, 9)
        q_hbm, kv_hbm, do_hbm, s_ref, dq_hbm, dkv_hbm, dsink_ref, q2, kvv, do2, dqv, dkvv, ds_acc, bias0, bias, sem = own
        host.phase(0, ex_refs)
        lo = lax.broadcasted_iota(jnp.int32, (BLOCK_Q, 128), 1) < HEAD_DIM
        loads = [pltpu.make_async_copy(kv_hbm, kvv, sem.at[0])]
        for g in range(2):
            loads += [pltpu.make_async_copy(src.at[:, pl.ds(256 * g, 256)], dst.at[g], sem.at[1 + 2 * g + i])
                      for i, (src, dst) in enumerate(((q_hbm, q2), (do_hbm, do2)))]
        for cp in loads:
            cp.start()
        bias0[...] = _attn_bias(True)
        bias[...] = _attn_bias(False)
        loads[0].wait()
        for g in range(2):
            cols = pl.ds(256 * g, 256)
            for cp in loads[1 + 2 * g:3 + 2 * g]:
                cp.wait()
            qv, dov = q2.at[g], do2.at[g]
            heads = [HEAD_ORDER[4 * g + i] for i in range(4)]
            sk = _sink_column([s_ref[0, h] for h in heads])
            ds_acc[...] = jnp.zeros_like(ds_acc)

            def block(n, bias_ref, has_prev):
                r0, rp, qs, k2, v2, prob, psink = _attn_scores(qv, kvv, n, bias_ref[...], sk, lo)
                pb = prob.astype(BF16)
                dos = _stack_heads(dov[pl.ds(r0, BLOCK_Q), :], lo)
                dp = _dot_nt(dos, v2)
                dsum = jnp.sum(prob * dp, axis=-1, keepdims=True)
                dsb = (prob * (dp - dsum) * ATTN_SCALE).astype(BF16)
                ds_acc[...] -= psink * dsum
                dqv[pl.ds(r0, BLOCK_Q), :] = _unstack_heads(_dot(dsb, k2), lo).astype(BF16)
                dk2 = _dot_tn(dsb, qs)
                dv2 = _dot_tn(pb, dos)
                cur = jnp.concatenate([dk2[BLOCK_Q:], dv2[BLOCK_Q:]], axis=1)
                if g == 0:
                    dkvv[pl.ds(r0, BLOCK_Q), :] = cur
                else:
                    dkvv[pl.ds(r0, BLOCK_Q), :] += cur
                if has_prev:
                    dkvv[pl.ds(rp, BLOCK_Q), :] += jnp.concatenate([dk2[:BLOCK_Q], dv2[:BLOCK_Q]], axis=1)

            block(0, bias0, False)

            def later(n, carry):
                block(n, bias, True)
                return carry

            lax.fori_loop(1, n_blk, later, 0, unroll=2)
            for i, h in enumerate(heads):
                tot = jnp.sum(ds_acc[BLOCK_Q * i:BLOCK_Q * (i + 1), :], axis=0, keepdims=True)
                dsink_ref[h:h + 1, :] = jnp.broadcast_to(tot, (1, 128))
            store = pltpu.make_async_copy(dqv, dq_hbm.at[:, cols], sem.at[5])
            store.start()
            store.wait()
        store = pltpu.make_async_copy(dkvv, dkv_hbm, sem.at[6])
        store.start()
        store.wait()
        if exchange is not None:
            for p in range(1, exchange.n_phases):
                host.phase(p, ex_refs)

    res = pl.pallas_call(
        body, name="attn_bwd", in_specs=[ANY, ANY, ANY, SMEM] + host.in_specs,
        out_specs=[ANY, ANY, pl.BlockSpec(memory_space=pltpu.VMEM)] + host.out_specs,
        out_shape=[jax.ShapeDtypeStruct((t, D_ATTN), BF16), jax.ShapeDtypeStruct((t, 256), F32),
                   jax.ShapeDtypeStruct((8, 128), F32)] + host.out_shape,
        scratch_shapes=[pltpu.VMEM((2, t, 256), BF16), pltpu.VMEM((t, 256), BF16), pltpu.VMEM((2, t, 256), BF16),
                        pltpu.VMEM((t, 256), BF16), pltpu.VMEM((t, 256), F32), pltpu.VMEM((4 * BLOCK_Q, 1), F32),
                        pltpu.VMEM((4 * BLOCK_Q, 2 * BLOCK_Q), F32), pltpu.VMEM((4 * BLOCK_Q, 2 * BLOCK_Q), F32),
                        pltpu.SemaphoreType.DMA((7,))] + host.scratch,
        compiler_params=_params(),
    )(q, kv, do, sinks, *host.args)
    return (*res[:3], list(res[3:]))


def _mix_out_fwd(x1, h, gl, o, g_lru, g_attn, g_post, w_o):
    t = x1.shape[0]
    tm = _token_tile(t)

    def body(x_ref, h_ref, gl_ref, o_ref, g1_ref, g2_ref, gp_ref, w_ref, x2_ref, m_ref):
        y = h_ref[...] * _gelu(gl_ref[...])[0]
        yn1 = _rms_fwd(y, g1_ref[...]).astype(BF16)
        yn2 = _rms_fwd(o_ref[...], g2_ref[...]).astype(BF16)
        m = _dot(yn1, w_ref[0:512, :]) + _dot(yn2, w_ref[512:1024, :])
        m_ref[...] = m
        x2_ref[...] = x_ref[...] + _rms_fwd(m, gp_ref[...])

    tok = pl.BlockSpec((tm, D_MODEL), lambda i: (i, 0))
    half = pl.BlockSpec((tm, 512), lambda i: (i, 0))
    vec = pl.BlockSpec((1, D_MODEL), lambda i: (0, 0))
    hvec = pl.BlockSpec((1, 512), lambda i: (0, 0))
    return pl.pallas_call(
        body, name="mix_out_fwd", grid=(t // tm,),
        in_specs=[tok, half, half, half, hvec, hvec, vec, pl.BlockSpec((D_MODEL, D_MODEL), lambda i: (0, 0))],
        out_specs=[tok, tok],
        out_shape=[jax.ShapeDtypeStruct((t, D_MODEL), F32), jax.ShapeDtypeStruct((t, D_MODEL), F32)],
        compiler_params=_params(1),
    )(x1, h, gl, o, g_lru, g_attn, g_post, w_o)


def _mix_out_bwd(dx2, m, h, gl, o, g_lru, g_attn, g_post, w_o):
    t = dx2.shape[0]
    tm = _token_tile(t)

    def body(dx_ref, m_ref, h_ref, gl_ref, o_ref, g1_ref, g2_ref, gp_ref, w_ref,
             dy_ref, do_ref, dw_ref, dgp_ref, dg1_ref, dg2_ref):
        _zero_at_first(pl.program_id(0) == 0, dw_ref, dgp_ref, dg1_ref, dg2_ref)
        dm, dgp = _rms_bwd(m_ref[...], gp_ref[...], dx_ref[...])
        dmb = dm.astype(BF16)
        y = h_ref[...] * _gelu(gl_ref[...])[0]
        o = o_ref[...]
        yn1 = _rms_fwd(y, g1_ref[...]).astype(BF16)
        yn2 = _rms_fwd(o, g2_ref[...]).astype(BF16)
        dw_ref[0:512, :] += _dot_tn(yn1, dmb)
        dw_ref[512:1024, :] += _dot_tn(yn2, dmb)
        dy, dg1 = _rms_bwd(y, g1_ref[...], _dot_nt(dmb, w_ref[0:512, :]))
        do, dg2 = _rms_bwd(o, g2_ref[...], _dot_nt(dmb, w_ref[512:1024, :]))
        dy_ref[...] = dy
        do_ref[...] = do.astype(BF16)
        dgp_ref[...] += dgp
        dg1_ref[...] += dg1
        dg2_ref[...] += dg2

    tok = pl.BlockSpec((tm, D_MODEL), lambda i: (i, 0))
    half = pl.BlockSpec((tm, 512), lambda i: (i, 0))
    vec = pl.BlockSpec((1, D_MODEL), lambda i: (0, 0))
    hvec = pl.BlockSpec((1, 512), lambda i: (0, 0))
    mat = pl.BlockSpec((D_MODEL, D_MODEL), lambda i: (0, 0))
    return pl.pallas_call(
        body, name="mix_out_bwd", grid=(t // tm,),
        in_specs=[tok, tok, half, half, half, hvec, hvec, vec, mat],
        out_specs=[half, half, mat, vec, hvec, hvec],
        out_shape=[jax.ShapeDtypeStruct((t, 512), F32), jax.ShapeDtypeStruct((t, 512), BF16),
                   jax.ShapeDtypeStruct((D_MODEL, D_MODEL), F32), jax.ShapeDtypeStruct((1, D_MODEL), F32),
                   jax.ShapeDtypeStruct((1, 512), F32), jax.ShapeDtypeStruct((1, 512), F32)],
        compiler_params=_params(1),
    )(dx2, m, h, gl, o, g_lru, g_attn, g_post, w_o)


def _mix_in_bwd(dx2, x1, g, dxl, dgl, dq, dkv, w_in, f1, g_post1):
    t = x1.shape[0]
    tm = _token_tile(t)

    def body(dx2_ref, x_ref, g_ref, dxl_ref, dgl_ref, dq_ref, dkv_ref, w_ref, f1_ref, gp1_ref,
             dx1_ref, dw_ref, dg_ref, df1_ref, dgp1_ref):
        _zero_at_first(pl.program_id(0) == 0, dw_ref, dg_ref, dgp1_ref)
        x = x_ref[...]
        nb = _rms_fwd(x, g_ref[...]).astype(BF16)
        dproj = jnp.concatenate([dxl_ref[...].astype(BF16), dgl_ref[...].astype(BF16), dq_ref[...],
                                 dkv_ref[...].astype(BF16)], axis=1)
        dw_ref[...] += _dot_tn(nb, dproj)
        dx, dg = _rms_bwd(x, g_ref[...], _dot(dproj, w_ref[...]))
        dx1 = dx2_ref[...] + dx
        dx1_ref[...] = dx1
        dg_ref[...] += dg
        df1, dgp1 = _rms_bwd(f1_ref[...], gp1_ref[...], 0.5 * dx1)
        df1_ref[...] = df1.astype(BF16)
        dgp1_ref[...] += dgp1

    tok = pl.BlockSpec((tm, D_MODEL), lambda i: (i, 0))
    half = pl.BlockSpec((tm, 512), lambda i: (i, 0))
    vec = pl.BlockSpec((1, D_MODEL), lambda i: (0, 0))
    mat = pl.BlockSpec((D_IN, D_MODEL), lambda i: (0, 0))
    dmat = pl.BlockSpec((D_MODEL, D_IN), lambda i: (0, 0))
    quarter = pl.BlockSpec((tm, 256), lambda i: (i, 0))
    return pl.pallas_call(
        body, name="mix_in_bwd", grid=(t // tm,),
        in_specs=[tok, tok, vec, half, half, half, quarter, mat, tok, vec], out_specs=[tok, dmat, vec, tok, vec],
        out_shape=[jax.ShapeDtypeStruct((t, D_MODEL), F32), jax.ShapeDtypeStruct((D_MODEL, D_IN), F32),
                   jax.ShapeDtypeStruct((1, D_MODEL), F32), jax.ShapeDtypeStruct((t, D_MODEL), BF16),
                   jax.ShapeDtypeStruct((1, D_MODEL), F32)],
        compiler_params=_params(1),
    )(dx2, x1, g, dxl, dgl, dq, dkv, w_in, f1, g_post1)


def _half(rows):
    return rows // 2


def _chip_sums(grads, from_sibling, other, name):
    n_arr = len(grads)

    def body(other_ref, *refs):
        for a in range(n_arr):
            refs[2 * n_arr + a][0] = (refs[2 * a][0, 0] + refs[2 * a + 1][0]).astype(BF16)

    in_specs, out_specs, out_shape, args = [], [], [], []
    for g, s in zip(grads, from_sibling):
        _, rows, cols = g.shape
        tr = _half(rows)
        in_specs += [pl.BlockSpec((1, 1, tr, cols), lambda j, i, other: (other[j], other[3], i, 0)),
                     pl.BlockSpec((1, tr, cols), lambda j, i, other: (other[j], i, 0))]
        out_specs.append(pl.BlockSpec((1, tr, cols), lambda j, i, other: (j, i, 0)))
        out_shape.append(jax.ShapeDtypeStruct((3, rows, cols), BF16))
        args += [g.reshape(4, 2, rows, cols), s]
    grid_spec = pltpu.PrefetchScalarGridSpec(num_scalar_prefetch=1, grid=(3, 2), in_specs=in_specs, out_specs=out_specs)
    return pl.pallas_call(body, name=name, grid_spec=grid_spec, out_shape=out_shape, compiler_params=_params(2))(other, *args)


def _adamw(w, g, m, v):
    m = ADAM_B1 * m + (1.0 - ADAM_B1) * g
    v = ADAM_B2 * v + (1.0 - ADAM_B2) * (g * g)
    m_hat = m / (1.0 - ADAM_B1 ** ADAM_STEP)
    v_hat = v / (1.0 - ADAM_B2 ** ADAM_STEP)
    delta = -ADAM_LR * (m_hat / (jnp.sqrt(v_hat) + ADAM_EPS) + ADAM_WD * w)
    return delta, m, v


def _shard_updates(grads, from_sibling, from_chips, w, m, v, place, name, transposed):
    n_arr = len(grads)

    def total(g_ref, s_ref, c_ref):
        g = g_ref[0, 0] + s_ref[0]
        g = g + c_ref[0].astype(F32)
        g = g + c_ref[1].astype(F32)
        return g + c_ref[2].astype(F32)

    part_specs, parts, flat, shapes = [], [], [], []
    for g in grads:
        _, rows, cols = g.shape
        tr = _half(rows)
        part_specs.append([pl.BlockSpec((1, 1, tr, cols), lambda i, place: (place[0], place[1], i, 0)),
                           pl.BlockSpec((1, tr, cols), lambda i, place: (place[0], i, 0)),
                           pl.BlockSpec((3, tr, cols), lambda i, place: (0, i, 0))])
        flat.append(pl.BlockSpec((tr, cols), lambda i, place: (i, 0)))
        shapes.append(jax.ShapeDtypeStruct((rows, cols), F32))
    for g, s, c in zip(grads, from_sibling, from_chips):
        parts += [g.reshape(4, 2, *g.shape[1:]), s, c]

    if not transposed:
        def body(place_ref, *refs):
            ins, wmv, outs = refs[:3 * n_arr], refs[3 * n_arr:6 * n_arr], refs[6 * n_arr:]
            for a in range(n_arr):
                g = total(*ins[3 * a:3 * a + 3])
                outs[4 * a][...] = g
                outs[4 * a + 1][...], outs[4 * a + 2][...], outs[4 * a + 3][...] = _adamw(
                    wmv[3 * a][...], g, wmv[3 * a + 1][...], wmv[3 * a + 2][...])

        grid_spec = pltpu.PrefetchScalarGridSpec(
            num_scalar_prefetch=1, grid=(2,),
            in_specs=[sp for specs in part_specs for sp in specs] + [f for f in flat for _ in range(3)],
            out_specs=[f for f in flat for _ in range(4)])
        res = pl.pallas_call(body, name=name, grid_spec=grid_spec, out_shape=[sh for sh in shapes for _ in range(4)],
                             compiler_params=_params(1))(place, *parts, *[x for wmv in zip(w, m, v) for x in wmv])
        return [tuple(res[4 * a:4 * a + 4]) for a in range(n_arr)]

    def sum_body(place_ref, *refs):
        for a in range(n_arr):
            refs[3 * n_arr + a][...] = total(*refs[3 * a:3 * a + 3])

    grid_spec = pltpu.PrefetchScalarGridSpec(num_scalar_prefetch=1, grid=(2,),
                                             in_specs=[sp for specs in part_specs for sp in specs], out_specs=flat)
    sums = pl.pallas_call(sum_body, name=name + "_sum", grid_spec=grid_spec, out_shape=shapes,
                          compiler_params=_params(1))(place, *parts)
    turned = [jnp.transpose(g, (1, 0)) for g in sums]

    def adam_body(*refs):
        ins, outs = refs[:4 * n_arr], refs[4 * n_arr:]
        for a in range(n_arr):
            g = ins[4 * a][...]
            outs[4 * a][...] = g
            outs[4 * a + 1][...], outs[4 * a + 2][...], outs[4 * a + 3][...] = _adamw(
                ins[4 * a + 1][...], g, ins[4 * a + 2][...], ins[4 * a + 3][...])

    blks = [pl.BlockSpec((g.shape[0] // 4, g.shape[1]), lambda i: (i, 0)) for g in turned]
    res = pl.pallas_call(
        adam_body, name=name + "_adam", grid=(4,), in_specs=[b for b in blks for _ in range(4)],
        out_specs=[b for b in blks for _ in range(4)],
        out_shape=[jax.ShapeDtypeStruct(g.shape, F32) for g in turned for _ in range(4)], compiler_params=_params(1),
    )(*[x for gwmv in zip(turned, w, m, v) for x in gwmv])
    return [tuple(res[4 * a:4 * a + 4]) for a in range(n_arr)]


GAINS = ("ffn1_pre_g", "ffn1_post_g", "mix_pre_g", "mix_post_g", "ffn2_pre_g", "ffn2_post_g")
HALVES = ("conv_b", "b_rg", "b_ig", "lru_lambda", "g_lru_out", "g_attn_out")
GATES = ("w_rg", "w_ig")
SMALL = GAINS + HALVES + GATES + ("sinks", "conv_w")


def _small_update(gathered, w, m, v):
    n_small = len(SMALL)

    def body(*refs):
        ga_ref, gb_ref, gc_ref, gd_ref, g0_ref, gconv_ref = refs[:6]
        wmv = refs[6:6 + 3 * n_small]
        outs = refs[6 + 3 * n_small:6 + 7 * n_small]
        loss_ref = refs[6 + 7 * n_small]

        def total(ref):
            s = ref[0]
            for d in range(1, N_DEV):
                s = s + ref[d]
            return s

        sa, sb, sc, sd = total(ga_ref), total(gb_ref), total(gc_ref), total(gd_ref)
        grads = {}
        for i, k in enumerate(GAINS):
            grads[k] = sa[i:i + 1]
        grads[GAINS[0]] = total(g0_ref)
        for i, k in enumerate(HALVES):
            grads[k] = sb[i:i + 1]
        grads["w_rg"], grads["w_ig"] = sc[0:512], sc[512:1024]
        grads["sinks"] = sd[4:5, 0:8]
        grads["conv_w"] = total(gconv_ref)
        for i, k in enumerate(SMALL):
            g = grads[k]
            outs[4 * i][...] = g
            outs[4 * i + 1][...], outs[4 * i + 2][...], outs[4 * i + 3][...] = _adamw(
                wmv[3 * i][...], g, wmv[3 * i + 1][...], wmv[3 * i + 2][...])
        loss_ref[...] = jnp.broadcast_to(sd[5:6, 0:128], loss_ref.shape)

    operands = list(gathered)
    out_shape = []
    for k in SMALL:
        operands += [w[k], m[k], v[k]]
        out_shape += [jax.ShapeDtypeStruct(w[k].shape, F32)] * 4
    out_shape.append(jax.ShapeDtypeStruct((8, 128), F32))
    res = pl.pallas_call(body, name="small_update", out_shape=out_shape, compiler_params=_params())(*operands)
    parts = [{k: res[4 * i + j] for i, k in enumerate(SMALL)} for j in range(4)]
    return (*parts, res[-1])


def _reorder_heads(a, axis, start, order):
    def slab(h):
        return lax.slice_in_dim(a, start + HEAD_DIM * h, start + HEAD_DIM * (h + 1), axis=axis)

    parts = [lax.slice_in_dim(a, 0, start, axis=axis)] + [slab(h) for h in order]
    parts.append(lax.slice_in_dim(a, start + 8 * HEAD_DIM, a.shape[axis], axis=axis))
    return jnp.concatenate(parts, axis=axis)


HEAD_ORDER_INVERSE = tuple(HEAD_ORDER.index(h) for h in range(8))


def _pair_block_diag(w):
    w = w.reshape(N_LRU_GROUP, 2, 64, 64)
    z = jnp.zeros((N_LRU_GROUP, 64, 64), w.dtype)
    top = jnp.concatenate([w[:, 0], z], axis=2)
    bot = jnp.concatenate([z, w[:, 1]], axis=2)
    return jnp.concatenate([top, bot], axis=1)


def _pair_block_diag_grad(dw2):
    return jnp.stack([dw2[:, :64, :64], dw2[:, 64:, 64:]], axis=1).reshape(512, 64)


def kernel(x, ffn1_pre_g, ffn1_w_gu, ffn1_w_down, ffn1_post_g, mix_pre_g, w_in, conv_w, conv_b, w_rg, b_rg, w_ig, b_ig, lru_lambda, sinks, g_lru_out, g_attn_out, w_o, mix_post_g, ffn2_pre_g, ffn2_w_gu, ffn2_w_down, ffn2_post_g, loss_target, m_ffn1_pre_g, m_ffn1_w_gu, m_ffn1_w_down, m_ffn1_post_g, m_mix_pre_g, m_w_in, m_conv_w, m_conv_b, m_w_rg, m_b_rg, m_w_ig, m_b_ig, m_lru_lambda, m_sinks, m_g_lru_out, m_g_attn_out, m_w_o, m_mix_post_g, m_ffn2_pre_g, m_ffn2_w_gu, m_ffn2_w_down, m_ffn2_post_g, v_ffn1_pre_g, v_ffn1_w_gu, v_ffn1_w_down, v_ffn1_post_g, v_mix_pre_g, v_w_in, v_conv_w, v_conv_b, v_w_rg, v_b_rg, v_w_ig, v_b_ig, v_lru_lambda, v_sinks, v_g_lru_out, v_g_attn_out, v_w_o, v_mix_post_g, v_ffn2_pre_g, v_ffn2_w_gu, v_ffn2_w_down, v_ffn2_post_g):
    args = dict(locals())
    names = ["ffn1_pre_g", "ffn1_w_gu", "ffn1_w_down", "ffn1_post_g", "mix_pre_g", "w_in", "conv_w", "conv_b", "w_rg",
             "b_rg", "w_ig", "b_ig", "lru_lambda", "sinks", "g_lru_out", "g_attn_out", "w_o", "mix_post_g",
             "ffn2_pre_g", "ffn2_w_gu", "ffn2_w_down", "ffn2_post_g"]
    big = ["ffn1_w_gu", "ffn1_w_down", "w_in", "w_o", "ffn2_w_gu", "ffn2_w_down"]
    w = {k: args[k] for k in names}
    mom = {k: args["m_" + k] for k in names}
    var = {k: args["v_" + k] for k in names}
    t = x.shape[1]
    xs = x.reshape(t, D_MODEL)
    target = loss_target.reshape(t, D_MODEL)
    cx, cy, cc = _coords()
    me = 4 * cx + 2 * cy + cc
    other = jnp.stack([2 * (1 - cx) + cy, 2 * cx + (1 - cy), 2 * (1 - cx) + (1 - cy), cc]).astype(jnp.int32)
    place = jnp.stack([2 * cx + cy, cc]).astype(jnp.int32)

    transposed = ("ffn1_w_gu", "w_in", "ffn2_w_gu")

    def shard_view(a, k):
        return jnp.transpose(a[0], (1, 0)) if k in transposed else a[0]

    def shard_unview(a, k):
        return (jnp.transpose(a, (1, 0)) if k in transposed else a)[None]

    shard2d = {k: shard_view(w[k], k) for k in big}
    shard_bf = {k: shard2d[k].astype(BF16) for k in big}
    conv_pad = jnp.pad(conv_w.reshape(4, 64), ((0, 4), (0, 64)))
    (first_w,) = _run_exchanges([_Gather([shard_bf["ffn1_w_gu"], shard_bf["ffn1_w_down"]], routed=True)], "all_gather_ffn1")
    wgu1 = first_w[0].reshape(2, N_CHUNK, CHUNK, D_MODEL)
    wd1 = first_w[1].reshape(N_CHUNK, CHUNK, D_MODEL)
    rest = _Gather([shard_bf["w_in"], shard_bf["w_o"], shard_bf["ffn2_w_gu"], shard_bf["ffn2_w_down"], conv_pad])

    x1, f1, n1, gu1, gathered = _ffn_fwd(xs, ffn1_pre_g, wgu1, wd1, ffn1_post_g, None, "ffn1_fwd", rest)
    w_in_full = _reorder_heads(gathered[0].reshape(D_IN, D_MODEL), 0, 2 * D_LRU, HEAD_ORDER)
    w_o_full = _reorder_heads(gathered[1].reshape(D_MODEL, D_MODEL), 0, D_LRU, HEAD_ORDER)
    g_attn_heads = _reorder_heads(g_attn_out, 1, 0, HEAD_ORDER)
    wgu2 = gathered[2].reshape(2, N_CHUNK, CHUNK, D_MODEL)
    wd2 = gathered[3].reshape(N_CHUNK, CHUNK, D_MODEL)
    conv_w_full = jnp.transpose(gathered[4][:, 0:4, 0:64], (1, 0, 2)).reshape(4, D_LRU)
    p_lru = jnp.concatenate([conv_b, b_rg, b_ig, lru_lambda, conv_w_full], axis=0)
    wrg2 = _pair_block_diag(w_rg[0]).astype(BF16)
    wig2 = _pair_block_diag(w_ig[0]).astype(BF16)
    xl, gl, q, kv = _mix_in_fwd(x1, mix_pre_g, w_in_full)
    h = _lru_fwd(xl, p_lru, wrg2, wig2)
    o = _attn_fwd(q, kv, sinks)
    x2, mo = _mix_out_fwd(x1, h, gl, o, g_lru_out, g_attn_heads, mix_post_g, w_o_full)
    g = {}
    dx3, n2, df2, gu2, g["ffn2_post_g"], loss_parts, _ = _ffn_fwd(x2, ffn2_pre_g, wgu2, wd2, ffn2_post_g, target, "ffn2_fwd")
    loss_local = jnp.sum(loss_parts[::8, 0])

    partial, from_sibling, from_chips = {}, {}, {}

    def chip_sums(keys):
        return _chip_sums([partial[k] for k in keys], [from_sibling[k] for k in keys], other, "chip_sum_" + keys[0])

    dgu2, dwgu2, dwd2, _ = _ffn_bwd_w(n2, df2, gu2, wd2, "ffn2_bwd_w")
    partial["ffn2_w_gu"] = dwgu2.reshape(N_DEV, D_MODEL, CHUNK)
    partial["ffn2_w_down"] = dwd2.reshape(N_DEV, D_FF // N_DEV, D_MODEL)
    ffn2_keys = ["ffn2_w_gu", "ffn2_w_down"]
    dx2, g["ffn2_pre_g"], got = _ffn_bwd_x(dgu2, wgu2, x2, ffn2_pre_g, dx3, "ffn2_bwd_x",
                                           _SiblingExchange([partial[k] for k in ffn2_keys]))
    from_sibling.update(zip(ffn2_keys, got))
    dy, do, dwo, g["mix_post_g"], g["g_lru_out"], dg_attn_heads = _mix_out_bwd(
        dx2, mo, h, gl, o, g_lru_out, g_attn_heads, mix_post_g, w_o_full)
    g["g_attn_out"] = _reorder_heads(dg_attn_heads, 1, 0, HEAD_ORDER_INVERSE)
    dwo = _reorder_heads(dwo, 0, D_LRU, HEAD_ORDER_INVERSE)
    dq, dkv, dsink, got = _attn_bwd(q, kv, do, sinks, _ChipExchange(chip_sums(ffn2_keys)))
    from_chips.update(zip(ffn2_keys, got))
    dxl, dgl, dp, dwrg2, dwig2 = _lru_bwd(dy, h, xl, gl, p_lru, wrg2, wig2)
    dx1, dwin, g["mix_pre_g"], df1, g["ffn1_post_g"] = _mix_in_bwd(
        dx2, x1, mix_pre_g, dxl, dgl, dq, dkv, w_in_full, f1, ffn1_post_g)
    dwin = _reorder_heads(dwin, 1, 2 * D_LRU, HEAD_ORDER_INVERSE)
    partial["w_in"] = jnp.transpose(dwin.reshape(D_MODEL, N_DEV, D_IN // N_DEV), (1, 0, 2))
    partial["w_o"] = dwo.reshape(N_DEV, D_MODEL // N_DEV, D_MODEL)
    mix_keys = ["w_in", "w_o"]
    (got,) = _run_exchanges([_SiblingExchange([partial[k] for k in mix_keys])], "mix_sibling_exchange")
    from_sibling.update(zip(mix_keys, got))
    dgu1, dwgu1, dwd1, got = _ffn_bwd_w(n1, df1, gu1, wd1, "ffn1_bwd_w", _ChipExchange(chip_sums(mix_keys)))
    from_chips.update(zip(mix_keys, got))
    partial["ffn1_w_gu"] = dwgu1.reshape(N_DEV, D_MODEL, CHUNK)
    partial["ffn1_w_down"] = dwd1.reshape(N_DEV, D_FF // N_DEV, D_MODEL)
    ffn1_keys = ["ffn1_w_gu", "ffn1_w_down"]
    (got,) = _run_exchanges([_SiblingExchange([partial[k] for k in ffn1_keys])], "ffn1_sibling_exchange")
    from_sibling.update(zip(ffn1_keys, got))
    zeros2 = jnp.zeros((2, D_MODEL), F32)
    g_gains = jnp.concatenate([zeros2[:1]] + [g[k] for k in GAINS[1:]] + [zeros2], axis=0)
    g_halves = jnp.concatenate([dp[0:4], g["g_lru_out"], g["g_attn_out"], zeros2[:, :D_LRU]], axis=0)
    g_gates = jnp.concatenate([_pair_block_diag_grad(dwrg2), _pair_block_diag_grad(dwig2)], axis=0)
    g_misc = jnp.concatenate([dp[4:8], jnp.pad(dsink[:, 0].reshape(1, 8), ((0, 0), (0, D_LRU - 8))),
                              jnp.pad(loss_local.reshape(1, 1), ((0, 0), (0, D_LRU - 1))), zeros2[:, :D_LRU]], axis=0)
    dx0, g_first, got = _ffn_bwd_x(dgu1, wgu1, xs, ffn1_pre_g, dx1, "ffn1_bwd_x",
                                   _Both(_ChipExchange(chip_sums(ffn1_keys)), _Gather([g_gains, g_halves, g_gates, g_misc])))
    from_chips.update(zip(ffn1_keys, got[:2]))
    gathered_small = got[2:]

    grads, delta, new_m, new_v = {}, {}, {}, {}
    for name, keys, turned in (("update_column_sharded", transposed, True),
                               ("update_row_sharded", tuple(k for k in big if k not in transposed), False)):
        res = _shard_updates([partial[k] for k in keys], [from_sibling[k] for k in keys], [from_chips[k] for k in keys],
                             [shard2d[k] for k in keys], [shard_view(mom[k], k) for k in keys],
                             [shard_view(var[k], k) for k in keys], place, name, turned)
        for k, out in zip(keys, res):
            grads[k], delta[k], new_m[k], new_v[k] = [shard_unview(r, k) for r in out]

    ((gathered_first,),) = _run_exchanges([_Gather([g_first])], "all_gather_first_gain")
    conv_parts = lax.dynamic_slice(gathered_small[3], (0, 0, me * 64), (N_DEV, 4, 64))

    def small_view(vals):
        out = {k: vals[k] for k in GAINS + HALVES + ("sinks",)}
        out.update({k: vals[k].reshape(512, 64) for k in GATES})
        out["conv_w"] = vals["conv_w"].reshape(4, 64)
        return out

    *small, loss_tile = _small_update([*gathered_small, gathered_first, conv_parts], small_view(w), small_view(mom),
                                      small_view(var))
    for dst, part in zip((grads, delta, new_m, new_v), small):
        for k in SMALL:
            dst[k] = part[k].reshape(w[k].shape)
    return (loss_tile[0, 0], dx0.reshape(x.shape), *[grads[k] for k in names], *[delta[k] for k in names],
            *[new_m[k] for k in names], *[new_v[k] for k in names])
```

```python
import functools

import jax
import jax.numpy as jnp
from jax import lax
from jax.experimental import pallas as pl
from jax.experimental.pallas import tpu as pltpu

F32 = jnp.float32
BF16 = jnp.bfloat16

D_MODEL = 1024
D_FF = 2816
N_DEV = 8
N_CHUNK = 4
CHUNK = D_FF // N_CHUNK
D_LRU = 512
D_ATTN = 512
LRU_GROUP = 128
N_LRU_GROUP = D_LRU // LRU_GROUP
HEAD_DIM = 64
BLOCK_Q = 128
D_IN = 1792
HEAD_ORDER = (0, 4, 1, 5, 2, 6, 3, 7)
RMS_EPS = 1e-6
LRU_C = 8.0
MASK_VALUE = -1e30
ATTN_SCALE = HEAD_DIM ** -0.5

ADAM_LR = 0.001
ADAM_B1 = 0.9
ADAM_B2 = 0.999
ADAM_EPS = 1e-08
ADAM_WD = 0.01
ADAM_STEP = 10

VMEM_LIMIT_V7X = 56 * 2 ** 20

ANY = pl.BlockSpec(memory_space=pl.ANY)
SMEM = pl.BlockSpec(memory_space=pltpu.SMEM)
MESH = pl.DeviceIdType.MESH


def _params(n_grid=0):
    sem = ("arbitrary",) * n_grid if n_grid else None
    return pltpu.CompilerParams(dimension_semantics=sem, vmem_limit_bytes=VMEM_LIMIT_V7X)


def _dot(a, b):
    return lax.dot_general(a, b, (((1,), (0,)), ((), ())), preferred_element_type=F32)


def _dot_nt(a, b):
    return lax.dot_general(a, b, (((1,), (1,)), ((), ())), preferred_element_type=F32)


def _dot_tn(a, b):
    return lax.dot_general(a, b, (((0,), (0,)), ((), ())), preferred_element_type=F32)


def _sigmoid(x):
    return 1.0 / (1.0 + jnp.exp(-x))


def _rms_fwd(x, g):
    r = lax.rsqrt(jnp.mean(x * x, axis=-1, keepdims=True) + RMS_EPS)
    return x * r * g


def _rms_bwd(x, g, dy):
    r = lax.rsqrt(jnp.mean(x * x, axis=-1, keepdims=True) + RMS_EPS)
    xh = x * r
    dg = jnp.sum(dy * xh, axis=0, keepdims=True)
    dxh = dy * g
    dx = r * (dxh - xh * jnp.mean(dxh * xh, axis=-1, keepdims=True))
    return dx, dg


def _gelu(x):
    c = 0.7978845608028654
    inner = c * (x + 0.044715 * x * x * x)
    th = jnp.tanh(inner)
    ge = 0.5 * x * (1.0 + th)
    dge = 0.5 * (1.0 + th) + 0.5 * x * (1.0 - th * th) * c * (1.0 + 3.0 * 0.044715 * x * x)
    return ge, dge


def _zero_at_first(first, *refs):
    @pl.when(first)
    def _():
        for ref in refs:
            ref[...] = jnp.zeros_like(ref)


def _token_tile(t):
    return 512 if t >= 2048 else t // 2


def _ffn_bwd_tile(t):
    return 1024 if t >= 4096 else t // 2


def _coords():
    return lax.axis_index("x"), lax.axis_index("y"), lax.axis_index("c")


class _Gather:
    n_phases = 3
    at = (0.0, 0.8, 1.0)

    def __init__(self, shards, routed=False):
        k = len(shards)
        self.routed = routed
        self.arrays = list(shards)
        self.out_shape = [jax.ShapeDtypeStruct((N_DEV,) + s.shape, s.dtype) for s in shards]
        self.scratch = [pltpu.SemaphoreType.DMA((7 * k,)), pltpu.SemaphoreType.DMA((7 * k,)), pltpu.SemaphoreType.DMA((k,))]

    def run(self, phase, ins, outs, sems):
        send_sems, recv_sems, local_sems = sems
        k_arr = len(ins)
        x, y, c = _coords()
        me, sibling = (x, y, c), (x, y, 1 - c)
        chips = [(1 - x, y), (x, 1 - y), (1 - x, 1 - y)]
        direct = 2 if self.routed else 3
        relay_from = (x + (1 - c) * (1 - 2 * x), y + c * (1 - 2 * y))
        relay_to = (x + c * (1 - 2 * x), y + (1 - c) * (1 - 2 * y))

        def rows(k, dev):
            return outs[k].at[4 * dev[0] + 2 * dev[1] + dev[2]]

        def copy(k, slot, block, to, src=None):
            return pltpu.make_async_remote_copy(
                src_ref=rows(k, block) if src is None else src, dst_ref=rows(k, block),
                send_sem=send_sems.at[7 * k + slot], recv_sem=recv_sems.at[7 * k + slot],
                device_id=to, device_id_type=MESH)

        def mine():
            return [pltpu.make_async_copy(ins[k], rows(k, me), local_sems.at[k]) for k in range(k_arr)]

        def first():
            return [copy(k, slot, me, to, src=ins[k]) for k in range(k_arr)
                    for slot, to in enumerate([sibling] + [(*chip, c) for chip in chips[:direct]])]

        def relayed(k):
            return copy(k, 3, (*relay_from, c), (*relay_to, c))

        def passed(j, k):
            return copy(k, 4 + j, (*chips[j], c), sibling)

        if phase == 0:
            for cp in mine() + first():
                cp.start()
        elif phase == 1:
            if self.routed:
                for k in range(k_arr):
                    copy(k, 1 + c, (*relay_from, c), me).wait_recv()
                    relayed(k).start()
                for k in range(k_arr):
                    copy(k, 2 - c, (*relay_to, c), me).wait_recv()
            else:
                for j in range(direct):
                    for k in range(k_arr):
                        copy(k, 1 + j, (*chips[j], c), me).wait_recv()
            for k in range(k_arr):
                for j in range(direct):
                    passed(j, k).start()
        else:
            for k in range(k_arr):
                if self.routed:
                    copy(k, 3, (*chips[2], c), me).wait_recv()
                    passed(2, k).start()
            for k in range(k_arr):
                copy(k, 0, sibling, me).wait_recv()
                for j, chip in enumerate(chips):
                    copy(k, 4 + j, (*chip, 1 - c), me).wait_recv()
            sent = first() + [passed(j, k) for j in range(3) for k in range(k_arr)]
            if self.routed:
                sent += [relayed(k) for k in range(k_arr)]
            for cp in sent:
                cp.wait_send()
            for cp in mine():
                cp.wait()


class _SiblingExchange:
    n_phases = 2
    at = (0.0, 1.0)

    def __init__(self, grads):
        k = len(grads)
        self.arrays = list(grads)
        self.out_shape = [jax.ShapeDtypeStruct((4,) + g.shape[1:], g.dtype) for g in grads]
        self.scratch = [pltpu.SemaphoreType.DMA((4 * k,)), pltpu.SemaphoreType.DMA((4 * k,))]

    def run(self, phase, ins, outs, sems):
        send_sems, recv_sems = sems
        x, y, c = _coords()
        copies = [pltpu.make_async_remote_copy(
            src_ref=ins[k].at[2 * q + (1 - c)], dst_ref=outs[k].at[q],
            send_sem=send_sems.at[4 * k + q], recv_sem=recv_sems.at[4 * k + q],
            device_id=(x, y, 1 - c), device_id_type=MESH) for k in range(len(ins)) for q in range(4)]
        for cp in copies:
            if phase == 0:
                cp.start()
            else:
                cp.wait_recv()
                cp.wait_send()


class _ChipExchange:
    n_phases = 2
    at = (0.0, 1.0)

    def __init__(self, chip_sums):
        k = len(chip_sums)
        self.arrays = list(chip_sums)
        self.out_shape = [jax.ShapeDtypeStruct((3,) + s.shape[1:], s.dtype) for s in chip_sums]
        self.scratch = [pltpu.SemaphoreType.DMA((3 * k,)), pltpu.SemaphoreType.DMA((3 * k,))]

    def run(self, phase, ins, outs, sems):
        send_sems, recv_sems = sems
        x, y, c = _coords()
        chips = [(1 - x, y), (x, 1 - y), (1 - x, 1 - y)]
        copies = [pltpu.make_async_remote_copy(
            src_ref=ins[k].at[j], dst_ref=outs[k].at[j],
            send_sem=send_sems.at[3 * k + j], recv_sem=recv_sems.at[3 * k + j],
            device_id=(*chip, c), device_id_type=MESH) for k in range(len(ins)) for j, chip in enumerate(chips)]
        for cp in copies:
            if phase == 0:
                cp.start()
            else:
                cp.wait_recv()
                cp.wait_send()


class _Both:
    n_phases = 3
    at = (0.0, 0.95, 1.0)

    def __init__(self, two_phase, gather):
        self.parts = (two_phase, gather)
        self.arrays = two_phase.arrays + gather.arrays
        self.out_shape = two_phase.out_shape + gather.out_shape
        self.scratch = two_phase.scratch + gather.scratch

    def run(self, phase, ins, outs, sems):
        a, b = self.parts
        n_in, n_out, n_sem = len(a.arrays), len(a.out_shape), len(a.scratch)
        refs_a = (ins[:n_in], outs[:n_out], sems[:n_sem])
        refs_b = (ins[n_in:], outs[n_out:], sems[n_sem:])
        b.run(phase, *refs_b)
        if phase == 0:
            a.run(0, *refs_a)
        if phase == 2:
            a.run(1, *refs_a)


class _Host:
    def __init__(self, exchange):
        self.ex = exchange
        self.args = [] if exchange is None else exchange.arrays
        self.in_specs = [ANY] * len(self.args)
        self.out_shape = [] if exchange is None else exchange.out_shape
        self.out_specs = [ANY] * len(self.out_shape)
        self.scratch = [] if exchange is None else exchange.scratch

    def split(self, refs, n_in, n_out, n_scratch):
        a, b, s = len(self.args), len(self.out_shape), len(self.scratch)
        own_in, ex_in = refs[:n_in], refs[n_in:n_in + a]
        rest = refs[n_in + a:]
        own_out, ex_out = rest[:n_out], rest[n_out:n_out + b]
        rest = rest[n_out + b:]
        own_scratch, ex_sems = rest[:n_scratch], rest[n_scratch:n_scratch + s]
        return list(own_in) + list(own_out) + list(own_scratch), (ex_in, ex_out, ex_sems)

    def at_steps(self, step, n_steps, ex_refs):
        if self.ex is None:
            return
        for p in range(self.ex.n_phases):
            pl.when(step == int(round(self.ex.at[p] * (n_steps - 1))))(functools.partial(self.ex.run, p, *ex_refs))

    def phase(self, p, ex_refs):
        if self.ex is not None:
            self.ex.run(p, *ex_refs)


def _run_exchanges(exchanges, name):
    hosts = [_Host(ex) for ex in exchanges]
    n_in = [len(h.args) for h in hosts]
    n_out = [len(h.out_shape) for h in hosts]
    n_sc = [len(h.scratch) for h in hosts]

    def body(*refs):
        ins, outs, scr = refs[:sum(n_in)], refs[sum(n_in):sum(n_in) + sum(n_out)], refs[sum(n_in) + sum(n_out):]
        parts = []
        for e in range(len(hosts)):
            parts.append((ins[sum(n_in[:e]):sum(n_in[:e + 1])], outs[sum(n_out[:e]):sum(n_out[:e + 1])],
                          scr[sum(n_sc[:e]):sum(n_sc[:e + 1])]))
        for h, part in zip(hosts, parts):
            h.phase(0, part)
        for h, part in zip(hosts, parts):
            for p in range(1, h.ex.n_phases):
                h.phase(p, part)

    res = pl.pallas_call(
        body, name=name, in_specs=[ANY] * sum(n_in), out_specs=[ANY] * sum(n_out),
        out_shape=[s for h in hosts for s in h.out_shape], scratch_shapes=[s for h in hosts for s in h.scratch],
    )(*[a for h in hosts for a in h.args])
    return [res[sum(n_out[:e]):sum(n_out[:e + 1])] for e in range(len(hosts))]


def _ffn_fwd(x, g_pre, wgu, wd, g_post, target, name, exchange=None):
    t = x.shape[0]
    tm = _token_tile(t)
    n_i = t // tm
    with_loss = target is not None
    host = _Host(exchange)
    n_in, n_out = (6, 6) if with_loss else (5, 4)

    def body(*refs):
        own, ex_refs = host.split(refs, n_in, n_out, 0)
        if with_loss:
            x_ref, gpre_ref, wgu_ref, wd_ref, gpost_ref, tgt_ref, xo_ref, n_ref, df_ref, gu_ref, dgpost_ref, loss_ref = own
            _zero_at_first(pl.program_id(0) == 0, dgpost_ref)
        else:
            x_ref, gpre_ref, wgu_ref, wd_ref, gpost_ref, xo_ref, f_ref, n_ref, gu_ref = own
        host.at_steps(pl.program_id(0), n_i, ex_refs)
        x = x_ref[...]
        n = _rms_fwd(x, gpre_ref[...]).astype(BF16)
        n_ref[...] = n
        f = None
        for j in range(N_CHUNK):
            gate = _dot_nt(n, wgu_ref[0, j])
            up = _dot_nt(n, wgu_ref[1, j])
            gu_ref[0, j] = gate.astype(BF16)
            gu_ref[1, j] = up.astype(BF16)
            part = _dot((gate * _sigmoid(gate) * up).astype(BF16), wd_ref[j])
            f = part if f is None else f + part
        xo = x + 0.5 * _rms_fwd(f, gpost_ref[...])
        if with_loss:
            err = xo - tgt_ref[...]
            d_out = err * (1.0 / D_MODEL)
            xo_ref[...] = d_out
            df, dg = _rms_bwd(f, gpost_ref[...], 0.5 * d_out)
            df_ref[...] = df.astype(BF16)
            dgpost_ref[...] += dg
            part = 0.5 * jnp.sum(jnp.sum(err * err, axis=-1, keepdims=True) * (1.0 / D_MODEL), axis=0, keepdims=True)
            loss_ref[...] = jnp.broadcast_to(part, loss_ref.shape)
        else:
            f_ref[...] = f
            xo_ref[...] = xo

    tok = pl.BlockSpec((tm, D_MODEL), lambda i: (i, 0))
    vec = pl.BlockSpec((1, D_MODEL), lambda i: (0, 0))
    act = pl.BlockSpec((2, N_CHUNK, tm, CHUNK), lambda i: (0, 0, i, 0))
    tok_f32 = jax.ShapeDtypeStruct((t, D_MODEL), F32)
    tok_bf16 = jax.ShapeDtypeStruct((t, D_MODEL), BF16)
    act_shape = jax.ShapeDtypeStruct((2, N_CHUNK, t, CHUNK), BF16)
    in_specs = [tok, vec,
                pl.BlockSpec((2, N_CHUNK, CHUNK, D_MODEL), lambda i: (0, 0, 0, 0), pipeline_mode=pl.Buffered(1)),
                pl.BlockSpec((N_CHUNK, CHUNK, D_MODEL), lambda i: (0, 0, 0), pipeline_mode=pl.Buffered(1)),
                vec]
    args = [x, g_pre, wgu, wd, g_post]
    if with_loss:
        in_specs.append(tok)
        args.append(target)
        out_shape = [tok_f32, tok_bf16, tok_bf16, act_shape, jax.ShapeDtypeStruct((1, D_MODEL), F32),
                     jax.ShapeDtypeStruct((n_i * 8, 128), F32)]
        out_specs = [tok, tok, tok, act, vec, pl.BlockSpec((8, 128), lambda i: (i, 0))]
    else:
        out_shape = [tok_f32, tok_f32, tok_bf16, act_shape]
        out_specs = [tok, tok, tok, act]
    res = pl.pallas_call(
        body, name=name, grid=(n_i,), in_specs=in_specs + host.in_specs, out_specs=out_specs + host.out_specs,
        out_shape=out_shape + host.out_shape, scratch_shapes=host.scratch, compiler_params=_params(1),
    )(*args, *host.args)
    return (*res[:n_out], list(res[n_out:]))


def _ffn_bwd_w(n, df, gu, wd, name, exchange=None):
    t = n.shape[0]
    tm = _ffn_bwd_tile(t)
    n_i = t // tm
    host = _Host(exchange)

    def body(*refs):
        (n_ref, df_ref, gu_ref, wd_ref, dgu_ref, dwgu_ref, dwd_ref), ex_refs = host.split(refs, 4, 3, 0)
        i = pl.program_id(1)
        host.at_steps(pl.program_id(0) * n_i + i, N_CHUNK * n_i, ex_refs)
        _zero_at_first(i == 0, dwgu_ref, dwd_ref)
        nb = n_ref[...]
        dfb = df_ref[...]
        gate = gu_ref[0, 0].astype(F32)
        up = gu_ref[1, 0].astype(F32)
        s = _sigmoid(gate)
        silu = gate * s
        a = (silu * up).astype(BF16)
        da = _dot_nt(dfb, wd_ref[0])
        dup = (da * silu).astype(BF16)
        dgate = (da * up * (s * (1.0 + gate * (1.0 - s)))).astype(BF16)
        dgu_ref[0, 0] = dgate
        dgu_ref[1, 0] = dup
        dwgu_ref[0, 0] += _dot_tn(nb, dgate)
        dwgu_ref[1, 0] += _dot_tn(nb, dup)
        dwd_ref[0] += _dot_tn(a, dfb)

    tok = pl.BlockSpec((tm, D_MODEL), lambda j, i: (i, 0))
    act = pl.BlockSpec((2, 1, tm, CHUNK), lambda j, i: (0, j, i, 0))
    wgu_spec = pl.BlockSpec((2, 1, D_MODEL, CHUNK), lambda j, i: (0, j, 0, 0))
    wd_spec = pl.BlockSpec((1, CHUNK, D_MODEL), lambda j, i: (j, 0, 0))
    res = pl.pallas_call(
        body, name=name, grid=(N_CHUNK, n_i),
        in_specs=[tok, tok, act, wd_spec] + host.in_specs,
        out_specs=[act, wgu_spec, wd_spec] + host.out_specs,
        out_shape=[jax.ShapeDtypeStruct((2, N_CHUNK, t, CHUNK), BF16),
                   jax.ShapeDtypeStruct((2, N_CHUNK, D_MODEL, CHUNK), F32),
                   jax.ShapeDtypeStruct((N_CHUNK, CHUNK, D_MODEL), F32)] + host.out_shape,
        scratch_shapes=host.scratch, compiler_params=_params(2),
    )(n, df, gu, wd, *host.args)
    return (*res[:3], list(res[3:]))


def _ffn_bwd_x(dgu, wgu, x, g_pre, d_out, name, exchange=None):
    t = x.shape[0]
    tm = _token_tile(t)
    n_i = t // tm
    host = _Host(exchange)

    def body(*refs):
        (dgu_ref, wgu_ref, x_ref, gpre_ref, do_ref, dx_ref, dgpre_ref), ex_refs = host.split(refs, 5, 2, 0)
        i = pl.program_id(0)
        host.at_steps(i, n_i, ex_refs)
        _zero_at_first(i == 0, dgpre_ref)
        dn = _dot(dgu_ref[0, 0], wgu_ref[0, 0]) + _dot(dgu_ref[1, 0], wgu_ref[1, 0])
        for j in range(1, N_CHUNK):
            dn = dn + _dot(dgu_ref[0, j], wgu_ref[0, j]) + _dot(dgu_ref[1, j], wgu_ref[1, j])
        dx, dg = _rms_bwd(x_ref[...], gpre_ref[...], dn)
        dx_ref[...] = do_ref[...] + dx
        dgpre_ref[...] += dg

    tok = pl.BlockSpec((tm, D_MODEL), lambda i: (i, 0))
    vec = pl.BlockSpec((1, D_MODEL), lambda i: (0, 0))
    res = pl.pallas_call(
        body, name=name, grid=(n_i,),
        in_specs=[pl.BlockSpec((2, N_CHUNK, tm, CHUNK), lambda i: (0, 0, i, 0)),
                  pl.BlockSpec((2, N_CHUNK, CHUNK, D_MODEL), lambda i: (0, 0, 0, 0), pipeline_mode=pl.Buffered(1)),
                  tok, vec, tok] + host.in_specs,
        out_specs=[tok, vec] + host.out_specs,
        out_shape=[jax.ShapeDtypeStruct((t, D_MODEL), F32), jax.ShapeDtypeStruct((1, D_MODEL), F32)] + host.out_shape,
        scratch_shapes=host.scratch, compiler_params=_params(1),
    )(dgu, wgu, x, g_pre, d_out, *host.args)
    return (*res[:2], list(res[2:]))


def _mix_in_fwd(x1, g, w_in):
    t = x1.shape[0]
    tm = _token_tile(t)

    def body(x_ref, g_ref, w_ref, xl_ref, gl_ref, q_ref, kv_ref):
        n = _rms_fwd(x_ref[...], g_ref[...]).astype(BF16)
        proj = _dot_nt(n, w_ref[...])
        xl_ref[...] = proj[:, 0:512]
        gl_ref[...] = proj[:, 512:1024]
        q_ref[...] = proj[:, 1024:1536].astype(BF16)
        kv_ref[...] = proj[:, 1536:1792].astype(BF16)

    tok = pl.BlockSpec((tm, D_MODEL), lambda i: (i, 0))
    half = pl.BlockSpec((tm, 512), lambda i: (i, 0))
    return pl.pallas_call(
        body, name="mix_in_fwd", grid=(t // tm,),
        in_specs=[tok, pl.BlockSpec((1, D_MODEL), lambda i: (0, 0)), pl.BlockSpec((D_IN, D_MODEL), lambda i: (0, 0))],
        out_specs=[half, half, half, pl.BlockSpec((tm, 256), lambda i: (i, 0))],
        out_shape=[jax.ShapeDtypeStruct((t, 512), F32), jax.ShapeDtypeStruct((t, 512), F32),
                   jax.ShapeDtypeStruct((t, 512), BF16), jax.ShapeDtypeStruct((t, 256), BF16)],
        compiler_params=_params(1),
    )(x1, g, w_in)


def _shift_down(x, before, s):
    if s == 0:
        return x
    rolled = pltpu.roll(x, s, 0)
    ext = jnp.concatenate([before, x[0:8]], axis=0)
    first8 = pltpu.roll(ext, s, 0)[8:16]
    return jnp.concatenate([first8, rolled[8:]], axis=0)


def _shift_up(x, after, s):
    if s == 0:
        return x
    rows = x.shape[0]
    rolled = pltpu.roll(x, rows - s, 0)
    ext = jnp.concatenate([x[rows - 8:rows], after], axis=0)
    last8 = pltpu.roll(ext, 16 - s, 0)[0:8]
    return jnp.concatenate([rolled[:rows - 8], last8], axis=0)


def _log_sigmoid(x):
    e = jnp.exp(-jnp.abs(x))
    log1p_e = jnp.where(e < 0.01, e * (1.0 - e * (0.5 - e * (1.0 / 3.0))), jnp.log(1.0 + e))
    return jnp.minimum(x, 0.0) - log1p_e


def _lru_gates(xc, p_ref, wrg, wig):
    xcb = xc.astype(BF16)
    r = _sigmoid(_dot(xcb, wrg) + p_ref[1:2, :])
    ig = _sigmoid(_dot(xcb, wig) + p_ref[2:3, :])
    ls = _log_sigmoid(p_ref[3:4, :])
    log_a = LRU_C * r * ls
    a = jnp.exp(log_a)
    mult = jnp.sqrt(-jnp.tanh(log_a) * (a * a + 1.0))
    return xcb, r, ig, ls, a, mult


def _conv_taps(x, before, p_ref):
    xc = x * p_ref[7:8, :]
    for s in (1, 2, 3):
        xc = xc + _shift_down(x, before, s) * p_ref[7 - s:8 - s, :]
    return xc + p_ref[0:1, :]


def _lru_block_rows(t):
    return 512 if t >= 1024 else t // 2


def _lru_fwd(xl, p, wrg2, wig2):
    t = xl.shape[0]
    tb = _lru_block_rows(t)

    def body(xl_ref, p_ref, wrg_ref, wig_ref, h_ref, x_tail, h_carry):
        tt = pl.program_id(1)

        @pl.when(tt == 0)
        def _():
            x_tail[...] = jnp.zeros_like(x_tail)
            h_carry[...] = jnp.zeros_like(h_carry)

        x = xl_ref[...]
        xc = _conv_taps(x, x_tail[...], p_ref)
        x_tail[...] = x[tb - 8:tb]
        _, r, ig, ls, a, mult = _lru_gates(xc, p_ref, wrg_ref[0], wig_ref[0])
        u = mult * ig * xc
        row = lax.broadcasted_iota(jnp.int32, (tb, LRU_GROUP), 0)
        s = 1
        while s < tb:
            keep = row >= s
            u = jnp.where(keep, a * pltpu.roll(u, s, 0) + u, u)
            a = jnp.where(keep, a * pltpu.roll(a, s, 0), a)
            s *= 2
        h = u + a * h_carry[0:1, :]
        h_ref[...] = h
        h_carry[...] = jnp.broadcast_to(h[tb - 1:tb], h_carry.shape)

    blk = pl.BlockSpec((tb, LRU_GROUP), lambda g, tt: (tt, g))
    par = pl.BlockSpec((8, LRU_GROUP), lambda g, tt: (0, g))
    wsp = pl.BlockSpec((1, LRU_GROUP, LRU_GROUP), lambda g, tt: (g, 0, 0))
    return pl.pallas_call(
        body, name="lru_fwd", grid=(N_LRU_GROUP, t // tb), in_specs=[blk, par, wsp, wsp], out_specs=blk,
        out_shape=jax.ShapeDtypeStruct((t, D_LRU), F32),
        scratch_shapes=[pltpu.VMEM((8, LRU_GROUP), F32), pltpu.VMEM((8, LRU_GROUP), F32)],
        compiler_params=_params(2),
    )(xl, p, wrg2, wig2)


def _lru_bwd(dy, h, xl, gl, p, wrg2, wig2):
    t = xl.shape[0]
    tb = _lru_block_rows(t)
    n_tb = t // tb
    tb8 = tb // 8

    def body(dy_ref, h_ref, hprev_ref, xl_ref, xprev_ref, gl_ref, p_ref, wrg_ref, wig_ref,
             dxl_ref, dgl_ref, dp_ref, dwrg_ref, dwig_ref, g_carry, a_carry, dxc_head):
        step = pl.program_id(1)
        tt = n_tb - 1 - step
        first = step == 0

        _zero_at_first(first, g_carry, a_carry, dxc_head, dp_ref, dwrg_ref, dwig_ref)

        has_prev = (tt > 0).astype(F32)
        x = xl_ref[...]
        x_before = xprev_ref[...] * has_prev
        xs = [_shift_down(x, x_before, s) for s in range(4)]
        xc = xs[0] * p_ref[7:8, :] + xs[1] * p_ref[6:7, :] + xs[2] * p_ref[5:6, :] + xs[3] * p_ref[4:5, :] + p_ref[0:1, :]
        wrg = wrg_ref[0]
        wig = wig_ref[0]
        xcb, r, ig, ls, a, mult = _lru_gates(xc, p_ref, wrg, wig)

        hh = h_ref[...]
        h_m1 = _shift_down(hh, hprev_ref[...] * has_prev, 1)
        ge, dge = _gelu(gl_ref[...])
        dy = dy_ref[...]
        dgl_ref[...] = dy * hh * dge
        dh = dy * ge

        b = _shift_up(a, a_carry[...], 1)
        row = lax.broadcasted_iota(jnp.int32, (tb, LRU_GROUP), 0)
        g = dh
        s = 1
        while s < tb:
            keep = row < tb - s
            g = jnp.where(keep, b * pltpu.roll(g, tb - s, 0) + g, g)
            b = jnp.where(keep, b * pltpu.roll(b, tb - s, 0), b)
            s *= 2
        g = g + b * g_carry[0:1, :]
        g_carry[...] = jnp.broadcast_to(g[0:1], g_carry.shape)
        a_carry[...] = jnp.broadcast_to(a[0:1], a_carry.shape)

        da = g * h_m1
        dmult = g * ig * xc
        dig = g * mult * xc
        dxc = g * mult * ig
        dlog_a = da * a - dmult * (a * a) / mult
        dr = dlog_a * (LRU_C * ls)
        dls = jnp.sum(dlog_a * (LRU_C * r), axis=0, keepdims=True)
        dlam = dls * _sigmoid(-p_ref[3:4, :])
        dpre_r = dr * r * (1.0 - r)
        dpre_i = dig * ig * (1.0 - ig)
        dprb = dpre_r.astype(BF16)
        dpib = dpre_i.astype(BF16)
        dxc = dxc + _dot_nt(dprb, wrg) + _dot_nt(dpib, wig)
        dwrg_ref[0] += _dot_tn(xcb, dprb)
        dwig_ref[0] += _dot_tn(xcb, dpib)

        after = dxc_head[...]
        dxl = dxc * p_ref[7:8, :]
        for s in (1, 2, 3):
            dxl = dxl + _shift_up(dxc, after, s) * p_ref[7 - s:8 - s, :]
        dxl_ref[...] = dxl
        dxc_head[...] = dxc[0:8]

        rows = [jnp.sum(dxc, axis=0, keepdims=True), jnp.sum(dpre_r, axis=0, keepdims=True),
                jnp.sum(dpre_i, axis=0, keepdims=True), dlam]
        rows += [jnp.sum(dxc * xs[3 - k], axis=0, keepdims=True) for k in range(4)]
        dp_ref[...] += jnp.concatenate(rows, axis=0)

    blk = pl.BlockSpec((tb, LRU_GROUP), lambda g, s: (n_tb - 1 - s, g))
    prev8 = pl.BlockSpec((8, LRU_GROUP), lambda g, s: (jnp.maximum((n_tb - 1 - s) * tb8 - 1, 0), g))
    par = pl.BlockSpec((8, LRU_GROUP), lambda g, s: (0, g))
    wsp = pl.BlockSpec((1, LRU_GROUP, LRU_GROUP), lambda g, s: (g, 0, 0))
    return pl.pallas_call(
        body, name="lru_bwd", grid=(N_LRU_GROUP, n_tb),
        in_specs=[blk, blk, prev8, blk, prev8, blk, par, wsp, wsp], out_specs=[blk, blk, par, wsp, wsp],
        out_shape=[jax.ShapeDtypeStruct((t, D_LRU), F32), jax.ShapeDtypeStruct((t, D_LRU), F32),
                   jax.ShapeDtypeStruct((8, D_LRU), F32),
                   jax.ShapeDtypeStruct((N_LRU_GROUP, LRU_GROUP, LRU_GROUP), F32),
                   jax.ShapeDtypeStruct((N_LRU_GROUP, LRU_GROUP, LRU_GROUP), F32)],
        scratch_shapes=[pltpu.VMEM((8, LRU_GROUP), F32)] * 3,
        compiler_params=_params(2),
    )(dy, h, h, xl, xl, gl, p, wrg2, wig2)


def _attn_bias(first_block):
    qi = jnp.bitwise_and(lax.broadcasted_iota(jnp.int32, (4 * BLOCK_Q, 2 * BLOCK_Q), 0), BLOCK_Q - 1)
    kj = lax.broadcasted_iota(jnp.int32, (4 * BLOCK_Q, 2 * BLOCK_Q), 1)
    rel = qi + BLOCK_Q - kj
    mask = (rel >= 0) & (rel < BLOCK_Q)
    if first_block:
        mask = mask & (kj >= BLOCK_Q)
    return jnp.where(mask, 0.0, MASK_VALUE)


def _sink_column(sinks):
    hrow = lax.broadcasted_iota(jnp.int32, (4 * BLOCK_Q, 1), 0)
    return jnp.where(hrow < BLOCK_Q, sinks[0],
                     jnp.where(hrow < 2 * BLOCK_Q, sinks[1], jnp.where(hrow < 3 * BLOCK_Q, sinks[2], sinks[3])))


def _attn_scores(qv, kvv, n, bias, sk, lo):
    r0 = pl.multiple_of(n * BLOCK_Q, BLOCK_Q)
    rp = pl.multiple_of(jnp.maximum(n - 1, 0) * BLOCK_Q, BLOCK_Q)
    kvb = jnp.concatenate([kvv[pl.ds(rp, BLOCK_Q), :], kvv[pl.ds(r0, BLOCK_Q), :]], axis=0)
    k2 = kvb[:, 0:128]
    v2 = kvb[:, 128:256]
    qs = _stack_heads(qv[pl.ds(r0, BLOCK_Q), :], lo)
    s = _dot_nt(qs, k2) * ATTN_SCALE + bias
    m = jnp.maximum(jnp.max(s, axis=-1, keepdims=True), sk)
    e = jnp.exp(s - m)
    es = jnp.exp(sk - m)
    inv = 1.0 / (jnp.sum(e, axis=-1, keepdims=True) + es)
    return r0, rp, qs, k2, v2, e * inv, es * inv


def _stack_heads(pair2, lo):
    p0 = pair2[:, 0:128]
    p1 = pair2[:, 128:256]
    z = jnp.zeros_like(p0)
    return jnp.concatenate([jnp.where(lo, p0, z), jnp.where(lo, z, p0), jnp.where(lo, p1, z), jnp.where(lo, z, p1)], axis=0)


def _unstack_heads(st, lo):
    b = BLOCK_Q
    return jnp.concatenate([jnp.where(lo, st[0:b], st[b:2 * b]), jnp.where(lo, st[2 * b:3 * b], st[3 * b:4 * b])], axis=1)


def _attn_fwd(q, kv, sinks):
    t = q.shape[0]
    n_blk = t // BLOCK_Q

    def body(q_hbm, kv_hbm, s_ref, o_hbm, q2, kvv, o2, bias0, bias, sem):
        lo = lax.broadcasted_iota(jnp.int32, (BLOCK_Q, 128), 1) < HEAD_DIM
        cols = [pl.ds(256 * g, 256) for g in range(2)]
        loads = [pltpu.make_async_copy(kv_hbm, kvv, sem.at[0])]
        loads += [pltpu.make_async_copy(q_hbm.at[:, cols[g]], q2.at[g], sem.at[1 + g]) for g in range(2)]
        stores = [pltpu.make_async_copy(o2.at[g], o_hbm.at[:, cols[g]], sem.at[3 + g]) for g in range(2)]
        for cp in loads:
            cp.start()
        bias0[...] = _attn_bias(True)
        bias[...] = _attn_bias(False)
        loads[0].wait()
        for g in range(2):
            loads[1 + g].wait()
            qv, ov = q2.at[g], o2.at[g]
            sk = _sink_column([s_ref[0, HEAD_ORDER[4 * g + i]] for i in range(4)])

            def block(n, bias_ref):
                r0, _, _, _, v2, prob, _ = _attn_scores(qv, kvv, n, bias_ref[...], sk, lo)
                ov[pl.ds(r0, BLOCK_Q), :] = _unstack_heads(_dot(prob.astype(BF16), v2), lo)

            block(0, bias0)

            def later(n, carry):
                block(n, bias)
                return carry

            lax.fori_loop(1, n_blk, later, 0, unroll=2)
            stores[g].start()
        for cp in stores:
            cp.wait()

    return pl.pallas_call(
        body, name="attn_fwd", in_specs=[ANY, ANY, SMEM], out_specs=ANY,
        out_shape=jax.ShapeDtypeStruct((t, D_ATTN), F32),
        scratch_shapes=[pltpu.VMEM((2, t, 256), BF16), pltpu.VMEM((t, 256), BF16), pltpu.VMEM((2, t, 256), F32),
                        pltpu.VMEM((4 * BLOCK_Q, 2 * BLOCK_Q), F32), pltpu.VMEM((4 * BLOCK_Q, 2 * BLOCK_Q), F32),
                        pltpu.SemaphoreType.DMA((5,))],
        compiler_params=_params(),
    )(q, kv, sinks)


def _attn_bwd(q, kv, do, sinks, exchange=None):
    t = q.shape[0]
    n_blk = t // BLOCK_Q
    host = _Host(exchange)

    def body(*refs):
        own, ex_refs = host.split(refs, 4, 3, 9)
        q_hbm, kv_hbm, do_hbm, s_ref, dq_hbm, dkv_hbm, dsink_ref, q2, kvv, do2, dqv, dkvv, ds_acc, bias0, bias, sem = own
        host.phase(0, ex_refs)
        lo = lax.broadcasted_iota(jnp.int32, (BLOCK_Q, 128), 1) < HEAD_DIM
        loads = [pltpu.make_async_copy(kv_hbm, kvv, sem.at[0])]
        for g in range(2):
            loads += [pltpu.make_async_copy(src.at[:, pl.ds(256 * g, 256)], dst.at[g], sem.at[1 + 2 * g + i])
                      for i, (src, dst) in enumerate(((q_hbm, q2), (do_hbm, do2)))]
        for cp in loads:
            cp.start()
        bias0[...] = _attn_bias(True)
        bias[...] = _attn_bias(False)
        loads[0].wait()
        for g in range(2):
            cols = pl.ds(256 * g, 256)
            for cp in loads[1 + 2 * g:3 + 2 * g]:
                cp.wait()
            qv, dov = q2.at[g], do2.at[g]
            heads = [HEAD_ORDER[4 * g + i] for i in range(4)]
            sk = _sink_column([s_ref[0, h] for h in heads])
            ds_acc[...] = jnp.zeros_like(ds_acc)

            def block(n, bias_ref, has_prev):
                r0, rp, qs, k2, v2, prob, psink = _attn_scores(qv, kvv, n, bias_ref[...], sk, lo)
                pb = prob.astype(BF16)
                dos = _stack_heads(dov[pl.ds(r0, BLOCK_Q), :], lo)
                dp = _dot_nt(dos, v2)
                dsum = jnp.sum(prob * dp, axis=-1, keepdims=True)
                dsb = (prob * (dp - dsum) * ATTN_SCALE).astype(BF16)
                ds_acc[...] -= psink * dsum
                dqv[pl.ds(r0, BLOCK_Q), :] = _unstack_heads(_dot(dsb, k2), lo).astype(BF16)
                dk2 = _dot_tn(dsb, qs)
                dv2 = _dot_tn(pb, dos)
                cur = jnp.concatenate([dk2[BLOCK_Q:], dv2[BLOCK_Q:]], axis=1)
                if g == 0:
                    dkvv[pl.ds(r0, BLOCK_Q), :] = cur
                else:
                    dkvv[pl.ds(r0, BLOCK_Q), :] += cur
                if has_prev:
                    dkvv[pl.ds(rp, BLOCK_Q), :] += jnp.concatenate([dk2[:BLOCK_Q], dv2[:BLOCK_Q]], axis=1)

            block(0, bias0, False)

            def later(n, carry):
                block(n, bias, True)
                return carry

            lax.fori_loop(1, n_blk, later, 0, unroll=2)
            for i, h in enumerate(heads):
                tot = jnp.sum(ds_acc[BLOCK_Q * i:BLOCK_Q * (i + 1), :], axis=0, keepdims=True)
                dsink_ref[h:h + 1, :] = jnp.broadcast_to(tot, (1, 128))
            store = pltpu.make_async_copy(dqv, dq_hbm.at[:, cols], sem.at[5])
            store.start()
            store.wait()
        store = pltpu.make_async_copy(dkvv, dkv_hbm, sem.at[6])
        store.start()
        store.wait()
        if exchange is not None:
            for p in range(1, exchange.n_phases):
                host.phase(p, ex_refs)

    res = pl.pallas_call(
        body, name="attn_bwd", in_specs=[ANY, ANY, ANY, SMEM] + host.in_specs,
        out_specs=[ANY, ANY, pl.BlockSpec(memory_space=pltpu.VMEM)] + host.out_specs,
        out_shape=[jax.ShapeDtypeStruct((t, D_ATTN), BF16), jax.ShapeDtypeStruct((t, 256), F32),
                   jax.ShapeDtypeStruct((8, 128), F32)] + host.out_shape,
        scratch_shapes=[pltpu.VMEM((2, t, 256), BF16), pltpu.VMEM((t, 256), BF16), pltpu.VMEM((2, t, 256), BF16),
                        pltpu.VMEM((t, 256), BF16), pltpu.VMEM((t, 256), F32), pltpu.VMEM((4 * BLOCK_Q, 1), F32),
                        pltpu.VMEM((4 * BLOCK_Q, 2 * BLOCK_Q), F32), pltpu.VMEM((4 * BLOCK_Q, 2 * BLOCK_Q), F32),
                        pltpu.SemaphoreType.DMA((7,))] + host.scratch,
        compiler_params=_params(),
    )(q, kv, do, sinks, *host.args)
    return (*res[:3], list(res[3:]))


def _mix_out_fwd(x1, h, gl, o, g_lru, g_attn, g_post, w_o):
    t = x1.shape[0]
    tm = _token_tile(t)

    def body(x_ref, h_ref, gl_ref, o_ref, g1_ref, g2_ref, gp_ref, w_ref, x2_ref, m_ref):
        y = h_ref[...] * _gelu(gl_ref[...])[0]
        yn1 = _rms_fwd(y, g1_ref[...]).astype(BF16)
        yn2 = _rms_fwd(o_ref[...], g2_ref[...]).astype(BF16)
        m = _dot(yn1, w_ref[0:512, :]) + _dot(yn2, w_ref[512:1024, :])
        m_ref[...] = m
        x2_ref[...] = x_ref[...] + _rms_fwd(m, gp_ref[...])

    tok = pl.BlockSpec((tm, D_MODEL), lambda i: (i, 0))
    half = pl.BlockSpec((tm, 512), lambda i: (i, 0))
    vec = pl.BlockSpec((1, D_MODEL), lambda i: (0, 0))
    hvec = pl.BlockSpec((1, 512), lambda i: (0, 0))
    return pl.pallas_call(
        body, name="mix_out_fwd", grid=(t // tm,),
        in_specs=[tok, half, half, half, hvec, hvec, vec, pl.BlockSpec((D_MODEL, D_MODEL), lambda i: (0, 0))],
        out_specs=[tok, tok],
        out_shape=[jax.ShapeDtypeStruct((t, D_MODEL), F32), jax.ShapeDtypeStruct((t, D_MODEL), F32)],
        compiler_params=_params(1),
    )(x1, h, gl, o, g_lru, g_attn, g_post, w_o)


def _mix_out_bwd(dx2, m, h, gl, o, g_lru, g_attn, g_post, w_o):
    t = dx2.shape[0]
    tm = _token_tile(t)

    def body(dx_ref, m_ref, h_ref, gl_ref, o_ref, g1_ref, g2_ref, gp_ref, w_ref,
             dy_ref, do_ref, dw_ref, dgp_ref, dg1_ref, dg2_ref):
        _zero_at_first(pl.program_id(0) == 0, dw_ref, dgp_ref, dg1_ref, dg2_ref)
        dm, dgp = _rms_bwd(m_ref[...], gp_ref[...], dx_ref[...])
        dmb = dm.astype(BF16)
        y = h_ref[...] * _gelu(gl_ref[...])[0]
        o = o_ref[...]
        yn1 = _rms_fwd(y, g1_ref[...]).astype(BF16)
        yn2 = _rms_fwd(o, g2_ref[...]).astype(BF16)
        dw_ref[0:512, :] += _dot_tn(yn1, dmb)
        dw_ref[512:1024, :] += _dot_tn(yn2, dmb)
        dy, dg1 = _rms_bwd(y, g1_ref[...], _dot_nt(dmb, w_ref[0:512, :]))
        do, dg2 = _rms_bwd(o, g2_ref[...], _dot_nt(dmb, w_ref[512:1024, :]))
        dy_ref[...] = dy
        do_ref[...] = do.astype(BF16)
        dgp_ref[...] += dgp
        dg1_ref[...] += dg1
        dg2_ref[...] += dg2

    tok = pl.BlockSpec((tm, D_MODEL), lambda i: (i, 0))
    half = pl.BlockSpec((tm, 512), lambda i: (i, 0))
    vec = pl.BlockSpec((1, D_MODEL), lambda i: (0, 0))
    hvec = pl.BlockSpec((1, 512), lambda i: (0, 0))
    mat = pl.BlockSpec((D_MODEL, D_MODEL), lambda i: (0, 0))
    return pl.pallas_call(
        body, name="mix_out_bwd", grid=(t // tm,),
        in_specs=[tok, tok, half, half, half, hvec, hvec, vec, mat],
        out_specs=[half, half, mat, vec, hvec, hvec],
        out_shape=[jax.ShapeDtypeStruct((t, 512), F32), jax.ShapeDtypeStruct((t, 512), BF16),
                   jax.ShapeDtypeStruct((D_MODEL, D_MODEL), F32), jax.ShapeDtypeStruct((1, D_MODEL), F32),
                   jax.ShapeDtypeStruct((1, 512), F32), jax.ShapeDtypeStruct((1, 512), F32)],
        compiler_params=_params(1),
    )(dx2, m, h, gl, o, g_lru, g_attn, g_post, w_o)


def _mix_in_bwd(dx2, x1, g, dxl, dgl, dq, dkv, w_in, f1, g_post1):
    t = x1.shape[0]
    tm = _token_tile(t)

    def body(dx2_ref, x_ref, g_ref, dxl_ref, dgl_ref, dq_ref, dkv_ref, w_ref, f1_ref, gp1_ref,
             dx1_ref, dw_ref, dg_ref, df1_ref, dgp1_ref):
        _zero_at_first(pl.program_id(0) == 0, dw_ref, dg_ref, dgp1_ref)
        x = x_ref[...]
        nb = _rms_fwd(x, g_ref[...]).astype(BF16)
        dproj = jnp.concatenate([dxl_ref[...].astype(BF16), dgl_ref[...].astype(BF16), dq_ref[...],
                                 dkv_ref[...].astype(BF16)], axis=1)
        dw_ref[...] += _dot_tn(nb, dproj)
        dx, dg = _rms_bwd(x, g_ref[...], _dot(dproj, w_ref[...]))
        dx1 = dx2_ref[...] + dx
        dx1_ref[...] = dx1
        dg_ref[...] += dg
        df1, dgp1 = _rms_bwd(f1_ref[...], gp1_ref[...], 0.5 * dx1)
        df1_ref[...] = df1.astype(BF16)
        dgp1_ref[...] += dgp1

    tok = pl.BlockSpec((tm, D_MODEL), lambda i: (i, 0))
    half = pl.BlockSpec((tm, 512), lambda i: (i, 0))
    vec = pl.BlockSpec((1, D_MODEL), lambda i: (0, 0))
    mat = pl.BlockSpec((D_IN, D_MODEL), lambda i: (0, 0))
    dmat = pl.BlockSpec((D_MODEL, D_IN), lambda i: (0, 0))
    quarter = pl.BlockSpec((tm, 256), lambda i: (i, 0))
    return pl.pallas_call(
        body, name="mix_in_bwd", grid=(t // tm,),
        in_specs=[tok, tok, vec, half, half, half, quarter, mat, tok, vec], out_specs=[tok, dmat, vec, tok, vec],
        out_shape=[jax.ShapeDtypeStruct((t, D_MODEL), F32), jax.ShapeDtypeStruct((D_MODEL, D_IN), F32),
                   jax.ShapeDtypeStruct((1, D_MODEL), F32), jax.ShapeDtypeStruct((t, D_MODEL), BF16),
                   jax.ShapeDtypeStruct((1, D_MODEL), F32)],
        compiler_params=_params(1),
    )(dx2, x1, g, dxl, dgl, dq, dkv, w_in, f1, g_post1)


def _half(rows):
    return rows // 2


def _chip_sums(grads, from_sibling, other, name):
    n_arr = len(grads)

    def body(other_ref, *refs):
        for a in range(n_arr):
            refs[2 * n_arr + a][0] = (refs[2 * a][0, 0] + refs[2 * a + 1][0]).astype(BF16)

    in_specs, out_specs, out_shape, args = [], [], [], []
    for g, s in zip(grads, from_sibling):
        _, rows, cols = g.shape
        tr = _half(rows)
        in_specs += [pl.BlockSpec((1, 1, tr, cols), lambda j, i, other: (other[j], other[3], i, 0)),
                     pl.BlockSpec((1, tr, cols), lambda j, i, other: (other[j], i, 0))]
        out_specs.append(pl.BlockSpec((1, tr, cols), lambda j, i, other: (j, i, 0)))
        out_shape.append(jax.ShapeDtypeStruct((3, rows, cols), BF16))
        args += [g.reshape(4, 2, rows, cols), s]
    grid_spec = pltpu.PrefetchScalarGridSpec(num_scalar_prefetch=1, grid=(3, 2), in_specs=in_specs, out_specs=out_specs)
    return pl.pallas_call(body, name=name, grid_spec=grid_spec, out_shape=out_shape, compiler_params=_params(2))(other, *args)


def _adamw(w, g, m, v):
    m = ADAM_B1 * m + (1.0 - ADAM_B1) * g
    v = ADAM_B2 * v + (1.0 - ADAM_B2) * (g * g)
    m_hat = m / (1.0 - ADAM_B1 ** ADAM_STEP)
    v_hat = v / (1.0 - ADAM_B2 ** ADAM_STEP)
    delta = -ADAM_LR * (m_hat / (jnp.sqrt(v_hat) + ADAM_EPS) + ADAM_WD * w)
    return delta, m, v


def _shard_updates(grads, from_sibling, from_chips, w, m, v, place, name, transposed):
    n_arr = len(grads)

    def total(g_ref, s_ref, c_ref):
        g = g_ref[0, 0] + s_ref[0]
        g = g + c_ref[0].astype(F32)
        g = g + c_ref[1].astype(F32)
        return g + c_ref[2].astype(F32)

    part_specs, parts, flat, shapes = [], [], [], []
    for g in grads:
        _, rows, cols = g.shape
        tr = _half(rows)
        part_specs.append([pl.BlockSpec((1, 1, tr, cols), lambda i, place: (place[0], place[1], i, 0)),
                           pl.BlockSpec((1, tr, cols), lambda i, place: (place[0], i, 0)),
                           pl.BlockSpec((3, tr, cols), lambda i, place: (0, i, 0))])
        flat.append(pl.BlockSpec((tr, cols), lambda i, place: (i, 0)))
        shapes.append(jax.ShapeDtypeStruct((rows, cols), F32))
    for g, s, c in zip(grads, from_sibling, from_chips):
        parts += [g.reshape(4, 2, *g.shape[1:]), s, c]

    if not transposed:
        def body(place_ref, *refs):
            ins, wmv, outs = refs[:3 * n_arr], refs[3 * n_arr:6 * n_arr], refs[6 * n_arr:]
            for a in range(n_arr):
                g = total(*ins[3 * a:3 * a + 3])
                outs[4 * a][...] = g
                outs[4 * a + 1][...], outs[4 * a + 2][...], outs[4 * a + 3][...] = _adamw(
                    wmv[3 * a][...], g, wmv[3 * a + 1][...], wmv[3 * a + 2][...])

        grid_spec = pltpu.PrefetchScalarGridSpec(
            num_scalar_prefetch=1, grid=(2,),
            in_specs=[sp for specs in part_specs for sp in specs] + [f for f in flat for _ in range(3)],
            out_specs=[f for f in flat for _ in range(4)])
        res = pl.pallas_call(body, name=name, grid_spec=grid_spec, out_shape=[sh for sh in shapes for _ in range(4)],
                             compiler_params=_params(1))(place, *parts, *[x for wmv in zip(w, m, v) for x in wmv])
        return [tuple(res[4 * a:4 * a + 4]) for a in range(n_arr)]

    def sum_body(place_ref, *refs):
        for a in range(n_arr):
            refs[3 * n_arr + a][...] = total(*refs[3 * a:3 * a + 3])

    grid_spec = pltpu.PrefetchScalarGridSpec(num_scalar_prefetch=1, grid=(2,),
                                             in_specs=[sp for specs in part_specs for sp in specs], out_specs=flat)
    sums = pl.pallas_call(sum_body, name=name + "_sum", grid_spec=grid_spec, out_shape=shapes,
                          compiler_params=_params(1))(place, *parts)
    turned = [jnp.transpose(g, (1, 0)) for g in sums]

    def adam_body(*refs):
        ins, outs = refs[:4 * n_arr], refs[4 * n_arr:]
        for a in range(n_arr):
            g = ins[4 * a][...]
            outs[4 * a][...] = g
            outs[4 * a + 1][...], outs[4 * a + 2][...], outs[4 * a + 3][...] = _adamw(
                ins[4 * a + 1][...], g, ins[4 * a + 2][...], ins[4 * a + 3][...])

    blks = [pl.BlockSpec((g.shape[0] // 4, g.shape[1]), lambda i: (i, 0)) for g in turned]
    res = pl.pallas_call(
        adam_body, name=name + "_adam", grid=(4,), in_specs=[b for b in blks for _ in range(4)],
        out_specs=[b for b in blks for _ in range(4)],
        out_shape=[jax.ShapeDtypeStruct(g.shape, F32) for g in turned for _ in range(4)], compiler_params=_params(1),
    )(*[x for gwmv in zip(turned, w, m, v) for x in gwmv])
    return [tuple(res[4 * a:4 * a + 4]) for a in range(n_arr)]


GAINS = ("ffn1_pre_g", "ffn1_post_g", "mix_pre_g", "mix_post_g", "ffn2_pre_g", "ffn2_post_g")
HALVES = ("conv_b", "b_rg", "b_ig", "lru_lambda", "g_lru_out", "g_attn_out")
GATES = ("w_rg", "w_ig")
SMALL = GAINS + HALVES + GATES + ("sinks", "conv_w")


def _small_update(gathered, w, m, v):
    n_small = len(SMALL)

    def body(*refs):
        ga_ref, gb_ref, gc_ref, gd_ref, g0_ref, gconv_ref = refs[:6]
        wmv = refs[6:6 + 3 * n_small]
        outs = refs[6 + 3 * n_small:6 + 7 * n_small]
        loss_ref = refs[6 + 7 * n_small]

        def total(ref):
            s = ref[0]
            for d in range(1, N_DEV):
                s = s + ref[d]
            return s

        sa, sb, sc, sd = total(ga_ref), total(gb_ref), total(gc_ref), total(gd_ref)
        grads = {}
        for i, k in enumerate(GAINS):
            grads[k] = sa[i:i + 1]
        grads[GAINS[0]] = total(g0_ref)
        for i, k in enumerate(HALVES):
            grads[k] = sb[i:i + 1]
        grads["w_rg"], grads["w_ig"] = sc[0:512], sc[512:1024]
        grads["sinks"] = sd[4:5, 0:8]
        grads["conv_w"] = total(gconv_ref)
        for i, k in enumerate(SMALL):
            g = grads[k]
            outs[4 * i][...] = g
            outs[4 * i + 1][...], outs[4 * i + 2][...], outs[4 * i + 3][...] = _adamw(
                wmv[3 * i][...], g, wmv[3 * i + 1][...], wmv[3 * i + 2][...])
        loss_ref[...] = jnp.broadcast_to(sd[5:6, 0:128], loss_ref.shape)

    operands = list(gathered)
    out_shape = []
    for k in SMALL:
        operands += [w[k], m[k], v[k]]
        out_shape += [jax.ShapeDtypeStruct(w[k].shape, F32)] * 4
    out_shape.append(jax.ShapeDtypeStruct((8, 128), F32))
    res = pl.pallas_call(body, name="small_update", out_shape=out_shape, compiler_params=_params())(*operands)
    parts = [{k: res[4 * i + j] for i, k in enumerate(SMALL)} for j in range(4)]
    return (*parts, res[-1])


def _reorder_heads(a, axis, start, order):
    def slab(h):
        return lax.slice_in_dim(a, start + HEAD_DIM * h, start + HEAD_DIM * (h + 1), axis=axis)

    parts = [lax.slice_in_dim(a, 0, start, axis=axis)] + [slab(h) for h in order]
    parts.append(lax.slice_in_dim(a, start + 8 * HEAD_DIM, a.shape[axis], axis=axis))
    return jnp.concatenate(parts, axis=axis)


HEAD_ORDER_INVERSE = tuple(HEAD_ORDER.index(h) for h in range(8))


def _pair_block_diag(w):
    w = w.reshape(N_LRU_GROUP, 2, 64, 64)
    z = jnp.zeros((N_LRU_GROUP, 64, 64), w.dtype)
    top = jnp.concatenate([w[:, 0], z], axis=2)
    bot = jnp.concatenate([z, w[:, 1]], axis=2)
    return jnp.concatenate([top, bot], axis=1)


def _pair_block_diag_grad(dw2):
    return jnp.stack([dw2[:, :64, :64], dw2[:, 64:, 64:]], axis=1).reshape(512, 64)


def kernel(x, ffn1_pre_g, ffn1_w_gu, ffn1_w_down, ffn1_post_g, mix_pre_g, w_in, conv_w, conv_b, w_rg, b_rg, w_ig, b_ig, lru_lambda, sinks, g_lru_out, g_attn_out, w_o, mix_post_g, ffn2_pre_g, ffn2_w_gu, ffn2_w_down, ffn2_post_g, loss_target, m_ffn1_pre_g, m_ffn1_w_gu, m_ffn1_w_down, m_ffn1_post_g, m_mix_pre_g, m_w_in, m_conv_w, m_conv_b, m_w_rg, m_b_rg, m_w_ig, m_b_ig, m_lru_lambda, m_sinks, m_g_lru_out, m_g_attn_out, m_w_o, m_mix_post_g, m_ffn2_pre_g, m_ffn2_w_gu, m_ffn2_w_down, m_ffn2_post_g, v_ffn1_pre_g, v_ffn1_w_gu, v_ffn1_w_down, v_ffn1_post_g, v_mix_pre_g, v_w_in, v_conv_w, v_conv_b, v_w_rg, v_b_rg, v_w_ig, v_b_ig, v_lru_lambda, v_sinks, v_g_lru_out, v_g_attn_out, v_w_o, v_mix_post_g, v_ffn2_pre_g, v_ffn2_w_gu, v_ffn2_w_down, v_ffn2_post_g):
    args = dict(locals())
    names = ["ffn1_pre_g", "ffn1_w_gu", "ffn1_w_down", "ffn1_post_g", "mix_pre_g", "w_in", "conv_w", "conv_b", "w_rg",
             "b_rg", "w_ig", "b_ig", "lru_lambda", "sinks", "g_lru_out", "g_attn_out", "w_o", "mix_post_g",
             "ffn2_pre_g", "ffn2_w_gu", "ffn2_w_down", "ffn2_post_g"]
    big = ["ffn1_w_gu", "ffn1_w_down", "w_in", "w_o", "ffn2_w_gu", "ffn2_w_down"]
    w = {k: args[k] for k in names}
    mom = {k: args["m_" + k] for k in names}
    var = {k: args["v_" + k] for k in names}
    t = x.shape[1]
    xs = x.reshape(t, D_MODEL)
    target = loss_target.reshape(t, D_MODEL)
    cx, cy, cc = _coords()
    me = 4 * cx + 2 * cy + cc
    other = jnp.stack([2 * (1 - cx) + cy, 2 * cx + (1 - cy), 2 * (1 - cx) + (1 - cy), cc]).astype(jnp.int32)
    place = jnp.stack([2 * cx + cy, cc]).astype(jnp.int32)

    transposed = ("ffn1_w_gu", "w_in", "ffn2_w_gu")

    def shard_view(a, k):
        return jnp.transpose(a[0], (1, 0)) if k in transposed else a[0]

    def shard_unview(a, k):
        return (jnp.transpose(a, (1, 0)) if k in transposed else a)[None]

    shard2d = {k: shard_view(w[k], k) for k in big}
    shard_bf = {k: shard2d[k].astype(BF16) for k in big}
    conv_pad = jnp.pad(conv_w.reshape(4, 64), ((0, 4), (0, 64)))
    (first_w,) = _run_exchanges([_Gather([shard_bf["ffn1_w_gu"], shard_bf["ffn1_w_down"]], routed=True)], "all_gather_ffn1")
    wgu1 = first_w[0].reshape(2, N_CHUNK, CHUNK, D_MODEL)
    wd1 = first_w[1].reshape(N_CHUNK, CHUNK, D_MODEL)
    rest = _Gather([shard_bf["w_in"], shard_bf["w_o"], shard_bf["ffn2_w_gu"], shard_bf["ffn2_w_down"], conv_pad])

    x1, f1, n1, gu1, gathered = _ffn_fwd(xs, ffn1_pre_g, wgu1, wd1, ffn1_post_g, None, "ffn1_fwd", rest)
    w_in_full = _reorder_heads(gathered[0].reshape(D_IN, D_MODEL), 0, 2 * D_LRU, HEAD_ORDER)
    w_o_full = _reorder_heads(gathered[1].reshape(D_MODEL, D_MODEL), 0, D_LRU, HEAD_ORDER)
    g_attn_heads = _reorder_heads(g_attn_out, 1, 0, HEAD_ORDER)
    wgu2 = gathered[2].reshape(2, N_CHUNK, CHUNK, D_MODEL)
    wd2 = gathered[3].reshape(N_CHUNK, CHUNK, D_MODEL)
    conv_w_full = jnp.transpose(gathered[4][:, 0:4, 0:64], (1, 0, 2)).reshape(4, D_LRU)
    p_lru = jnp.concatenate([conv_b, b_rg, b_ig, lru_lambda, conv_w_full], axis=0)
    wrg2 = _pair_block_diag(w_rg[0]).astype(BF16)
    wig2 = _pair_block_diag(w_ig[0]).astype(BF16)
    xl, gl, q, kv = _mix_in_fwd(x1, mix_pre_g, w_in_full)
    h = _lru_fwd(xl, p_lru, wrg2, wig2)
    o = _attn_fwd(q, kv, sinks)
    x2, mo = _mix_out_fwd(x1, h, gl, o, g_lru_out, g_attn_heads, mix_post_g, w_o_full)
    g = {}
    dx3, n2, df2, gu2, g["ffn2_post_g"], loss_parts, _ = _ffn_fwd(x2, ffn2_pre_g, wgu2, wd2, ffn2_post_g, target, "ffn2_fwd")
    loss_local = jnp.sum(loss_parts[::8, 0])

    partial, from_sibling, from_chips = {}, {}, {}

    def chip_sums(keys):
        return _chip_sums([partial[k] for k in keys], [from_sibling[k] for k in keys], other, "chip_sum_" + keys[0])

    dgu2, dwgu2, dwd2, _ = _ffn_bwd_w(n2, df2, gu2, wd2, "ffn2_bwd_w")
    partial["ffn2_w_gu"] = dwgu2.reshape(N_DEV, D_MODEL, CHUNK)
    partial["ffn2_w_down"] = dwd2.reshape(N_DEV, D_FF // N_DEV, D_MODEL)
    ffn2_keys = ["ffn2_w_gu", "ffn2_w_down"]
    dx2, g["ffn2_pre_g"], got = _ffn_bwd_x(dgu2, wgu2, x2, ffn2_pre_g, dx3, "ffn2_bwd_x",
                                           _SiblingExchange([partial[k] for k in ffn2_keys]))
    from_sibling.update(zip(ffn2_keys, got))
    dy, do, dwo, g["mix_post_g"], g["g_lru_out"], dg_attn_heads = _mix_out_bwd(
        dx2, mo, h, gl, o, g_lru_out, g_attn_heads, mix_post_g, w_o_full)
    g["g_attn_out"] = _reorder_heads(dg_attn_heads, 1, 0, HEAD_ORDER_INVERSE)
    dwo = _reorder_heads(dwo, 0, D_LRU, HEAD_ORDER_INVERSE)
    dq, dkv, dsink, got = _attn_bwd(q, kv, do, sinks, _ChipExchange(chip_sums(ffn2_keys)))
    from_chips.update(zip(ffn2_keys, got))
    dxl, dgl, dp, dwrg2, dwig2 = _lru_bwd(dy, h, xl, gl, p_lru, wrg2, wig2)
    dx1, dwin, g["mix_pre_g"], df1, g["ffn1_post_g"] = _mix_in_bwd(
        dx2, x1, mix_pre_g, dxl, dgl, dq, dkv, w_in_full, f1, ffn1_post_g)
    dwin = _reorder_heads(dwin, 1, 2 * D_LRU, HEAD_ORDER_INVERSE)
    partial["w_in"] = jnp.transpose(dwin.reshape(D_MODEL, N_DEV, D_IN // N_DEV), (1, 0, 2))
    partial["w_o"] = dwo.reshape(N_DEV, D_MODEL // N_DEV, D_MODEL)
    mix_keys = ["w_in", "w_o"]
    (got,) = _run_exchanges([_SiblingExchange([partial[k] for k in mix_keys])], "mix_sibling_exchange")
    from_sibling.update(zip(mix_keys, got))
    dgu1, dwgu1, dwd1, got = _ffn_bwd_w(n1, df1, gu1, wd1, "ffn1_bwd_w", _ChipExchange(chip_sums(mix_keys)))
    from_chips.update(zip(mix_keys, got))
    partial["ffn1_w_gu"] = dwgu1.reshape(N_DEV, D_MODEL, CHUNK)
    partial["ffn1_w_down"] = dwd1.reshape(N_DEV, D_FF // N_DEV, D_MODEL)
    ffn1_keys = ["ffn1_w_gu", "ffn1_w_down"]
    (got,) = _run_exchanges([_SiblingExchange([partial[k] for k in ffn1_keys])], "ffn1_sibling_exchange")
    from_sibling.update(zip(ffn1_keys, got))
    zeros2 = jnp.zeros((2, D_MODEL), F32)
    g_gains = jnp.concatenate([zeros2[:1]] + [g[k] for k in GAINS[1:]] + [zeros2], axis=0)
    g_halves = jnp.concatenate([dp[0:4], g["g_lru_out"], g["g_attn_out"], zeros2[:, :D_LRU]], axis=0)
    g_gates = jnp.concatenate([_pair_block_diag_grad(dwrg2), _pair_block_diag_grad(dwig2)], axis=0)
    g_misc = jnp.concatenate([dp[4:8], jnp.pad(dsink[:, 0].reshape(1, 8), ((0, 0), (0, D_LRU - 8))),
                              jnp.pad(loss_local.reshape(1, 1), ((0, 0), (0, D_LRU - 1))), zeros2[:, :D_LRU]], axis=0)
    dx0, g_first, got = _ffn_bwd_x(dgu1, wgu1, xs, ffn1_pre_g, dx1, "ffn1_bwd_x",
                                   _Both(_ChipExchange(chip_sums(ffn1_keys)), _Gather([g_gains, g_halves, g_gates, g_misc])))
    from_chips.update(zip(ffn1_keys, got[:2]))
    gathered_small = got[2:]

    grads, delta, new_m, new_v = {}, {}, {}, {}
    for name, keys, turned in (("update_column_sharded", transposed, True),
                               ("update_row_sharded", tuple(k for k in big if k not in transposed), False)):
        res = _shard_updates([partial[k] for k in keys], [from_sibling[k] for k in keys], [from_chips[k] for k in keys],
                             [shard2d[k] for k in keys], [shard_view(mom[k], k) for k in keys],
                             [shard_view(var[k], k) for k in keys], place, name, turned)
        for k, out in zip(keys, res):
            grads[k], delta[k], new_m[k], new_v[k] = [shard_unview(r, k) for r in out]

    ((gathered_first,),) = _run_exchanges([_Gather([g_first])], "all_gather_first_gain")
    conv_parts = lax.dynamic_slice(gathered_small[3], (0, 0, me * 64), (N_DEV, 4, 64))

    def small_view(vals):
        out = {k: vals[k] for k in GAINS + HALVES + ("sinks",)}
        out.update({k: vals[k].reshape(512, 64) for k in GATES})
        out["conv_w"] = vals["conv_w"].reshape(4, 64)
        return out

    *small, loss_tile = _small_update([*gathered_small, gathered_first, conv_parts], small_view(w), small_view(mom),
                                      small_view(var))
    for dst, part in zip((grads, delta, new_m, new_v), small):
        for k in SMALL:
            dst[k] = part[k].reshape(w[k].shape)
    return (loss_tile[0, 0], dx0.reshape(x.shape), *[grads[k] for k in names], *[delta[k] for k in names],
            *[new_m[k] for k in names], *[new_v[k] for k in names])
```

```python
import functools

import jax
import jax.numpy as jnp
from jax import lax
from jax.experimental import pallas as pl
from jax.experimental.pallas import tpu as pltpu

F32 = jnp.float32
BF16 = jnp.bfloat16

D_MODEL = 1024
D_FF = 2816
N_DEV = 8
N_CHUNK = 4
CHUNK = D_FF // N_CHUNK
D_LRU = 512
D_ATTN = 512
LRU_GROUP = 128
N_LRU_GROUP = D_LRU // LRU_GROUP
HEAD_DIM = 64
BLOCK_Q = 128
D_IN = 1792
HEAD_ORDER = (0, 4, 1, 5, 2, 6, 3, 7)
ATTN_UNROLL = 9
RMS_EPS = 1e-6
LRU_C = 8.0
MASK_VALUE = -1e30
ATTN_SCALE = HEAD_DIM ** -0.5

ADAM_LR = 0.001
ADAM_B1 = 0.9
ADAM_B2 = 0.999
ADAM_EPS = 1e-08
ADAM_WD = 0.01
ADAM_STEP = 10

VMEM_LIMIT_V7X = 56 * 2 ** 20

ANY = pl.BlockSpec(memory_space=pl.ANY)
SMEM = pl.BlockSpec(memory_space=pltpu.SMEM)
MESH = pl.DeviceIdType.MESH


def _params(n_grid=0):
    sem = ("arbitrary",) * n_grid if n_grid else None
    return pltpu.CompilerParams(dimension_semantics=sem, vmem_limit_bytes=VMEM_LIMIT_V7X)


def _dot(a, b):
    return lax.dot_general(a, b, (((1,), (0,)), ((), ())), preferred_element_type=F32)


def _dot_nt(a, b):
    return lax.dot_general(a, b, (((1,), (1,)), ((), ())), preferred_element_type=F32)


def _dot_tn(a, b):
    return lax.dot_general(a, b, (((0,), (0,)), ((), ())), preferred_element_type=F32)


def _sigmoid(x):
    return 1.0 / (1.0 + jnp.exp(-x))


def _rms_fwd(x, g):
    r = lax.rsqrt(jnp.mean(x * x, axis=-1, keepdims=True) + RMS_EPS)
    return x * r * g


def _rms_bwd(x, g, dy):
    r = lax.rsqrt(jnp.mean(x * x, axis=-1, keepdims=True) + RMS_EPS)
    xh = x * r
    dg = jnp.sum(dy * xh, axis=0, keepdims=True)
    dxh = dy * g
    dx = r * (dxh - xh * jnp.mean(dxh * xh, axis=-1, keepdims=True))
    return dx, dg


def _gelu(x):
    c = 0.7978845608028654
    inner = c * (x + 0.044715 * x * x * x)
    th = jnp.tanh(inner)
    ge = 0.5 * x * (1.0 + th)
    dge = 0.5 * (1.0 + th) + 0.5 * x * (1.0 - th * th) * c * (1.0 + 3.0 * 0.044715 * x * x)
    return ge, dge


def _zero_at_first(first, *refs):
    @pl.when(first)
    def _():
        for ref in refs:
            ref[...] = jnp.zeros_like(ref)


def _token_tile(t):
    return 512 if t >= 2048 else t // 2


def _ffn_bwd_tile(t):
    return 1024 if t >= 4096 else t // 2


def _coords():
    return lax.axis_index("x"), lax.axis_index("y"), lax.axis_index("c")


class _Gather:
    n_phases = 3
    at = (0.0, 0.8, 1.0)

    def __init__(self, shards, routed=False):
        k = len(shards)
        self.routed = routed
        self.arrays = list(shards)
        self.out_shape = [jax.ShapeDtypeStruct((N_DEV,) + s.shape, s.dtype) for s in shards]
        self.scratch = [pltpu.SemaphoreType.DMA((7 * k,)), pltpu.SemaphoreType.DMA((7 * k,)), pltpu.SemaphoreType.DMA((k,))]

    def run(self, phase, ins, outs, sems):
        send_sems, recv_sems, local_sems = sems
        k_arr = len(ins)
        x, y, c = _coords()
        me, sibling = (x, y, c), (x, y, 1 - c)
        chips = [(1 - x, y), (x, 1 - y), (1 - x, 1 - y)]
        direct = 2 if self.routed else 3
        relay_from = (x + (1 - c) * (1 - 2 * x), y + c * (1 - 2 * y))
        relay_to = (x + c * (1 - 2 * x), y + (1 - c) * (1 - 2 * y))

        def rows(k, dev):
            return outs[k].at[4 * dev[0] + 2 * dev[1] + dev[2]]

        def copy(k, slot, block, to, src=None):
            return pltpu.make_async_remote_copy(
                src_ref=rows(k, block) if src is None else src, dst_ref=rows(k, block),
                send_sem=send_sems.at[7 * k + slot], recv_sem=recv_sems.at[7 * k + slot],
                device_id=to, device_id_type=MESH)

        def mine():
            return [pltpu.make_async_copy(ins[k], rows(k, me), local_sems.at[k]) for k in range(k_arr)]

        def first():
            return [copy(k, slot, me, to, src=ins[k]) for k in range(k_arr)
                    for slot, to in enumerate([sibling] + [(*chip, c) for chip in chips[:direct]])]

        def relayed(k):
            return copy(k, 3, (*relay_from, c), (*relay_to, c))

        def passed(j, k):
            return copy(k, 4 + j, (*chips[j], c), sibling)

        if phase == 0:
            for cp in mine() + first():
                cp.start()
        elif phase == 1:
            if self.routed:
                for k in range(k_arr):
                    copy(k, 1 + c, (*relay_from, c), me).wait_recv()
                    relayed(k).start()
                for k in range(k_arr):
                    copy(k, 2 - c, (*relay_to, c), me).wait_recv()
            else:
                for j in range(direct):
                    for k in range(k_arr):
                        copy(k, 1 + j, (*chips[j], c), me).wait_recv()
            for k in range(k_arr):
                for j in range(direct):
                    passed(j, k).start()
        else:
            for k in range(k_arr):
                if self.routed:
                    copy(k, 3, (*chips[2], c), me).wait_recv()
                    passed(2, k).start()
            for k in range(k_arr):
                copy(k, 0, sibling, me).wait_recv()
                for j, chip in enumerate(chips):
                    copy(k, 4 + j, (*chip, 1 - c), me).wait_recv()
            sent = first() + [passed(j, k) for j in range(3) for k in range(k_arr)]
            if self.routed:
                sent += [relayed(k) for k in range(k_arr)]
            for cp in sent:
                cp.wait_send()
            for cp in mine():
                cp.wait()


class _SiblingExchange:
    n_phases = 2
    at = (0.0, 1.0)

    def __init__(self, grads):
        k = len(grads)
        self.arrays = list(grads)
        self.out_shape = [jax.ShapeDtypeStruct((4,) + g.shape[1:], g.dtype) for g in grads]
        self.scratch = [pltpu.SemaphoreType.DMA((4 * k,)), pltpu.SemaphoreType.DMA((4 * k,))]

    def run(self, phase, ins, outs, sems):
        send_sems, recv_sems = sems
        x, y, c = _coords()
        copies = [pltpu.make_async_remote_copy(
            src_ref=ins[k].at[2 * q + (1 - c)], dst_ref=outs[k].at[q],
            send_sem=send_sems.at[4 * k + q], recv_sem=recv_sems.at[4 * k + q],
            device_id=(x, y, 1 - c), device_id_type=MESH) for k in range(len(ins)) for q in range(4)]
        for cp in copies:
            if phase == 0:
                cp.start()
            else:
                cp.wait_recv()
                cp.wait_send()


class _ChipExchange:
    n_phases = 2
    at = (0.0, 1.0)

    def __init__(self, chip_sums):
        k = len(chip_sums)
        self.arrays = list(chip_sums)
        self.out_shape = [jax.ShapeDtypeStruct((3,) + s.shape[1:], s.dtype) for s in chip_sums]
        self.scratch = [pltpu.SemaphoreType.DMA((3 * k,)), pltpu.SemaphoreType.DMA((3 * k,))]

    def run(self, phase, ins, outs, sems):
        send_sems, recv_sems = sems
        x, y, c = _coords()
        chips = [(1 - x, y), (x, 1 - y), (1 - x, 1 - y)]
        copies = [pltpu.make_async_remote_copy(
            src_ref=ins[k].at[j], dst_ref=outs[k].at[j],
            send_sem=send_sems.at[3 * k + j], recv_sem=recv_sems.at[3 * k + j],
            device_id=(*chip, c), device_id_type=MESH) for k in range(len(ins)) for j, chip in enumerate(chips)]
        for cp in copies:
            if phase == 0:
                cp.start()
            else:
                cp.wait_recv()
                cp.wait_send()


class _Both:
    n_phases = 3
    at = (0.0, 0.95, 1.0)

    def __init__(self, two_phase, gather):
        self.parts = (two_phase, gather)
        self.arrays = two_phase.arrays + gather.arrays
        self.out_shape = two_phase.out_shape + gather.out_shape
        self.scratch = two_phase.scratch + gather.scratch

    def run(self, phase, ins, outs, sems):
        a, b = self.parts
        n_in, n_out, n_sem = len(a.arrays), len(a.out_shape), len(a.scratch)
        refs_a = (ins[:n_in], outs[:n_out], sems[:n_sem])
        refs_b = (ins[n_in:], outs[n_out:], sems[n_sem:])
        b.run(phase, *refs_b)
        if phase == 0:
            a.run(0, *refs_a)
        if phase == 2:
            a.run(1, *refs_a)


class _Host:
    def __init__(self, exchange):
        self.ex = exchange
        self.args = [] if exchange is None else exchange.arrays
        self.in_specs = [ANY] * len(self.args)
        self.out_shape = [] if exchange is None else exchange.out_shape
        self.out_specs = [ANY] * len(self.out_shape)
        self.scratch = [] if exchange is None else exchange.scratch

    def split(self, refs, n_in, n_out, n_scratch):
        a, b, s = len(self.args), len(self.out_shape), len(self.scratch)
        own_in, ex_in = refs[:n_in], refs[n_in:n_in + a]
        rest = refs[n_in + a:]
        own_out, ex_out = rest[:n_out], rest[n_out:n_out + b]
        rest = rest[n_out + b:]
        own_scratch, ex_sems = rest[:n_scratch], rest[n_scratch:n_scratch + s]
        return list(own_in) + list(own_out) + list(own_scratch), (ex_in, ex_out, ex_sems)

    def at_steps(self, step, n_steps, ex_refs):
        if self.ex is None:
            return
        for p in range(self.ex.n_phases):
            pl.when(step == int(round(self.ex.at[p] * (n_steps - 1))))(functools.partial(self.ex.run, p, *ex_refs))

    def phase(self, p, ex_refs):
        if self.ex is not None:
            self.ex.run(p, *ex_refs)


def _run_exchanges(exchanges, name):
    hosts = [_Host(ex) for ex in exchanges]
    n_in = [len(h.args) for h in hosts]
    n_out = [len(h.out_shape) for h in hosts]
    n_sc = [len(h.scratch) for h in hosts]

    def body(*refs):
        ins, outs, scr = refs[:sum(n_in)], refs[sum(n_in):sum(n_in) + sum(n_out)], refs[sum(n_in) + sum(n_out):]
        parts = []
        for e in range(len(hosts)):
            parts.append((ins[sum(n_in[:e]):sum(n_in[:e + 1])], outs[sum(n_out[:e]):sum(n_out[:e + 1])],
                          scr[sum(n_sc[:e]):sum(n_sc[:e + 1])]))
        for h, part in zip(hosts, parts):
            h.phase(0, part)
        for h, part in zip(hosts, parts):
            for p in range(1, h.ex.n_phases):
                h.phase(p, part)

    res = pl.pallas_call(
        body, name=name, in_specs=[ANY] * sum(n_in), out_specs=[ANY] * sum(n_out),
        out_shape=[s for h in hosts for s in h.out_shape], scratch_shapes=[s for h in hosts for s in h.scratch],
    )(*[a for h in hosts for a in h.args])
    return [res[sum(n_out[:e]):sum(n_out[:e + 1])] for e in range(len(hosts))]


def _ffn_fwd(x, g_pre, wgu, wd, g_post, target, name, exchange=None):
    t = x.shape[0]
    tm = _token_tile(t)
    n_i = t // tm
    with_loss = target is not None
    host = _Host(exchange)
    n_in, n_out = (6, 6) if with_loss else (5, 4)

    def body(*refs):
        own, ex_refs = host.split(refs, n_in, n_out, 0)
        if with_loss:
            x_ref, gpre_ref, wgu_ref, wd_ref, gpost_ref, tgt_ref, xo_ref, n_ref, df_ref, gu_ref, dgpost_ref, loss_ref = own
            _zero_at_first(pl.program_id(0) == 0, dgpost_ref)
        else:
            x_ref, gpre_ref, wgu_ref, wd_ref, gpost_ref, xo_ref, f_ref, n_ref, gu_ref = own
        host.at_steps(pl.program_id(0), n_i, ex_refs)
        x = x_ref[...]
        n = _rms_fwd(x, gpre_ref[...]).astype(BF16)
        n_ref[...] = n
        f = None
        for j in range(N_CHUNK):
            gate = _dot_nt(n, wgu_ref[0, j])
            up = _dot_nt(n, wgu_ref[1, j])
            gu_ref[0, j] = gate.astype(BF16)
            gu_ref[1, j] = up.astype(BF16)
            part = _dot((gate * _sigmoid(gate) * up).astype(BF16), wd_ref[j])
            f = part if f is None else f + part
        xo = x + 0.5 * _rms_fwd(f, gpost_ref[...])
        if with_loss:
            err = xo - tgt_ref[...]
            d_out = err * (1.0 / D_MODEL)
            xo_ref[...] = d_out
            df, dg = _rms_bwd(f, gpost_ref[...], 0.5 * d_out)
            df_ref[...] = df.astype(BF16)
            dgpost_ref[...] += dg
            part = 0.5 * jnp.sum(jnp.sum(err * err, axis=-1, keepdims=True) * (1.0 / D_MODEL), axis=0, keepdims=True)
            loss_ref[...] = jnp.broadcast_to(part, loss_ref.shape)
        else:
            f_ref[...] = f
            xo_ref[...] = xo

    tok = pl.BlockSpec((tm, D_MODEL), lambda i: (i, 0))
    vec = pl.BlockSpec((1, D_MODEL), lambda i: (0, 0))
    act = pl.BlockSpec((2, N_CHUNK, tm, CHUNK), lambda i: (0, 0, i, 0))
    tok_f32 = jax.ShapeDtypeStruct((t, D_MODEL), F32)
    tok_bf16 = jax.ShapeDtypeStruct((t, D_MODEL), BF16)
    act_shape = jax.ShapeDtypeStruct((2, N_CHUNK, t, CHUNK), BF16)
    in_specs = [tok, vec,
                pl.BlockSpec((2, N_CHUNK, CHUNK, D_MODEL), lambda i: (0, 0, 0, 0), pipeline_mode=pl.Buffered(1)),
                pl.BlockSpec((N_CHUNK, CHUNK, D_MODEL), lambda i: (0, 0, 0), pipeline_mode=pl.Buffered(1)),
                vec]
    args = [x, g_pre, wgu, wd, g_post]
    if with_loss:
        in_specs.append(tok)
        args.append(target)
        out_shape = [tok_f32, tok_bf16, tok_bf16, act_shape, jax.ShapeDtypeStruct((1, D_MODEL), F32),
                     jax.ShapeDtypeStruct((n_i * 8, 128), F32)]
        out_specs = [tok, tok, tok, act, vec, pl.BlockSpec((8, 128), lambda i: (i, 0))]
    else:
        out_shape = [tok_f32, tok_f32, tok_bf16, act_shape]
        out_specs = [tok, tok, tok, act]
    res = pl.pallas_call(
        body, name=name, grid=(n_i,), in_specs=in_specs + host.in_specs, out_specs=out_specs + host.out_specs,
        out_shape=out_shape + host.out_shape, scratch_shapes=host.scratch, compiler_params=_params(1),
    )(*args, *host.args)
    return (*res[:n_out], list(res[n_out:]))


def _ffn_bwd_w(n, df, gu, wd, name, exchange=None):
    t = n.shape[0]
    tm = _ffn_bwd_tile(t)
    n_i = t // tm
    host = _Host(exchange)

    def body(*refs):
        (n_ref, df_ref, gu_ref, wd_ref, dgu_ref, dwgu_ref, dwd_ref), ex_refs = host.split(refs, 4, 3, 0)
        i = pl.program_id(1)
        host.at_steps(pl.program_id(0) * n_i + i, N_CHUNK * n_i, ex_refs)
        _zero_at_first(i == 0, dwgu_ref, dwd_ref)
        nb = n_ref[...]
        dfb = df_ref[...]
        gate = gu_ref[0, 0].astype(F32)
        up = gu_ref[1, 0].astype(F32)
        s = _sigmoid(gate)
        silu = gate * s
        a = (silu * up).astype(BF16)
        da = _dot_nt(dfb, wd_ref[0])
        dup = (da * silu).astype(BF16)
        dgate = (da * up * (s * (1.0 + gate * (1.0 - s)))).astype(BF16)
        dgu_ref[0, 0] = dgate
        dgu_ref[1, 0] = dup
        dwgu_ref[0, 0] += _dot_tn(nb, dgate)
        dwgu_ref[1, 0] += _dot_tn(nb, dup)
        dwd_ref[0] += _dot_tn(a, dfb)

    tok = pl.BlockSpec((tm, D_MODEL), lambda j, i: (i, 0))
    act = pl.BlockSpec((2, 1, tm, CHUNK), lambda j, i: (0, j, i, 0))
    wgu_spec = pl.BlockSpec((2, 1, D_MODEL, CHUNK), lambda j, i: (0, j, 0, 0))
    wd_spec = pl.BlockSpec((1, CHUNK, D_MODEL), lambda j, i: (j, 0, 0))
    res = pl.pallas_call(
        body, name=name, grid=(N_CHUNK, n_i),
        in_specs=[tok, tok, act, wd_spec] + host.in_specs,
        out_specs=[act, wgu_spec, wd_spec] + host.out_specs,
        out_shape=[jax.ShapeDtypeStruct((2, N_CHUNK, t, CHUNK), BF16),
                   jax.ShapeDtypeStruct((2, N_CHUNK, D_MODEL, CHUNK), F32),
                   jax.ShapeDtypeStruct((N_CHUNK, CHUNK, D_MODEL), F32)] + host.out_shape,
        scratch_shapes=host.scratch, compiler_params=_params(2),
    )(n, df, gu, wd, *host.args)
    return (*res[:3], list(res[3:]))


def _ffn_bwd_x(dgu, wgu, x, g_pre, d_out, name, exchange=None):
    t = x.shape[0]
    tm = _token_tile(t)
    n_i = t // tm
    host = _Host(exchange)

    def body(*refs):
        (dgu_ref, wgu_ref, x_ref, gpre_ref, do_ref, dx_ref, dgpre_ref), ex_refs = host.split(refs, 5, 2, 0)
        i = pl.program_id(0)
        host.at_steps(i, n_i, ex_refs)
        _zero_at_first(i == 0, dgpre_ref)
        dn = _dot(dgu_ref[0, 0], wgu_ref[0, 0]) + _dot(dgu_ref[1, 0], wgu_ref[1, 0])
        for j in range(1, N_CHUNK):
            dn = dn + _dot(dgu_ref[0, j], wgu_ref[0, j]) + _dot(dgu_ref[1, j], wgu_ref[1, j])
        dx, dg = _rms_bwd(x_ref[...], gpre_ref[...], dn)
        dx_ref[...] = do_ref[...] + dx
        dgpre_ref[...] += dg

    tok = pl.BlockSpec((tm, D_MODEL), lambda i: (i, 0))
    vec = pl.BlockSpec((1, D_MODEL), lambda i: (0, 0))
    res = pl.pallas_call(
        body, name=name, grid=(n_i,),
        in_specs=[pl.BlockSpec((2, N_CHUNK, tm, CHUNK), lambda i: (0, 0, i, 0)),
                  pl.BlockSpec((2, N_CHUNK, CHUNK, D_MODEL), lambda i: (0, 0, 0, 0), pipeline_mode=pl.Buffered(1)),
                  tok, vec, tok] + host.in_specs,
        out_specs=[tok, vec] + host.out_specs,
        out_shape=[jax.ShapeDtypeStruct((t, D_MODEL), F32), jax.ShapeDtypeStruct((1, D_MODEL), F32)] + host.out_shape,
        scratch_shapes=host.scratch, compiler_params=_params(1),
    )(dgu, wgu, x, g_pre, d_out, *host.args)
    return (*res[:2], list(res[2:]))


def _mix_in_fwd(x1, g, w_in):
    t = x1.shape[0]
    tm = _token_tile(t)

    def body(x_ref, g_ref, w_ref, xl_ref, gl_ref, q_ref, kv_ref):
        n = _rms_fwd(x_ref[...], g_ref[...]).astype(BF16)
        proj = _dot_nt(n, w_ref[...])
        xl_ref[...] = proj[:, 0:512]
        gl_ref[...] = proj[:, 512:1024]
        q_ref[...] = proj[:, 1024:1536].astype(BF16)
        kv_ref[...] = proj[:, 1536:1792].astype(BF16)

    tok = pl.BlockSpec((tm, D_MODEL), lambda i: (i, 0))
    half = pl.BlockSpec((tm, 512), lambda i: (i, 0))
    return pl.pallas_call(
        body, name="mix_in_fwd", grid=(t // tm,),
        in_specs=[tok, pl.BlockSpec((1, D_MODEL), lambda i: (0, 0)), pl.BlockSpec((D_IN, D_MODEL), lambda i: (0, 0))],
        out_specs=[half, half, half, pl.BlockSpec((tm, 256), lambda i: (i, 0))],
        out_shape=[jax.ShapeDtypeStruct((t, 512), F32), jax.ShapeDtypeStruct((t, 512), F32),
                   jax.ShapeDtypeStruct((t, 512), BF16), jax.ShapeDtypeStruct((t, 256), BF16)],
        compiler_params=_params(1),
    )(x1, g, w_in)


def _shift_down(x, before, s):
    if s == 0:
        return x
    rolled = pltpu.roll(x, s, 0)
    ext = jnp.concatenate([before, x[0:8]], axis=0)
    first8 = pltpu.roll(ext, s, 0)[8:16]
    return jnp.concatenate([first8, rolled[8:]], axis=0)


def _shift_up(x, after, s):
    if s == 0:
        return x
    rows = x.shape[0]
    rolled = pltpu.roll(x, rows - s, 0)
    ext = jnp.concatenate([x[rows - 8:rows], after], axis=0)
    last8 = pltpu.roll(ext, 16 - s, 0)[0:8]
    return jnp.concatenate([rolled[:rows - 8], last8], axis=0)


def _log_sigmoid(x):
    e = jnp.exp(-jnp.abs(x))
    log1p_e = jnp.where(e < 0.01, e * (1.0 - e * (0.5 - e * (1.0 / 3.0))), jnp.log(1.0 + e))
    return jnp.minimum(x, 0.0) - log1p_e


def _lru_gates(xc, p_ref, wrg, wig):
    xcb = xc.astype(BF16)
    r = _sigmoid(_dot(xcb, wrg) + p_ref[1:2, :])
    ig = _sigmoid(_dot(xcb, wig) + p_ref[2:3, :])
    ls = _log_sigmoid(p_ref[3:4, :])
    log_a = LRU_C * r * ls
    a = jnp.exp(log_a)
    mult = jnp.sqrt(-jnp.tanh(log_a) * (a * a + 1.0))
    return xcb, r, ig, ls, a, mult


def _conv_taps(x, before, p_ref):
    xc = x * p_ref[7:8, :]
    for s in (1, 2, 3):
        xc = xc + _shift_down(x, before, s) * p_ref[7 - s:8 - s, :]
    return xc + p_ref[0:1, :]


def _lru_block_rows(t):
    return 512 if t >= 1024 else t // 2


def _lru_fwd(xl, p, wrg2, wig2):
    t = xl.shape[0]
    tb = _lru_block_rows(t)

    def body(xl_ref, p_ref, wrg_ref, wig_ref, h_ref, x_tail, h_carry):
        tt = pl.program_id(1)

        @pl.when(tt == 0)
        def _():
            x_tail[...] = jnp.zeros_like(x_tail)
            h_carry[...] = jnp.zeros_like(h_carry)

        x = xl_ref[...]
        xc = _conv_taps(x, x_tail[...], p_ref)
        x_tail[...] = x[tb - 8:tb]
        _, r, ig, ls, a, mult = _lru_gates(xc, p_ref, wrg_ref[0], wig_ref[0])
        u = mult * ig * xc
        row = lax.broadcasted_iota(jnp.int32, (tb, LRU_GROUP), 0)
        s = 1
        while s < tb:
            keep = row >= s
            u = jnp.where(keep, a * pltpu.roll(u, s, 0) + u, u)
            a = jnp.where(keep, a * pltpu.roll(a, s, 0), a)
            s *= 2
        h = u + a * h_carry[0:1, :]
        h_ref[...] = h
        h_carry[...] = jnp.broadcast_to(h[tb - 1:tb], h_carry.shape)

    blk = pl.BlockSpec((tb, LRU_GROUP), lambda g, tt: (tt, g))
    par = pl.BlockSpec((8, LRU_GROUP), lambda g, tt: (0, g))
    wsp = pl.BlockSpec((1, LRU_GROUP, LRU_GROUP), lambda g, tt: (g, 0, 0))
    return pl.pallas_call(
        body, name="lru_fwd", grid=(N_LRU_GROUP, t // tb), in_specs=[blk, par, wsp, wsp], out_specs=blk,
        out_shape=jax.ShapeDtypeStruct((t, D_LRU), F32),
        scratch_shapes=[pltpu.VMEM((8, LRU_GROUP), F32), pltpu.VMEM((8, LRU_GROUP), F32)],
        compiler_params=_params(2),
    )(xl, p, wrg2, wig2)


def _lru_bwd(dy, h, xl, gl, p, wrg2, wig2):
    t = xl.shape[0]
    tb = _lru_block_rows(t)
    n_tb = t // tb
    tb8 = tb // 8

    def body(dy_ref, h_ref, hprev_ref, xl_ref, xprev_ref, gl_ref, p_ref, wrg_ref, wig_ref,
             dxl_ref, dgl_ref, dp_ref, dwrg_ref, dwig_ref, g_carry, a_carry, dxc_head):
        step = pl.program_id(1)
        tt = n_tb - 1 - step
        first = step == 0

        _zero_at_first(first, g_carry, a_carry, dxc_head, dp_ref, dwrg_ref, dwig_ref)

        has_prev = (tt > 0).astype(F32)
        x = xl_ref[...]
        x_before = xprev_ref[...] * has_prev
        xs = [_shift_down(x, x_before, s) for s in range(4)]
        xc = xs[0] * p_ref[7:8, :] + xs[1] * p_ref[6:7, :] + xs[2] * p_ref[5:6, :] + xs[3] * p_ref[4:5, :] + p_ref[0:1, :]
        wrg = wrg_ref[0]
        wig = wig_ref[0]
        xcb, r, ig, ls, a, mult = _lru_gates(xc, p_ref, wrg, wig)

        hh = h_ref[...]
        h_m1 = _shift_down(hh, hprev_ref[...] * has_prev, 1)
        ge, dge = _gelu(gl_ref[...])
        dy = dy_ref[...]
        dgl_ref[...] = dy * hh * dge
        dh = dy * ge

        b = _shift_up(a, a_carry[...], 1)
        row = lax.broadcasted_iota(jnp.int32, (tb, LRU_GROUP), 0)
        g = dh
        s = 1
        while s < tb:
            keep = row < tb - s
            g = jnp.where(keep, b * pltpu.roll(g, tb - s, 0) + g, g)
            b = jnp.where(keep, b * pltpu.roll(b, tb - s, 0), b)
            s *= 2
        g = g + b * g_carry[0:1, :]
        g_carry[...] = jnp.broadcast_to(g[0:1], g_carry.shape)
        a_carry[...] = jnp.broadcast_to(a[0:1], a_carry.shape)

        da = g * h_m1
        dmult = g * ig * xc
        dig = g * mult * xc
        dxc = g * mult * ig
        dlog_a = da * a - dmult * (a * a) / mult
        dr = dlog_a * (LRU_C * ls)
        dls = jnp.sum(dlog_a * (LRU_C * r), axis=0, keepdims=True)
        dlam = dls * _sigmoid(-p_ref[3:4, :])
        dpre_r = dr * r * (1.0 - r)
        dpre_i = dig * ig * (1.0 - ig)
        dprb = dpre_r.astype(BF16)
        dpib = dpre_i.astype(BF16)
        dxc = dxc + _dot_nt(dprb, wrg) + _dot_nt(dpib, wig)
        dwrg_ref[0] += _dot_tn(xcb, dprb)
        dwig_ref[0] += _dot_tn(xcb, dpib)

        after = dxc_head[...]
        dxl = dxc * p_ref[7:8, :]
        for s in (1, 2, 3):
            dxl = dxl + _shift_up(dxc, after, s) * p_ref[7 - s:8 - s, :]
        dxl_ref[...] = dxl
        dxc_head[...] = dxc[0:8]

        rows = [jnp.sum(dxc, axis=0, keepdims=True), jnp.sum(dpre_r, axis=0, keepdims=True),
                jnp.sum(dpre_i, axis=0, keepdims=True), dlam]
        rows += [jnp.sum(dxc * xs[3 - k], axis=0, keepdims=True) for k in range(4)]
        dp_ref[...] += jnp.concatenate(rows, axis=0)

    blk = pl.BlockSpec((tb, LRU_GROUP), lambda g, s: (n_tb - 1 - s, g))
    prev8 = pl.BlockSpec((8, LRU_GROUP), lambda g, s: (jnp.maximum((n_tb - 1 - s) * tb8 - 1, 0), g))
    par = pl.BlockSpec((8, LRU_GROUP), lambda g, s: (0, g))
    wsp = pl.BlockSpec((1, LRU_GROUP, LRU_GROUP), lambda g, s: (g, 0, 0))
    return pl.pallas_call(
        body, name="lru_bwd", grid=(N_LRU_GROUP, n_tb),
        in_specs=[blk, blk, prev8, blk, prev8, blk, par, wsp, wsp], out_specs=[blk, blk, par, wsp, wsp],
        out_shape=[jax.ShapeDtypeStruct((t, D_LRU), F32), jax.ShapeDtypeStruct((t, D_LRU), F32),
                   jax.ShapeDtypeStruct((8, D_LRU), F32),
                   jax.ShapeDtypeStruct((N_LRU_GROUP, LRU_GROUP, LRU_GROUP), F32),
                   jax.ShapeDtypeStruct((N_LRU_GROUP, LRU_GROUP, LRU_GROUP), F32)],
        scratch_shapes=[pltpu.VMEM((8, LRU_GROUP), F32)] * 3,
        compiler_params=_params(2),
    )(dy, h, h, xl, xl, gl, p, wrg2, wig2)


def _attn_bias(first_block):
    qi = jnp.bitwise_and(lax.broadcasted_iota(jnp.int32, (4 * BLOCK_Q, 2 * BLOCK_Q), 0), BLOCK_Q - 1)
    kj = lax.broadcasted_iota(jnp.int32, (4 * BLOCK_Q, 2 * BLOCK_Q), 1)
    rel = qi + BLOCK_Q - kj
    mask = (rel >= 0) & (rel < BLOCK_Q)
    if first_block:
        mask = mask & (kj >= BLOCK_Q)
    return jnp.where(mask, 0.0, MASK_VALUE)


def _sink_column(sinks):
    hrow = lax.broadcasted_iota(jnp.int32, (4 * BLOCK_Q, 1), 0)
    return jnp.where(hrow < BLOCK_Q, sinks[0],
                     jnp.where(hrow < 2 * BLOCK_Q, sinks[1], jnp.where(hrow < 3 * BLOCK_Q, sinks[2], sinks[3])))


def _attn_scores(qv, kvv, n, bias, sk, lo):
    r0 = pl.multiple_of(n * BLOCK_Q, BLOCK_Q)
    rp = pl.multiple_of(jnp.maximum(n - 1, 0) * BLOCK_Q, BLOCK_Q)
    kvb = jnp.concatenate([kvv[pl.ds(rp, BLOCK_Q), :], kvv[pl.ds(r0, BLOCK_Q), :]], axis=0)
    k2 = kvb[:, 0:128]
    v2 = kvb[:, 128:256]
    qs = _stack_heads(qv[pl.ds(r0, BLOCK_Q), :], lo)
    s = _dot_nt(qs, k2) * ATTN_SCALE + bias
    m = jnp.maximum(jnp.max(s, axis=-1, keepdims=True), sk)
    e = jnp.exp(s - m)
    es = jnp.exp(sk - m)
    inv = 1.0 / (jnp.sum(e, axis=-1, keepdims=True) + es)
    return r0, rp, qs, k2, v2, e * inv, es * inv


def _stack_heads(pair2, lo):
    p0 = pair2[:, 0:128]
    p1 = pair2[:, 128:256]
    z = jnp.zeros_like(p0)
    return jnp.concatenate([jnp.where(lo, p0, z), jnp.where(lo, z, p0), jnp.where(lo, p1, z), jnp.where(lo, z, p1)], axis=0)


def _unstack_heads(st, lo):
    b = BLOCK_Q
    return jnp.concatenate([jnp.where(lo, st[0:b], st[b:2 * b]), jnp.where(lo, st[2 * b:3 * b], st[3 * b:4 * b])], axis=1)


def _attn_fwd(q, kv, sinks):
    t = q.shape[0]
    n_blk = t // BLOCK_Q

    def body(q_hbm, kv_hbm, s_ref, o_hbm, q2, kvv, o2, bias0, bias, sem):
        lo = lax.broadcasted_iota(jnp.int32, (BLOCK_Q, 128), 1) < HEAD_DIM
        cols = [pl.ds(256 * g, 256) for g in range(2)]
        loads = [pltpu.make_async_copy(kv_hbm, kvv, sem.at[0])]
        loads += [pltpu.make_async_copy(q_hbm.at[:, cols[g]], q2.at[g], sem.at[1 + g]) for g in range(2)]
        stores = [pltpu.make_async_copy(o2.at[g], o_hbm.at[:, cols[g]], sem.at[3 + g]) for g in range(2)]
        for cp in loads:
            cp.start()
        bias0[...] = _attn_bias(True)
        bias[...] = _attn_bias(False)
        loads[0].wait()
        for g in range(2):
            loads[1 + g].wait()
            qv, ov = q2.at[g], o2.at[g]
            sk = _sink_column([s_ref[0, HEAD_ORDER[4 * g + i]] for i in range(4)])

            def block(n, bias_ref):
                r0, _, _, _, v2, prob, _ = _attn_scores(qv, kvv, n, bias_ref[...], sk, lo)
                ov[pl.ds(r0, BLOCK_Q), :] = _unstack_heads(_dot(prob.astype(BF16), v2), lo)

            block(0, bias0)

            def later(n, carry):
                block(n, bias)
                return carry

            lax.fori_loop(1, n_blk, later, 0, unroll=ATTN_UNROLL)
            stores[g].start()
        for cp in stores:
            cp.wait()

    return pl.pallas_call(
        body, name="attn_fwd", in_specs=[ANY, ANY, SMEM], out_specs=ANY,
        out_shape=jax.ShapeDtypeStruct((t, D_ATTN), F32),
        scratch_shapes=[pltpu.VMEM((2, t, 256), BF16), pltpu.VMEM((t, 256), BF16), pltpu.VMEM((2, t, 256), F32),
                        pltpu.VMEM((4 * BLOCK_Q, 2 * BLOCK_Q), F32), pltpu.VMEM((4 * BLOCK_Q, 2 * BLOCK_Q), F32),
                        pltpu.SemaphoreType.DMA((5,))],
        compiler_params=_params(),
    )(q, kv, sinks)


def _attn_bwd(q, kv, do, sinks, exchange=None):
    t = q.shape[0]
    n_blk = t // BLOCK_Q
    host = _Host(exchange)

    def body(*refs):
        own, ex_refs = host.split(refs, 4, 3, 9)
        q_hbm, kv_hbm, do_hbm, s_ref, dq_hbm, dkv_hbm, dsink_ref, q2, kvv, do2, dqv, dkvv, ds_acc, bias0, bias, sem = own
        host.phase(0, ex_refs)
        lo = lax.broadcasted_iota(jnp.int32, (BLOCK_Q, 128), 1) < HEAD_DIM
        loads = [pltpu.make_async_copy(kv_hbm, kvv, sem.at[0])]
        for g in range(2):
            loads += [pltpu.make_async_copy(src.at[:, pl.ds(256 * g, 256)], dst.at[g], sem.at[1 + 2 * g + i])
                      for i, (src, dst) in enumerate(((q_hbm, q2), (do_hbm, do2)))]
        for cp in loads:
            cp.start()
        bias0[...] = _attn_bias(True)
        bias[...] = _attn_bias(False)
        loads[0].wait()
        for g in range(2):
            cols = pl.ds(256 * g, 256)
            for cp in loads[1 + 2 * g:3 + 2 * g]:
                cp.wait()
            qv, dov = q2.at[g], do2.at[g]
            heads = [HEAD_ORDER[4 * g + i] for i in range(4)]
            sk = _sink_column([s_ref[0, h] for h in heads])
            ds_acc[...] = jnp.zeros_like(ds_acc)

            def block(n, bias_ref, has_prev):
                r0, rp, qs, k2, v2, prob, psink = _attn_scores(qv, kvv, n, bias_ref[...], sk, lo)
                pb = prob.astype(BF16)
                dos = _stack_heads(dov[pl.ds(r0, BLOCK_Q), :], lo)
                dp = _dot_nt(dos, v2)
                dsum = jnp.sum(prob * dp, axis=-1, keepdims=True)
                dsb = (prob * (dp - dsum) * ATTN_SCALE).astype(BF16)
                ds_acc[...] -= psink * dsum
                dqv[pl.ds(r0, BLOCK_Q), :] = _unstack_heads(_dot(dsb, k2), lo).astype(BF16)
                dk2 = _dot_tn(dsb, qs)
                dv2 = _dot_tn(pb, dos)
                cur = jnp.concatenate([dk2[BLOCK_Q:], dv2[BLOCK_Q:]], axis=1)
                if g == 0:
                    dkvv[pl.ds(r0, BLOCK_Q), :] = cur
                else:
                    dkvv[pl.ds(r0, BLOCK_Q), :] += cur
                if has_prev:
                    dkvv[pl.ds(rp, BLOCK_Q), :] += jnp.concatenate([dk2[:BLOCK_Q], dv2[:BLOCK_Q]], axis=1)

            block(0, bias0, False)

            def later(n, carry):
                block(n, bias, True)
                return carry

            lax.fori_loop(1, n_blk, later, 0, unroll=ATTN_UNROLL)
            for i, h in enumerate(heads):
                tot = jnp.sum(ds_acc[BLOCK_Q * i:BLOCK_Q * (i + 1), :], axis=0, keepdims=True)
                dsink_ref[h:h + 1, :] = jnp.broadcast_to(tot, (1, 128))
            store = pltpu.make_async_copy(dqv, dq_hbm.at[:, cols], sem.at[5])
            store.start()
            store.wait()
        store = pltpu.make_async_copy(dkvv, dkv_hbm, sem.at[6])
        store.start()
        store.wait()
        if exchange is not None:
            for p in range(1, exchange.n_phases):
                host.phase(p, ex_refs)

    res = pl.pallas_call(
        body, name="attn_bwd", in_specs=[ANY, ANY, ANY, SMEM] + host.in_specs,
        out_specs=[ANY, ANY, pl.BlockSpec(memory_space=pltpu.VMEM)] + host.out_specs,
        out_shape=[jax.ShapeDtypeStruct((t, D_ATTN), BF16), jax.ShapeDtypeStruct((t, 256), F32),
                   jax.ShapeDtypeStruct((8, 128), F32)] + host.out_shape,
        scratch_shapes=[pltpu.VMEM((2, t, 256), BF16), pltpu.VMEM((t, 256), BF16), pltpu.VMEM((2, t, 256), BF16),
                        pltpu.VMEM((t, 256), BF16), pltpu.VMEM((t, 256), F32), pltpu.VMEM((4 * BLOCK_Q, 1), F32),
                        pltpu.VMEM((4 * BLOCK_Q, 2 * BLOCK_Q), F32), pltpu.VMEM((4 * BLOCK_Q, 2 * BLOCK_Q), F32),
                        pltpu.SemaphoreType.DMA((7,))] + host.scratch,
        compiler_params=_params(),
    )(q, kv, do, sinks, *host.args)
    return (*res[:3], list(res[3:]))


def _mix_out_fwd(x1, h, gl, o, g_lru, g_attn, g_post, w_o):
    t = x1.shape[0]
    tm = _token_tile(t)

    def body(x_ref, h_ref, gl_ref, o_ref, g1_ref, g2_ref, gp_ref, w_ref, x2_ref, m_ref):
        y = h_ref[...] * _gelu(gl_ref[...])[0]
        yn1 = _rms_fwd(y, g1_ref[...]).astype(BF16)
        yn2 = _rms_fwd(o_ref[...], g2_ref[...]).astype(BF16)
        m = _dot(yn1, w_ref[0:512, :]) + _dot(yn2, w_ref[512:1024, :])
        m_ref[...] = m
        x2_ref[...] = x_ref[...] + _rms_fwd(m, gp_ref[...])

    tok = pl.BlockSpec((tm, D_MODEL), lambda i: (i, 0))
    half = pl.BlockSpec((tm, 512), lambda i: (i, 0))
    vec = pl.BlockSpec((1, D_MODEL), lambda i: (0, 0))
    hvec = pl.BlockSpec((1, 512), lambda i: (0, 0))
    return pl.pallas_call(
        body, name="mix_out_fwd", grid=(t // tm,),
        in_specs=[tok, half, half, half, hvec, hvec, vec, pl.BlockSpec((D_MODEL, D_MODEL), lambda i: (0, 0))],
        out_specs=[tok, tok],
        out_shape=[jax.ShapeDtypeStruct((t, D_MODEL), F32), jax.ShapeDtypeStruct((t, D_MODEL), F32)],
        compiler_params=_params(1),
    )(x1, h, gl, o, g_lru, g_attn, g_post, w_o)


def _mix_out_bwd(dx2, m, h, gl, o, g_lru, g_attn, g_post, w_o):
    t = dx2.shape[0]
    tm = _token_tile(t)

    def body(dx_ref, m_ref, h_ref, gl_ref, o_ref, g1_ref, g2_ref, gp_ref, w_ref,
             dy_ref, do_ref, dw_ref, dgp_ref, dg1_ref, dg2_ref):
        _zero_at_first(pl.program_id(0) == 0, dw_ref, dgp_ref, dg1_ref, dg2_ref)
        dm, dgp = _rms_bwd(m_ref[...], gp_ref[...], dx_ref[...])
        dmb = dm.astype(BF16)
        y = h_ref[...] * _gelu(gl_ref[...])[0]
        o = o_ref[...]
        yn1 = _rms_fwd(y, g1_ref[...]).astype(BF16)
        yn2 = _rms_fwd(o, g2_ref[...]).astype(BF16)
        dw_ref[0:512, :] += _dot_tn(yn1, dmb)
        dw_ref[512:1024, :] += _dot_tn(yn2, dmb)
        dy, dg1 = _rms_bwd(y, g1_ref[...], _dot_nt(dmb, w_ref[0:512, :]))
        do, dg2 = _rms_bwd(o, g2_ref[...], _dot_nt(dmb, w_ref[512:1024, :]))
        dy_ref[...] = dy
        do_ref[...] = do.astype(BF16)
        dgp_ref[...] += dgp
        dg1_ref[...] += dg1
        dg2_ref[...] += dg2

    tok = pl.BlockSpec((tm, D_MODEL), lambda i: (i, 0))
    half = pl.BlockSpec((tm, 512), lambda i: (i, 0))
    vec = pl.BlockSpec((1, D_MODEL), lambda i: (0, 0))
    hvec = pl.BlockSpec((1, 512), lambda i: (0, 0))
    mat = pl.BlockSpec((D_MODEL, D_MODEL), lambda i: (0, 0))
    return pl.pallas_call(
        body, name="mix_out_bwd", grid=(t // tm,),
        in_specs=[tok, tok, half, half, half, hvec, hvec, vec, mat],
        out_specs=[half, half, mat, vec, hvec, hvec],
        out_shape=[jax.ShapeDtypeStruct((t, 512), F32), jax.ShapeDtypeStruct((t, 512), BF16),
                   jax.ShapeDtypeStruct((D_MODEL, D_MODEL), F32), jax.ShapeDtypeStruct((1, D_MODEL), F32),
                   jax.ShapeDtypeStruct((1, 512), F32), jax.ShapeDtypeStruct((1, 512), F32)],
        compiler_params=_params(1),
    )(dx2, m, h, gl, o, g_lru, g_attn, g_post, w_o)


def _mix_in_bwd(dx2, x1, g, dxl, dgl, dq, dkv, w_in, f1, g_post1):
    t = x1.shape[0]
    tm = _token_tile(t)

    def body(dx2_ref, x_ref, g_ref, dxl_ref, dgl_ref, dq_ref, dkv_ref, w_ref, f1_ref, gp1_ref,
             dx1_ref, dw_ref, dg_ref, df1_ref, dgp1_ref):
        _zero_at_first(pl.program_id(0) == 0, dw_ref, dg_ref, dgp1_ref)
        x = x_ref[...]
        nb = _rms_fwd(x, g_ref[...]).astype(BF16)
        dproj = jnp.concatenate([dxl_ref[...].astype(BF16), dgl_ref[...].astype(BF16), dq_ref[...],
                                 dkv_ref[...].astype(BF16)], axis=1)
        dw_ref[...] += _dot_tn(nb, dproj)
        dx, dg = _rms_bwd(x, g_ref[...], _dot(dproj, w_ref[...]))
        dx1 = dx2_ref[...] + dx
        dx1_ref[...] = dx1
        dg_ref[...] += dg
        df1, dgp1 = _rms_bwd(f1_ref[...], gp1_ref[...], 0.5 * dx1)
        df1_ref[...] = df1.astype(BF16)
        dgp1_ref[...] += dgp1

    tok = pl.BlockSpec((tm, D_MODEL), lambda i: (i, 0))
    half = pl.BlockSpec((tm, 512), lambda i: (i, 0))
    vec = pl.BlockSpec((1, D_MODEL), lambda i: (0, 0))
    mat = pl.BlockSpec((D_IN, D_MODEL), lambda i: (0, 0))
    dmat = pl.BlockSpec((D_MODEL, D_IN), lambda i: (0, 0))
    quarter = pl.BlockSpec((tm, 256), lambda i: (i, 0))
    return pl.pallas_call(
        body, name="mix_in_bwd", grid=(t // tm,),
        in_specs=[tok, tok, vec, half, half, half, quarter, mat, tok, vec], out_specs=[tok, dmat, vec, tok, vec],
        out_shape=[jax.ShapeDtypeStruct((t, D_MODEL), F32), jax.ShapeDtypeStruct((D_MODEL, D_IN), F32),
                   jax.ShapeDtypeStruct((1, D_MODEL), F32), jax.ShapeDtypeStruct((t, D_MODEL), BF16),
                   jax.ShapeDtypeStruct((1, D_MODEL), F32)],
        compiler_params=_params(1),
    )(dx2, x1, g, dxl, dgl, dq, dkv, w_in, f1, g_post1)


def _half(rows):
    return rows // 2


def _chip_sums(grads, from_sibling, other, name):
    n_arr = len(grads)

    def body(other_ref, *refs):
        for a in range(n_arr):
            refs[2 * n_arr + a][0] = (refs[2 * a][0, 0] + refs[2 * a + 1][0]).astype(BF16)

    in_specs, out_specs, out_shape, args = [], [], [], []
    for g, s in zip(grads, from_sibling):
        _, rows, cols = g.shape
        tr = _half(rows)
        in_specs += [pl.BlockSpec((1, 1, tr, cols), lambda j, i, other: (other[j], other[3], i, 0)),
                     pl.BlockSpec((1, tr, cols), lambda j, i, other: (other[j], i, 0))]
        out_specs.append(pl.BlockSpec((1, tr, cols), lambda j, i, other: (j, i, 0)))
        out_shape.append(jax.ShapeDtypeStruct((3, rows, cols), BF16))
        args += [g.reshape(4, 2, rows, cols), s]
    grid_spec = pltpu.PrefetchScalarGridSpec(num_scalar_prefetch=1, grid=(3, 2), in_specs=in_specs, out_specs=out_specs)
    return pl.pallas_call(body, name=name, grid_spec=grid_spec, out_shape=out_shape, compiler_params=_params(2))(other, *args)


def _adamw(w, g, m, v):
    m = ADAM_B1 * m + (1.0 - ADAM_B1) * g
    v = ADAM_B2 * v + (1.0 - ADAM_B2) * (g * g)
    m_hat = m / (1.0 - ADAM_B1 ** ADAM_STEP)
    v_hat = v / (1.0 - ADAM_B2 ** ADAM_STEP)
    delta = -ADAM_LR * (m_hat / (jnp.sqrt(v_hat) + ADAM_EPS) + ADAM_WD * w)
    return delta, m, v


def _shard_updates(grads, from_sibling, from_chips, w, m, v, place, name, transposed):
    n_arr = len(grads)

    def total(g_ref, s_ref, c_ref):
        g = g_ref[0, 0] + s_ref[0]
        g = g + c_ref[0].astype(F32)
        g = g + c_ref[1].astype(F32)
        return g + c_ref[2].astype(F32)

    part_specs, parts, flat, shapes = [], [], [], []
    for g in grads:
        _, rows, cols = g.shape
        tr = _half(rows)
        part_specs.append([pl.BlockSpec((1, 1, tr, cols), lambda i, place: (place[0], place[1], i, 0)),
                           pl.BlockSpec((1, tr, cols), lambda i, place: (place[0], i, 0)),
                           pl.BlockSpec((3, tr, cols), lambda i, place: (0, i, 0))])
        flat.append(pl.BlockSpec((tr, cols), lambda i, place: (i, 0)))
        shapes.append(jax.ShapeDtypeStruct((rows, cols), F32))
    for g, s, c in zip(grads, from_sibling, from_chips):
        parts += [g.reshape(4, 2, *g.shape[1:]), s, c]

    if not transposed:
        def body(place_ref, *refs):
            ins, wmv, outs = refs[:3 * n_arr], refs[3 * n_arr:6 * n_arr], refs[6 * n_arr:]
            for a in range(n_arr):
                g = total(*ins[3 * a:3 * a + 3])
                outs[4 * a][...] = g
                outs[4 * a + 1][...], outs[4 * a + 2][...], outs[4 * a + 3][...] = _adamw(
                    wmv[3 * a][...], g, wmv[3 * a + 1][...], wmv[3 * a + 2][...])

        grid_spec = pltpu.PrefetchScalarGridSpec(
            num_scalar_prefetch=1, grid=(2,),
            in_specs=[sp for specs in part_specs for sp in specs] + [f for f in flat for _ in range(3)],
            out_specs=[f for f in flat for _ in range(4)])
        res = pl.pallas_call(body, name=name, grid_spec=grid_spec, out_shape=[sh for sh in shapes for _ in range(4)],
                             compiler_params=_params(1))(place, *parts, *[x for wmv in zip(w, m, v) for x in wmv])
        return [tuple(res[4 * a:4 * a + 4]) for a in range(n_arr)]

    def sum_body(place_ref, *refs):
        for a in range(n_arr):
            refs[3 * n_arr + a][...] = total(*refs[3 * a:3 * a + 3])

    grid_spec = pltpu.PrefetchScalarGridSpec(num_scalar_prefetch=1, grid=(2,),
                                             in_specs=[sp for specs in part_specs for sp in specs], out_specs=flat)
    sums = pl.pallas_call(sum_body, name=name + "_sum", grid_spec=grid_spec, out_shape=shapes,
                          compiler_params=_params(1))(place, *parts)
    turned = [jnp.transpose(g, (1, 0)) for g in sums]

    def adam_body(*refs):
        ins, outs = refs[:4 * n_arr], refs[4 * n_arr:]
        for a in range(n_arr):
            g = ins[4 * a][...]
            outs[4 * a][...] = g
            outs[4 * a + 1][...], outs[4 * a + 2][...], outs[4 * a + 3][...] = _adamw(
                ins[4 * a + 1][...], g, ins[4 * a + 2][...], ins[4 * a + 3][...])

    blks = [pl.BlockSpec((g.shape[0] // 4, g.shape[1]), lambda i: (i, 0)) for g in turned]
    res = pl.pallas_call(
        adam_body, name=name + "_adam", grid=(4,), in_specs=[b for b in blks for _ in range(4)],
        out_specs=[b for b in blks for _ in range(4)],
        out_shape=[jax.ShapeDtypeStruct(g.shape, F32) for g in turned for _ in range(4)], compiler_params=_params(1),
    )(*[x for gwmv in zip(turned, w, m, v) for x in gwmv])
    return [tuple(res[4 * a:4 * a + 4]) for a in range(n_arr)]


GAINS = ("ffn1_pre_g", "ffn1_post_g", "mix_pre_g", "mix_post_g", "ffn2_pre_g", "ffn2_post_g")
HALVES = ("conv_b", "b_rg", "b_ig", "lru_lambda", "g_lru_out", "g_attn_out")
GATES = ("w_rg", "w_ig")
SMALL = GAINS + HALVES + GATES + ("sinks", "conv_w")


def _small_update(gathered, w, m, v):
    n_small = len(SMALL)

    def body(*refs):
        ga_ref, gb_ref, gc_ref, gd_ref, g0_ref, gconv_ref = refs[:6]
        wmv = refs[6:6 + 3 * n_small]
        outs = refs[6 + 3 * n_small:6 + 7 * n_small]
        loss_ref = refs[6 + 7 * n_small]

        def total(ref):
            s = ref[0]
            for d in range(1, N_DEV):
                s = s + ref[d]
            return s

        sa, sb, sc, sd = total(ga_ref), total(gb_ref), total(gc_ref), total(gd_ref)
        grads = {}
        for i, k in enumerate(GAINS):
            grads[k] = sa[i:i + 1]
        grads[GAINS[0]] = total(g0_ref)
        for i, k in enumerate(HALVES):
            grads[k] = sb[i:i + 1]
        grads["w_rg"], grads["w_ig"] = sc[0:512], sc[512:1024]
        grads["sinks"] = sd[4:5, 0:8]
        grads["conv_w"] = total(gconv_ref)
        for i, k in enumerate(SMALL):
            g = grads[k]
            outs[4 * i][...] = g
            outs[4 * i + 1][...], outs[4 * i + 2][...], outs[4 * i + 3][...] = _adamw(
                wmv[3 * i][...], g, wmv[3 * i + 1][...], wmv[3 * i + 2][...])
        loss_ref[...] = jnp.broadcast_to(sd[5:6, 0:128], loss_ref.shape)

    operands = list(gathered)
    out_shape = []
    for k in SMALL:
        operands += [w[k], m[k], v[k]]
        out_shape += [jax.ShapeDtypeStruct(w[k].shape, F32)] * 4
    out_shape.append(jax.ShapeDtypeStruct((8, 128), F32))
    res = pl.pallas_call(body, name="small_update", out_shape=out_shape, compiler_params=_params())(*operands)
    parts = [{k: res[4 * i + j] for i, k in enumerate(SMALL)} for j in range(4)]
    return (*parts, res[-1])


def _reorder_heads(a, axis, start, order):
    def slab(h):
        return lax.slice_in_dim(a, start + HEAD_DIM * h, start + HEAD_DIM * (h + 1), axis=axis)

    parts = [lax.slice_in_dim(a, 0, start, axis=axis)] + [slab(h) for h in order]
    parts.append(lax.slice_in_dim(a, start + 8 * HEAD_DIM, a.shape[axis], axis=axis))
    return jnp.concatenate(parts, axis=axis)


HEAD_ORDER_INVERSE = tuple(HEAD_ORDER.index(h) for h in range(8))


def _pair_block_diag(w):
    w = w.reshape(N_LRU_GROUP, 2, 64, 64)
    z = jnp.zeros((N_LRU_GROUP, 64, 64), w.dtype)
    top = jnp.concatenate([w[:, 0], z], axis=2)
    bot = jnp.concatenate([z, w[:, 1]], axis=2)
    return jnp.concatenate([top, bot], axis=1)


def _pair_block_diag_grad(dw2):
    return jnp.stack([dw2[:, :64, :64], dw2[:, 64:, 64:]], axis=1).reshape(512, 64)


def kernel(x, ffn1_pre_g, ffn1_w_gu, ffn1_w_down, ffn1_post_g, mix_pre_g, w_in, conv_w, conv_b, w_rg, b_rg, w_ig, b_ig, lru_lambda, sinks, g_lru_out, g_attn_out, w_o, mix_post_g, ffn2_pre_g, ffn2_w_gu, ffn2_w_down, ffn2_post_g, loss_target, m_ffn1_pre_g, m_ffn1_w_gu, m_ffn1_w_down, m_ffn1_post_g, m_mix_pre_g, m_w_in, m_conv_w, m_conv_b, m_w_rg, m_b_rg, m_w_ig, m_b_ig, m_lru_lambda, m_sinks, m_g_lru_out, m_g_attn_out, m_w_o, m_mix_post_g, m_ffn2_pre_g, m_ffn2_w_gu, m_ffn2_w_down, m_ffn2_post_g, v_ffn1_pre_g, v_ffn1_w_gu, v_ffn1_w_down, v_ffn1_post_g, v_mix_pre_g, v_w_in, v_conv_w, v_conv_b, v_w_rg, v_b_rg, v_w_ig, v_b_ig, v_lru_lambda, v_sinks, v_g_lru_out, v_g_attn_out, v_w_o, v_mix_post_g, v_ffn2_pre_g, v_ffn2_w_gu, v_ffn2_w_down, v_ffn2_post_g):
    args = dict(locals())
    names = ["ffn1_pre_g", "ffn1_w_gu", "ffn1_w_down", "ffn1_post_g", "mix_pre_g", "w_in", "conv_w", "conv_b", "w_rg",
             "b_rg", "w_ig", "b_ig", "lru_lambda", "sinks", "g_lru_out", "g_attn_out", "w_o", "mix_post_g",
             "ffn2_pre_g", "ffn2_w_gu", "ffn2_w_down", "ffn2_post_g"]
    big = ["ffn1_w_gu", "ffn1_w_down", "w_in", "w_o", "ffn2_w_gu", "ffn2_w_down"]
    w = {k: args[k] for k in names}
    mom = {k: args["m_" + k] for k in names}
    var = {k: args["v_" + k] for k in names}
    t = x.shape[1]
    xs = x.reshape(t, D_MODEL)
    target = loss_target.reshape(t, D_MODEL)
    cx, cy, cc = _coords()
    me = 4 * cx + 2 * cy + cc
    other = jnp.stack([2 * (1 - cx) + cy, 2 * cx + (1 - cy), 2 * (1 - cx) + (1 - cy), cc]).astype(jnp.int32)
    place = jnp.stack([2 * cx + cy, cc]).astype(jnp.int32)

    transposed = ("ffn1_w_gu", "w_in", "ffn2_w_gu")

    def shard_view(a, k):
        return jnp.transpose(a[0], (1, 0)) if k in transposed else a[0]

    def shard_unview(a, k):
        return (jnp.transpose(a, (1, 0)) if k in transposed else a)[None]

    shard2d = {k: shard_view(w[k], k) for k in big}
    shard_bf = {k: shard2d[k].astype(BF16) for k in big}
    conv_pad = jnp.pad(conv_w.reshape(4, 64), ((0, 4), (0, 64)))
    (first_w,) = _run_exchanges([_Gather([shard_bf["ffn1_w_gu"], shard_bf["ffn1_w_down"]], routed=True)], "all_gather_ffn1")
    wgu1 = first_w[0].reshape(2, N_CHUNK, CHUNK, D_MODEL)
    wd1 = first_w[1].reshape(N_CHUNK, CHUNK, D_MODEL)
    rest = _Gather([shard_bf["w_in"], shard_bf["w_o"], shard_bf["ffn2_w_gu"], shard_bf["ffn2_w_down"], conv_pad])

    x1, f1, n1, gu1, gathered = _ffn_fwd(xs, ffn1_pre_g, wgu1, wd1, ffn1_post_g, None, "ffn1_fwd", rest)
    w_in_full = _reorder_heads(gathered[0].reshape(D_IN, D_MODEL), 0, 2 * D_LRU, HEAD_ORDER)
    w_o_full = _reorder_heads(gathered[1].reshape(D_MODEL, D_MODEL), 0, D_LRU, HEAD_ORDER)
    g_attn_heads = _reorder_heads(g_attn_out, 1, 0, HEAD_ORDER)
    wgu2 = gathered[2].reshape(2, N_CHUNK, CHUNK, D_MODEL)
    wd2 = gathered[3].reshape(N_CHUNK, CHUNK, D_MODEL)
    conv_w_full = jnp.transpose(gathered[4][:, 0:4, 0:64], (1, 0, 2)).reshape(4, D_LRU)
    p_lru = jnp.concatenate([conv_b, b_rg, b_ig, lru_lambda, conv_w_full], axis=0)
    wrg2 = _pair_block_diag(w_rg[0]).astype(BF16)
    wig2 = _pair_block_diag(w_ig[0]).astype(BF16)
    xl, gl, q, kv = _mix_in_fwd(x1, mix_pre_g, w_in_full)
    h = _lru_fwd(xl, p_lru, wrg2, wig2)
    o = _attn_fwd(q, kv, sinks)
    x2, mo = _mix_out_fwd(x1, h, gl, o, g_lru_out, g_attn_heads, mix_post_g, w_o_full)
    g = {}
    dx3, n2, df2, gu2, g["ffn2_post_g"], loss_parts, _ = _ffn_fwd(x2, ffn2_pre_g, wgu2, wd2, ffn2_post_g, target, "ffn2_fwd")
    loss_local = jnp.sum(loss_parts[::8, 0])

    partial, from_sibling, from_chips = {}, {}, {}

    def chip_sums(keys):
        return _chip_sums([partial[k] for k in keys], [from_sibling[k] for k in keys], other, "chip_sum_" + keys[0])

    dgu2, dwgu2, dwd2, _ = _ffn_bwd_w(n2, df2, gu2, wd2, "ffn2_bwd_w")
    partial["ffn2_w_gu"] = dwgu2.reshape(N_DEV, D_MODEL, CHUNK)
    partial["ffn2_w_down"] = dwd2.reshape(N_DEV, D_FF // N_DEV, D_MODEL)
    ffn2_keys = ["ffn2_w_gu", "ffn2_w_down"]
    dx2, g["ffn2_pre_g"], got = _ffn_bwd_x(dgu2, wgu2, x2, ffn2_pre_g, dx3, "ffn2_bwd_x",
                                           _SiblingExchange([partial[k] for k in ffn2_keys]))
    from_sibling.update(zip(ffn2_keys, got))
    dy, do, dwo, g["mix_post_g"], g["g_lru_out"], dg_attn_heads = _mix_out_bwd(
        dx2, mo, h, gl, o, g_lru_out, g_attn_heads, mix_post_g, w_o_full)
    g["g_attn_out"] = _reorder_heads(dg_attn_heads, 1, 0, HEAD_ORDER_INVERSE)
    dwo = _reorder_heads(dwo, 0, D_LRU, HEAD_ORDER_INVERSE)
    dq, dkv, dsink, got = _attn_bwd(q, kv, do, sinks, _ChipExchange(chip_sums(ffn2_keys)))
    from_chips.update(zip(ffn2_keys, got))
    dxl, dgl, dp, dwrg2, dwig2 = _lru_bwd(dy, h, xl, gl, p_lru, wrg2, wig2)
    dx1, dwin, g["mix_pre_g"], df1, g["ffn1_post_g"] = _mix_in_bwd(
        dx2, x1, mix_pre_g, dxl, dgl, dq, dkv, w_in_full, f1, ffn1_post_g)
    dwin = _reorder_heads(dwin, 1, 2 * D_LRU, HEAD_ORDER_INVERSE)
    partial["w_in"] = jnp.transpose(dwin.reshape(D_MODEL, N_DEV, D_IN // N_DEV), (1, 0, 2))
    partial["w_o"] = dwo.reshape(N_DEV, D_MODEL // N_DEV, D_MODEL)
    mix_keys = ["w_in", "w_o"]
    (got,) = _run_exchanges([_SiblingExchange([partial[k] for k in mix_keys])], "mix_sibling_exchange")
    from_sibling.update(zip(mix_keys, got))
    dgu1, dwgu1, dwd1, got = _ffn_bwd_w(n1, df1, gu1, wd1, "ffn1_bwd_w", _ChipExchange(chip_sums(mix_keys)))
    from_chips.update(zip(mix_keys, got))
    partial["ffn1_w_gu"] = dwgu1.reshape(N_DEV, D_MODEL, CHUNK)
    partial["ffn1_w_down"] = dwd1.reshape(N_DEV, D_FF // N_DEV, D_MODEL)
    ffn1_keys = ["ffn1_w_gu", "ffn1_w_down"]
    (got,) = _run_exchanges([_SiblingExchange([partial[k] for k in ffn1_keys])], "ffn1_sibling_exchange")
    from_sibling.update(zip(ffn1_keys, got))
    zeros2 = jnp.zeros((2, D_MODEL), F32)
    g_gains = jnp.concatenate([zeros2[:1]] + [g[k] for k in GAINS[1:]] + [zeros2], axis=0)
    g_halves = jnp.concatenate([dp[0:4], g["g_lru_out"], g["g_attn_out"], zeros2[:, :D_LRU]], axis=0)
    g_gates = jnp.concatenate([_pair_block_diag_grad(dwrg2), _pair_block_diag_grad(dwig2)], axis=0)
    g_misc = jnp.concatenate([dp[4:8], jnp.pad(dsink[:, 0].reshape(1, 8), ((0, 0), (0, D_LRU - 8))),
                              jnp.pad(loss_local.reshape(1, 1), ((0, 0), (0, D_LRU - 1))), zeros2[:, :D_LRU]], axis=0)
    dx0, g_first, got = _ffn_bwd_x(dgu1, wgu1, xs, ffn1_pre_g, dx1, "ffn1_bwd_x",
                                   _Both(_ChipExchange(chip_sums(ffn1_keys)), _Gather([g_gains, g_halves, g_gates, g_misc])))
    from_chips.update(zip(ffn1_keys, got[:2]))
    gathered_small = got[2:]

    grads, delta, new_m, new_v = {}, {}, {}, {}
    for name, keys, turned in (("update_column_sharded", transposed, True),
                               ("update_row_sharded", tuple(k for k in big if k not in transposed), False)):
        res = _shard_updates([partial[k] for k in keys], [from_sibling[k] for k in keys], [from_chips[k] for k in keys],
                             [shard2d[k] for k in keys], [shard_view(mom[k], k) for k in keys],
                             [shard_view(var[k], k) for k in keys], place, name, turned)
        for k, out in zip(keys, res):
            grads[k], delta[k], new_m[k], new_v[k] = [shard_unview(r, k) for r in out]

    ((gathered_first,),) = _run_exchanges([_Gather([g_first])], "all_gather_first_gain")
    conv_parts = lax.dynamic_slice(gathered_small[3], (0, 0, me * 64), (N_DEV, 4, 64))

    def small_view(vals):
        out = {k: vals[k] for k in GAINS + HALVES + ("sinks",)}
        out.update({k: vals[k].reshape(512, 64) for k in GATES})
        out["conv_w"] = vals["conv_w"].reshape(4, 64)
        return out

    *small, loss_tile = _small_update([*gathered_small, gathered_first, conv_parts], small_view(w), small_view(mom),
                                      small_view(var))
    for dst, part in zip((grads, delta, new_m, new_v), small):
        for k in SMALL:
            dst[k] = part[k].reshape(w[k].shape)
    return (loss_tile[0, 0], dx0.reshape(x.shape), *[grads[k] for k in names], *[delta[k] for k in names],
            *[new_m[k] for k in names], *[new_v[k] for k in names])
```

```python
import functools

import jax
import jax.numpy as jnp
from jax import lax
from jax.experimental import pallas as pl
from jax.experimental.pallas import tpu as pltpu

F32 = jnp.float32
BF16 = jnp.bfloat16

D_MODEL = 1024
D_FF = 2816
N_DEV = 8
N_CHUNK = 4
CHUNK = D_FF // N_CHUNK
D_LRU = 512
D_ATTN = 512
LRU_GROUP = 128
N_LRU_GROUP = D_LRU // LRU_GROUP
LRU_GROUPS_PER_STEP = 4
HEAD_DIM = 64
BLOCK_Q = 128
D_IN = 1792
HEAD_ORDER = (0, 4, 1, 5, 2, 6, 3, 7)
ATTN_UNROLL = 9
RMS_EPS = 1e-6
LRU_C = 8.0
MASK_VALUE = -1e30
ATTN_SCALE = HEAD_DIM ** -0.5

ADAM_LR = 0.001
ADAM_B1 = 0.9
ADAM_B2 = 0.999
ADAM_EPS = 1e-08
ADAM_WD = 0.01
ADAM_STEP = 10

VMEM_LIMIT_V7X = 56 * 2 ** 20

ANY = pl.BlockSpec(memory_space=pl.ANY)
SMEM = pl.BlockSpec(memory_space=pltpu.SMEM)
MESH = pl.DeviceIdType.MESH


def _params(n_grid=0):
    sem = ("arbitrary",) * n_grid if n_grid else None
    return pltpu.CompilerParams(dimension_semantics=sem, vmem_limit_bytes=VMEM_LIMIT_V7X)


def _dot(a, b):
    return lax.dot_general(a, b, (((1,), (0,)), ((), ())), preferred_element_type=F32)


def _dot_nt(a, b):
    return lax.dot_general(a, b, (((1,), (1,)), ((), ())), preferred_element_type=F32)


def _dot_tn(a, b):
    return lax.dot_general(a, b, (((0,), (0,)), ((), ())), preferred_element_type=F32)


def _sigmoid(x):
    return 1.0 / (1.0 + jnp.exp(-x))


def _rms_fwd(x, g):
    r = lax.rsqrt(jnp.mean(x * x, axis=-1, keepdims=True) + RMS_EPS)
    return x * r * g


def _rms_bwd(x, g, dy):
    r = lax.rsqrt(jnp.mean(x * x, axis=-1, keepdims=True) + RMS_EPS)
    xh = x * r
    dg = jnp.sum(dy * xh, axis=0, keepdims=True)
    dxh = dy * g
    dx = r * (dxh - xh * jnp.mean(dxh * xh, axis=-1, keepdims=True))
    return dx, dg


def _gelu(x):
    c = 0.7978845608028654
    inner = c * (x + 0.044715 * x * x * x)
    th = jnp.tanh(inner)
    ge = 0.5 * x * (1.0 + th)
    dge = 0.5 * (1.0 + th) + 0.5 * x * (1.0 - th * th) * c * (1.0 + 3.0 * 0.044715 * x * x)
    return ge, dge


def _zero_at_first(first, *refs):
    @pl.when(first)
    def _():
        for ref in refs:
            ref[...] = jnp.zeros_like(ref)


def _token_tile(t):
    return 512 if t >= 2048 else t // 2


def _ffn_bwd_tile(t):
    return 1024 if t >= 4096 else t // 2


def _coords():
    return lax.axis_index("x"), lax.axis_index("y"), lax.axis_index("c")


class _Gather:
    n_phases = 3
    at = (0.0, 0.8, 1.0)

    def __init__(self, shards, routed=False):
        k = len(shards)
        self.routed = routed
        self.arrays = list(shards)
        self.out_shape = [jax.ShapeDtypeStruct((N_DEV,) + s.shape, s.dtype) for s in shards]
        self.scratch = [pltpu.SemaphoreType.DMA((7 * k,)), pltpu.SemaphoreType.DMA((7 * k,)), pltpu.SemaphoreType.DMA((k,))]

    def run(self, phase, ins, outs, sems):
        send_sems, recv_sems, local_sems = sems
        k_arr = len(ins)
        x, y, c = _coords()
        me, sibling = (x, y, c), (x, y, 1 - c)
        chips = [(1 - x, y), (x, 1 - y), (1 - x, 1 - y)]
        direct = 2 if self.routed else 3
        relay_from = (x + (1 - c) * (1 - 2 * x), y + c * (1 - 2 * y))
        relay_to = (x + c * (1 - 2 * x), y + (1 - c) * (1 - 2 * y))

        def rows(k, dev):
            return outs[k].at[4 * dev[0] + 2 * dev[1] + dev[2]]

        def copy(k, slot, block, to, src=None):
            return pltpu.make_async_remote_copy(
                src_ref=rows(k, block) if src is None else src, dst_ref=rows(k, block),
                send_sem=send_sems.at[7 * k + slot], recv_sem=recv_sems.at[7 * k + slot],
                device_id=to, device_id_type=MESH)

        def mine():
            return [pltpu.make_async_copy(ins[k], rows(k, me), local_sems.at[k]) for k in range(k_arr)]

        def first():
            return [copy(k, slot, me, to, src=ins[k]) for k in range(k_arr)
                    for slot, to in enumerate([sibling] + [(*chip, c) for chip in chips[:direct]])]

        def relayed(k):
            return copy(k, 3, (*relay_from, c), (*relay_to, c))

        def passed(j, k):
            return copy(k, 4 + j, (*chips[j], c), sibling)

        if phase == 0:
            for cp in mine() + first():
                cp.start()
        elif phase == 1:
            if self.routed:
                for k in range(k_arr):
                    copy(k, 1 + c, (*relay_from, c), me).wait_recv()
                    relayed(k).start()
                for k in range(k_arr):
                    copy(k, 2 - c, (*relay_to, c), me).wait_recv()
            else:
                for j in range(direct):
                    for k in range(k_arr):
                        copy(k, 1 + j, (*chips[j], c), me).wait_recv()
            for k in range(k_arr):
                for j in range(direct):
                    passed(j, k).start()
        else:
            for k in range(k_arr):
                if self.routed:
                    copy(k, 3, (*chips[2], c), me).wait_recv()
                    passed(2, k).start()
            for k in range(k_arr):
                copy(k, 0, sibling, me).wait_recv()
                for j, chip in enumerate(chips):
                    copy(k, 4 + j, (*chip, 1 - c), me).wait_recv()
            sent = first() + [passed(j, k) for j in range(3) for k in range(k_arr)]
            if self.routed:
                sent += [relayed(k) for k in range(k_arr)]
            for cp in sent:
                cp.wait_send()
            for cp in mine():
                cp.wait()


class _SiblingExchange:
    n_phases = 2
    at = (0.0, 1.0)

    def __init__(self, grads):
        k = len(grads)
        self.arrays = list(grads)
        self.out_shape = [jax.ShapeDtypeStruct((4,) + g.shape[1:], g.dtype) for g in grads]
        self.scratch = [pltpu.SemaphoreType.DMA((4 * k,)), pltpu.SemaphoreType.DMA((4 * k,))]

    def run(self, phase, ins, outs, sems):
        send_sems, recv_sems = sems
        x, y, c = _coords()
        copies = [pltpu.make_async_remote_copy(
            src_ref=ins[k].at[2 * q + (1 - c)], dst_ref=outs[k].at[q],
            send_sem=send_sems.at[4 * k + q], recv_sem=recv_sems.at[4 * k + q],
            device_id=(x, y, 1 - c), device_id_type=MESH) for k in range(len(ins)) for q in range(4)]
        for cp in copies:
            if phase == 0:
                cp.start()
            else:
                cp.wait_recv()
                cp.wait_send()


class _ChipExchange:
    n_phases = 2
    at = (0.0, 1.0)

    def __init__(self, chip_sums):
        k = len(chip_sums)
        self.arrays = list(chip_sums)
        self.out_shape = [jax.ShapeDtypeStruct((3,) + s.shape[1:], s.dtype) for s in chip_sums]
        self.scratch = [pltpu.SemaphoreType.DMA((3 * k,)), pltpu.SemaphoreType.DMA((3 * k,))]

    def run(self, phase, ins, outs, sems):
        send_sems, recv_sems = sems
        x, y, c = _coords()
        chips = [(1 - x, y), (x, 1 - y), (1 - x, 1 - y)]
        copies = [pltpu.make_async_remote_copy(
            src_ref=ins[k].at[j], dst_ref=outs[k].at[j],
            send_sem=send_sems.at[3 * k + j], recv_sem=recv_sems.at[3 * k + j],
            device_id=(*chip, c), device_id_type=MESH) for k in range(len(ins)) for j, chip in enumerate(chips)]
        for cp in copies:
            if phase == 0:
                cp.start()
            else:
                cp.wait_recv()
                cp.wait_send()


class _Both:
    n_phases = 3
    at = (0.0, 0.95, 1.0)

    def __init__(self, two_phase, gather):
        self.parts = (two_phase, gather)
        self.arrays = two_phase.arrays + gather.arrays
        self.out_shape = two_phase.out_shape + gather.out_shape
        self.scratch = two_phase.scratch + gather.scratch

    def run(self, phase, ins, outs, sems):
        a, b = self.parts
        n_in, n_out, n_sem = len(a.arrays), len(a.out_shape), len(a.scratch)
        refs_a = (ins[:n_in], outs[:n_out], sems[:n_sem])
        refs_b = (ins[n_in:], outs[n_out:], sems[n_sem:])
        b.run(phase, *refs_b)
        if phase == 0:
            a.run(0, *refs_a)
        if phase == 2:
            a.run(1, *refs_a)


class _Host:
    def __init__(self, exchange):
        self.ex = exchange
        self.args = [] if exchange is None else exchange.arrays
        self.in_specs = [ANY] * len(self.args)
        self.out_shape = [] if exchange is None else exchange.out_shape
        self.out_specs = [ANY] * len(self.out_shape)
        self.scratch = [] if exchange is None else exchange.scratch

    def split(self, refs, n_in, n_out, n_scratch):
        a, b, s = len(self.args), len(self.out_shape), len(self.scratch)
        own_in, ex_in = refs[:n_in], refs[n_in:n_in + a]
        rest = refs[n_in + a:]
        own_out, ex_out = rest[:n_out], rest[n_out:n_out + b]
        rest = rest[n_out + b:]
        own_scratch, ex_sems = rest[:n_scratch], rest[n_scratch:n_scratch + s]
        return list(own_in) + list(own_out) + list(own_scratch), (ex_in, ex_out, ex_sems)

    def at_steps(self, step, n_steps, ex_refs):
        if self.ex is None:
            return
        for p in range(self.ex.n_phases):
            pl.when(step == int(round(self.ex.at[p] * (n_steps - 1))))(functools.partial(self.ex.run, p, *ex_refs))

    def phase(self, p, ex_refs):
        if self.ex is not None:
            self.ex.run(p, *ex_refs)


def _run_exchanges(exchanges, name):
    hosts = [_Host(ex) for ex in exchanges]
    n_in = [len(h.args) for h in hosts]
    n_out = [len(h.out_shape) for h in hosts]
    n_sc = [len(h.scratch) for h in hosts]

    def body(*refs):
        ins, outs, scr = refs[:sum(n_in)], refs[sum(n_in):sum(n_in) + sum(n_out)], refs[sum(n_in) + sum(n_out):]
        parts = []
        for e in range(len(hosts)):
            parts.append((ins[sum(n_in[:e]):sum(n_in[:e + 1])], outs[sum(n_out[:e]):sum(n_out[:e + 1])],
                          scr[sum(n_sc[:e]):sum(n_sc[:e + 1])]))
        for h, part in zip(hosts, parts):
            h.phase(0, part)
        for h, part in zip(hosts, parts):
            for p in range(1, h.ex.n_phases):
                h.phase(p, part)

    res = pl.pallas_call(
        body, name=name, in_specs=[ANY] * sum(n_in), out_specs=[ANY] * sum(n_out),
        out_shape=[s for h in hosts for s in h.out_shape], scratch_shapes=[s for h in hosts for s in h.scratch],
    )(*[a for h in hosts for a in h.args])
    return [res[sum(n_out[:e]):sum(n_out[:e + 1])] for e in range(len(hosts))]


def _ffn_fwd(x, g_pre, wgu, wd, g_post, target, name, exchange=None):
    t = x.shape[0]
    tm = _token_tile(t)
    n_i = t // tm
    with_loss = target is not None
    host = _Host(exchange)
    n_in, n_out = (6, 6) if with_loss else (5, 4)

    def body(*refs):
        own, ex_refs = host.split(refs, n_in, n_out, 0)
        if with_loss:
            x_ref, gpre_ref, wgu_ref, wd_ref, gpost_ref, tgt_ref, xo_ref, n_ref, df_ref, gu_ref, dgpost_ref, loss_ref = own
            _zero_at_first(pl.program_id(0) == 0, dgpost_ref)
        else:
            x_ref, gpre_ref, wgu_ref, wd_ref, gpost_ref, xo_ref, f_ref, n_ref, gu_ref = own
        host.at_steps(pl.program_id(0), n_i, ex_refs)
        x = x_ref[...]
        n = _rms_fwd(x, gpre_ref[...]).astype(BF16)
        n_ref[...] = n
        f = None
        for j in range(N_CHUNK):
            gate = _dot_nt(n, wgu_ref[0, j])
            up = _dot_nt(n, wgu_ref[1, j])
            gu_ref[0, j] = gate.astype(BF16)
            gu_ref[1, j] = up.astype(BF16)
            part = _dot((gate * _sigmoid(gate) * up).astype(BF16), wd_ref[j])
            f = part if f is None else f + part
        xo = x + 0.5 * _rms_fwd(f, gpost_ref[...])
        if with_loss:
            err = xo - tgt_ref[...]
            d_out = err * (1.0 / D_MODEL)
            xo_ref[...] = d_out
            df, dg = _rms_bwd(f, gpost_ref[...], 0.5 * d_out)
            df_ref[...] = df.astype(BF16)
            dgpost_ref[...] += dg
            part = 0.5 * jnp.sum(jnp.sum(err * err, axis=-1, keepdims=True) * (1.0 / D_MODEL), axis=0, keepdims=True)
            loss_ref[...] = jnp.broadcast_to(part, loss_ref.shape)
        else:
            f_ref[...] = f
            xo_ref[...] = xo

    tok = pl.BlockSpec((tm, D_MODEL), lambda i: (i, 0))
    vec = pl.BlockSpec((1, D_MODEL), lambda i: (0, 0))
    act = pl.BlockSpec((2, N_CHUNK, tm, CHUNK), lambda i: (0, 0, i, 0))
    tok_f32 = jax.ShapeDtypeStruct((t, D_MODEL), F32)
    tok_bf16 = jax.ShapeDtypeStruct((t, D_MODEL), BF16)
    act_shape = jax.ShapeDtypeStruct((2, N_CHUNK, t, CHUNK), BF16)
    in_specs = [tok, vec,
                pl.BlockSpec((2, N_CHUNK, CHUNK, D_MODEL), lambda i: (0, 0, 0, 0), pipeline_mode=pl.Buffered(1)),
                pl.BlockSpec((N_CHUNK, CHUNK, D_MODEL), lambda i: (0, 0, 0), pipeline_mode=pl.Buffered(1)),
                vec]
    args = [x, g_pre, wgu, wd, g_post]
    if with_loss:
        in_specs.append(tok)
        args.append(target)
        out_shape = [tok_f32, tok_bf16, tok_bf16, act_shape, jax.ShapeDtypeStruct((1, D_MODEL), F32),
                     jax.ShapeDtypeStruct((n_i * 8, 128), F32)]
        out_specs = [tok, tok, tok, act, vec, pl.BlockSpec((8, 128), lambda i: (i, 0))]
    else:
        out_shape = [tok_f32, tok_f32, tok_bf16, act_shape]
        out_specs = [tok, tok, tok, act]
    res = pl.pallas_call(
        body, name=name, grid=(n_i,), in_specs=in_specs + host.in_specs, out_specs=out_specs + host.out_specs,
        out_shape=out_shape + host.out_shape, scratch_shapes=host.scratch, compiler_params=_params(1),
    )(*args, *host.args)
    return (*res[:n_out], list(res[n_out:]))


def _ffn_bwd_w(n, df, gu, wd, name, exchange=None):
    t = n.shape[0]
    tm = _ffn_bwd_tile(t)
    n_i = t // tm
    host = _Host(exchange)

    def body(*refs):
        (n_ref, df_ref, gu_ref, wd_ref, dgu_ref, dwgu_ref, dwd_ref), ex_refs = host.split(refs, 4, 3, 0)
        i = pl.program_id(1)
        host.at_steps(pl.program_id(0) * n_i + i, N_CHUNK * n_i, ex_refs)
        _zero_at_first(i == 0, dwgu_ref, dwd_ref)
        nb = n_ref[...]
        dfb = df_ref[...]
        gate = gu_ref[0, 0].astype(F32)
        up = gu_ref[1, 0].astype(F32)
        s = _sigmoid(gate)
        silu = gate * s
        a = (silu * up).astype(BF16)
        da = _dot_nt(dfb, wd_ref[0])
        dup = (da * silu).astype(BF16)
        dgate = (da * up * (s * (1.0 + gate * (1.0 - s)))).astype(BF16)
        dgu_ref[0, 0] = dgate
        dgu_ref[1, 0] = dup
        dwgu_ref[0, 0] += _dot_tn(nb, dgate)
        dwgu_ref[1, 0] += _dot_tn(nb, dup)
        dwd_ref[0] += _dot_tn(a, dfb)

    tok = pl.BlockSpec((tm, D_MODEL), lambda j, i: (i, 0))
    act = pl.BlockSpec((2, 1, tm, CHUNK), lambda j, i: (0, j, i, 0))
    wgu_spec = pl.BlockSpec((2, 1, D_MODEL, CHUNK), lambda j, i: (0, j, 0, 0))
    wd_spec = pl.BlockSpec((1, CHUNK, D_MODEL), lambda j, i: (j, 0, 0))
    res = pl.pallas_call(
        body, name=name, grid=(N_CHUNK, n_i),
        in_specs=[tok, tok, act, wd_spec] + host.in_specs,
        out_specs=[act, wgu_spec, wd_spec] + host.out_specs,
        out_shape=[jax.ShapeDtypeStruct((2, N_CHUNK, t, CHUNK), BF16),
                   jax.ShapeDtypeStruct((2, N_CHUNK, D_MODEL, CHUNK), F32),
                   jax.ShapeDtypeStruct((N_CHUNK, CHUNK, D_MODEL), F32)] + host.out_shape,
        scratch_shapes=host.scratch, compiler_params=_params(2),
    )(n, df, gu, wd, *host.args)
    return (*res[:3], list(res[3:]))


def _ffn_bwd_x(dgu, wgu, x, g_pre, d_out, name, exchange=None):
    t = x.shape[0]
    tm = _token_tile(t)
    n_i = t // tm
    host = _Host(exchange)

    def body(*refs):
        (dgu_ref, wgu_ref, x_ref, gpre_ref, do_ref, dx_ref, dgpre_ref), ex_refs = host.split(refs, 5, 2, 0)
        i = pl.program_id(0)
        host.at_steps(i, n_i, ex_refs)
        _zero_at_first(i == 0, dgpre_ref)
        dn = _dot(dgu_ref[0, 0], wgu_ref[0, 0]) + _dot(dgu_ref[1, 0], wgu_ref[1, 0])
        for j in range(1, N_CHUNK):
            dn = dn + _dot(dgu_ref[0, j], wgu_ref[0, j]) + _dot(dgu_ref[1, j], wgu_ref[1, j])
        dx, dg = _rms_bwd(x_ref[...], gpre_ref[...], dn)
        dx_ref[...] = do_ref[...] + dx
        dgpre_ref[...] += dg

    tok = pl.BlockSpec((tm, D_MODEL), lambda i: (i, 0))
    vec = pl.BlockSpec((1, D_MODEL), lambda i: (0, 0))
    res = pl.pallas_call(
        body, name=name, grid=(n_i,),
        in_specs=[pl.BlockSpec((2, N_CHUNK, tm, CHUNK), lambda i: (0, 0, i, 0)),
                  pl.BlockSpec((2, N_CHUNK, CHUNK, D_MODEL), lambda i: (0, 0, 0, 0), pipeline_mode=pl.Buffered(1)),
                  tok, vec, tok] + host.in_specs,
        out_specs=[tok, vec] + host.out_specs,
        out_shape=[jax.ShapeDtypeStruct((t, D_MODEL), F32), jax.ShapeDtypeStruct((1, D_MODEL), F32)] + host.out_shape,
        scratch_shapes=host.scratch, compiler_params=_params(1),
    )(dgu, wgu, x, g_pre, d_out, *host.args)
    return (*res[:2], list(res[2:]))


def _mix_in_fwd(x1, g, w_in):
    t = x1.shape[0]
    tm = _token_tile(t)

    def body(x_ref, g_ref, w_ref, xl_ref, gl_ref, q_ref, kv_ref):
        n = _rms_fwd(x_ref[...], g_ref[...]).astype(BF16)
        proj = _dot_nt(n, w_ref[...])
        xl_ref[...] = proj[:, 0:512]
        gl_ref[...] = proj[:, 512:1024]
        q_ref[...] = proj[:, 1024:1536].astype(BF16)
        kv_ref[...] = proj[:, 1536:1792].astype(BF16)

    tok = pl.BlockSpec((tm, D_MODEL), lambda i: (i, 0))
    half = pl.BlockSpec((tm, 512), lambda i: (i, 0))
    return pl.pallas_call(
        body, name="mix_in_fwd", grid=(t // tm,),
        in_specs=[tok, pl.BlockSpec((1, D_MODEL), lambda i: (0, 0)), pl.BlockSpec((D_IN, D_MODEL), lambda i: (0, 0))],
        out_specs=[half, half, half, pl.BlockSpec((tm, 256), lambda i: (i, 0))],
        out_shape=[jax.ShapeDtypeStruct((t, 512), F32), jax.ShapeDtypeStruct((t, 512), F32),
                   jax.ShapeDtypeStruct((t, 512), BF16), jax.ShapeDtypeStruct((t, 256), BF16)],
        compiler_params=_params(1),
    )(x1, g, w_in)


def _shift_down(x, before, s):
    if s == 0:
        return x
    rolled = pltpu.roll(x, s, 0)
    ext = jnp.concatenate([before, x[0:8]], axis=0)
    first8 = pltpu.roll(ext, s, 0)[8:16]
    return jnp.concatenate([first8, rolled[8:]], axis=0)


def _shift_up(x, after, s):
    if s == 0:
        return x
    rows = x.shape[0]
    rolled = pltpu.roll(x, rows - s, 0)
    ext = jnp.concatenate([x[rows - 8:rows], after], axis=0)
    last8 = pltpu.roll(ext, 16 - s, 0)[0:8]
    return jnp.concatenate([rolled[:rows - 8], last8], axis=0)


def _log_sigmoid(x):
    e = jnp.exp(-jnp.abs(x))
    log1p_e = jnp.where(e < 0.01, e * (1.0 - e * (0.5 - e * (1.0 / 3.0))), jnp.log(1.0 + e))
    return jnp.minimum(x, 0.0) - log1p_e


def _lru_gates(xc, p_ref, wrg, wig):
    xcb = xc.astype(BF16)
    r = _sigmoid(_dot(xcb, wrg) + p_ref[1:2, :])
    ig = _sigmoid(_dot(xcb, wig) + p_ref[2:3, :])
    ls = _log_sigmoid(p_ref[3:4, :])
    log_a = LRU_C * r * ls
    a = jnp.exp(log_a)
    mult = jnp.sqrt(-jnp.tanh(log_a) * (a * a + 1.0))
    return xcb, r, ig, ls, a, mult


def _conv_taps(x, before, p_ref):
    xc = x * p_ref[7:8, :]
    for s in (1, 2, 3):
        xc = xc + _shift_down(x, before, s) * p_ref[7 - s:8 - s, :]
    return xc + p_ref[0:1, :]


def _group_view(ref, gi):
    if len(ref.shape) == 2:
        return ref.at[:, pl.ds(LRU_GROUP * gi, LRU_GROUP)]
    return ref.at[pl.ds(gi, 1)]


def _lru_block_rows(t):
    return 512 if t >= 1024 else t // 2


def _lru_fwd(xl, p, wrg2, wig2):
    t = xl.shape[0]
    tb = _lru_block_rows(t)

    def body(*refs):
        x_tail, h_carry = refs[-2:]

        @pl.when(pl.program_id(1) == 0)
        def _():
            x_tail[...] = jnp.zeros_like(x_tail)
            h_carry[...] = jnp.zeros_like(h_carry)

        for gi in range(LRU_GROUPS_PER_STEP):
            group(*[_group_view(r, gi) for r in refs])

    def group(xl_ref, p_ref, wrg_ref, wig_ref, h_ref, x_tail, h_carry):
        x = xl_ref[...]
        xc = _conv_taps(x, x_tail[...], p_ref)
        x_tail[...] = x[tb - 8:tb]
        _, r, ig, ls, a, mult = _lru_gates(xc, p_ref, wrg_ref[0], wig_ref[0])
        u = mult * ig * xc
        row = lax.broadcasted_iota(jnp.int32, (tb, LRU_GROUP), 0)
        s = 1
        while s < tb:
            keep = row >= s
            u = jnp.where(keep, a * pltpu.roll(u, s, 0) + u, u)
            a = jnp.where(keep, a * pltpu.roll(a, s, 0), a)
            s *= 2
        h = u + a * h_carry[0:1, :]
        h_ref[...] = h
        h_carry[...] = jnp.broadcast_to(h[tb - 1:tb], h_carry.shape)

    wide = LRU_GROUPS_PER_STEP * LRU_GROUP
    blk = pl.BlockSpec((tb, wide), lambda g, tt: (tt, g))
    par = pl.BlockSpec((8, wide), lambda g, tt: (0, g))
    wsp = pl.BlockSpec((LRU_GROUPS_PER_STEP, LRU_GROUP, LRU_GROUP), lambda g, tt: (g, 0, 0))
    return pl.pallas_call(
        body, name="lru_fwd", grid=(N_LRU_GROUP // LRU_GROUPS_PER_STEP, t // tb), in_specs=[blk, par, wsp, wsp],
        out_specs=blk, out_shape=jax.ShapeDtypeStruct((t, D_LRU), F32),
        scratch_shapes=[pltpu.VMEM((8, wide), F32), pltpu.VMEM((8, wide), F32)],
        compiler_params=_params(2),
    )(xl, p, wrg2, wig2)


def _lru_bwd(dy, h, xl, gl, p, wrg2, wig2):
    t = xl.shape[0]
    tb = _lru_block_rows(t)
    n_tb = t // tb
    tb8 = tb // 8

    def body(*refs):
        _zero_at_first(pl.program_id(1) == 0, *refs[11:])
        for gi in range(LRU_GROUPS_PER_STEP):
            group(*[_group_view(r, gi) for r in refs])

    def group(dy_ref, h_ref, hprev_ref, xl_ref, xprev_ref, gl_ref, p_ref, wrg_ref, wig_ref,
              dxl_ref, dgl_ref, dp_ref, dwrg_ref, dwig_ref, g_carry, a_carry, dxc_head):
        tt = n_tb - 1 - pl.program_id(1)
        has_prev = (tt > 0).astype(F32)
        x = xl_ref[...]
        x_before = xprev_ref[...] * has_prev
        xs = [_shift_down(x, x_before, s) for s in range(4)]
        xc = xs[0] * p_ref[7:8, :] + xs[1] * p_ref[6:7, :] + xs[2] * p_ref[5:6, :] + xs[3] * p_ref[4:5, :] + p_ref[0:1, :]
        wrg = wrg_ref[0]
        wig = wig_ref[0]
        xcb, r, ig, ls, a, mult = _lru_gates(xc, p_ref, wrg, wig)

        hh = h_ref[...]
        h_m1 = _shift_down(hh, hprev_ref[...] * has_prev, 1)
        ge, dge = _gelu(gl_ref[...])
        dy = dy_ref[...]
        dgl_ref[...] = dy * hh * dge
        dh = dy * ge

        b = _shift_up(a, a_carry[...], 1)
        row = lax.broadcasted_iota(jnp.int32, (tb, LRU_GROUP), 0)
        g = dh
        s = 1
        while s < tb:
            keep = row < tb - s
            g = jnp.where(keep, b * pltpu.roll(g, tb - s, 0) + g, g)
            b = jnp.where(keep, b * pltpu.roll(b, tb - s, 0), b)
            s *= 2
        g = g + b * g_carry[0:1, :]
        g_carry[...] = jnp.broadcast_to(g[0:1], g_carry.shape)
        a_carry[...] = jnp.broadcast_to(a[0:1], a_carry.shape)

        da = g * h_m1
        dmult = g * ig * xc
        dig = g * mult * xc
        dxc = g * mult * ig
        dlog_a = da * a - dmult * (a * a) / mult
        dr = dlog_a * (LRU_C * ls)
        dls = jnp.sum(dlog_a * (LRU_C * r), axis=0, keepdims=True)
        dlam = dls * _sigmoid(-p_ref[3:4, :])
        dpre_r = dr * r * (1.0 - r)
        dpre_i = dig * ig * (1.0 - ig)
        dprb = dpre_r.astype(BF16)
        dpib = dpre_i.astype(BF16)
        dxc = dxc + _dot_nt(dprb, wrg) + _dot_nt(dpib, wig)
        dwrg_ref[0] += _dot_tn(xcb, dprb)
        dwig_ref[0] += _dot_tn(xcb, dpib)

        after = dxc_head[...]
        dxl = dxc * p_ref[7:8, :]
        for s in (1, 2, 3):
            dxl = dxl + _shift_up(dxc, after, s) * p_ref[7 - s:8 - s, :]
        dxl_ref[...] = dxl
        dxc_head[...] = dxc[0:8]

        rows = [jnp.sum(dxc, axis=0, keepdims=True), jnp.sum(dpre_r, axis=0, keepdims=True),
                jnp.sum(dpre_i, axis=0, keepdims=True), dlam]
        rows += [jnp.sum(dxc * xs[3 - k], axis=0, keepdims=True) for k in range(4)]
        dp_ref[...] += jnp.concatenate(rows, axis=0)

    wide = LRU_GROUPS_PER_STEP * LRU_GROUP
    blk = pl.BlockSpec((tb, wide), lambda g, s: (n_tb - 1 - s, g))
    prev8 = pl.BlockSpec((8, wide), lambda g, s: (jnp.maximum((n_tb - 1 - s) * tb8 - 1, 0), g))
    par = pl.BlockSpec((8, wide), lambda g, s: (0, g))
    wsp = pl.BlockSpec((LRU_GROUPS_PER_STEP, LRU_GROUP, LRU_GROUP), lambda g, s: (g, 0, 0))
    return pl.pallas_call(
        body, name="lru_bwd", grid=(N_LRU_GROUP // LRU_GROUPS_PER_STEP, n_tb),
        in_specs=[blk, blk, prev8, blk, prev8, blk, par, wsp, wsp], out_specs=[blk, blk, par, wsp, wsp],
        out_shape=[jax.ShapeDtypeStruct((t, D_LRU), F32), jax.ShapeDtypeStruct((t, D_LRU), F32),
                   jax.ShapeDtypeStruct((8, D_LRU), F32),
                   jax.ShapeDtypeStruct((N_LRU_GROUP, LRU_GROUP, LRU_GROUP), F32),
                   jax.ShapeDtypeStruct((N_LRU_GROUP, LRU_GROUP, LRU_GROUP), F32)],
        scratch_shapes=[pltpu.VMEM((8, wide), F32)] * 3,
        compiler_params=_params(2),
    )(dy, h, h, xl, xl, gl, p, wrg2, wig2)


def _attn_bias(first_block):
    qi = jnp.bitwise_and(lax.broadcasted_iota(jnp.int32, (4 * BLOCK_Q, 2 * BLOCK_Q), 0), BLOCK_Q - 1)
    kj = lax.broadcasted_iota(jnp.int32, (4 * BLOCK_Q, 2 * BLOCK_Q), 1)
    rel = qi + BLOCK_Q - kj
    mask = (rel >= 0) & (rel < BLOCK_Q)
    if first_block:
        mask = mask & (kj >= BLOCK_Q)
    return jnp.where(mask, 0.0, MASK_VALUE)


def _sink_column(sinks):
    hrow = lax.broadcasted_iota(jnp.int32, (4 * BLOCK_Q, 1), 0)
    return jnp.where(hrow < BLOCK_Q, sinks[0],
                     jnp.where(hrow < 2 * BLOCK_Q, sinks[1], jnp.where(hrow < 3 * BLOCK_Q, sinks[2], sinks[3])))


def _attn_scores(qv, kvv, n, bias, sk, lo):
    r0 = pl.multiple_of(n * BLOCK_Q, BLOCK_Q)
    rp = pl.multiple_of(jnp.maximum(n - 1, 0) * BLOCK_Q, BLOCK_Q)
    kvb = jnp.concatenate([kvv[pl.ds(rp, BLOCK_Q), :], kvv[pl.ds(r0, BLOCK_Q), :]], axis=0)
    k2 = kvb[:, 0:128]
    v2 = kvb[:, 128:256]
    qs = _stack_heads(qv[pl.ds(r0, BLOCK_Q), :], lo)
    s = _dot_nt(qs, k2) * ATTN_SCALE + bias
    m = jnp.maximum(jnp.max(s, axis=-1, keepdims=True), sk)
    e = jnp.exp(s - m)
    es = jnp.exp(sk - m)
    inv = 1.0 / (jnp.sum(e, axis=-1, keepdims=True) + es)
    return r0, rp, qs, k2, v2, e * inv, es * inv


def _stack_heads(pair2, lo):
    p0 = pair2[:, 0:128]
    p1 = pair2[:, 128:256]
    z = jnp.zeros_like(p0)
    return jnp.concatenate([jnp.where(lo, p0, z), jnp.where(lo, z, p0), jnp.where(lo, p1, z), jnp.where(lo, z, p1)], axis=0)


def _unstack_heads(st, lo):
    b = BLOCK_Q
    return jnp.concatenate([jnp.where(lo, st[0:b], st[b:2 * b]), jnp.where(lo, st[2 * b:3 * b], st[3 * b:4 * b])], axis=1)


def _attn_fwd(q, kv, sinks):
    t = q.shape[0]
    n_blk = t // BLOCK_Q

    def body(q_hbm, kv_hbm, s_ref, o_hbm, q2, kvv, o2, bias0, bias, sem):
        lo = lax.broadcasted_iota(jnp.int32, (BLOCK_Q, 128), 1) < HEAD_DIM
        cols = [pl.ds(256 * g, 256) for g in range(2)]
        loads = [pltpu.make_async_copy(kv_hbm, kvv, sem.at[0])]
        loads += [pltpu.make_async_copy(q_hbm.at[:, cols[g]], q2.at[g], sem.at[1 + g]) for g in range(2)]
        stores = [pltpu.make_async_copy(o2.at[g], o_hbm.at[:, cols[g]], sem.at[3 + g]) for g in range(2)]
        for cp in loads:
            cp.start()
        bias0[...] = _attn_bias(True)
        bias[...] = _attn_bias(False)
        loads[0].wait()
        for g in range(2):
            loads[1 + g].wait()
            qv, ov = q2.at[g], o2.at[g]
            sk = _sink_column([s_ref[0, HEAD_ORDER[4 * g + i]] for i in range(4)])

            def block(n, bias_ref):
                r0, _, _, _, v2, prob, _ = _attn_scores(qv, kvv, n, bias_ref[...], sk, lo)
                ov[pl.ds(r0, BLOCK_Q), :] = _unstack_heads(_dot(prob.astype(BF16), v2), lo)

            block(0, bias0)

            def later(n, carry):
                block(n, bias)
                return carry

            lax.fori_loop(1, n_blk, later, 0, unroll=ATTN_UNROLL)
            stores[g].start()
        for cp in stores:
            cp.wait()

    return pl.pallas_call(
        body, name="attn_fwd", in_specs=[ANY, ANY, SMEM], out_specs=ANY,
        out_shape=jax.ShapeDtypeStruct((t, D_ATTN), F32),
        scratch_shapes=[pltpu.VMEM((2, t, 256), BF16), pltpu.VMEM((t, 256), BF16), pltpu.VMEM((2, t, 256), F32),
                        pltpu.VMEM((4 * BLOCK_Q, 2 * BLOCK_Q), F32), pltpu.VMEM((4 * BLOCK_Q, 2 * BLOCK_Q), F32),
                        pltpu.SemaphoreType.DMA((5,))],
        compiler_params=_params(),
    )(q, kv, sinks)


def _attn_bwd(q, kv, do, sinks, exchange=None):
    t = q.shape[0]
    n_blk = t // BLOCK_Q
    host = _Host(exchange)

    def body(*refs):
        own, ex_refs = host.split(refs, 4, 3, 9)
        q_hbm, kv_hbm, do_hbm, s_ref, dq_hbm, dkv_hbm, dsink_ref, q2, kvv, do2, dqv, dkvv, ds_acc, bias0, bias, sem = own
        host.phase(0, ex_refs)
        lo = lax.broadcasted_iota(jnp.int32, (BLOCK_Q, 128), 1) < HEAD_DIM
        loads = [pltpu.make_async_copy(kv_hbm, kvv, sem.at[0])]
        for g in range(2):
            loads += [pltpu.make_async_copy(src.at[:, pl.ds(256 * g, 256)], dst.at[g], sem.at[1 + 2 * g + i])
                      for i, (src, dst) in enumerate(((q_hbm, q2), (do_hbm, do2)))]
        for cp in loads:
            cp.start()
        bias0[...] = _attn_bias(True)
        bias[...] = _attn_bias(False)
        loads[0].wait()
        for g in range(2):
            cols = pl.ds(256 * g, 256)
            for cp in loads[1 + 2 * g:3 + 2 * g]:
                cp.wait()
            qv, dov = q2.at[g], do2.at[g]
            heads = [HEAD_ORDER[4 * g + i] for i in range(4)]
            sk = _sink_column([s_ref[0, h] for h in heads])
            ds_acc[...] = jnp.zeros_like(ds_acc)

            def block(n, bias_ref, has_prev):
                r0, rp, qs, k2, v2, prob, psink = _attn_scores(qv, kvv, n, bias_ref[...], sk, lo)
                pb = prob.astype(BF16)
                dos = _stack_heads(dov[pl.ds(r0, BLOCK_Q), :], lo)
                dp = _dot_nt(dos, v2)
                dsum = jnp.sum(prob * dp, axis=-1, keepdims=True)
                dsb = (prob * (dp - dsum) * ATTN_SCALE).astype(BF16)
                ds_acc[...] -= psink * dsum
                dqv[pl.ds(r0, BLOCK_Q), :] = _unstack_heads(_dot(dsb, k2), lo).astype(BF16)
                dk2 = _dot_tn(dsb, qs)
                dv2 = _dot_tn(pb, dos)
                cur = jnp.concatenate([dk2[BLOCK_Q:], dv2[BLOCK_Q:]], axis=1)
                if g == 0:
                    dkvv[pl.ds(r0, BLOCK_Q), :] = cur
                else:
                    dkvv[pl.ds(r0, BLOCK_Q), :] += cur
                if has_prev:
                    dkvv[pl.ds(rp, BLOCK_Q), :] += jnp.concatenate([dk2[:BLOCK_Q], dv2[:BLOCK_Q]], axis=1)

            block(0, bias0, False)

            def later(n, carry):
                block(n, bias, True)
                return carry

            lax.fori_loop(1, n_blk, later, 0, unroll=ATTN_UNROLL)
            for i, h in enumerate(heads):
                tot = jnp.sum(ds_acc[BLOCK_Q * i:BLOCK_Q * (i + 1), :], axis=0, keepdims=True)
                dsink_ref[h:h + 1, :] = jnp.broadcast_to(tot, (1, 128))
            store = pltpu.make_async_copy(dqv, dq_hbm.at[:, cols], sem.at[5])
            store.start()
            store.wait()
        store = pltpu.make_async_copy(dkvv, dkv_hbm, sem.at[6])
        store.start()
        store.wait()
        if exchange is not None:
            for p in range(1, exchange.n_phases):
                host.phase(p, ex_refs)

    res = pl.pallas_call(
        body, name="attn_bwd", in_specs=[ANY, ANY, ANY, SMEM] + host.in_specs,
        out_specs=[ANY, ANY, pl.BlockSpec(memory_space=pltpu.VMEM)] + host.out_specs,
        out_shape=[jax.ShapeDtypeStruct((t, D_ATTN), BF16), jax.ShapeDtypeStruct((t, 256), F32),
                   jax.ShapeDtypeStruct((8, 128), F32)] + host.out_shape,
        scratch_shapes=[pltpu.VMEM((2, t, 256), BF16), pltpu.VMEM((t, 256), BF16), pltpu.VMEM((2, t, 256), BF16),
                        pltpu.VMEM((t, 256), BF16), pltpu.VMEM((t, 256), F32), pltpu.VMEM((4 * BLOCK_Q, 1), F32),
                        pltpu.VMEM((4 * BLOCK_Q, 2 * BLOCK_Q), F32), pltpu.VMEM((4 * BLOCK_Q, 2 * BLOCK_Q), F32),
                        pltpu.SemaphoreType.DMA((7,))] + host.scratch,
        compiler_params=_params(),
    )(q, kv, do, sinks, *host.args)
    return (*res[:3], list(res[3:]))


def _mix_out_fwd(x1, h, gl, o, g_lru, g_attn, g_post, w_o):
    t = x1.shape[0]
    tm = _token_tile(t)

    def body(x_ref, h_ref, gl_ref, o_ref, g1_ref, g2_ref, gp_ref, w_ref, x2_ref, m_ref):
        y = h_ref[...] * _gelu(gl_ref[...])[0]
        yn1 = _rms_fwd(y, g1_ref[...]).astype(BF16)
        yn2 = _rms_fwd(o_ref[...], g2_ref[...]).astype(BF16)
        m = _dot(yn1, w_ref[0:512, :]) + _dot(yn2, w_ref[512:1024, :])
        m_ref[...] = m
        x2_ref[...] = x_ref[...] + _rms_fwd(m, gp_ref[...])

    tok = pl.BlockSpec((tm, D_MODEL), lambda i: (i, 0))
    half = pl.BlockSpec((tm, 512), lambda i: (i, 0))
    vec = pl.BlockSpec((1, D_MODEL), lambda i: (0, 0))
    hvec = pl.BlockSpec((1, 512), lambda i: (0, 0))
    return pl.pallas_call(
        body, name="mix_out_fwd", grid=(t // tm,),
        in_specs=[tok, half, half, half, hvec, hvec, vec, pl.BlockSpec((D_MODEL, D_MODEL), lambda i: (0, 0))],
        out_specs=[tok, tok],
        out_shape=[jax.ShapeDtypeStruct((t, D_MODEL), F32), jax.ShapeDtypeStruct((t, D_MODEL), F32)],
        compiler_params=_params(1),
    )(x1, h, gl, o, g_lru, g_attn, g_post, w_o)


def _mix_out_bwd(dx2, m, h, gl, o, g_lru, g_attn, g_post, w_o):
    t = dx2.shape[0]
    tm = _token_tile(t)

    def body(dx_ref, m_ref, h_ref, gl_ref, o_ref, g1_ref, g2_ref, gp_ref, w_ref,
             dy_ref, do_ref, dw_ref, dgp_ref, dg1_ref, dg2_ref):
        _zero_at_first(pl.program_id(0) == 0, dw_ref, dgp_ref, dg1_ref, dg2_ref)
        dm, dgp = _rms_bwd(m_ref[...], gp_ref[...], dx_ref[...])
        dmb = dm.astype(BF16)
        y = h_ref[...] * _gelu(gl_ref[...])[0]
        o = o_ref[...]
        yn1 = _rms_fwd(y, g1_ref[...]).astype(BF16)
        yn2 = _rms_fwd(o, g2_ref[...]).astype(BF16)
        dw_ref[0:512, :] += _dot_tn(yn1, dmb)
        dw_ref[512:1024, :] += _dot_tn(yn2, dmb)
        dy, dg1 = _rms_bwd(y, g1_ref[...], _dot_nt(dmb, w_ref[0:512, :]))
        do, dg2 = _rms_bwd(o, g2_ref[...], _dot_nt(dmb, w_ref[512:1024, :]))
        dy_ref[...] = dy
        do_ref[...] = do.astype(BF16)
        dgp_ref[...] += dgp
        dg1_ref[...] += dg1
        dg2_ref[...] += dg2

    tok = pl.BlockSpec((tm, D_MODEL), lambda i: (i, 0))
    half = pl.BlockSpec((tm, 512), lambda i: (i, 0))
    vec = pl.BlockSpec((1, D_MODEL), lambda i: (0, 0))
    hvec = pl.BlockSpec((1, 512), lambda i: (0, 0))
    mat = pl.BlockSpec((D_MODEL, D_MODEL), lambda i: (0, 0))
    return pl.pallas_call(
        body, name="mix_out_bwd", grid=(t // tm,),
        in_specs=[tok, tok, half, half, half, hvec, hvec, vec, mat],
        out_specs=[half, half, mat, vec, hvec, hvec],
        out_shape=[jax.ShapeDtypeStruct((t, 512), F32), jax.ShapeDtypeStruct((t, 512), BF16),
                   jax.ShapeDtypeStruct((D_MODEL, D_MODEL), F32), jax.ShapeDtypeStruct((1, D_MODEL), F32),
                   jax.ShapeDtypeStruct((1, 512), F32), jax.ShapeDtypeStruct((1, 512), F32)],
        compiler_params=_params(1),
    )(dx2, m, h, gl, o, g_lru, g_attn, g_post, w_o)


def _mix_in_bwd(dx2, x1, g, dxl, dgl, dq, dkv, w_in, f1, g_post1):
    t = x1.shape[0]
    tm = _token_tile(t)

    def body(dx2_ref, x_ref, g_ref, dxl_ref, dgl_ref, dq_ref, dkv_ref, w_ref, f1_ref, gp1_ref,
             dx1_ref, dw_ref, dg_ref, df1_ref, dgp1_ref):
        _zero_at_first(pl.program_id(0) == 0, dw_ref, dg_ref, dgp1_ref)
        x = x_ref[...]
        nb = _rms_fwd(x, g_ref[...]).astype(BF16)
        dproj = jnp.concatenate([dxl_ref[...].astype(BF16), dgl_ref[...].astype(BF16), dq_ref[...],
                                 dkv_ref[...].astype(BF16)], axis=1)
        dw_ref[...] += _dot_tn(nb, dproj)
        dx, dg = _rms_bwd(x, g_ref[...], _dot(dproj, w_ref[...]))
        dx1 = dx2_ref[...] + dx
        dx1_ref[...] = dx1
        dg_ref[...] += dg
        df1, dgp1 = _rms_bwd(f1_ref[...], gp1_ref[...], 0.5 * dx1)
        df1_ref[...] = df1.astype(BF16)
        dgp1_ref[...] += dgp1

    tok = pl.BlockSpec((tm, D_MODEL), lambda i: (i, 0))
    half = pl.BlockSpec((tm, 512), lambda i: (i, 0))
    vec = pl.BlockSpec((1, D_MODEL), lambda i: (0, 0))
    mat = pl.BlockSpec((D_IN, D_MODEL), lambda i: (0, 0))
    dmat = pl.BlockSpec((D_MODEL, D_IN), lambda i: (0, 0))
    quarter = pl.BlockSpec((tm, 256), lambda i: (i, 0))
    return pl.pallas_call(
        body, name="mix_in_bwd", grid=(t // tm,),
        in_specs=[tok, tok, vec, half, half, half, quarter, mat, tok, vec], out_specs=[tok, dmat, vec, tok, vec],
        out_shape=[jax.ShapeDtypeStruct((t, D_MODEL), F32), jax.ShapeDtypeStruct((D_MODEL, D_IN), F32),
                   jax.ShapeDtypeStruct((1, D_MODEL), F32), jax.ShapeDtypeStruct((t, D_MODEL), BF16),
                   jax.ShapeDtypeStruct((1, D_MODEL), F32)],
        compiler_params=_params(1),
    )(dx2, x1, g, dxl, dgl, dq, dkv, w_in, f1, g_post1)


def _half(rows):
    return rows // 2


def _chip_sums(grads, from_sibling, other, name):
    n_arr = len(grads)

    def body(other_ref, *refs):
        for a in range(n_arr):
            refs[2 * n_arr + a][0] = (refs[2 * a][0, 0] + refs[2 * a + 1][0]).astype(BF16)

    in_specs, out_specs, out_shape, args = [], [], [], []
    for g, s in zip(grads, from_sibling):
        _, rows, cols = g.shape
        tr = _half(rows)
        in_specs += [pl.BlockSpec((1, 1, tr, cols), lambda j, i, other: (other[j], other[3], i, 0)),
                     pl.BlockSpec((1, tr, cols), lambda j, i, other: (other[j], i, 0))]
        out_specs.append(pl.BlockSpec((1, tr, cols), lambda j, i, other: (j, i, 0)))
        out_shape.append(jax.ShapeDtypeStruct((3, rows, cols), BF16))
        args += [g.reshape(4, 2, rows, cols), s]
    grid_spec = pltpu.PrefetchScalarGridSpec(num_scalar_prefetch=1, grid=(3, 2), in_specs=in_specs, out_specs=out_specs)
    return pl.pallas_call(body, name=name, grid_spec=grid_spec, out_shape=out_shape, compiler_params=_params(2))(other, *args)


def _adamw(w, g, m, v):
    m = ADAM_B1 * m + (1.0 - ADAM_B1) * g
    v = ADAM_B2 * v + (1.0 - ADAM_B2) * (g * g)
    m_hat = m / (1.0 - ADAM_B1 ** ADAM_STEP)
    v_hat = v / (1.0 - ADAM_B2 ** ADAM_STEP)
    delta = -ADAM_LR * (m_hat / (jnp.sqrt(v_hat) + ADAM_EPS) + ADAM_WD * w)
    return delta, m, v


def _shard_updates(grads, from_sibling, from_chips, w, m, v, place, name, transposed):
    n_arr = len(grads)

    def total(g_ref, s_ref, c_ref):
        g = g_ref[0, 0] + s_ref[0]
        g = g + c_ref[0].astype(F32)
        g = g + c_ref[1].astype(F32)
        return g + c_ref[2].astype(F32)

    part_specs, parts, flat, shapes = [], [], [], []
    for g in grads:
        _, rows, cols = g.shape
        tr = _half(rows)
        part_specs.append([pl.BlockSpec((1, 1, tr, cols), lambda i, place: (place[0], place[1], i, 0)),
                           pl.BlockSpec((1, tr, cols), lambda i, place: (place[0], i, 0)),
                           pl.BlockSpec((3, tr, cols), lambda i, place: (0, i, 0))])
        flat.append(pl.BlockSpec((tr, cols), lambda i, place: (i, 0)))
        shapes.append(jax.ShapeDtypeStruct((rows, cols), F32))
    for g, s, c in zip(grads, from_sibling, from_chips):
        parts += [g.reshape(4, 2, *g.shape[1:]), s, c]

    if not transposed:
        def body(place_ref, *refs):
            ins, wmv, outs = refs[:3 * n_arr], refs[3 * n_arr:6 * n_arr], refs[6 * n_arr:]
            for a in range(n_arr):
                g = total(*ins[3 * a:3 * a + 3])
                outs[4 * a][...] = g
                outs[4 * a + 1][...], outs[4 * a + 2][...], outs[4 * a + 3][...] = _adamw(
                    wmv[3 * a][...], g, wmv[3 * a + 1][...], wmv[3 * a + 2][...])

        grid_spec = pltpu.PrefetchScalarGridSpec(
            num_scalar_prefetch=1, grid=(2,),
            in_specs=[sp for specs in part_specs for sp in specs] + [f for f in flat for _ in range(3)],
            out_specs=[f for f in flat for _ in range(4)])
        res = pl.pallas_call(body, name=name, grid_spec=grid_spec, out_shape=[sh for sh in shapes for _ in range(4)],
                             compiler_params=_params(1))(place, *parts, *[x for wmv in zip(w, m, v) for x in wmv])
        return [tuple(res[4 * a:4 * a + 4]) for a in range(n_arr)]

    def sum_body(place_ref, *refs):
        for a in range(n_arr):
            refs[3 * n_arr + a][...] = total(*refs[3 * a:3 * a + 3])

    grid_spec = pltpu.PrefetchScalarGridSpec(num_scalar_prefetch=1, grid=(2,),
                                             in_specs=[sp for specs in part_specs for sp in specs], out_specs=flat)
    sums = pl.pallas_call(sum_body, name=name + "_sum", grid_spec=grid_spec, out_shape=shapes,
                          compiler_params=_params(1))(place, *parts)
    turned = [jnp.transpose(g, (1, 0)) for g in sums]

    def adam_body(*refs):
        ins, outs = refs[:4 * n_arr], refs[4 * n_arr:]
        for a in range(n_arr):
            g = ins[4 * a][...]
            outs[4 * a][...] = g
            outs[4 * a + 1][...], outs[4 * a + 2][...], outs[4 * a + 3][...] = _adamw(
                ins[4 * a + 1][...], g, ins[4 * a + 2][...], ins[4 * a + 3][...])

    blks = [pl.BlockSpec((g.shape[0] // 4, g.shape[1]), lambda i: (i, 0)) for g in turned]
    res = pl.pallas_call(
        adam_body, name=name + "_adam", grid=(4,), in_specs=[b for b in blks for _ in range(4)],
        out_specs=[b for b in blks for _ in range(4)],
        out_shape=[jax.ShapeDtypeStruct(g.shape, F32) for g in turned for _ in range(4)], compiler_params=_params(1),
    )(*[x for gwmv in zip(turned, w, m, v) for x in gwmv])
    return [tuple(res[4 * a:4 * a + 4]) for a in range(n_arr)]


GAINS = ("ffn1_pre_g", "ffn1_post_g", "mix_pre_g", "mix_post_g", "ffn2_pre_g", "ffn2_post_g")
HALVES = ("conv_b", "b_rg", "b_ig", "lru_lambda", "g_lru_out", "g_attn_out")
GATES = ("w_rg", "w_ig")
SMALL = GAINS + HALVES + GATES + ("sinks", "conv_w")


def _small_update(gathered, w, m, v):
    n_small = len(SMALL)

    def body(*refs):
        ga_ref, gb_ref, gc_ref, gd_ref, g0_ref, gconv_ref = refs[:6]
        wmv = refs[6:6 + 3 * n_small]
        outs = refs[6 + 3 * n_small:6 + 7 * n_small]
        loss_ref = refs[6 + 7 * n_small]

        def total(ref):
            s = ref[0]
            for d in range(1, N_DEV):
                s = s + ref[d]
            return s

        sa, sb, sc, sd = total(ga_ref), total(gb_ref), total(gc_ref), total(gd_ref)
        grads = {}
        for i, k in enumerate(GAINS):
            grads[k] = sa[i:i + 1]
        grads[GAINS[0]] = total(g0_ref)
        for i, k in enumerate(HALVES):
            grads[k] = sb[i:i + 1]
        grads["w_rg"], grads["w_ig"] = sc[0:512], sc[512:1024]
        grads["sinks"] = sd[4:5, 0:8]
        grads["conv_w"] = total(gconv_ref)
        for i, k in enumerate(SMALL):
            g = grads[k]
            outs[4 * i][...] = g
            outs[4 * i + 1][...], outs[4 * i + 2][...], outs[4 * i + 3][...] = _adamw(
                wmv[3 * i][...], g, wmv[3 * i + 1][...], wmv[3 * i + 2][...])
        loss_ref[...] = jnp.broadcast_to(sd[5:6, 0:128], loss_ref.shape)

    operands = list(gathered)
    out_shape = []
    for k in SMALL:
        operands += [w[k], m[k], v[k]]
        out_shape += [jax.ShapeDtypeStruct(w[k].shape, F32)] * 4
    out_shape.append(jax.ShapeDtypeStruct((8, 128), F32))
    res = pl.pallas_call(body, name="small_update", out_shape=out_shape, compiler_params=_params())(*operands)
    parts = [{k: res[4 * i + j] for i, k in enumerate(SMALL)} for j in range(4)]
    return (*parts, res[-1])


def _reorder_heads(a, axis, start, order):
    def slab(h):
        return lax.slice_in_dim(a, start + HEAD_DIM * h, start + HEAD_DIM * (h + 1), axis=axis)

    parts = [lax.slice_in_dim(a, 0, start, axis=axis)] + [slab(h) for h in order]
    parts.append(lax.slice_in_dim(a, start + 8 * HEAD_DIM, a.shape[axis], axis=axis))
    return jnp.concatenate(parts, axis=axis)


HEAD_ORDER_INVERSE = tuple(HEAD_ORDER.index(h) for h in range(8))


def _pair_block_diag(w):
    w = w.reshape(N_LRU_GROUP, 2, 64, 64)
    z = jnp.zeros((N_LRU_GROUP, 64, 64), w.dtype)
    top = jnp.concatenate([w[:, 0], z], axis=2)
    bot = jnp.concatenate([z, w[:, 1]], axis=2)
    return jnp.concatenate([top, bot], axis=1)


def _pair_block_diag_grad(dw2):
    return jnp.stack([dw2[:, :64, :64], dw2[:, 64:, 64:]], axis=1).reshape(512, 64)


def kernel(x, ffn1_pre_g, ffn1_w_gu, ffn1_w_down, ffn1_post_g, mix_pre_g, w_in, conv_w, conv_b, w_rg, b_rg, w_ig, b_ig, lru_lambda, sinks, g_lru_out, g_attn_out, w_o, mix_post_g, ffn2_pre_g, ffn2_w_gu, ffn2_w_down, ffn2_post_g, loss_target, m_ffn1_pre_g, m_ffn1_w_gu, m_ffn1_w_down, m_ffn1_post_g, m_mix_pre_g, m_w_in, m_conv_w, m_conv_b, m_w_rg, m_b_rg, m_w_ig, m_b_ig, m_lru_lambda, m_sinks, m_g_lru_out, m_g_attn_out, m_w_o, m_mix_post_g, m_ffn2_pre_g, m_ffn2_w_gu, m_ffn2_w_down, m_ffn2_post_g, v_ffn1_pre_g, v_ffn1_w_gu, v_ffn1_w_down, v_ffn1_post_g, v_mix_pre_g, v_w_in, v_conv_w, v_conv_b, v_w_rg, v_b_rg, v_w_ig, v_b_ig, v_lru_lambda, v_sinks, v_g_lru_out, v_g_attn_out, v_w_o, v_mix_post_g, v_ffn2_pre_g, v_ffn2_w_gu, v_ffn2_w_down, v_ffn2_post_g):
    args = dict(locals())
    names = ["ffn1_pre_g", "ffn1_w_gu", "ffn1_w_down", "ffn1_post_g", "mix_pre_g", "w_in", "conv_w", "conv_b", "w_rg",
             "b_rg", "w_ig", "b_ig", "lru_lambda", "sinks", "g_lru_out", "g_attn_out", "w_o", "mix_post_g",
             "ffn2_pre_g", "ffn2_w_gu", "ffn2_w_down", "ffn2_post_g"]
    big = ["ffn1_w_gu", "ffn1_w_down", "w_in", "w_o", "ffn2_w_gu", "ffn2_w_down"]
    w = {k: args[k] for k in names}
    mom = {k: args["m_" + k] for k in names}
    var = {k: args["v_" + k] for k in names}
    t = x.shape[1]
    xs = x.reshape(t, D_MODEL)
    target = loss_target.reshape(t, D_MODEL)
    cx, cy, cc = _coords()
    me = 4 * cx + 2 * cy + cc
    other = jnp.stack([2 * (1 - cx) + cy, 2 * cx + (1 - cy), 2 * (1 - cx) + (1 - cy), cc]).astype(jnp.int32)
    place = jnp.stack([2 * cx + cy, cc]).astype(jnp.int32)

    transposed = ("ffn1_w_gu", "w_in", "ffn2_w_gu")

    def shard_view(a, k):
        return jnp.transpose(a[0], (1, 0)) if k in transposed else a[0]

    def shard_unview(a, k):
        return (jnp.transpose(a, (1, 0)) if k in transposed else a)[None]

    shard2d = {k: shard_view(w[k], k) for k in big}
    shard_bf = {k: shard2d[k].astype(BF16) for k in big}
    conv_pad = jnp.pad(conv_w.reshape(4, 64), ((0, 4), (0, 64)))
    (first_w,) = _run_exchanges([_Gather([shard_bf["ffn1_w_gu"], shard_bf["ffn1_w_down"]], routed=True)], "all_gather_ffn1")
    wgu1 = first_w[0].reshape(2, N_CHUNK, CHUNK, D_MODEL)
    wd1 = first_w[1].reshape(N_CHUNK, CHUNK, D_MODEL)
    rest = _Gather([shard_bf["w_in"], shard_bf["w_o"], shard_bf["ffn2_w_gu"], shard_bf["ffn2_w_down"], conv_pad])

    x1, f1, n1, gu1, gathered = _ffn_fwd(xs, ffn1_pre_g, wgu1, wd1, ffn1_post_g, None, "ffn1_fwd", rest)
    w_in_full = _reorder_heads(gathered[0].reshape(D_IN, D_MODEL), 0, 2 * D_LRU, HEAD_ORDER)
    w_o_full = _reorder_heads(gathered[1].reshape(D_MODEL, D_MODEL), 0, D_LRU, HEAD_ORDER)
    g_attn_heads = _reorder_heads(g_attn_out, 1, 0, HEAD_ORDER)
    wgu2 = gathered[2].reshape(2, N_CHUNK, CHUNK, D_MODEL)
    wd2 = gathered[3].reshape(N_CHUNK, CHUNK, D_MODEL)
    conv_w_full = jnp.transpose(gathered[4][:, 0:4, 0:64], (1, 0, 2)).reshape(4, D_LRU)
    p_lru = jnp.concatenate([conv_b, b_rg, b_ig, lru_lambda, conv_w_full], axis=0)
    wrg2 = _pair_block_diag(w_rg[0]).astype(BF16)
    wig2 = _pair_block_diag(w_ig[0]).astype(BF16)
    xl, gl, q, kv = _mix_in_fwd(x1, mix_pre_g, w_in_full)
    h = _lru_fwd(xl, p_lru, wrg2, wig2)
    o = _attn_fwd(q, kv, sinks)
    x2, mo = _mix_out_fwd(x1, h, gl, o, g_lru_out, g_attn_heads, mix_post_g, w_o_full)
    g = {}
    dx3, n2, df2, gu2, g["ffn2_post_g"], loss_parts, _ = _ffn_fwd(x2, ffn2_pre_g, wgu2, wd2, ffn2_post_g, target, "ffn2_fwd")
    loss_local = jnp.sum(loss_parts[::8, 0])

    partial, from_sibling, from_chips = {}, {}, {}

    def chip_sums(keys):
        return _chip_sums([partial[k] for k in keys], [from_sibling[k] for k in keys], other, "chip_sum_" + keys[0])

    dgu2, dwgu2, dwd2, _ = _ffn_bwd_w(n2, df2, gu2, wd2, "ffn2_bwd_w")
    partial["ffn2_w_gu"] = dwgu2.reshape(N_DEV, D_MODEL, CHUNK)
    partial["ffn2_w_down"] = dwd2.reshape(N_DEV, D_FF // N_DEV, D_MODEL)
    ffn2_keys = ["ffn2_w_gu", "ffn2_w_down"]
    dx2, g["ffn2_pre_g"], got = _ffn_bwd_x(dgu2, wgu2, x2, ffn2_pre_g, dx3, "ffn2_bwd_x",
                                           _SiblingExchange([partial[k] for k in ffn2_keys]))
    from_sibling.update(zip(ffn2_keys, got))
    dy, do, dwo, g["mix_post_g"], g["g_lru_out"], dg_attn_heads = _mix_out_bwd(
        dx2, mo, h, gl, o, g_lru_out, g_attn_heads, mix_post_g, w_o_full)
    g["g_attn_out"] = _reorder_heads(dg_attn_heads, 1, 0, HEAD_ORDER_INVERSE)
    dwo = _reorder_heads(dwo, 0, D_LRU, HEAD_ORDER_INVERSE)
    dq, dkv, dsink, got = _attn_bwd(q, kv, do, sinks, _ChipExchange(chip_sums(ffn2_keys)))
    from_chips.update(zip(ffn2_keys, got))
    dxl, dgl, dp, dwrg2, dwig2 = _lru_bwd(dy, h, xl, gl, p_lru, wrg2, wig2)
    dx1, dwin, g["mix_pre_g"], df1, g["ffn1_post_g"] = _mix_in_bwd(
        dx2, x1, mix_pre_g, dxl, dgl, dq, dkv, w_in_full, f1, ffn1_post_g)
    dwin = _reorder_heads(dwin, 1, 2 * D_LRU, HEAD_ORDER_INVERSE)
    partial["w_in"] = jnp.transpose(dwin.reshape(D_MODEL, N_DEV, D_IN // N_DEV), (1, 0, 2))
    partial["w_o"] = dwo.reshape(N_DEV, D_MODEL // N_DEV, D_MODEL)
    mix_keys = ["w_in", "w_o"]
    (got,) = _run_exchanges([_SiblingExchange([partial[k] for k in mix_keys])], "mix_sibling_exchange")
    from_sibling.update(zip(mix_keys, got))
    dgu1, dwgu1, dwd1, got = _ffn_bwd_w(n1, df1, gu1, wd1, "ffn1_bwd_w", _ChipExchange(chip_sums(mix_keys)))
    from_chips.update(zip(mix_keys, got))
    partial["ffn1_w_gu"] = dwgu1.reshape(N_DEV, D_MODEL, CHUNK)
    partial["ffn1_w_down"] = dwd1.reshape(N_DEV, D_FF // N_DEV, D_MODEL)
    ffn1_keys = ["ffn1_w_gu", "ffn1_w_down"]
    (got,) = _run_exchanges([_SiblingExchange([partial[k] for k in ffn1_keys])], "ffn1_sibling_exchange")
    from_sibling.update(zip(ffn1_keys, got))
    zeros2 = jnp.zeros((2, D_MODEL), F32)
    g_gains = jnp.concatenate([zeros2[:1]] + [g[k] for k in GAINS[1:]] + [zeros2], axis=0)
    g_halves = jnp.concatenate([dp[0:4], g["g_lru_out"], g["g_attn_out"], zeros2[:, :D_LRU]], axis=0)
    g_gates = jnp.concatenate([_pair_block_diag_grad(dwrg2), _pair_block_diag_grad(dwig2)], axis=0)
    g_misc = jnp.concatenate([dp[4:8], jnp.pad(dsink[:, 0].reshape(1, 8), ((0, 0), (0, D_LRU - 8))),
                              jnp.pad(loss_local.reshape(1, 1), ((0, 0), (0, D_LRU - 1))), zeros2[:, :D_LRU]], axis=0)
    dx0, g_first, got = _ffn_bwd_x(dgu1, wgu1, xs, ffn1_pre_g, dx1, "ffn1_bwd_x",
                                   _Both(_ChipExchange(chip_sums(ffn1_keys)), _Gather([g_gains, g_halves, g_gates, g_misc])))
    from_chips.update(zip(ffn1_keys, got[:2]))
    gathered_small = got[2:]

    grads, delta, new_m, new_v = {}, {}, {}, {}
    for name, keys, turned in (("update_column_sharded", transposed, True),
                               ("update_row_sharded", tuple(k for k in big if k not in transposed), False)):
        res = _shard_updates([partial[k] for k in keys], [from_sibling[k] for k in keys], [from_chips[k] for k in keys],
                             [shard2d[k] for k in keys], [shard_view(mom[k], k) for k in keys],
                             [shard_view(var[k], k) for k in keys], place, name, turned)
        for k, out in zip(keys, res):
            grads[k], delta[k], new_m[k], new_v[k] = [shard_unview(r, k) for r in out]

    ((gathered_first,),) = _run_exchanges([_Gather([g_first])], "all_gather_first_gain")
    conv_parts = lax.dynamic_slice(gathered_small[3], (0, 0, me * 64), (N_DEV, 4, 64))

    def small_view(vals):
        out = {k: vals[k] for k in GAINS + HALVES + ("sinks",)}
        out.update({k: vals[k].reshape(512, 64) for k in GATES})
        out["conv_w"] = vals["conv_w"].reshape(4, 64)
        return out

    *small, loss_tile = _small_update([*gathered_small, gathered_first, conv_parts], small_view(w), small_view(mom),
                                      small_view(var))
    for dst, part in zip((grads, delta, new_m, new_v), small):
        for k in SMALL:
            dst[k] = part[k].reshape(w[k].shape)
    return (loss_tile[0, 0], dx0.reshape(x.shape), *[grads[k] for k in names], *[delta[k] for k in names],
            *[new_m[k] for k in names], *[new_v[k] for k in names])
```

```python
import functools

import jax
import jax.numpy as jnp
from jax import lax
from jax.experimental import pallas as pl
from jax.experimental.pallas import tpu as pltpu

F32 = jnp.float32
BF16 = jnp.bfloat16

D_MODEL = 1024
D_FF = 2816
N_DEV = 8
N_CHUNK = 4
CHUNK = D_FF // N_CHUNK
D_LRU = 512
D_ATTN = 512
LRU_GROUP = 128
N_LRU_GROUP = D_LRU // LRU_GROUP
LRU_GROUPS_PER_STEP = 4
HEAD_DIM = 64
BLOCK_Q = 128
D_IN = 1792
HEAD_ORDER = (0, 4, 1, 5, 2, 6, 3, 7)
ATTN_UNROLL = 9
RMS_EPS = 1e-6
LRU_C = 8.0
MASK_VALUE = -1e30
ATTN_SCALE = HEAD_DIM ** -0.5

ADAM_LR = 0.001
ADAM_B1 = 0.9
ADAM_B2 = 0.999
ADAM_EPS = 1e-08
ADAM_WD = 0.01
ADAM_STEP = 10

VMEM_LIMIT_V7X = 56 * 2 ** 20

ANY = pl.BlockSpec(memory_space=pl.ANY)
SMEM = pl.BlockSpec(memory_space=pltpu.SMEM)
MESH = pl.DeviceIdType.MESH


def _params(n_grid=0):
    sem = ("arbitrary",) * n_grid if n_grid else None
    return pltpu.CompilerParams(dimension_semantics=sem, vmem_limit_bytes=VMEM_LIMIT_V7X)


def _dot(a, b):
    return lax.dot_general(a, b, (((1,), (0,)), ((), ())), preferred_element_type=F32)


def _dot_nt(a, b):
    return lax.dot_general(a, b, (((1,), (1,)), ((), ())), preferred_element_type=F32)


def _dot_tn(a, b):
    return lax.dot_general(a, b, (((0,), (0,)), ((), ())), preferred_element_type=F32)


def _sigmoid(x):
    return 1.0 / (1.0 + jnp.exp(-x))


def _rms_fwd(x, g):
    r = lax.rsqrt(jnp.mean(x * x, axis=-1, keepdims=True) + RMS_EPS)
    return x * r * g


def _rms_bwd(x, g, dy):
    r = lax.rsqrt(jnp.mean(x * x, axis=-1, keepdims=True) + RMS_EPS)
    xh = x * r
    dg = jnp.sum(dy * xh, axis=0, keepdims=True)
    dxh = dy * g
    dx = r * (dxh - xh * jnp.mean(dxh * xh, axis=-1, keepdims=True))
    return dx, dg


def _gelu(x):
    c = 0.7978845608028654
    inner = c * (x + 0.044715 * x * x * x)
    th = jnp.tanh(inner)
    ge = 0.5 * x * (1.0 + th)
    dge = 0.5 * (1.0 + th) + 0.5 * x * (1.0 - th * th) * c * (1.0 + 3.0 * 0.044715 * x * x)
    return ge, dge


def _zero_at_first(first, *refs):
    @pl.when(first)
    def _():
        for ref in refs:
            ref[...] = jnp.zeros_like(ref)


def _token_tile(t):
    return 512 if t >= 2048 else t // 2


def _ffn_bwd_tile(t):
    return 1024 if t >= 4096 else t // 2


def _coords():
    return lax.axis_index("x"), lax.axis_index("y"), lax.axis_index("c")


class _Gather:
    n_phases = 3
    at = (0.0, 0.8, 1.0)

    def __init__(self, shards, routed=False):
        k = len(shards)
        self.routed = routed
        self.arrays = list(shards)
        self.out_shape = [jax.ShapeDtypeStruct((N_DEV,) + s.shape, s.dtype) for s in shards]
        self.scratch = [pltpu.SemaphoreType.DMA((7 * k,)), pltpu.SemaphoreType.DMA((7 * k,)), pltpu.SemaphoreType.DMA((k,))]

    def run(self, phase, ins, outs, sems):
        send_sems, recv_sems, local_sems = sems
        k_arr = len(ins)
        x, y, c = _coords()
        me, sibling = (x, y, c), (x, y, 1 - c)
        chips = [(1 - x, y), (x, 1 - y), (1 - x, 1 - y)]
        direct = 2 if self.routed else 3
        relay_from = (x + (1 - c) * (1 - 2 * x), y + c * (1 - 2 * y))
        relay_to = (x + c * (1 - 2 * x), y + (1 - c) * (1 - 2 * y))

        def rows(k, dev):
            return outs[k].at[4 * dev[0] + 2 * dev[1] + dev[2]]

        def copy(k, slot, block, to, src=None):
            return pltpu.make_async_remote_copy(
                src_ref=rows(k, block) if src is None else src, dst_ref=rows(k, block),
                send_sem=send_sems.at[7 * k + slot], recv_sem=recv_sems.at[7 * k + slot],
                device_id=to, device_id_type=MESH)

        def mine():
            return [pltpu.make_async_copy(ins[k], rows(k, me), local_sems.at[k]) for k in range(k_arr)]

        def first():
            return [copy(k, slot, me, to, src=ins[k]) for k in range(k_arr)
                    for slot, to in enumerate([sibling] + [(*chip, c) for chip in chips[:direct]])]

        def relayed(k):
            return copy(k, 3, (*relay_from, c), (*relay_to, c))

        def passed(j, k):
            return copy(k, 4 + j, (*chips[j], c), sibling)

        if phase == 0:
            for cp in mine() + first():
                cp.start()
        elif phase == 1:
            if self.routed:
                for k in range(k_arr):
                    copy(k, 1 + c, (*relay_from, c), me).wait_recv()
                    relayed(k).start()
                for k in range(k_arr):
                    copy(k, 2 - c, (*relay_to, c), me).wait_recv()
            else:
                for j in range(direct):
                    for k in range(k_arr):
                        copy(k, 1 + j, (*chips[j], c), me).wait_recv()
            for k in range(k_arr):
                for j in range(direct):
                    passed(j, k).start()
        else:
            for k in range(k_arr):
                if self.routed:
                    copy(k, 3, (*chips[2], c), me).wait_recv()
                    passed(2, k).start()
            for k in range(k_arr):
                copy(k, 0, sibling, me).wait_recv()
                for j, chip in enumerate(chips):
                    copy(k, 4 + j, (*chip, 1 - c), me).wait_recv()
            sent = first() + [passed(j, k) for j in range(3) for k in range(k_arr)]
            if self.routed:
                sent += [relayed(k) for k in range(k_arr)]
            for cp in sent:
                cp.wait_send()
            for cp in mine():
                cp.wait()


class _SiblingExchange:
    n_phases = 2
    at = (0.0, 1.0)

    def __init__(self, grads):
        k = len(grads)
        self.arrays = list(grads)
        self.out_shape = [jax.ShapeDtypeStruct((4,) + g.shape[1:], g.dtype) for g in grads]
        self.scratch = [pltpu.SemaphoreType.DMA((4 * k,)), pltpu.SemaphoreType.DMA((4 * k,))]

    def run(self, phase, ins, outs, sems):
        send_sems, recv_sems = sems
        x, y, c = _coords()
        copies = [pltpu.make_async_remote_copy(
            src_ref=ins[k].at[2 * q + (1 - c)], dst_ref=outs[k].at[q],
            send_sem=send_sems.at[4 * k + q], recv_sem=recv_sems.at[4 * k + q],
            device_id=(x, y, 1 - c), device_id_type=MESH) for k in range(len(ins)) for q in range(4)]
        for cp in copies:
            if phase == 0:
                cp.start()
            else:
                cp.wait_recv()
                cp.wait_send()


class _ChipExchange:
    n_phases = 2
    at = (0.0, 1.0)

    def __init__(self, chip_sums):
        k = len(chip_sums)
        self.arrays = list(chip_sums)
        self.out_shape = [jax.ShapeDtypeStruct((3,) + s.shape[1:], s.dtype) for s in chip_sums]
        self.scratch = [pltpu.SemaphoreType.DMA((3 * k,)), pltpu.SemaphoreType.DMA((3 * k,))]

    def run(self, phase, ins, outs, sems):
        send_sems, recv_sems = sems
        x, y, c = _coords()
        chips = [(1 - x, y), (x, 1 - y), (1 - x, 1 - y)]
        copies = [pltpu.make_async_remote_copy(
            src_ref=ins[k].at[j], dst_ref=outs[k].at[j],
            send_sem=send_sems.at[3 * k + j], recv_sem=recv_sems.at[3 * k + j],
            device_id=(*chip, c), device_id_type=MESH) for k in range(len(ins)) for j, chip in enumerate(chips)]
        for cp in copies:
            if phase == 0:
                cp.start()
            else:
                cp.wait_recv()
                cp.wait_send()


class _Both:
    n_phases = 3
    at = (0.0, 0.95, 1.0)

    def __init__(self, two_phase, gather):
        self.parts = (two_phase, gather)
        self.arrays = two_phase.arrays + gather.arrays
        self.out_shape = two_phase.out_shape + gather.out_shape
        self.scratch = two_phase.scratch + gather.scratch

    def run(self, phase, ins, outs, sems):
        a, b = self.parts
        n_in, n_out, n_sem = len(a.arrays), len(a.out_shape), len(a.scratch)
        refs_a = (ins[:n_in], outs[:n_out], sems[:n_sem])
        refs_b = (ins[n_in:], outs[n_out:], sems[n_sem:])
        b.run(phase, *refs_b)
        if phase == 0:
            a.run(0, *refs_a)
        if phase == 2:
            a.run(1, *refs_a)


class _Pair:
    n_phases = 2
    at = (0.0, 1.0)

    def __init__(self, first, second):
        self.parts = (first, second)
        self.arrays = first.arrays + second.arrays
        self.out_shape = first.out_shape + second.out_shape
        self.scratch = first.scratch + second.scratch

    def run(self, phase, ins, outs, sems):
        a, b = self.parts
        n_in, n_out, n_sem = len(a.arrays), len(a.out_shape), len(a.scratch)
        a.run(phase, ins[:n_in], outs[:n_out], sems[:n_sem])
        b.run(phase, ins[n_in:], outs[n_out:], sems[n_sem:])


class _Host:
    def __init__(self, exchange):
        self.ex = exchange
        self.args = [] if exchange is None else exchange.arrays
        self.in_specs = [ANY] * len(self.args)
        self.out_shape = [] if exchange is None else exchange.out_shape
        self.out_specs = [ANY] * len(self.out_shape)
        self.scratch = [] if exchange is None else exchange.scratch

    def split(self, refs, n_in, n_out, n_scratch):
        a, b, s = len(self.args), len(self.out_shape), len(self.scratch)
        own_in, ex_in = refs[:n_in], refs[n_in:n_in + a]
        rest = refs[n_in + a:]
        own_out, ex_out = rest[:n_out], rest[n_out:n_out + b]
        rest = rest[n_out + b:]
        own_scratch, ex_sems = rest[:n_scratch], rest[n_scratch:n_scratch + s]
        return list(own_in) + list(own_out) + list(own_scratch), (ex_in, ex_out, ex_sems)

    def at_steps(self, step, n_steps, ex_refs):
        if self.ex is None:
            return
        for p in range(self.ex.n_phases):
            pl.when(step == int(round(self.ex.at[p] * (n_steps - 1))))(functools.partial(self.ex.run, p, *ex_refs))

    def phase(self, p, ex_refs):
        if self.ex is not None:
            self.ex.run(p, *ex_refs)


def _run_exchanges(exchanges, name):
    hosts = [_Host(ex) for ex in exchanges]
    n_in = [len(h.args) for h in hosts]
    n_out = [len(h.out_shape) for h in hosts]
    n_sc = [len(h.scratch) for h in hosts]

    def body(*refs):
        ins, outs, scr = refs[:sum(n_in)], refs[sum(n_in):sum(n_in) + sum(n_out)], refs[sum(n_in) + sum(n_out):]
        parts = []
        for e in range(len(hosts)):
            parts.append((ins[sum(n_in[:e]):sum(n_in[:e + 1])], outs[sum(n_out[:e]):sum(n_out[:e + 1])],
                          scr[sum(n_sc[:e]):sum(n_sc[:e + 1])]))
        for h, part in zip(hosts, parts):
            h.phase(0, part)
        for h, part in zip(hosts, parts):
            for p in range(1, h.ex.n_phases):
                h.phase(p, part)

    res = pl.pallas_call(
        body, name=name, in_specs=[ANY] * sum(n_in), out_specs=[ANY] * sum(n_out),
        out_shape=[s for h in hosts for s in h.out_shape], scratch_shapes=[s for h in hosts for s in h.scratch],
    )(*[a for h in hosts for a in h.args])
    return [res[sum(n_out[:e]):sum(n_out[:e + 1])] for e in range(len(hosts))]


def _ffn_fwd(x, g_pre, wgu, wd, g_post, target, name, exchange=None):
    t = x.shape[0]
    tm = _token_tile(t)
    n_i = t // tm
    with_loss = target is not None
    host = _Host(exchange)
    n_in, n_out = (6, 6) if with_loss else (5, 4)

    def body(*refs):
        own, ex_refs = host.split(refs, n_in, n_out, 0)
        if with_loss:
            x_ref, gpre_ref, wgu_ref, wd_ref, gpost_ref, tgt_ref, xo_ref, n_ref, df_ref, gu_ref, dgpost_ref, loss_ref = own
            _zero_at_first(pl.program_id(0) == 0, dgpost_ref)
        else:
            x_ref, gpre_ref, wgu_ref, wd_ref, gpost_ref, xo_ref, f_ref, n_ref, gu_ref = own
        host.at_steps(pl.program_id(0), n_i, ex_refs)
        x = x_ref[...]
        n = _rms_fwd(x, gpre_ref[...]).astype(BF16)
        n_ref[...] = n
        f = None
        for j in range(N_CHUNK):
            gate = _dot_nt(n, wgu_ref[0, j])
            up = _dot_nt(n, wgu_ref[1, j])
            gu_ref[0, j] = gate.astype(BF16)
            gu_ref[1, j] = up.astype(BF16)
            part = _dot((gate * _sigmoid(gate) * up).astype(BF16), wd_ref[j])
            f = part if f is None else f + part
        xo = x + 0.5 * _rms_fwd(f, gpost_ref[...])
        if with_loss:
            err = xo - tgt_ref[...]
            d_out = err * (1.0 / D_MODEL)
            xo_ref[...] = d_out
            df, dg = _rms_bwd(f, gpost_ref[...], 0.5 * d_out)
            df_ref[...] = df.astype(BF16)
            dgpost_ref[...] += dg
            part = 0.5 * jnp.sum(jnp.sum(err * err, axis=-1, keepdims=True) * (1.0 / D_MODEL), axis=0, keepdims=True)
            loss_ref[...] = jnp.broadcast_to(part, loss_ref.shape)
        else:
            f_ref[...] = f
            xo_ref[...] = xo

    tok = pl.BlockSpec((tm, D_MODEL), lambda i: (i, 0))
    vec = pl.BlockSpec((1, D_MODEL), lambda i: (0, 0))
    act = pl.BlockSpec((2, N_CHUNK, tm, CHUNK), lambda i: (0, 0, i, 0))
    tok_f32 = jax.ShapeDtypeStruct((t, D_MODEL), F32)
    tok_bf16 = jax.ShapeDtypeStruct((t, D_MODEL), BF16)
    act_shape = jax.ShapeDtypeStruct((2, N_CHUNK, t, CHUNK), BF16)
    in_specs = [tok, vec,
                pl.BlockSpec((2, N_CHUNK, CHUNK, D_MODEL), lambda i: (0, 0, 0, 0), pipeline_mode=pl.Buffered(1)),
                pl.BlockSpec((N_CHUNK, CHUNK, D_MODEL), lambda i: (0, 0, 0), pipeline_mode=pl.Buffered(1)),
                vec]
    args = [x, g_pre, wgu, wd, g_post]
    if with_loss:
        in_specs.append(tok)
        args.append(target)
        out_shape = [tok_f32, tok_bf16, tok_bf16, act_shape, jax.ShapeDtypeStruct((1, D_MODEL), F32),
                     jax.ShapeDtypeStruct((n_i * 8, 128), F32)]
        out_specs = [tok, tok, tok, act, vec, pl.BlockSpec((8, 128), lambda i: (i, 0))]
    else:
        out_shape = [tok_f32, tok_f32, tok_bf16, act_shape]
        out_specs = [tok, tok, tok, act]
    res = pl.pallas_call(
        body, name=name, grid=(n_i,), in_specs=in_specs + host.in_specs, out_specs=out_specs + host.out_specs,
        out_shape=out_shape + host.out_shape, scratch_shapes=host.scratch, compiler_params=_params(1),
    )(*args, *host.args)
    return (*res[:n_out], list(res[n_out:]))


def _ffn_bwd_w(n, df, gu, wd, name, exchange=None):
    t = n.shape[0]
    tm = _ffn_bwd_tile(t)
    n_i = t // tm
    host = _Host(exchange)

    def body(*refs):
        (n_ref, df_ref, gu_ref, wd_ref, dgu_ref, dwgu_ref, dwd_ref), ex_refs = host.split(refs, 4, 3, 0)
        i = pl.program_id(1)
        host.at_steps(pl.program_id(0) * n_i + i, N_CHUNK * n_i, ex_refs)
        _zero_at_first(i == 0, dwgu_ref, dwd_ref)
        nb = n_ref[...]
        dfb = df_ref[...]
        gate = gu_ref[0, 0].astype(F32)
        up = gu_ref[1, 0].astype(F32)
        s = _sigmoid(gate)
        silu = gate * s
        a = (silu * up).astype(BF16)
        da = _dot_nt(dfb, wd_ref[0])
        dup = (da * silu).astype(BF16)
        dgate = (da * up * (s * (1.0 + gate * (1.0 - s)))).astype(BF16)
        dgu_ref[0, 0] = dgate
        dgu_ref[1, 0] = dup
        dwgu_ref[0, 0] += _dot_tn(nb, dgate)
        dwgu_ref[1, 0] += _dot_tn(nb, dup)
        dwd_ref[0] += _dot_tn(a, dfb)

    tok = pl.BlockSpec((tm, D_MODEL), lambda j, i: (i, 0))
    act = pl.BlockSpec((2, 1, tm, CHUNK), lambda j, i: (0, j, i, 0))
    wgu_spec = pl.BlockSpec((2, 1, D_MODEL, CHUNK), lambda j, i: (0, j, 0, 0))
    wd_spec = pl.BlockSpec((1, CHUNK, D_MODEL), lambda j, i: (j, 0, 0))
    res = pl.pallas_call(
        body, name=name, grid=(N_CHUNK, n_i),
        in_specs=[tok, tok, act, wd_spec] + host.in_specs,
        out_specs=[act, wgu_spec, wd_spec] + host.out_specs,
        out_shape=[jax.ShapeDtypeStruct((2, N_CHUNK, t, CHUNK), BF16),
                   jax.ShapeDtypeStruct((2, N_CHUNK, D_MODEL, CHUNK), F32),
                   jax.ShapeDtypeStruct((N_CHUNK, CHUNK, D_MODEL), F32)] + host.out_shape,
        scratch_shapes=host.scratch, compiler_params=_params(2),
    )(n, df, gu, wd, *host.args)
    return (*res[:3], list(res[3:]))


def _ffn_bwd_x(dgu, wgu, x, g_pre, d_out, dg_so_far, tiles, name, exchange=None):
    tm = _token_tile(x.shape[0])
    first, n_i = tiles
    host = _Host(exchange)

    def body(*refs):
        (dgu_ref, wgu_ref, x_ref, gpre_ref, do_ref, dg0_ref, dx_ref, dgpre_ref), ex_refs = host.split(refs, 6, 2, 0)
        i = pl.program_id(0)
        host.at_steps(i, n_i, ex_refs)

        @pl.when(i == 0)
        def _():
            dgpre_ref[...] = dg0_ref[...]

        dn = _dot(dgu_ref[0, 0], wgu_ref[0, 0]) + _dot(dgu_ref[1, 0], wgu_ref[1, 0])
        for j in range(1, N_CHUNK):
            dn = dn + _dot(dgu_ref[0, j], wgu_ref[0, j]) + _dot(dgu_ref[1, j], wgu_ref[1, j])
        dx, dg = _rms_bwd(x_ref[...], gpre_ref[...], dn)
        dx_ref[...] = do_ref[...] + dx
        dgpre_ref[...] += dg

    tok = pl.BlockSpec((tm, D_MODEL), lambda i: (first + i, 0))
    vec = pl.BlockSpec((1, D_MODEL), lambda i: (0, 0))
    res = pl.pallas_call(
        body, name=name, grid=(n_i,),
        in_specs=[pl.BlockSpec((2, N_CHUNK, tm, CHUNK), lambda i: (0, 0, first + i, 0)),
                  pl.BlockSpec((2, N_CHUNK, CHUNK, D_MODEL), lambda i: (0, 0, 0, 0), pipeline_mode=pl.Buffered(1)),
                  tok, vec, tok, vec] + host.in_specs,
        out_specs=[pl.BlockSpec((tm, D_MODEL), lambda i: (i, 0)), vec] + host.out_specs,
        out_shape=[jax.ShapeDtypeStruct((n_i * tm, D_MODEL), F32), jax.ShapeDtypeStruct((1, D_MODEL), F32)] + host.out_shape,
        scratch_shapes=host.scratch, compiler_params=_params(1),
    )(dgu, wgu, x, g_pre, d_out, dg_so_far, *host.args)
    return (*res[:2], list(res[2:]))


def _mix_in_fwd(x1, g, w_in):
    t = x1.shape[0]
    tm = _token_tile(t)

    def body(x_ref, g_ref, w_ref, xl_ref, gl_ref, q_ref, kv_ref):
        n = _rms_fwd(x_ref[...], g_ref[...]).astype(BF16)
        proj = _dot_nt(n, w_ref[...])
        xl_ref[...] = proj[:, 0:512]
        gl_ref[...] = proj[:, 512:1024]
        q_ref[...] = proj[:, 1024:1536].astype(BF16)
        kv_ref[...] = proj[:, 1536:1792].astype(BF16)

    tok = pl.BlockSpec((tm, D_MODEL), lambda i: (i, 0))
    half = pl.BlockSpec((tm, 512), lambda i: (i, 0))
    return pl.pallas_call(
        body, name="mix_in_fwd", grid=(t // tm,),
        in_specs=[tok, pl.BlockSpec((1, D_MODEL), lambda i: (0, 0)), pl.BlockSpec((D_IN, D_MODEL), lambda i: (0, 0))],
        out_specs=[half, half, half, pl.BlockSpec((tm, 256), lambda i: (i, 0))],
        out_shape=[jax.ShapeDtypeStruct((t, 512), F32), jax.ShapeDtypeStruct((t, 512), F32),
                   jax.ShapeDtypeStruct((t, 512), BF16), jax.ShapeDtypeStruct((t, 256), BF16)],
        compiler_params=_params(1),
    )(x1, g, w_in)


def _shift_down(x, before, s):
    if s == 0:
        return x
    rolled = pltpu.roll(x, s, 0)
    ext = jnp.concatenate([before, x[0:8]], axis=0)
    first8 = pltpu.roll(ext, s, 0)[8:16]
    return jnp.concatenate([first8, rolled[8:]], axis=0)


def _shift_up(x, after, s):
    if s == 0:
        return x
    rows = x.shape[0]
    rolled = pltpu.roll(x, rows - s, 0)
    ext = jnp.concatenate([x[rows - 8:rows], after], axis=0)
    last8 = pltpu.roll(ext, 16 - s, 0)[0:8]
    return jnp.concatenate([rolled[:rows - 8], last8], axis=0)


def _log_sigmoid(x):
    e = jnp.exp(-jnp.abs(x))
    log1p_e = jnp.where(e < 0.01, e * (1.0 - e * (0.5 - e * (1.0 / 3.0))), jnp.log(1.0 + e))
    return jnp.minimum(x, 0.0) - log1p_e


def _lru_gates(xc, p_ref, wrg, wig):
    xcb = xc.astype(BF16)
    r = _sigmoid(_dot(xcb, wrg) + p_ref[1:2, :])
    ig = _sigmoid(_dot(xcb, wig) + p_ref[2:3, :])
    ls = _log_sigmoid(p_ref[3:4, :])
    log_a = LRU_C * r * ls
    a = jnp.exp(log_a)
    mult = jnp.sqrt(-jnp.tanh(log_a) * (a * a + 1.0))
    return xcb, r, ig, ls, a, mult


def _conv_taps(x, before, p_ref):
    xc = x * p_ref[7:8, :]
    for s in (1, 2, 3):
        xc = xc + _shift_down(x, before, s) * p_ref[7 - s:8 - s, :]
    return xc + p_ref[0:1, :]


def _group_view(ref, gi):
    if len(ref.shape) == 2:
        return ref.at[:, pl.ds(LRU_GROUP * gi, LRU_GROUP)]
    return ref.at[pl.ds(gi, 1)]


def _lru_block_rows(t):
    return 512 if t >= 1024 else t // 2


def _lru_fwd(xl, p, wrg2, wig2):
    t = xl.shape[0]
    tb = _lru_block_rows(t)

    def body(*refs):
        x_tail, h_carry = refs[-2:]

        @pl.when(pl.program_id(1) == 0)
        def _():
            x_tail[...] = jnp.zeros_like(x_tail)
            h_carry[...] = jnp.zeros_like(h_carry)

        for gi in range(LRU_GROUPS_PER_STEP):
            group(*[_group_view(r, gi) for r in refs])

    def group(xl_ref, p_ref, wrg_ref, wig_ref, h_ref, x_tail, h_carry):
        x = xl_ref[...]
        xc = _conv_taps(x, x_tail[...], p_ref)
        x_tail[...] = x[tb - 8:tb]
        _, r, ig, ls, a, mult = _lru_gates(xc, p_ref, wrg_ref[0], wig_ref[0])
        u = mult * ig * xc
        row = lax.broadcasted_iota(jnp.int32, (tb, LRU_GROUP), 0)
        s = 1
        while s < tb:
            keep = row >= s
            u = jnp.where(keep, a * pltpu.roll(u, s, 0) + u, u)
            a = jnp.where(keep, a * pltpu.roll(a, s, 0), a)
            s *= 2
        h = u + a * h_carry[0:1, :]
        h_ref[...] = h
        h_carry[...] = jnp.broadcast_to(h[tb - 1:tb], h_carry.shape)

    wide = LRU_GROUPS_PER_STEP * LRU_GROUP
    blk = pl.BlockSpec((tb, wide), lambda g, tt: (tt, g))
    par = pl.BlockSpec((8, wide), lambda g, tt: (0, g))
    wsp = pl.BlockSpec((LRU_GROUPS_PER_STEP, LRU_GROUP, LRU_GROUP), lambda g, tt: (g, 0, 0))
    return pl.pallas_call(
        body, name="lru_fwd", grid=(N_LRU_GROUP // LRU_GROUPS_PER_STEP, t // tb), in_specs=[blk, par, wsp, wsp],
        out_specs=blk, out_shape=jax.ShapeDtypeStruct((t, D_LRU), F32),
        scratch_shapes=[pltpu.VMEM((8, wide), F32), pltpu.VMEM((8, wide), F32)],
        compiler_params=_params(2),
    )(xl, p, wrg2, wig2)


def _lru_bwd(dy, h, xl, gl, p, wrg2, wig2):
    t = xl.shape[0]
    tb = _lru_block_rows(t)
    n_tb = t // tb
    tb8 = tb // 8

    def body(*refs):
        _zero_at_first(pl.program_id(1) == 0, *refs[11:])
        for gi in range(LRU_GROUPS_PER_STEP):
            group(*[_group_view(r, gi) for r in refs])

    def group(dy_ref, h_ref, hprev_ref, xl_ref, xprev_ref, gl_ref, p_ref, wrg_ref, wig_ref,
              dxl_ref, dgl_ref, dp_ref, dwrg_ref, dwig_ref, g_carry, a_carry, dxc_head):
        tt = n_tb - 1 - pl.program_id(1)
        has_prev = (tt > 0).astype(F32)
        x = xl_ref[...]
        x_before = xprev_ref[...] * has_prev
        xs = [_shift_down(x, x_before, s) for s in range(4)]
        xc = xs[0] * p_ref[7:8, :] + xs[1] * p_ref[6:7, :] + xs[2] * p_ref[5:6, :] + xs[3] * p_ref[4:5, :] + p_ref[0:1, :]
        wrg = wrg_ref[0]
        wig = wig_ref[0]
        xcb, r, ig, ls, a, mult = _lru_gates(xc, p_ref, wrg, wig)

        hh = h_ref[...]
        h_m1 = _shift_down(hh, hprev_ref[...] * has_prev, 1)
        ge, dge = _gelu(gl_ref[...])
        dy = dy_ref[...]
        dgl_ref[...] = dy * hh * dge
        dh = dy * ge

        b = _shift_up(a, a_carry[...], 1)
        row = lax.broadcasted_iota(jnp.int32, (tb, LRU_GROUP), 0)
        g = dh
        s = 1
        while s < tb:
            keep = row < tb - s
            g = jnp.where(keep, b * pltpu.roll(g, tb - s, 0) + g, g)
            b = jnp.where(keep, b * pltpu.roll(b, tb - s, 0), b)
            s *= 2
        g = g + b * g_carry[0:1, :]
        g_carry[...] = jnp.broadcast_to(g[0:1], g_carry.shape)
        a_carry[...] = jnp.broadcast_to(a[0:1], a_carry.shape)

        da = g * h_m1
        dmult = g * ig * xc
        dig = g * mult * xc
        dxc = g * mult * ig
        dlog_a = da * a - dmult * (a * a) / mult
        dr = dlog_a * (LRU_C * ls)
        dls = jnp.sum(dlog_a * (LRU_C * r), axis=0, keepdims=True)
        dlam = dls * _sigmoid(-p_ref[3:4, :])
        dpre_r = dr * r * (1.0 - r)
        dpre_i = dig * ig * (1.0 - ig)
        dprb = dpre_r.astype(BF16)
        dpib = dpre_i.astype(BF16)
        dxc = dxc + _dot_nt(dprb, wrg) + _dot_nt(dpib, wig)
        dwrg_ref[0] += _dot_tn(xcb, dprb)
        dwig_ref[0] += _dot_tn(xcb, dpib)

        after = dxc_head[...]
        dxl = dxc * p_ref[7:8, :]
        for s in (1, 2, 3):
            dxl = dxl + _shift_up(dxc, after, s) * p_ref[7 - s:8 - s, :]
        dxl_ref[...] = dxl
        dxc_head[...] = dxc[0:8]

        rows = [jnp.sum(dxc, axis=0, keepdims=True), jnp.sum(dpre_r, axis=0, keepdims=True),
                jnp.sum(dpre_i, axis=0, keepdims=True), dlam]
        rows += [jnp.sum(dxc * xs[3 - k], axis=0, keepdims=True) for k in range(4)]
        dp_ref[...] += jnp.concatenate(rows, axis=0)

    wide = LRU_GROUPS_PER_STEP * LRU_GROUP
    blk = pl.BlockSpec((tb, wide), lambda g, s: (n_tb - 1 - s, g))
    prev8 = pl.BlockSpec((8, wide), lambda g, s: (jnp.maximum((n_tb - 1 - s) * tb8 - 1, 0), g))
    par = pl.BlockSpec((8, wide), lambda g, s: (0, g))
    wsp = pl.BlockSpec((LRU_GROUPS_PER_STEP, LRU_GROUP, LRU_GROUP), lambda g, s: (g, 0, 0))
    return pl.pallas_call(
        body, name="lru_bwd", grid=(N_LRU_GROUP // LRU_GROUPS_PER_STEP, n_tb),
        in_specs=[blk, blk, prev8, blk, prev8, blk, par, wsp, wsp], out_specs=[blk, blk, par, wsp, wsp],
        out_shape=[jax.ShapeDtypeStruct((t, D_LRU), F32), jax.ShapeDtypeStruct((t, D_LRU), F32),
                   jax.ShapeDtypeStruct((8, D_LRU), F32),
                   jax.ShapeDtypeStruct((N_LRU_GROUP, LRU_GROUP, LRU_GROUP), F32),
                   jax.ShapeDtypeStruct((N_LRU_GROUP, LRU_GROUP, LRU_GROUP), F32)],
        scratch_shapes=[pltpu.VMEM((8, wide), F32)] * 3,
        compiler_params=_params(2),
    )(dy, h, h, xl, xl, gl, p, wrg2, wig2)


def _attn_bias(first_block):
    qi = jnp.bitwise_and(lax.broadcasted_iota(jnp.int32, (4 * BLOCK_Q, 2 * BLOCK_Q), 0), BLOCK_Q - 1)
    kj = lax.broadcasted_iota(jnp.int32, (4 * BLOCK_Q, 2 * BLOCK_Q), 1)
    rel = qi + BLOCK_Q - kj
    mask = (rel >= 0) & (rel < BLOCK_Q)
    if first_block:
        mask = mask & (kj >= BLOCK_Q)
    return jnp.where(mask, 0.0, MASK_VALUE)


def _sink_column(sinks):
    hrow = lax.broadcasted_iota(jnp.int32, (4 * BLOCK_Q, 1), 0)
    return jnp.where(hrow < BLOCK_Q, sinks[0],
                     jnp.where(hrow < 2 * BLOCK_Q, sinks[1], jnp.where(hrow < 3 * BLOCK_Q, sinks[2], sinks[3])))


def _attn_scores(qv, kvv, n, bias, sk, lo):
    r0 = pl.multiple_of(n * BLOCK_Q, BLOCK_Q)
    rp = pl.multiple_of(jnp.maximum(n - 1, 0) * BLOCK_Q, BLOCK_Q)
    kvb = jnp.concatenate([kvv[pl.ds(rp, BLOCK_Q), :], kvv[pl.ds(r0, BLOCK_Q), :]], axis=0)
    k2 = kvb[:, 0:128]
    v2 = kvb[:, 128:256]
    qs = _stack_heads(qv[pl.ds(r0, BLOCK_Q), :], lo)
    s = _dot_nt(qs, k2) * ATTN_SCALE + bias
    m = jnp.maximum(jnp.max(s, axis=-1, keepdims=True), sk)
    e = jnp.exp(s - m)
    es = jnp.exp(sk - m)
    inv = 1.0 / (jnp.sum(e, axis=-1, keepdims=True) + es)
    return r0, rp, qs, k2, v2, e * inv, es * inv


def _stack_heads(pair2, lo):
    p0 = pair2[:, 0:128]
    p1 = pair2[:, 128:256]
    z = jnp.zeros_like(p0)
    return jnp.concatenate([jnp.where(lo, p0, z), jnp.where(lo, z, p0), jnp.where(lo, p1, z), jnp.where(lo, z, p1)], axis=0)


def _unstack_heads(st, lo):
    b = BLOCK_Q
    return jnp.concatenate([jnp.where(lo, st[0:b], st[b:2 * b]), jnp.where(lo, st[2 * b:3 * b], st[3 * b:4 * b])], axis=1)


def _attn_fwd(q, kv, sinks):
    t = q.shape[0]
    n_blk = t // BLOCK_Q

    def body(q_hbm, kv_hbm, s_ref, o_hbm, q2, kvv, o2, bias0, bias, sem):
        lo = lax.broadcasted_iota(jnp.int32, (BLOCK_Q, 128), 1) < HEAD_DIM
        cols = [pl.ds(256 * g, 256) for g in range(2)]
        loads = [pltpu.make_async_copy(kv_hbm, kvv, sem.at[0])]
        loads += [pltpu.make_async_copy(q_hbm.at[:, cols[g]], q2.at[g], sem.at[1 + g]) for g in range(2)]
        stores = [pltpu.make_async_copy(o2.at[g], o_hbm.at[:, cols[g]], sem.at[3 + g]) for g in range(2)]
        for cp in loads:
            cp.start()
        bias0[...] = _attn_bias(True)
        bias[...] = _attn_bias(False)
        loads[0].wait()
        for g in range(2):
            loads[1 + g].wait()
            qv, ov = q2.at[g], o2.at[g]
            sk = _sink_column([s_ref[0, HEAD_ORDER[4 * g + i]] for i in range(4)])

            def block(n, bias_ref):
                r0, _, _, _, v2, prob, _ = _attn_scores(qv, kvv, n, bias_ref[...], sk, lo)
                ov[pl.ds(r0, BLOCK_Q), :] = _unstack_heads(_dot(prob.astype(BF16), v2), lo)

            block(0, bias0)

            def later(n, carry):
                block(n, bias)
                return carry

            lax.fori_loop(1, n_blk, later, 0, unroll=ATTN_UNROLL)
            stores[g].start()
        for cp in stores:
            cp.wait()

    return pl.pallas_call(
        body, name="attn_fwd", in_specs=[ANY, ANY, SMEM], out_specs=ANY,
        out_shape=jax.ShapeDtypeStruct((t, D_ATTN), F32),
        scratch_shapes=[pltpu.VMEM((2, t, 256), BF16), pltpu.VMEM((t, 256), BF16), pltpu.VMEM((2, t, 256), F32),
                        pltpu.VMEM((4 * BLOCK_Q, 2 * BLOCK_Q), F32), pltpu.VMEM((4 * BLOCK_Q, 2 * BLOCK_Q), F32),
                        pltpu.SemaphoreType.DMA((5,))],
        compiler_params=_params(),
    )(q, kv, sinks)


def _attn_bwd(q, kv, do, sinks, exchange=None):
    t = q.shape[0]
    n_blk = t // BLOCK_Q
    host = _Host(exchange)

    def body(*refs):
        own, ex_refs = host.split(refs, 4, 3, 9)
        q_hbm, kv_hbm, do_hbm, s_ref, dq_hbm, dkv_hbm, dsink_ref, q2, kvv, do2, dqv, dkvv, ds_acc, bias0, bias, sem = own
        host.phase(0, ex_refs)
        lo = lax.broadcasted_iota(jnp.int32, (BLOCK_Q, 128), 1) < HEAD_DIM
        loads = [pltpu.make_async_copy(kv_hbm, kvv, sem.at[0])]
        for g in range(2):
            loads += [pltpu.make_async_copy(src.at[:, pl.ds(256 * g, 256)], dst.at[g], sem.at[1 + 2 * g + i])
                      for i, (src, dst) in enumerate(((q_hbm, q2), (do_hbm, do2)))]
        for cp in loads:
            cp.start()
        bias0[...] = _attn_bias(True)
        bias[...] = _attn_bias(False)
        loads[0].wait()
        for g in range(2):
            cols = pl.ds(256 * g, 256)
            for cp in loads[1 + 2 * g:3 + 2 * g]:
                cp.wait()
            qv, dov = q2.at[g], do2.at[g]
            heads = [HEAD_ORDER[4 * g + i] for i in range(4)]
            sk = _sink_column([s_ref[0, h] for h in heads])
            ds_acc[...] = jnp.zeros_like(ds_acc)

            def block(n, bias_ref, has_prev):
                r0, rp, qs, k2, v2, prob, psink = _attn_scores(qv, kvv, n, bias_ref[...], sk, lo)
                pb = prob.astype(BF16)
                dos = _stack_heads(dov[pl.ds(r0, BLOCK_Q), :], lo)
                dp = _dot_nt(dos, v2)
                dsum = jnp.sum(prob * dp, axis=-1, keepdims=True)
                dsb = (prob * (dp - dsum) * ATTN_SCALE).astype(BF16)
                ds_acc[...] -= psink * dsum
                dqv[pl.ds(r0, BLOCK_Q), :] = _unstack_heads(_dot(dsb, k2), lo).astype(BF16)
                dk2 = _dot_tn(dsb, qs)
                dv2 = _dot_tn(pb, dos)
                cur = jnp.concatenate([dk2[BLOCK_Q:], dv2[BLOCK_Q:]], axis=1)
                if g == 0:
                    dkvv[pl.ds(r0, BLOCK_Q), :] = cur
                else:
                    dkvv[pl.ds(r0, BLOCK_Q), :] += cur
                if has_prev:
                    dkvv[pl.ds(rp, BLOCK_Q), :] += jnp.concatenate([dk2[:BLOCK_Q], dv2[:BLOCK_Q]], axis=1)

            block(0, bias0, False)

            def later(n, carry):
                block(n, bias, True)
                return carry

            lax.fori_loop(1, n_blk, later, 0, unroll=ATTN_UNROLL)
            for i, h in enumerate(heads):
                tot = jnp.sum(ds_acc[BLOCK_Q * i:BLOCK_Q * (i + 1), :], axis=0, keepdims=True)
                dsink_ref[h:h + 1, :] = jnp.broadcast_to(tot, (1, 128))
            store = pltpu.make_async_copy(dqv, dq_hbm.at[:, cols], sem.at[5])
            store.start()
            store.wait()
        store = pltpu.make_async_copy(dkvv, dkv_hbm, sem.at[6])
        store.start()
        store.wait()
        if exchange is not None:
            for p in range(1, exchange.n_phases):
                host.phase(p, ex_refs)

    res = pl.pallas_call(
        body, name="attn_bwd", in_specs=[ANY, ANY, ANY, SMEM] + host.in_specs,
        out_specs=[ANY, ANY, pl.BlockSpec(memory_space=pltpu.VMEM)] + host.out_specs,
        out_shape=[jax.ShapeDtypeStruct((t, D_ATTN), BF16), jax.ShapeDtypeStruct((t, 256), F32),
                   jax.ShapeDtypeStruct((8, 128), F32)] + host.out_shape,
        scratch_shapes=[pltpu.VMEM((2, t, 256), BF16), pltpu.VMEM((t, 256), BF16), pltpu.VMEM((2, t, 256), BF16),
                        pltpu.VMEM((t, 256), BF16), pltpu.VMEM((t, 256), F32), pltpu.VMEM((4 * BLOCK_Q, 1), F32),
                        pltpu.VMEM((4 * BLOCK_Q, 2 * BLOCK_Q), F32), pltpu.VMEM((4 * BLOCK_Q, 2 * BLOCK_Q), F32),
                        pltpu.SemaphoreType.DMA((7,))] + host.scratch,
        compiler_params=_params(),
    )(q, kv, do, sinks, *host.args)
    return (*res[:3], list(res[3:]))


def _mix_out_fwd(x1, h, gl, o, g_lru, g_attn, g_post, w_o):
    t = x1.shape[0]
    tm = _token_tile(t)

    def body(x_ref, h_ref, gl_ref, o_ref, g1_ref, g2_ref, gp_ref, w_ref, x2_ref, m_ref):
        y = h_ref[...] * _gelu(gl_ref[...])[0]
        yn1 = _rms_fwd(y, g1_ref[...]).astype(BF16)
        yn2 = _rms_fwd(o_ref[...], g2_ref[...]).astype(BF16)
        m = _dot(yn1, w_ref[0:512, :]) + _dot(yn2, w_ref[512:1024, :])
        m_ref[...] = m
        x2_ref[...] = x_ref[...] + _rms_fwd(m, gp_ref[...])

    tok = pl.BlockSpec((tm, D_MODEL), lambda i: (i, 0))
    half = pl.BlockSpec((tm, 512), lambda i: (i, 0))
    vec = pl.BlockSpec((1, D_MODEL), lambda i: (0, 0))
    hvec = pl.BlockSpec((1, 512), lambda i: (0, 0))
    return pl.pallas_call(
        body, name="mix_out_fwd", grid=(t // tm,),
        in_specs=[tok, half, half, half, hvec, hvec, vec, pl.BlockSpec((D_MODEL, D_MODEL), lambda i: (0, 0))],
        out_specs=[tok, tok],
        out_shape=[jax.ShapeDtypeStruct((t, D_MODEL), F32), jax.ShapeDtypeStruct((t, D_MODEL), F32)],
        compiler_params=_params(1),
    )(x1, h, gl, o, g_lru, g_attn, g_post, w_o)


def _mix_out_bwd(dx2, m, h, gl, o, g_lru, g_attn, g_post, w_o):
    t = dx2.shape[0]
    tm = _token_tile(t)

    def body(dx_ref, m_ref, h_ref, gl_ref, o_ref, g1_ref, g2_ref, gp_ref, w_ref,
             dy_ref, do_ref, dw_ref, dgp_ref, dg1_ref, dg2_ref):
        _zero_at_first(pl.program_id(0) == 0, dw_ref, dgp_ref, dg1_ref, dg2_ref)
        dm, dgp = _rms_bwd(m_ref[...], gp_ref[...], dx_ref[...])
        dmb = dm.astype(BF16)
        y = h_ref[...] * _gelu(gl_ref[...])[0]
        o = o_ref[...]
        yn1 = _rms_fwd(y, g1_ref[...]).astype(BF16)
        yn2 = _rms_fwd(o, g2_ref[...]).astype(BF16)
        dw_ref[0:512, :] += _dot_tn(yn1, dmb)
        dw_ref[512:1024, :] += _dot_tn(yn2, dmb)
        dy, dg1 = _rms_bwd(y, g1_ref[...], _dot_nt(dmb, w_ref[0:512, :]))
        do, dg2 = _rms_bwd(o, g2_ref[...], _dot_nt(dmb, w_ref[512:1024, :]))
        dy_ref[...] = dy
        do_ref[...] = do.astype(BF16)
        dgp_ref[...] += dgp
        dg1_ref[...] += dg1
        dg2_ref[...] += dg2

    tok = pl.BlockSpec((tm, D_MODEL), lambda i: (i, 0))
    half = pl.BlockSpec((tm, 512), lambda i: (i, 0))
    vec = pl.BlockSpec((1, D_MODEL), lambda i: (0, 0))
    hvec = pl.BlockSpec((1, 512), lambda i: (0, 0))
    mat = pl.BlockSpec((D_MODEL, D_MODEL), lambda i: (0, 0))
    return pl.pallas_call(
        body, name="mix_out_bwd", grid=(t // tm,),
        in_specs=[tok, tok, half, half, half, hvec, hvec, vec, mat],
        out_specs=[half, half, mat, vec, hvec, hvec],
        out_shape=[jax.ShapeDtypeStruct((t, 512), F32), jax.ShapeDtypeStruct((t, 512), BF16),
                   jax.ShapeDtypeStruct((D_MODEL, D_MODEL), F32), jax.ShapeDtypeStruct((1, D_MODEL), F32),
                   jax.ShapeDtypeStruct((1, 512), F32), jax.ShapeDtypeStruct((1, 512), F32)],
        compiler_params=_params(1),
    )(dx2, m, h, gl, o, g_lru, g_attn, g_post, w_o)


def _mix_in_bwd(dx2, x1, g, dxl, dgl, dq, dkv, w_in, f1, g_post1):
    t = x1.shape[0]
    tm = _token_tile(t)

    def body(dx2_ref, x_ref, g_ref, dxl_ref, dgl_ref, dq_ref, dkv_ref, w_ref, f1_ref, gp1_ref,
             dx1_ref, dw_ref, dg_ref, df1_ref, dgp1_ref):
        _zero_at_first(pl.program_id(0) == 0, dw_ref, dg_ref, dgp1_ref)
        x = x_ref[...]
        nb = _rms_fwd(x, g_ref[...]).astype(BF16)
        dproj = jnp.concatenate([dxl_ref[...].astype(BF16), dgl_ref[...].astype(BF16), dq_ref[...],
                                 dkv_ref[...].astype(BF16)], axis=1)
        dw_ref[...] += _dot_tn(nb, dproj)
        dx, dg = _rms_bwd(x, g_ref[...], _dot(dproj, w_ref[...]))
        dx1 = dx2_ref[...] + dx
        dx1_ref[...] = dx1
        dg_ref[...] += dg
        df1, dgp1 = _rms_bwd(f1_ref[...], gp1_ref[...], 0.5 * dx1)
        df1_ref[...] = df1.astype(BF16)
        dgp1_ref[...] += dgp1

    tok = pl.BlockSpec((tm, D_MODEL), lambda i: (i, 0))
    half = pl.BlockSpec((tm, 512), lambda i: (i, 0))
    vec = pl.BlockSpec((1, D_MODEL), lambda i: (0, 0))
    mat = pl.BlockSpec((D_IN, D_MODEL), lambda i: (0, 0))
    dmat = pl.BlockSpec((D_MODEL, D_IN), lambda i: (0, 0))
    quarter = pl.BlockSpec((tm, 256), lambda i: (i, 0))
    return pl.pallas_call(
        body, name="mix_in_bwd", grid=(t // tm,),
        in_specs=[tok, tok, vec, half, half, half, quarter, mat, tok, vec], out_specs=[tok, dmat, vec, tok, vec],
        out_shape=[jax.ShapeDtypeStruct((t, D_MODEL), F32), jax.ShapeDtypeStruct((D_MODEL, D_IN), F32),
                   jax.ShapeDtypeStruct((1, D_MODEL), F32), jax.ShapeDtypeStruct((t, D_MODEL), BF16),
                   jax.ShapeDtypeStruct((1, D_MODEL), F32)],
        compiler_params=_params(1),
    )(dx2, x1, g, dxl, dgl, dq, dkv, w_in, f1, g_post1)


def _half(rows):
    return rows // 2


def _chip_sums(grads, from_sibling, other, name):
    n_arr = len(grads)

    def body(other_ref, *refs):
        for a in range(n_arr):
            refs[2 * n_arr + a][0] = (refs[2 * a][0, 0] + refs[2 * a + 1][0]).astype(BF16)

    in_specs, out_specs, out_shape, args = [], [], [], []
    for g, s in zip(grads, from_sibling):
        _, rows, cols = g.shape
        tr = _half(rows)
        in_specs += [pl.BlockSpec((1, 1, tr, cols), lambda j, i, other: (other[j], other[3], i, 0)),
                     pl.BlockSpec((1, tr, cols), lambda j, i, other: (other[j], i, 0))]
        out_specs.append(pl.BlockSpec((1, tr, cols), lambda j, i, other: (j, i, 0)))
        out_shape.append(jax.ShapeDtypeStruct((3, rows, cols), BF16))
        args += [g.reshape(4, 2, rows, cols), s]
    grid_spec = pltpu.PrefetchScalarGridSpec(num_scalar_prefetch=1, grid=(3, 2), in_specs=in_specs, out_specs=out_specs)
    return pl.pallas_call(body, name=name, grid_spec=grid_spec, out_shape=out_shape, compiler_params=_params(2))(other, *args)


def _adamw(w, g, m, v):
    m = ADAM_B1 * m + (1.0 - ADAM_B1) * g
    v = ADAM_B2 * v + (1.0 - ADAM_B2) * (g * g)
    m_hat = m / (1.0 - ADAM_B1 ** ADAM_STEP)
    v_hat = v / (1.0 - ADAM_B2 ** ADAM_STEP)
    delta = -ADAM_LR * (m_hat / (jnp.sqrt(v_hat) + ADAM_EPS) + ADAM_WD * w)
    return delta, m, v


def _shard_updates(grads, from_sibling, from_chips, w, m, v, place, name, transposed):
    n_arr = len(grads)

    def total(g_ref, s_ref, c_ref):
        g = g_ref[0, 0] + s_ref[0]
        g = g + c_ref[0].astype(F32)
        g = g + c_ref[1].astype(F32)
        return g + c_ref[2].astype(F32)

    part_specs, parts, flat, shapes = [], [], [], []
    for g in grads:
        _, rows, cols = g.shape
        tr = _half(rows)
        part_specs.append([pl.BlockSpec((1, 1, tr, cols), lambda i, place: (place[0], place[1], i, 0)),
                           pl.BlockSpec((1, tr, cols), lambda i, place: (place[0], i, 0)),
                           pl.BlockSpec((3, tr, cols), lambda i, place: (0, i, 0))])
        flat.append(pl.BlockSpec((tr, cols), lambda i, place: (i, 0)))
        shapes.append(jax.ShapeDtypeStruct((rows, cols), F32))
    for g, s, c in zip(grads, from_sibling, from_chips):
        parts += [g.reshape(4, 2, *g.shape[1:]), s, c]

    if not transposed:
        def body(place_ref, *refs):
            ins, wmv, outs = refs[:3 * n_arr], refs[3 * n_arr:6 * n_arr], refs[6 * n_arr:]
            for a in range(n_arr):
                g = total(*ins[3 * a:3 * a + 3])
                outs[4 * a][...] = g
                outs[4 * a + 1][...], outs[4 * a + 2][...], outs[4 * a + 3][...] = _adamw(
                    wmv[3 * a][...], g, wmv[3 * a + 1][...], wmv[3 * a + 2][...])

        grid_spec = pltpu.PrefetchScalarGridSpec(
            num_scalar_prefetch=1, grid=(2,),
            in_specs=[sp for specs in part_specs for sp in specs] + [f for f in flat for _ in range(3)],
            out_specs=[f for f in flat for _ in range(4)])
        res = pl.pallas_call(body, name=name, grid_spec=grid_spec, out_shape=[sh for sh in shapes for _ in range(4)],
                             compiler_params=_params(1))(place, *parts, *[x for wmv in zip(w, m, v) for x in wmv])
        return [tuple(res[4 * a:4 * a + 4]) for a in range(n_arr)]

    def sum_body(place_ref, *refs):
        for a in range(n_arr):
            refs[3 * n_arr + a][...] = total(*refs[3 * a:3 * a + 3])

    grid_spec = pltpu.PrefetchScalarGridSpec(num_scalar_prefetch=1, grid=(2,),
                                             in_specs=[sp for specs in part_specs for sp in specs], out_specs=flat)
    sums = pl.pallas_call(sum_body, name=name + "_sum", grid_spec=grid_spec, out_shape=shapes,
                          compiler_params=_params(1))(place, *parts)
    turned = [jnp.transpose(g, (1, 0)) for g in sums]

    def adam_body(*refs):
        ins, outs = refs[:4 * n_arr], refs[4 * n_arr:]
        for a in range(n_arr):
            g = ins[4 * a][...]
            outs[4 * a][...] = g
            outs[4 * a + 1][...], outs[4 * a + 2][...], outs[4 * a + 3][...] = _adamw(
                ins[4 * a + 1][...], g, ins[4 * a + 2][...], ins[4 * a + 3][...])

    blks = [pl.BlockSpec((g.shape[0] // 4, g.shape[1]), lambda i: (i, 0)) for g in turned]
    res = pl.pallas_call(
        adam_body, name=name + "_adam", grid=(4,), in_specs=[b for b in blks for _ in range(4)],
        out_specs=[b for b in blks for _ in range(4)],
        out_shape=[jax.ShapeDtypeStruct(g.shape, F32) for g in turned for _ in range(4)], compiler_params=_params(1),
    )(*[x for gwmv in zip(turned, w, m, v) for x in gwmv])
    return [tuple(res[4 * a:4 * a + 4]) for a in range(n_arr)]


GAINS = ("ffn1_pre_g", "ffn1_post_g", "mix_pre_g", "mix_post_g", "ffn2_pre_g", "ffn2_post_g")
HALVES = ("conv_b", "b_rg", "b_ig", "lru_lambda", "g_lru_out", "g_attn_out")
GATES = ("w_rg", "w_ig")
SMALL = GAINS + HALVES + GATES + ("sinks", "conv_w")


def _small_update(gathered, w, m, v):
    n_small = len(SMALL)

    def body(*refs):
        ga_ref, gb_ref, gc_ref, gd_ref, g0_ref, gconv_ref = refs[:6]
        wmv = refs[6:6 + 3 * n_small]
        outs = refs[6 + 3 * n_small:6 + 7 * n_small]
        loss_ref = refs[6 + 7 * n_small]

        def total(ref):
            s = ref[0]
            for d in range(1, N_DEV):
                s = s + ref[d]
            return s

        sa, sb, sc, sd = total(ga_ref), total(gb_ref), total(gc_ref), total(gd_ref)
        grads = {}
        for i, k in enumerate(GAINS):
            grads[k] = sa[i:i + 1]
        grads[GAINS[0]] = total(g0_ref)
        for i, k in enumerate(HALVES):
            grads[k] = sb[i:i + 1]
        grads["w_rg"], grads["w_ig"] = sc[0:512], sc[512:1024]
        grads["sinks"] = sd[4:5, 0:8]
        grads["conv_w"] = total(gconv_ref)
        for i, k in enumerate(SMALL):
            g = grads[k]
            outs[4 * i][...] = g
            outs[4 * i + 1][...], outs[4 * i + 2][...], outs[4 * i + 3][...] = _adamw(
                wmv[3 * i][...], g, wmv[3 * i + 1][...], wmv[3 * i + 2][...])
        loss_ref[...] = jnp.broadcast_to(sd[5:6, 0:128], loss_ref.shape)

    operands = list(gathered)
    out_shape = []
    for k in SMALL:
        operands += [w[k], m[k], v[k]]
        out_shape += [jax.ShapeDtypeStruct(w[k].shape, F32)] * 4
    out_shape.append(jax.ShapeDtypeStruct((8, 128), F32))
    res = pl.pallas_call(body, name="small_update", out_shape=out_shape, compiler_params=_params())(*operands)
    parts = [{k: res[4 * i + j] for i, k in enumerate(SMALL)} for j in range(4)]
    return (*parts, res[-1])


def _reorder_heads(a, axis, start, order):
    def slab(h):
        return lax.slice_in_dim(a, start + HEAD_DIM * h, start + HEAD_DIM * (h + 1), axis=axis)

    parts = [lax.slice_in_dim(a, 0, start, axis=axis)] + [slab(h) for h in order]
    parts.append(lax.slice_in_dim(a, start + 8 * HEAD_DIM, a.shape[axis], axis=axis))
    return jnp.concatenate(parts, axis=axis)


HEAD_ORDER_INVERSE = tuple(HEAD_ORDER.index(h) for h in range(8))


def _pair_block_diag(w):
    w = w.reshape(N_LRU_GROUP, 2, 64, 64)
    z = jnp.zeros((N_LRU_GROUP, 64, 64), w.dtype)
    top = jnp.concatenate([w[:, 0], z], axis=2)
    bot = jnp.concatenate([z, w[:, 1]], axis=2)
    return jnp.concatenate([top, bot], axis=1)


def _pair_block_diag_grad(dw2):
    return jnp.stack([dw2[:, :64, :64], dw2[:, 64:, 64:]], axis=1).reshape(512, 64)


def kernel(x, ffn1_pre_g, ffn1_w_gu, ffn1_w_down, ffn1_post_g, mix_pre_g, w_in, conv_w, conv_b, w_rg, b_rg, w_ig, b_ig, lru_lambda, sinks, g_lru_out, g_attn_out, w_o, mix_post_g, ffn2_pre_g, ffn2_w_gu, ffn2_w_down, ffn2_post_g, loss_target, m_ffn1_pre_g, m_ffn1_w_gu, m_ffn1_w_down, m_ffn1_post_g, m_mix_pre_g, m_w_in, m_conv_w, m_conv_b, m_w_rg, m_b_rg, m_w_ig, m_b_ig, m_lru_lambda, m_sinks, m_g_lru_out, m_g_attn_out, m_w_o, m_mix_post_g, m_ffn2_pre_g, m_ffn2_w_gu, m_ffn2_w_down, m_ffn2_post_g, v_ffn1_pre_g, v_ffn1_w_gu, v_ffn1_w_down, v_ffn1_post_g, v_mix_pre_g, v_w_in, v_conv_w, v_conv_b, v_w_rg, v_b_rg, v_w_ig, v_b_ig, v_lru_lambda, v_sinks, v_g_lru_out, v_g_attn_out, v_w_o, v_mix_post_g, v_ffn2_pre_g, v_ffn2_w_gu, v_ffn2_w_down, v_ffn2_post_g):
    args = dict(locals())
    names = ["ffn1_pre_g", "ffn1_w_gu", "ffn1_w_down", "ffn1_post_g", "mix_pre_g", "w_in", "conv_w", "conv_b", "w_rg",
             "b_rg", "w_ig", "b_ig", "lru_lambda", "sinks", "g_lru_out", "g_attn_out", "w_o", "mix_post_g",
             "ffn2_pre_g", "ffn2_w_gu", "ffn2_w_down", "ffn2_post_g"]
    big = ["ffn1_w_gu", "ffn1_w_down", "w_in", "w_o", "ffn2_w_gu", "ffn2_w_down"]
    w = {k: args[k] for k in names}
    mom = {k: args["m_" + k] for k in names}
    var = {k: args["v_" + k] for k in names}
    t = x.shape[1]
    xs = x.reshape(t, D_MODEL)
    target = loss_target.reshape(t, D_MODEL)
    cx, cy, cc = _coords()
    me = 4 * cx + 2 * cy + cc
    other = jnp.stack([2 * (1 - cx) + cy, 2 * cx + (1 - cy), 2 * (1 - cx) + (1 - cy), cc]).astype(jnp.int32)
    place = jnp.stack([2 * cx + cy, cc]).astype(jnp.int32)

    transposed = ("ffn1_w_gu", "w_in", "ffn2_w_gu")

    def shard_view(a, k):
        return jnp.transpose(a[0], (1, 0)) if k in transposed else a[0]

    def shard_unview(a, k):
        return (jnp.transpose(a, (1, 0)) if k in transposed else a)[None]

    shard2d = {k: shard_view(w[k], k) for k in big}
    shard_bf = {k: shard2d[k].astype(BF16) for k in big}
    conv_pad = jnp.pad(conv_w.reshape(4, 64), ((0, 4), (0, 64)))
    (first_w,) = _run_exchanges([_Gather([shard_bf["ffn1_w_gu"], shard_bf["ffn1_w_down"]], routed=True)], "all_gather_ffn1")
    wgu1 = first_w[0].reshape(2, N_CHUNK, CHUNK, D_MODEL)
    wd1 = first_w[1].reshape(N_CHUNK, CHUNK, D_MODEL)
    rest = _Gather([shard_bf["w_in"], shard_bf["w_o"], shard_bf["ffn2_w_gu"], shard_bf["ffn2_w_down"], conv_pad])

    x1, f1, n1, gu1, gathered = _ffn_fwd(xs, ffn1_pre_g, wgu1, wd1, ffn1_post_g, None, "ffn1_fwd", rest)
    w_in_full = _reorder_heads(gathered[0].reshape(D_IN, D_MODEL), 0, 2 * D_LRU, HEAD_ORDER)
    w_o_full = _reorder_heads(gathered[1].reshape(D_MODEL, D_MODEL), 0, D_LRU, HEAD_ORDER)
    g_attn_heads = _reorder_heads(g_attn_out, 1, 0, HEAD_ORDER)
    wgu2 = gathered[2].reshape(2, N_CHUNK, CHUNK, D_MODEL)
    wd2 = gathered[3].reshape(N_CHUNK, CHUNK, D_MODEL)
    conv_w_full = jnp.transpose(gathered[4][:, 0:4, 0:64], (1, 0, 2)).reshape(4, D_LRU)
    p_lru = jnp.concatenate([conv_b, b_rg, b_ig, lru_lambda, conv_w_full], axis=0)
    wrg2 = _pair_block_diag(w_rg[0]).astype(BF16)
    wig2 = _pair_block_diag(w_ig[0]).astype(BF16)
    xl, gl, q, kv = _mix_in_fwd(x1, mix_pre_g, w_in_full)
    h = _lru_fwd(xl, p_lru, wrg2, wig2)
    o = _attn_fwd(q, kv, sinks)
    x2, mo = _mix_out_fwd(x1, h, gl, o, g_lru_out, g_attn_heads, mix_post_g, w_o_full)
    g = {}
    dx3, n2, df2, gu2, g["ffn2_post_g"], loss_parts, _ = _ffn_fwd(x2, ffn2_pre_g, wgu2, wd2, ffn2_post_g, target, "ffn2_fwd")
    loss_local = jnp.sum(loss_parts[::8, 0])

    partial, from_sibling, from_chips = {}, {}, {}

    def chip_sums(keys):
        return _chip_sums([partial[k] for k in keys], [from_sibling[k] for k in keys], other, "chip_sum_" + keys[0])

    dgu2, dwgu2, dwd2, _ = _ffn_bwd_w(n2, df2, gu2, wd2, "ffn2_bwd_w")
    partial["ffn2_w_gu"] = dwgu2.reshape(N_DEV, D_MODEL, CHUNK)
    partial["ffn2_w_down"] = dwd2.reshape(N_DEV, D_FF // N_DEV, D_MODEL)
    ffn2_keys = ["ffn2_w_gu", "ffn2_w_down"]
    n_tiles = t // _token_tile(t)
    zero_gain = jnp.zeros((1, D_MODEL), F32)
    dx2, g["ffn2_pre_g"], got = _ffn_bwd_x(dgu2, wgu2, x2, ffn2_pre_g, dx3, zero_gain, (0, n_tiles), "ffn2_bwd_x",
                                           _SiblingExchange([partial[k] for k in ffn2_keys]))
    from_sibling.update(zip(ffn2_keys, got))
    dy, do, dwo, g["mix_post_g"], g["g_lru_out"], dg_attn_heads = _mix_out_bwd(
        dx2, mo, h, gl, o, g_lru_out, g_attn_heads, mix_post_g, w_o_full)
    g["g_attn_out"] = _reorder_heads(dg_attn_heads, 1, 0, HEAD_ORDER_INVERSE)
    dwo = _reorder_heads(dwo, 0, D_LRU, HEAD_ORDER_INVERSE)
    dq, dkv, dsink, got = _attn_bwd(q, kv, do, sinks, _ChipExchange(chip_sums(ffn2_keys)))
    from_chips.update(zip(ffn2_keys, got))
    dxl, dgl, dp, dwrg2, dwig2 = _lru_bwd(dy, h, xl, gl, p_lru, wrg2, wig2)
    dx1, dwin, g["mix_pre_g"], df1, g["ffn1_post_g"] = _mix_in_bwd(
        dx2, x1, mix_pre_g, dxl, dgl, dq, dkv, w_in_full, f1, ffn1_post_g)
    dwin = _reorder_heads(dwin, 1, 2 * D_LRU, HEAD_ORDER_INVERSE)
    partial["w_in"] = jnp.transpose(dwin.reshape(D_MODEL, N_DEV, D_IN // N_DEV), (1, 0, 2))
    partial["w_o"] = dwo.reshape(N_DEV, D_MODEL // N_DEV, D_MODEL)
    mix_keys = ["w_in", "w_o"]
    dgu1, dwgu1, dwd1, got = _ffn_bwd_w(n1, df1, gu1, wd1, "ffn1_bwd_w", _SiblingExchange([partial[k] for k in mix_keys]))
    from_sibling.update(zip(mix_keys, got))
    partial["ffn1_w_gu"] = dwgu1.reshape(N_DEV, D_MODEL, CHUNK)
    partial["ffn1_w_down"] = dwd1.reshape(N_DEV, D_FF // N_DEV, D_MODEL)
    ffn1_keys = ["ffn1_w_gu", "ffn1_w_down"]
    dx0_head, g_first, got = _ffn_bwd_x(
        dgu1, wgu1, xs, ffn1_pre_g, dx1, zero_gain, (0, n_tiles // 2), "ffn1_bwd_x_head",
        _Pair(_SiblingExchange([partial[k] for k in ffn1_keys]), _ChipExchange(chip_sums(mix_keys))))
    from_sibling.update(zip(ffn1_keys, got[:2]))
    from_chips.update(zip(mix_keys, got[2:]))
    zeros2 = jnp.zeros((2, D_MODEL), F32)
    g_gains = jnp.concatenate([zeros2[:1]] + [g[k] for k in GAINS[1:]] + [zeros2], axis=0)
    g_halves = jnp.concatenate([dp[0:4], g["g_lru_out"], g["g_attn_out"], zeros2[:, :D_LRU]], axis=0)
    g_gates = jnp.concatenate([_pair_block_diag_grad(dwrg2), _pair_block_diag_grad(dwig2)], axis=0)
    g_misc = jnp.concatenate([dp[4:8], jnp.pad(dsink[:, 0].reshape(1, 8), ((0, 0), (0, D_LRU - 8))),
                              jnp.pad(loss_local.reshape(1, 1), ((0, 0), (0, D_LRU - 1))), zeros2[:, :D_LRU]], axis=0)
    dx0_tail, g_first, got = _ffn_bwd_x(
        dgu1, wgu1, xs, ffn1_pre_g, dx1, g_first, (n_tiles // 2, n_tiles - n_tiles // 2), "ffn1_bwd_x_tail",
        _Both(_ChipExchange(chip_sums(ffn1_keys)), _Gather([g_gains, g_halves, g_gates, g_misc])))
    dx0 = jnp.concatenate([dx0_head, dx0_tail], axis=0)
    from_chips.update(zip(ffn1_keys, got[:2]))
    gathered_small = got[2:]

    grads, delta, new_m, new_v = {}, {}, {}, {}
    for name, keys, turned in (("update_column_sharded", transposed, True),
                               ("update_row_sharded", tuple(k for k in big if k not in transposed), False)):
        res = _shard_updates([partial[k] for k in keys], [from_sibling[k] for k in keys], [from_chips[k] for k in keys],
                             [shard2d[k] for k in keys], [shard_view(mom[k], k) for k in keys],
                             [shard_view(var[k], k) for k in keys], place, name, turned)
        for k, out in zip(keys, res):
            grads[k], delta[k], new_m[k], new_v[k] = [shard_unview(r, k) for r in out]

    ((gathered_first,),) = _run_exchanges([_Gather([g_first])], "all_gather_first_gain")
    conv_parts = lax.dynamic_slice(gathered_small[3], (0, 0, me * 64), (N_DEV, 4, 64))

    def small_view(vals):
        out = {k: vals[k] for k in GAINS + HALVES + ("sinks",)}
        out.update({k: vals[k].reshape(512, 64) for k in GATES})
        out["conv_w"] = vals["conv_w"].reshape(4, 64)
        return out

    *small, loss_tile = _small_update([*gathered_small, gathered_first, conv_parts], small_view(w), small_view(mom),
                                      small_view(var))
    for dst, part in zip((grads, delta, new_m, new_v), small):
        for k in SMALL:
            dst[k] = part[k].reshape(w[k].shape)
    return (loss_tile[0, 0], dx0.reshape(x.shape), *[grads[k] for k in names], *[delta[k] for k in names],
            *[new_m[k] for k in names], *[new_v[k] for k in names])
```

```python
import functools

import jax
import jax.numpy as jnp
from jax import lax
from jax.experimental import pallas as pl
from jax.experimental.pallas import tpu as pltpu

F32 = jnp.float32
BF16 = jnp.bfloat16

D_MODEL = 1024
D_FF = 2816
N_DEV = 8
N_CHUNK = 4
CHUNK = D_FF // N_CHUNK
D_LRU = 512
D_ATTN = 512
LRU_GROUP = 128
N_LRU_GROUP = D_LRU // LRU_GROUP
LRU_GROUPS_PER_STEP = 4
HEAD_DIM = 64
BLOCK_Q = 128
D_IN = 1792
HEAD_ORDER = (0, 4, 1, 5, 2, 6, 3, 7)
ATTN_UNROLL = 9
RMS_EPS = 1e-6
LRU_C = 8.0
MASK_VALUE = -1e30
ATTN_SCALE = HEAD_DIM ** -0.5

ADAM_LR = 0.001
ADAM_B1 = 0.9
ADAM_B2 = 0.999
ADAM_EPS = 1e-08
ADAM_WD = 0.01
ADAM_STEP = 10

VMEM_LIMIT_V7X = 56 * 2 ** 20

ANY = pl.BlockSpec(memory_space=pl.ANY)
SMEM = pl.BlockSpec(memory_space=pltpu.SMEM)
MESH = pl.DeviceIdType.MESH


def _params(n_grid=0):
    sem = ("arbitrary",) * n_grid if n_grid else None
    return pltpu.CompilerParams(dimension_semantics=sem, vmem_limit_bytes=VMEM_LIMIT_V7X)


def _dot(a, b):
    return lax.dot_general(a, b, (((1,), (0,)), ((), ())), preferred_element_type=F32)


def _dot_nt(a, b):
    return lax.dot_general(a, b, (((1,), (1,)), ((), ())), preferred_element_type=F32)


def _dot_tn(a, b):
    return lax.dot_general(a, b, (((0,), (0,)), ((), ())), preferred_element_type=F32)


def _sigmoid(x):
    return 1.0 / (1.0 + jnp.exp(-x))


def _rms_fwd(x, g):
    r = lax.rsqrt(jnp.mean(x * x, axis=-1, keepdims=True) + RMS_EPS)
    return x * r * g


def _rms_bwd(x, g, dy):
    r = lax.rsqrt(jnp.mean(x * x, axis=-1, keepdims=True) + RMS_EPS)
    xh = x * r
    dg = jnp.sum(dy * xh, axis=0, keepdims=True)
    dxh = dy * g
    dx = r * (dxh - xh * jnp.mean(dxh * xh, axis=-1, keepdims=True))
    return dx, dg


def _gelu(x):
    c = 0.7978845608028654
    inner = c * (x + 0.044715 * x * x * x)
    th = jnp.tanh(inner)
    ge = 0.5 * x * (1.0 + th)
    dge = 0.5 * (1.0 + th) + 0.5 * x * (1.0 - th * th) * c * (1.0 + 3.0 * 0.044715 * x * x)
    return ge, dge


def _zero_at_first(first, *refs):
    @pl.when(first)
    def _():
        for ref in refs:
            ref[...] = jnp.zeros_like(ref)


def _token_tile(t):
    return 512 if t >= 2048 else t // 2


def _ffn_bwd_tile(t):
    return 1024 if t >= 4096 else t // 2


def _coords():
    return lax.axis_index("x"), lax.axis_index("y"), lax.axis_index("c")


class _Gather:
    n_phases = 3
    at = (0.0, 0.8, 1.0)

    def __init__(self, shards, routed=False):
        k = len(shards)
        self.routed = routed
        self.arrays = list(shards)
        self.out_shape = [jax.ShapeDtypeStruct((N_DEV,) + s.shape, s.dtype) for s in shards]
        self.scratch = [pltpu.SemaphoreType.DMA((7 * k,)), pltpu.SemaphoreType.DMA((7 * k,)), pltpu.SemaphoreType.DMA((k,))]

    def run(self, phase, ins, outs, sems):
        send_sems, recv_sems, local_sems = sems
        k_arr = len(ins)
        x, y, c = _coords()
        me, sibling = (x, y, c), (x, y, 1 - c)
        chips = [(1 - x, y), (x, 1 - y), (1 - x, 1 - y)]
        direct = 2 if self.routed else 3
        relay_from = (x + (1 - c) * (1 - 2 * x), y + c * (1 - 2 * y))
        relay_to = (x + c * (1 - 2 * x), y + (1 - c) * (1 - 2 * y))

        def rows(k, dev):
            return outs[k].at[4 * dev[0] + 2 * dev[1] + dev[2]]

        def copy(k, slot, block, to, src=None):
            return pltpu.make_async_remote_copy(
                src_ref=rows(k, block) if src is None else src, dst_ref=rows(k, block),
                send_sem=send_sems.at[7 * k + slot], recv_sem=recv_sems.at[7 * k + slot],
                device_id=to, device_id_type=MESH)

        def mine():
            return [pltpu.make_async_copy(ins[k], rows(k, me), local_sems.at[k]) for k in range(k_arr)]

        def first():
            return [copy(k, slot, me, to, src=ins[k]) for k in range(k_arr)
                    for slot, to in enumerate([sibling] + [(*chip, c) for chip in chips[:direct]])]

        def relayed(k):
            return copy(k, 3, (*relay_from, c), (*relay_to, c))

        def passed(j, k):
            return copy(k, 4 + j, (*chips[j], c), sibling)

        if phase == 0:
            for cp in mine() + first():
                cp.start()
        elif phase == 1:
            if self.routed:
                for k in range(k_arr):
                    copy(k, 1 + c, (*relay_from, c), me).wait_recv()
                    relayed(k).start()
                for k in range(k_arr):
                    copy(k, 2 - c, (*relay_to, c), me).wait_recv()
            else:
                for j in range(direct):
                    for k in range(k_arr):
                        copy(k, 1 + j, (*chips[j], c), me).wait_recv()
            for k in range(k_arr):
                for j in range(direct):
                    passed(j, k).start()
        else:
            for k in range(k_arr):
                if self.routed:
                    copy(k, 3, (*chips[2], c), me).wait_recv()
                    passed(2, k).start()
            for k in range(k_arr):
                copy(k, 0, sibling, me).wait_recv()
                for j, chip in enumerate(chips):
                    copy(k, 4 + j, (*chip, 1 - c), me).wait_recv()
            sent = first() + [passed(j, k) for j in range(3) for k in range(k_arr)]
            if self.routed:
                sent += [relayed(k) for k in range(k_arr)]
            for cp in sent:
                cp.wait_send()
            for cp in mine():
                cp.wait()


class _SiblingExchange:
    n_phases = 2
    at = (0.0, 1.0)

    def __init__(self, grads):
        k = len(grads)
        self.arrays = list(grads)
        self.out_shape = [jax.ShapeDtypeStruct((4,) + g.shape[1:], g.dtype) for g in grads]
        self.scratch = [pltpu.SemaphoreType.DMA((4 * k,)), pltpu.SemaphoreType.DMA((4 * k,))]

    def run(self, phase, ins, outs, sems):
        send_sems, recv_sems = sems
        x, y, c = _coords()
        copies = [pltpu.make_async_remote_copy(
            src_ref=ins[k].at[2 * q + (1 - c)], dst_ref=outs[k].at[q],
            send_sem=send_sems.at[4 * k + q], recv_sem=recv_sems.at[4 * k + q],
            device_id=(x, y, 1 - c), device_id_type=MESH) for k in range(len(ins)) for q in range(4)]
        for cp in copies:
            if phase == 0:
                cp.start()
            else:
                cp.wait_recv()
                cp.wait_send()


class _ChipExchange:
    n_phases = 2
    at = (0.0, 1.0)

    def __init__(self, chip_sums):
        k = len(chip_sums)
        self.arrays = list(chip_sums)
        self.out_shape = [jax.ShapeDtypeStruct((3,) + s.shape[1:], s.dtype) for s in chip_sums]
        self.scratch = [pltpu.SemaphoreType.DMA((3 * k,)), pltpu.SemaphoreType.DMA((3 * k,))]

    def run(self, phase, ins, outs, sems):
        send_sems, recv_sems = sems
        x, y, c = _coords()
        chips = [(1 - x, y), (x, 1 - y), (1 - x, 1 - y)]
        copies = [pltpu.make_async_remote_copy(
            src_ref=ins[k].at[j], dst_ref=outs[k].at[j],
            send_sem=send_sems.at[3 * k + j], recv_sem=recv_sems.at[3 * k + j],
            device_id=(*chip, c), device_id_type=MESH) for k in range(len(ins)) for j, chip in enumerate(chips)]
        for cp in copies:
            if phase == 0:
                cp.start()
            else:
                cp.wait_recv()
                cp.wait_send()


class _Both:
    n_phases = 3
    at = (0.0, 0.95, 1.0)

    def __init__(self, two_phase, gather):
        self.parts = (two_phase, gather)
        self.arrays = two_phase.arrays + gather.arrays
        self.out_shape = two_phase.out_shape + gather.out_shape
        self.scratch = two_phase.scratch + gather.scratch

    def run(self, phase, ins, outs, sems):
        a, b = self.parts
        n_in, n_out, n_sem = len(a.arrays), len(a.out_shape), len(a.scratch)
        refs_a = (ins[:n_in], outs[:n_out], sems[:n_sem])
        refs_b = (ins[n_in:], outs[n_out:], sems[n_sem:])
        b.run(phase, *refs_b)
        if phase == 0:
            a.run(0, *refs_a)
        if phase == 2:
            a.run(1, *refs_a)


class _Host:
    def __init__(self, exchange):
        self.ex = exchange
        self.args = [] if exchange is None else exchange.arrays
        self.in_specs = [ANY] * len(self.args)
        self.out_shape = [] if exchange is None else exchange.out_shape
        self.out_specs = [ANY] * len(self.out_shape)
        self.scratch = [] if exchange is None else exchange.scratch

    def split(self, refs, n_in, n_out, n_scratch):
        a, b, s = len(self.args), len(self.out_shape), len(self.scratch)
        own_in, ex_in = refs[:n_in], refs[n_in:n_in + a]
        rest = refs[n_in + a:]
        own_out, ex_out = rest[:n_out], rest[n_out:n_out + b]
        rest = rest[n_out + b:]
        own_scratch, ex_sems = rest[:n_scratch], rest[n_scratch:n_scratch + s]
        return list(own_in) + list(own_out) + list(own_scratch), (ex_in, ex_out, ex_sems)

    def at_steps(self, step, n_steps, ex_refs):
        if self.ex is None:
            return
        for p in range(self.ex.n_phases):
            pl.when(step == int(round(self.ex.at[p] * (n_steps - 1))))(functools.partial(self.ex.run, p, *ex_refs))

    def phase(self, p, ex_refs):
        if self.ex is not None:
            self.ex.run(p, *ex_refs)


def _run_exchanges(exchanges, name):
    hosts = [_Host(ex) for ex in exchanges]
    n_in = [len(h.args) for h in hosts]
    n_out = [len(h.out_shape) for h in hosts]
    n_sc = [len(h.scratch) for h in hosts]

    def body(*refs):
        ins, outs, scr = refs[:sum(n_in)], refs[sum(n_in):sum(n_in) + sum(n_out)], refs[sum(n_in) + sum(n_out):]
        parts = []
        for e in range(len(hosts)):
            parts.append((ins[sum(n_in[:e]):sum(n_in[:e + 1])], outs[sum(n_out[:e]):sum(n_out[:e + 1])],
                          scr[sum(n_sc[:e]):sum(n_sc[:e + 1])]))
        for h, part in zip(hosts, parts):
            h.phase(0, part)
        for h, part in zip(hosts, parts):
            for p in range(1, h.ex.n_phases):
                h.phase(p, part)

    res = pl.pallas_call(
        body, name=name, in_specs=[ANY] * sum(n_in), out_specs=[ANY] * sum(n_out),
        out_shape=[s for h in hosts for s in h.out_shape], scratch_shapes=[s for h in hosts for s in h.scratch],
    )(*[a for h in hosts for a in h.args])
    return [res[sum(n_out[:e]):sum(n_out[:e + 1])] for e in range(len(hosts))]


def _ffn_fwd(x, g_pre, wgu, wd, g_post, target, name, exchange=None):
    t = x.shape[0]
    tm = _token_tile(t)
    n_i = t // tm
    with_loss = target is not None
    host = _Host(exchange)
    n_in, n_out = (6, 6) if with_loss else (5, 4)

    def body(*refs):
        own, ex_refs = host.split(refs, n_in, n_out, 0)
        if with_loss:
            x_ref, gpre_ref, wgu_ref, wd_ref, gpost_ref, tgt_ref, xo_ref, n_ref, df_ref, gu_ref, dgpost_ref, loss_ref = own
            _zero_at_first(pl.program_id(0) == 0, dgpost_ref)
        else:
            x_ref, gpre_ref, wgu_ref, wd_ref, gpost_ref, xo_ref, f_ref, n_ref, gu_ref = own
        host.at_steps(pl.program_id(0), n_i, ex_refs)
        x = x_ref[...]
        n = _rms_fwd(x, gpre_ref[...]).astype(BF16)
        n_ref[...] = n
        f = None
        for j in range(N_CHUNK):
            gate = _dot_nt(n, wgu_ref[0, j])
            up = _dot_nt(n, wgu_ref[1, j])
            gu_ref[0, j] = gate.astype(BF16)
            gu_ref[1, j] = up.astype(BF16)
            part = _dot((gate * _sigmoid(gate) * up).astype(BF16), wd_ref[j])
            f = part if f is None else f + part
        xo = x + 0.5 * _rms_fwd(f, gpost_ref[...])
        if with_loss:
            err = xo - tgt_ref[...]
            d_out = err * (1.0 / D_MODEL)
            xo_ref[...] = d_out
            df, dg = _rms_bwd(f, gpost_ref[...], 0.5 * d_out)
            df_ref[...] = df.astype(BF16)
            dgpost_ref[...] += dg
            part = 0.5 * jnp.sum(jnp.sum(err * err, axis=-1, keepdims=True) * (1.0 / D_MODEL), axis=0, keepdims=True)
            loss_ref[...] = jnp.broadcast_to(part, loss_ref.shape)
        else:
            f_ref[...] = f
            xo_ref[...] = xo

    tok = pl.BlockSpec((tm, D_MODEL), lambda i: (i, 0))
    vec = pl.BlockSpec((1, D_MODEL), lambda i: (0, 0))
    act = pl.BlockSpec((2, N_CHUNK, tm, CHUNK), lambda i: (0, 0, i, 0))
    tok_f32 = jax.ShapeDtypeStruct((t, D_MODEL), F32)
    tok_bf16 = jax.ShapeDtypeStruct((t, D_MODEL), BF16)
    act_shape = jax.ShapeDtypeStruct((2, N_CHUNK, t, CHUNK), BF16)
    in_specs = [tok, vec,
                pl.BlockSpec((2, N_CHUNK, CHUNK, D_MODEL), lambda i: (0, 0, 0, 0), pipeline_mode=pl.Buffered(1)),
                pl.BlockSpec((N_CHUNK, CHUNK, D_MODEL), lambda i: (0, 0, 0), pipeline_mode=pl.Buffered(1)),
                vec]
    args = [x, g_pre, wgu, wd, g_post]
    if with_loss:
        in_specs.append(tok)
        args.append(target)
        out_shape = [tok_f32, tok_bf16, tok_bf16, act_shape, jax.ShapeDtypeStruct((1, D_MODEL), F32),
                     jax.ShapeDtypeStruct((n_i * 8, 128), F32)]
        out_specs = [tok, tok, tok, act, vec, pl.BlockSpec((8, 128), lambda i: (i, 0))]
    else:
        out_shape = [tok_f32, tok_f32, tok_bf16, act_shape]
        out_specs = [tok, tok, tok, act]
    res = pl.pallas_call(
        body, name=name, grid=(n_i,), in_specs=in_specs + host.in_specs, out_specs=out_specs + host.out_specs,
        out_shape=out_shape + host.out_shape, scratch_shapes=host.scratch, compiler_params=_params(1),
    )(*args, *host.args)
    return (*res[:n_out], list(res[n_out:]))


def _ffn_bwd_w(n, df, gu, wd, name, exchange=None):
    t = n.shape[0]
    tm = _ffn_bwd_tile(t)
    n_i = t // tm
    host = _Host(exchange)

    def body(*refs):
        (n_ref, df_ref, gu_ref, wd_ref, dgu_ref, dwgu_ref, dwd_ref), ex_refs = host.split(refs, 4, 3, 0)
        i = pl.program_id(1)
        host.at_steps(pl.program_id(0) * n_i + i, N_CHUNK * n_i, ex_refs)
        _zero_at_first(i == 0, dwgu_ref, dwd_ref)
        nb = n_ref[...]
        dfb = df_ref[...]
        gate = gu_ref[0, 0].astype(F32)
        up = gu_ref[1, 0].astype(F32)
        s = _sigmoid(gate)
        silu = gate * s
        a = (silu * up).astype(BF16)
        da = _dot_nt(dfb, wd_ref[0])
        dup = (da * silu).astype(BF16)
        dgate = (da * up * (s * (1.0 + gate * (1.0 - s)))).astype(BF16)
        dgu_ref[0, 0] = dgate
        dgu_ref[1, 0] = dup
        dwgu_ref[0, 0] += _dot_tn(nb, dgate)
        dwgu_ref[1, 0] += _dot_tn(nb, dup)
        dwd_ref[0] += _dot_tn(a, dfb)

    tok = pl.BlockSpec((tm, D_MODEL), lambda j, i: (i, 0))
    act = pl.BlockSpec((2, 1, tm, CHUNK), lambda j, i: (0, j, i, 0))
    wgu_spec = pl.BlockSpec((2, 1, D_MODEL, CHUNK), lambda j, i: (0, j, 0, 0))
    wd_spec = pl.BlockSpec((1, CHUNK, D_MODEL), lambda j, i: (j, 0, 0))
    res = pl.pallas_call(
        body, name=name, grid=(N_CHUNK, n_i),
        in_specs=[tok, tok, act, wd_spec] + host.in_specs,
        out_specs=[act, wgu_spec, wd_spec] + host.out_specs,
        out_shape=[jax.ShapeDtypeStruct((2, N_CHUNK, t, CHUNK), BF16),
                   jax.ShapeDtypeStruct((2, N_CHUNK, D_MODEL, CHUNK), F32),
                   jax.ShapeDtypeStruct((N_CHUNK, CHUNK, D_MODEL), F32)] + host.out_shape,
        scratch_shapes=host.scratch, compiler_params=_params(2),
    )(n, df, gu, wd, *host.args)
    return (*res[:3], list(res[3:]))


def _ffn_bwd_x(dgu, wgu, x, g_pre, d_out, name, exchange=None):
    t = x.shape[0]
    tm = _token_tile(t)
    n_i = t // tm
    host = _Host(exchange)

    def body(*refs):
        (dgu_ref, wgu_ref, x_ref, gpre_ref, do_ref, dx_ref, dgpre_ref), ex_refs = host.split(refs, 5, 2, 0)
        i = pl.program_id(0)
        host.at_steps(i, n_i, ex_refs)
        _zero_at_first(i == 0, dgpre_ref)
        dn = _dot(dgu_ref[0, 0], wgu_ref[0, 0]) + _dot(dgu_ref[1, 0], wgu_ref[1, 0])
        for j in range(1, N_CHUNK):
            dn = dn + _dot(dgu_ref[0, j], wgu_ref[0, j]) + _dot(dgu_ref[1, j], wgu_ref[1, j])
        dx, dg = _rms_bwd(x_ref[...], gpre_ref[...], dn)
        dx_ref[...] = do_ref[...] + dx
        dgpre_ref[...] += dg

    tok = pl.BlockSpec((tm, D_MODEL), lambda i: (i, 0))
    vec = pl.BlockSpec((1, D_MODEL), lambda i: (0, 0))
    res = pl.pallas_call(
        body, name=name, grid=(n_i,),
        in_specs=[pl.BlockSpec((2, N_CHUNK, tm, CHUNK), lambda i: (0, 0, i, 0)),
                  pl.BlockSpec((2, N_CHUNK, CHUNK, D_MODEL), lambda i: (0, 0, 0, 0), pipeline_mode=pl.Buffered(1)),
                  tok, vec, tok] + host.in_specs,
        out_specs=[tok, vec] + host.out_specs,
        out_shape=[jax.ShapeDtypeStruct((t, D_MODEL), F32), jax.ShapeDtypeStruct((1, D_MODEL), F32)] + host.out_shape,
        scratch_shapes=host.scratch, compiler_params=_params(1),
    )(dgu, wgu, x, g_pre, d_out, *host.args)
    return (*res[:2], list(res[2:]))


def _mix_in_fwd(x1, g, w_in):
    t = x1.shape[0]
    tm = _token_tile(t)

    def body(x_ref, g_ref, w_ref, xl_ref, gl_ref, q_ref, kv_ref):
        n = _rms_fwd(x_ref[...], g_ref[...]).astype(BF16)
        proj = _dot_nt(n, w_ref[...])
        xl_ref[...] = proj[:, 0:512]
        gl_ref[...] = proj[:, 512:1024]
        q_ref[...] = proj[:, 1024:1536].astype(BF16)
        kv_ref[...] = proj[:, 1536:1792].astype(BF16)

    tok = pl.BlockSpec((tm, D_MODEL), lambda i: (i, 0))
    half = pl.BlockSpec((tm, 512), lambda i: (i, 0))
    return pl.pallas_call(
        body, name="mix_in_fwd", grid=(t // tm,),
        in_specs=[tok, pl.BlockSpec((1, D_MODEL), lambda i: (0, 0)), pl.BlockSpec((D_IN, D_MODEL), lambda i: (0, 0))],
        out_specs=[half, half, half, pl.BlockSpec((tm, 256), lambda i: (i, 0))],
        out_shape=[jax.ShapeDtypeStruct((t, 512), F32), jax.ShapeDtypeStruct((t, 512), F32),
                   jax.ShapeDtypeStruct((t, 512), BF16), jax.ShapeDtypeStruct((t, 256), BF16)],
        compiler_params=_params(1),
    )(x1, g, w_in)


def _shift_down(x, before, s):
    if s == 0:
        return x
    rolled = pltpu.roll(x, s, 0)
    ext = jnp.concatenate([before, x[0:8]], axis=0)
    first8 = pltpu.roll(ext, s, 0)[8:16]
    return jnp.concatenate([first8, rolled[8:]], axis=0)


def _shift_up(x, after, s):
    if s == 0:
        return x
    rows = x.shape[0]
    rolled = pltpu.roll(x, rows - s, 0)
    ext = jnp.concatenate([x[rows - 8:rows], after], axis=0)
    last8 = pltpu.roll(ext, 16 - s, 0)[0:8]
    return jnp.concatenate([rolled[:rows - 8], last8], axis=0)


def _log_sigmoid(x):
    e = jnp.exp(-jnp.abs(x))
    log1p_e = jnp.where(e < 0.01, e * (1.0 - e * (0.5 - e * (1.0 / 3.0))), jnp.log(1.0 + e))
    return jnp.minimum(x, 0.0) - log1p_e


def _lru_gates(xc, p_ref, wrg, wig):
    xcb = xc.astype(BF16)
    r = _sigmoid(_dot(xcb, wrg) + p_ref[1:2, :])
    ig = _sigmoid(_dot(xcb, wig) + p_ref[2:3, :])
    ls = _log_sigmoid(p_ref[3:4, :])
    log_a = LRU_C * r * ls
    a = jnp.exp(log_a)
    mult = jnp.sqrt(-jnp.tanh(log_a) * (a * a + 1.0))
    return xcb, r, ig, ls, a, mult


def _conv_taps(x, before, p_ref):
    xc = x * p_ref[7:8, :]
    for s in (1, 2, 3):
        xc = xc + _shift_down(x, before, s) * p_ref[7 - s:8 - s, :]
    return xc + p_ref[0:1, :]


def _group_view(ref, gi):
    if len(ref.shape) == 2:
        return ref.at[:, pl.ds(LRU_GROUP * gi, LRU_GROUP)]
    return ref.at[pl.ds(gi, 1)]


def _lru_block_rows(t):
    return 512 if t >= 1024 else t // 2


def _lru_fwd(xl, p, wrg2, wig2):
    t = xl.shape[0]
    tb = _lru_block_rows(t)

    def body(*refs):
        x_tail, h_carry = refs[-2:]

        @pl.when(pl.program_id(1) == 0)
        def _():
            x_tail[...] = jnp.zeros_like(x_tail)
            h_carry[...] = jnp.zeros_like(h_carry)

        for gi in range(LRU_GROUPS_PER_STEP):
            group(*[_group_view(r, gi) for r in refs])

    def group(xl_ref, p_ref, wrg_ref, wig_ref, h_ref, x_tail, h_carry):
        x = xl_ref[...]
        xc = _conv_taps(x, x_tail[...], p_ref)
        x_tail[...] = x[tb - 8:tb]
        _, r, ig, ls, a, mult = _lru_gates(xc, p_ref, wrg_ref[0], wig_ref[0])
        u = mult * ig * xc
        row = lax.broadcasted_iota(jnp.int32, (tb, LRU_GROUP), 0)
        s = 1
        while s < tb:
            keep = row >= s
            u = jnp.where(keep, a * pltpu.roll(u, s, 0) + u, u)
            a = jnp.where(keep, a * pltpu.roll(a, s, 0), a)
            s *= 2
        h = u + a * h_carry[0:1, :]
        h_ref[...] = h
        h_carry[...] = jnp.broadcast_to(h[tb - 1:tb], h_carry.shape)

    wide = LRU_GROUPS_PER_STEP * LRU_GROUP
    blk = pl.BlockSpec((tb, wide), lambda g, tt: (tt, g))
    par = pl.BlockSpec((8, wide), lambda g, tt: (0, g))
    wsp = pl.BlockSpec((LRU_GROUPS_PER_STEP, LRU_GROUP, LRU_GROUP), lambda g, tt: (g, 0, 0))
    return pl.pallas_call(
        body, name="lru_fwd", grid=(N_LRU_GROUP // LRU_GROUPS_PER_STEP, t // tb), in_specs=[blk, par, wsp, wsp],
        out_specs=blk, out_shape=jax.ShapeDtypeStruct((t, D_LRU), F32),
        scratch_shapes=[pltpu.VMEM((8, wide), F32), pltpu.VMEM((8, wide), F32)],
        compiler_params=_params(2),
    )(xl, p, wrg2, wig2)


def _lru_bwd(dy, h, xl, gl, p, wrg2, wig2):
    t = xl.shape[0]
    tb = _lru_block_rows(t)
    n_tb = t // tb
    tb8 = tb // 8

    def body(*refs):
        _zero_at_first(pl.program_id(1) == 0, *refs[11:])
        for gi in range(LRU_GROUPS_PER_STEP):
            group(*[_group_view(r, gi) for r in refs])

    def group(dy_ref, h_ref, hprev_ref, xl_ref, xprev_ref, gl_ref, p_ref, wrg_ref, wig_ref,
              dxl_ref, dgl_ref, dp_ref, dwrg_ref, dwig_ref, g_carry, a_carry, dxc_head):
        tt = n_tb - 1 - pl.program_id(1)
        has_prev = (tt > 0).astype(F32)
        x = xl_ref[...]
        x_before = xprev_ref[...] * has_prev
        xs = [_shift_down(x, x_before, s) for s in range(4)]
        xc = xs[0] * p_ref[7:8, :] + xs[1] * p_ref[6:7, :] + xs[2] * p_ref[5:6, :] + xs[3] * p_ref[4:5, :] + p_ref[0:1, :]
        wrg = wrg_ref[0]
        wig = wig_ref[0]
        xcb, r, ig, ls, a, mult = _lru_gates(xc, p_ref, wrg, wig)

        hh = h_ref[...]
        h_m1 = _shift_down(hh, hprev_ref[...] * has_prev, 1)
        ge, dge = _gelu(gl_ref[...])
        dy = dy_ref[...]
        dgl_ref[...] = dy * hh * dge
        dh = dy * ge

        b = _shift_up(a, a_carry[...], 1)
        row = lax.broadcasted_iota(jnp.int32, (tb, LRU_GROUP), 0)
        g = dh
        s = 1
        while s < tb:
            keep = row < tb - s
            g = jnp.where(keep, b * pltpu.roll(g, tb - s, 0) + g, g)
            b = jnp.where(keep, b * pltpu.roll(b, tb - s, 0), b)
            s *= 2
        g = g + b * g_carry[0:1, :]
        g_carry[...] = jnp.broadcast_to(g[0:1], g_carry.shape)
        a_carry[...] = jnp.broadcast_to(a[0:1], a_carry.shape)

        da = g * h_m1
        dmult = g * ig * xc
        dig = g * mult * xc
        dxc = g * mult * ig
        dlog_a = da * a - dmult * (a * a) / mult
        dr = dlog_a * (LRU_C * ls)
        dls = jnp.sum(dlog_a * (LRU_C * r), axis=0, keepdims=True)
        dlam = dls * _sigmoid(-p_ref[3:4, :])
        dpre_r = dr * r * (1.0 - r)
        dpre_i = dig * ig * (1.0 - ig)
        dprb = dpre_r.astype(BF16)
        dpib = dpre_i.astype(BF16)
        dxc = dxc + _dot_nt(dprb, wrg) + _dot_nt(dpib, wig)
        dwrg_ref[0] += _dot_tn(xcb, dprb)
        dwig_ref[0] += _dot_tn(xcb, dpib)

        after = dxc_head[...]
        dxl = dxc * p_ref[7:8, :]
        for s in (1, 2, 3):
            dxl = dxl + _shift_up(dxc, after, s) * p_ref[7 - s:8 - s, :]
        dxl_ref[...] = dxl
        dxc_head[...] = dxc[0:8]

        rows = [jnp.sum(dxc, axis=0, keepdims=True), jnp.sum(dpre_r, axis=0, keepdims=True),
                jnp.sum(dpre_i, axis=0, keepdims=True), dlam]
        rows += [jnp.sum(dxc * xs[3 - k], axis=0, keepdims=True) for k in range(4)]
        dp_ref[...] += jnp.concatenate(rows, axis=0)

    wide = LRU_GROUPS_PER_STEP * LRU_GROUP
    blk = pl.BlockSpec((tb, wide), lambda g, s: (n_tb - 1 - s, g))
    prev8 = pl.BlockSpec((8, wide), lambda g, s: (jnp.maximum((n_tb - 1 - s) * tb8 - 1, 0), g))
    par = pl.BlockSpec((8, wide), lambda g, s: (0, g))
    wsp = pl.BlockSpec((LRU_GROUPS_PER_STEP, LRU_GROUP, LRU_GROUP), lambda g, s: (g, 0, 0))
    return pl.pallas_call(
        body, name="lru_bwd", grid=(N_LRU_GROUP // LRU_GROUPS_PER_STEP, n_tb),
        in_specs=[blk, blk, prev8, blk, prev8, blk, par, wsp, wsp], out_specs=[blk, blk, par, wsp, wsp],
        out_shape=[jax.ShapeDtypeStruct((t, D_LRU), F32), jax.ShapeDtypeStruct((t, D_LRU), F32),
                   jax.ShapeDtypeStruct((8, D_LRU), F32),
                   jax.ShapeDtypeStruct((N_LRU_GROUP, LRU_GROUP, LRU_GROUP), F32),
                   jax.ShapeDtypeStruct((N_LRU_GROUP, LRU_GROUP, LRU_GROUP), F32)],
        scratch_shapes=[pltpu.VMEM((8, wide), F32)] * 3,
        compiler_params=_params(2),
    )(dy, h, h, xl, xl, gl, p, wrg2, wig2)


def _attn_bias(first_block):
    qi = jnp.bitwise_and(lax.broadcasted_iota(jnp.int32, (4 * BLOCK_Q, 2 * BLOCK_Q), 0), BLOCK_Q - 1)
    kj = lax.broadcasted_iota(jnp.int32, (4 * BLOCK_Q, 2 * BLOCK_Q), 1)
    rel = qi + BLOCK_Q - kj
    mask = (rel >= 0) & (rel < BLOCK_Q)
    if first_block:
        mask = mask & (kj >= BLOCK_Q)
    return jnp.where(mask, 0.0, MASK_VALUE)


def _sink_column(sinks):
    hrow = lax.broadcasted_iota(jnp.int32, (4 * BLOCK_Q, 1), 0)
    return jnp.where(hrow < BLOCK_Q, sinks[0],
                     jnp.where(hrow < 2 * BLOCK_Q, sinks[1], jnp.where(hrow < 3 * BLOCK_Q, sinks[2], sinks[3])))


def _attn_scores(qv, kvv, n, bias, sk, lo):
    r0 = pl.multiple_of(n * BLOCK_Q, BLOCK_Q)
    rp = pl.multiple_of(jnp.maximum(n - 1, 0) * BLOCK_Q, BLOCK_Q)
    kvb = jnp.concatenate([kvv[pl.ds(rp, BLOCK_Q), :], kvv[pl.ds(r0, BLOCK_Q), :]], axis=0)
    k2 = kvb[:, 0:128]
    v2 = kvb[:, 128:256]
    qs = _stack_heads(qv[pl.ds(r0, BLOCK_Q), :], lo)
    s = _dot_nt(qs, k2) * ATTN_SCALE + bias
    m = jnp.maximum(jnp.max(s, axis=-1, keepdims=True), sk)
    e = jnp.exp(s - m)
    es = jnp.exp(sk - m)
    inv = 1.0 / (jnp.sum(e, axis=-1, keepdims=True) + es)
    return r0, rp, qs, k2, v2, e * inv, es * inv


def _stack_heads(pair2, lo):
    p0 = pair2[:, 0:128]
    p1 = pair2[:, 128:256]
    z = jnp.zeros_like(p0)
    return jnp.concatenate([jnp.where(lo, p0, z), jnp.where(lo, z, p0), jnp.where(lo, p1, z), jnp.where(lo, z, p1)], axis=0)


def _unstack_heads(st, lo):
    b = BLOCK_Q
    return jnp.concatenate([jnp.where(lo, st[0:b], st[b:2 * b]), jnp.where(lo, st[2 * b:3 * b], st[3 * b:4 * b])], axis=1)


def _attn_fwd(q, kv, sinks):
    t = q.shape[0]
    n_blk = t // BLOCK_Q

    def body(q_hbm, kv_hbm, s_ref, o_hbm, q2, kvv, o2, bias0, bias, sem):
        lo = lax.broadcasted_iota(jnp.int32, (BLOCK_Q, 128), 1) < HEAD_DIM
        cols = [pl.ds(256 * g, 256) for g in range(2)]
        loads = [pltpu.make_async_copy(kv_hbm, kvv, sem.at[0])]
        loads += [pltpu.make_async_copy(q_hbm.at[:, cols[g]], q2.at[g], sem.at[1 + g]) for g in range(2)]
        stores = [pltpu.make_async_copy(o2.at[g], o_hbm.at[:, cols[g]], sem.at[3 + g]) for g in range(2)]
        for cp in loads:
            cp.start()
        bias0[...] = _attn_bias(True)
        bias[...] = _attn_bias(False)
        loads[0].wait()
        for g in range(2):
            loads[1 + g].wait()
            qv, ov = q2.at[g], o2.at[g]
            sk = _sink_column([s_ref[0, HEAD_ORDER[4 * g + i]] for i in range(4)])

            def block(n, bias_ref):
                r0, _, _, _, v2, prob, _ = _attn_scores(qv, kvv, n, bias_ref[...], sk, lo)
                ov[pl.ds(r0, BLOCK_Q), :] = _unstack_heads(_dot(prob.astype(BF16), v2), lo)

            block(0, bias0)

            def later(n, carry):
                block(n, bias)
                return carry

            lax.fori_loop(1, n_blk, later, 0, unroll=ATTN_UNROLL)
            stores[g].start()
        for cp in stores:
            cp.wait()

    return pl.pallas_call(
        body, name="attn_fwd", in_specs=[ANY, ANY, SMEM], out_specs=ANY,
        out_shape=jax.ShapeDtypeStruct((t, D_ATTN), F32),
        scratch_shapes=[pltpu.VMEM((2, t, 256), BF16), pltpu.VMEM((t, 256), BF16), pltpu.VMEM((2, t, 256), F32),
                        pltpu.VMEM((4 * BLOCK_Q, 2 * BLOCK_Q), F32), pltpu.VMEM((4 * BLOCK_Q, 2 * BLOCK_Q), F32),
                        pltpu.SemaphoreType.DMA((5,))],
        compiler_params=_params(),
    )(q, kv, sinks)


def _attn_bwd(q, kv, do, sinks, exchange=None):
    t = q.shape[0]
    n_blk = t // BLOCK_Q
    host = _Host(exchange)

    def body(*refs):
        own, ex_refs = host.split(refs, 4, 3, 9)
        q_hbm, kv_hbm, do_hbm, s_ref, dq_hbm, dkv_hbm, dsink_ref, q2, kvv, do2, dqv, dkvv, ds_acc, bias0, bias, sem = own
        host.phase(0, ex_refs)
        lo = lax.broadcasted_iota(jnp.int32, (BLOCK_Q, 128), 1) < HEAD_DIM
        loads = [pltpu.make_async_copy(kv_hbm, kvv, sem.at[0])]
        for g in range(2):
            loads += [pltpu.make_async_copy(src.at[:, pl.ds(256 * g, 256)], dst.at[g], sem.at[1 + 2 * g + i])
                      for i, (src, dst) in enumerate(((q_hbm, q2), (do_hbm, do2)))]
        for cp in loads:
            cp.start()
        bias0[...] = _attn_bias(True)
        bias[...] = _attn_bias(False)
        loads[0].wait()
        for g in range(2):
            cols = pl.ds(256 * g, 256)
            for cp in loads[1 + 2 * g:3 + 2 * g]:
                cp.wait()
            qv, dov = q2.at[g], do2.at[g]
            heads = [HEAD_ORDER[4 * g + i] for i in range(4)]
            sk = _sink_column([s_ref[0, h] for h in heads])
            ds_acc[...] = jnp.zeros_like(ds_acc)

            def block(n, bias_ref, has_prev):
                r0, rp, qs, k2, v2, prob, psink = _attn_scores(qv, kvv, n, bias_ref[...], sk, lo)
                pb = prob.astype(BF16)
                dos = _stack_heads(dov[pl.ds(r0, BLOCK_Q), :], lo)
                dp = _dot_nt(dos, v2)
                dsum = jnp.sum(prob * dp, axis=-1, keepdims=True)
                dsb = (prob * (dp - dsum) * ATTN_SCALE).astype(BF16)
                ds_acc[...] -= psink * dsum
                dqv[pl.ds(r0, BLOCK_Q), :] = _unstack_heads(_dot(dsb, k2), lo).astype(BF16)
                dk2 = _dot_tn(dsb, qs)
                dv2 = _dot_tn(pb, dos)
                cur = jnp.concatenate([dk2[BLOCK_Q:], dv2[BLOCK_Q:]], axis=1)
                if g == 0:
                    dkvv[pl.ds(r0, BLOCK_Q), :] = cur
                else:
                    dkvv[pl.ds(r0, BLOCK_Q), :] += cur
                if has_prev:
                    dkvv[pl.ds(rp, BLOCK_Q), :] += jnp.concatenate([dk2[:BLOCK_Q], dv2[:BLOCK_Q]], axis=1)

            block(0, bias0, False)

            def later(n, carry):
                block(n, bias, True)
                return carry

            lax.fori_loop(1, n_blk, later, 0, unroll=ATTN_UNROLL)
            for i, h in enumerate(heads):
                tot = jnp.sum(ds_acc[BLOCK_Q * i:BLOCK_Q * (i + 1), :], axis=0, keepdims=True)
                dsink_ref[h:h + 1, :] = jnp.broadcast_to(tot, (1, 128))
            store = pltpu.make_async_copy(dqv, dq_hbm.at[:, cols], sem.at[5])
            store.start()
            store.wait()
        store = pltpu.make_async_copy(dkvv, dkv_hbm, sem.at[6])
        store.start()
        store.wait()
        if exchange is not None:
            for p in range(1, exchange.n_phases):
                host.phase(p, ex_refs)

    res = pl.pallas_call(
        body, name="attn_bwd", in_specs=[ANY, ANY, ANY, SMEM] + host.in_specs,
        out_specs=[ANY, ANY, pl.BlockSpec(memory_space=pltpu.VMEM)] + host.out_specs,
        out_shape=[jax.ShapeDtypeStruct((t, D_ATTN), BF16), jax.ShapeDtypeStruct((t, 256), F32),
                   jax.ShapeDtypeStruct((8, 128), F32)] + host.out_shape,
        scratch_shapes=[pltpu.VMEM((2, t, 256), BF16), pltpu.VMEM((t, 256), BF16), pltpu.VMEM((2, t, 256), BF16),
                        pltpu.VMEM((t, 256), BF16), pltpu.VMEM((t, 256), F32), pltpu.VMEM((4 * BLOCK_Q, 1), F32),
                        pltpu.VMEM((4 * BLOCK_Q, 2 * BLOCK_Q), F32), pltpu.VMEM((4 * BLOCK_Q, 2 * BLOCK_Q), F32),
                        pltpu.SemaphoreType.DMA((7,))] + host.scratch,
        compiler_params=_params(),
    )(q, kv, do, sinks, *host.args)
    return (*res[:3], list(res[3:]))


def _mix_out_fwd(x1, h, gl, o, g_lru, g_attn, g_post, w_o):
    t = x1.shape[0]
    tm = _token_tile(t)

    def body(x_ref, h_ref, gl_ref, o_ref, g1_ref, g2_ref, gp_ref, w_ref, x2_ref, m_ref):
        y = h_ref[...] * _gelu(gl_ref[...])[0]
        yn1 = _rms_fwd(y, g1_ref[...]).astype(BF16)
        yn2 = _rms_fwd(o_ref[...], g2_ref[...]).astype(BF16)
        m = _dot(yn1, w_ref[0:512, :]) + _dot(yn2, w_ref[512:1024, :])
        m_ref[...] = m
        x2_ref[...] = x_ref[...] + _rms_fwd(m, gp_ref[...])

    tok = pl.BlockSpec((tm, D_MODEL), lambda i: (i, 0))
    half = pl.BlockSpec((tm, 512), lambda i: (i, 0))
    vec = pl.BlockSpec((1, D_MODEL), lambda i: (0, 0))
    hvec = pl.BlockSpec((1, 512), lambda i: (0, 0))
    return pl.pallas_call(
        body, name="mix_out_fwd", grid=(t // tm,),
        in_specs=[tok, half, half, half, hvec, hvec, vec, pl.BlockSpec((D_MODEL, D_MODEL), lambda i: (0, 0))],
        out_specs=[tok, tok],
        out_shape=[jax.ShapeDtypeStruct((t, D_MODEL), F32), jax.ShapeDtypeStruct((t, D_MODEL), F32)],
        compiler_params=_params(1),
    )(x1, h, gl, o, g_lru, g_attn, g_post, w_o)


def _mix_out_bwd(dx2, m, h, gl, o, g_lru, g_attn, g_post, w_o):
    t = dx2.shape[0]
    tm = _token_tile(t)

    def body(dx_ref, m_ref, h_ref, gl_ref, o_ref, g1_ref, g2_ref, gp_ref, w_ref,
             dy_ref, do_ref, dw_ref, dgp_ref, dg1_ref, dg2_ref):
        _zero_at_first(pl.program_id(0) == 0, dw_ref, dgp_ref, dg1_ref, dg2_ref)
        dm, dgp = _rms_bwd(m_ref[...], gp_ref[...], dx_ref[...])
        dmb = dm.astype(BF16)
        y = h_ref[...] * _gelu(gl_ref[...])[0]
        o = o_ref[...]
        yn1 = _rms_fwd(y, g1_ref[...]).astype(BF16)
        yn2 = _rms_fwd(o, g2_ref[...]).astype(BF16)
        dw_ref[0:512, :] += _dot_tn(yn1, dmb)
        dw_ref[512:1024, :] += _dot_tn(yn2, dmb)
        dy, dg1 = _rms_bwd(y, g1_ref[...], _dot_nt(dmb, w_ref[0:512, :]))
        do, dg2 = _rms_bwd(o, g2_ref[...], _dot_nt(dmb, w_ref[512:1024, :]))
        dy_ref[...] = dy
        do_ref[...] = do.astype(BF16)
        dgp_ref[...] += dgp
        dg1_ref[...] += dg1
        dg2_ref[...] += dg2

    tok = pl.BlockSpec((tm, D_MODEL), lambda i: (i, 0))
    half = pl.BlockSpec((tm, 512), lambda i: (i, 0))
    vec = pl.BlockSpec((1, D_MODEL), lambda i: (0, 0))
    hvec = pl.BlockSpec((1, 512), lambda i: (0, 0))
    mat = pl.BlockSpec((D_MODEL, D_MODEL), lambda i: (0, 0))
    return pl.pallas_call(
        body, name="mix_out_bwd", grid=(t // tm,),
        in_specs=[tok, tok, half, half, half, hvec, hvec, vec, mat],
        out_specs=[half, half, mat, vec, hvec, hvec],
        out_shape=[jax.ShapeDtypeStruct((t, 512), F32), jax.ShapeDtypeStruct((t, 512), BF16),
                   jax.ShapeDtypeStruct((D_MODEL, D_MODEL), F32), jax.ShapeDtypeStruct((1, D_MODEL), F32),
                   jax.ShapeDtypeStruct((1, 512), F32), jax.ShapeDtypeStruct((1, 512), F32)],
        compiler_params=_params(1),
    )(dx2, m, h, gl, o, g_lru, g_attn, g_post, w_o)


def _mix_in_bwd(dx2, x1, g, dxl, dgl, dq, dkv, w_in, f1, g_post1):
    t = x1.shape[0]
    tm = _token_tile(t)

    def body(dx2_ref, x_ref, g_ref, dxl_ref, dgl_ref, dq_ref, dkv_ref, w_ref, f1_ref, gp1_ref,
             dx1_ref, dw_ref, dg_ref, df1_ref, dgp1_ref):
        _zero_at_first(pl.program_id(0) == 0, dw_ref, dg_ref, dgp1_ref)
        x = x_ref[...]
        nb = _rms_fwd(x, g_ref[...]).astype(BF16)
        dproj = jnp.concatenate([dxl_ref[...].astype(BF16), dgl_ref[...].astype(BF16), dq_ref[...],
                                 dkv_ref[...].astype(BF16)], axis=1)
        dw_ref[...] += _dot_tn(nb, dproj)
        dx, dg = _rms_bwd(x, g_ref[...], _dot(dproj, w_ref[...]))
        dx1 = dx2_ref[...] + dx
        dx1_ref[...] = dx1
        dg_ref[...] += dg
        df1, dgp1 = _rms_bwd(f1_ref[...], gp1_ref[...], 0.5 * dx1)
        df1_ref[...] = df1.astype(BF16)
        dgp1_ref[...] += dgp1

    tok = pl.BlockSpec((tm, D_MODEL), lambda i: (i, 0))
    half = pl.BlockSpec((tm, 512), lambda i: (i, 0))
    vec = pl.BlockSpec((1, D_MODEL), lambda i: (0, 0))
    mat = pl.BlockSpec((D_IN, D_MODEL), lambda i: (0, 0))
    dmat = pl.BlockSpec((D_MODEL, D_IN), lambda i: (0, 0))
    quarter = pl.BlockSpec((tm, 256), lambda i: (i, 0))
    return pl.pallas_call(
        body, name="mix_in_bwd", grid=(t // tm,),
        in_specs=[tok, tok, vec, half, half, half, quarter, mat, tok, vec], out_specs=[tok, dmat, vec, tok, vec],
        out_shape=[jax.ShapeDtypeStruct((t, D_MODEL), F32), jax.ShapeDtypeStruct((D_MODEL, D_IN), F32),
                   jax.ShapeDtypeStruct((1, D_MODEL), F32), jax.ShapeDtypeStruct((t, D_MODEL), BF16),
                   jax.ShapeDtypeStruct((1, D_MODEL), F32)],
        compiler_params=_params(1),
    )(dx2, x1, g, dxl, dgl, dq, dkv, w_in, f1, g_post1)


def _half(rows):
    return rows // 2


def _chip_sums(grads, from_sibling, other, name):
    n_arr = len(grads)

    def body(other_ref, *refs):
        for a in range(n_arr):
            refs[2 * n_arr + a][0] = (refs[2 * a][0, 0] + refs[2 * a + 1][0]).astype(BF16)

    in_specs, out_specs, out_shape, args = [], [], [], []
    for g, s in zip(grads, from_sibling):
        _, rows, cols = g.shape
        tr = _half(rows)
        in_specs += [pl.BlockSpec((1, 1, tr, cols), lambda j, i, other: (other[j], other[3], i, 0)),
                     pl.BlockSpec((1, tr, cols), lambda j, i, other: (other[j], i, 0))]
        out_specs.append(pl.BlockSpec((1, tr, cols), lambda j, i, other: (j, i, 0)))
        out_shape.append(jax.ShapeDtypeStruct((3, rows, cols), BF16))
        args += [g.reshape(4, 2, rows, cols), s]
    grid_spec = pltpu.PrefetchScalarGridSpec(num_scalar_prefetch=1, grid=(3, 2), in_specs=in_specs, out_specs=out_specs)
    return pl.pallas_call(body, name=name, grid_spec=grid_spec, out_shape=out_shape, compiler_params=_params(2))(other, *args)


def _adamw(w, g, m, v):
    m = ADAM_B1 * m + (1.0 - ADAM_B1) * g
    v = ADAM_B2 * v + (1.0 - ADAM_B2) * (g * g)
    m_hat = m / (1.0 - ADAM_B1 ** ADAM_STEP)
    v_hat = v / (1.0 - ADAM_B2 ** ADAM_STEP)
    delta = -ADAM_LR * (m_hat / (jnp.sqrt(v_hat) + ADAM_EPS) + ADAM_WD * w)
    return delta, m, v


def _shard_updates(grads, from_sibling, from_chips, w, m, v, place, name, transposed):
    n_arr = len(grads)

    def total(g_ref, s_ref, c_ref):
        g = g_ref[0, 0] + s_ref[0]
        g = g + c_ref[0].astype(F32)
        g = g + c_ref[1].astype(F32)
        return g + c_ref[2].astype(F32)

    part_specs, parts, flat, shapes = [], [], [], []
    for g in grads:
        _, rows, cols = g.shape
        tr = _half(rows)
        part_specs.append([pl.BlockSpec((1, 1, tr, cols), lambda i, place: (place[0], place[1], i, 0)),
                           pl.BlockSpec((1, tr, cols), lambda i, place: (place[0], i, 0)),
                           pl.BlockSpec((3, tr, cols), lambda i, place: (0, i, 0))])
        flat.append(pl.BlockSpec((tr, cols), lambda i, place: (i, 0)))
        shapes.append(jax.ShapeDtypeStruct((rows, cols), F32))
    for g, s, c in zip(grads, from_sibling, from_chips):
        parts += [g.reshape(4, 2, *g.shape[1:]), s, c]

    if not transposed:
        def body(place_ref, *refs):
            ins, wmv, outs = refs[:3 * n_arr], refs[3 * n_arr:6 * n_arr], refs[6 * n_arr:]
            for a in range(n_arr):
                g = total(*ins[3 * a:3 * a + 3])
                outs[4 * a][...] = g
                outs[4 * a + 1][...], outs[4 * a + 2][...], outs[4 * a + 3][...] = _adamw(
                    wmv[3 * a][...], g, wmv[3 * a + 1][...], wmv[3 * a + 2][...])

        grid_spec = pltpu.PrefetchScalarGridSpec(
            num_scalar_prefetch=1, grid=(2,),
            in_specs=[sp for specs in part_specs for sp in specs] + [f for f in flat for _ in range(3)],
            out_specs=[f for f in flat for _ in range(4)])
        res = pl.pallas_call(body, name=name, grid_spec=grid_spec, out_shape=[sh for sh in shapes for _ in range(4)],
                             compiler_params=_params(1))(place, *parts, *[x for wmv in zip(w, m, v) for x in wmv])
        return [tuple(res[4 * a:4 * a + 4]) for a in range(n_arr)]

    def sum_body(place_ref, *refs):
        for a in range(n_arr):
            refs[3 * n_arr + a][...] = total(*refs[3 * a:3 * a + 3])

    grid_spec = pltpu.PrefetchScalarGridSpec(num_scalar_prefetch=1, grid=(2,),
                                             in_specs=[sp for specs in part_specs for sp in specs], out_specs=flat)
    sums = pl.pallas_call(sum_body, name=name + "_sum", grid_spec=grid_spec, out_shape=shapes,
                          compiler_params=_params(1))(place, *parts)
    turned = [jnp.transpose(g, (1, 0)) for g in sums]

    def adam_body(*refs):
        ins, outs = refs[:4 * n_arr], refs[4 * n_arr:]
        for a in range(n_arr):
            g = ins[4 * a][...]
            outs[4 * a][...] = g
            outs[4 * a + 1][...], outs[4 * a + 2][...], outs[4 * a + 3][...] = _adamw(
                ins[4 * a + 1][...], g, ins[4 * a + 2][...], ins[4 * a + 3][...])

    blks = [pl.BlockSpec((g.shape[0] // 4, g.shape[1]), lambda i: (i, 0)) for g in turned]
    res = pl.pallas_call(
        adam_body, name=name + "_adam", grid=(4,), in_specs=[b for b in blks for _ in range(4)],
        out_specs=[b for b in blks for _ in range(4)],
        out_shape=[jax.ShapeDtypeStruct(g.shape, F32) for g in turned for _ in range(4)], compiler_params=_params(1),
    )(*[x for gwmv in zip(turned, w, m, v) for x in gwmv])
    return [tuple(res[4 * a:4 * a + 4]) for a in range(n_arr)]


GAINS = ("ffn1_pre_g", "ffn1_post_g", "mix_pre_g", "mix_post_g", "ffn2_pre_g", "ffn2_post_g")
HALVES = ("conv_b", "b_rg", "b_ig", "lru_lambda", "g_lru_out", "g_attn_out")
GATES = ("w_rg", "w_ig")
SMALL = GAINS + HALVES + GATES + ("sinks", "conv_w")


def _small_update(gathered, w, m, v):
    n_small = len(SMALL)

    def body(*refs):
        ga_ref, gb_ref, gc_ref, gd_ref, g0_ref, gconv_ref = refs[:6]
        wmv = refs[6:6 + 3 * n_small]
        outs = refs[6 + 3 * n_small:6 + 7 * n_small]
        loss_ref = refs[6 + 7 * n_small]

        def total(ref):
            s = ref[0]
            for d in range(1, N_DEV):
                s = s + ref[d]
            return s

        sa, sb, sc, sd = total(ga_ref), total(gb_ref), total(gc_ref), total(gd_ref)
        grads = {}
        for i, k in enumerate(GAINS):
            grads[k] = sa[i:i + 1]
        grads[GAINS[0]] = total(g0_ref)
        for i, k in enumerate(HALVES):
            grads[k] = sb[i:i + 1]
        grads["w_rg"], grads["w_ig"] = sc[0:512], sc[512:1024]
        grads["sinks"] = sd[4:5, 0:8]
        grads["conv_w"] = total(gconv_ref)
        for i, k in enumerate(SMALL):
            g = grads[k]
            outs[4 * i][...] = g
            outs[4 * i + 1][...], outs[4 * i + 2][...], outs[4 * i + 3][...] = _adamw(
                wmv[3 * i][...], g, wmv[3 * i + 1][...], wmv[3 * i + 2][...])
        loss_ref[...] = jnp.broadcast_to(sd[5:6, 0:128], loss_ref.shape)

    operands = list(gathered)
    out_shape = []
    for k in SMALL:
        operands += [w[k], m[k], v[k]]
        out_shape += [jax.ShapeDtypeStruct(w[k].shape, F32)] * 4
    out_shape.append(jax.ShapeDtypeStruct((8, 128), F32))
    res = pl.pallas_call(body, name="small_update", out_shape=out_shape, compiler_params=_params())(*operands)
    parts = [{k: res[4 * i + j] for i, k in enumerate(SMALL)} for j in range(4)]
    return (*parts, res[-1])


def _reorder_heads(a, axis, start, order):
    def slab(h):
        return lax.slice_in_dim(a, start + HEAD_DIM * h, start + HEAD_DIM * (h + 1), axis=axis)

    parts = [lax.slice_in_dim(a, 0, start, axis=axis)] + [slab(h) for h in order]
    parts.append(lax.slice_in_dim(a, start + 8 * HEAD_DIM, a.shape[axis], axis=axis))
    return jnp.concatenate(parts, axis=axis)


HEAD_ORDER_INVERSE = tuple(HEAD_ORDER.index(h) for h in range(8))


def _pair_block_diag(w):
    w = w.reshape(N_LRU_GROUP, 2, 64, 64)
    z = jnp.zeros((N_LRU_GROUP, 64, 64), w.dtype)
    top = jnp.concatenate([w[:, 0], z], axis=2)
    bot = jnp.concatenate([z, w[:, 1]], axis=2)
    return jnp.concatenate([top, bot], axis=1)


def _pair_block_diag_grad(dw2):
    return jnp.stack([dw2[:, :64, :64], dw2[:, 64:, 64:]], axis=1).reshape(512, 64)


def kernel(x, ffn1_pre_g, ffn1_w_gu, ffn1_w_down, ffn1_post_g, mix_pre_g, w_in, conv_w, conv_b, w_rg, b_rg, w_ig, b_ig, lru_lambda, sinks, g_lru_out, g_attn_out, w_o, mix_post_g, ffn2_pre_g, ffn2_w_gu, ffn2_w_down, ffn2_post_g, loss_target, m_ffn1_pre_g, m_ffn1_w_gu, m_ffn1_w_down, m_ffn1_post_g, m_mix_pre_g, m_w_in, m_conv_w, m_conv_b, m_w_rg, m_b_rg, m_w_ig, m_b_ig, m_lru_lambda, m_sinks, m_g_lru_out, m_g_attn_out, m_w_o, m_mix_post_g, m_ffn2_pre_g, m_ffn2_w_gu, m_ffn2_w_down, m_ffn2_post_g, v_ffn1_pre_g, v_ffn1_w_gu, v_ffn1_w_down, v_ffn1_post_g, v_mix_pre_g, v_w_in, v_conv_w, v_conv_b, v_w_rg, v_b_rg, v_w_ig, v_b_ig, v_lru_lambda, v_sinks, v_g_lru_out, v_g_attn_out, v_w_o, v_mix_post_g, v_ffn2_pre_g, v_ffn2_w_gu, v_ffn2_w_down, v_ffn2_post_g):
    args = dict(locals())
    names = ["ffn1_pre_g", "ffn1_w_gu", "ffn1_w_down", "ffn1_post_g", "mix_pre_g", "w_in", "conv_w", "conv_b", "w_rg",
             "b_rg", "w_ig", "b_ig", "lru_lambda", "sinks", "g_lru_out", "g_attn_out", "w_o", "mix_post_g",
             "ffn2_pre_g", "ffn2_w_gu", "ffn2_w_down", "ffn2_post_g"]
    big = ["ffn1_w_gu", "ffn1_w_down", "w_in", "w_o", "ffn2_w_gu", "ffn2_w_down"]
    w = {k: args[k] for k in names}
    mom = {k: args["m_" + k] for k in names}
    var = {k: args["v_" + k] for k in names}
    t = x.shape[1]
    xs = x.reshape(t, D_MODEL)
    target = loss_target.reshape(t, D_MODEL)
    cx, cy, cc = _coords()
    me = 4 * cx + 2 * cy + cc
    other = jnp.stack([2 * (1 - cx) + cy, 2 * cx + (1 - cy), 2 * (1 - cx) + (1 - cy), cc]).astype(jnp.int32)
    place = jnp.stack([2 * cx + cy, cc]).astype(jnp.int32)

    transposed = ("ffn1_w_gu", "w_in", "ffn2_w_gu")

    def shard_view(a, k):
        return jnp.transpose(a[0], (1, 0)) if k in transposed else a[0]

    def shard_unview(a, k):
        return (jnp.transpose(a, (1, 0)) if k in transposed else a)[None]

    shard2d = {k: shard_view(w[k], k) for k in big}
    shard_bf = {k: shard2d[k].astype(BF16) for k in big}
    conv_pad = jnp.pad(conv_w.reshape(4, 64), ((0, 4), (0, 64)))
    (first_w,) = _run_exchanges([_Gather([shard_bf["ffn1_w_gu"], shard_bf["ffn1_w_down"]], routed=True)], "all_gather_ffn1")
    wgu1 = first_w[0].reshape(2, N_CHUNK, CHUNK, D_MODEL)
    wd1 = first_w[1].reshape(N_CHUNK, CHUNK, D_MODEL)
    rest = _Gather([shard_bf["w_in"], shard_bf["w_o"], shard_bf["ffn2_w_gu"], shard_bf["ffn2_w_down"], conv_pad])

    x1, f1, n1, gu1, gathered = _ffn_fwd(xs, ffn1_pre_g, wgu1, wd1, ffn1_post_g, None, "ffn1_fwd", rest)
    w_in_full = _reorder_heads(gathered[0].reshape(D_IN, D_MODEL), 0, 2 * D_LRU, HEAD_ORDER)
    w_o_full = _reorder_heads(gathered[1].reshape(D_MODEL, D_MODEL), 0, D_LRU, HEAD_ORDER)
    g_attn_heads = _reorder_heads(g_attn_out, 1, 0, HEAD_ORDER)
    wgu2 = gathered[2].reshape(2, N_CHUNK, CHUNK, D_MODEL)
    wd2 = gathered[3].reshape(N_CHUNK, CHUNK, D_MODEL)
    conv_w_full = jnp.transpose(gathered[4][:, 0:4, 0:64], (1, 0, 2)).reshape(4, D_LRU)
    p_lru = jnp.concatenate([conv_b, b_rg, b_ig, lru_lambda, conv_w_full], axis=0)
    wrg2 = _pair_block_diag(w_rg[0]).astype(BF16)
    wig2 = _pair_block_diag(w_ig[0]).astype(BF16)
    xl, gl, q, kv = _mix_in_fwd(x1, mix_pre_g, w_in_full)
    h = _lru_fwd(xl, p_lru, wrg2, wig2)
    o = _attn_fwd(q, kv, sinks)
    x2, mo = _mix_out_fwd(x1, h, gl, o, g_lru_out, g_attn_heads, mix_post_g, w_o_full)
    g = {}
    dx3, n2, df2, gu2, g["ffn2_post_g"], loss_parts, _ = _ffn_fwd(x2, ffn2_pre_g, wgu2, wd2, ffn2_post_g, target, "ffn2_fwd")
    loss_local = jnp.sum(loss_parts[::8, 0])

    partial, from_sibling, from_chips = {}, {}, {}

    def chip_sums(keys):
        return _chip_sums([partial[k] for k in keys], [from_sibling[k] for k in keys], other, "chip_sum_" + keys[0])

    dgu2, dwgu2, dwd2, _ = _ffn_bwd_w(n2, df2, gu2, wd2, "ffn2_bwd_w")
    partial["ffn2_w_gu"] = dwgu2.reshape(N_DEV, D_MODEL, CHUNK)
    partial["ffn2_w_down"] = dwd2.reshape(N_DEV, D_FF // N_DEV, D_MODEL)
    ffn2_keys = ["ffn2_w_gu", "ffn2_w_down"]
    dx2, g["ffn2_pre_g"], got = _ffn_bwd_x(dgu2, wgu2, x2, ffn2_pre_g, dx3, "ffn2_bwd_x",
                                           _SiblingExchange([partial[k] for k in ffn2_keys]))
    from_sibling.update(zip(ffn2_keys, got))
    dy, do, dwo, g["mix_post_g"], g["g_lru_out"], dg_attn_heads = _mix_out_bwd(
        dx2, mo, h, gl, o, g_lru_out, g_attn_heads, mix_post_g, w_o_full)
    g["g_attn_out"] = _reorder_heads(dg_attn_heads, 1, 0, HEAD_ORDER_INVERSE)
    dwo = _reorder_heads(dwo, 0, D_LRU, HEAD_ORDER_INVERSE)
    dq, dkv, dsink, got = _attn_bwd(q, kv, do, sinks, _ChipExchange(chip_sums(ffn2_keys)))
    from_chips.update(zip(ffn2_keys, got))
    dxl, dgl, dp, dwrg2, dwig2 = _lru_bwd(dy, h, xl, gl, p_lru, wrg2, wig2)
    dx1, dwin, g["mix_pre_g"], df1, g["ffn1_post_g"] = _mix_in_bwd(
        dx2, x1, mix_pre_g, dxl, dgl, dq, dkv, w_in_full, f1, ffn1_post_g)
    dwin = _reorder_heads(dwin, 1, 2 * D_LRU, HEAD_ORDER_INVERSE)
    partial["w_in"] = jnp.transpose(dwin.reshape(D_MODEL, N_DEV, D_IN // N_DEV), (1, 0, 2))
    partial["w_o"] = dwo.reshape(N_DEV, D_MODEL // N_DEV, D_MODEL)
    mix_keys = ["w_in", "w_o"]
    (got,) = _run_exchanges([_SiblingExchange([partial[k] for k in mix_keys])], "mix_sibling_exchange")
    from_sibling.update(zip(mix_keys, got))
    zeros2 = jnp.zeros((2, D_MODEL), F32)
    g_gains = jnp.concatenate([zeros2[:1]] + [g[k] for k in GAINS[1:]] + [zeros2], axis=0)
    g_halves = jnp.concatenate([dp[0:4], g["g_lru_out"], g["g_attn_out"], zeros2[:, :D_LRU]], axis=0)
    g_gates = jnp.concatenate([_pair_block_diag_grad(dwrg2), _pair_block_diag_grad(dwig2)], axis=0)
    g_misc = jnp.concatenate([dp[4:8], jnp.pad(dsink[:, 0].reshape(1, 8), ((0, 0), (0, D_LRU - 8))),
                              jnp.pad(loss_local.reshape(1, 1), ((0, 0), (0, D_LRU - 1))), zeros2[:, :D_LRU]], axis=0)
    dgu1, dwgu1, dwd1, got = _ffn_bwd_w(
        n1, df1, gu1, wd1, "ffn1_bwd_w",
        _Both(_ChipExchange(chip_sums(mix_keys)), _Gather([g_gains, g_halves, g_gates, g_misc])))
    from_chips.update(zip(mix_keys, got[:2]))
    gathered_small = got[2:]
    partial["ffn1_w_gu"] = dwgu1.reshape(N_DEV, D_MODEL, CHUNK)
    partial["ffn1_w_down"] = dwd1.reshape(N_DEV, D_FF // N_DEV, D_MODEL)
    ffn1_keys = ["ffn1_w_gu", "ffn1_w_down"]
    (got,) = _run_exchanges([_SiblingExchange([partial[k] for k in ffn1_keys])], "ffn1_sibling_exchange")
    from_sibling.update(zip(ffn1_keys, got))
    dx0, g_first, got = _ffn_bwd_x(dgu1, wgu1, xs, ffn1_pre_g, dx1, "ffn1_bwd_x", _ChipExchange(chip_sums(ffn1_keys)))
    from_chips.update(zip(ffn1_keys, got))

    grads, delta, new_m, new_v = {}, {}, {}, {}
    for name, keys, turned in (("update_column_sharded", transposed, True),
                               ("update_row_sharded", tuple(k for k in big if k not in transposed), False)):
        res = _shard_updates([partial[k] for k in keys], [from_sibling[k] for k in keys], [from_chips[k] for k in keys],
                             [shard2d[k] for k in keys], [shard_view(mom[k], k) for k in keys],
                             [shard_view(var[k], k) for k in keys], place, name, turned)
        for k, out in zip(keys, res):
            grads[k], delta[k], new_m[k], new_v[k] = [shard_unview(r, k) for r in out]

    ((gathered_first,),) = _run_exchanges([_Gather([g_first])], "all_gather_first_gain")
    conv_parts = lax.dynamic_slice(gathered_small[3], (0, 0, me * 64), (N_DEV, 4, 64))

    def small_view(vals):
        out = {k: vals[k] for k in GAINS + HALVES + ("sinks",)}
        out.update({k: vals[k].reshape(512, 64) for k in GATES})
        out["conv_w"] = vals["conv_w"].reshape(4, 64)
        return out

    *small, loss_tile = _small_update([*gathered_small, gathered_first, conv_parts], small_view(w), small_view(mom),
                                      small_view(var))
    for dst, part in zip((grads, delta, new_m, new_v), small):
        for k in SMALL:
            dst[k] = part[k].reshape(w[k].shape)
    return (loss_tile[0, 0], dx0.reshape(x.shape), *[grads[k] for k in names], *[delta[k] for k in names],
            *[new_m[k] for k in names], *[new_v[k] for k in names])
```
